```python
import jax, jax.numpy as jnp
from jax import lax
import numpy as np

D_MODEL = 1024
BATCH = 8
SEQ = 2048
DEPTH = 1

D_POOL = D_MODEL
POOL_WINDOWS = (2, 4, 8, 16)
N_POOL_GROUPS = len(POOL_WINDOWS)
POOL_GROUP = D_POOL // N_POOL_GROUPS
D_REC = D_MODEL
HEAD_DIM = 128
N_REC_HEADS = D_REC // HEAD_DIM
CHUNK = 64
D_MIX = D_POOL + D_REC
D_IN_PROJ = 2 * D_POOL + 4 * D_REC
EPS = 1e-6

kernel_name = "hybrid_pool_hgrn2_parallel_heads"


def rmsnorm(x, g):
    xf = x.astype(jnp.float32)
    y = xf * lax.rsqrt(jnp.mean(xf * xf, axis=-1, keepdims=True) + EPS)
    return y.astype(x.dtype) * g


def causal_multiscale_pool(u, pool_w, pool_scale):
    B, T, _ = u.shape
    ug = u.reshape(B, T, N_POOL_GROUPS, POOL_GROUP).astype(jnp.float32)
    cs0 = jnp.pad(jnp.cumsum(ug, axis=1), ((0, 0), (1, 0), (0, 0), (0, 0)))
    pos = jnp.arange(T, dtype=jnp.float32)
    pooled = []
    for gi, w in enumerate(POOL_WINDOWS):
        c = cs0[:, :, gi]
        hi = c[:, 1:]
        lo = jnp.pad(c[:, : T + 1 - w], ((0, 0), (w - 1, 0), (0, 0)))
        count = jnp.minimum(pos + 1.0, float(w))[None, :, None]
        pooled.append((hi - lo) / count)
    pooled = jnp.stack(pooled, axis=2)
    mixed = jnp.einsum('btgc,gcd->btgd', pooled - ug, pool_w.astype(jnp.float32))
    return mixed.reshape(B, T, D_POOL).astype(u.dtype) * pool_scale


def hgrn2_chunked(q, k, v, log_f):
    B, H, T, Dk = q.shape
    Dv = v.shape[-1]
    n = T // CHUNK

    def to_chunks(a):
        return a.reshape(B, H, n, CHUNK, a.shape[-1]).transpose(2, 0, 1, 3, 4)

    qc, kc, vc, gc = to_chunks(q), to_chunks(k), to_chunks(v), to_chunks(log_f)
    causal = jnp.tril(jnp.ones((CHUNK, CHUNK), dtype=bool))[:, :, None]

    def step(S, inp):
        qb, kb, vb, gb = inp
        G = jnp.cumsum(gb, axis=2)
        diff = G[:, :, :, None, :] - G[:, :, None, :, :]
        decay = jnp.exp(jnp.where(causal, diff, -jnp.inf))
        A = jnp.einsum('bhtd,bhsd,bhtsd->bhts', qb, kb, decay)
        o = (jnp.einsum('bhts,bhsv->bhtv', A, vb)
             + jnp.einsum('bhtd,bhdv->bhtv', qb * jnp.exp(G), S))
        G_last = G[:, :, -1]
        k_dec = kb * jnp.exp(G_last[:, :, None, :] - G)
        S = jnp.exp(G_last)[..., None] * S + jnp.einsum('bhsd,bhsv->bhdv', k_dec, vb)
        return S, o

    S0 = jnp.zeros((B, H, Dk, Dv), jnp.float32)
    _, o = lax.scan(step, S0, (qc, kc, vc, gc))
    return o.transpose(1, 2, 0, 3, 4).reshape(B, H, T, Dv)


def hybrid_layer(x, norm_g, w_in, pool_w, pool_scale, lb, rec_norm_g, w_out):
    B, T, _ = x.shape
    h = rmsnorm(x, norm_g)
    proj = jnp.einsum('btd,de->bte', h, w_in)
    o = 0
    pool_in = proj[..., o:o + D_POOL]; o += D_POOL
    pool_gate = proj[..., o:o + D_POOL]; o += D_POOL
    q = proj[..., o:o + D_REC]; o += D_REC
    f_logit = proj[..., o:o + D_REC]; o += D_REC
    i_in = proj[..., o:o + D_REC]; o += D_REC
    rec_gate = proj[..., o:o + D_REC]

    y_pool = causal_multiscale_pool(pool_in, pool_w, pool_scale) * jax.nn.silu(pool_gate)

    def heads(a):
        return a.astype(jnp.float32).reshape(B, T, N_REC_HEADS, HEAD_DIM).transpose(0, 2, 1, 3)

    f = lb + (1.0 - lb) * jax.nn.sigmoid(f_logit.astype(jnp.float32))
    rec = hgrn2_chunked(heads(jax.nn.silu(q)), heads(1.0 - f), heads(i_in), heads(jnp.log(f)))
    rec = rec.transpose(0, 2, 1, 3)
    rec = rec * lax.rsqrt(jnp.mean(rec * rec, axis=-1, keepdims=True) + EPS)
    rec = rec.reshape(B, T, D_REC).astype(x.dtype) * rec_norm_g
    y_rec = rec * jax.nn.silu(rec_gate)

    y = jnp.concatenate([y_pool, y_rec], axis=-1)
    return x + jnp.einsum('bte,ed->btd', y, w_out)


def _fwd_setup_inputs(seed: int = 0) -> dict:
    key = jax.random.key(seed)
    ks = jax.random.split(key, 9)
    f32 = jnp.float32
    return {
        "x": jax.random.normal(ks[0], (BATCH, SEQ, D_MODEL), f32),
        "norm1_g": 1.0 + 0.02 * jax.random.normal(ks[1], (DEPTH, D_MODEL), f32),
        "w_in": jax.random.normal(ks[2], (DEPTH, D_MODEL, D_IN_PROJ), f32) * D_MODEL ** -0.5,
        "pool_w": jax.random.normal(ks[3], (DEPTH, N_POOL_GROUPS, POOL_GROUP, POOL_GROUP), f32) * POOL_GROUP ** -0.5,
        "pool_scale": 1.0 + 0.02 * jax.random.normal(ks[4], (DEPTH, D_POOL), f32),
        "lb_logits": 0.5 * jax.random.normal(ks[5], (DEPTH + 1, D_REC), f32),
        "rec_norm_g": 1.0 + 0.02 * jax.random.normal(ks[6], (DEPTH, D_REC), f32),
        "w_out": jax.random.normal(ks[7], (DEPTH, D_MIX, D_MODEL), f32) * D_MIX ** -0.5,
        "final_norm_g": 1.0 + 0.02 * jax.random.normal(ks[8], (D_MODEL,), f32),
    }


def _fwd_reference(x, norm1_g, w_in, pool_w, pool_scale, lb_logits, rec_norm_g, w_out, final_norm_g):
    lb_all = jnp.cumsum(jax.nn.softmax(lb_logits.astype(jnp.float32), axis=0), axis=0)
    for layer in range(DEPTH):
        x = hybrid_layer(x, norm1_g[layer], w_in[layer], pool_w[layer], pool_scale[layer],
                         lb_all[layer], rec_norm_g[layer], w_out[layer])
    return rmsnorm(x, final_norm_g)


import jax as _jax
import jax.numpy as _jnp

TWIN_FORMAT = 'train_step'
FWD_PARAMS = ['x', 'norm1_g', 'w_in', 'pool_w', 'pool_scale', 'lb_logits', 'rec_norm_g', 'w_out', 'final_norm_g']
TWIN_WEIGHTS = ['norm1_g', 'w_in', 'pool_w', 'pool_scale', 'lb_logits', 'rec_norm_g', 'w_out', 'final_norm_g']
TWIN_DIFF_INPUT = 'x'
TWIN_INPUTS = ['x', 'norm1_g', 'w_in', 'pool_w', 'pool_scale', 'lb_logits', 'rec_norm_g', 'w_out', 'final_norm_g', 'loss_target', 'm_norm1_g', 'm_w_in', 'm_pool_w', 'm_pool_scale', 'm_lb_logits', 'm_rec_norm_g', 'm_w_out', 'm_final_norm_g', 'v_norm1_g', 'v_w_in', 'v_pool_w', 'v_pool_scale', 'v_lb_logits', 'v_rec_norm_g', 'v_w_out', 'v_final_norm_g']
TWIN_OUTPUTS = ['loss', 'grad_x', 'grad_norm1_g', 'grad_w_in', 'grad_pool_w', 'grad_pool_scale', 'grad_lb_logits', 'grad_rec_norm_g', 'grad_w_out', 'grad_final_norm_g', 'delta_norm1_g', 'delta_w_in', 'delta_pool_w', 'delta_pool_scale', 'delta_lb_logits', 'delta_rec_norm_g', 'delta_w_out', 'delta_final_norm_g', 'new_m_norm1_g', 'new_m_w_in', 'new_m_pool_w', 'new_m_pool_scale', 'new_m_lb_logits', 'new_m_rec_norm_g', 'new_m_w_out', 'new_m_final_norm_g', 'new_v_norm1_g', 'new_v_w_in', 'new_v_pool_w', 'new_v_pool_scale', 'new_v_lb_logits', 'new_v_rec_norm_g', 'new_v_w_out', 'new_v_final_norm_g']
TWIN_LEAF_KINDS = {'loss': 'loss', 'grad_x': 'grad_x', 'grad_norm1_g': 'grad_w', 'grad_w_in': 'grad_w', 'grad_pool_w': 'grad_w', 'grad_pool_scale': 'grad_w', 'grad_lb_logits': 'grad_w', 'grad_rec_norm_g': 'grad_w', 'grad_w_out': 'grad_w', 'grad_final_norm_g': 'grad_w', 'delta_norm1_g': 'delta_w', 'delta_w_in': 'delta_w', 'delta_pool_w': 'delta_w', 'delta_pool_scale': 'delta_w', 'delta_lb_logits': 'delta_w', 'delta_rec_norm_g': 'delta_w', 'delta_w_out': 'delta_w', 'delta_final_norm_g': 'delta_w', 'new_m_norm1_g': 'new_m', 'new_m_w_in': 'new_m', 'new_m_pool_w': 'new_m', 'new_m_pool_scale': 'new_m', 'new_m_lb_logits': 'new_m', 'new_m_rec_norm_g': 'new_m', 'new_m_w_out': 'new_m', 'new_m_final_norm_g': 'new_m', 'new_v_norm1_g': 'new_v', 'new_v_w_in': 'new_v', 'new_v_pool_w': 'new_v', 'new_v_pool_scale': 'new_v', 'new_v_lb_logits': 'new_v', 'new_v_rec_norm_g': 'new_v', 'new_v_w_out': 'new_v', 'new_v_final_norm_g': 'new_v'}


def _forward(args):
    return _fwd_reference(*[args[k] for k in FWD_PARAMS])


def _output_shape():
    out = _jax.eval_shape(lambda: _forward(_fwd_setup_inputs(0)))
    return out.shape, out.dtype

N_MICROBATCH = 1
ADAM_LR = 0.001
ADAM_B1 = 0.9
ADAM_B2 = 0.999
ADAM_EPS = 1e-08
ADAM_WD = 0.01
ADAM_STEP = 10
PER_EXAMPLE_BATCH_AXIS = {'x': 0, 'loss_target': 0}
SHARED_INPUTS = []
_WEIGHT_DTYPES = {'norm1_g': _jnp.float32, 'w_in': _jnp.float32, 'pool_w': _jnp.float32, 'pool_scale': _jnp.float32, 'lb_logits': _jnp.float32, 'rec_norm_g': _jnp.float32, 'w_out': _jnp.float32, 'final_norm_g': _jnp.float32}
MOMENT_SCALE = {'norm1_g': 8.858182e-02, 'w_in': 3.604176e-02, 'pool_w': 4.042643e-02, 'pool_scale': 4.146428e-02, 'lb_logits': 4.058581e-03, 'rec_norm_g': 4.909875e-02, 'w_out': 6.123138e-02, 'final_norm_g': 1.599697e+01}


def _to_microbatches(a, axis):
    t = _jnp.moveaxis(a, axis, 0)
    t = t.reshape((N_MICROBATCH, t.shape[0] // N_MICROBATCH) + t.shape[1:])
    return _jnp.moveaxis(t, 1, axis + 1)


def setup_inputs(seed: int = 0) -> dict:
    inp = _fwd_setup_inputs(seed)
    key = _jax.random.fold_in(_jax.random.key(seed), 7919)
    shape, _ = _output_shape()
    out = dict(inp)
    out["loss_target"] = _jax.random.normal(_jax.random.fold_in(key, 0), shape, _jnp.float32)
    for i, name in enumerate(TWIN_WEIGHTS):
        w = inp[name].astype(_jnp.float32)
        if MOMENT_SCALE is None:
            s = _jnp.sqrt(_jnp.mean(_jnp.square(w)) + 1e-30)
        else:
            s = MOMENT_SCALE[name]
        km, kv = _jax.random.split(_jax.random.fold_in(key, i + 1))
        out[name] = w
        out["m_" + name] = s * _jax.random.normal(km, w.shape, _jnp.float32)
        out["v_" + name] = (s * s) * _jax.random.uniform(kv, w.shape, _jnp.float32, 0.5, 1.5)
    if N_MICROBATCH > 1:
        for name, axis in PER_EXAMPLE_BATCH_AXIS.items():
            out[name] = _to_microbatches(out[name], axis)
    return {'x': out['x'], 'norm1_g': out['norm1_g'], 'w_in': out['w_in'], 'pool_w': out['pool_w'], 'pool_scale': out['pool_scale'], 'lb_logits': out['lb_logits'], 'rec_norm_g': out['rec_norm_g'], 'w_out': out['w_out'], 'final_norm_g': out['final_norm_g'], 'loss_target': out['loss_target'], 'm_norm1_g': out['m_norm1_g'], 'm_w_in': out['m_w_in'], 'm_pool_w': out['m_pool_w'], 'm_pool_scale': out['m_pool_scale'], 'm_lb_logits': out['m_lb_logits'], 'm_rec_norm_g': out['m_rec_norm_g'], 'm_w_out': out['m_w_out'], 'm_final_norm_g': out['m_final_norm_g'], 'v_norm1_g': out['v_norm1_g'], 'v_w_in': out['v_w_in'], 'v_pool_w': out['v_pool_w'], 'v_pool_scale': out['v_pool_scale'], 'v_lb_logits': out['v_lb_logits'], 'v_rec_norm_g': out['v_rec_norm_g'], 'v_w_out': out['v_w_out'], 'v_final_norm_g': out['v_final_norm_g']}


def _loss(weights, diff, rest, loss_target):
    with _jax.named_scope("forward"):
        args = {**rest, TWIN_DIFF_INPUT: diff, **{k: w.astype(_WEIGHT_DTYPES[k]) for k, w in weights.items()}}
        y = _forward(args)
    with _jax.named_scope("loss_head"):
        err = _jnp.square(y.astype(_jnp.float32) - loss_target)
        return 0.5 * _jnp.sum(_jnp.mean(err, axis=-1)) if err.ndim else 0.5 * err


def _adamw(w, g, m, v):
    m = ADAM_B1 * m + (1.0 - ADAM_B1) * g
    v = ADAM_B2 * v + (1.0 - ADAM_B2) * _jnp.square(g)
    m_hat = m / (1.0 - ADAM_B1 ** ADAM_STEP)
    v_hat = v / (1.0 - ADAM_B2 ** ADAM_STEP)
    delta = -ADAM_LR * (m_hat / (_jnp.sqrt(v_hat) + ADAM_EPS) + ADAM_WD * w)
    return delta, m, v


def reference(x, norm1_g, w_in, pool_w, pool_scale, lb_logits, rec_norm_g, w_out, final_norm_g, loss_target, m_norm1_g, m_w_in, m_pool_w, m_pool_scale, m_lb_logits, m_rec_norm_g, m_w_out, m_final_norm_g, v_norm1_g, v_w_in, v_pool_w, v_pool_scale, v_lb_logits, v_rec_norm_g, v_w_out, v_final_norm_g):
    given = dict(x=x, norm1_g=norm1_g, w_in=w_in, pool_w=pool_w, pool_scale=pool_scale, lb_logits=lb_logits, rec_norm_g=rec_norm_g, w_out=w_out, final_norm_g=final_norm_g, loss_target=loss_target, m_norm1_g=m_norm1_g, m_w_in=m_w_in, m_pool_w=m_pool_w, m_pool_scale=m_pool_scale, m_lb_logits=m_lb_logits, m_rec_norm_g=m_rec_norm_g, m_w_out=m_w_out, m_final_norm_g=m_final_norm_g, v_norm1_g=v_norm1_g, v_w_in=v_w_in, v_pool_w=v_pool_w, v_pool_scale=v_pool_scale, v_lb_logits=v_lb_logits, v_rec_norm_g=v_rec_norm_g, v_w_out=v_w_out, v_final_norm_g=v_final_norm_g)
    weights = {n: given[n] for n in TWIN_WEIGHTS}
    shared = {n: given[n] for n in SHARED_INPUTS}
    per_example = {n: given[n] for n in ['x']}
    grad_fn = _jax.value_and_grad(_loss, argnums=(0, 1))

    def one_microbatch(ex, loss_target):
        ex = dict(ex)
        diff = ex.pop(TWIN_DIFF_INPUT)
        return grad_fn(weights, diff, {**shared, **ex}, loss_target)

    if N_MICROBATCH == 1:
        loss, (grad_w, grad_x) = one_microbatch(per_example, given["loss_target"])
    else:
        def body(carry, xs):
            loss_sum, grad_sum = carry
            l_k, (gw_k, gx_k) = one_microbatch(xs[0], xs[1])
            with _jax.named_scope("update"):
                return (loss_sum + l_k, _jax.tree.map(_jnp.add, grad_sum, gw_k)), gx_k

        init = (_jnp.zeros((), _jnp.float32), _jax.tree.map(_jnp.zeros_like, weights))
        (loss, grad_w), grad_x = _jax.lax.scan(body, init, (per_example, given["loss_target"]))
    with _jax.named_scope("update"):
        delta_w, new_m, new_v = {}, {}, {}
        for n in TWIN_WEIGHTS:
            delta_w[n], new_m[n], new_v[n] = _adamw(weights[n], grad_w[n], given["m_" + n], given["v_" + n])
    return (loss, grad_x, *[grad_w[n] for n in TWIN_WEIGHTS], *[delta_w[n] for n in TWIN_WEIGHTS],
            *[new_m[n] for n in TWIN_WEIGHTS], *[new_v[n] for n in TWIN_WEIGHTS])
```

```python
import functools

import numpy as np
import jax
import jax.numpy as jnp
from jax import lax
from jax.experimental import pallas as pl
from jax.experimental.pallas import tpu as pltpu

F32 = jnp.float32
BF16 = jnp.bfloat16

SEQ = 2048
D_MODEL = 1024
D_PROJ = 6144
N_SEC = 6
N_GROUPS = 4
PG = 256
N_HEADS = 8
HEAD = 128
CHUNK = 64
N_LEVELS = 6
N_SHARDS = 4
W_IN_SHARD = D_PROJ // N_SHARDS
W_OUT_SHARD = 2048 // N_SHARDS
PW_SHARD = PG // N_SHARDS
COL_BLK = 512
EPS = 1e-6

ADAM_LR = 0.001
ADAM_B1 = 0.9
ADAM_B2 = 0.999
ADAM_EPS = 1e-08
ADAM_WD = 0.01
ADAM_STEP = 10

V7X_VMEM_LIMIT = 56 * 1024 * 1024
MESH = pl.DeviceIdType.MESH


def _params(**kw):
    return pltpu.CompilerParams(vmem_limit_bytes=V7X_VMEM_LIMIT, **kw)


def _sig(x):
    return 1.0 / (1.0 + jnp.exp(-x))


def _dot(a, b):
    return jnp.dot(a, b, preferred_element_type=F32)


def _dot_nt(a, b):
    return lax.dot_general(a, b, (((1,), (1,)), ((), ())), preferred_element_type=F32)


def _dot_tn(a, b):
    return lax.dot_general(a, b, (((0,), (0,)), ((), ())), preferred_element_type=F32)


def _split3(a):
    p1 = a.astype(BF16)
    r1 = a - p1.astype(F32)
    p2 = r1.astype(BF16)
    p3 = (r1 - p2.astype(F32)).astype(BF16)
    return jnp.concatenate([p1, p2, p3], axis=-1)


def _dot3(w01, a):
    n = a.shape[-1]
    r = _dot(w01, _split3(a))
    return r[:, :n] + r[:, n:2 * n] + r[:, 2 * n:]


def _chunk_constants():
    c = CHUNK
    w = np.zeros((2 + N_LEVELS, c, c), np.float32)
    j = np.arange(c)
    for t in range(c):
        w[0, t] = j <= t
        w[1, t] = j > t
        for l in range(N_LEVELS):
            m = 1 << l
            pos = t % (2 * m)
            mid = t - pos + m - 1
            w[2 + l, t] = ((j > mid) & (j <= t)) if pos >= m else ((j > t) & (j <= mid))
    tt, ss = np.meshgrid(j, j, indexing="ij")
    x = tt ^ ss
    hb = np.full((c, c), -1, np.int32)
    for l in range(N_LEVELS):
        hb[x >= (1 << l)] = l
    lvl = np.where(tt > ss, hb, -1).astype(np.int32)
    tri_t = (ss >= tt).astype(np.float32)
    return w.reshape(-1, c), lvl, np.ascontiguousarray(lvl.T), tri_t


def _in_proj(x, g1, w_in_g):
    n_col = D_PROJ // COL_BLK
    per_shard = W_IN_SHARD // COL_BLK
    rows = 256

    def body(x_ref, g_ref, w_ref, proj_ref, h_ref):
        @pl.when(pl.program_id(0) == 0)
        def _():
            def norm(i, _):
                r = pl.ds(pl.multiple_of(i * rows, rows), rows)
                xv = x_ref[r, :]
                inv = lax.rsqrt(jnp.mean(xv * xv, axis=-1, keepdims=True) + EPS)
                h_ref[r, :] = (xv * inv * g_ref[...]).astype(BF16)
                return 0
            lax.fori_loop(0, SEQ // rows, norm, 0)

        def mm(i, _):
            r = pl.ds(pl.multiple_of(i * rows, rows), rows)
            proj_ref[r, :] = _dot(h_ref[r, :], w_ref[...])
            return 0
        lax.fori_loop(0, SEQ // rows, mm, 0)

    return pl.pallas_call(
        body, name="in_proj", grid=(n_col,),
        in_specs=[pl.BlockSpec((SEQ, D_MODEL), lambda j: (0, 0)),
                  pl.BlockSpec((1, D_MODEL), lambda j: (0, 0)),
                  pl.BlockSpec((None, D_MODEL, COL_BLK), lambda j: (j // per_shard, 0, j % per_shard))],
        out_specs=[pl.BlockSpec((SEQ, COL_BLK), lambda j: (0, j)),
                   pl.BlockSpec((SEQ, D_MODEL), lambda j: (0, 0))],
        out_shape=[jax.ShapeDtypeStruct((SEQ, D_PROJ), F32),
                   jax.ShapeDtypeStruct((SEQ, D_MODEL), BF16)],
        compiler_params=_params(dimension_semantics=("arbitrary",)),
    )(x, g1, w_in_g)


POOL_ROWS = 256
POOL_HALO = 16


def _window_sums(ext, g, shift_of):
    s = ext
    for k in range(N_GROUPS):
        s = jnp.where(k <= g, s + pltpu.roll(s, shift_of(k), 0), s)
    return s


def _pool_diff(u_ref, i, g):
    n = POOL_ROWS + POOL_HALO
    r0 = i * POOL_ROWS
    cur = u_ref[pl.ds(pl.multiple_of(r0, POOL_ROWS), POOL_ROWS), :]
    before = u_ref[pl.ds(pl.multiple_of(jnp.maximum(r0 - POOL_HALO, 0), 8), POOL_HALO), :]
    before = jnp.where(i > 0, before, 0.0)
    ext = jnp.concatenate([before, cur], axis=0)
    s = _window_sums(ext, g, lambda k: 1 << k)[POOL_HALO:, :]
    t = r0 + lax.broadcasted_iota(jnp.int32, (POOL_ROWS, 1), 0)
    width = (2 << g).astype(F32)
    inv_count = 1.0 / jnp.minimum((t + 1).astype(F32), width)
    return s * inv_count - cur, inv_count


def _pool_fwd(proj, pw_g, pool_scale):
    def body(u_ref, gate_ref, pw_ref, sc_ref, y_ref):
        g = pl.program_id(0)

        def step(i, _):
            r = pl.ds(pl.multiple_of(i * POOL_ROWS, POOL_ROWS), POOL_ROWS)
            d, _ = _pool_diff(u_ref, i, g)
            mixed = _dot(d.astype(BF16), pw_ref[...])
            gate = gate_ref[r, :]
            y_ref[r, :] = (mixed * sc_ref[...] * (gate * _sig(gate))).astype(BF16)
            return 0
        lax.fori_loop(0, SEQ // POOL_ROWS, step, 0)

    return pl.pallas_call(
        body, name="pool_fwd", grid=(N_GROUPS,),
        in_specs=[pl.BlockSpec((SEQ, PG), lambda g: (0, g)),
                  pl.BlockSpec((SEQ, PG), lambda g: (0, N_GROUPS + g)),
                  pl.BlockSpec((None, PG, PG), lambda g: (g, 0, 0)),
                  pl.BlockSpec((1, PG), lambda g: (0, g))],
        out_specs=pl.BlockSpec((SEQ, PG), lambda g: (0, g)),
        out_shape=jax.ShapeDtypeStruct((SEQ, D_MODEL), BF16),
        compiler_params=_params(dimension_semantics=("arbitrary",)),
    )(proj, proj, pw_g, pool_scale)


REC_ROWS = 512
REC_CHUNKS = REC_ROWS // CHUNK
N_REC_BLK = SEQ // REC_ROWS
SEC_BLK = D_MODEL // HEAD


def _lower_bound(lb_ref):
    l0 = lb_ref[0:1, :]
    l1 = lb_ref[1:2, :]
    mx = jnp.maximum(l0, l1)
    e0 = jnp.exp(l0 - mx)
    e1 = jnp.exp(l1 - mx)
    return e0 / (e0 + e1)


def _gates(q, fl, lb):
    qs = q * _sig(q)
    sf = _sig(fl)
    f = lb + (1.0 - lb) * sf
    return qs, sf, f, 1.0 - f, jnp.log(f)


def _level_factors(cs, qs, k):
    t = lax.broadcasted_iota(jnp.int32, (CHUNK, HEAD), 0)
    out = []
    for l in range(N_LEVELS):
        e = jnp.exp(cs[(2 + l) * CHUNK:(3 + l) * CHUNK, :])
        up = ((t >> l) & 1) == 1
        x = jnp.where(up, qs, k) * e
        hi = x.astype(BF16)
        out.append((hi, (x - hi.astype(F32)).astype(BF16), e, up))
    return out


def _rec_fwd(proj, lb_logits, rec_g, wseg, lvl):
    def body(q_ref, f_ref, i_ref, rg_ref, lb_ref, g_ref, w_ref, lvl_ref, y_ref, o_ref, stp_ref, st_ref):
        @pl.when(pl.program_id(1) == 0)
        def _():
            st_ref[...] = jnp.zeros_like(st_ref)
        lb = _lower_bound(lb_ref)

        def chunk(c, _):
            r = pl.ds(pl.multiple_of(c * CHUNK, CHUNK), CHUNK)
            v = i_ref[r, :]
            rg = rg_ref[r, :]
            qs, _, _, k, g = _gates(q_ref[r, :], f_ref[r, :], lb)
            cs = _dot3(w_ref[...], g)
            big_g = cs[0:CHUNK, :]
            g_rev = cs[CHUNK:2 * CHUNK, :]
            a = jnp.zeros((CHUNK, CHUNK), F32)
            lv = lvl_ref[...]
            for l, (xl, _, _, _) in enumerate(_level_factors(cs, qs, k)):
                a = a + jnp.where(lv == l, _dot_nt(xl, xl), 0.0)
            st = st_ref[...]
            stp_ref[c] = st
            vb = v.astype(BF16)
            diag = jnp.sum(qs * k, axis=-1, keepdims=True)
            o = (_dot(a.astype(BF16), vb) + diag * v
                 + _dot_nt((qs * jnp.exp(big_g)).astype(BF16), st.astype(BF16)))
            kdec = (k * jnp.exp(g_rev)).astype(BF16)
            st_ref[...] = st * jnp.exp(big_g[CHUNK - 1:CHUNK, :]) + _dot_tn(vb, kdec)
            o_ref[r, :] = o
            inv = lax.rsqrt(jnp.mean(o * o, axis=-1, keepdims=True) + EPS)
            y_ref[r, :] = (o * inv * g_ref[...] * (rg * _sig(rg))).astype(BF16)
            return 0
        lax.fori_loop(0, REC_CHUNKS, chunk, 0)

    sec = lambda n: pl.BlockSpec((REC_ROWS, HEAD), lambda h, b: (b, n * SEC_BLK + h))
    vec = lambda rows: pl.BlockSpec((rows, HEAD), lambda h, b: (0, h))
    full = lambda a: pl.BlockSpec(a.shape, lambda h, b: (0,) * a.ndim)
    return pl.pallas_call(
        body, name="rec_fwd", grid=(N_HEADS, N_REC_BLK),
        in_specs=[sec(2), sec(3), sec(4), sec(5), vec(2), vec(1), full(wseg), full(lvl)],
        out_specs=[pl.BlockSpec((REC_ROWS, HEAD), lambda h, b: (b, h)),
                   pl.BlockSpec((REC_ROWS, HEAD), lambda h, b: (b, h)),
                   pl.BlockSpec((None, REC_CHUNKS, HEAD, HEAD), lambda h, b: (h, b, 0, 0))],
        out_shape=[jax.ShapeDtypeStruct((SEQ, D_MODEL), BF16),
                   jax.ShapeDtypeStruct((SEQ, D_MODEL), F32),
                   jax.ShapeDtypeStruct((N_HEADS, SEQ // CHUNK, HEAD, HEAD), F32)],
        scratch_shapes=[pltpu.VMEM((HEAD, HEAD), F32)],
        compiler_params=_params(dimension_semantics=("arbitrary", "arbitrary")),
    )(proj, proj, proj, proj, lb_logits, rec_g, wseg, lvl)


OUT_ROWS = 256


def _out_proj_loss(y_pool, y_rec, w_out_g, x, target, gf):
    def body(yp_ref, yr_ref, w_ref, x_ref, t_ref, gf_ref, dout_ref, doutb_ref, part_ref):
        @pl.when(pl.program_id(0) == 0)
        def _():
            part_ref[...] = jnp.zeros_like(part_ref)
        out = (x_ref[...] + _dot(yp_ref[...], w_ref[0:D_MODEL, :])
               + _dot(yr_ref[...], w_ref[D_MODEL:2 * D_MODEL, :]))
        inv = lax.rsqrt(jnp.mean(out * out, axis=-1, keepdims=True) + EPS)
        gf_v = gf_ref[...]
        diff = out * inv * gf_v - t_ref[...]
        dyf = diff * (1.0 / D_MODEL)
        a = dyf * gf_v
        dout = inv * a - out * (inv * inv * inv) * jnp.mean(a * out, axis=-1, keepdims=True)
        dout_ref[...] = dout
        doutb_ref[...] = dout.astype(BF16)
        part_ref[0:1, :] += jnp.sum(dyf * out * inv, axis=0, keepdims=True)
        part_ref[1:2, :] += jnp.sum(diff * diff, axis=0, keepdims=True)

    row = lambda n: pl.BlockSpec((OUT_ROWS, n), lambda i: (i, 0))
    return pl.pallas_call(
        body, name="out_proj_loss", grid=(SEQ // OUT_ROWS,),
        in_specs=[row(D_MODEL), row(D_MODEL), pl.BlockSpec((2 * D_MODEL, D_MODEL), lambda i: (0, 0)),
                  row(D_MODEL), row(D_MODEL), pl.BlockSpec((1, D_MODEL), lambda i: (0, 0))],
        out_specs=[row(D_MODEL), row(D_MODEL), pl.BlockSpec((8, D_MODEL), lambda i: (0, 0))],
        out_shape=[jax.ShapeDtypeStruct((SEQ, D_MODEL), F32),
                   jax.ShapeDtypeStruct((SEQ, D_MODEL), BF16),
                   jax.ShapeDtypeStruct((8, D_MODEL), F32)],
        compiler_params=_params(dimension_semantics=("arbitrary",)),
    )(y_pool, y_rec, w_out_g, x, target, gf)


def _grad_w_out(y_pool, y_rec, dout_b):
    blk = 256
    per = D_MODEL // blk

    def body(yp_ref, yr_ref, d_ref, o_ref):
        j = pl.program_id(0)

        @pl.when(j < per)
        def _():
            o_ref[...] = _dot_tn(yp_ref[...], d_ref[...])

        @pl.when(j >= per)
        def _():
            o_ref[...] = _dot_tn(yr_ref[...], d_ref[...])

    return pl.pallas_call(
        body, name="grad_w_out", grid=(2 * per,),
        in_specs=[pl.BlockSpec((SEQ, blk), lambda j: (0, jnp.minimum(j, per - 1))),
                  pl.BlockSpec((SEQ, blk), lambda j: (0, jnp.maximum(j - per, 0))),
                  pl.BlockSpec((SEQ, D_MODEL), lambda j: (0, 0))],
        out_specs=pl.BlockSpec((blk, D_MODEL), lambda j: (j, 0)),
        out_shape=jax.ShapeDtypeStruct((2 * D_MODEL, D_MODEL), F32),
        compiler_params=_params(dimension_semantics=("arbitrary",)),
    )(y_pool, y_rec, dout_b)


def _pool_bwd(proj, dout_b, w_out_g, pw_g, pool_scale):
    n = POOL_ROWS + POOL_HALO

    def body(u_ref, gate_ref, d_ref, wo_ref, pw_ref, sc_ref,
             du_ref, dgate_ref, dpw_ref, dsc_ref, dd_ref, ddw_ref):
        g = pl.program_id(0)
        dpw_ref[...] = jnp.zeros_like(dpw_ref)
        dsc_ref[...] = jnp.zeros_like(dsc_ref)

        def first(i, _):
            r = pl.ds(pl.multiple_of(i * POOL_ROWS, POOL_ROWS), POOL_ROWS)
            d, inv_count = _pool_diff(u_ref, i, g)
            db = d.astype(BF16)
            mixed = _dot(db, pw_ref[...])
            gate = gate_ref[r, :]
            sg = _sig(gate)
            silu = gate * sg
            dy = _dot_nt(d_ref[r, :], wo_ref[...])
            sc = sc_ref[...]
            dmixed = dy * silu * sc
            dgate_ref[r, :] = (dy * mixed * sc * (sg * (1.0 + gate * (1.0 - sg)))).astype(BF16)
            dsc_ref[...] += jnp.sum(dy * silu * mixed, axis=0, keepdims=True)
            dmb = dmixed.astype(BF16)
            dpw_ref[...] += _dot_tn(db, dmb)
            dd = _dot_nt(dmb, pw_ref[...])
            dd_ref[r, :] = dd
            ddw_ref[r, :] = dd * inv_count
            return 0
        lax.fori_loop(0, SEQ // POOL_ROWS, first, 0)

        def second(i, _):
            r0 = i * POOL_ROWS
            r = pl.ds(pl.multiple_of(r0, POOL_ROWS), POOL_ROWS)
            last = i == SEQ // POOL_ROWS - 1
            after = ddw_ref[pl.ds(pl.multiple_of(jnp.minimum(r0 + POOL_ROWS, SEQ - POOL_HALO), 8), POOL_HALO), :]
            after = jnp.where(last, 0.0, after)
            ext = jnp.concatenate([ddw_ref[r, :], after], axis=0)
            s = _window_sums(ext, g, lambda k: n - (1 << k))[:POOL_ROWS, :]
            du_ref[r, :] = (s - dd_ref[r, :]).astype(BF16)
            return 0
        lax.fori_loop(0, SEQ // POOL_ROWS, second, 0)

    return pl.pallas_call(
        body, name="pool_bwd", grid=(N_GROUPS,),
        in_specs=[pl.BlockSpec((SEQ, PG), lambda g: (0, g)),
                  pl.BlockSpec((SEQ, PG), lambda g: (0, N_GROUPS + g)),
                  pl.BlockSpec((SEQ, D_MODEL), lambda g: (0, 0)),
                  pl.BlockSpec((PG, D_MODEL), lambda g: (g, 0)),
                  pl.BlockSpec((None, PG, PG), lambda g: (g, 0, 0)),
                  pl.BlockSpec((1, PG), lambda g: (0, g))],
        out_specs=[pl.BlockSpec((SEQ, PG), lambda g: (0, g)),
                   pl.BlockSpec((SEQ, PG), lambda g: (0, g)),
                   pl.BlockSpec((None, PG, PG), lambda g: (g, 0, 0)),
                   pl.BlockSpec((1, PG), lambda g: (0, g))],
        out_shape=[jax.ShapeDtypeStruct((SEQ, D_MODEL), BF16),
                   jax.ShapeDtypeStruct((SEQ, D_MODEL), BF16),
                   jax.ShapeDtypeStruct((N_GROUPS, PG, PG), F32),
                   jax.ShapeDtypeStruct((1, D_MODEL), F32)],
        scratch_shapes=[pltpu.VMEM((SEQ, PG), F32), pltpu.VMEM((SEQ, PG), F32)],
        compiler_params=_params(dimension_semantics=("arbitrary",)),
    )(proj, proj, dout_b, w_out_g, pw_g, pool_scale)


def _rec_bwd(proj, o_raw, st_prev, dout_b, w_out_g, lb_logits, rec_g, wseg, lvl, lvl_t, tri_t):
    def body(q_ref, f_ref, i_ref, rg_ref, o_ref, stp_ref, d_ref, wo_ref, lb_ref, g_ref,
             w_ref, lvl_ref, lvlt_ref, tri_ref,
             dq_ref, df_ref, di_ref, drg_ref, part_ref, dst_ref):
        @pl.when(pl.program_id(1) == 0)
        def _():
            dst_ref[...] = jnp.zeros_like(dst_ref)
            part_ref[...] = jnp.zeros_like(part_ref)
        lb = _lower_bound(lb_ref)
        grec = g_ref[...]

        def chunk(cc, _):
            c = REC_CHUNKS - 1 - cc
            r = pl.ds(pl.multiple_of(c * CHUNK, CHUNK), CHUNK)
            q = q_ref[r, :]
            v = i_ref[r, :]
            rg = rg_ref[r, :]
            o = o_ref[r, :]
            dy = _dot_nt(d_ref[r, :], wo_ref[...])
            sg = _sig(rg)
            silu = rg * sg
            inv = lax.rsqrt(jnp.mean(o * o, axis=-1, keepdims=True) + EPS)
            recn = o * inv
            drg_ref[r, :] = (dy * recn * grec * (sg * (1.0 + rg * (1.0 - sg)))).astype(BF16)
            part_ref[0:1, :] += jnp.sum(dy * silu * recn, axis=0, keepdims=True)
            drecn = dy * silu * grec
            do = inv * drecn - o * (inv * inv * inv) * jnp.mean(drecn * o, axis=-1, keepdims=True)
            qs, sf, f, k, g = _gates(q, f_ref[r, :], lb)
            cs = _dot3(w_ref[...], g)
            e_g = jnp.exp(cs[0:CHUNK, :])
            e_rev = jnp.exp(cs[CHUNK:2 * CHUNK, :])
            e_last = e_g[CHUNK - 1:CHUNK, :]
            levels = _level_factors(cs, qs, k)
            lv = lvl_ref[...]
            lvt = lvlt_ref[...]
            a_t = jnp.zeros((CHUNK, CHUNK), F32)
            for l, (xl, _, _, _) in enumerate(levels):
                a_t = a_t + jnp.where(lvt == l, _dot_nt(xl, xl), 0.0)
            dob = do.astype(BF16)
            vb = v.astype(BF16)
            d_a = _dot_nt(dob, vb)
            d_at = _dot_nt(vb, dob)
            stp = stp_ref[c]
            dst = dst_ref[...]
            dstb = dst.astype(BF16)
            q_g = qs * e_g
            kdec = k * e_rev
            diag = jnp.sum(qs * k, axis=-1, keepdims=True)
            dv = _dot(a_t.astype(BF16), dob) + diag * do + _dot_nt(kdec.astype(BF16), dstb)
            dq_g = _dot(dob, stp.astype(BF16))
            dkdec = _dot(vb, dstb)
            de_last = jnp.sum(stp * dst, axis=0, keepdims=True)
            dst_ref[...] = dst * e_last + _dot_tn(dob, q_g.astype(BF16))
            dqs_i = jnp.zeros((CHUNK, HEAD), F32)
            dk_i = jnp.zeros((CHUNK, HEAD), F32)
            for l, (xl, xlo, e, up) in enumerate(levels):
                z = jnp.where(lv == l, d_a, jnp.where(lvt == l, d_at, 0.0))
                tmp = _dot(z.astype(BF16), jnp.concatenate([xl, xlo], axis=-1))
                tmp = (tmp[:, :HEAD] + tmp[:, HEAD:]) * e
                dqs_i = dqs_i + jnp.where(up, tmp, 0.0)
                dk_i = dk_i + jnp.where(up, 0.0, tmp)
            ddiag = jnp.sum(do * v, axis=-1, keepdims=True)
            dqs = dqs_i + ddiag * k + dq_g * e_g
            dk = dk_i + ddiag * qs + dkdec * e_rev
            dg_rev = dkdec * kdec
            dg_last = jnp.sum(dg_rev, axis=0, keepdims=True) + de_last * e_last
            dbig_g = qs * dqs_i - k * dk_i + dq_g * q_g - dg_rev
            dg = _dot3(tri_ref[...], dbig_g) + dg_last
            df = dg / f - dk
            df_ref[r, :] = (df * (1.0 - lb) * sf * (1.0 - sf)).astype(BF16)
            part_ref[1:2, :] += jnp.sum(df * (1.0 - sf), axis=0, keepdims=True)
            sq = _sig(q)
            dq_ref[r, :] = (dqs * (sq * (1.0 + q * (1.0 - sq)))).astype(BF16)
            di_ref[r, :] = dv.astype(BF16)
            return 0
        lax.fori_loop(0, REC_CHUNKS, chunk, 0)

    rev = lambda b: N_REC_BLK - 1 - b
    sec = lambda n: pl.BlockSpec((REC_ROWS, HEAD), lambda h, b: (rev(b), n * SEC_BLK + h))
    col = pl.BlockSpec((REC_ROWS, HEAD), lambda h, b: (rev(b), h))
    vec = lambda rows: pl.BlockSpec((rows, HEAD), lambda h, b: (0, h))
    full = lambda a: pl.BlockSpec(a.shape, lambda h, b: (0,) * a.ndim)
    return pl.pallas_call(
        body, name="rec_bwd", grid=(N_HEADS, N_REC_BLK),
        in_specs=[sec(2), sec(3), sec(4), sec(5), col,
                  pl.BlockSpec((None, REC_CHUNKS, HEAD, HEAD), lambda h, b: (h, rev(b), 0, 0)),
                  pl.BlockSpec((REC_ROWS, D_MODEL), lambda h, b: (rev(b), 0)),
                  pl.BlockSpec((HEAD, D_MODEL), lambda h, b: (SEC_BLK + h, 0)),
                  vec(2), vec(1), full(wseg), full(lvl), full(lvl_t), full(tri_t)],
        out_specs=[col, col, col, col, vec(8)],
        out_shape=[jax.ShapeDtypeStruct((SEQ, D_MODEL), BF16)] * 4
                  + [jax.ShapeDtypeStruct((8, D_MODEL), F32)],
        scratch_shapes=[pltpu.VMEM((HEAD, HEAD), F32)],
        compiler_params=_params(dimension_semantics=("arbitrary", "arbitrary")),
    )(proj, proj, proj, proj, o_raw, st_prev, dout_b, w_out_g, lb_logits, rec_g, wseg, lvl, lvl_t, tri_t)


def _grad_x(dproj, w_in_g, x, g1, dout):
    rows = 256
    per_shard = W_IN_SHARD // COL_BLK
    per_sec = D_MODEL // COL_BLK

    def body(*refs):
        dp_refs = refs[:N_SEC]
        w_ref, x_ref, g_ref, dout_ref, dx_ref, part_ref = refs[N_SEC:]

        @pl.when(pl.program_id(0) == 0)
        def _():
            part_ref[...] = jnp.zeros_like(part_ref)
        dh = jnp.zeros((rows, D_MODEL), F32)
        for n in range(N_SEC):
            for p in range(per_sec):
                j = n * per_sec + p
                w = w_ref[j // per_shard, :, (j % per_shard) * COL_BLK:(j % per_shard + 1) * COL_BLK]
                dh = dh + _dot_nt(dp_refs[n][:, p * COL_BLK:(p + 1) * COL_BLK], w)
        xv = x_ref[...]
        inv = lax.rsqrt(jnp.mean(xv * xv, axis=-1, keepdims=True) + EPS)
        a = dh * g_ref[...]
        dx_ref[...] = (dout_ref[...] + inv * a
                       - xv * (inv * inv * inv) * jnp.mean(a * xv, axis=-1, keepdims=True))
        part_ref[0:1, :] += jnp.sum(dh * xv * inv, axis=0, keepdims=True)

    row = lambda: pl.BlockSpec((rows, D_MODEL), lambda i: (i, 0))
    return pl.pallas_call(
        body, name="grad_x", grid=(SEQ // rows,),
        in_specs=[row() for _ in range(N_SEC)]
                 + [pl.BlockSpec((N_SHARDS, D_MODEL, W_IN_SHARD), lambda i: (0, 0, 0)),
                    row(), pl.BlockSpec((1, D_MODEL), lambda i: (0, 0)), row()],
        out_specs=[row(), pl.BlockSpec((8, D_MODEL), lambda i: (0, 0))],
        out_shape=[jax.ShapeDtypeStruct((SEQ, D_MODEL), F32),
                   jax.ShapeDtypeStruct((8, D_MODEL), F32)],
        compiler_params=_params(dimension_semantics=("arbitrary",)),
    )(*dproj, w_in_g, x, g1, dout)


def _grad_w_in(h, dproj):
    per_shard = W_IN_SHARD // COL_BLK
    per_sec = D_MODEL // COL_BLK
    n_col = D_PROJ // COL_BLK

    def body(h_ref, *refs):
        dp_refs = refs[:N_SEC]
        o_ref = refs[N_SEC]
        j = pl.program_id(0)
        for n in range(N_SEC):
            @pl.when(j // per_sec == n)
            def _(n=n):
                o_ref[...] = _dot_tn(h_ref[...], dp_refs[n][...])

    def dp_spec(n):
        return pl.BlockSpec((SEQ, COL_BLK), lambda j: (0, jnp.where(j // per_sec == n, j % per_sec, 0)))

    return pl.pallas_call(
        body, name="grad_w_in", grid=(n_col,),
        in_specs=[pl.BlockSpec((SEQ, D_MODEL), lambda j: (0, 0))] + [dp_spec(n) for n in range(N_SEC)],
        out_specs=pl.BlockSpec((None, D_MODEL, COL_BLK), lambda j: (j // per_shard, 0, j % per_shard)),
        out_shape=jax.ShapeDtypeStruct((N_SHARDS, D_MODEL, W_IN_SHARD), F32),
        compiler_params=_params(dimension_semantics=("arbitrary",)),
    )(h, *dproj)


def _local_step(x, target, g1, w_in_g, pw_g, pool_scale, lb_logits, rec_g, w_out_g, gf):
    wseg, lvl, lvl_t, tri_t = _chunk_constants()
    wseg = jnp.asarray(wseg, BF16)
    tri_t = jnp.asarray(tri_t, BF16)
    lvl = jnp.asarray(lvl)
    lvl_t = jnp.asarray(lvl_t)
    proj, h = _in_proj(x, g1, w_in_g)
    y_pool = _pool_fwd(proj, pw_g, pool_scale)
    y_rec, o_raw, st_prev = _rec_fwd(proj, lb_logits, rec_g, wseg, lvl)
    dout, dout_b, part_out = _out_proj_loss(y_pool, y_rec, w_out_g, x, target, gf)
    gw_out = _grad_w_out(y_pool, y_rec, dout_b)
    du, dgate, gpw, dscale = _pool_bwd(proj, dout_b, w_out_g, pw_g, pool_scale)
    dq, df, di, drg, part_rec = _rec_bwd(proj, o_raw, st_prev, dout_b, w_out_g, lb_logits, rec_g,
                                         wseg, lvl, lvl_t, tri_t)
    dproj = (du, dgate, dq, df, di, drg)
    dx, part_x = _grad_x(dproj, w_in_g, x, g1, dout)
    gw_in = _grad_w_in(h, dproj)
    return dict(dx=dx, gw_in=gw_in, gw_out=gw_out, gpw=gpw, dscale=dscale,
                part_out=part_out, part_rec=part_rec, part_x=part_x)


CHIP_FLIPS = ((1, 0), (0, 1), (1, 1))
N_TENSORS = 3


def _place():
    return lax.axis_index("x"), lax.axis_index("y"), lax.axis_index("c")


def _remote(src, dst, send_sems, recv_sems, k, to):
    return pltpu.make_async_remote_copy(src_ref=src, dst_ref=dst, send_sem=send_sems.at[k],
                                        recv_sem=recv_sems.at[k], device_id=to, device_id_type=MESH)


def _half_rows(ref, c):
    half = ref.shape[-2] // 2
    rows = pl.ds(pl.multiple_of(c * half, half), half)
    return ref.at[:, rows, :] if len(ref.shape) == 3 else ref.at[rows, :]


def _gather_weights(w_in, w_out, pool_w):
    shards = (w_in, w_out, pool_w)
    n_fl = len(CHIP_FLIPS)

    def body(wi_ref, wo_ref, pw_ref, wi_g, wo_g, pw_g, send_sems, recv_sems):
        x, y, c = _place()
        s = 2 * x + y
        for src, dst in ((wi_ref, wi_g), (wo_ref, wo_g), (pw_ref, pw_g)):
            step = 64
            def cast(i, _, src=src, dst=dst):
                r = pl.ds(pl.multiple_of(i * step, step), step)
                dst.at[s][r, :] = src[r, :].astype(BF16)
                return 0
            lax.fori_loop(0, src.shape[0] // step, cast, 0)
        outs = (wi_g, wo_g, pw_g)

        def piece(t, shard, half):
            return _half_rows(outs[t].at[shard], half)

        first, passed = [], []
        for j, (fx, fy) in enumerate(CHIP_FLIPS):
            for t in range(N_TENSORS):
                cp = _remote(piece(t, s, c), piece(t, s, c), send_sems, recv_sems,
                             N_TENSORS * j + t, (x ^ fx, y ^ fy, c))
                cp.start()
                first.append(cp)
        for j, (fx, fy) in enumerate(CHIP_FLIPS):
            sj = 2 * (x ^ fx) + (y ^ fy)
            for t in range(N_TENSORS):
                k = N_TENSORS * j + t
                _remote(piece(t, sj, c), piece(t, sj, c), send_sems, recv_sems, k, (x, y, c)).wait_recv()
                cp = _remote(piece(t, sj, c), piece(t, sj, c), send_sems, recv_sems,
                             N_TENSORS * n_fl + k, (x, y, 1 - c))
                cp.start()
                passed.append(cp)
        for j, (fx, fy) in enumerate(CHIP_FLIPS):
            sj = 2 * (x ^ fx) + (y ^ fy)
            for t in range(N_TENSORS):
                k = N_TENSORS * n_fl + N_TENSORS * j + t
                _remote(piece(t, sj, 1 - c), piece(t, sj, 1 - c), send_sems, recv_sems, k, (x, y, c)).wait_recv()
        for cp in first + passed:
            cp.wait_send()

    n_sem = 2 * N_TENSORS * n_fl
    vmem = pl.BlockSpec(memory_space=pltpu.VMEM)
    return pl.pallas_call(
        body, name="gather_weights",
        in_specs=[vmem] * 3, out_specs=[vmem] * 3,
        out_shape=[jax.ShapeDtypeStruct((N_SHARDS,) + a.shape, BF16) for a in shards],
        scratch_shapes=[pltpu.SemaphoreType.DMA((n_sem,)), pltpu.SemaphoreType.DMA((n_sem,))],
        compiler_params=_params(),
    )(*shards)


def _gather_small(parts):
    def body(p_ref, out_ref, send_sems, recv_sems, local_sem):
        x, y, c = _place()
        me = 4 * x + 2 * y + c
        mine = pltpu.make_async_copy(p_ref, out_ref.at[me], local_sem)
        mine.start()
        sends = []
        for mask in range(1, 8):
            to = (x ^ (mask >> 2), y ^ ((mask >> 1) & 1), c ^ (mask & 1))
            cp = _remote(p_ref, out_ref.at[me], send_sems, recv_sems, mask - 1, to)
            cp.start()
            sends.append(cp)
        for mask in range(1, 8):
            frm = me ^ mask
            _remote(p_ref, out_ref.at[frm], send_sems, recv_sems, mask - 1, (x, y, c)).wait_recv()
        for cp in sends:
            cp.wait_send()
        mine.wait()

    vmem = pl.BlockSpec(memory_space=pltpu.VMEM)
    return pl.pallas_call(
        body, name="gather_small", in_specs=[vmem], out_specs=vmem,
        out_shape=jax.ShapeDtypeStruct((8,) + parts.shape, F32),
        scratch_shapes=[pltpu.SemaphoreType.DMA((7,)), pltpu.SemaphoreType.DMA((7,)),
                        pltpu.SemaphoreType.DMA],
        compiler_params=_params(),
    )(parts)


def _swap_halves(grads):
    def body(*refs):
        g_refs, r_refs = refs[:N_TENSORS], refs[N_TENSORS:2 * N_TENSORS]
        send_sems, recv_sems = refs[2 * N_TENSORS:]
        x, y, c = _place()
        cps = []
        for t in range(N_TENSORS):
            cp = _remote(_half_rows(g_refs[t], 1 - c), r_refs[t], send_sems, recv_sems, t, (x, y, 1 - c))
            cp.start()
            cps.append(cp)
        for cp in cps:
            cp.wait_recv()
        for cp in cps:
            cp.wait_send()

    hbm = pl.BlockSpec(memory_space=pl.ANY)
    return pl.pallas_call(
        body, name="swap_halves", in_specs=[hbm] * N_TENSORS, out_specs=[hbm] * N_TENSORS,
        out_shape=[jax.ShapeDtypeStruct((a.shape[0], a.shape[1] // 2, a.shape[2]), F32) for a in grads],
        scratch_shapes=[pltpu.SemaphoreType.DMA((N_TENSORS,)), pltpu.SemaphoreType.DMA((N_TENSORS,))],
        compiler_params=_params(),
    )(*grads)


def _add_halves(grad, recv, place, tile, name):
    _, rows, cols = grad.shape
    per_half = rows // 2 // tile

    def body(place_ref, g_ref, r_ref, o32_ref, o16_ref):
        v = g_ref[...] + r_ref[...]
        o32_ref[...] = v
        o16_ref[...] = v.astype(BF16)

    blk = lambda f: pl.BlockSpec((None, tile, cols), f)
    out = lambda s, i, p: (s, i, 0)
    return pl.pallas_call(
        body, name=name,
        grid_spec=pltpu.PrefetchScalarGridSpec(
            num_scalar_prefetch=1, grid=(N_SHARDS, per_half),
            in_specs=[blk(lambda s, i, p: (s, p[0] * per_half + i, 0)), blk(out)],
            out_specs=[blk(out), blk(out)]),
        out_shape=[jax.ShapeDtypeStruct(recv.shape, F32), jax.ShapeDtypeStruct(recv.shape, BF16)],
        compiler_params=_params(dimension_semantics=("arbitrary", "arbitrary")),
    )(place, grad, recv)


def _send_partials(parts16):
    n_fl = len(CHIP_FLIPS)

    def body(*refs):
        p_refs, r_refs = refs[:N_TENSORS], refs[N_TENSORS:2 * N_TENSORS]
        send_sems, recv_sems = refs[2 * N_TENSORS:]
        x, y, c = _place()
        cps = []
        for j, (fx, fy) in enumerate(CHIP_FLIPS):
            sj = 2 * (x ^ fx) + (y ^ fy)
            for t in range(N_TENSORS):
                cp = _remote(p_refs[t].at[sj], r_refs[t].at[j], send_sems, recv_sems,
                             N_TENSORS * j + t, (x ^ fx, y ^ fy, c))
                cp.start()
                cps.append(cp)
        for cp in cps:
            cp.wait_recv()
        for cp in cps:
            cp.wait_send()

    hbm = pl.BlockSpec(memory_space=pl.ANY)
    n_sem = N_TENSORS * n_fl
    return pl.pallas_call(
        body, name="send_partials", in_specs=[hbm] * N_TENSORS, out_specs=[hbm] * N_TENSORS,
        out_shape=[jax.ShapeDtypeStruct((n_fl,) + a.shape[1:], BF16) for a in parts16],
        scratch_shapes=[pltpu.SemaphoreType.DMA((n_sem,)), pltpu.SemaphoreType.DMA((n_sem,))],
        compiler_params=_params(),
    )(*parts16)


def _sum_partials(part32, recv16, place, tile, name):
    _, rows, cols = part32.shape

    def body(place_ref, p_ref, r_ref, o_ref):
        acc = p_ref[...]
        for j in range(len(CHIP_FLIPS)):
            acc = acc + r_ref[j].astype(F32)
        o_ref[...] = acc

    return pl.pallas_call(
        body, name=name,
        grid_spec=pltpu.PrefetchScalarGridSpec(
            num_scalar_prefetch=1, grid=(rows // tile,),
            in_specs=[pl.BlockSpec((None, tile, cols), lambda i, p: (p[1], i, 0)),
                      pl.BlockSpec((len(CHIP_FLIPS), tile, cols), lambda i, p: (0, i, 0))],
            out_specs=pl.BlockSpec((tile, cols), lambda i, p: (i, 0))),
        out_shape=jax.ShapeDtypeStruct((rows, cols), F32),
        compiler_params=_params(dimension_semantics=("arbitrary",)),
    )(place, part32, recv16)


def _join_halves(halves):
    def body(*refs):
        h_refs, o_refs = refs[:N_TENSORS], refs[N_TENSORS:2 * N_TENSORS]
        send_sems, recv_sems, local_sems = refs[2 * N_TENSORS:]
        x, y, c = _place()
        cps, loc = [], []
        for t in range(N_TENSORS):
            lc = pltpu.make_async_copy(h_refs[t], _half_rows(o_refs[t], c), local_sems.at[t])
            lc.start()
            loc.append(lc)
            cp = _remote(h_refs[t], _half_rows(o_refs[t], c), send_sems, recv_sems, t, (x, y, 1 - c))
            cp.start()
            cps.append(cp)
        for t in range(N_TENSORS):
            _remote(h_refs[t], _half_rows(o_refs[t], 1 - c), send_sems, recv_sems, t, (x, y, c)).wait_recv()
        for cp in cps:
            cp.wait_send()
        for lc in loc:
            lc.wait()

    hbm = pl.BlockSpec(memory_space=pl.ANY)
    return pl.pallas_call(
        body, name="join_halves", in_specs=[hbm] * N_TENSORS, out_specs=[hbm] * N_TENSORS,
        out_shape=[jax.ShapeDtypeStruct((2 * a.shape[0], a.shape[1]), F32) for a in halves],
        scratch_shapes=[pltpu.SemaphoreType.DMA((N_TENSORS,)), pltpu.SemaphoreType.DMA((N_TENSORS,)),
                        pltpu.SemaphoreType.DMA((N_TENSORS,))],
        compiler_params=_params(),
    )(*halves)


def _adamw_math(w, g, m, v):
    m = ADAM_B1 * m + (1.0 - ADAM_B1) * g
    v = ADAM_B2 * v + (1.0 - ADAM_B2) * (g * g)
    m_hat = m / (1.0 - ADAM_B1 ** ADAM_STEP)
    v_hat = v / (1.0 - ADAM_B2 ** ADAM_STEP)
    delta = -ADAM_LR * (m_hat / (jnp.sqrt(v_hat) + ADAM_EPS) + ADAM_WD * w)
    return delta, m, v


def _adamw(w, g, m, v, tile, name):
    rows, cols = w.shape

    def body(w_ref, g_ref, m_ref, v_ref, d_ref, nm_ref, nv_ref):
        d_ref[...], nm_ref[...], nv_ref[...] = _adamw_math(w_ref[...], g_ref[...], m_ref[...], v_ref[...])

    blk = pl.BlockSpec((tile, cols), lambda i: (i, 0))
    return pl.pallas_call(
        body, name=name, grid=(rows // tile,), in_specs=[blk] * 4, out_specs=[blk] * 3,
        out_shape=[jax.ShapeDtypeStruct(w.shape, F32)] * 3,
        compiler_params=_params(dimension_semantics=("arbitrary",)),
    )(w, g, m, v)


ROW_NORM1, ROW_SCALE, ROW_LB, ROW_REC, ROW_FINAL, ROW_LOSS = 0, 1, 2, 4, 5, 6


def _small_update(gathered, w, m, v):
    def body(p_ref, w_ref, m_ref, v_ref, loss_ref, g_ref, d_ref, nm_ref, nv_ref):
        tot = p_ref[0]
        for d in range(1, 8):
            tot = tot + p_ref[d]
        wv = w_ref[...]
        l0 = wv[ROW_LB:ROW_LB + 1, :]
        l1 = wv[ROW_LB + 1:ROW_LB + 2, :]
        mx = jnp.maximum(l0, l1)
        e0 = jnp.exp(l0 - mx)
        e1 = jnp.exp(l1 - mx)
        lb = e0 / (e0 + e1)
        dl0 = tot[ROW_LB:ROW_LB + 1, :] * lb * (1.0 - lb)
        row = lax.broadcasted_iota(jnp.int32, tot.shape, 0)
        g = jnp.where(row == ROW_LB, dl0, jnp.where(row == ROW_LB + 1, -dl0, tot))
        g = jnp.where(row >= ROW_LOSS, 0.0, g)
        g_ref[...] = g
        d_ref[...], nm_ref[...], nv_ref[...] = _adamw_math(wv, g, m_ref[...], v_ref[...])
        loss_ref[...] = (0.5 / D_MODEL) * jnp.sum(tot[ROW_LOSS:ROW_LOSS + 1, :], axis=-1, keepdims=True)

    return pl.pallas_call(
        body, name="small_update",
        out_shape=[jax.ShapeDtypeStruct((1, 1), F32)] + [jax.ShapeDtypeStruct(w.shape, F32)] * 4,
        compiler_params=_params(),
    )(gathered, w, m, v)


def _small_rows(norm1, scale, lb, rec, final):
    pad = jnp.zeros((2, D_MODEL), F32)
    return jnp.concatenate([norm1, scale, lb, rec, final.reshape(1, D_MODEL), pad], axis=0)


def kernel(x, norm1_g, w_in, pool_w, pool_scale, lb_logits, rec_norm_g, w_out, final_norm_g, loss_target, m_norm1_g, m_w_in, m_pool_w, m_pool_scale, m_lb_logits, m_rec_norm_g, m_w_out, m_final_norm_g, v_norm1_g, v_w_in, v_pool_w, v_pool_scale, v_lb_logits, v_rec_norm_g, v_w_out, v_final_norm_g):
    pw_rows = N_GROUPS * PW_SHARD
    flat_pw = lambda a: a.reshape(pw_rows, PG)
    w_in_g, w_out_g, pw_g = _gather_weights(w_in[0], w_out[0], flat_pw(pool_w))
    pw_full = pw_g.reshape(N_SHARDS, N_GROUPS, PW_SHARD, PG).transpose(1, 0, 2, 3).reshape(N_GROUPS, PG, PG)
    loc = _local_step(x[0], loss_target[0], norm1_g, w_in_g, pw_full, pool_scale, lb_logits, rec_norm_g,
                      w_out_g.reshape(2 * D_MODEL, D_MODEL), final_norm_g.reshape(1, D_MODEL))

    zero = jnp.zeros((1, D_MODEL), F32)
    parts = jnp.concatenate([loc["part_x"][0:1], loc["dscale"], loc["part_rec"][1:2], zero,
                             loc["part_rec"][0:1], loc["part_out"][0:1], loc["part_out"][1:2], zero], axis=0)
    small_w = _small_rows(norm1_g, pool_scale, lb_logits, rec_norm_g, final_norm_g)
    small_m = _small_rows(m_norm1_g, m_pool_scale, m_lb_logits, m_rec_norm_g, m_final_norm_g)
    small_v = _small_rows(v_norm1_g, v_pool_scale, v_lb_logits, v_rec_norm_g, v_final_norm_g)
    loss, *small = _small_update(_gather_small(parts), small_w, small_m, small_v)

    xi, yi, ci = _place()
    place = jnp.stack([ci, 2 * xi + yi]).astype(jnp.int32)
    gpw = loc["gpw"].reshape(N_GROUPS, N_SHARDS, PW_SHARD, PG).transpose(1, 0, 2, 3).reshape(N_SHARDS, pw_rows, PG)
    grads = (loc["gw_in"], loc["gw_out"].reshape(N_SHARDS, W_OUT_SHARD, D_MODEL), gpw)
    tiles = (256, 256, 128)
    names = ("w_in", "w_out", "pool_w")
    recv = _swap_halves(grads)
    part = [_add_halves(g, r, place, t, "add_halves_" + n) for g, r, t, n in zip(grads, recv, tiles, names)]
    recv16 = _send_partials([p[1] for p in part])
    halves = [_sum_partials(p[0], r, place, t, "sum_partials_" + n)
              for p, r, t, n in zip(part, recv16, tiles, names)]
    g_in, g_out, g_pw = _join_halves(halves)
    big = []
    for w, g, m, v, name in ((w_in[0], g_in, m_w_in[0], v_w_in[0], "adamw_w_in"),
                             (w_out[0], g_out, m_w_out[0], v_w_out[0], "adamw_w_out"),
                             (flat_pw(pool_w), g_pw, flat_pw(m_pool_w), flat_pw(v_pool_w), "adamw_pool_w")):
        big.append((g,) + tuple(_adamw(w, g, m, v, 256, name)))

    def leaves(k):
        s = small[k]
        return (s[ROW_NORM1:ROW_NORM1 + 1], big[0][k][None], big[2][k].reshape(pool_w.shape),
                s[ROW_SCALE:ROW_SCALE + 1], s[ROW_LB:ROW_LB + 2], s[ROW_REC:ROW_REC + 1],
                big[1][k][None], s[ROW_FINAL])

    return (loss.reshape(()), loc["dx"][None], *leaves(0), *leaves(1), *leaves(2), *leaves(3))
```

```python
import functools

import numpy as np
import jax
import jax.numpy as jnp
from jax import lax
from jax.experimental import pallas as pl
from jax.experimental.pallas import tpu as pltpu

F32 = jnp.float32
BF16 = jnp.bfloat16

SEQ = 2048
D_MODEL = 1024
D_PROJ = 6144
N_SEC = 6
N_GROUPS = 4
PG = 256
N_HEADS = 8
HEAD = 128
CHUNK = 64
N_LEVELS = 6
N_SHARDS = 4
W_IN_SHARD = D_PROJ // N_SHARDS
W_OUT_SHARD = 2048 // N_SHARDS
PW_SHARD = PG // N_SHARDS
COL_BLK = 512
EPS = 1e-6

ADAM_LR = 0.001
ADAM_B1 = 0.9
ADAM_B2 = 0.999
ADAM_EPS = 1e-08
ADAM_WD = 0.01
ADAM_STEP = 10

V7X_VMEM_LIMIT = 56 * 1024 * 1024
MESH = pl.DeviceIdType.MESH


def _params(**kw):
    return pltpu.CompilerParams(vmem_limit_bytes=V7X_VMEM_LIMIT, **kw)


def _sig(x):
    return 1.0 / (1.0 + jnp.exp(-x))


def _dot(a, b):
    return jnp.dot(a, b, preferred_element_type=F32)


def _dot_nt(a, b):
    return lax.dot_general(a, b, (((1,), (1,)), ((), ())), preferred_element_type=F32)


def _dot_tn(a, b):
    return lax.dot_general(a, b, (((0,), (0,)), ((), ())), preferred_element_type=F32)


def _split3(a):
    p1 = a.astype(BF16)
    r1 = a - p1.astype(F32)
    p2 = r1.astype(BF16)
    p3 = (r1 - p2.astype(F32)).astype(BF16)
    return jnp.concatenate([p1, p2, p3], axis=-1)


def _dot3(w01, a):
    n = a.shape[-1]
    r = _dot(w01, _split3(a))
    return r[:, :n] + r[:, n:2 * n] + r[:, 2 * n:]


def _chunk_constants():
    c = CHUNK
    w = np.zeros((2 + N_LEVELS, c, c), np.float32)
    j = np.arange(c)
    for t in range(c):
        w[0, t] = j <= t
        w[1, t] = j > t
        for l in range(N_LEVELS):
            m = 1 << l
            pos = t % (2 * m)
            mid = t - pos + m - 1
            w[2 + l, t] = ((j > mid) & (j <= t)) if pos >= m else ((j > t) & (j <= mid))
    tt, ss = np.meshgrid(j, j, indexing="ij")
    x = tt ^ ss
    hb = np.full((c, c), -1, np.int32)
    for l in range(N_LEVELS):
        hb[x >= (1 << l)] = l
    lvl = np.where(tt > ss, hb, -1).astype(np.int32)
    tri_t = (ss >= tt).astype(np.float32)
    return w.reshape(-1, c), lvl, np.ascontiguousarray(lvl.T), tri_t


def _in_proj(x, g1, w_in_g):
    n_col = D_PROJ // COL_BLK
    per_shard = W_IN_SHARD // COL_BLK
    rows = 256

    def body(x_ref, g_ref, w_ref, proj_ref, h_ref):
        @pl.when(pl.program_id(0) == 0)
        def _():
            def norm(i, _):
                r = pl.ds(pl.multiple_of(i * rows, rows), rows)
                xv = x_ref[r, :]
                inv = lax.rsqrt(jnp.mean(xv * xv, axis=-1, keepdims=True) + EPS)
                h_ref[r, :] = (xv * inv * g_ref[...]).astype(BF16)
                return 0
            lax.fori_loop(0, SEQ // rows, norm, 0)

        def mm(i, _):
            r = pl.ds(pl.multiple_of(i * rows, rows), rows)
            proj_ref[r, :] = _dot(h_ref[r, :], w_ref[...])
            return 0
        lax.fori_loop(0, SEQ // rows, mm, 0)

    return pl.pallas_call(
        body, name="in_proj", grid=(n_col,),
        in_specs=[pl.BlockSpec((SEQ, D_MODEL), lambda j: (0, 0)),
                  pl.BlockSpec((1, D_MODEL), lambda j: (0, 0)),
                  pl.BlockSpec((None, D_MODEL, COL_BLK), lambda j: (j // per_shard, 0, j % per_shard))],
        out_specs=[pl.BlockSpec((SEQ, COL_BLK), lambda j: (0, j)),
                   pl.BlockSpec((SEQ, D_MODEL), lambda j: (0, 0))],
        out_shape=[jax.ShapeDtypeStruct((SEQ, D_PROJ), F32),
                   jax.ShapeDtypeStruct((SEQ, D_MODEL), BF16)],
        compiler_params=_params(dimension_semantics=("arbitrary",)),
    )(x, g1, w_in_g)


POOL_ROWS = 256
POOL_HALO = 16


def _window_sums(ext, g, shift_of):
    s = ext
    for k in range(N_GROUPS):
        s = jnp.where(k <= g, s + pltpu.roll(s, shift_of(k), 0), s)
    return s


def _pool_diff(u_ref, i, g):
    n = POOL_ROWS + POOL_HALO
    r0 = i * POOL_ROWS
    cur = u_ref[pl.ds(pl.multiple_of(r0, POOL_ROWS), POOL_ROWS), :]
    before = u_ref[pl.ds(pl.multiple_of(jnp.maximum(r0 - POOL_HALO, 0), 8), POOL_HALO), :]
    before = jnp.where(i > 0, before, 0.0)
    ext = jnp.concatenate([before, cur], axis=0)
    s = _window_sums(ext, g, lambda k: 1 << k)[POOL_HALO:, :]
    t = r0 + lax.broadcasted_iota(jnp.int32, (POOL_ROWS, 1), 0)
    width = (2 << g).astype(F32)
    inv_count = 1.0 / jnp.minimum((t + 1).astype(F32), width)
    return s * inv_count - cur, inv_count


def _pool_fwd(proj, pw_g, pool_scale):
    def body(u_ref, gate_ref, pw_ref, sc_ref, y_ref):
        g = pl.program_id(0)

        def step(i, _):
            r = pl.ds(pl.multiple_of(i * POOL_ROWS, POOL_ROWS), POOL_ROWS)
            d, _ = _pool_diff(u_ref, i, g)
            mixed = _dot(d.astype(BF16), pw_ref[...])
            gate = gate_ref[r, :]
            y_ref[r, :] = (mixed * sc_ref[...] * (gate * _sig(gate))).astype(BF16)
            return 0
        lax.fori_loop(0, SEQ // POOL_ROWS, step, 0)

    return pl.pallas_call(
        body, name="pool_fwd", grid=(N_GROUPS,),
        in_specs=[pl.BlockSpec((SEQ, PG), lambda g: (0, g)),
                  pl.BlockSpec((SEQ, PG), lambda g: (0, N_GROUPS + g)),
                  pl.BlockSpec((None, PG, PG), lambda g: (g, 0, 0)),
                  pl.BlockSpec((1, PG), lambda g: (0, g))],
        out_specs=pl.BlockSpec((SEQ, PG), lambda g: (0, g)),
        out_shape=jax.ShapeDtypeStruct((SEQ, D_MODEL), BF16),
        compiler_params=_params(dimension_semantics=("arbitrary",)),
    )(proj, proj, pw_g, pool_scale)


REC_ROWS = 512
REC_CHUNKS = REC_ROWS // CHUNK
N_REC_BLK = SEQ // REC_ROWS
SEC_BLK = D_MODEL // HEAD


def _lower_bound(lb_ref):
    l0 = lb_ref[0:1, :]
    l1 = lb_ref[1:2, :]
    mx = jnp.maximum(l0, l1)
    e0 = jnp.exp(l0 - mx)
    e1 = jnp.exp(l1 - mx)
    return e0 / (e0 + e1)


def _gates(q, fl, lb):
    qs = q * _sig(q)
    sf = _sig(fl)
    f = lb + (1.0 - lb) * sf
    return qs, sf, f, 1.0 - f, jnp.log(f)


def _level_factors(cs, qs, k):
    t = lax.broadcasted_iota(jnp.int32, (CHUNK, HEAD), 0)
    out = []
    for l in range(N_LEVELS):
        e = jnp.exp(cs[(2 + l) * CHUNK:(3 + l) * CHUNK, :])
        up = ((t >> l) & 1) == 1
        x = jnp.where(up, qs, k) * e
        hi = x.astype(BF16)
        out.append((hi, (x - hi.astype(F32)).astype(BF16), e, up))
    return out


def _rec_fwd(proj, lb_logits, rec_g, wseg, lvl):
    def body(q_ref, f_ref, i_ref, rg_ref, lb_ref, g_ref, w_ref, lvl_ref, y_ref, o_ref, stp_ref, st_ref):
        @pl.when(pl.program_id(1) == 0)
        def _():
            st_ref[...] = jnp.zeros_like(st_ref)
        lb = _lower_bound(lb_ref)

        def chunk(c, _):
            r = pl.ds(pl.multiple_of(c * CHUNK, CHUNK), CHUNK)
            v = i_ref[r, :]
            rg = rg_ref[r, :]
            qs, _, _, k, g = _gates(q_ref[r, :], f_ref[r, :], lb)
            cs = _dot3(w_ref[...], g)
            big_g = cs[0:CHUNK, :]
            g_rev = cs[CHUNK:2 * CHUNK, :]
            a = jnp.zeros((CHUNK, CHUNK), F32)
            lv = lvl_ref[...]
            for l, (xl, _, _, _) in enumerate(_level_factors(cs, qs, k)):
                a = a + jnp.where(lv == l, _dot_nt(xl, xl), 0.0)
            st = st_ref[...]
            stp_ref[c] = st
            vb = v.astype(BF16)
            diag = jnp.sum(qs * k, axis=-1, keepdims=True)
            o = (_dot(a.astype(BF16), vb) + diag * v
                 + _dot_nt((qs * jnp.exp(big_g)).astype(BF16), st.astype(BF16)))
            kdec = (k * jnp.exp(g_rev)).astype(BF16)
            st_ref[...] = st * jnp.exp(big_g[CHUNK - 1:CHUNK, :]) + _dot_tn(vb, kdec)
            o_ref[r, :] = o
            inv = lax.rsqrt(jnp.mean(o * o, axis=-1, keepdims=True) + EPS)
            y_ref[r, :] = (o * inv * g_ref[...] * (rg * _sig(rg))).astype(BF16)
            return 0
        lax.fori_loop(0, REC_CHUNKS, chunk, 0)

    sec = lambda n: pl.BlockSpec((REC_ROWS, HEAD), lambda h, b: (b, n * SEC_BLK + h))
    vec = lambda rows: pl.BlockSpec((rows, HEAD), lambda h, b: (0, h))
    full = lambda a: pl.BlockSpec(a.shape, lambda h, b: (0,) * a.ndim)
    return pl.pallas_call(
        body, name="rec_fwd", grid=(N_HEADS, N_REC_BLK),
        in_specs=[sec(2), sec(3), sec(4), sec(5), vec(2), vec(1), full(wseg), full(lvl)],
        out_specs=[pl.BlockSpec((REC_ROWS, HEAD), lambda h, b: (b, h)),
                   pl.BlockSpec((REC_ROWS, HEAD), lambda h, b: (b, h)),
                   pl.BlockSpec((None, REC_CHUNKS, HEAD, HEAD), lambda h, b: (h, b, 0, 0))],
        out_shape=[jax.ShapeDtypeStruct((SEQ, D_MODEL), BF16),
                   jax.ShapeDtypeStruct((SEQ, D_MODEL), F32),
                   jax.ShapeDtypeStruct((N_HEADS, SEQ // CHUNK, HEAD, HEAD), F32)],
        scratch_shapes=[pltpu.VMEM((HEAD, HEAD), F32)],
        compiler_params=_params(dimension_semantics=("arbitrary", "arbitrary")),
    )(proj, proj, proj, proj, lb_logits, rec_g, wseg, lvl)


OUT_ROWS = 256


def _out_proj_loss(y_pool, y_rec, w_out_g, x, target, gf):
    def body(yp_ref, yr_ref, w_ref, x_ref, t_ref, gf_ref, dout_ref, doutb_ref, part_ref):
        @pl.when(pl.program_id(0) == 0)
        def _():
            part_ref[...] = jnp.zeros_like(part_ref)
        out = (x_ref[...] + _dot(yp_ref[...], w_ref[0:D_MODEL, :])
               + _dot(yr_ref[...], w_ref[D_MODEL:2 * D_MODEL, :]))
        inv = lax.rsqrt(jnp.mean(out * out, axis=-1, keepdims=True) + EPS)
        gf_v = gf_ref[...]
        diff = out * inv * gf_v - t_ref[...]
        dyf = diff * (1.0 / D_MODEL)
        a = dyf * gf_v
        dout = inv * a - out * (inv * inv * inv) * jnp.mean(a * out, axis=-1, keepdims=True)
        dout_ref[...] = dout
        doutb_ref[...] = dout.astype(BF16)
        part_ref[0:1, :] += jnp.sum(dyf * out * inv, axis=0, keepdims=True)
        part_ref[1:2, :] += jnp.sum(diff * diff, axis=0, keepdims=True)

    row = lambda n: pl.BlockSpec((OUT_ROWS, n), lambda i: (i, 0))
    return pl.pallas_call(
        body, name="out_proj_loss", grid=(SEQ // OUT_ROWS,),
        in_specs=[row(D_MODEL), row(D_MODEL), pl.BlockSpec((2 * D_MODEL, D_MODEL), lambda i: (0, 0)),
                  row(D_MODEL), row(D_MODEL), pl.BlockSpec((1, D_MODEL), lambda i: (0, 0))],
        out_specs=[row(D_MODEL), row(D_MODEL), pl.BlockSpec((8, D_MODEL), lambda i: (0, 0))],
        out_shape=[jax.ShapeDtypeStruct((SEQ, D_MODEL), F32),
                   jax.ShapeDtypeStruct((SEQ, D_MODEL), BF16),
                   jax.ShapeDtypeStruct((8, D_MODEL), F32)],
        compiler_params=_params(dimension_semantics=("arbitrary",)),
    )(y_pool, y_rec, w_out_g, x, target, gf)


def _grad_w_out(y_pool, y_rec, dout_b):
    blk = 256
    per = D_MODEL // blk

    def body(yp_ref, yr_ref, d_ref, o_ref):
        j = pl.program_id(0)

        @pl.when(j < per)
        def _():
            o_ref[...] = _dot_tn(yp_ref[...], d_ref[...])

        @pl.when(j >= per)
        def _():
            o_ref[...] = _dot_tn(yr_ref[...], d_ref[...])

    return pl.pallas_call(
        body, name="grad_w_out", grid=(2 * per,),
        in_specs=[pl.BlockSpec((SEQ, blk), lambda j: (0, jnp.minimum(j, per - 1))),
                  pl.BlockSpec((SEQ, blk), lambda j: (0, jnp.maximum(j - per, 0))),
                  pl.BlockSpec((SEQ, D_MODEL), lambda j: (0, 0))],
        out_specs=pl.BlockSpec((blk, D_MODEL), lambda j: (j, 0)),
        out_shape=jax.ShapeDtypeStruct((2 * D_MODEL, D_MODEL), F32),
        compiler_params=_params(dimension_semantics=("arbitrary",)),
    )(y_pool, y_rec, dout_b)


def _pool_bwd(proj, dout_b, w_out_g, pw_g, pool_scale):
    n = POOL_ROWS + POOL_HALO

    def body(u_ref, gate_ref, d_ref, wo_ref, pw_ref, sc_ref,
             du_ref, dgate_ref, dpw_ref, dsc_ref, dd_ref, ddw_ref):
        g = pl.program_id(0)
        dpw_ref[...] = jnp.zeros_like(dpw_ref)
        dsc_ref[...] = jnp.zeros_like(dsc_ref)

        def first(i, _):
            r = pl.ds(pl.multiple_of(i * POOL_ROWS, POOL_ROWS), POOL_ROWS)
            d, inv_count = _pool_diff(u_ref, i, g)
            db = d.astype(BF16)
            mixed = _dot(db, pw_ref[...])
            gate = gate_ref[r, :]
            sg = _sig(gate)
            silu = gate * sg
            dy = _dot_nt(d_ref[r, :], wo_ref[...])
            sc = sc_ref[...]
            dmixed = dy * silu * sc
            dgate_ref[r, :] = (dy * mixed * sc * (sg * (1.0 + gate * (1.0 - sg)))).astype(BF16)
            dsc_ref[...] += jnp.sum(dy * silu * mixed, axis=0, keepdims=True)
            dmb = dmixed.astype(BF16)
            dpw_ref[...] += _dot_tn(db, dmb)
            dd = _dot_nt(dmb, pw_ref[...])
            dd_ref[r, :] = dd
            ddw_ref[r, :] = dd * inv_count
            return 0
        lax.fori_loop(0, SEQ // POOL_ROWS, first, 0)

        def second(i, _):
            r0 = i * POOL_ROWS
            r = pl.ds(pl.multiple_of(r0, POOL_ROWS), POOL_ROWS)
            last = i == SEQ // POOL_ROWS - 1
            after = ddw_ref[pl.ds(pl.multiple_of(jnp.minimum(r0 + POOL_ROWS, SEQ - POOL_HALO), 8), POOL_HALO), :]
            after = jnp.where(last, 0.0, after)
            ext = jnp.concatenate([ddw_ref[r, :], after], axis=0)
            s = _window_sums(ext, g, lambda k: n - (1 << k))[:POOL_ROWS, :]
            du_ref[r, :] = (s - dd_ref[r, :]).astype(BF16)
            return 0
        lax.fori_loop(0, SEQ // POOL_ROWS, second, 0)

    return pl.pallas_call(
        body, name="pool_bwd", grid=(N_GROUPS,),
        in_specs=[pl.BlockSpec((SEQ, PG), lambda g: (0, g)),
                  pl.BlockSpec((SEQ, PG), lambda g: (0, N_GROUPS + g)),
                  pl.BlockSpec((SEQ, D_MODEL), lambda g: (0, 0)),
                  pl.BlockSpec((PG, D_MODEL), lambda g: (g, 0)),
                  pl.BlockSpec((None, PG, PG), lambda g: (g, 0, 0)),
                  pl.BlockSpec((1, PG), lambda g: (0, g))],
        out_specs=[pl.BlockSpec((SEQ, PG), lambda g: (0, g)),
                   pl.BlockSpec((SEQ, PG), lambda g: (0, g)),
                   pl.BlockSpec((None, PG, PG), lambda g: (g, 0, 0)),
                   pl.BlockSpec((1, PG), lambda g: (0, g))],
        out_shape=[jax.ShapeDtypeStruct((SEQ, D_MODEL), BF16),
                   jax.ShapeDtypeStruct((SEQ, D_MODEL), BF16),
                   jax.ShapeDtypeStruct((N_GROUPS, PG, PG), F32),
                   jax.ShapeDtypeStruct((1, D_MODEL), F32)],
        scratch_shapes=[pltpu.VMEM((SEQ, PG), F32), pltpu.VMEM((SEQ, PG), F32)],
        compiler_params=_params(dimension_semantics=("arbitrary",)),
    )(proj, proj, dout_b, w_out_g, pw_g, pool_scale)


def _rec_bwd(proj, o_raw, st_prev, dout_b, w_out_g, lb_logits, rec_g, wseg, lvl, lvl_t, tri_t):
    def body(q_ref, f_ref, i_ref, rg_ref, o_ref, stp_ref, d_ref, wo_ref, lb_ref, g_ref,
             w_ref, lvl_ref, lvlt_ref, tri_ref,
             dq_ref, df_ref, di_ref, drg_ref, part_ref, dst_ref):
        @pl.when(pl.program_id(1) == 0)
        def _():
            dst_ref[...] = jnp.zeros_like(dst_ref)
            part_ref[...] = jnp.zeros_like(part_ref)
        lb = _lower_bound(lb_ref)
        grec = g_ref[...]

        def chunk(cc, _):
            c = REC_CHUNKS - 1 - cc
            r = pl.ds(pl.multiple_of(c * CHUNK, CHUNK), CHUNK)
            q = q_ref[r, :]
            v = i_ref[r, :]
            rg = rg_ref[r, :]
            o = o_ref[r, :]
            dy = _dot_nt(d_ref[r, :], wo_ref[...])
            sg = _sig(rg)
            silu = rg * sg
            inv = lax.rsqrt(jnp.mean(o * o, axis=-1, keepdims=True) + EPS)
            recn = o * inv
            drg_ref[r, :] = (dy * recn * grec * (sg * (1.0 + rg * (1.0 - sg)))).astype(BF16)
            part_ref[0:1, :] += jnp.sum(dy * silu * recn, axis=0, keepdims=True)
            drecn = dy * silu * grec
            do = inv * drecn - o * (inv * inv * inv) * jnp.mean(drecn * o, axis=-1, keepdims=True)
            qs, sf, f, k, g = _gates(q, f_ref[r, :], lb)
            cs = _dot3(w_ref[...], g)
            e_g = jnp.exp(cs[0:CHUNK, :])
            e_rev = jnp.exp(cs[CHUNK:2 * CHUNK, :])
            e_last = e_g[CHUNK - 1:CHUNK, :]
            levels = _level_factors(cs, qs, k)
            lv = lvl_ref[...]
            lvt = lvlt_ref[...]
            a_t = jnp.zeros((CHUNK, CHUNK), F32)
            for l, (xl, _, _, _) in enumerate(levels):
                a_t = a_t + jnp.where(lvt == l, _dot_nt(xl, xl), 0.0)
            dob = do.astype(BF16)
            vb = v.astype(BF16)
            d_a = _dot_nt(dob, vb)
            d_at = _dot_nt(vb, dob)
            stp = stp_ref[c]
            dst = dst_ref[...]
            dstb = dst.astype(BF16)
            q_g = qs * e_g
            kdec = k * e_rev
            diag = jnp.sum(qs * k, axis=-1, keepdims=True)
            dv = _dot(a_t.astype(BF16), dob) + diag * do + _dot_nt(kdec.astype(BF16), dstb)
            dq_g = _dot(dob, stp.astype(BF16))
            dkdec = _dot(vb, dstb)
            de_last = jnp.sum(stp * dst, axis=0, keepdims=True)
            dst_ref[...] = dst * e_last + _dot_tn(dob, q_g.astype(BF16))
            dqs_i = jnp.zeros((CHUNK, HEAD), F32)
            dk_i = jnp.zeros((CHUNK, HEAD), F32)
            for l, (xl, xlo, e, up) in enumerate(levels):
                z = jnp.where(lv == l, d_a, jnp.where(lvt == l, d_at, 0.0))
                tmp = _dot(z.astype(BF16), jnp.concatenate([xl, xlo], axis=-1))
                tmp = (tmp[:, :HEAD] + tmp[:, HEAD:]) * e
                dqs_i = dqs_i + jnp.where(up, tmp, 0.0)
                dk_i = dk_i + jnp.where(up, 0.0, tmp)
            ddiag = jnp.sum(do * v, axis=-1, keepdims=True)
            dqs = dqs_i + ddiag * k + dq_g * e_g
            dk = dk_i + ddiag * qs + dkdec * e_rev
            dg_rev = dkdec * kdec
            dg_last = jnp.sum(dg_rev, axis=0, keepdims=True) + de_last * e_last
            dbig_g = qs * dqs_i - k * dk_i + dq_g * q_g - dg_rev
            dg = _dot3(tri_ref[...], dbig_g) + dg_last
            df = dg / f - dk
            df_ref[r, :] = (df * (1.0 - lb) * sf * (1.0 - sf)).astype(BF16)
            part_ref[1:2, :] += jnp.sum(df * (1.0 - sf), axis=0, keepdims=True)
            sq = _sig(q)
            dq_ref[r, :] = (dqs * (sq * (1.0 + q * (1.0 - sq)))).astype(BF16)
            di_ref[r, :] = dv.astype(BF16)
            return 0
        lax.fori_loop(0, REC_CHUNKS, chunk, 0)

    rev = lambda b: N_REC_BLK - 1 - b
    sec = lambda n: pl.BlockSpec((REC_ROWS, HEAD), lambda h, b: (rev(b), n * SEC_BLK + h))
    col = pl.BlockSpec((REC_ROWS, HEAD), lambda h, b: (rev(b), h))
    vec = lambda rows: pl.BlockSpec((rows, HEAD), lambda h, b: (0, h))
    full = lambda a: pl.BlockSpec(a.shape, lambda h, b: (0,) * a.ndim)
    return pl.pallas_call(
        body, name="rec_bwd", grid=(N_HEADS, N_REC_BLK),
        in_specs=[sec(2), sec(3), sec(4), sec(5), col,
                  pl.BlockSpec((None, REC_CHUNKS, HEAD, HEAD), lambda h, b: (h, rev(b), 0, 0)),
                  pl.BlockSpec((REC_ROWS, D_MODEL), lambda h, b: (rev(b), 0)),
                  pl.BlockSpec((HEAD, D_MODEL), lambda h, b: (SEC_BLK + h, 0)),
                  vec(2), vec(1), full(wseg), full(lvl), full(lvl_t), full(tri_t)],
        out_specs=[col, col, col, col, vec(8)],
        out_shape=[jax.ShapeDtypeStruct((SEQ, D_MODEL), BF16)] * 4
                  + [jax.ShapeDtypeStruct((8, D_MODEL), F32)],
        scratch_shapes=[pltpu.VMEM((HEAD, HEAD), F32)],
        compiler_params=_params(dimension_semantics=("arbitrary", "arbitrary")),
    )(proj, proj, proj, proj, o_raw, st_prev, dout_b, w_out_g, lb_logits, rec_g, wseg, lvl, lvl_t, tri_t)


def _grad_x(dproj, w_in_g, x, g1, dout):
    rows = 256
    per_shard = W_IN_SHARD // COL_BLK
    per_sec = D_MODEL // COL_BLK

    def body(*refs):
        dp_refs = refs[:N_SEC]
        w_ref, x_ref, g_ref, dout_ref, dx_ref, part_ref = refs[N_SEC:]

        @pl.when(pl.program_id(0) == 0)
        def _():
            part_ref[...] = jnp.zeros_like(part_ref)
        dh = jnp.zeros((rows, D_MODEL), F32)
        for n in range(N_SEC):
            for p in range(per_sec):
                j = n * per_sec + p
                w = w_ref[j // per_shard, :, (j % per_shard) * COL_BLK:(j % per_shard + 1) * COL_BLK]
                dh = dh + _dot_nt(dp_refs[n][:, p * COL_BLK:(p + 1) * COL_BLK], w)
        xv = x_ref[...]
        inv = lax.rsqrt(jnp.mean(xv * xv, axis=-1, keepdims=True) + EPS)
        a = dh * g_ref[...]
        dx_ref[...] = (dout_ref[...] + inv * a
                       - xv * (inv * inv * inv) * jnp.mean(a * xv, axis=-1, keepdims=True))
        part_ref[0:1, :] += jnp.sum(dh * xv * inv, axis=0, keepdims=True)

    row = lambda: pl.BlockSpec((rows, D_MODEL), lambda i: (i, 0))
    return pl.pallas_call(
        body, name="grad_x", grid=(SEQ // rows,),
        in_specs=[row() for _ in range(N_SEC)]
                 + [pl.BlockSpec((N_SHARDS, D_MODEL, W_IN_SHARD), lambda i: (0, 0, 0)),
                    row(), pl.BlockSpec((1, D_MODEL), lambda i: (0, 0)), row()],
        out_specs=[row(), pl.BlockSpec((8, D_MODEL), lambda i: (0, 0))],
        out_shape=[jax.ShapeDtypeStruct((SEQ, D_MODEL), F32),
                   jax.ShapeDtypeStruct((8, D_MODEL), F32)],
        compiler_params=_params(dimension_semantics=("arbitrary",)),
    )(*dproj, w_in_g, x, g1, dout)


def _grad_w_in(h, dproj):
    per_shard = W_IN_SHARD // COL_BLK
    per_sec = D_MODEL // COL_BLK
    n_col = D_PROJ // COL_BLK

    def body(h_ref, *refs):
        dp_refs = refs[:N_SEC]
        o_ref = refs[N_SEC]
        j = pl.program_id(0)
        for n in range(N_SEC):
            @pl.when(j // per_sec == n)
            def _(n=n):
                o_ref[...] = _dot_tn(h_ref[...], dp_refs[n][...])

    def dp_spec(n):
        return pl.BlockSpec((SEQ, COL_BLK), lambda j: (0, jnp.where(j // per_sec == n, j % per_sec, 0)))

    return pl.pallas_call(
        body, name="grad_w_in", grid=(n_col,),
        in_specs=[pl.BlockSpec((SEQ, D_MODEL), lambda j: (0, 0))] + [dp_spec(n) for n in range(N_SEC)],
        out_specs=pl.BlockSpec((None, D_MODEL, COL_BLK), lambda j: (j // per_shard, 0, j % per_shard)),
        out_shape=jax.ShapeDtypeStruct((N_SHARDS, D_MODEL, W_IN_SHARD), F32),
        compiler_params=_params(dimension_semantics=("arbitrary",)),
    )(h, *dproj)


def _local_step(x, target, g1, w_in_g, pw_g, pool_scale, lb_logits, rec_g, w_out_g, gf):
    wseg, lvl, lvl_t, tri_t = _chunk_constants()
    wseg = jnp.asarray(wseg, BF16)
    tri_t = jnp.asarray(tri_t, BF16)
    lvl = jnp.asarray(lvl)
    lvl_t = jnp.asarray(lvl_t)
    proj, h = _in_proj(x, g1, w_in_g)
    y_pool = _pool_fwd(proj, pw_g, pool_scale)
    y_rec, o_raw, st_prev = _rec_fwd(proj, lb_logits, rec_g, wseg, lvl)
    dout, dout_b, part_out = _out_proj_loss(y_pool, y_rec, w_out_g, x, target, gf)
    gw_out = _grad_w_out(y_pool, y_rec, dout_b)
    du, dgate, gpw, dscale = _pool_bwd(proj, dout_b, w_out_g, pw_g, pool_scale)
    dq, df, di, drg, part_rec = _rec_bwd(proj, o_raw, st_prev, dout_b, w_out_g, lb_logits, rec_g,
                                         wseg, lvl, lvl_t, tri_t)
    dproj = (du, dgate, dq, df, di, drg)
    dx, part_x = _grad_x(dproj, w_in_g, x, g1, dout)
    gw_in = _grad_w_in(h, dproj)
    return dict(dx=dx, gw_in=gw_in, gw_out=gw_out, gpw=gpw, dscale=dscale,
                part_out=part_out, part_rec=part_rec, part_x=part_x)


CHIP_FLIPS = ((1, 0), (0, 1), (1, 1))
N_TENSORS = 3


def _place():
    return lax.axis_index("x"), lax.axis_index("y"), lax.axis_index("c")


def _remote(src, dst, send_sems, recv_sems, k, to):
    return pltpu.make_async_remote_copy(src_ref=src, dst_ref=dst, send_sem=send_sems.at[k],
                                        recv_sem=recv_sems.at[k], device_id=to, device_id_type=MESH)


def _half_rows(ref, c):
    half = ref.shape[-2] // 2
    rows = pl.ds(pl.multiple_of(c * half, half), half)
    return ref.at[:, rows, :] if len(ref.shape) == 3 else ref.at[rows, :]


def _gather_weights(w_in, w_out, pool_w):
    shards = (w_in, w_out, pool_w)
    n_fl = len(CHIP_FLIPS)

    def body(wi_ref, wo_ref, pw_ref, wi_g, wo_g, pw_g, send_sems, recv_sems):
        x, y, c = _place()
        s = 2 * x + y
        for src, dst in ((wi_ref, wi_g), (wo_ref, wo_g), (pw_ref, pw_g)):
            step = 64
            def cast(i, _, src=src, dst=dst):
                r = pl.ds(pl.multiple_of(i * step, step), step)
                dst.at[s][r, :] = src[r, :].astype(BF16)
                return 0
            lax.fori_loop(0, src.shape[0] // step, cast, 0)
        outs = (wi_g, wo_g, pw_g)

        def piece(t, shard, half):
            return _half_rows(outs[t].at[shard], half)

        first, passed = [], []
        for j, (fx, fy) in enumerate(CHIP_FLIPS):
            for t in range(N_TENSORS):
                cp = _remote(piece(t, s, c), piece(t, s, c), send_sems, recv_sems,
                             N_TENSORS * j + t, (x ^ fx, y ^ fy, c))
                cp.start()
                first.append(cp)
        for j, (fx, fy) in enumerate(CHIP_FLIPS):
            sj = 2 * (x ^ fx) + (y ^ fy)
            for t in range(N_TENSORS):
                k = N_TENSORS * j + t
                _remote(piece(t, sj, c), piece(t, sj, c), send_sems, recv_sems, k, (x, y, c)).wait_recv()
                cp = _remote(piece(t, sj, c), piece(t, sj, c), send_sems, recv_sems,
                             N_TENSORS * n_fl + k, (x, y, 1 - c))
                cp.start()
                passed.append(cp)
        for j, (fx, fy) in enumerate(CHIP_FLIPS):
            sj = 2 * (x ^ fx) + (y ^ fy)
            for t in range(N_TENSORS):
                k = N_TENSORS * n_fl + N_TENSORS * j + t
                _remote(piece(t, sj, 1 - c), piece(t, sj, 1 - c), send_sems, recv_sems, k, (x, y, c)).wait_recv()
        for cp in first + passed:
            cp.wait_send()

    n_sem = 2 * N_TENSORS * n_fl
    vmem = pl.BlockSpec(memory_space=pltpu.VMEM)
    return pl.pallas_call(
        body, name="gather_weights",
        in_specs=[vmem] * 3, out_specs=[vmem] * 3,
        out_shape=[jax.ShapeDtypeStruct((N_SHARDS,) + a.shape, BF16) for a in shards],
        scratch_shapes=[pltpu.SemaphoreType.DMA((n_sem,)), pltpu.SemaphoreType.DMA((n_sem,))],
        compiler_params=_params(),
    )(*shards)


def _gather_small(parts):
    def body(p_ref, out_ref, send_sems, recv_sems, local_sem):
        x, y, c = _place()
        me = 4 * x + 2 * y + c
        mine = pltpu.make_async_copy(p_ref, out_ref.at[me], local_sem)
        mine.start()
        sends = []
        for mask in range(1, 8):
            to = (x ^ (mask >> 2), y ^ ((mask >> 1) & 1), c ^ (mask & 1))
            cp = _remote(p_ref, out_ref.at[me], send_sems, recv_sems, mask - 1, to)
            cp.start()
            sends.append(cp)
        for mask in range(1, 8):
            frm = me ^ mask
            _remote(p_ref, out_ref.at[frm], send_sems, recv_sems, mask - 1, (x, y, c)).wait_recv()
        for cp in sends:
            cp.wait_send()
        mine.wait()

    vmem = pl.BlockSpec(memory_space=pltpu.VMEM)
    return pl.pallas_call(
        body, name="gather_small", in_specs=[vmem], out_specs=vmem,
        out_shape=jax.ShapeDtypeStruct((8,) + parts.shape, F32),
        scratch_shapes=[pltpu.SemaphoreType.DMA((7,)), pltpu.SemaphoreType.DMA((7,)),
                        pltpu.SemaphoreType.DMA],
        compiler_params=_params(),
    )(parts)


def _swap_halves(grads):
    def body(*refs):
        g_refs, r_refs = refs[:N_TENSORS], refs[N_TENSORS:2 * N_TENSORS]
        send_sems, recv_sems = refs[2 * N_TENSORS:]
        x, y, c = _place()
        cps = []
        for t in range(N_TENSORS):
            cp = _remote(_half_rows(g_refs[t], 1 - c), r_refs[t], send_sems, recv_sems, t, (x, y, 1 - c))
            cp.start()
            cps.append(cp)
        for cp in cps:
            cp.wait_recv()
        for cp in cps:
            cp.wait_send()

    hbm = pl.BlockSpec(memory_space=pl.ANY)
    return pl.pallas_call(
        body, name="swap_halves", in_specs=[hbm] * N_TENSORS, out_specs=[hbm] * N_TENSORS,
        out_shape=[jax.ShapeDtypeStruct((a.shape[0], a.shape[1] // 2, a.shape[2]), F32) for a in grads],
        scratch_shapes=[pltpu.SemaphoreType.DMA((N_TENSORS,)), pltpu.SemaphoreType.DMA((N_TENSORS,))],
        compiler_params=_params(),
    )(*grads)


def _add_halves(grad, recv, place, tile, name):
    _, rows, cols = grad.shape
    per_half = rows // 2 // tile

    def body(place_ref, g_ref, r_ref, o32_ref, o16_ref):
        v = g_ref[...] + r_ref[...]
        o32_ref[...] = v
        o16_ref[...] = v.astype(BF16)

    blk = lambda f: pl.BlockSpec((None, tile, cols), f)
    out = lambda s, i, p: (s, i, 0)
    return pl.pallas_call(
        body, name=name,
        grid_spec=pltpu.PrefetchScalarGridSpec(
            num_scalar_prefetch=1, grid=(N_SHARDS, per_half),
            in_specs=[blk(lambda s, i, p: (s, p[0] * per_half + i, 0)), blk(out)],
            out_specs=[blk(out), blk(out)]),
        out_shape=[jax.ShapeDtypeStruct(recv.shape, F32), jax.ShapeDtypeStruct(recv.shape, BF16)],
        compiler_params=_params(dimension_semantics=("arbitrary", "arbitrary")),
    )(place, grad, recv)


def _send_partials(parts16):
    n_fl = len(CHIP_FLIPS)

    def body(*refs):
        p_refs, r_refs = refs[:N_TENSORS], refs[N_TENSORS:2 * N_TENSORS]
        send_sems, recv_sems = refs[2 * N_TENSORS:]
        x, y, c = _place()
        cps = []
        for j, (fx, fy) in enumerate(CHIP_FLIPS):
            sj = 2 * (x ^ fx) + (y ^ fy)
            for t in range(N_TENSORS):
                cp = _remote(p_refs[t].at[sj], r_refs[t].at[j], send_sems, recv_sems,
                             N_TENSORS * j + t, (x ^ fx, y ^ fy, c))
                cp.start()
                cps.append(cp)
        for cp in cps:
            cp.wait_recv()
        for cp in cps:
            cp.wait_send()

    hbm = pl.BlockSpec(memory_space=pl.ANY)
    n_sem = N_TENSORS * n_fl
    return pl.pallas_call(
        body, name="send_partials", in_specs=[hbm] * N_TENSORS, out_specs=[hbm] * N_TENSORS,
        out_shape=[jax.ShapeDtypeStruct((n_fl,) + a.shape[1:], BF16) for a in parts16],
        scratch_shapes=[pltpu.SemaphoreType.DMA((n_sem,)), pltpu.SemaphoreType.DMA((n_sem,))],
        compiler_params=_params(),
    )(*parts16)


def _sum_partials(part32, recv16, place, tile, name):
    _, half, cols = part32.shape
    per_half = half // tile

    def body(place_ref, p_ref, r_ref, o_ref):
        acc = p_ref[...]
        for j in range(len(CHIP_FLIPS)):
            acc = acc + r_ref[j].astype(F32)
        o_ref[...] = acc

    return pl.pallas_call(
        body, name=name,
        grid_spec=pltpu.PrefetchScalarGridSpec(
            num_scalar_prefetch=1, grid=(per_half,),
            in_specs=[pl.BlockSpec((None, tile, cols), lambda i, p: (p[1], i, 0)),
                      pl.BlockSpec((len(CHIP_FLIPS), tile, cols), lambda i, p: (0, i, 0))],
            out_specs=pl.BlockSpec((tile, cols), lambda i, p: (p[0] * per_half + i, 0))),
        out_shape=jax.ShapeDtypeStruct((2 * half, cols), F32),
        compiler_params=_params(dimension_semantics=("arbitrary",)),
    )(place, part32, recv16)


def _join_halves(shards):
    def body(*refs):
        o_refs = refs[N_TENSORS:2 * N_TENSORS]
        send_sems, recv_sems = refs[2 * N_TENSORS:]
        x, y, c = _place()
        cps = []
        for t in range(N_TENSORS):
            cp = _remote(_half_rows(o_refs[t], c), _half_rows(o_refs[t], c), send_sems, recv_sems,
                         t, (x, y, 1 - c))
            cp.start()
            cps.append(cp)
        for t in range(N_TENSORS):
            _remote(_half_rows(o_refs[t], 1 - c), _half_rows(o_refs[t], 1 - c), send_sems, recv_sems,
                    t, (x, y, c)).wait_recv()
        for cp in cps:
            cp.wait_send()

    hbm = pl.BlockSpec(memory_space=pl.ANY)
    return pl.pallas_call(
        body, name="join_halves", in_specs=[hbm] * N_TENSORS, out_specs=[hbm] * N_TENSORS,
        out_shape=[jax.ShapeDtypeStruct(a.shape, F32) for a in shards],
        input_output_aliases={t: t for t in range(N_TENSORS)},
        scratch_shapes=[pltpu.SemaphoreType.DMA((N_TENSORS,)), pltpu.SemaphoreType.DMA((N_TENSORS,))],
        compiler_params=_params(),
    )(*shards)


def _adamw_math(w, g, m, v):
    m = ADAM_B1 * m + (1.0 - ADAM_B1) * g
    v = ADAM_B2 * v + (1.0 - ADAM_B2) * (g * g)
    m_hat = m / (1.0 - ADAM_B1 ** ADAM_STEP)
    v_hat = v / (1.0 - ADAM_B2 ** ADAM_STEP)
    delta = -ADAM_LR * (m_hat / (jnp.sqrt(v_hat) + ADAM_EPS) + ADAM_WD * w)
    return delta, m, v


def _adamw(w, g, m, v, tile, name):
    rows, cols = w.shape

    def body(w_ref, g_ref, m_ref, v_ref, d_ref, nm_ref, nv_ref):
        d_ref[...], nm_ref[...], nv_ref[...] = _adamw_math(w_ref[...], g_ref[...], m_ref[...], v_ref[...])

    blk = pl.BlockSpec((tile, cols), lambda i: (i, 0))
    return pl.pallas_call(
        body, name=name, grid=(rows // tile,), in_specs=[blk] * 4, out_specs=[blk] * 3,
        out_shape=[jax.ShapeDtypeStruct(w.shape, F32)] * 3,
        compiler_params=_params(dimension_semantics=("arbitrary",)),
    )(w, g, m, v)


ROW_NORM1, ROW_SCALE, ROW_LB, ROW_REC, ROW_FINAL, ROW_LOSS = 0, 1, 2, 4, 5, 6


def _small_update(gathered, w, m, v):
    def body(p_ref, w_ref, m_ref, v_ref, loss_ref, g_ref, d_ref, nm_ref, nv_ref):
        tot = p_ref[0]
        for d in range(1, 8):
            tot = tot + p_ref[d]
        wv = w_ref[...]
        l0 = wv[ROW_LB:ROW_LB + 1, :]
        l1 = wv[ROW_LB + 1:ROW_LB + 2, :]
        mx = jnp.maximum(l0, l1)
        e0 = jnp.exp(l0 - mx)
        e1 = jnp.exp(l1 - mx)
        lb = e0 / (e0 + e1)
        dl0 = tot[ROW_LB:ROW_LB + 1, :] * lb * (1.0 - lb)
        row = lax.broadcasted_iota(jnp.int32, tot.shape, 0)
        g = jnp.where(row == ROW_LB, dl0, jnp.where(row == ROW_LB + 1, -dl0, tot))
        g = jnp.where(row >= ROW_LOSS, 0.0, g)
        g_ref[...] = g
        d_ref[...], nm_ref[...], nv_ref[...] = _adamw_math(wv, g, m_ref[...], v_ref[...])
        loss_ref[...] = (0.5 / D_MODEL) * jnp.sum(tot[ROW_LOSS:ROW_LOSS + 1, :], axis=-1, keepdims=True)

    return pl.pallas_call(
        body, name="small_update",
        out_shape=[jax.ShapeDtypeStruct((1, 1), F32)] + [jax.ShapeDtypeStruct(w.shape, F32)] * 4,
        compiler_params=_params(),
    )(gathered, w, m, v)


def _small_rows(norm1, scale, lb, rec, final):
    pad = jnp.zeros((2, D_MODEL), F32)
    return jnp.concatenate([norm1, scale, lb, rec, final.reshape(1, D_MODEL), pad], axis=0)


def kernel(x, norm1_g, w_in, pool_w, pool_scale, lb_logits, rec_norm_g, w_out, final_norm_g, loss_target, m_norm1_g, m_w_in, m_pool_w, m_pool_scale, m_lb_logits, m_rec_norm_g, m_w_out, m_final_norm_g, v_norm1_g, v_w_in, v_pool_w, v_pool_scale, v_lb_logits, v_rec_norm_g, v_w_out, v_final_norm_g):
    pw_rows = N_GROUPS * PW_SHARD
    flat_pw = lambda a: a.reshape(pw_rows, PG)
    w_in_g, w_out_g, pw_g = _gather_weights(w_in[0], w_out[0], flat_pw(pool_w))
    pw_full = pw_g.reshape(N_SHARDS, N_GROUPS, PW_SHARD, PG).transpose(1, 0, 2, 3).reshape(N_GROUPS, PG, PG)
    loc = _local_step(x[0], loss_target[0], norm1_g, w_in_g, pw_full, pool_scale, lb_logits, rec_norm_g,
                      w_out_g.reshape(2 * D_MODEL, D_MODEL), final_norm_g.reshape(1, D_MODEL))

    zero = jnp.zeros((1, D_MODEL), F32)
    parts = jnp.concatenate([loc["part_x"][0:1], loc["dscale"], loc["part_rec"][1:2], zero,
                             loc["part_rec"][0:1], loc["part_out"][0:1], loc["part_out"][1:2], zero], axis=0)
    small_w = _small_rows(norm1_g, pool_scale, lb_logits, rec_norm_g, final_norm_g)
    small_m = _small_rows(m_norm1_g, m_pool_scale, m_lb_logits, m_rec_norm_g, m_final_norm_g)
    small_v = _small_rows(v_norm1_g, v_pool_scale, v_lb_logits, v_rec_norm_g, v_final_norm_g)
    loss, *small = _small_update(_gather_small(parts), small_w, small_m, small_v)

    xi, yi, ci = _place()
    place = jnp.stack([ci, 2 * xi + yi]).astype(jnp.int32)
    gpw = loc["gpw"].reshape(N_GROUPS, N_SHARDS, PW_SHARD, PG).transpose(1, 0, 2, 3).reshape(N_SHARDS, pw_rows, PG)
    grads = (loc["gw_in"], loc["gw_out"].reshape(N_SHARDS, W_OUT_SHARD, D_MODEL), gpw)
    tiles = (256, 256, 128)
    names = ("w_in", "w_out", "pool_w")
    recv = _swap_halves(grads)
    part = [_add_halves(g, r, place, t, "add_halves_" + n) for g, r, t, n in zip(grads, recv, tiles, names)]
    recv16 = _send_partials([p[1] for p in part])
    halves = [_sum_partials(p[0], r, place, t, "sum_partials_" + n)
              for p, r, t, n in zip(part, recv16, tiles, names)]
    g_in, g_out, g_pw = _join_halves(halves)
    big = []
    for w, g, m, v, name in ((w_in[0], g_in, m_w_in[0], v_w_in[0], "adamw_w_in"),
                             (w_out[0], g_out, m_w_out[0], v_w_out[0], "adamw_w_out"),
                             (flat_pw(pool_w), g_pw, flat_pw(m_pool_w), flat_pw(v_pool_w), "adamw_pool_w")):
        big.append((g,) + tuple(_adamw(w, g, m, v, 256, name)))

    def leaves(k):
        s = small[k]
        return (s[ROW_NORM1:ROW_NORM1 + 1], big[0][k][None], big[2][k].reshape(pool_w.shape),
                s[ROW_SCALE:ROW_SCALE + 1], s[ROW_LB:ROW_LB + 2], s[ROW_REC:ROW_REC + 1],
                big[1][k][None], s[ROW_FINAL])

    return (loss.reshape(()), loc["dx"][None], *leaves(0), *leaves(1), *leaves(2), *leaves(3))
```

```python
import functools

import numpy as np
import jax
import jax.numpy as jnp
from jax import lax
from jax.experimental import pallas as pl
from jax.experimental.pallas import tpu as pltpu

F32 = jnp.float32
BF16 = jnp.bfloat16

SEQ = 2048
D_MODEL = 1024
D_PROJ = 6144
N_SEC = 6
N_GROUPS = 4
PG = 256
N_HEADS = 8
HEAD = 128
CHUNK = 64
N_LEVELS = 6
N_SHARDS = 4
W_IN_SHARD = D_PROJ // N_SHARDS
W_OUT_SHARD = 2048 // N_SHARDS
PW_SHARD = PG // N_SHARDS
COL_BLK = 512
EPS = 1e-6

ADAM_LR = 0.001
ADAM_B1 = 0.9
ADAM_B2 = 0.999
ADAM_EPS = 1e-08
ADAM_WD = 0.01
ADAM_STEP = 10

V7X_VMEM_LIMIT = 56 * 1024 * 1024
MESH = pl.DeviceIdType.MESH


def _params(**kw):
    return pltpu.CompilerParams(vmem_limit_bytes=V7X_VMEM_LIMIT, **kw)


def _sig(x):
    return 1.0 / (1.0 + jnp.exp(-x))


def _dot(a, b):
    return jnp.dot(a, b, preferred_element_type=F32)


def _dot_nt(a, b):
    return lax.dot_general(a, b, (((1,), (1,)), ((), ())), preferred_element_type=F32)


def _dot_tn(a, b):
    return lax.dot_general(a, b, (((0,), (0,)), ((), ())), preferred_element_type=F32)


def _split3(a):
    p1 = a.astype(BF16)
    r1 = a - p1.astype(F32)
    p2 = r1.astype(BF16)
    p3 = (r1 - p2.astype(F32)).astype(BF16)
    return jnp.concatenate([p1, p2, p3], axis=-1)


def _dot3(w01, a):
    n = a.shape[-1]
    r = _dot(w01, _split3(a))
    return r[:, :n] + r[:, n:2 * n] + r[:, 2 * n:]


def _chunk_constants():
    j = np.arange(CHUNK)
    tt, ss = np.meshgrid(j, j, indexing="ij")
    x = tt ^ ss
    hb = np.full((CHUNK, CHUNK), -1, np.int32)
    for l in range(N_LEVELS):
        hb[x >= (1 << l)] = l
    lvl = np.where(tt > ss, hb, -1).astype(np.int32)
    tri = (ss <= tt).astype(np.float32)
    return tri, np.ascontiguousarray(tri.T), lvl, np.ascontiguousarray(lvl.T)


def _in_proj(x, g1, w_in_g):
    n_col = D_PROJ // COL_BLK
    per_shard = W_IN_SHARD // COL_BLK
    rows = 256

    def body(x_ref, g_ref, w_ref, proj_ref, h_ref):
        @pl.when(pl.program_id(0) == 0)
        def _():
            def norm(i, _):
                r = pl.ds(pl.multiple_of(i * rows, rows), rows)
                xv = x_ref[r, :]
                inv = lax.rsqrt(jnp.mean(xv * xv, axis=-1, keepdims=True) + EPS)
                h_ref[r, :] = (xv * inv * g_ref[...]).astype(BF16)
                return 0
            lax.fori_loop(0, SEQ // rows, norm, 0)

        def mm(i, _):
            r = pl.ds(pl.multiple_of(i * rows, rows), rows)
            proj_ref[r, :] = _dot(h_ref[r, :], w_ref[...])
            return 0
        lax.fori_loop(0, SEQ // rows, mm, 0)

    return pl.pallas_call(
        body, name="in_proj", grid=(n_col,),
        in_specs=[pl.BlockSpec((SEQ, D_MODEL), lambda j: (0, 0)),
                  pl.BlockSpec((1, D_MODEL), lambda j: (0, 0)),
                  pl.BlockSpec((None, D_MODEL, COL_BLK), lambda j: (j // per_shard, 0, j % per_shard))],
        out_specs=[pl.BlockSpec((SEQ, COL_BLK), lambda j: (0, j)),
                   pl.BlockSpec((SEQ, D_MODEL), lambda j: (0, 0))],
        out_shape=[jax.ShapeDtypeStruct((SEQ, D_PROJ), F32),
                   jax.ShapeDtypeStruct((SEQ, D_MODEL), BF16)],
        compiler_params=_params(dimension_semantics=("arbitrary",)),
    )(x, g1, w_in_g)


POOL_ROWS = 256
POOL_HALO = 16


def _window_sums(ext, g, shift_of):
    s = ext
    for k in range(N_GROUPS):
        s = jnp.where(k <= g, s + pltpu.roll(s, shift_of(k), 0), s)
    return s


def _pool_diff(u_ref, i, g):
    n = POOL_ROWS + POOL_HALO
    r0 = i * POOL_ROWS
    cur = u_ref[pl.ds(pl.multiple_of(r0, POOL_ROWS), POOL_ROWS), :]
    before = u_ref[pl.ds(pl.multiple_of(jnp.maximum(r0 - POOL_HALO, 0), 8), POOL_HALO), :]
    before = jnp.where(i > 0, before, 0.0)
    ext = jnp.concatenate([before, cur], axis=0)
    s = _window_sums(ext, g, lambda k: 1 << k)[POOL_HALO:, :]
    t = r0 + lax.broadcasted_iota(jnp.int32, (POOL_ROWS, 1), 0)
    width = (2 << g).astype(F32)
    inv_count = 1.0 / jnp.minimum((t + 1).astype(F32), width)
    return s * inv_count - cur, inv_count


def _pool_fwd(proj, pw_g, pool_scale):
    def body(u_ref, gate_ref, pw_ref, sc_ref, y_ref):
        g = pl.program_id(0)

        def step(i, _):
            r = pl.ds(pl.multiple_of(i * POOL_ROWS, POOL_ROWS), POOL_ROWS)
            d, _ = _pool_diff(u_ref, i, g)
            mixed = _dot(d.astype(BF16), pw_ref[...])
            gate = gate_ref[r, :]
            y_ref[r, :] = (mixed * sc_ref[...] * (gate * _sig(gate))).astype(BF16)
            return 0
        lax.fori_loop(0, SEQ // POOL_ROWS, step, 0)

    return pl.pallas_call(
        body, name="pool_fwd", grid=(N_GROUPS,),
        in_specs=[pl.BlockSpec((SEQ, PG), lambda g: (0, g)),
                  pl.BlockSpec((SEQ, PG), lambda g: (0, N_GROUPS + g)),
                  pl.BlockSpec((None, PG, PG), lambda g: (g, 0, 0)),
                  pl.BlockSpec((1, PG), lambda g: (0, g))],
        out_specs=pl.BlockSpec((SEQ, PG), lambda g: (0, g)),
        out_shape=jax.ShapeDtypeStruct((SEQ, D_MODEL), BF16),
        compiler_params=_params(dimension_semantics=("arbitrary",)),
    )(proj, proj, pw_g, pool_scale)


REC_ROWS = 512
REC_CHUNKS = REC_ROWS // CHUNK
N_REC_BLK = SEQ // REC_ROWS
SEC_BLK = D_MODEL // HEAD


def _lower_bound(lb_ref):
    l0 = lb_ref[0:1, :]
    l1 = lb_ref[1:2, :]
    mx = jnp.maximum(l0, l1)
    e0 = jnp.exp(l0 - mx)
    e1 = jnp.exp(l1 - mx)
    return e0 / (e0 + e1)


def _gates(q, fl, lb):
    qs = q * _sig(q)
    sf = _sig(fl)
    f = lb + (1.0 - lb) * sf
    return qs, sf, f, 1.0 - f, jnp.log(f)


def _level_factors(big_g, qs, k):
    t = lax.broadcasted_iota(jnp.int32, (CHUNK, HEAD), 0)
    row = lambda r, n: jnp.broadcast_to(big_g[r:r + 1, :], (n, HEAD))
    out = []
    for l in range(N_LEVELS):
        m = 1 << l
        if l == 0:
            g_mid = jnp.where((t & 1) == 1, pltpu.roll(big_g, 1, 0), big_g)
        elif l == 1:
            low = (t & 7) < 4
            g_mid = jnp.concatenate([jnp.where(low[:8], row(8 * i + 1, 8), row(8 * i + 5, 8))
                                     for i in range(CHUNK // 8)], axis=0)
        else:
            g_mid = jnp.concatenate([row(b * 2 * m + m - 1, 2 * m) for b in range(CHUNK // (2 * m))], axis=0)
        up = ((t >> l) & 1) == 1
        e = jnp.exp(jnp.where(up, big_g - g_mid, g_mid - big_g))
        x = jnp.where(up, qs, k) * e
        hi = x.astype(BF16)
        out.append((hi, (x - hi.astype(F32)).astype(BF16), e, up))
    return out


def _rec_fwd(proj, lb_logits, rec_g, tri, lvl):
    def body(q_ref, f_ref, i_ref, rg_ref, lb_ref, g_ref, w_ref, lvl_ref, y_ref, o_ref, stp_ref, st_ref):
        @pl.when(pl.program_id(1) == 0)
        def _():
            st_ref[...] = jnp.zeros_like(st_ref)
        lb = _lower_bound(lb_ref)
        st = st_ref[...]
        for c in range(REC_CHUNKS):
            r = pl.ds(c * CHUNK, CHUNK)
            v = i_ref[r, :]
            rg = rg_ref[r, :]
            qs, _, _, k, g = _gates(q_ref[r, :], f_ref[r, :], lb)
            big_g = _dot3(w_ref[...], g)
            a = jnp.zeros((CHUNK, CHUNK), F32)
            lv = lvl_ref[...]
            for l, (xl, _, _, _) in enumerate(_level_factors(big_g, qs, k)):
                a = a + jnp.where(lv == l, _dot_nt(xl, xl), 0.0)
            stp_ref[c] = st
            vb = v.astype(BF16)
            diag = jnp.sum(qs * k, axis=-1, keepdims=True)
            o = (_dot(a.astype(BF16), vb) + diag * v
                 + _dot_nt((qs * jnp.exp(big_g)).astype(BF16), st.astype(BF16)))
            g_last = big_g[CHUNK - 1:CHUNK, :]
            kdec = (k * jnp.exp(g_last - big_g)).astype(BF16)
            st = st * jnp.exp(g_last) + _dot_tn(vb, kdec)
            o_ref[r, :] = o
            inv = lax.rsqrt(jnp.mean(o * o, axis=-1, keepdims=True) + EPS)
            y_ref[r, :] = (o * inv * g_ref[...] * (rg * _sig(rg))).astype(BF16)
        st_ref[...] = st

    sec = lambda n: pl.BlockSpec((REC_ROWS, HEAD), lambda h, b: (b, n * SEC_BLK + h))
    vec = lambda rows: pl.BlockSpec((rows, HEAD), lambda h, b: (0, h))
    full = lambda a: pl.BlockSpec(a.shape, lambda h, b: (0,) * a.ndim)
    return pl.pallas_call(
        body, name="rec_fwd", grid=(N_HEADS, N_REC_BLK),
        in_specs=[sec(2), sec(3), sec(4), sec(5), vec(2), vec(1), full(tri), full(lvl)],
        out_specs=[pl.BlockSpec((REC_ROWS, HEAD), lambda h, b: (b, h)),
                   pl.BlockSpec((REC_ROWS, HEAD), lambda h, b: (b, h)),
                   pl.BlockSpec((None, REC_CHUNKS, HEAD, HEAD), lambda h, b: (h, b, 0, 0))],
        out_shape=[jax.ShapeDtypeStruct((SEQ, D_MODEL), BF16),
                   jax.ShapeDtypeStruct((SEQ, D_MODEL), F32),
                   jax.ShapeDtypeStruct((N_HEADS, SEQ // CHUNK, HEAD, HEAD), F32)],
        scratch_shapes=[pltpu.VMEM((HEAD, HEAD), F32)],
        compiler_params=_params(dimension_semantics=("arbitrary", "arbitrary")),
    )(proj, proj, proj, proj, lb_logits, rec_g, tri, lvl)


OUT_ROWS = 256


def _out_proj_loss(y_pool, y_rec, w_out_g, x, target, gf):
    def body(yp_ref, yr_ref, w_ref, x_ref, t_ref, gf_ref, dout_ref, doutb_ref, part_ref):
        @pl.when(pl.program_id(0) == 0)
        def _():
            part_ref[...] = jnp.zeros_like(part_ref)
        out = (x_ref[...] + _dot(yp_ref[...], w_ref[0:D_MODEL, :])
               + _dot(yr_ref[...], w_ref[D_MODEL:2 * D_MODEL, :]))
        inv = lax.rsqrt(jnp.mean(out * out, axis=-1, keepdims=True) + EPS)
        gf_v = gf_ref[...]
        diff = out * inv * gf_v - t_ref[...]
        dyf = diff * (1.0 / D_MODEL)
        a = dyf * gf_v
        dout = inv * a - out * (inv * inv * inv) * jnp.mean(a * out, axis=-1, keepdims=True)
        dout_ref[...] = dout
        doutb_ref[...] = dout.astype(BF16)
        part_ref[0:1, :] += jnp.sum(dyf * out * inv, axis=0, keepdims=True)
        part_ref[1:2, :] += jnp.sum(diff * diff, axis=0, keepdims=True)

    row = lambda n: pl.BlockSpec((OUT_ROWS, n), lambda i: (i, 0))
    return pl.pallas_call(
        body, name="out_proj_loss", grid=(SEQ // OUT_ROWS,),
        in_specs=[row(D_MODEL), row(D_MODEL), pl.BlockSpec((2 * D_MODEL, D_MODEL), lambda i: (0, 0)),
                  row(D_MODEL), row(D_MODEL), pl.BlockSpec((1, D_MODEL), lambda i: (0, 0))],
        out_specs=[row(D_MODEL), row(D_MODEL), pl.BlockSpec((8, D_MODEL), lambda i: (0, 0))],
        out_shape=[jax.ShapeDtypeStruct((SEQ, D_MODEL), F32),
                   jax.ShapeDtypeStruct((SEQ, D_MODEL), BF16),
                   jax.ShapeDtypeStruct((8, D_MODEL), F32)],
        compiler_params=_params(dimension_semantics=("arbitrary",)),
    )(y_pool, y_rec, w_out_g, x, target, gf)


def _grad_w_out(y_pool, y_rec, dout_b):
    blk = 256
    per = D_MODEL // blk

    def body(yp_ref, yr_ref, d_ref, o_ref):
        j = pl.program_id(0)

        @pl.when(j < per)
        def _():
            o_ref[...] = _dot_tn(yp_ref[...], d_ref[...])

        @pl.when(j >= per)
        def _():
            o_ref[...] = _dot_tn(yr_ref[...], d_ref[...])

    return pl.pallas_call(
        body, name="grad_w_out", grid=(2 * per,),
        in_specs=[pl.BlockSpec((SEQ, blk), lambda j: (0, jnp.minimum(j, per - 1))),
                  pl.BlockSpec((SEQ, blk), lambda j: (0, jnp.maximum(j - per, 0))),
                  pl.BlockSpec((SEQ, D_MODEL), lambda j: (0, 0))],
        out_specs=pl.BlockSpec((blk, D_MODEL), lambda j: (j, 0)),
        out_shape=jax.ShapeDtypeStruct((2 * D_MODEL, D_MODEL), F32),
        compiler_params=_params(dimension_semantics=("arbitrary",)),
    )(y_pool, y_rec, dout_b)


def _pool_bwd(proj, dout_b, w_out_g, pw_g, pool_scale):
    n = POOL_ROWS + POOL_HALO

    def body(u_ref, gate_ref, d_ref, wo_ref, pw_ref, sc_ref,
             du_ref, dgate_ref, dpw_ref, dsc_ref, dd_ref, ddw_ref):
        g = pl.program_id(0)
        dpw_ref[...] = jnp.zeros_like(dpw_ref)
        dsc_ref[...] = jnp.zeros_like(dsc_ref)

        def first(i, _):
            r = pl.ds(pl.multiple_of(i * POOL_ROWS, POOL_ROWS), POOL_ROWS)
            d, inv_count = _pool_diff(u_ref, i, g)
            db = d.astype(BF16)
            mixed = _dot(db, pw_ref[...])
            gate = gate_ref[r, :]
            sg = _sig(gate)
            silu = gate * sg
            dy = _dot_nt(d_ref[r, :], wo_ref[...])
            sc = sc_ref[...]
            dmixed = dy * silu * sc
            dgate_ref[r, :] = (dy * mixed * sc * (sg * (1.0 + gate * (1.0 - sg)))).astype(BF16)
            dsc_ref[...] += jnp.sum(dy * silu * mixed, axis=0, keepdims=True)
            dmb = dmixed.astype(BF16)
            dpw_ref[...] += _dot_tn(db, dmb)
            dd = _dot_nt(dmb, pw_ref[...])
            dd_ref[r, :] = dd
            ddw_ref[r, :] = dd * inv_count
            return 0
        lax.fori_loop(0, SEQ // POOL_ROWS, first, 0)

        def second(i, _):
            r0 = i * POOL_ROWS
            r = pl.ds(pl.multiple_of(r0, POOL_ROWS), POOL_ROWS)
            last = i == SEQ // POOL_ROWS - 1
            after = ddw_ref[pl.ds(pl.multiple_of(jnp.minimum(r0 + POOL_ROWS, SEQ - POOL_HALO), 8), POOL_HALO), :]
            after = jnp.where(last, 0.0, after)
            ext = jnp.concatenate([ddw_ref[r, :], after], axis=0)
            s = _window_sums(ext, g, lambda k: n - (1 << k))[:POOL_ROWS, :]
            du_ref[r, :] = (s - dd_ref[r, :]).astype(BF16)
            return 0
        lax.fori_loop(0, SEQ // POOL_ROWS, second, 0)

    return pl.pallas_call(
        body, name="pool_bwd", grid=(N_GROUPS,),
        in_specs=[pl.BlockSpec((SEQ, PG), lambda g: (0, g)),
                  pl.BlockSpec((SEQ, PG), lambda g: (0, N_GROUPS + g)),
                  pl.BlockSpec((SEQ, D_MODEL), lambda g: (0, 0)),
                  pl.BlockSpec((PG, D_MODEL), lambda g: (g, 0)),
                  pl.BlockSpec((None, PG, PG), lambda g: (g, 0, 0)),
                  pl.BlockSpec((1, PG), lambda g: (0, g))],
        out_specs=[pl.BlockSpec((SEQ, PG), lambda g: (0, g)),
                   pl.BlockSpec((SEQ, PG), lambda g: (0, g)),
                   pl.BlockSpec((None, PG, PG), lambda g: (g, 0, 0)),
                   pl.BlockSpec((1, PG), lambda g: (0, g))],
        out_shape=[jax.ShapeDtypeStruct((SEQ, D_MODEL), BF16),
                   jax.ShapeDtypeStruct((SEQ, D_MODEL), BF16),
                   jax.ShapeDtypeStruct((N_GROUPS, PG, PG), F32),
                   jax.ShapeDtypeStruct((1, D_MODEL), F32)],
        scratch_shapes=[pltpu.VMEM((SEQ, PG), F32), pltpu.VMEM((SEQ, PG), F32)],
        compiler_params=_params(dimension_semantics=("arbitrary",)),
    )(proj, proj, dout_b, w_out_g, pw_g, pool_scale)


def _rec_bwd(proj, o_raw, st_prev, dout_b, w_out_g, lb_logits, rec_g, tri, tri_t, lvl, lvl_t):
    def body(q_ref, f_ref, i_ref, rg_ref, o_ref, stp_ref, d_ref, wo_ref, lb_ref, g_ref,
             w_ref, lvl_ref, lvlt_ref, tri_ref,
             dq_ref, df_ref, di_ref, drg_ref, part_ref, dst_ref):
        @pl.when(pl.program_id(1) == 0)
        def _():
            dst_ref[...] = jnp.zeros_like(dst_ref)
            part_ref[...] = jnp.zeros_like(part_ref)
        lb = _lower_bound(lb_ref)
        grec = g_ref[...]

        dst = dst_ref[...]
        acc_grec = jnp.zeros((1, HEAD), F32)
        acc_lb = jnp.zeros((1, HEAD), F32)
        for c in reversed(range(REC_CHUNKS)):
            r = pl.ds(c * CHUNK, CHUNK)
            q = q_ref[r, :]
            v = i_ref[r, :]
            rg = rg_ref[r, :]
            o = o_ref[r, :]
            dy = _dot_nt(d_ref[r, :], wo_ref[...])
            sg = _sig(rg)
            silu = rg * sg
            inv = lax.rsqrt(jnp.mean(o * o, axis=-1, keepdims=True) + EPS)
            recn = o * inv
            drg_ref[r, :] = (dy * recn * grec * (sg * (1.0 + rg * (1.0 - sg)))).astype(BF16)
            acc_grec = acc_grec + jnp.sum(dy * silu * recn, axis=0, keepdims=True)
            drecn = dy * silu * grec
            do = inv * drecn - o * (inv * inv * inv) * jnp.mean(drecn * o, axis=-1, keepdims=True)
            qs, sf, f, k, g = _gates(q, f_ref[r, :], lb)
            big_g = _dot3(w_ref[...], g)
            g_last = big_g[CHUNK - 1:CHUNK, :]
            e_g = jnp.exp(big_g)
            e_rev = jnp.exp(g_last - big_g)
            e_last = jnp.exp(g_last)
            levels = _level_factors(big_g, qs, k)
            lv = lvl_ref[...]
            lvt = lvlt_ref[...]
            a_t = jnp.zeros((CHUNK, CHUNK), F32)
            for l, (xl, _, _, _) in enumerate(levels):
                a_t = a_t + jnp.where(lvt == l, _dot_nt(xl, xl), 0.0)
            dob = do.astype(BF16)
            vb = v.astype(BF16)
            d_a = _dot_nt(dob, vb)
            d_at = _dot_nt(vb, dob)
            stp = stp_ref[c]
            dstb = dst.astype(BF16)
            q_g = qs * e_g
            kdec = k * e_rev
            diag = jnp.sum(qs * k, axis=-1, keepdims=True)
            dv = _dot(a_t.astype(BF16), dob) + diag * do + _dot_nt(kdec.astype(BF16), dstb)
            dq_g = _dot(dob, stp.astype(BF16))
            dkdec = _dot(vb, dstb)
            de_last = jnp.sum(stp * dst, axis=0, keepdims=True)
            dst = dst * e_last + _dot_tn(dob, q_g.astype(BF16))
            dqs_i = jnp.zeros((CHUNK, HEAD), F32)
            dk_i = jnp.zeros((CHUNK, HEAD), F32)
            for l, (xl, xlo, e, up) in enumerate(levels):
                z = jnp.where(lv == l, d_a, jnp.where(lvt == l, d_at, 0.0))
                tmp = _dot(z.astype(BF16), jnp.concatenate([xl, xlo], axis=-1))
                tmp = (tmp[:, :HEAD] + tmp[:, HEAD:]) * e
                dqs_i = dqs_i + jnp.where(up, tmp, 0.0)
                dk_i = dk_i + jnp.where(up, 0.0, tmp)
            ddiag = jnp.sum(do * v, axis=-1, keepdims=True)
            dqs = dqs_i + ddiag * k + dq_g * e_g
            dk = dk_i + ddiag * qs + dkdec * e_rev
            dg_rev = dkdec * kdec
            dg_last = jnp.sum(dg_rev, axis=0, keepdims=True) + de_last * e_last
            dbig_g = qs * dqs_i - k * dk_i + dq_g * q_g - dg_rev
            dg = _dot3(tri_ref[...], dbig_g) + dg_last
            df = dg / f - dk
            df_ref[r, :] = (df * (1.0 - lb) * sf * (1.0 - sf)).astype(BF16)
            acc_lb = acc_lb + jnp.sum(df * (1.0 - sf), axis=0, keepdims=True)
            sq = _sig(q)
            dq_ref[r, :] = (dqs * (sq * (1.0 + q * (1.0 - sq)))).astype(BF16)
            di_ref[r, :] = dv.astype(BF16)
        dst_ref[...] = dst
        part_ref[0:1, :] += acc_grec
        part_ref[1:2, :] += acc_lb

    rev = lambda b: N_REC_BLK - 1 - b
    sec = lambda n: pl.BlockSpec((REC_ROWS, HEAD), lambda h, b: (rev(b), n * SEC_BLK + h))
    col = pl.BlockSpec((REC_ROWS, HEAD), lambda h, b: (rev(b), h))
    vec = lambda rows: pl.BlockSpec((rows, HEAD), lambda h, b: (0, h))
    full = lambda a: pl.BlockSpec(a.shape, lambda h, b: (0,) * a.ndim)
    return pl.pallas_call(
        body, name="rec_bwd", grid=(N_HEADS, N_REC_BLK),
        in_specs=[sec(2), sec(3), sec(4), sec(5), col,
                  pl.BlockSpec((None, REC_CHUNKS, HEAD, HEAD), lambda h, b: (h, rev(b), 0, 0)),
                  pl.BlockSpec((REC_ROWS, D_MODEL), lambda h, b: (rev(b), 0)),
                  pl.BlockSpec((HEAD, D_MODEL), lambda h, b: (SEC_BLK + h, 0)),
                  vec(2), vec(1), full(tri), full(lvl), full(lvl_t), full(tri_t)],
        out_specs=[col, col, col, col, vec(8)],
        out_shape=[jax.ShapeDtypeStruct((SEQ, D_MODEL), BF16)] * 4
                  + [jax.ShapeDtypeStruct((8, D_MODEL), F32)],
        scratch_shapes=[pltpu.VMEM((HEAD, HEAD), F32)],
        compiler_params=_params(dimension_semantics=("arbitrary", "arbitrary")),
    )(proj, proj, proj, proj, o_raw, st_prev, dout_b, w_out_g, lb_logits, rec_g, tri, lvl, lvl_t, tri_t)


def _grad_x(dproj, w_in_g, x, g1, dout):
    rows = 256
    per_shard = W_IN_SHARD // COL_BLK
    per_sec = D_MODEL // COL_BLK

    def body(*refs):
        dp_refs = refs[:N_SEC]
        w_ref, x_ref, g_ref, dout_ref, dx_ref, part_ref = refs[N_SEC:]

        @pl.when(pl.program_id(0) == 0)
        def _():
            part_ref[...] = jnp.zeros_like(part_ref)
        dh = jnp.zeros((rows, D_MODEL), F32)
        for n in range(N_SEC):
            for p in range(per_sec):
                j = n * per_sec + p
                w = w_ref[j // per_shard, :, (j % per_shard) * COL_BLK:(j % per_shard + 1) * COL_BLK]
                dh = dh + _dot_nt(dp_refs[n][:, p * COL_BLK:(p + 1) * COL_BLK], w)
        xv = x_ref[...]
        inv = lax.rsqrt(jnp.mean(xv * xv, axis=-1, keepdims=True) + EPS)
        a = dh * g_ref[...]
        dx_ref[...] = (dout_ref[...] + inv * a
                       - xv * (inv * inv * inv) * jnp.mean(a * xv, axis=-1, keepdims=True))
        part_ref[0:1, :] += jnp.sum(dh * xv * inv, axis=0, keepdims=True)

    row = lambda: pl.BlockSpec((rows, D_MODEL), lambda i: (i, 0))
    return pl.pallas_call(
        body, name="grad_x", grid=(SEQ // rows,),
        in_specs=[row() for _ in range(N_SEC)]
                 + [pl.BlockSpec((N_SHARDS, D_MODEL, W_IN_SHARD), lambda i: (0, 0, 0)),
                    row(), pl.BlockSpec((1, D_MODEL), lambda i: (0, 0)), row()],
        out_specs=[row(), pl.BlockSpec((8, D_MODEL), lambda i: (0, 0))],
        out_shape=[jax.ShapeDtypeStruct((SEQ, D_MODEL), F32),
                   jax.ShapeDtypeStruct((8, D_MODEL), F32)],
        compiler_params=_params(dimension_semantics=("arbitrary",)),
    )(*dproj, w_in_g, x, g1, dout)


def _grad_w_in(h, dproj):
    per_shard = W_IN_SHARD // COL_BLK
    per_sec = D_MODEL // COL_BLK
    n_col = D_PROJ // COL_BLK

    def body(h_ref, *refs):
        dp_refs = refs[:N_SEC]
        o_ref = refs[N_SEC]
        j = pl.program_id(0)
        for n in range(N_SEC):
            @pl.when(j // per_sec == n)
            def _(n=n):
                o_ref[...] = _dot_tn(h_ref[...], dp_refs[n][...])

    def dp_spec(n):
        return pl.BlockSpec((SEQ, COL_BLK), lambda j: (0, jnp.where(j // per_sec == n, j % per_sec, 0)))

    return pl.pallas_call(
        body, name="grad_w_in", grid=(n_col,),
        in_specs=[pl.BlockSpec((SEQ, D_MODEL), lambda j: (0, 0))] + [dp_spec(n) for n in range(N_SEC)],
        out_specs=pl.BlockSpec((None, D_MODEL, COL_BLK), lambda j: (j // per_shard, 0, j % per_shard)),
        out_shape=jax.ShapeDtypeStruct((N_SHARDS, D_MODEL, W_IN_SHARD), F32),
        compiler_params=_params(dimension_semantics=("arbitrary",)),
    )(h, *dproj)


def _local_step(x, target, g1, w_in_g, pw_g, pool_scale, lb_logits, rec_g, w_out_g, gf):
    tri, tri_t, lvl, lvl_t = _chunk_constants()
    tri = jnp.asarray(tri, BF16)
    tri_t = jnp.asarray(tri_t, BF16)
    lvl = jnp.asarray(lvl)
    lvl_t = jnp.asarray(lvl_t)
    proj, h = _in_proj(x, g1, w_in_g)
    y_pool = _pool_fwd(proj, pw_g, pool_scale)
    y_rec, o_raw, st_prev = _rec_fwd(proj, lb_logits, rec_g, tri, lvl)
    dout, dout_b, part_out = _out_proj_loss(y_pool, y_rec, w_out_g, x, target, gf)
    gw_out = _grad_w_out(y_pool, y_rec, dout_b)
    du, dgate, gpw, dscale = _pool_bwd(proj, dout_b, w_out_g, pw_g, pool_scale)
    dq, df, di, drg, part_rec = _rec_bwd(proj, o_raw, st_prev, dout_b, w_out_g, lb_logits, rec_g,
                                         tri, tri_t, lvl, lvl_t)
    dproj = (du, dgate, dq, df, di, drg)
    dx, part_x = _grad_x(dproj, w_in_g, x, g1, dout)
    gw_in = _grad_w_in(h, dproj)
    return dict(dx=dx, gw_in=gw_in, gw_out=gw_out, gpw=gpw, dscale=dscale,
                part_out=part_out, part_rec=part_rec, part_x=part_x)


CHIP_FLIPS = ((1, 0), (0, 1), (1, 1))
N_TENSORS = 3


def _place():
    return lax.axis_index("x"), lax.axis_index("y"), lax.axis_index("c")


def _remote(src, dst, send_sems, recv_sems, k, to):
    return pltpu.make_async_remote_copy(src_ref=src, dst_ref=dst, send_sem=send_sems.at[k],
                                        recv_sem=recv_sems.at[k], device_id=to, device_id_type=MESH)


def _half_rows(ref, c):
    half = ref.shape[-2] // 2
    rows = pl.ds(pl.multiple_of(c * half, half), half)
    return ref.at[:, rows, :] if len(ref.shape) == 3 else ref.at[rows, :]


def _gather_weights(w_in, w_out, pool_w):
    shards = (w_in, w_out, pool_w)
    n_fl = len(CHIP_FLIPS)

    def body(wi_ref, wo_ref, pw_ref, wi_g, wo_g, pw_g, send_sems, recv_sems):
        x, y, c = _place()
        s = 2 * x + y
        for src, dst in ((wi_ref, wi_g), (wo_ref, wo_g), (pw_ref, pw_g)):
            step = 64
            def cast(i, _, src=src, dst=dst):
                r = pl.ds(pl.multiple_of(i * step, step), step)
                dst.at[s][r, :] = src[r, :].astype(BF16)
                return 0
            lax.fori_loop(0, src.shape[0] // step, cast, 0)
        outs = (wi_g, wo_g, pw_g)

        def piece(t, shard, half):
            return _half_rows(outs[t].at[shard], half)

        first, passed = [], []
        for j, (fx, fy) in enumerate(CHIP_FLIPS):
            for t in range(N_TENSORS):
                cp = _remote(piece(t, s, c), piece(t, s, c), send_sems, recv_sems,
                             N_TENSORS * j + t, (x ^ fx, y ^ fy, c))
                cp.start()
                first.append(cp)
        for j, (fx, fy) in enumerate(CHIP_FLIPS):
            sj = 2 * (x ^ fx) + (y ^ fy)
            for t in range(N_TENSORS):
                k = N_TENSORS * j + t
                _remote(piece(t, sj, c), piece(t, sj, c), send_sems, recv_sems, k, (x, y, c)).wait_recv()
                cp = _remote(piece(t, sj, c), piece(t, sj, c), send_sems, recv_sems,
                             N_TENSORS * n_fl + k, (x, y, 1 - c))
                cp.start()
                passed.append(cp)
        for j, (fx, fy) in enumerate(CHIP_FLIPS):
            sj = 2 * (x ^ fx) + (y ^ fy)
            for t in range(N_TENSORS):
                k = N_TENSORS * n_fl + N_TENSORS * j + t
                _remote(piece(t, sj, 1 - c), piece(t, sj, 1 - c), send_sems, recv_sems, k, (x, y, c)).wait_recv()
        for cp in first + passed:
            cp.wait_send()

    n_sem = 2 * N_TENSORS * n_fl
    vmem = pl.BlockSpec(memory_space=pltpu.VMEM)
    return pl.pallas_call(
        body, name="gather_weights",
        in_specs=[vmem] * 3, out_specs=[vmem] * 3,
        out_shape=[jax.ShapeDtypeStruct((N_SHARDS,) + a.shape, BF16) for a in shards],
        scratch_shapes=[pltpu.SemaphoreType.DMA((n_sem,)), pltpu.SemaphoreType.DMA((n_sem,))],
        compiler_params=_params(),
    )(*shards)


def _gather_small(parts):
    def body(p_ref, out_ref, send_sems, recv_sems, local_sem):
        x, y, c = _place()
        me = 4 * x + 2 * y + c
        mine = pltpu.make_async_copy(p_ref, out_ref.at[me], local_sem)
        mine.start()
        sends = []
        for mask in range(1, 8):
            to = (x ^ (mask >> 2), y ^ ((mask >> 1) & 1), c ^ (mask & 1))
            cp = _remote(p_ref, out_ref.at[me], send_sems, recv_sems, mask - 1, to)
            cp.start()
            sends.append(cp)
        for mask in range(1, 8):
            frm = me ^ mask
            _remote(p_ref, out_ref.at[frm], send_sems, recv_sems, mask - 1, (x, y, c)).wait_recv()
        for cp in sends:
            cp.wait_send()
        mine.wait()

    vmem = pl.BlockSpec(memory_space=pltpu.VMEM)
    return pl.pallas_call(
        body, name="gather_small", in_specs=[vmem], out_specs=vmem,
        out_shape=jax.ShapeDtypeStruct((8,) + parts.shape, F32),
        scratch_shapes=[pltpu.SemaphoreType.DMA((7,)), pltpu.SemaphoreType.DMA((7,)),
                        pltpu.SemaphoreType.DMA],
        compiler_params=_params(),
    )(parts)


def _swap_halves(grads):
    def body(*refs):
        g_refs, r_refs = refs[:N_TENSORS], refs[N_TENSORS:2 * N_TENSORS]
        send_sems, recv_sems = refs[2 * N_TENSORS:]
        x, y, c = _place()
        cps = []
        for t in range(N_TENSORS):
            cp = _remote(_half_rows(g_refs[t], 1 - c), r_refs[t], send_sems, recv_sems, t, (x, y, 1 - c))
            cp.start()
            cps.append(cp)
        for cp in cps:
            cp.wait_recv()
        for cp in cps:
            cp.wait_send()

    hbm = pl.BlockSpec(memory_space=pl.ANY)
    return pl.pallas_call(
        body, name="swap_halves", in_specs=[hbm] * N_TENSORS, out_specs=[hbm] * N_TENSORS,
        out_shape=[jax.ShapeDtypeStruct((a.shape[0], a.shape[1] // 2, a.shape[2]), F32) for a in grads],
        scratch_shapes=[pltpu.SemaphoreType.DMA((N_TENSORS,)), pltpu.SemaphoreType.DMA((N_TENSORS,))],
        compiler_params=_params(),
    )(*grads)


def _add_halves(grad, recv, place, tile, name):
    _, rows, cols = grad.shape
    per_half = rows // 2 // tile

    def body(place_ref, g_ref, r_ref, o32_ref, o16_ref):
        v = g_ref[...] + r_ref[...]
        o32_ref[...] = v
        o16_ref[...] = v.astype(BF16)

    blk = lambda f: pl.BlockSpec((None, tile, cols), f)
    out = lambda s, i, p: (s, i, 0)
    return pl.pallas_call(
        body, name=name,
        grid_spec=pltpu.PrefetchScalarGridSpec(
            num_scalar_prefetch=1, grid=(N_SHARDS, per_half),
            in_specs=[blk(lambda s, i, p: (s, p[0] * per_half + i, 0)), blk(out)],
            out_specs=[blk(out), blk(out)]),
        out_shape=[jax.ShapeDtypeStruct(recv.shape, F32), jax.ShapeDtypeStruct(recv.shape, BF16)],
        compiler_params=_params(dimension_semantics=("arbitrary", "arbitrary")),
    )(place, grad, recv)


def _send_partials(parts16):
    n_fl = len(CHIP_FLIPS)

    def body(*refs):
        p_refs, r_refs = refs[:N_TENSORS], refs[N_TENSORS:2 * N_TENSORS]
        send_sems, recv_sems = refs[2 * N_TENSORS:]
        x, y, c = _place()
        cps = []
        for j, (fx, fy) in enumerate(CHIP_FLIPS):
            sj = 2 * (x ^ fx) + (y ^ fy)
            for t in range(N_TENSORS):
                cp = _remote(p_refs[t].at[sj], r_refs[t].at[j], send_sems, recv_sems,
                             N_TENSORS * j + t, (x ^ fx, y ^ fy, c))
                cp.start()
                cps.append(cp)
        for cp in cps:
            cp.wait_recv()
        for cp in cps:
            cp.wait_send()

    hbm = pl.BlockSpec(memory_space=pl.ANY)
    n_sem = N_TENSORS * n_fl
    return pl.pallas_call(
        body, name="send_partials", in_specs=[hbm] * N_TENSORS, out_specs=[hbm] * N_TENSORS,
        out_shape=[jax.ShapeDtypeStruct((n_fl,) + a.shape[1:], BF16) for a in parts16],
        scratch_shapes=[pltpu.SemaphoreType.DMA((n_sem,)), pltpu.SemaphoreType.DMA((n_sem,))],
        compiler_params=_params(),
    )(*parts16)


def _sum_partials(part32, recv16, place, tile, name):
    _, half, cols = part32.shape
    per_half = half // tile

    def body(place_ref, p_ref, r_ref, o_ref):
        acc = p_ref[...]
        for j in range(len(CHIP_FLIPS)):
            acc = acc + r_ref[j].astype(F32)
        o_ref[...] = acc

    return pl.pallas_call(
        body, name=name,
        grid_spec=pltpu.PrefetchScalarGridSpec(
            num_scalar_prefetch=1, grid=(per_half,),
            in_specs=[pl.BlockSpec((None, tile, cols), lambda i, p: (p[1], i, 0)),
                      pl.BlockSpec((len(CHIP_FLIPS), tile, cols), lambda i, p: (0, i, 0))],
            out_specs=pl.BlockSpec((tile, cols), lambda i, p: (p[0] * per_half + i, 0))),
        out_shape=jax.ShapeDtypeStruct((2 * half, cols), F32),
        compiler_params=_params(dimension_semantics=("arbitrary",)),
    )(place, part32, recv16)


def _join_halves(shards):
    def body(*refs):
        o_refs = refs[N_TENSORS:2 * N_TENSORS]
        send_sems, recv_sems = refs[2 * N_TENSORS:]
        x, y, c = _place()
        cps = []
        for t in range(N_TENSORS):
            cp = _remote(_half_rows(o_refs[t], c), _half_rows(o_refs[t], c), send_sems, recv_sems,
                         t, (x, y, 1 - c))
            cp.start()
            cps.append(cp)
        for t in range(N_TENSORS):
            _remote(_half_rows(o_refs[t], 1 - c), _half_rows(o_refs[t], 1 - c), send_sems, recv_sems,
                    t, (x, y, c)).wait_recv()
        for cp in cps:
            cp.wait_send()

    hbm = pl.BlockSpec(memory_space=pl.ANY)
    return pl.pallas_call(
        body, name="join_halves", in_specs=[hbm] * N_TENSORS, out_specs=[hbm] * N_TENSORS,
        out_shape=[jax.ShapeDtypeStruct(a.shape, F32) for a in shards],
        input_output_aliases={t: t for t in range(N_TENSORS)},
        scratch_shapes=[pltpu.SemaphoreType.DMA((N_TENSORS,)), pltpu.SemaphoreType.DMA((N_TENSORS,))],
        compiler_params=_params(),
    )(*shards)


def _adamw_math(w, g, m, v):
    m = ADAM_B1 * m + (1.0 - ADAM_B1) * g
    v = ADAM_B2 * v + (1.0 - ADAM_B2) * (g * g)
    m_hat = m / (1.0 - ADAM_B1 ** ADAM_STEP)
    v_hat = v / (1.0 - ADAM_B2 ** ADAM_STEP)
    delta = -ADAM_LR * (m_hat / (jnp.sqrt(v_hat) + ADAM_EPS) + ADAM_WD * w)
    return delta, m, v


def _adamw(w, g, m, v, tile, name):
    rows, cols = w.shape

    def body(w_ref, g_ref, m_ref, v_ref, d_ref, nm_ref, nv_ref):
        d_ref[...], nm_ref[...], nv_ref[...] = _adamw_math(w_ref[...], g_ref[...], m_ref[...], v_ref[...])

    blk = pl.BlockSpec((tile, cols), lambda i: (i, 0))
    return pl.pallas_call(
        body, name=name, grid=(rows // tile,), in_specs=[blk] * 4, out_specs=[blk] * 3,
        out_shape=[jax.ShapeDtypeStruct(w.shape, F32)] * 3,
        compiler_params=_params(dimension_semantics=("arbitrary",)),
    )(w, g, m, v)


ROW_NORM1, ROW_SCALE, ROW_LB, ROW_REC, ROW_FINAL, ROW_LOSS = 0, 1, 2, 4, 5, 6


def _small_update(gathered, w, m, v):
    def body(p_ref, w_ref, m_ref, v_ref, loss_ref, g_ref, d_ref, nm_ref, nv_ref):
        tot = p_ref[0]
        for d in range(1, 8):
            tot = tot + p_ref[d]
        wv = w_ref[...]
        l0 = wv[ROW_LB:ROW_LB + 1, :]
        l1 = wv[ROW_LB + 1:ROW_LB + 2, :]
        mx = jnp.maximum(l0, l1)
        e0 = jnp.exp(l0 - mx)
        e1 = jnp.exp(l1 - mx)
        lb = e0 / (e0 + e1)
        dl0 = tot[ROW_LB:ROW_LB + 1, :] * lb * (1.0 - lb)
        row = lax.broadcasted_iota(jnp.int32, tot.shape, 0)
        g = jnp.where(row == ROW_LB, dl0, jnp.where(row == ROW_LB + 1, -dl0, tot))
        g = jnp.where(row >= ROW_LOSS, 0.0, g)
        g_ref[...] = g
        d_ref[...], nm_ref[...], nv_ref[...] = _adamw_math(wv, g, m_ref[...], v_ref[...])
        loss_ref[...] = (0.5 / D_MODEL) * jnp.sum(tot[ROW_LOSS:ROW_LOSS + 1, :], axis=-1, keepdims=True)

    return pl.pallas_call(
        body, name="small_update",
        out_shape=[jax.ShapeDtypeStruct((1, 1), F32)] + [jax.ShapeDtypeStruct(w.shape, F32)] * 4,
        compiler_params=_params(),
    )(gathered, w, m, v)


def _small_rows(norm1, scale, lb, rec, final):
    pad = jnp.zeros((2, D_MODEL), F32)
    return jnp.concatenate([norm1, scale, lb, rec, final.reshape(1, D_MODEL), pad], axis=0)


def kernel(x, norm1_g, w_in, pool_w, pool_scale, lb_logits, rec_norm_g, w_out, final_norm_g, loss_target, m_norm1_g, m_w_in, m_pool_w, m_pool_scale, m_lb_logits, m_rec_norm_g, m_w_out, m_final_norm_g, v_norm1_g, v_w_in, v_pool_w, v_pool_scale, v_lb_logits, v_rec_norm_g, v_w_out, v_final_norm_g):
    pw_rows = N_GROUPS * PW_SHARD
    flat_pw = lambda a: a.reshape(pw_rows, PG)
    w_in_g, w_out_g, pw_g = _gather_weights(w_in[0], w_out[0], flat_pw(pool_w))
    pw_full = pw_g.reshape(N_SHARDS, N_GROUPS, PW_SHARD, PG).transpose(1, 0, 2, 3).reshape(N_GROUPS, PG, PG)
    loc = _local_step(x[0], loss_target[0], norm1_g, w_in_g, pw_full, pool_scale, lb_logits, rec_norm_g,
                      w_out_g.reshape(2 * D_MODEL, D_MODEL), final_norm_g.reshape(1, D_MODEL))

    zero = jnp.zeros((1, D_MODEL), F32)
    parts = jnp.concatenate([loc["part_x"][0:1], loc["dscale"], loc["part_rec"][1:2], zero,
                             loc["part_rec"][0:1], loc["part_out"][0:1], loc["part_out"][1:2], zero], axis=0)
    small_w = _small_rows(norm1_g, pool_scale, lb_logits, rec_norm_g, final_norm_g)
    small_m = _small_rows(m_norm1_g, m_pool_scale, m_lb_logits, m_rec_norm_g, m_final_norm_g)
    small_v = _small_rows(v_norm1_g, v_pool_scale, v_lb_logits, v_rec_norm_g, v_final_norm_g)
    loss, *small = _small_update(_gather_small(parts), small_w, small_m, small_v)

    xi, yi, ci = _place()
    place = jnp.stack([ci, 2 * xi + yi]).astype(jnp.int32)
    gpw = loc["gpw"].reshape(N_GROUPS, N_SHARDS, PW_SHARD, PG).transpose(1, 0, 2, 3).reshape(N_SHARDS, pw_rows, PG)
    grads = (loc["gw_in"], loc["gw_out"].reshape(N_SHARDS, W_OUT_SHARD, D_MODEL), gpw)
    tiles = (256, 256, 128)
    names = ("w_in", "w_out", "pool_w")
    recv = _swap_halves(grads)
    part = [_add_halves(g, r, place, t, "add_halves_" + n) for g, r, t, n in zip(grads, recv, tiles, names)]
    recv16 = _send_partials([p[1] for p in part])
    halves = [_sum_partials(p[0], r, place, t, "sum_partials_" + n)
              for p, r, t, n in zip(part, recv16, tiles, names)]
    g_in, g_out, g_pw = _join_halves(halves)
    big = []
    for w, g, m, v, name in ((w_in[0], g_in, m_w_in[0], v_w_in[0], "adamw_w_in"),
                             (w_out[0], g_out, m_w_out[0], v_w_out[0], "adamw_w_out"),
                             (flat_pw(pool_w), g_pw, flat_pw(m_pool_w), flat_pw(v_pool_w), "adamw_pool_w")):
        big.append((g,) + tuple(_adamw(w, g, m, v, 256, name)))

    def leaves(k):
        s = small[k]
        return (s[ROW_NORM1:ROW_NORM1 + 1], big[0][k][None], big[2][k].reshape(pool_w.shape),
                s[ROW_SCALE:ROW_SCALE + 1], s[ROW_LB:ROW_LB + 2], s[ROW_REC:ROW_REC + 1],
                big[1][k][None], s[ROW_FINAL])

    return (loss.reshape(()), loc["dx"][None], *leaves(0), *leaves(1), *leaves(2), *leaves(3))
```

```python
import functools

import numpy as np
import jax
import jax.numpy as jnp
from jax import lax
from jax.experimental import pallas as pl
from jax.experimental.pallas import tpu as pltpu

F32 = jnp.float32
BF16 = jnp.bfloat16

SEQ = 2048
D_MODEL = 1024
D_PROJ = 6144
N_SEC = 6
N_GROUPS = 4
PG = 256
N_HEADS = 8
HEAD = 128
CHUNK = 64
N_LEVELS = 6
N_SHARDS = 4
W_IN_SHARD = D_PROJ // N_SHARDS
W_OUT_SHARD = 2048 // N_SHARDS
PW_SHARD = PG // N_SHARDS
COL_BLK = 512
EPS = 1e-6

ADAM_LR = 0.001
ADAM_B1 = 0.9
ADAM_B2 = 0.999
ADAM_EPS = 1e-08
ADAM_WD = 0.01
ADAM_STEP = 10

V7X_VMEM_LIMIT = 56 * 1024 * 1024
MESH = pl.DeviceIdType.MESH


def _params(**kw):
    return pltpu.CompilerParams(vmem_limit_bytes=V7X_VMEM_LIMIT, **kw)


def _sig(x):
    return 1.0 / (1.0 + jnp.exp(-x))


def _dot(a, b):
    return jnp.dot(a, b, preferred_element_type=F32)


def _dot_nt(a, b):
    return lax.dot_general(a, b, (((1,), (1,)), ((), ())), preferred_element_type=F32)


def _dot_tn(a, b):
    return lax.dot_general(a, b, (((0,), (0,)), ((), ())), preferred_element_type=F32)


def _split3(a):
    p1 = a.astype(BF16)
    r1 = a - p1.astype(F32)
    p2 = r1.astype(BF16)
    p3 = (r1 - p2.astype(F32)).astype(BF16)
    return jnp.concatenate([p1, p2, p3], axis=-1)


def _dot3(w01, a):
    n = a.shape[-1]
    r = _dot(w01, _split3(a))
    return r[:, :n] + r[:, n:2 * n] + r[:, 2 * n:]


def _chunk_constants():
    j = np.arange(CHUNK)
    tt, ss = np.meshgrid(j, j, indexing="ij")
    x = tt ^ ss
    hb = np.full((CHUNK, CHUNK), -1, np.int32)
    for l in range(N_LEVELS):
        hb[x >= (1 << l)] = l
    lvl = np.where(tt > ss, hb, -1).astype(np.int32)
    tri = (ss <= tt).astype(np.float32)
    return tri, np.ascontiguousarray(tri.T), lvl, np.ascontiguousarray(lvl.T)


def _in_proj(x, g1, w_in_g):
    n_col = D_PROJ // COL_BLK
    per_shard = W_IN_SHARD // COL_BLK
    rows = 256

    def body(x_ref, g_ref, w_ref, proj_ref, h_ref):
        @pl.when(pl.program_id(0) == 0)
        def _():
            def norm(i, _):
                r = pl.ds(pl.multiple_of(i * rows, rows), rows)
                xv = x_ref[r, :]
                inv = lax.rsqrt(jnp.mean(xv * xv, axis=-1, keepdims=True) + EPS)
                h_ref[r, :] = (xv * inv * g_ref[...]).astype(BF16)
                return 0
            lax.fori_loop(0, SEQ // rows, norm, 0)

        def mm(i, _):
            r = pl.ds(pl.multiple_of(i * rows, rows), rows)
            proj_ref[r, :] = _dot(h_ref[r, :], w_ref[...])
            return 0
        lax.fori_loop(0, SEQ // rows, mm, 0)

    return pl.pallas_call(
        body, name="in_proj", grid=(n_col,),
        in_specs=[pl.BlockSpec((SEQ, D_MODEL), lambda j: (0, 0)),
                  pl.BlockSpec((1, D_MODEL), lambda j: (0, 0)),
                  pl.BlockSpec((None, D_MODEL, COL_BLK), lambda j: (j // per_shard, 0, j % per_shard))],
        out_specs=[pl.BlockSpec((SEQ, COL_BLK), lambda j: (0, j)),
                   pl.BlockSpec((SEQ, D_MODEL), lambda j: (0, 0))],
        out_shape=[jax.ShapeDtypeStruct((SEQ, D_PROJ), F32),
                   jax.ShapeDtypeStruct((SEQ, D_MODEL), BF16)],
        compiler_params=_params(dimension_semantics=("arbitrary",)),
    )(x, g1, w_in_g)


POOL_ROWS = 256
POOL_HALO = 16


def _window_sums(ext, g, shift_of):
    s = ext
    for k in range(N_GROUPS):
        s = jnp.where(k <= g, s + pltpu.roll(s, shift_of(k), 0), s)
    return s


def _pool_diff(u_ref, i, g):
    n = POOL_ROWS + POOL_HALO
    r0 = i * POOL_ROWS
    cur = u_ref[pl.ds(pl.multiple_of(r0, POOL_ROWS), POOL_ROWS), :]
    before = u_ref[pl.ds(pl.multiple_of(jnp.maximum(r0 - POOL_HALO, 0), 8), POOL_HALO), :]
    before = jnp.where(i > 0, before, 0.0)
    ext = jnp.concatenate([before, cur], axis=0)
    s = _window_sums(ext, g, lambda k: 1 << k)[POOL_HALO:, :]
    t = r0 + lax.broadcasted_iota(jnp.int32, (POOL_ROWS, 1), 0)
    width = (2 << g).astype(F32)
    inv_count = 1.0 / jnp.minimum((t + 1).astype(F32), width)
    return s * inv_count - cur, inv_count


def _pool_fwd(proj, pw_g, pool_scale):
    def body(u_ref, gate_ref, pw_ref, sc_ref, y_ref):
        g = pl.program_id(0)

        def step(i, _):
            r = pl.ds(pl.multiple_of(i * POOL_ROWS, POOL_ROWS), POOL_ROWS)
            d, _ = _pool_diff(u_ref, i, g)
            mixed = _dot(d.astype(BF16), pw_ref[...])
            gate = gate_ref[r, :]
            y_ref[r, :] = (mixed * sc_ref[...] * (gate * _sig(gate))).astype(BF16)
            return 0
        lax.fori_loop(0, SEQ // POOL_ROWS, step, 0)

    return pl.pallas_call(
        body, name="pool_fwd", grid=(N_GROUPS,),
        in_specs=[pl.BlockSpec((SEQ, PG), lambda g: (0, g)),
                  pl.BlockSpec((SEQ, PG), lambda g: (0, N_GROUPS + g)),
                  pl.BlockSpec((None, PG, PG), lambda g: (g, 0, 0)),
                  pl.BlockSpec((1, PG), lambda g: (0, g))],
        out_specs=pl.BlockSpec((SEQ, PG), lambda g: (0, g)),
        out_shape=jax.ShapeDtypeStruct((SEQ, D_MODEL), BF16),
        compiler_params=_params(dimension_semantics=("arbitrary",)),
    )(proj, proj, pw_g, pool_scale)


REC_ROWS = 512
REC_CHUNKS = REC_ROWS // CHUNK
N_REC_BLK = SEQ // REC_ROWS
SEC_BLK = D_MODEL // HEAD


def _lower_bound(lb_ref):
    l0 = lb_ref[0:1, :]
    l1 = lb_ref[1:2, :]
    mx = jnp.maximum(l0, l1)
    e0 = jnp.exp(l0 - mx)
    e1 = jnp.exp(l1 - mx)
    return e0 / (e0 + e1)


def _gates(q, fl, lb):
    qs = q * _sig(q)
    sf = _sig(fl)
    f = lb + (1.0 - lb) * sf
    return qs, sf, f, 1.0 - f, jnp.log(f)


def _level_factors(big_g, qs, k):
    t = lax.broadcasted_iota(jnp.int32, (CHUNK, HEAD), 0)
    row = lambda r, n: jnp.broadcast_to(big_g[r:r + 1, :], (n, HEAD))
    out = []
    for l in range(N_LEVELS):
        m = 1 << l
        if l == 0:
            g_mid = jnp.where((t & 1) == 1, pltpu.roll(big_g, 1, 0), big_g)
        elif l == 1:
            low = (t & 7) < 4
            g_mid = jnp.concatenate([jnp.where(low[:8], row(8 * i + 1, 8), row(8 * i + 5, 8))
                                     for i in range(CHUNK // 8)], axis=0)
        else:
            g_mid = jnp.concatenate([row(b * 2 * m + m - 1, 2 * m) for b in range(CHUNK // (2 * m))], axis=0)
        up = ((t >> l) & 1) == 1
        e = jnp.exp(jnp.where(up, big_g - g_mid, g_mid - big_g))
        x = jnp.where(up, qs, k) * e
        hi = x.astype(BF16)
        out.append((hi, (x - hi.astype(F32)).astype(BF16), e, up))
    return out


CHIP_FLIPS = ((1, 0), (0, 1), (1, 1))
HBM = pl.BlockSpec(memory_space=pl.ANY)


def _place():
    return lax.axis_index("x"), lax.axis_index("y"), lax.axis_index("c")


def _remote(src, dst, send_sems, recv_sems, k, to):
    return pltpu.make_async_remote_copy(src_ref=src, dst_ref=dst, send_sem=send_sems.at[k],
                                        recv_sem=recv_sems.at[k], device_id=to, device_id_type=MESH)


def _half_rows(ref, c):
    half = ref.shape[-2] // 2
    rows = pl.ds(pl.multiple_of(c * half, half), half)
    return ref.at[:, rows, :] if len(ref.shape) == 3 else ref.at[rows, :]


class _Exchange:
    def __init__(self, inputs, out_shapes, n_sems, start, finish, aliases=None):
        self.inputs, self.out_shapes, self.n_sems = list(inputs), list(out_shapes), n_sems
        self.start, self.finish, self.aliases = start, finish, dict(aliases or {})


def _ex_swap(grads):
    def copies(ins, outs, send, recv):
        x, y, c = _place()
        return [_remote(_half_rows(g, 1 - c), o, send, recv, t, (x, y, 1 - c))
                for t, (g, o) in enumerate(zip(ins, outs))]

    def start(*refs):
        for cp in copies(*refs):
            cp.start()

    def finish(*refs):
        cps = copies(*refs)
        for cp in cps:
            cp.wait_recv()
        for cp in cps:
            cp.wait_send()

    shapes = [jax.ShapeDtypeStruct((a.shape[0], a.shape[1] // 2, a.shape[2]), F32) for a in grads]
    return _Exchange(grads, shapes, len(grads), start, finish)


def _ex_send(parts16, owners):
    def each(ins, outs, send, recv, to_sender, to_owner):
        x, y, c = _place()
        k = 0
        for t, own in enumerate(owners):
            for j, o in enumerate(own):
                for r, (fx, fy) in enumerate(CHIP_FLIPS):
                    tx, ty = x ^ fx, y ^ fy
                    cp = _remote(ins[t].at[j], outs[t].at[j, r], send, recv, k, (tx, ty, c))
                    if to_sender is not None:
                        pl.when(2 * tx + ty == o)(functools.partial(to_sender, cp))
                    if to_owner is not None:
                        pl.when(2 * x + y == o)(functools.partial(to_owner, cp))
                    k += 1

    def start(*refs):
        each(*refs, lambda cp: cp.start(), None)

    def finish(*refs):
        each(*refs, None, lambda cp: cp.wait_recv())
        each(*refs, lambda cp: cp.wait_send(), None)

    shapes = [jax.ShapeDtypeStruct((a.shape[0], len(CHIP_FLIPS)) + a.shape[1:], BF16) for a in parts16]
    return _Exchange(parts16, shapes, len(CHIP_FLIPS) * sum(len(o) for o in owners), start, finish)


def _ex_join(units, owners):
    def each(ins, outs, send, recv, fn):
        x, y, c = _place()
        k = 0
        for t, own in enumerate(owners):
            for j, o in enumerate(own):
                def half(cc, to, u=outs[t].at[j], k=k):
                    return _remote(_half_rows(u, cc), _half_rows(u, cc), send, recv, k, to)
                mine = functools.partial(half, c, (x, y, 1 - c))
                theirs = functools.partial(half, 1 - c, (x, y, c))
                pl.when(2 * x + y == o)(functools.partial(fn, mine, theirs))
                k += 1

    def start(*refs):
        each(*refs, lambda mine, theirs: mine().start())

    def finish(*refs):
        each(*refs, lambda mine, theirs: theirs().wait_recv())
        each(*refs, lambda mine, theirs: mine().wait_send())

    shapes = [jax.ShapeDtypeStruct(a.shape, F32) for a in units]
    return _Exchange(units, shapes, sum(len(o) for o in owners), start, finish,
                     aliases={t: t for t in range(len(units))})


def _ex_gather(slots):
    n_t = len(slots)
    n_fl = len(CHIP_FLIPS)

    def piece(ref, shard, half):
        return _half_rows(ref.at[shard], half)

    def first(outs, send, recv):
        x, y, c = _place()
        s = 2 * x + y
        return [_remote(piece(outs[t], s, c), piece(outs[t], s, c), send, recv, n_t * j + t, (x ^ fx, y ^ fy, c))
                for j, (fx, fy) in enumerate(CHIP_FLIPS) for t in range(n_t)]

    def start(ins, outs, send, recv):
        for cp in first(outs, send, recv):
            cp.start()

    def finish(ins, outs, send, recv):
        x, y, c = _place()
        passed = []
        for j, (fx, fy) in enumerate(CHIP_FLIPS):
            sj = 2 * (x ^ fx) + (y ^ fy)
            for t in range(n_t):
                k = n_t * j + t
                _remote(piece(outs[t], sj, c), piece(outs[t], sj, c), send, recv, k, (x, y, c)).wait_recv()
                cp = _remote(piece(outs[t], sj, c), piece(outs[t], sj, c), send, recv, n_t * n_fl + k, (x, y, 1 - c))
                cp.start()
                passed.append(cp)
        for j, (fx, fy) in enumerate(CHIP_FLIPS):
            sj = 2 * (x ^ fx) + (y ^ fy)
            for t in range(n_t):
                k = n_t * n_fl + n_t * j + t
                _remote(piece(outs[t], sj, 1 - c), piece(outs[t], sj, 1 - c), send, recv, k, (x, y, c)).wait_recv()
        for cp in first(outs, send, recv) + passed:
            cp.wait_send()

    shapes = [jax.ShapeDtypeStruct(a.shape, BF16) for a in slots]
    return _Exchange(slots, shapes, 2 * n_t * n_fl, start, finish, aliases={t: t for t in range(n_t)})


def _call(body, *, name, args=(), in_specs=(), out_specs=(), out_shape=(), grid=(), scratch_shapes=(),
          exchanges=()):
    n_in, n_out, n_scr = len(args), len(out_shape), len(scratch_shapes)
    ex_in, ex_out, ex_scr, spans, alias = [], [], [], [], {}
    for ex in exchanges:
        spans.append((len(ex_in), len(ex.inputs), len(ex_out), len(ex.out_shapes)))
        for i, o in ex.aliases.items():
            alias[n_in + len(ex_in) + i] = n_out + len(ex_out) + o
        ex_in += ex.inputs
        ex_out += ex.out_shapes
        ex_scr += [pltpu.SemaphoreType.DMA((ex.n_sems,)), pltpu.SemaphoreType.DMA((ex.n_sems,))]

    def full(*refs):
        ins, x_in = refs[:n_in], refs[n_in:n_in + len(ex_in)]
        outs = refs[n_in + len(ex_in):n_in + len(ex_in) + n_out]
        x_out = refs[n_in + len(ex_in) + n_out:n_in + len(ex_in) + n_out + len(ex_out)]
        scr = refs[len(refs) - n_scr - len(ex_scr):len(refs) - len(ex_scr)]
        sems = refs[len(refs) - len(ex_scr):]

        def run(which):
            for e, (ex, (i0, ni, o0, no)) in enumerate(zip(exchanges, spans)):
                getattr(ex, which)(x_in[i0:i0 + ni], x_out[o0:o0 + no], sems[2 * e], sems[2 * e + 1])

        if grid:
            ids = [pl.program_id(a) for a in range(len(grid))]
            is_first = functools.reduce(jnp.logical_and, [i == 0 for i in ids])
            is_last = functools.reduce(jnp.logical_and, [i == g - 1 for i, g in zip(ids, grid)])
            pl.when(is_first)(lambda: run("start"))
            body(*ins, *outs, *scr)
            pl.when(is_last)(lambda: run("finish"))
        else:
            run("start")
            if body is not None:
                body(*ins, *outs, *scr)
            run("finish")

    kw = dict(grid=grid) if grid else {}
    if grid:
        kw["compiler_params"] = _params(dimension_semantics=("arbitrary",) * len(grid))
    else:
        kw["compiler_params"] = _params()
    res = pl.pallas_call(
        full, name=name,
        in_specs=list(in_specs) + [HBM] * len(ex_in),
        out_specs=list(out_specs) + [HBM] * len(ex_out),
        out_shape=list(out_shape) + ex_out,
        scratch_shapes=list(scratch_shapes) + ex_scr,
        input_output_aliases=alias, **kw,
    )(*args, *ex_in)
    own = list(res[:n_out])
    per_ex = [list(res[n_out + o0:n_out + o0 + no]) for (_, _, o0, no) in spans]
    return own, per_ex


def _cast_own(w, place, name):
    rows, cols = w.shape
    tile = min(rows, 256)

    def body(place_ref, w_ref, o_ref):
        o_ref[...] = w_ref[...].astype(BF16)

    return pl.pallas_call(
        body, name=name,
        grid_spec=pltpu.PrefetchScalarGridSpec(
            num_scalar_prefetch=1, grid=(rows // tile,),
            in_specs=[pl.BlockSpec((tile, cols), lambda i, p: (i, 0))],
            out_specs=pl.BlockSpec((None, tile, cols), lambda i, p: (p[1], i, 0))),
        out_shape=jax.ShapeDtypeStruct((N_SHARDS, rows, cols), BF16),
        compiler_params=_params(dimension_semantics=("arbitrary",)),
    )(place, w)


def _gather_small(parts):
    def body(p_ref, out_ref, send_sems, recv_sems, local_sem):
        x, y, c = _place()
        me = 4 * x + 2 * y + c
        mine = pltpu.make_async_copy(p_ref, out_ref.at[me], local_sem)
        mine.start()
        sends = []
        for mask in range(1, 8):
            to = (x ^ (mask >> 2), y ^ ((mask >> 1) & 1), c ^ (mask & 1))
            cp = _remote(p_ref, out_ref.at[me], send_sems, recv_sems, mask - 1, to)
            cp.start()
            sends.append(cp)
        for mask in range(1, 8):
            frm = me ^ mask
            _remote(p_ref, out_ref.at[frm], send_sems, recv_sems, mask - 1, (x, y, c)).wait_recv()
        for cp in sends:
            cp.wait_send()
        mine.wait()

    vmem = pl.BlockSpec(memory_space=pltpu.VMEM)
    return pl.pallas_call(
        body, name="gather_small", in_specs=[vmem], out_specs=vmem,
        out_shape=jax.ShapeDtypeStruct((8,) + parts.shape, F32),
        scratch_shapes=[pltpu.SemaphoreType.DMA((7,)), pltpu.SemaphoreType.DMA((7,)),
                        pltpu.SemaphoreType.DMA],
        compiler_params=_params(),
    )(parts)


def _rec_fwd(proj, lb_logits, rec_g, tri, lvl, exchanges):
    def body(q_ref, f_ref, i_ref, rg_ref, lb_ref, g_ref, w_ref, lvl_ref, y_ref, o_ref, stp_ref, st_ref):
        @pl.when(pl.program_id(1) == 0)
        def _():
            st_ref[...] = jnp.zeros_like(st_ref)
        lb = _lower_bound(lb_ref)
        st = st_ref[...]
        for c in range(REC_CHUNKS):
            r = pl.ds(c * CHUNK, CHUNK)
            v = i_ref[r, :]
            rg = rg_ref[r, :]
            qs, _, _, k, g = _gates(q_ref[r, :], f_ref[r, :], lb)
            big_g = _dot3(w_ref[...], g)
            a = jnp.zeros((CHUNK, CHUNK), F32)
            lv = lvl_ref[...]
            for l, (xl, _, _, _) in enumerate(_level_factors(big_g, qs, k)):
                a = a + jnp.where(lv == l, _dot_nt(xl, xl), 0.0)
            stp_ref[c] = st
            vb = v.astype(BF16)
            diag = jnp.sum(qs * k, axis=-1, keepdims=True)
            o = (_dot(a.astype(BF16), vb) + diag * v
                 + _dot_nt((qs * jnp.exp(big_g)).astype(BF16), st.astype(BF16)))
            g_last = big_g[CHUNK - 1:CHUNK, :]
            kdec = (k * jnp.exp(g_last - big_g)).astype(BF16)
            st = st * jnp.exp(g_last) + _dot_tn(vb, kdec)
            o_ref[r, :] = o
            inv = lax.rsqrt(jnp.mean(o * o, axis=-1, keepdims=True) + EPS)
            y_ref[r, :] = (o * inv * g_ref[...] * (rg * _sig(rg))).astype(BF16)
        st_ref[...] = st

    sec = lambda n: pl.BlockSpec((REC_ROWS, HEAD), lambda h, b: (b, n * SEC_BLK + h))
    vec = lambda rows: pl.BlockSpec((rows, HEAD), lambda h, b: (0, h))
    full = lambda a: pl.BlockSpec(a.shape, lambda h, b: (0,) * a.ndim)
    return _call(
        body, name="rec_fwd", grid=(N_HEADS, N_REC_BLK),
        args=(proj, proj, proj, proj, lb_logits, rec_g, tri, lvl),
        in_specs=[sec(2), sec(3), sec(4), sec(5), vec(2), vec(1), full(tri), full(lvl)],
        out_specs=[pl.BlockSpec((REC_ROWS, HEAD), lambda h, b: (b, h)),
                   pl.BlockSpec((REC_ROWS, HEAD), lambda h, b: (b, h)),
                   pl.BlockSpec((None, REC_CHUNKS, HEAD, HEAD), lambda h, b: (h, b, 0, 0))],
        out_shape=[jax.ShapeDtypeStruct((SEQ, D_MODEL), BF16),
                   jax.ShapeDtypeStruct((SEQ, D_MODEL), F32),
                   jax.ShapeDtypeStruct((N_HEADS, SEQ // CHUNK, HEAD, HEAD), F32)],
        scratch_shapes=[pltpu.VMEM((HEAD, HEAD), F32)],
        exchanges=exchanges)


OUT_ROWS = 256


def _out_proj_loss(y_pool, y_rec, w_out_g, x, target, gf):
    def body(yp_ref, yr_ref, w_ref, x_ref, t_ref, gf_ref, dout_ref, doutb_ref, part_ref):
        @pl.when(pl.program_id(0) == 0)
        def _():
            part_ref[...] = jnp.zeros_like(part_ref)
        out = (x_ref[...] + _dot(yp_ref[...], w_ref[0:D_MODEL, :])
               + _dot(yr_ref[...], w_ref[D_MODEL:2 * D_MODEL, :]))
        inv = lax.rsqrt(jnp.mean(out * out, axis=-1, keepdims=True) + EPS)
        gf_v = gf_ref[...]
        diff = out * inv * gf_v - t_ref[...]
        dyf = diff * (1.0 / D_MODEL)
        a = dyf * gf_v
        dout = inv * a - out * (inv * inv * inv) * jnp.mean(a * out, axis=-1, keepdims=True)
        dout_ref[...] = dout
        doutb_ref[...] = dout.astype(BF16)
        part_ref[0:1, :] += jnp.sum(dyf * out * inv, axis=0, keepdims=True)
        part_ref[1:2, :] += jnp.sum(diff * diff, axis=0, keepdims=True)

    row = lambda n: pl.BlockSpec((OUT_ROWS, n), lambda i: (i, 0))
    return pl.pallas_call(
        body, name="out_proj_loss", grid=(SEQ // OUT_ROWS,),
        in_specs=[row(D_MODEL), row(D_MODEL), pl.BlockSpec((2 * D_MODEL, D_MODEL), lambda i: (0, 0)),
                  row(D_MODEL), row(D_MODEL), pl.BlockSpec((1, D_MODEL), lambda i: (0, 0))],
        out_specs=[row(D_MODEL), row(D_MODEL), pl.BlockSpec((8, D_MODEL), lambda i: (0, 0))],
        out_shape=[jax.ShapeDtypeStruct((SEQ, D_MODEL), F32),
                   jax.ShapeDtypeStruct((SEQ, D_MODEL), BF16),
                   jax.ShapeDtypeStruct((8, D_MODEL), F32)],
        compiler_params=_params(dimension_semantics=("arbitrary",)),
    )(y_pool, y_rec, w_out_g, x, target, gf)


def _grad_w_out(y_pool, y_rec, dout_b):
    blk = 256
    per = D_MODEL // blk

    def body(yp_ref, yr_ref, d_ref, o_ref):
        j = pl.program_id(0)

        @pl.when(j < per)
        def _():
            o_ref[...] = _dot_tn(yp_ref[...], d_ref[...])

        @pl.when(j >= per)
        def _():
            o_ref[...] = _dot_tn(yr_ref[...], d_ref[...])

    return pl.pallas_call(
        body, name="grad_w_out", grid=(2 * per,),
        in_specs=[pl.BlockSpec((SEQ, blk), lambda j: (0, jnp.minimum(j, per - 1))),
                  pl.BlockSpec((SEQ, blk), lambda j: (0, jnp.maximum(j - per, 0))),
                  pl.BlockSpec((SEQ, D_MODEL), lambda j: (0, 0))],
        out_specs=pl.BlockSpec((blk, D_MODEL), lambda j: (j, 0)),
        out_shape=jax.ShapeDtypeStruct((2 * D_MODEL, D_MODEL), F32),
        compiler_params=_params(dimension_semantics=("arbitrary",)),
    )(y_pool, y_rec, dout_b).reshape(N_SHARDS, W_OUT_SHARD, D_MODEL)


def _pool_bwd(proj, dout_b, w_out_g, pw_g, pool_scale, exchanges):
    n = POOL_ROWS + POOL_HALO

    def body(u_ref, gate_ref, d_ref, wo_ref, pw_ref, sc_ref,
             du_ref, dgate_ref, dpw_ref, dsc_ref, dd_ref, ddw_ref):
        g = pl.program_id(0)
        dpw_ref[...] = jnp.zeros_like(dpw_ref)
        dsc_ref[...] = jnp.zeros_like(dsc_ref)

        def first(i, _):
            r = pl.ds(pl.multiple_of(i * POOL_ROWS, POOL_ROWS), POOL_ROWS)
            d, inv_count = _pool_diff(u_ref, i, g)
            db = d.astype(BF16)
            mixed = _dot(db, pw_ref[...])
            gate = gate_ref[r, :]
            sg = _sig(gate)
            silu = gate * sg
            dy = _dot_nt(d_ref[r, :], wo_ref[...])
            sc = sc_ref[...]
            dmixed = dy * silu * sc
            dgate_ref[r, :] = (dy * mixed * sc * (sg * (1.0 + gate * (1.0 - sg)))).astype(BF16)
            dsc_ref[...] += jnp.sum(dy * silu * mixed, axis=0, keepdims=True)
            dmb = dmixed.astype(BF16)
            dpw_ref[...] += _dot_tn(db, dmb)
            dd = _dot_nt(dmb, pw_ref[...])
            dd_ref[r, :] = dd
            ddw_ref[r, :] = dd * inv_count
            return 0
        lax.fori_loop(0, SEQ // POOL_ROWS, first, 0)

        def second(i, _):
            r0 = i * POOL_ROWS
            r = pl.ds(pl.multiple_of(r0, POOL_ROWS), POOL_ROWS)
            last = i == SEQ // POOL_ROWS - 1
            after = ddw_ref[pl.ds(pl.multiple_of(jnp.minimum(r0 + POOL_ROWS, SEQ - POOL_HALO), 8), POOL_HALO), :]
            after = jnp.where(last, 0.0, after)
            ext = jnp.concatenate([ddw_ref[r, :], after], axis=0)
            s = _window_sums(ext, g, lambda k: n - (1 << k))[:POOL_ROWS, :]
            du_ref[r, :] = (s - dd_ref[r, :]).astype(BF16)
            return 0
        lax.fori_loop(0, SEQ // POOL_ROWS, second, 0)

    return _call(
        body, name="pool_bwd", grid=(N_GROUPS,),
        args=(proj, proj, dout_b, w_out_g, pw_g, pool_scale),
        in_specs=[pl.BlockSpec((SEQ, PG), lambda g: (0, g)),
                  pl.BlockSpec((SEQ, PG), lambda g: (0, N_GROUPS + g)),
                  pl.BlockSpec((SEQ, D_MODEL), lambda g: (0, 0)),
                  pl.BlockSpec((PG, D_MODEL), lambda g: (g, 0)),
                  pl.BlockSpec((None, PG, PG), lambda g: (g, 0, 0)),
                  pl.BlockSpec((1, PG), lambda g: (0, g))],
        out_specs=[pl.BlockSpec((SEQ, PG), lambda g: (0, g)),
                   pl.BlockSpec((SEQ, PG), lambda g: (0, g)),
                   pl.BlockSpec((None, PG, PG), lambda g: (g, 0, 0)),
                   pl.BlockSpec((1, PG), lambda g: (0, g))],
        out_shape=[jax.ShapeDtypeStruct((SEQ, D_MODEL), BF16),
                   jax.ShapeDtypeStruct((SEQ, D_MODEL), BF16),
                   jax.ShapeDtypeStruct((N_GROUPS, PG, PG), F32),
                   jax.ShapeDtypeStruct((1, D_MODEL), F32)],
        scratch_shapes=[pltpu.VMEM((SEQ, PG), F32), pltpu.VMEM((SEQ, PG), F32)],
        exchanges=exchanges)


HALF_HEADS = N_HEADS // 2
HALF_COLS = HALF_HEADS * HEAD


def _rec_bwd(proj, o_raw, st_prev, dout_b, w_out_g, lb_logits, rec_g, tri, tri_t, lvl, lvl_t, h0, name, exchanges):
    def body(q_ref, f_ref, i_ref, rg_ref, o_ref, stp_ref, d_ref, wo_ref, lb_ref, g_ref,
             w_ref, lvl_ref, lvlt_ref, tri_ref,
             dq_ref, df_ref, di_ref, drg_ref, part_ref, dst_ref):
        @pl.when(pl.program_id(1) == 0)
        def _():
            dst_ref[...] = jnp.zeros_like(dst_ref)
            part_ref[...] = jnp.zeros_like(part_ref)
        lb = _lower_bound(lb_ref)
        grec = g_ref[...]
        dst = dst_ref[...]
        acc_grec = jnp.zeros((1, HEAD), F32)
        acc_lb = jnp.zeros((1, HEAD), F32)
        for c in reversed(range(REC_CHUNKS)):
            r = pl.ds(c * CHUNK, CHUNK)
            q = q_ref[r, :]
            v = i_ref[r, :]
            rg = rg_ref[r, :]
            o = o_ref[r, :]
            dy = _dot_nt(d_ref[r, :], wo_ref[...])
            sg = _sig(rg)
            silu = rg * sg
            inv = lax.rsqrt(jnp.mean(o * o, axis=-1, keepdims=True) + EPS)
            recn = o * inv
            drg_ref[r, :] = (dy * recn * grec * (sg * (1.0 + rg * (1.0 - sg)))).astype(BF16)
            acc_grec = acc_grec + jnp.sum(dy * silu * recn, axis=0, keepdims=True)
            drecn = dy * silu * grec
            do = inv * drecn - o * (inv * inv * inv) * jnp.mean(drecn * o, axis=-1, keepdims=True)
            qs, sf, f, k, g = _gates(q, f_ref[r, :], lb)
            big_g = _dot3(w_ref[...], g)
            g_last = big_g[CHUNK - 1:CHUNK, :]
            e_g = jnp.exp(big_g)
            e_rev = jnp.exp(g_last - big_g)
            e_last = jnp.exp(g_last)
            levels = _level_factors(big_g, qs, k)
            lv = lvl_ref[...]
            lvt = lvlt_ref[...]
            a_t = jnp.zeros((CHUNK, CHUNK), F32)
            for l, (xl, _, _, _) in enumerate(levels):
                a_t = a_t + jnp.where(lvt == l, _dot_nt(xl, xl), 0.0)
            dob = do.astype(BF16)
            vb = v.astype(BF16)
            d_a = _dot_nt(dob, vb)
            d_at = _dot_nt(vb, dob)
            stp = stp_ref[c]
            dstb = dst.astype(BF16)
            q_g = qs * e_g
            kdec = k * e_rev
            diag = jnp.sum(qs * k, axis=-1, keepdims=True)
            dv = _dot(a_t.astype(BF16), dob) + diag * do + _dot_nt(kdec.astype(BF16), dstb)
            dq_g = _dot(dob, stp.astype(BF16))
            dkdec = _dot(vb, dstb)
            de_last = jnp.sum(stp * dst, axis=0, keepdims=True)
            dst = dst * e_last + _dot_tn(dob, q_g.astype(BF16))
            dqs_i = jnp.zeros((CHUNK, HEAD), F32)
            dk_i = jnp.zeros((CHUNK, HEAD), F32)
            for l, (xl, xlo, e, up) in enumerate(levels):
                z = jnp.where(lv == l, d_a, jnp.where(lvt == l, d_at, 0.0))
                tmp = _dot(z.astype(BF16), jnp.concatenate([xl, xlo], axis=-1))
                tmp = (tmp[:, :HEAD] + tmp[:, HEAD:]) * e
                dqs_i = dqs_i + jnp.where(up, tmp, 0.0)
                dk_i = dk_i + jnp.where(up, 0.0, tmp)
            ddiag = jnp.sum(do * v, axis=-1, keepdims=True)
            dqs = dqs_i + ddiag * k + dq_g * e_g
            dk = dk_i + ddiag * qs + dkdec * e_rev
            dg_rev = dkdec * kdec
            dg_last = jnp.sum(dg_rev, axis=0, keepdims=True) + de_last * e_last
            dbig_g = qs * dqs_i - k * dk_i + dq_g * q_g - dg_rev
            dg = _dot3(tri_ref[...], dbig_g) + dg_last
            df = dg / f - dk
            df_ref[r, :] = (df * (1.0 - lb) * sf * (1.0 - sf)).astype(BF16)
            acc_lb = acc_lb + jnp.sum(df * (1.0 - sf), axis=0, keepdims=True)
            sq = _sig(q)
            dq_ref[r, :] = (dqs * (sq * (1.0 + q * (1.0 - sq)))).astype(BF16)
            di_ref[r, :] = dv.astype(BF16)
        dst_ref[...] = dst
        part_ref[0:1, :] += acc_grec
        part_ref[1:2, :] += acc_lb

    rev = lambda b: N_REC_BLK - 1 - b
    sec = lambda n: pl.BlockSpec((REC_ROWS, HEAD), lambda h, b: (rev(b), n * SEC_BLK + h0 + h))
    col_in = pl.BlockSpec((REC_ROWS, HEAD), lambda h, b: (rev(b), h0 + h))
    col = pl.BlockSpec((REC_ROWS, HEAD), lambda h, b: (rev(b), h))
    vec_in = lambda rows: pl.BlockSpec((rows, HEAD), lambda h, b: (0, h0 + h))
    full = lambda a: pl.BlockSpec(a.shape, lambda h, b: (0,) * a.ndim)
    return _call(
        body, name=name, grid=(HALF_HEADS, N_REC_BLK),
        args=(proj, proj, proj, proj, o_raw, st_prev, dout_b, w_out_g, lb_logits, rec_g, tri, lvl, lvl_t, tri_t),
        in_specs=[sec(2), sec(3), sec(4), sec(5), col_in,
                  pl.BlockSpec((None, REC_CHUNKS, HEAD, HEAD), lambda h, b: (h0 + h, rev(b), 0, 0)),
                  pl.BlockSpec((REC_ROWS, D_MODEL), lambda h, b: (rev(b), 0)),
                  pl.BlockSpec((HEAD, D_MODEL), lambda h, b: (SEC_BLK + h0 + h, 0)),
                  vec_in(2), vec_in(1), full(tri), full(lvl), full(lvl_t), full(tri_t)],
        out_specs=[col, col, col, col, pl.BlockSpec((8, HEAD), lambda h, b: (0, h))],
        out_shape=[jax.ShapeDtypeStruct((SEQ, HALF_COLS), BF16)] * 4
                  + [jax.ShapeDtypeStruct((8, HALF_COLS), F32)],
        scratch_shapes=[pltpu.VMEM((HEAD, HEAD), F32)],
        exchanges=exchanges)


def _w_in_block(w_ref, j):
    per_shard = W_IN_SHARD // COL_BLK
    return w_ref[j // per_shard, :, (j % per_shard) * COL_BLK:(j % per_shard + 1) * COL_BLK]


def _grad_x(dproj, w_in_g, x, g1, dout, exchanges):
    rows = 256
    n_blk = len(dproj)

    def body(*refs):
        dp_refs = refs[:n_blk]
        w_ref, x_ref, g_ref, dout_ref, dx_ref, part_ref = refs[n_blk:]

        @pl.when(pl.program_id(0) == 0)
        def _():
            part_ref[...] = jnp.zeros_like(part_ref)
        dh = jnp.zeros((rows, D_MODEL), F32)
        for j in range(n_blk):
            dh = dh + _dot_nt(dp_refs[j][...], _w_in_block(w_ref, j))
        xv = x_ref[...]
        inv = lax.rsqrt(jnp.mean(xv * xv, axis=-1, keepdims=True) + EPS)
        a = dh * g_ref[...]
        dx_ref[...] = (dout_ref[...] + inv * a
                       - xv * (inv * inv * inv) * jnp.mean(a * xv, axis=-1, keepdims=True))
        part_ref[0:1, :] += jnp.sum(dh * xv * inv, axis=0, keepdims=True)

    row = lambda: pl.BlockSpec((rows, D_MODEL), lambda i: (i, 0))
    dp_spec = lambda cb: pl.BlockSpec((rows, COL_BLK), lambda i: (i, cb))
    return _call(
        body, name="grad_x", grid=(SEQ // rows,),
        args=tuple(a for a, _ in dproj) + (w_in_g, x, g1, dout),
        in_specs=[dp_spec(cb) for _, cb in dproj]
                 + [pl.BlockSpec((N_SHARDS, D_MODEL, W_IN_SHARD), lambda i: (0, 0, 0)),
                    row(), pl.BlockSpec((1, D_MODEL), lambda i: (0, 0)), row()],
        out_specs=[row(), pl.BlockSpec((8, D_MODEL), lambda i: (0, 0))],
        out_shape=[jax.ShapeDtypeStruct((SEQ, D_MODEL), F32),
                   jax.ShapeDtypeStruct((8, D_MODEL), F32)],
        exchanges=exchanges)


def _grad_w_in(h, blocks, name):
    n_blk = len(blocks)

    def body(h_ref, *refs):
        dp_refs, o_ref = refs[:n_blk], refs[n_blk]
        j = pl.program_id(0)
        for i in range(n_blk):
            @pl.when(j == i)
            def _(i=i):
                o_ref[...] = _dot_tn(h_ref[...], dp_refs[i][...])

    return pl.pallas_call(
        body, name=name, grid=(n_blk,),
        in_specs=[pl.BlockSpec((SEQ, D_MODEL), lambda j: (0, 0))]
                 + [pl.BlockSpec((SEQ, COL_BLK), lambda j, cb=cb: (0, cb)) for _, cb in blocks],
        out_specs=pl.BlockSpec((None, D_MODEL, COL_BLK), lambda j: (j, 0, 0)),
        out_shape=jax.ShapeDtypeStruct((n_blk, D_MODEL, COL_BLK), F32),
        compiler_params=_params(dimension_semantics=("arbitrary",)),
    )(h, *[a for a, _ in blocks])


def _add_units(grad, recv, place, tile, name):
    n, rows, cols = grad.shape
    per_half = rows // 2 // tile

    def body(place_ref, g_ref, r_ref, o32_ref, o16_ref):
        v = g_ref[...] + r_ref[...]
        o32_ref[...] = v
        o16_ref[...] = v.astype(BF16)

    blk = lambda f: pl.BlockSpec((None, tile, cols), f)
    out = lambda s, i, p: (s, i, 0)
    return pl.pallas_call(
        body, name=name,
        grid_spec=pltpu.PrefetchScalarGridSpec(
            num_scalar_prefetch=1, grid=(n, per_half),
            in_specs=[blk(lambda s, i, p: (s, p[0] * per_half + i, 0)), blk(out)],
            out_specs=[blk(out), blk(out)]),
        out_shape=[jax.ShapeDtypeStruct(recv.shape, F32), jax.ShapeDtypeStruct(recv.shape, BF16)],
        compiler_params=_params(dimension_semantics=("arbitrary", "arbitrary")),
    )(place, grad, recv)


def _sum_units(part32, recv16, place, tile, name):
    n, half, cols = part32.shape
    per_half = half // tile

    def body(place_ref, p_ref, r_ref, o_ref):
        acc = p_ref[...]
        for j in range(len(CHIP_FLIPS)):
            acc = acc + r_ref[j].astype(F32)
        o_ref[...] = acc

    return pl.pallas_call(
        body, name=name,
        grid_spec=pltpu.PrefetchScalarGridSpec(
            num_scalar_prefetch=1, grid=(n, per_half),
            in_specs=[pl.BlockSpec((None, tile, cols), lambda s, i, p: (s, i, 0)),
                      pl.BlockSpec((None, len(CHIP_FLIPS), tile, cols), lambda s, i, p: (s, 0, i, 0))],
            out_specs=pl.BlockSpec((None, tile, cols), lambda s, i, p: (s, p[0] * per_half + i, 0))),
        out_shape=jax.ShapeDtypeStruct((n, 2 * half, cols), F32),
        compiler_params=_params(dimension_semantics=("arbitrary", "arbitrary")),
    )(place, part32, recv16)


def _adamw_math(w, g, m, v):
    m = ADAM_B1 * m + (1.0 - ADAM_B1) * g
    v = ADAM_B2 * v + (1.0 - ADAM_B2) * (g * g)
    m_hat = m / (1.0 - ADAM_B1 ** ADAM_STEP)
    v_hat = v / (1.0 - ADAM_B2 ** ADAM_STEP)
    delta = -ADAM_LR * (m_hat / (jnp.sqrt(v_hat) + ADAM_EPS) + ADAM_WD * w)
    return delta, m, v


def _adamw_units(w, m, v, grads, pick, name):
    rows, cols = w.shape
    bc = grads[0].shape[-1]
    tile = min(rows, 256)
    n_g = len(grads)

    def body(pick_ref, w_ref, m_ref, v_ref, *refs):
        g_refs, (g_out, d_ref, nm_ref, nv_ref) = refs[:n_g], refs[n_g:]
        p = pl.program_id(0)
        for a in range(n_g):
            @pl.when(pick_ref[0, p] == a)
            def _(a=a):
                g = g_refs[a][...]
                g_out[...] = g
                d_ref[...], nm_ref[...], nv_ref[...] = _adamw_math(w_ref[...], g, m_ref[...], v_ref[...])

    blk = pl.BlockSpec((tile, bc), lambda p, i, pick: (i, p))

    def g_spec(a):
        return pl.BlockSpec((None, tile, bc),
                            lambda p, i, pick: (jnp.where(pick[0, p] == a, pick[1, p], 0), i, 0))

    return pl.pallas_call(
        body, name=name,
        grid_spec=pltpu.PrefetchScalarGridSpec(
            num_scalar_prefetch=1, grid=(cols // bc, rows // tile),
            in_specs=[blk] * 3 + [g_spec(a) for a in range(n_g)],
            out_specs=[blk] * 4),
        out_shape=[jax.ShapeDtypeStruct(w.shape, F32)] * 4,
        compiler_params=_params(dimension_semantics=("arbitrary", "arbitrary")),
    )(pick, w, m, v, *grads)


ROW_NORM1, ROW_SCALE, ROW_LB, ROW_REC, ROW_FINAL, ROW_LOSS = 0, 1, 2, 4, 5, 6


def _small_update(gathered, w, m, v):
    def body(p_ref, w_ref, m_ref, v_ref, loss_ref, g_ref, d_ref, nm_ref, nv_ref):
        tot = p_ref[0]
        for d in range(1, 8):
            tot = tot + p_ref[d]
        wv = w_ref[...]
        l0 = wv[ROW_LB:ROW_LB + 1, :]
        l1 = wv[ROW_LB + 1:ROW_LB + 2, :]
        mx = jnp.maximum(l0, l1)
        e0 = jnp.exp(l0 - mx)
        e1 = jnp.exp(l1 - mx)
        lb = e0 / (e0 + e1)
        dl0 = tot[ROW_LB:ROW_LB + 1, :] * lb * (1.0 - lb)
        row = lax.broadcasted_iota(jnp.int32, tot.shape, 0)
        g = jnp.where(row == ROW_LB, dl0, jnp.where(row == ROW_LB + 1, -dl0, tot))
        g = jnp.where(row >= ROW_LOSS, 0.0, g)
        g_ref[...] = g
        d_ref[...], nm_ref[...], nv_ref[...] = _adamw_math(wv, g, m_ref[...], v_ref[...])
        loss_ref[...] = (0.5 / D_MODEL) * jnp.sum(tot[ROW_LOSS:ROW_LOSS + 1, :], axis=-1, keepdims=True)

    return pl.pallas_call(
        body, name="small_update",
        out_shape=[jax.ShapeDtypeStruct((1, 1), F32)] + [jax.ShapeDtypeStruct(w.shape, F32)] * 4,
        compiler_params=_params(),
    )(gathered, w, m, v)


SHARD_OWNERS = tuple(range(N_SHARDS))
BLOCKS_POOL = (0, 1, 2, 3)
BLOCKS_A = (4, 6, 8, 10)
BLOCKS_B = (5, 7, 9, 11)
BLOCK_GROUPS = (BLOCKS_POOL, BLOCKS_A, BLOCKS_B)


def _block_owners(blocks):
    return tuple(j // (W_IN_SHARD // COL_BLK) for j in blocks)


def _small_rows(norm1, scale, lb, rec, final):
    pad = jnp.zeros((2, D_MODEL), F32)
    return jnp.concatenate([norm1, scale, lb, rec, final.reshape(1, D_MODEL), pad], axis=0)


def kernel(x, norm1_g, w_in, pool_w, pool_scale, lb_logits, rec_norm_g, w_out, final_norm_g, loss_target, m_norm1_g, m_w_in, m_pool_w, m_pool_scale, m_lb_logits, m_rec_norm_g, m_w_out, m_final_norm_g, v_norm1_g, v_w_in, v_pool_w, v_pool_scale, v_lb_logits, v_rec_norm_g, v_w_out, v_final_norm_g):
    xi, yi, ci = _place()
    chip = 2 * xi + yi
    place = jnp.stack([ci, chip]).astype(jnp.int32)
    pw_rows = N_GROUPS * PW_SHARD
    flat_pw = lambda a: a.reshape(pw_rows, PG)
    x2, target, gf = x[0], loss_target[0], final_norm_g.reshape(1, D_MODEL)
    tri, tri_t, lvl, lvl_t = _chunk_constants()
    tri, tri_t = jnp.asarray(tri, BF16), jnp.asarray(tri_t, BF16)
    lvl, lvl_t = jnp.asarray(lvl), jnp.asarray(lvl_t)

    _, ((w_in_g,),) = _call(None, name="gather_w_in",
                            exchanges=[_ex_gather([_cast_own(w_in[0], place, "cast_w_in")])])
    proj, h = _in_proj(x2, norm1_g, w_in_g)
    (y_rec, o_raw, st_prev), ((w_out_g, pw_g),) = _rec_fwd(
        proj, lb_logits, rec_norm_g, tri, lvl,
        [_ex_gather([_cast_own(w_out[0], place, "cast_w_out"), _cast_own(flat_pw(pool_w), place, "cast_pool_w")])])
    w_out_g = w_out_g.reshape(2 * D_MODEL, D_MODEL)
    pw_full = pw_g.reshape(N_SHARDS, N_GROUPS, PW_SHARD, PG).transpose(1, 0, 2, 3).reshape(N_GROUPS, PG, PG)
    y_pool = _pool_fwd(proj, pw_full, pool_scale)
    dout, dout_b, part_out = _out_proj_loss(y_pool, y_rec, w_out_g, x2, target, gf)

    gw_out = _grad_w_out(y_pool, y_rec, dout_b)
    (du, dgate, gpw, dscale), ((ra_out,),) = _pool_bwd(proj, dout_b, w_out_g, pw_full, pool_scale,
                                                       [_ex_swap([gw_out])])
    p_out32, p_out16 = _add_units(gw_out, ra_out, place, 256, "add_w_out")
    gpw = gpw.reshape(N_GROUPS, N_SHARDS, PW_SHARD, PG).transpose(1, 0, 2, 3).reshape(N_SHARDS, pw_rows, PG)
    gw_in_p = _grad_w_in(h, [(du, 0), (du, 1), (dgate, 0), (dgate, 1)], "grad_w_in_pool")

    rec_args = (proj, o_raw, st_prev, dout_b, w_out_g, lb_logits, rec_norm_g, tri, tri_t, lvl, lvl_t)
    (dq_a, df_a, di_a, drg_a, part_a), ((rb_out,), (ra_inp, ra_pw)) = _rec_bwd(
        *rec_args, 0, "rec_bwd_a", [_ex_send([p_out16], [SHARD_OWNERS]), _ex_swap([gw_in_p, gpw])])
    g_out = _sum_units(p_out32, rb_out, place, 256, "sum_w_out")
    p_inp32, p_inp16 = _add_units(gw_in_p, ra_inp, place, 256, "add_w_in_pool")
    p_pw32, p_pw16 = _add_units(gpw, ra_pw, place, 128, "add_pool_w")
    gw_in_a = _grad_w_in(h, [(dq_a, 0), (df_a, 0), (di_a, 0), (drg_a, 0)], "grad_w_in_a")

    (dq_b, df_b, di_b, drg_b, part_b), ((rb_inp, rb_pw), (ra_ina,)) = _rec_bwd(
        *rec_args, HALF_HEADS, "rec_bwd_b",
        [_ex_send([p_inp16, p_pw16], [_block_owners(BLOCKS_POOL), SHARD_OWNERS]), _ex_swap([gw_in_a])])
    g_inp = _sum_units(p_inp32, rb_inp, place, 256, "sum_w_in_pool")
    g_pw = _sum_units(p_pw32, rb_pw, place, 128, "sum_pool_w")
    p_ina32, p_ina16 = _add_units(gw_in_a, ra_ina, place, 256, "add_w_in_a")
    gw_in_b = _grad_w_in(h, [(dq_b, 0), (df_b, 0), (di_b, 0), (drg_b, 0)], "grad_w_in_b")
    _, ((ra_inb,),) = _call(None, name="swap_w_in_b", exchanges=[_ex_swap([gw_in_b])])
    p_inb32, p_inb16 = _add_units(gw_in_b, ra_inb, place, 256, "add_w_in_b")

    dproj = [(du, 0), (du, 1), (dgate, 0), (dgate, 1), (dq_a, 0), (dq_b, 0), (df_a, 0), (df_b, 0),
             (di_a, 0), (di_b, 0), (drg_a, 0), (drg_b, 0)]
    (dx, part_x), ((rb_ina, rb_inb),) = _grad_x(
        dproj, w_in_g, x2, norm1_g, dout,
        [_ex_send([p_ina16, p_inb16], [_block_owners(BLOCKS_A), _block_owners(BLOCKS_B)])])
    g_ina = _sum_units(p_ina32, rb_ina, place, 256, "sum_w_in_a")
    g_inb = _sum_units(p_inb32, rb_inb, place, 256, "sum_w_in_b")
    _, ((g_out, g_pw, g_inp, g_ina, g_inb),) = _call(
        None, name="join_halves",
        exchanges=[_ex_join([g_out, g_pw, g_inp, g_ina, g_inb],
                            [SHARD_OWNERS, SHARD_OWNERS] + [_block_owners(b) for b in BLOCK_GROUPS])])

    group_of = np.zeros((D_PROJ // COL_BLK,), np.int32)
    index_of = np.zeros((D_PROJ // COL_BLK,), np.int32)
    for gi, blocks in enumerate(BLOCK_GROUPS):
        for i, j in enumerate(blocks):
            group_of[j], index_of[j] = gi, i
    per_shard = W_IN_SHARD // COL_BLK
    pick_in = jnp.stack([lax.dynamic_slice(jnp.asarray(group_of), (per_shard * chip,), (per_shard,)),
                         lax.dynamic_slice(jnp.asarray(index_of), (per_shard * chip,), (per_shard,))])
    pick_own = jnp.stack([jnp.zeros((1,), jnp.int32), chip.reshape(1).astype(jnp.int32)])
    big = [_adamw_units(w_in[0], m_w_in[0], v_w_in[0], [g_inp, g_ina, g_inb], pick_in, "adamw_w_in"),
           _adamw_units(w_out[0], m_w_out[0], v_w_out[0], [g_out], pick_own, "adamw_w_out"),
           _adamw_units(flat_pw(pool_w), flat_pw(m_pool_w), flat_pw(v_pool_w), [g_pw], pick_own, "adamw_pool_w")]

    zero = jnp.zeros((1, D_MODEL), F32)
    part_rec = jnp.concatenate([part_a, part_b], axis=1)
    parts = jnp.concatenate([part_x[0:1], dscale, part_rec[1:2], zero, part_rec[0:1], part_out[0:1],
                             part_out[1:2], zero], axis=0)
    small_w = _small_rows(norm1_g, pool_scale, lb_logits, rec_norm_g, final_norm_g)
    small_m = _small_rows(m_norm1_g, m_pool_scale, m_lb_logits, m_rec_norm_g, m_final_norm_g)
    small_v = _small_rows(v_norm1_g, v_pool_scale, v_lb_logits, v_rec_norm_g, v_final_norm_g)
    loss, *small = _small_update(_gather_small(parts), small_w, small_m, small_v)

    def leaves(k):
        s = small[k]
        return (s[ROW_NORM1:ROW_NORM1 + 1], big[0][k][None], big[2][k].reshape(pool_w.shape),
                s[ROW_SCALE:ROW_SCALE + 1], s[ROW_LB:ROW_LB + 2], s[ROW_REC:ROW_REC + 1],
                big[1][k][None], s[ROW_FINAL])

    return (loss.reshape(()), dx[None], *leaves(0), *leaves(1), *leaves(2), *leaves(3))
```

```python
import functools

import numpy as np
import jax
import jax.numpy as jnp
from jax import lax
from jax.experimental import pallas as pl
from jax.experimental.pallas import tpu as pltpu

F32 = jnp.float32
BF16 = jnp.bfloat16

SEQ = 2048
D_MODEL = 1024
D_PROJ = 6144
N_SEC = 6
N_GROUPS = 4
PG = 256
N_HEADS = 8
HEAD = 128
CHUNK = 64
N_LEVELS = 6
N_SHARDS = 4
W_IN_SHARD = D_PROJ // N_SHARDS
W_OUT_SHARD = 2048 // N_SHARDS
PW_SHARD = PG // N_SHARDS
COL_BLK = 512
EPS = 1e-6

ADAM_LR = 0.001
ADAM_B1 = 0.9
ADAM_B2 = 0.999
ADAM_EPS = 1e-08
ADAM_WD = 0.01
ADAM_STEP = 10

V7X_VMEM_LIMIT = 56 * 1024 * 1024
MESH = pl.DeviceIdType.MESH


def _params(**kw):
    return pltpu.CompilerParams(vmem_limit_bytes=V7X_VMEM_LIMIT, **kw)


def _sig(x):
    return 1.0 / (1.0 + jnp.exp(-x))


def _dot(a, b):
    return jnp.dot(a, b, preferred_element_type=F32)


def _dot_nt(a, b):
    return lax.dot_general(a, b, (((1,), (1,)), ((), ())), preferred_element_type=F32)


def _dot_tn(a, b):
    return lax.dot_general(a, b, (((0,), (0,)), ((), ())), preferred_element_type=F32)


def _split3(a):
    p1 = a.astype(BF16)
    r1 = a - p1.astype(F32)
    p2 = r1.astype(BF16)
    p3 = (r1 - p2.astype(F32)).astype(BF16)
    return jnp.concatenate([p1, p2, p3], axis=-1)


def _dot3(w01, a):
    n = a.shape[-1]
    r = _dot(w01, _split3(a))
    return r[:, :n] + r[:, n:2 * n] + r[:, 2 * n:]


def _chunk_constants():
    j = np.arange(CHUNK)
    tt, ss = np.meshgrid(j, j, indexing="ij")
    x = tt ^ ss
    hb = np.full((CHUNK, CHUNK), -1, np.int32)
    for l in range(N_LEVELS):
        hb[x >= (1 << l)] = l
    lvl = np.where(tt > ss, hb, -1).astype(np.int32)
    tri = (ss <= tt).astype(np.float32)
    return tri, np.ascontiguousarray(tri.T), lvl, np.ascontiguousarray(lvl.T)


def _in_proj(x, g1, w_in_g):
    n_col = D_PROJ // COL_BLK
    per_shard = W_IN_SHARD // COL_BLK
    rows = 256

    def body(x_ref, g_ref, w_ref, proj_ref, h_ref):
        @pl.when(pl.program_id(0) == 0)
        def _():
            def norm(i, _):
                r = pl.ds(pl.multiple_of(i * rows, rows), rows)
                xv = x_ref[r, :]
                inv = lax.rsqrt(jnp.mean(xv * xv, axis=-1, keepdims=True) + EPS)
                h_ref[r, :] = (xv * inv * g_ref[...]).astype(BF16)
                return 0
            lax.fori_loop(0, SEQ // rows, norm, 0)

        def mm(i, _):
            r = pl.ds(pl.multiple_of(i * rows, rows), rows)
            proj_ref[r, :] = _dot(h_ref[r, :], w_ref[...])
            return 0
        lax.fori_loop(0, SEQ // rows, mm, 0)

    return pl.pallas_call(
        body, name="in_proj", grid=(n_col,),
        in_specs=[pl.BlockSpec((SEQ, D_MODEL), lambda j: (0, 0)),
                  pl.BlockSpec((1, D_MODEL), lambda j: (0, 0)),
                  pl.BlockSpec((None, D_MODEL, COL_BLK), lambda j: (j // per_shard, 0, j % per_shard))],
        out_specs=[pl.BlockSpec((SEQ, COL_BLK), lambda j: (0, j)),
                   pl.BlockSpec((SEQ, D_MODEL), lambda j: (0, 0))],
        out_shape=[jax.ShapeDtypeStruct((SEQ, D_PROJ), F32),
                   jax.ShapeDtypeStruct((SEQ, D_MODEL), BF16)],
        compiler_params=_params(dimension_semantics=("arbitrary",)),
    )(x, g1, w_in_g)


POOL_ROWS = 256
POOL_HALO = 16


def _window_sums(ext, g, shift_of):
    s = ext
    for k in range(N_GROUPS):
        s = jnp.where(k <= g, s + pltpu.roll(s, shift_of(k), 0), s)
    return s


def _pool_diff(u_ref, i, g):
    n = POOL_ROWS + POOL_HALO
    r0 = i * POOL_ROWS
    cur = u_ref[pl.ds(pl.multiple_of(r0, POOL_ROWS), POOL_ROWS), :]
    before = u_ref[pl.ds(pl.multiple_of(jnp.maximum(r0 - POOL_HALO, 0), 8), POOL_HALO), :]
    before = jnp.where(i > 0, before, 0.0)
    ext = jnp.concatenate([before, cur], axis=0)
    s = _window_sums(ext, g, lambda k: 1 << k)[POOL_HALO:, :]
    t = r0 + lax.broadcasted_iota(jnp.int32, (POOL_ROWS, 1), 0)
    width = (2 << g).astype(F32)
    inv_count = 1.0 / jnp.minimum((t + 1).astype(F32), width)
    return s * inv_count - cur, inv_count


def _pool_fwd(proj, pw_g, pool_scale):
    def body(u_ref, gate_ref, pw_ref, sc_ref, y_ref):
        g = pl.program_id(0)

        def step(i, _):
            r = pl.ds(pl.multiple_of(i * POOL_ROWS, POOL_ROWS), POOL_ROWS)
            d, _ = _pool_diff(u_ref, i, g)
            mixed = _dot(d.astype(BF16), pw_ref[...])
            gate = gate_ref[r, :]
            y_ref[r, :] = (mixed * sc_ref[...] * (gate * _sig(gate))).astype(BF16)
            return 0
        lax.fori_loop(0, SEQ // POOL_ROWS, step, 0)

    return pl.pallas_call(
        body, name="pool_fwd", grid=(N_GROUPS,),
        in_specs=[pl.BlockSpec((SEQ, PG), lambda g: (0, g)),
                  pl.BlockSpec((SEQ, PG), lambda g: (0, N_GROUPS + g)),
                  pl.BlockSpec((None, PG, PG), lambda g: (g, 0, 0)),
                  pl.BlockSpec((1, PG), lambda g: (0, g))],
        out_specs=pl.BlockSpec((SEQ, PG), lambda g: (0, g)),
        out_shape=jax.ShapeDtypeStruct((SEQ, D_MODEL), BF16),
        compiler_params=_params(dimension_semantics=("arbitrary",)),
    )(proj, proj, pw_g, pool_scale)


REC_ROWS = 512
REC_CHUNKS = REC_ROWS // CHUNK
N_REC_BLK = SEQ // REC_ROWS
SEC_BLK = D_MODEL // HEAD


def _lower_bound(lb_ref):
    l0 = lb_ref[0:1, :]
    l1 = lb_ref[1:2, :]
    mx = jnp.maximum(l0, l1)
    e0 = jnp.exp(l0 - mx)
    e1 = jnp.exp(l1 - mx)
    return e0 / (e0 + e1)


def _gates(q, fl, lb):
    qs = q * _sig(q)
    sf = _sig(fl)
    f = lb + (1.0 - lb) * sf
    return qs, sf, f, 1.0 - f, jnp.log(f)


def _level_factors(big_g, qs, k):
    t = lax.broadcasted_iota(jnp.int32, (CHUNK, HEAD), 0)
    row = lambda r, n: jnp.broadcast_to(big_g[r:r + 1, :], (n, HEAD))
    out = []
    for l in range(N_LEVELS):
        m = 1 << l
        if l == 0:
            g_mid = jnp.where((t & 1) == 1, pltpu.roll(big_g, 1, 0), big_g)
        elif l == 1:
            low = (t & 7) < 4
            g_mid = jnp.concatenate([jnp.where(low[:8], row(8 * i + 1, 8), row(8 * i + 5, 8))
                                     for i in range(CHUNK // 8)], axis=0)
        else:
            g_mid = jnp.concatenate([row(b * 2 * m + m - 1, 2 * m) for b in range(CHUNK // (2 * m))], axis=0)
        up = ((t >> l) & 1) == 1
        e = jnp.exp(jnp.where(up, big_g - g_mid, g_mid - big_g))
        x = jnp.where(up, qs, k) * e
        hi = x.astype(BF16)
        out.append((hi, (x - hi.astype(F32)).astype(BF16), e, up))
    return out


CHIP_FLIPS = ((1, 0), (0, 1), (1, 1))
HBM = pl.BlockSpec(memory_space=pl.ANY)


def _place():
    return lax.axis_index("x"), lax.axis_index("y"), lax.axis_index("c")


def _remote(src, dst, send_sems, recv_sems, k, to):
    return pltpu.make_async_remote_copy(src_ref=src, dst_ref=dst, send_sem=send_sems.at[k],
                                        recv_sem=recv_sems.at[k], device_id=to, device_id_type=MESH)


def _half_rows(ref, c):
    half = ref.shape[-2] // 2
    rows = pl.ds(pl.multiple_of(c * half, half), half)
    return ref.at[:, rows, :] if len(ref.shape) == 3 else ref.at[rows, :]


class _Exchange:
    def __init__(self, inputs, out_shapes, n_sems, start, finish, aliases=None):
        self.inputs, self.out_shapes, self.n_sems = list(inputs), list(out_shapes), n_sems
        self.start, self.finish, self.aliases = start, finish, dict(aliases or {})


def _ex_swap(grads):
    def copies(ins, outs, send, recv):
        x, y, c = _place()
        return [_remote(_half_rows(g, 1 - c), o, send, recv, t, (x, y, 1 - c))
                for t, (g, o) in enumerate(zip(ins, outs))]

    def start(*refs):
        for cp in copies(*refs):
            cp.start()

    def finish(*refs):
        cps = copies(*refs)
        for cp in cps:
            cp.wait_recv()
        for cp in cps:
            cp.wait_send()

    shapes = [jax.ShapeDtypeStruct((a.shape[0], a.shape[1] // 2, a.shape[2]), F32) for a in grads]
    return _Exchange(grads, shapes, len(grads), start, finish)


def _ex_send(parts16, owners):
    def each(ins, outs, send, recv, to_sender, to_owner):
        x, y, c = _place()
        k = 0
        for t, own in enumerate(owners):
            for j, o in enumerate(own):
                for r, (fx, fy) in enumerate(CHIP_FLIPS):
                    tx, ty = x ^ fx, y ^ fy
                    cp = _remote(ins[t].at[j], outs[t].at[j, r], send, recv, k, (tx, ty, c))
                    if to_sender is not None:
                        pl.when(2 * tx + ty == o)(functools.partial(to_sender, cp))
                    if to_owner is not None:
                        pl.when(2 * x + y == o)(functools.partial(to_owner, cp))
                    k += 1

    def start(*refs):
        each(*refs, lambda cp: cp.start(), None)

    def finish(*refs):
        each(*refs, None, lambda cp: cp.wait_recv())
        each(*refs, lambda cp: cp.wait_send(), None)

    shapes = [jax.ShapeDtypeStruct((a.shape[0], len(CHIP_FLIPS)) + a.shape[1:], BF16) for a in parts16]
    return _Exchange(parts16, shapes, len(CHIP_FLIPS) * sum(len(o) for o in owners), start, finish)


def _ex_join(units, owners):
    def each(ins, outs, send, recv, fn):
        x, y, c = _place()
        k = 0
        for t, own in enumerate(owners):
            for j, o in enumerate(own):
                def half(cc, to, u=outs[t].at[j], k=k):
                    return _remote(_half_rows(u, cc), _half_rows(u, cc), send, recv, k, to)
                mine = functools.partial(half, c, (x, y, 1 - c))
                theirs = functools.partial(half, 1 - c, (x, y, c))
                pl.when(2 * x + y == o)(functools.partial(fn, mine, theirs))
                k += 1

    def start(*refs):
        each(*refs, lambda mine, theirs: mine().start())

    def finish(*refs):
        each(*refs, lambda mine, theirs: theirs().wait_recv())
        each(*refs, lambda mine, theirs: mine().wait_send())

    shapes = [jax.ShapeDtypeStruct(a.shape, F32) for a in units]
    return _Exchange(units, shapes, sum(len(o) for o in owners), start, finish,
                     aliases={t: t for t in range(len(units))})


def _ex_gather(slots):
    n_t = len(slots)
    n_fl = len(CHIP_FLIPS)

    def piece(ref, shard, half):
        return _half_rows(ref.at[shard], half)

    def first(outs, send, recv):
        x, y, c = _place()
        s = 2 * x + y
        return [_remote(piece(outs[t], s, c), piece(outs[t], s, c), send, recv, n_t * j + t, (x ^ fx, y ^ fy, c))
                for j, (fx, fy) in enumerate(CHIP_FLIPS) for t in range(n_t)]

    def start(ins, outs, send, recv):
        for cp in first(outs, send, recv):
            cp.start()

    def finish(ins, outs, send, recv):
        x, y, c = _place()
        passed = []
        for j, (fx, fy) in enumerate(CHIP_FLIPS):
            sj = 2 * (x ^ fx) + (y ^ fy)
            for t in range(n_t):
                k = n_t * j + t
                _remote(piece(outs[t], sj, c), piece(outs[t], sj, c), send, recv, k, (x, y, c)).wait_recv()
                cp = _remote(piece(outs[t], sj, c), piece(outs[t], sj, c), send, recv, n_t * n_fl + k, (x, y, 1 - c))
                cp.start()
                passed.append(cp)
        for j, (fx, fy) in enumerate(CHIP_FLIPS):
            sj = 2 * (x ^ fx) + (y ^ fy)
            for t in range(n_t):
                k = n_t * n_fl + n_t * j + t
                _remote(piece(outs[t], sj, 1 - c), piece(outs[t], sj, 1 - c), send, recv, k, (x, y, c)).wait_recv()
        for cp in first(outs, send, recv) + passed:
            cp.wait_send()

    shapes = [jax.ShapeDtypeStruct(a.shape, BF16) for a in slots]
    return _Exchange(slots, shapes, 2 * n_t * n_fl, start, finish, aliases={t: t for t in range(n_t)})


def _call(body, *, name, args=(), in_specs=(), out_specs=(), out_shape=(), grid=(), scratch_shapes=(),
          exchanges=()):
    n_in, n_out, n_scr = len(args), len(out_shape), len(scratch_shapes)
    ex_in, ex_out, ex_scr, spans, alias = [], [], [], [], {}
    for ex in exchanges:
        spans.append((len(ex_in), len(ex.inputs), len(ex_out), len(ex.out_shapes)))
        for i, o in ex.aliases.items():
            alias[n_in + len(ex_in) + i] = n_out + len(ex_out) + o
        ex_in += ex.inputs
        ex_out += ex.out_shapes
        ex_scr += [pltpu.SemaphoreType.DMA((ex.n_sems,)), pltpu.SemaphoreType.DMA((ex.n_sems,))]

    def full(*refs):
        ins, x_in = refs[:n_in], refs[n_in:n_in + len(ex_in)]
        outs = refs[n_in + len(ex_in):n_in + len(ex_in) + n_out]
        x_out = refs[n_in + len(ex_in) + n_out:n_in + len(ex_in) + n_out + len(ex_out)]
        scr = refs[len(refs) - n_scr - len(ex_scr):len(refs) - len(ex_scr)]
        sems = refs[len(refs) - len(ex_scr):]

        def run(which):
            for e, (ex, (i0, ni, o0, no)) in enumerate(zip(exchanges, spans)):
                getattr(ex, which)(x_in[i0:i0 + ni], x_out[o0:o0 + no], sems[2 * e], sems[2 * e + 1])

        if grid:
            ids = [pl.program_id(a) for a in range(len(grid))]
            is_first = functools.reduce(jnp.logical_and, [i == 0 for i in ids])
            is_last = functools.reduce(jnp.logical_and, [i == g - 1 for i, g in zip(ids, grid)])
            pl.when(is_first)(lambda: run("start"))
            body(*ins, *outs, *scr)
            pl.when(is_last)(lambda: run("finish"))
        else:
            run("start")
            if body is not None:
                body(*ins, *outs, *scr)
            run("finish")

    kw = dict(grid=grid) if grid else {}
    if grid:
        kw["compiler_params"] = _params(dimension_semantics=("arbitrary",) * len(grid))
    else:
        kw["compiler_params"] = _params()
    res = pl.pallas_call(
        full, name=name,
        in_specs=list(in_specs) + [HBM] * len(ex_in),
        out_specs=list(out_specs) + [HBM] * len(ex_out),
        out_shape=list(out_shape) + ex_out,
        scratch_shapes=list(scratch_shapes) + ex_scr,
        input_output_aliases=alias, **kw,
    )(*args, *ex_in)
    own = list(res[:n_out])
    per_ex = [list(res[n_out + o0:n_out + o0 + no]) for (_, _, o0, no) in spans]
    return own, per_ex


def _cast_own(w, place, name):
    rows, cols = w.shape
    tile = min(rows, 256)

    def body(place_ref, w_ref, o_ref):
        o_ref[...] = w_ref[...].astype(BF16)

    return pl.pallas_call(
        body, name=name,
        grid_spec=pltpu.PrefetchScalarGridSpec(
            num_scalar_prefetch=1, grid=(rows // tile,),
            in_specs=[pl.BlockSpec((tile, cols), lambda i, p: (i, 0))],
            out_specs=pl.BlockSpec((None, tile, cols), lambda i, p: (p[1], i, 0))),
        out_shape=jax.ShapeDtypeStruct((N_SHARDS, rows, cols), BF16),
        compiler_params=_params(dimension_semantics=("arbitrary",)),
    )(place, w)


def _gather_small(parts):
    def body(p_ref, out_ref, send_sems, recv_sems, local_sem):
        x, y, c = _place()
        me = 4 * x + 2 * y + c
        mine = pltpu.make_async_copy(p_ref, out_ref.at[me], local_sem)
        mine.start()
        sends = []
        for mask in range(1, 8):
            to = (x ^ (mask >> 2), y ^ ((mask >> 1) & 1), c ^ (mask & 1))
            cp = _remote(p_ref, out_ref.at[me], send_sems, recv_sems, mask - 1, to)
            cp.start()
            sends.append(cp)
        for mask in range(1, 8):
            frm = me ^ mask
            _remote(p_ref, out_ref.at[frm], send_sems, recv_sems, mask - 1, (x, y, c)).wait_recv()
        for cp in sends:
            cp.wait_send()
        mine.wait()

    vmem = pl.BlockSpec(memory_space=pltpu.VMEM)
    return pl.pallas_call(
        body, name="gather_small", in_specs=[vmem], out_specs=vmem,
        out_shape=jax.ShapeDtypeStruct((8,) + parts.shape, F32),
        scratch_shapes=[pltpu.SemaphoreType.DMA((7,)), pltpu.SemaphoreType.DMA((7,)),
                        pltpu.SemaphoreType.DMA],
        compiler_params=_params(),
    )(parts)


def _rec_fwd(proj, lb_logits, rec_g, tri, lvl, exchanges):
    def body(q_ref, f_ref, i_ref, rg_ref, lb_ref, g_ref, w_ref, lvl_ref, y_ref, o_ref, stp_ref, st_ref):
        @pl.when(pl.program_id(1) == 0)
        def _():
            st_ref[...] = jnp.zeros_like(st_ref)
        lb = _lower_bound(lb_ref)
        st = st_ref[...]
        for c in range(REC_CHUNKS):
            r = pl.ds(c * CHUNK, CHUNK)
            v = i_ref[r, :]
            rg = rg_ref[r, :]
            qs, _, _, k, g = _gates(q_ref[r, :], f_ref[r, :], lb)
            big_g = _dot3(w_ref[...], g)
            a = jnp.zeros((CHUNK, CHUNK), F32)
            lv = lvl_ref[...]
            for l, (xl, _, _, _) in enumerate(_level_factors(big_g, qs, k)):
                a = a + jnp.where(lv == l, _dot_nt(xl, xl), 0.0)
            stp_ref[c] = st
            vb = v.astype(BF16)
            diag = jnp.sum(qs * k, axis=-1, keepdims=True)
            o = (_dot(a.astype(BF16), vb) + diag * v
                 + _dot_nt((qs * jnp.exp(big_g)).astype(BF16), st.astype(BF16)))
            g_last = big_g[CHUNK - 1:CHUNK, :]
            kdec = (k * jnp.exp(g_last - big_g)).astype(BF16)
            st = st * jnp.exp(g_last) + _dot_tn(vb, kdec)
            o_ref[r, :] = o
            inv = lax.rsqrt(jnp.mean(o * o, axis=-1, keepdims=True) + EPS)
            y_ref[r, :] = (o * inv * g_ref[...] * (rg * _sig(rg))).astype(BF16)
        st_ref[...] = st

    sec = lambda n: pl.BlockSpec((REC_ROWS, HEAD), lambda h, b: (b, n * SEC_BLK + h))
    vec = lambda rows: pl.BlockSpec((rows, HEAD), lambda h, b: (0, h))
    full = lambda a: pl.BlockSpec(a.shape, lambda h, b: (0,) * a.ndim)
    return _call(
        body, name="rec_fwd", grid=(N_HEADS, N_REC_BLK),
        args=(proj, proj, proj, proj, lb_logits, rec_g, tri, lvl),
        in_specs=[sec(2), sec(3), sec(4), sec(5), vec(2), vec(1), full(tri), full(lvl)],
        out_specs=[pl.BlockSpec((REC_ROWS, HEAD), lambda h, b: (b, h)),
                   pl.BlockSpec((REC_ROWS, HEAD), lambda h, b: (b, h)),
                   pl.BlockSpec((None, REC_CHUNKS, HEAD, HEAD), lambda h, b: (h, b, 0, 0))],
        out_shape=[jax.ShapeDtypeStruct((SEQ, D_MODEL), BF16),
                   jax.ShapeDtypeStruct((SEQ, D_MODEL), F32),
                   jax.ShapeDtypeStruct((N_HEADS, SEQ // CHUNK, HEAD, HEAD), F32)],
        scratch_shapes=[pltpu.VMEM((HEAD, HEAD), F32)],
        exchanges=exchanges)


OUT_ROWS = 256


def _out_proj_loss(y_pool, y_rec, w_out_g, x, target, gf):
    def body(yp_ref, yr_ref, w_ref, x_ref, t_ref, gf_ref, dout_ref, doutb_ref, part_ref):
        @pl.when(pl.program_id(0) == 0)
        def _():
            part_ref[...] = jnp.zeros_like(part_ref)
        out = (x_ref[...] + _dot(yp_ref[...], w_ref[0:D_MODEL, :])
               + _dot(yr_ref[...], w_ref[D_MODEL:2 * D_MODEL, :]))
        inv = lax.rsqrt(jnp.mean(out * out, axis=-1, keepdims=True) + EPS)
        gf_v = gf_ref[...]
        diff = out * inv * gf_v - t_ref[...]
        dyf = diff * (1.0 / D_MODEL)
        a = dyf * gf_v
        dout = inv * a - out * (inv * inv * inv) * jnp.mean(a * out, axis=-1, keepdims=True)
        dout_ref[...] = dout
        doutb_ref[...] = dout.astype(BF16)
        part_ref[0:1, :] += jnp.sum(dyf * out * inv, axis=0, keepdims=True)
        part_ref[1:2, :] += jnp.sum(diff * diff, axis=0, keepdims=True)

    row = lambda n: pl.BlockSpec((OUT_ROWS, n), lambda i: (i, 0))
    return pl.pallas_call(
        body, name="out_proj_loss", grid=(SEQ // OUT_ROWS,),
        in_specs=[row(D_MODEL), row(D_MODEL), pl.BlockSpec((2 * D_MODEL, D_MODEL), lambda i: (0, 0)),
                  row(D_MODEL), row(D_MODEL), pl.BlockSpec((1, D_MODEL), lambda i: (0, 0))],
        out_specs=[row(D_MODEL), row(D_MODEL), pl.BlockSpec((8, D_MODEL), lambda i: (0, 0))],
        out_shape=[jax.ShapeDtypeStruct((SEQ, D_MODEL), F32),
                   jax.ShapeDtypeStruct((SEQ, D_MODEL), BF16),
                   jax.ShapeDtypeStruct((8, D_MODEL), F32)],
        compiler_params=_params(dimension_semantics=("arbitrary",)),
    )(y_pool, y_rec, w_out_g, x, target, gf)


def _grad_w_out(y_pool, y_rec, dout_b):
    blk = W_OUT_SHARD // 2
    per = D_MODEL // blk
    n = 2 * per

    def body(yp_ref, yr_ref, d_ref, p32_ref, p16_ref, send_ref, recv_ref, send_sems, recv_sems):
        j = pl.program_id(0)
        x, y, c = _place()

        def copy(u):
            return _remote(send_ref.at[u], recv_ref.at[u], send_sems, recv_sems, u, (x, y, 1 - c))

        for i in range(n):
            @pl.when(j == i)
            def _(i=i):
                res = _dot_tn((yp_ref if i < per else yr_ref)[...], d_ref[...])

                @pl.when(i % 2 == c)
                def _():
                    p32_ref[i // 2] = res

                @pl.when(i % 2 != c)
                def _():
                    send_ref[i // 2] = res
                    copy(i // 2).start()

        @pl.when(j == n - 1)
        def _():
            for u in range(N_SHARDS):
                copy(u).wait_recv()
                tot = p32_ref[u] + recv_ref[u]
                p32_ref[u] = tot
                p16_ref[u] = tot.astype(BF16)
            for u in range(N_SHARDS):
                copy(u).wait_send()

    whole = pl.BlockSpec((N_SHARDS, blk, D_MODEL), lambda j: (0, 0, 0))
    return pl.pallas_call(
        body, name="grad_w_out", grid=(n,),
        in_specs=[pl.BlockSpec((SEQ, blk), lambda j: (0, jnp.minimum(j, per - 1))),
                  pl.BlockSpec((SEQ, blk), lambda j: (0, jnp.maximum(j - per, 0))),
                  pl.BlockSpec((SEQ, D_MODEL), lambda j: (0, 0))],
        out_specs=[whole, whole],
        out_shape=[jax.ShapeDtypeStruct((N_SHARDS, blk, D_MODEL), F32),
                   jax.ShapeDtypeStruct((N_SHARDS, blk, D_MODEL), BF16)],
        scratch_shapes=[pltpu.VMEM((N_SHARDS, blk, D_MODEL), F32), pltpu.VMEM((N_SHARDS, blk, D_MODEL), F32),
                        pltpu.SemaphoreType.DMA((N_SHARDS,)), pltpu.SemaphoreType.DMA((N_SHARDS,))],
        compiler_params=_params(dimension_semantics=("arbitrary",)),
    )(y_pool, y_rec, dout_b)


def _pool_bwd(proj, dout_b, w_out_g, pw_g, pool_scale, exchanges):
    n = POOL_ROWS + POOL_HALO

    def body(u_ref, gate_ref, d_ref, wo_ref, pw_ref, sc_ref,
             du_ref, dgate_ref, dpw_ref, dsc_ref, dd_ref, ddw_ref):
        g = pl.program_id(0)
        dpw_ref[...] = jnp.zeros_like(dpw_ref)
        dsc_ref[...] = jnp.zeros_like(dsc_ref)

        def first(i, _):
            r = pl.ds(pl.multiple_of(i * POOL_ROWS, POOL_ROWS), POOL_ROWS)
            d, inv_count = _pool_diff(u_ref, i, g)
            db = d.astype(BF16)
            mixed = _dot(db, pw_ref[...])
            gate = gate_ref[r, :]
            sg = _sig(gate)
            silu = gate * sg
            dy = _dot_nt(d_ref[r, :], wo_ref[...])
            sc = sc_ref[...]
            dmixed = dy * silu * sc
            dgate_ref[r, :] = (dy * mixed * sc * (sg * (1.0 + gate * (1.0 - sg)))).astype(BF16)
            dsc_ref[...] += jnp.sum(dy * silu * mixed, axis=0, keepdims=True)
            dmb = dmixed.astype(BF16)
            dpw_ref[...] += _dot_tn(db, dmb)
            dd = _dot_nt(dmb, pw_ref[...])
            dd_ref[r, :] = dd
            ddw_ref[r, :] = dd * inv_count
            return 0
        lax.fori_loop(0, SEQ // POOL_ROWS, first, 0)

        def second(i, _):
            r0 = i * POOL_ROWS
            r = pl.ds(pl.multiple_of(r0, POOL_ROWS), POOL_ROWS)
            last = i == SEQ // POOL_ROWS - 1
            after = ddw_ref[pl.ds(pl.multiple_of(jnp.minimum(r0 + POOL_ROWS, SEQ - POOL_HALO), 8), POOL_HALO), :]
            after = jnp.where(last, 0.0, after)
            ext = jnp.concatenate([ddw_ref[r, :], after], axis=0)
            s = _window_sums(ext, g, lambda k: n - (1 << k))[:POOL_ROWS, :]
            du_ref[r, :] = (s - dd_ref[r, :]).astype(BF16)
            return 0
        lax.fori_loop(0, SEQ // POOL_ROWS, second, 0)

    return _call(
        body, name="pool_bwd", grid=(N_GROUPS,),
        args=(proj, proj, dout_b, w_out_g, pw_g, pool_scale),
        in_specs=[pl.BlockSpec((SEQ, PG), lambda g: (0, g)),
                  pl.BlockSpec((SEQ, PG), lambda g: (0, N_GROUPS + g)),
                  pl.BlockSpec((SEQ, D_MODEL), lambda g: (0, 0)),
                  pl.BlockSpec((PG, D_MODEL), lambda g: (g, 0)),
                  pl.BlockSpec((None, PG, PG), lambda g: (g, 0, 0)),
                  pl.BlockSpec((1, PG), lambda g: (0, g))],
        out_specs=[pl.BlockSpec((SEQ, PG), lambda g: (0, g)),
                   pl.BlockSpec((SEQ, PG), lambda g: (0, g)),
                   pl.BlockSpec((None, PG, PG), lambda g: (g, 0, 0)),
                   pl.BlockSpec((1, PG), lambda g: (0, g))],
        out_shape=[jax.ShapeDtypeStruct((SEQ, D_MODEL), BF16),
                   jax.ShapeDtypeStruct((SEQ, D_MODEL), BF16),
                   jax.ShapeDtypeStruct((N_GROUPS, PG, PG), F32),
                   jax.ShapeDtypeStruct((1, D_MODEL), F32)],
        scratch_shapes=[pltpu.VMEM((SEQ, PG), F32), pltpu.VMEM((SEQ, PG), F32)],
        exchanges=exchanges)


HALF_HEADS = N_HEADS // 2
HALF_COLS = HALF_HEADS * HEAD


def _rec_bwd(proj, o_raw, st_prev, dout_b, w_out_g, lb_logits, rec_g, tri, tri_t, lvl, lvl_t, h0, name, exchanges):
    def body(q_ref, f_ref, i_ref, rg_ref, o_ref, stp_ref, d_ref, wo_ref, lb_ref, g_ref,
             w_ref, lvl_ref, lvlt_ref, tri_ref,
             dq_ref, df_ref, di_ref, drg_ref, part_ref, dst_ref):
        @pl.when(pl.program_id(1) == 0)
        def _():
            dst_ref[...] = jnp.zeros_like(dst_ref)
            part_ref[...] = jnp.zeros_like(part_ref)
        lb = _lower_bound(lb_ref)
        grec = g_ref[...]
        dst = dst_ref[...]
        acc_grec = jnp.zeros((1, HEAD), F32)
        acc_lb = jnp.zeros((1, HEAD), F32)
        for c in reversed(range(REC_CHUNKS)):
            r = pl.ds(c * CHUNK, CHUNK)
            q = q_ref[r, :]
            v = i_ref[r, :]
            rg = rg_ref[r, :]
            o = o_ref[r, :]
            dy = _dot_nt(d_ref[r, :], wo_ref[...])
            sg = _sig(rg)
            silu = rg * sg
            inv = lax.rsqrt(jnp.mean(o * o, axis=-1, keepdims=True) + EPS)
            recn = o * inv
            drg_ref[r, :] = (dy * recn * grec * (sg * (1.0 + rg * (1.0 - sg)))).astype(BF16)
            acc_grec = acc_grec + jnp.sum(dy * silu * recn, axis=0, keepdims=True)
            drecn = dy * silu * grec
            do = inv * drecn - o * (inv * inv * inv) * jnp.mean(drecn * o, axis=-1, keepdims=True)
            qs, sf, f, k, g = _gates(q, f_ref[r, :], lb)
            big_g = _dot3(w_ref[...], g)
            g_last = big_g[CHUNK - 1:CHUNK, :]
            e_g = jnp.exp(big_g)
            e_rev = jnp.exp(g_last - big_g)
            e_last = jnp.exp(g_last)
            levels = _level_factors(big_g, qs, k)
            lv = lvl_ref[...]
            lvt = lvlt_ref[...]
            a_t = jnp.zeros((CHUNK, CHUNK), F32)
            for l, (xl, _, _, _) in enumerate(levels):
                a_t = a_t + jnp.where(lvt == l, _dot_nt(xl, xl), 0.0)
            dob = do.astype(BF16)
            vb = v.astype(BF16)
            d_a = _dot_nt(dob, vb)
            d_at = _dot_nt(vb, dob)
            stp = stp_ref[c]
            dstb = dst.astype(BF16)
            q_g = qs * e_g
            kdec = k * e_rev
            diag = jnp.sum(qs * k, axis=-1, keepdims=True)
            dv = _dot(a_t.astype(BF16), dob) + diag * do + _dot_nt(kdec.astype(BF16), dstb)
            dq_g = _dot(dob, stp.astype(BF16))
            dkdec = _dot(vb, dstb)
            de_last = jnp.sum(stp * dst, axis=0, keepdims=True)
            dst = dst * e_last + _dot_tn(dob, q_g.astype(BF16))
            dqs_i = jnp.zeros((CHUNK, HEAD), F32)
            dk_i = jnp.zeros((CHUNK, HEAD), F32)
            for l, (xl, xlo, e, up) in enumerate(levels):
                z = jnp.where(lv == l, d_a, jnp.where(lvt == l, d_at, 0.0))
                tmp = _dot(z.astype(BF16), jnp.concatenate([xl, xlo], axis=-1))
                tmp = (tmp[:, :HEAD] + tmp[:, HEAD:]) * e
                dqs_i = dqs_i + jnp.where(up, tmp, 0.0)
                dk_i = dk_i + jnp.where(up, 0.0, tmp)
            ddiag = jnp.sum(do * v, axis=-1, keepdims=True)
            dqs = dqs_i + ddiag * k + dq_g * e_g
            dk = dk_i + ddiag * qs + dkdec * e_rev
            dg_rev = dkdec * kdec
            dg_last = jnp.sum(dg_rev, axis=0, keepdims=True) + de_last * e_last
            dbig_g = qs * dqs_i - k * dk_i + dq_g * q_g - dg_rev
            dg = _dot3(tri_ref[...], dbig_g) + dg_last
            df = dg / f - dk
            df_ref[r, :] = (df * (1.0 - lb) * sf * (1.0 - sf)).astype(BF16)
            acc_lb = acc_lb + jnp.sum(df * (1.0 - sf), axis=0, keepdims=True)
            sq = _sig(q)
            dq_ref[r, :] = (dqs * (sq * (1.0 + q * (1.0 - sq)))).astype(BF16)
            di_ref[r, :] = dv.astype(BF16)
        dst_ref[...] = dst
        part_ref[0:1, :] += acc_grec
        part_ref[1:2, :] += acc_lb

    rev = lambda b: N_REC_BLK - 1 - b
    sec = lambda n: pl.BlockSpec((REC_ROWS, HEAD), lambda h, b: (rev(b), n * SEC_BLK + h0 + h))
    col_in = pl.BlockSpec((REC_ROWS, HEAD), lambda h, b: (rev(b), h0 + h))
    col = pl.BlockSpec((REC_ROWS, HEAD), lambda h, b: (rev(b), h))
    vec_in = lambda rows: pl.BlockSpec((rows, HEAD), lambda h, b: (0, h0 + h))
    full = lambda a: pl.BlockSpec(a.shape, lambda h, b: (0,) * a.ndim)
    return _call(
        body, name=name, grid=(HALF_HEADS, N_REC_BLK),
        args=(proj, proj, proj, proj, o_raw, st_prev, dout_b, w_out_g, lb_logits, rec_g, tri, lvl, lvl_t, tri_t),
        in_specs=[sec(2), sec(3), sec(4), sec(5), col_in,
                  pl.BlockSpec((None, REC_CHUNKS, HEAD, HEAD), lambda h, b: (h0 + h, rev(b), 0, 0)),
                  pl.BlockSpec((REC_ROWS, D_MODEL), lambda h, b: (rev(b), 0)),
                  pl.BlockSpec((HEAD, D_MODEL), lambda h, b: (SEC_BLK + h0 + h, 0)),
                  vec_in(2), vec_in(1), full(tri), full(lvl), full(lvl_t), full(tri_t)],
        out_specs=[col, col, col, col, pl.BlockSpec((8, HEAD), lambda h, b: (0, h))],
        out_shape=[jax.ShapeDtypeStruct((SEQ, HALF_COLS), BF16)] * 4
                  + [jax.ShapeDtypeStruct((8, HALF_COLS), F32)],
        scratch_shapes=[pltpu.VMEM((HEAD, HEAD), F32)],
        exchanges=exchanges)


def _w_in_block(w_ref, j):
    per_shard = W_IN_SHARD // COL_BLK
    return w_ref[j // per_shard, :, (j % per_shard) * COL_BLK:(j % per_shard + 1) * COL_BLK]


def _grad_x(dproj, w_in_g, x, g1, dout, exchanges):
    rows = 256
    n_blk = len(dproj)

    def body(*refs):
        dp_refs = refs[:n_blk]
        w_ref, x_ref, g_ref, dout_ref, dx_ref, part_ref = refs[n_blk:]

        @pl.when(pl.program_id(0) == 0)
        def _():
            part_ref[...] = jnp.zeros_like(part_ref)
        dh = jnp.zeros((rows, D_MODEL), F32)
        for j in range(n_blk):
            dh = dh + _dot_nt(dp_refs[j][...], _w_in_block(w_ref, j))
        xv = x_ref[...]
        inv = lax.rsqrt(jnp.mean(xv * xv, axis=-1, keepdims=True) + EPS)
        a = dh * g_ref[...]
        dx_ref[...] = (dout_ref[...] + inv * a
                       - xv * (inv * inv * inv) * jnp.mean(a * xv, axis=-1, keepdims=True))
        part_ref[0:1, :] += jnp.sum(dh * xv * inv, axis=0, keepdims=True)

    row = lambda: pl.BlockSpec((rows, D_MODEL), lambda i: (i, 0))
    dp_spec = lambda cb: pl.BlockSpec((rows, COL_BLK), lambda i: (i, cb))
    return _call(
        body, name="grad_x", grid=(SEQ // rows,),
        args=tuple(a for a, _ in dproj) + (w_in_g, x, g1, dout),
        in_specs=[dp_spec(cb) for _, cb in dproj]
                 + [pl.BlockSpec((N_SHARDS, D_MODEL, W_IN_SHARD), lambda i: (0, 0, 0)),
                    row(), pl.BlockSpec((1, D_MODEL), lambda i: (0, 0)), row()],
        out_specs=[row(), pl.BlockSpec((8, D_MODEL), lambda i: (0, 0))],
        out_shape=[jax.ShapeDtypeStruct((SEQ, D_MODEL), F32),
                   jax.ShapeDtypeStruct((8, D_MODEL), F32)],
        exchanges=exchanges)


def _grad_w_in(h, blocks, name):
    n_blk = len(blocks)
    half = D_MODEL // 2

    def body(h_ref, *refs):
        dp_refs = refs[:n_blk]
        p32_ref, p16_ref, send_ref, recv_ref, send_sems, recv_sems = refs[n_blk:]
        j = pl.program_id(0)
        x, y, c = _place()
        cols = lambda cc: pl.ds(pl.multiple_of(cc * half, half), half)

        def copy(i):
            return _remote(send_ref.at[i], recv_ref.at[i], send_sems, recv_sems, i, (x, y, 1 - c))

        for i in range(n_blk):
            @pl.when(j == i)
            def _(i=i):
                send_ref[i] = _dot_tn(h_ref[:, cols(1 - c)], dp_refs[i][...])
                copy(i).start()
                p32_ref[i] = _dot_tn(h_ref[:, cols(c)], dp_refs[i][...])

        @pl.when(j == n_blk - 1)
        def _():
            for i in range(n_blk):
                copy(i).wait_recv()
                tot = p32_ref[i] + recv_ref[i]
                p32_ref[i] = tot
                p16_ref[i] = tot.astype(BF16)
            for i in range(n_blk):
                copy(i).wait_send()

    whole = pl.BlockSpec((n_blk, half, COL_BLK), lambda j: (0, 0, 0))
    return pl.pallas_call(
        body, name=name, grid=(n_blk,),
        in_specs=[pl.BlockSpec((SEQ, D_MODEL), lambda j: (0, 0))]
                 + [pl.BlockSpec((SEQ, COL_BLK), lambda j, cb=cb: (0, cb)) for _, cb in blocks],
        out_specs=[whole, whole],
        out_shape=[jax.ShapeDtypeStruct((n_blk, half, COL_BLK), F32),
                   jax.ShapeDtypeStruct((n_blk, half, COL_BLK), BF16)],
        scratch_shapes=[pltpu.VMEM((n_blk, half, COL_BLK), F32), pltpu.VMEM((n_blk, half, COL_BLK), F32),
                        pltpu.SemaphoreType.DMA((n_blk,)), pltpu.SemaphoreType.DMA((n_blk,))],
        compiler_params=_params(dimension_semantics=("arbitrary",)),
    )(h, *[a for a, _ in blocks])


def _add_units(grad, recv, place, tile, name):
    n, rows, cols = grad.shape
    per_half = rows // 2 // tile

    def body(place_ref, g_ref, r_ref, o32_ref, o16_ref):
        v = g_ref[...] + r_ref[...]
        o32_ref[...] = v
        o16_ref[...] = v.astype(BF16)

    blk = lambda f: pl.BlockSpec((None, tile, cols), f)
    out = lambda s, i, p: (s, i, 0)
    return pl.pallas_call(
        body, name=name,
        grid_spec=pltpu.PrefetchScalarGridSpec(
            num_scalar_prefetch=1, grid=(n, per_half),
            in_specs=[blk(lambda s, i, p: (s, p[0] * per_half + i, 0)), blk(out)],
            out_specs=[blk(out), blk(out)]),
        out_shape=[jax.ShapeDtypeStruct(recv.shape, F32), jax.ShapeDtypeStruct(recv.shape, BF16)],
        compiler_params=_params(dimension_semantics=("arbitrary", "arbitrary")),
    )(place, grad, recv)


def _sum_units(part32, recv16, place, tile, name):
    n, half, cols = part32.shape
    per_half = half // tile

    def body(place_ref, p_ref, r_ref, o_ref):
        acc = p_ref[...]
        for j in range(len(CHIP_FLIPS)):
            acc = acc + r_ref[j].astype(F32)
        o_ref[...] = acc

    return pl.pallas_call(
        body, name=name,
        grid_spec=pltpu.PrefetchScalarGridSpec(
            num_scalar_prefetch=1, grid=(n, per_half),
            in_specs=[pl.BlockSpec((None, tile, cols), lambda s, i, p: (s, i, 0)),
                      pl.BlockSpec((None, len(CHIP_FLIPS), tile, cols), lambda s, i, p: (s, 0, i, 0))],
            out_specs=pl.BlockSpec((None, tile, cols), lambda s, i, p: (s, p[0] * per_half + i, 0))),
        out_shape=jax.ShapeDtypeStruct((n, 2 * half, cols), F32),
        compiler_params=_params(dimension_semantics=("arbitrary", "arbitrary")),
    )(place, part32, recv16)


def _adamw_math(w, g, m, v):
    m = ADAM_B1 * m + (1.0 - ADAM_B1) * g
    v = ADAM_B2 * v + (1.0 - ADAM_B2) * (g * g)
    m_hat = m / (1.0 - ADAM_B1 ** ADAM_STEP)
    v_hat = v / (1.0 - ADAM_B2 ** ADAM_STEP)
    delta = -ADAM_LR * (m_hat / (jnp.sqrt(v_hat) + ADAM_EPS) + ADAM_WD * w)
    return delta, m, v


def _adamw_units(w, m, v, grads, pick, name):
    rows, cols = w.shape
    bc = grads[0].shape[-1]
    tile = min(rows, 256)
    n_g = len(grads)

    def body(pick_ref, w_ref, m_ref, v_ref, *refs):
        g_refs, (g_out, d_ref, nm_ref, nv_ref) = refs[:n_g], refs[n_g:]
        p = pl.program_id(0)
        for a in range(n_g):
            @pl.when(pick_ref[0, p] == a)
            def _(a=a):
                g = g_refs[a][...]
                g_out[...] = g
                d_ref[...], nm_ref[...], nv_ref[...] = _adamw_math(w_ref[...], g, m_ref[...], v_ref[...])

    blk = pl.BlockSpec((tile, bc), lambda p, i, pick: (i, p))

    def g_spec(a):
        return pl.BlockSpec((None, tile, bc),
                            lambda p, i, pick: (jnp.where(pick[0, p] == a, pick[1, p], 0), i, 0))

    return pl.pallas_call(
        body, name=name,
        grid_spec=pltpu.PrefetchScalarGridSpec(
            num_scalar_prefetch=1, grid=(cols // bc, rows // tile),
            in_specs=[blk] * 3 + [g_spec(a) for a in range(n_g)],
            out_specs=[blk] * 4),
        out_shape=[jax.ShapeDtypeStruct(w.shape, F32)] * 4,
        compiler_params=_params(dimension_semantics=("arbitrary", "arbitrary")),
    )(pick, w, m, v, *grads)


ROW_NORM1, ROW_SCALE, ROW_LB, ROW_REC, ROW_FINAL, ROW_LOSS = 0, 1, 2, 4, 5, 6


def _small_update(gathered, w, m, v):
    def body(p_ref, w_ref, m_ref, v_ref, loss_ref, g_ref, d_ref, nm_ref, nv_ref):
        tot = p_ref[0]
        for d in range(1, 8):
            tot = tot + p_ref[d]
        wv = w_ref[...]
        l0 = wv[ROW_LB:ROW_LB + 1, :]
        l1 = wv[ROW_LB + 1:ROW_LB + 2, :]
        mx = jnp.maximum(l0, l1)
        e0 = jnp.exp(l0 - mx)
        e1 = jnp.exp(l1 - mx)
        lb = e0 / (e0 + e1)
        dl0 = tot[ROW_LB:ROW_LB + 1, :] * lb * (1.0 - lb)
        row = lax.broadcasted_iota(jnp.int32, tot.shape, 0)
        g = jnp.where(row == ROW_LB, dl0, jnp.where(row == ROW_LB + 1, -dl0, tot))
        g = jnp.where(row >= ROW_LOSS, 0.0, g)
        g_ref[...] = g
        d_ref[...], nm_ref[...], nv_ref[...] = _adamw_math(wv, g, m_ref[...], v_ref[...])
        loss_ref[...] = (0.5 / D_MODEL) * jnp.sum(tot[ROW_LOSS:ROW_LOSS + 1, :], axis=-1, keepdims=True)

    return pl.pallas_call(
        body, name="small_update",
        out_shape=[jax.ShapeDtypeStruct((1, 1), F32)] + [jax.ShapeDtypeStruct(w.shape, F32)] * 4,
        compiler_params=_params(),
    )(gathered, w, m, v)


SHARD_OWNERS = tuple(range(N_SHARDS))
BLOCKS_POOL = (0, 1, 2, 3)
BLOCKS_A = (4, 6, 8, 10)
BLOCKS_B = (5, 7, 9, 11)
BLOCK_GROUPS = (BLOCKS_POOL, BLOCKS_A, BLOCKS_B)


def _block_owners(blocks):
    return tuple(j // (W_IN_SHARD // COL_BLK) for j in blocks)


def _small_rows(norm1, scale, lb, rec, final):
    pad = jnp.zeros((2, D_MODEL), F32)
    return jnp.concatenate([norm1, scale, lb, rec, final.reshape(1, D_MODEL), pad], axis=0)


def kernel(x, norm1_g, w_in, pool_w, pool_scale, lb_logits, rec_norm_g, w_out, final_norm_g, loss_target, m_norm1_g, m_w_in, m_pool_w, m_pool_scale, m_lb_logits, m_rec_norm_g, m_w_out, m_final_norm_g, v_norm1_g, v_w_in, v_pool_w, v_pool_scale, v_lb_logits, v_rec_norm_g, v_w_out, v_final_norm_g):
    xi, yi, ci = _place()
    chip = 2 * xi + yi
    place = jnp.stack([ci, chip]).astype(jnp.int32)
    pw_rows = N_GROUPS * PW_SHARD
    flat_pw = lambda a: a.reshape(pw_rows, PG)
    x2, target, gf = x[0], loss_target[0], final_norm_g.reshape(1, D_MODEL)
    tri, tri_t, lvl, lvl_t = _chunk_constants()
    tri, tri_t = jnp.asarray(tri, BF16), jnp.asarray(tri_t, BF16)
    lvl, lvl_t = jnp.asarray(lvl), jnp.asarray(lvl_t)

    _, ((w_in_g,),) = _call(None, name="gather_w_in",
                            exchanges=[_ex_gather([_cast_own(w_in[0], place, "cast_w_in")])])
    proj, h = _in_proj(x2, norm1_g, w_in_g)
    (y_rec, o_raw, st_prev), ((w_out_g, pw_g),) = _rec_fwd(
        proj, lb_logits, rec_norm_g, tri, lvl,
        [_ex_gather([_cast_own(w_out[0], place, "cast_w_out"), _cast_own(flat_pw(pool_w), place, "cast_pool_w")])])
    w_out_g = w_out_g.reshape(2 * D_MODEL, D_MODEL)
    pw_full = pw_g.reshape(N_SHARDS, N_GROUPS, PW_SHARD, PG).transpose(1, 0, 2, 3).reshape(N_GROUPS, PG, PG)
    y_pool = _pool_fwd(proj, pw_full, pool_scale)
    dout, dout_b, part_out = _out_proj_loss(y_pool, y_rec, w_out_g, x2, target, gf)

    p_out32, p_out16 = _grad_w_out(y_pool, y_rec, dout_b)
    (du, dgate, gpw, dscale), ((rb_out,),) = _pool_bwd(proj, dout_b, w_out_g, pw_full, pool_scale,
                                                       [_ex_send([p_out16], [SHARD_OWNERS])])
    g_out = _sum_units(p_out32, rb_out, place, 256, "sum_w_out")
    gpw = gpw.reshape(N_GROUPS, N_SHARDS, PW_SHARD, PG).transpose(1, 0, 2, 3).reshape(N_SHARDS, pw_rows, PG)
    p_inp32, p_inp16 = _grad_w_in(h, [(du, 0), (du, 1), (dgate, 0), (dgate, 1)], "grad_w_in_pool")

    rec_args = (proj, o_raw, st_prev, dout_b, w_out_g, lb_logits, rec_norm_g, tri, tri_t, lvl, lvl_t)
    (dq_a, df_a, di_a, drg_a, part_a), ((rb_inp,), (ra_pw,)) = _rec_bwd(
        *rec_args, 0, "rec_bwd_a", [_ex_send([p_inp16], [_block_owners(BLOCKS_POOL)]), _ex_swap([gpw])])
    g_inp = _sum_units(p_inp32, rb_inp, place, 256, "sum_w_in_pool")
    p_pw32, p_pw16 = _add_units(gpw, ra_pw, place, 128, "add_pool_w")
    p_ina32, p_ina16 = _grad_w_in(h, [(dq_a, 0), (df_a, 0), (di_a, 0), (drg_a, 0)], "grad_w_in_a")

    (dq_b, df_b, di_b, drg_b, part_b), ((rb_ina, rb_pw),) = _rec_bwd(
        *rec_args, HALF_HEADS, "rec_bwd_b",
        [_ex_send([p_ina16, p_pw16], [_block_owners(BLOCKS_A), SHARD_OWNERS])])
    g_ina = _sum_units(p_ina32, rb_ina, place, 256, "sum_w_in_a")
    g_pw = _sum_units(p_pw32, rb_pw, place, 128, "sum_pool_w")
    p_inb32, p_inb16 = _grad_w_in(h, [(dq_b, 0), (df_b, 0), (di_b, 0), (drg_b, 0)], "grad_w_in_b")

    dproj = [(du, 0), (du, 1), (dgate, 0), (dgate, 1), (dq_a, 0), (dq_b, 0), (df_a, 0), (df_b, 0),
             (di_a, 0), (di_b, 0), (drg_a, 0), (drg_b, 0)]
    (dx, part_x), ((rb_inb,),) = _grad_x(dproj, w_in_g, x2, norm1_g, dout,
                                         [_ex_send([p_inb16], [_block_owners(BLOCKS_B)])])
    g_inb = _sum_units(p_inb32, rb_inb, place, 256, "sum_w_in_b")
    _, ((g_out, g_pw, g_inp, g_ina, g_inb),) = _call(
        None, name="join_halves",
        exchanges=[_ex_join([g_out, g_pw, g_inp, g_ina, g_inb],
                            [SHARD_OWNERS, SHARD_OWNERS] + [_block_owners(b) for b in BLOCK_GROUPS])])

    group_of = np.zeros((D_PROJ // COL_BLK,), np.int32)
    index_of = np.zeros((D_PROJ // COL_BLK,), np.int32)
    for gi, blocks in enumerate(BLOCK_GROUPS):
        for i, j in enumerate(blocks):
            group_of[j], index_of[j] = gi, i
    per_shard = W_IN_SHARD // COL_BLK
    pick_in = jnp.stack([lax.dynamic_slice(jnp.asarray(group_of), (per_shard * chip,), (per_shard,)),
                         lax.dynamic_slice(jnp.asarray(index_of), (per_shard * chip,), (per_shard,))])
    pick_own = jnp.stack([jnp.zeros((1,), jnp.int32), chip.reshape(1).astype(jnp.int32)])
    big = [_adamw_units(w_in[0], m_w_in[0], v_w_in[0], [g_inp, g_ina, g_inb], pick_in, "adamw_w_in"),
           _adamw_units(w_out[0], m_w_out[0], v_w_out[0], [g_out], pick_own, "adamw_w_out"),
           _adamw_units(flat_pw(pool_w), flat_pw(m_pool_w), flat_pw(v_pool_w), [g_pw], pick_own, "adamw_pool_w")]

    zero = jnp.zeros((1, D_MODEL), F32)
    part_rec = jnp.concatenate([part_a, part_b], axis=1)
    parts = jnp.concatenate([part_x[0:1], dscale, part_rec[1:2], zero, part_rec[0:1], part_out[0:1],
                             part_out[1:2], zero], axis=0)
    small_w = _small_rows(norm1_g, pool_scale, lb_logits, rec_norm_g, final_norm_g)
    small_m = _small_rows(m_norm1_g, m_pool_scale, m_lb_logits, m_rec_norm_g, m_final_norm_g)
    small_v = _small_rows(v_norm1_g, v_pool_scale, v_lb_logits, v_rec_norm_g, v_final_norm_g)
    loss, *small = _small_update(_gather_small(parts), small_w, small_m, small_v)

    def leaves(k):
        s = small[k]
        return (s[ROW_NORM1:ROW_NORM1 + 1], big[0][k][None], big[2][k].reshape(pool_w.shape),
                s[ROW_SCALE:ROW_SCALE + 1], s[ROW_LB:ROW_LB + 2], s[ROW_REC:ROW_REC + 1],
                big[1][k][None], s[ROW_FINAL])

    return (loss.reshape(()), dx[None], *leaves(0), *leaves(1), *leaves(2), *leaves(3))
```

```python
import functools

import numpy as np
import jax
import jax.numpy as jnp
from jax import lax
from jax.experimental import pallas as pl
from jax.experimental.pallas import tpu as pltpu

F32 = jnp.float32
BF16 = jnp.bfloat16

SEQ = 2048
D_MODEL = 1024
D_PROJ = 6144
N_SEC = 6
N_GROUPS = 4
PG = 256
N_HEADS = 8
HEAD = 128
CHUNK = 64
N_LEVELS = 6
N_SHARDS = 4
W_IN_SHARD = D_PROJ // N_SHARDS
W_OUT_SHARD = 2048 // N_SHARDS
PW_SHARD = PG // N_SHARDS
COL_BLK = 512
EPS = 1e-6

ADAM_LR = 0.001
ADAM_B1 = 0.9
ADAM_B2 = 0.999
ADAM_EPS = 1e-08
ADAM_WD = 0.01
ADAM_STEP = 10

V7X_VMEM_LIMIT = 56 * 1024 * 1024
MESH = pl.DeviceIdType.MESH


def _params(**kw):
    return pltpu.CompilerParams(vmem_limit_bytes=V7X_VMEM_LIMIT, **kw)


def _sig(x):
    return 1.0 / (1.0 + jnp.exp(-x))


def _dot(a, b):
    return jnp.dot(a, b, preferred_element_type=F32)


def _dot_nt(a, b):
    return lax.dot_general(a, b, (((1,), (1,)), ((), ())), preferred_element_type=F32)


def _dot_tn(a, b):
    return lax.dot_general(a, b, (((0,), (0,)), ((), ())), preferred_element_type=F32)


def _split3(a):
    p1 = a.astype(BF16)
    r1 = a - p1.astype(F32)
    p2 = r1.astype(BF16)
    p3 = (r1 - p2.astype(F32)).astype(BF16)
    return jnp.concatenate([p1, p2, p3], axis=-1)


def _dot3(w01, a):
    n = a.shape[-1]
    r = _dot(w01, _split3(a))
    return r[:, :n] + r[:, n:2 * n] + r[:, 2 * n:]


def _chunk_constants():
    j = np.arange(CHUNK)
    tt, ss = np.meshgrid(j, j, indexing="ij")
    x = tt ^ ss
    hb = np.full((CHUNK, CHUNK), -1, np.int32)
    for l in range(N_LEVELS):
        hb[x >= (1 << l)] = l
    lvl = np.where(tt > ss, hb, -1).astype(np.int32)
    tri = (ss <= tt).astype(np.float32)
    return tri, np.ascontiguousarray(tri.T), lvl, np.ascontiguousarray(lvl.T)


def _in_proj(x, g1, w_slots, place):
    n_col = D_PROJ // COL_BLK
    per_shard = W_IN_SHARD // COL_BLK
    n_fl = len(CHIP_FLIPS)
    rows = 1024

    def shard_at(m, chip):
        return chip ^ jnp.where(m == 0, 0, jnp.where(m == 1, 2, jnp.where(m == 2, 1, 3)))

    def body(place_ref, x_ref, g_ref, w_in_ref, proj_ref, h_ref, w_ref, wbuf, load_sems, send_sems, recv_sems):
        t = pl.program_id(0)
        x_, y_, c = _place()
        chip = 2 * x_ + y_
        piece = lambda shard, half: _half_rows(w_ref.at[shard], half)

        def sends():
            return [_remote(piece(chip, c), piece(chip, c), send_sems, recv_sems, j, (x_ ^ fx, y_ ^ fy, c))
                    for j, (fx, fy) in enumerate(CHIP_FLIPS)]

        def passed_on(j, half, to):
            sj = shard_at(j + 1, chip)
            return _remote(piece(sj, half), piece(sj, half), send_sems, recv_sems, n_fl + j, to)

        def load(shard, p):
            return pltpu.make_async_copy(w_ref.at[shard, :, pl.ds(p * COL_BLK, COL_BLK)], wbuf.at[p],
                                         load_sems.at[p])

        @pl.when(t == 0)
        def _():
            for cp in sends():
                cp.start()

            def norm(i, _):
                r = pl.ds(pl.multiple_of(i * rows, rows), rows)
                xv = x_ref[r, :]
                inv = lax.rsqrt(jnp.mean(xv * xv, axis=-1, keepdims=True) + EPS)
                h_ref[r, :] = (xv * inv * g_ref[...]).astype(BF16)
                return 0
            lax.fori_loop(0, SEQ // rows, norm, 0)

        for m in range(N_SHARDS):
            @pl.when(t == per_shard * m)
            def _(m=m):
                if m > 0:
                    j = m - 1
                    sj = shard_at(m, chip)
                    _remote(piece(sj, c), piece(sj, c), send_sems, recv_sems, j, (x_, y_, c)).wait_recv()
                    passed_on(j, c, (x_, y_, 1 - c)).start()
                    passed_on(j, 1 - c, (x_, y_, c)).wait_recv()
                for p in range(per_shard):
                    load(shard_at(m, chip), p).start()

        for p in range(per_shard):
            @pl.when(t % per_shard == p)
            def _(p=p):
                load(0, p).wait()

                def mm(i, _):
                    r = pl.ds(pl.multiple_of(i * rows, rows), rows)
                    proj_ref[r, :] = _dot(h_ref[r, :], wbuf[p])
                    return 0
                lax.fori_loop(0, SEQ // rows, mm, 0)

        @pl.when(t == n_col - 1)
        def _():
            for cp in sends():
                cp.wait_send()
            for j in range(n_fl):
                passed_on(j, c, (x_, y_, 1 - c)).wait_send()

    return pl.pallas_call(
        body, name="in_proj",
        grid_spec=pltpu.PrefetchScalarGridSpec(
            num_scalar_prefetch=1, grid=(n_col,),
            in_specs=[pl.BlockSpec((SEQ, D_MODEL), lambda t, p: (0, 0)),
                      pl.BlockSpec((1, D_MODEL), lambda t, p: (0, 0)),
                      pl.BlockSpec(memory_space=pl.ANY)],
            out_specs=[pl.BlockSpec((SEQ, COL_BLK),
                                    lambda t, p: (0, per_shard * shard_at(t // per_shard, p[1]) + t % per_shard)),
                       pl.BlockSpec((SEQ, D_MODEL), lambda t, p: (0, 0)),
                       pl.BlockSpec(memory_space=pl.ANY)],
            scratch_shapes=[pltpu.VMEM((per_shard, D_MODEL, COL_BLK), BF16),
                            pltpu.SemaphoreType.DMA((per_shard,)),
                            pltpu.SemaphoreType.DMA((2 * n_fl,)), pltpu.SemaphoreType.DMA((2 * n_fl,))]),
        out_shape=[jax.ShapeDtypeStruct((SEQ, D_PROJ), F32),
                   jax.ShapeDtypeStruct((SEQ, D_MODEL), BF16),
                   jax.ShapeDtypeStruct(w_slots.shape, BF16)],
        input_output_aliases={3: 2},
        compiler_params=_params(dimension_semantics=("arbitrary",)),
    )(place, x, g1, w_slots)


POOL_ROWS = 256
POOL_HALO = 16


def _window_sums(ext, g, shift_of):
    s = ext
    for k in range(N_GROUPS):
        s = jnp.where(k <= g, s + pltpu.roll(s, shift_of(k), 0), s)
    return s


def _pool_diff(u_ref, i, g):
    n = POOL_ROWS + POOL_HALO
    r0 = i * POOL_ROWS
    cur = u_ref[pl.ds(pl.multiple_of(r0, POOL_ROWS), POOL_ROWS), :]
    before = u_ref[pl.ds(pl.multiple_of(jnp.maximum(r0 - POOL_HALO, 0), 8), POOL_HALO), :]
    before = jnp.where(i > 0, before, 0.0)
    ext = jnp.concatenate([before, cur], axis=0)
    s = _window_sums(ext, g, lambda k: 1 << k)[POOL_HALO:, :]
    t = r0 + lax.broadcasted_iota(jnp.int32, (POOL_ROWS, 1), 0)
    width = (2 << g).astype(F32)
    inv_count = 1.0 / jnp.minimum((t + 1).astype(F32), width)
    return s * inv_count - cur, inv_count


def _pool_fwd(proj, pw_g, pool_scale):
    def body(u_ref, gate_ref, pw_ref, sc_ref, y_ref):
        g = pl.program_id(0)

        def step(i, _):
            r = pl.ds(pl.multiple_of(i * POOL_ROWS, POOL_ROWS), POOL_ROWS)
            d, _ = _pool_diff(u_ref, i, g)
            mixed = _dot(d.astype(BF16), pw_ref[...])
            gate = gate_ref[r, :]
            y_ref[r, :] = (mixed * sc_ref[...] * (gate * _sig(gate))).astype(BF16)
            return 0
        lax.fori_loop(0, SEQ // POOL_ROWS, step, 0)

    return pl.pallas_call(
        body, name="pool_fwd", grid=(N_GROUPS,),
        in_specs=[pl.BlockSpec((SEQ, PG), lambda g: (0, g)),
                  pl.BlockSpec((SEQ, PG), lambda g: (0, N_GROUPS + g)),
                  pl.BlockSpec((None, PG, PG), lambda g: (g, 0, 0)),
                  pl.BlockSpec((1, PG), lambda g: (0, g))],
        out_specs=pl.BlockSpec((SEQ, PG), lambda g: (0, g)),
        out_shape=jax.ShapeDtypeStruct((SEQ, D_MODEL), BF16),
        compiler_params=_params(dimension_semantics=("arbitrary",)),
    )(proj, proj, pw_g, pool_scale)


REC_ROWS = 512
REC_CHUNKS = REC_ROWS // CHUNK
N_REC_BLK = SEQ // REC_ROWS
SEC_BLK = D_MODEL // HEAD


def _lower_bound(lb_ref):
    l0 = lb_ref[0:1, :]
    l1 = lb_ref[1:2, :]
    mx = jnp.maximum(l0, l1)
    e0 = jnp.exp(l0 - mx)
    e1 = jnp.exp(l1 - mx)
    return e0 / (e0 + e1)


def _gates(q, fl, lb):
    qs = q * _sig(q)
    sf = _sig(fl)
    f = lb + (1.0 - lb) * sf
    return qs, sf, f, 1.0 - f, jnp.log(f)


def _level_factors(big_g, qs, k):
    t = lax.broadcasted_iota(jnp.int32, (CHUNK, HEAD), 0)
    row = lambda r, n: jnp.broadcast_to(big_g[r:r + 1, :], (n, HEAD))
    out = []
    for l in range(N_LEVELS):
        m = 1 << l
        if l == 0:
            g_mid = jnp.where((t & 1) == 1, pltpu.roll(big_g, 1, 0), big_g)
        elif l == 1:
            low = (t & 7) < 4
            g_mid = jnp.concatenate([jnp.where(low[:8], row(8 * i + 1, 8), row(8 * i + 5, 8))
                                     for i in range(CHUNK // 8)], axis=0)
        else:
            g_mid = jnp.concatenate([row(b * 2 * m + m - 1, 2 * m) for b in range(CHUNK // (2 * m))], axis=0)
        up = ((t >> l) & 1) == 1
        e = jnp.exp(jnp.where(up, big_g - g_mid, g_mid - big_g))
        x = jnp.where(up, qs, k) * e
        hi = x.astype(BF16)
        out.append((hi, (x - hi.astype(F32)).astype(BF16), e, up))
    return out


CHIP_FLIPS = ((1, 0), (0, 1), (1, 1))
HBM = pl.BlockSpec(memory_space=pl.ANY)


def _place():
    return lax.axis_index("x"), lax.axis_index("y"), lax.axis_index("c")


def _remote(src, dst, send_sems, recv_sems, k, to):
    return pltpu.make_async_remote_copy(src_ref=src, dst_ref=dst, send_sem=send_sems.at[k],
                                        recv_sem=recv_sems.at[k], device_id=to, device_id_type=MESH)


def _half_rows(ref, c):
    half = ref.shape[-2] // 2
    rows = pl.ds(pl.multiple_of(c * half, half), half)
    return ref.at[:, rows, :] if len(ref.shape) == 3 else ref.at[rows, :]


class _Exchange:
    def __init__(self, inputs, out_shapes, n_sems, start, finish, aliases=None):
        self.inputs, self.out_shapes, self.n_sems = list(inputs), list(out_shapes), n_sems
        self.start, self.finish, self.aliases = start, finish, dict(aliases or {})


def _ex_swap(grads):
    def copies(ins, outs, send, recv):
        x, y, c = _place()
        return [_remote(_half_rows(g, 1 - c), o, send, recv, t, (x, y, 1 - c))
                for t, (g, o) in enumerate(zip(ins, outs))]

    def start(*refs):
        for cp in copies(*refs):
            cp.start()

    def finish(*refs):
        cps = copies(*refs)
        for cp in cps:
            cp.wait_recv()
        for cp in cps:
            cp.wait_send()

    shapes = [jax.ShapeDtypeStruct((a.shape[0], a.shape[1] // 2, a.shape[2]), F32) for a in grads]
    return _Exchange(grads, shapes, len(grads), start, finish)


def _ex_send(parts16, owners):
    def each(ins, outs, send, recv, to_sender, to_owner):
        x, y, c = _place()
        k = 0
        for t, own in enumerate(owners):
            for j, o in enumerate(own):
                for r, (fx, fy) in enumerate(CHIP_FLIPS):
                    tx, ty = x ^ fx, y ^ fy
                    cp = _remote(ins[t].at[j], outs[t].at[j, r], send, recv, k, (tx, ty, c))
                    if to_sender is not None:
                        pl.when(2 * tx + ty == o)(functools.partial(to_sender, cp))
                    if to_owner is not None:
                        pl.when(2 * x + y == o)(functools.partial(to_owner, cp))
                    k += 1

    def start(*refs):
        each(*refs, lambda cp: cp.start(), None)

    def finish(*refs):
        each(*refs, None, lambda cp: cp.wait_recv())
        each(*refs, lambda cp: cp.wait_send(), None)

    shapes = [jax.ShapeDtypeStruct((a.shape[0], len(CHIP_FLIPS)) + a.shape[1:], BF16) for a in parts16]
    return _Exchange(parts16, shapes, len(CHIP_FLIPS) * sum(len(o) for o in owners), start, finish)


def _ex_join(units, owners):
    def each(ins, outs, send, recv, fn):
        x, y, c = _place()
        k = 0
        for t, own in enumerate(owners):
            for j, o in enumerate(own):
                def half(cc, to, u=outs[t].at[j], k=k):
                    return _remote(_half_rows(u, cc), _half_rows(u, cc), send, recv, k, to)
                mine = functools.partial(half, c, (x, y, 1 - c))
                theirs = functools.partial(half, 1 - c, (x, y, c))
                pl.when(2 * x + y == o)(functools.partial(fn, mine, theirs))
                k += 1

    def start(*refs):
        each(*refs, lambda mine, theirs: mine().start())

    def finish(*refs):
        each(*refs, lambda mine, theirs: theirs().wait_recv())
        each(*refs, lambda mine, theirs: mine().wait_send())

    shapes = [jax.ShapeDtypeStruct(a.shape, F32) for a in units]
    return _Exchange(units, shapes, sum(len(o) for o in owners), start, finish,
                     aliases={t: t for t in range(len(units))})


def _ex_gather(slots):
    n_t = len(slots)
    n_fl = len(CHIP_FLIPS)

    def piece(ref, shard, half):
        return _half_rows(ref.at[shard], half)

    def first(outs, send, recv):
        x, y, c = _place()
        s = 2 * x + y
        return [_remote(piece(outs[t], s, c), piece(outs[t], s, c), send, recv, n_t * j + t, (x ^ fx, y ^ fy, c))
                for j, (fx, fy) in enumerate(CHIP_FLIPS) for t in range(n_t)]

    def start(ins, outs, send, recv):
        for cp in first(outs, send, recv):
            cp.start()

    def finish(ins, outs, send, recv):
        x, y, c = _place()
        passed = []
        for j, (fx, fy) in enumerate(CHIP_FLIPS):
            sj = 2 * (x ^ fx) + (y ^ fy)
            for t in range(n_t):
                k = n_t * j + t
                _remote(piece(outs[t], sj, c), piece(outs[t], sj, c), send, recv, k, (x, y, c)).wait_recv()
                cp = _remote(piece(outs[t], sj, c), piece(outs[t], sj, c), send, recv, n_t * n_fl + k, (x, y, 1 - c))
                cp.start()
                passed.append(cp)
        for j, (fx, fy) in enumerate(CHIP_FLIPS):
            sj = 2 * (x ^ fx) + (y ^ fy)
            for t in range(n_t):
                k = n_t * n_fl + n_t * j + t
                _remote(piece(outs[t], sj, 1 - c), piece(outs[t], sj, 1 - c), send, recv, k, (x, y, c)).wait_recv()
        for cp in first(outs, send, recv) + passed:
            cp.wait_send()

    shapes = [jax.ShapeDtypeStruct(a.shape, BF16) for a in slots]
    return _Exchange(slots, shapes, 2 * n_t * n_fl, start, finish, aliases={t: t for t in range(n_t)})


def _call(body, *, name, args=(), in_specs=(), out_specs=(), out_shape=(), grid=(), scratch_shapes=(),
          exchanges=()):
    n_in, n_out, n_scr = len(args), len(out_shape), len(scratch_shapes)
    ex_in, ex_out, ex_scr, spans, alias = [], [], [], [], {}
    for ex in exchanges:
        spans.append((len(ex_in), len(ex.inputs), len(ex_out), len(ex.out_shapes)))
        for i, o in ex.aliases.items():
            alias[n_in + len(ex_in) + i] = n_out + len(ex_out) + o
        ex_in += ex.inputs
        ex_out += ex.out_shapes
        ex_scr += [pltpu.SemaphoreType.DMA((ex.n_sems,)), pltpu.SemaphoreType.DMA((ex.n_sems,))]

    def full(*refs):
        ins, x_in = refs[:n_in], refs[n_in:n_in + len(ex_in)]
        outs = refs[n_in + len(ex_in):n_in + len(ex_in) + n_out]
        x_out = refs[n_in + len(ex_in) + n_out:n_in + len(ex_in) + n_out + len(ex_out)]
        scr = refs[len(refs) - n_scr - len(ex_scr):len(refs) - len(ex_scr)]
        sems = refs[len(refs) - len(ex_scr):]

        def run(which):
            for e, (ex, (i0, ni, o0, no)) in enumerate(zip(exchanges, spans)):
                getattr(ex, which)(x_in[i0:i0 + ni], x_out[o0:o0 + no], sems[2 * e], sems[2 * e + 1])

        if grid:
            ids = [pl.program_id(a) for a in range(len(grid))]
            is_first = functools.reduce(jnp.logical_and, [i == 0 for i in ids])
            is_last = functools.reduce(jnp.logical_and, [i == g - 1 for i, g in zip(ids, grid)])
            pl.when(is_first)(lambda: run("start"))
            body(*ins, *outs, *scr)
            pl.when(is_last)(lambda: run("finish"))
        else:
            run("start")
            if body is not None:
                body(*ins, *outs, *scr)
            run("finish")

    kw = dict(grid=grid) if grid else {}
    if grid:
        kw["compiler_params"] = _params(dimension_semantics=("arbitrary",) * len(grid))
    else:
        kw["compiler_params"] = _params()
    res = pl.pallas_call(
        full, name=name,
        in_specs=list(in_specs) + [HBM] * len(ex_in),
        out_specs=list(out_specs) + [HBM] * len(ex_out),
        out_shape=list(out_shape) + ex_out,
        scratch_shapes=list(scratch_shapes) + ex_scr,
        input_output_aliases=alias, **kw,
    )(*args, *ex_in)
    own = list(res[:n_out])
    per_ex = [list(res[n_out + o0:n_out + o0 + no]) for (_, _, o0, no) in spans]
    return own, per_ex


def _cast_own(w, place, name):
    rows, cols = w.shape
    tile = min(rows, 256)

    def body(place_ref, w_ref, o_ref):
        o_ref[...] = w_ref[...].astype(BF16)

    return pl.pallas_call(
        body, name=name,
        grid_spec=pltpu.PrefetchScalarGridSpec(
            num_scalar_prefetch=1, grid=(rows // tile,),
            in_specs=[pl.BlockSpec((tile, cols), lambda i, p: (i, 0))],
            out_specs=pl.BlockSpec((None, tile, cols), lambda i, p: (p[1], i, 0))),
        out_shape=jax.ShapeDtypeStruct((N_SHARDS, rows, cols), BF16),
        compiler_params=_params(dimension_semantics=("arbitrary",)),
    )(place, w)


def _gather_small(parts):
    def body(p_ref, out_ref, send_sems, recv_sems, local_sem):
        x, y, c = _place()
        me = 4 * x + 2 * y + c
        mine = pltpu.make_async_copy(p_ref, out_ref.at[me], local_sem)
        mine.start()
        sends = []
        for mask in range(1, 8):
            to = (x ^ (mask >> 2), y ^ ((mask >> 1) & 1), c ^ (mask & 1))
            cp = _remote(p_ref, out_ref.at[me], send_sems, recv_sems, mask - 1, to)
            cp.start()
            sends.append(cp)
        for mask in range(1, 8):
            frm = me ^ mask
            _remote(p_ref, out_ref.at[frm], send_sems, recv_sems, mask - 1, (x, y, c)).wait_recv()
        for cp in sends:
            cp.wait_send()
        mine.wait()

    vmem = pl.BlockSpec(memory_space=pltpu.VMEM)
    return pl.pallas_call(
        body, name="gather_small", in_specs=[vmem], out_specs=vmem,
        out_shape=jax.ShapeDtypeStruct((8,) + parts.shape, F32),
        scratch_shapes=[pltpu.SemaphoreType.DMA((7,)), pltpu.SemaphoreType.DMA((7,)),
                        pltpu.SemaphoreType.DMA],
        compiler_params=_params(),
    )(parts)


def _rec_fwd(proj, lb_logits, rec_g, tri, lvl, exchanges):
    def body(q_ref, f_ref, i_ref, rg_ref, lb_ref, g_ref, w_ref, lvl_ref, y_ref, o_ref, stp_ref, st_ref):
        @pl.when(pl.program_id(1) == 0)
        def _():
            st_ref[...] = jnp.zeros_like(st_ref)
        lb = _lower_bound(lb_ref)
        st = st_ref[...]
        for c in range(REC_CHUNKS):
            r = pl.ds(c * CHUNK, CHUNK)
            v = i_ref[r, :]
            rg = rg_ref[r, :]
            qs, _, _, k, g = _gates(q_ref[r, :], f_ref[r, :], lb)
            big_g = _dot3(w_ref[...], g)
            a = jnp.zeros((CHUNK, CHUNK), F32)
            lv = lvl_ref[...]
            for l, (xl, _, _, _) in enumerate(_level_factors(big_g, qs, k)):
                a = a + jnp.where(lv == l, _dot_nt(xl, xl), 0.0)
            stp_ref[c] = st
            vb = v.astype(BF16)
            diag = jnp.sum(qs * k, axis=-1, keepdims=True)
            o = (_dot(a.astype(BF16), vb) + diag * v
                 + _dot_nt((qs * jnp.exp(big_g)).astype(BF16), st.astype(BF16)))
            g_last = big_g[CHUNK - 1:CHUNK, :]
            kdec = (k * jnp.exp(g_last - big_g)).astype(BF16)
            st = st * jnp.exp(g_last) + _dot_tn(vb, kdec)
            o_ref[r, :] = o
            inv = lax.rsqrt(jnp.mean(o * o, axis=-1, keepdims=True) + EPS)
            y_ref[r, :] = (o * inv * g_ref[...] * (rg * _sig(rg))).astype(BF16)
        st_ref[...] = st

    sec = lambda n: pl.BlockSpec((REC_ROWS, HEAD), lambda h, b: (b, n * SEC_BLK + h))
    vec = lambda rows: pl.BlockSpec((rows, HEAD), lambda h, b: (0, h))
    full = lambda a: pl.BlockSpec(a.shape, lambda h, b: (0,) * a.ndim)
    return _call(
        body, name="rec_fwd", grid=(N_HEADS, N_REC_BLK),
        args=(proj, proj, proj, proj, lb_logits, rec_g, tri, lvl),
        in_specs=[sec(2), sec(3), sec(4), sec(5), vec(2), vec(1), full(tri), full(lvl)],
        out_specs=[pl.BlockSpec((REC_ROWS, HEAD), lambda h, b: (b, h)),
                   pl.BlockSpec((REC_ROWS, HEAD), lambda h, b: (b, h)),
                   pl.BlockSpec((None, REC_CHUNKS, HEAD, HEAD), lambda h, b: (h, b, 0, 0))],
        out_shape=[jax.ShapeDtypeStruct((SEQ, D_MODEL), BF16),
                   jax.ShapeDtypeStruct((SEQ, D_MODEL), F32),
                   jax.ShapeDtypeStruct((N_HEADS, SEQ // CHUNK, HEAD, HEAD), F32)],
        scratch_shapes=[pltpu.VMEM((HEAD, HEAD), F32)],
        exchanges=exchanges)


OUT_ROWS = 256


def _out_proj_loss(y_pool, y_rec, w_out_g, x, target, gf):
    def body(yp_ref, yr_ref, w_ref, x_ref, t_ref, gf_ref, dout_ref, doutb_ref, part_ref):
        @pl.when(pl.program_id(0) == 0)
        def _():
            part_ref[...] = jnp.zeros_like(part_ref)
        out = (x_ref[...] + _dot(yp_ref[...], w_ref[0:D_MODEL, :])
               + _dot(yr_ref[...], w_ref[D_MODEL:2 * D_MODEL, :]))
        inv = lax.rsqrt(jnp.mean(out * out, axis=-1, keepdims=True) + EPS)
        gf_v = gf_ref[...]
        diff = out * inv * gf_v - t_ref[...]
        dyf = diff * (1.0 / D_MODEL)
        a = dyf * gf_v
        dout = inv * a - out * (inv * inv * inv) * jnp.mean(a * out, axis=-1, keepdims=True)
        dout_ref[...] = dout
        doutb_ref[...] = dout.astype(BF16)
        part_ref[0:1, :] += jnp.sum(dyf * out * inv, axis=0, keepdims=True)
        part_ref[1:2, :] += jnp.sum(diff * diff, axis=0, keepdims=True)

    row = lambda n: pl.BlockSpec((OUT_ROWS, n), lambda i: (i, 0))
    return pl.pallas_call(
        body, name="out_proj_loss", grid=(SEQ // OUT_ROWS,),
        in_specs=[row(D_MODEL), row(D_MODEL), pl.BlockSpec((2 * D_MODEL, D_MODEL), lambda i: (0, 0)),
                  row(D_MODEL), row(D_MODEL), pl.BlockSpec((1, D_MODEL), lambda i: (0, 0))],
        out_specs=[row(D_MODEL), row(D_MODEL), pl.BlockSpec((8, D_MODEL), lambda i: (0, 0))],
        out_shape=[jax.ShapeDtypeStruct((SEQ, D_MODEL), F32),
                   jax.ShapeDtypeStruct((SEQ, D_MODEL), BF16),
                   jax.ShapeDtypeStruct((8, D_MODEL), F32)],
        compiler_params=_params(dimension_semantics=("arbitrary",)),
    )(y_pool, y_rec, w_out_g, x, target, gf)


def _grad_w_out(y_pool, y_rec, dout_b):
    blk = W_OUT_SHARD // 2
    per = D_MODEL // blk
    n = 2 * per

    def body(yp_ref, yr_ref, d_ref, p32_ref, p16_ref, send_ref, recv_ref, send_sems, recv_sems):
        j = pl.program_id(0)
        x, y, c = _place()

        def copy(u):
            return _remote(send_ref.at[u], recv_ref.at[u], send_sems, recv_sems, u, (x, y, 1 - c))

        for i in range(n):
            @pl.when(j == i)
            def _(i=i):
                res = _dot_tn((yp_ref if i < per else yr_ref)[...], d_ref[...])

                @pl.when(i % 2 == c)
                def _():
                    p32_ref[i // 2] = res

                @pl.when(i % 2 != c)
                def _():
                    send_ref[i // 2] = res
                    copy(i // 2).start()

        @pl.when(j == n - 1)
        def _():
            for u in range(N_SHARDS):
                copy(u).wait_recv()
                tot = p32_ref[u] + recv_ref[u]
                p32_ref[u] = tot
                p16_ref[u] = tot.astype(BF16)
            for u in range(N_SHARDS):
                copy(u).wait_send()

    whole = pl.BlockSpec((N_SHARDS, blk, D_MODEL), lambda j: (0, 0, 0))
    return pl.pallas_call(
        body, name="grad_w_out", grid=(n,),
        in_specs=[pl.BlockSpec((SEQ, blk), lambda j: (0, jnp.minimum(j, per - 1))),
                  pl.BlockSpec((SEQ, blk), lambda j: (0, jnp.maximum(j - per, 0))),
                  pl.BlockSpec((SEQ, D_MODEL), lambda j: (0, 0))],
        out_specs=[whole, whole],
        out_shape=[jax.ShapeDtypeStruct((N_SHARDS, blk, D_MODEL), F32),
                   jax.ShapeDtypeStruct((N_SHARDS, blk, D_MODEL), BF16)],
        scratch_shapes=[pltpu.VMEM((N_SHARDS, blk, D_MODEL), F32), pltpu.VMEM((N_SHARDS, blk, D_MODEL), F32),
                        pltpu.SemaphoreType.DMA((N_SHARDS,)), pltpu.SemaphoreType.DMA((N_SHARDS,))],
        compiler_params=_params(dimension_semantics=("arbitrary",)),
    )(y_pool, y_rec, dout_b)


def _pool_bwd(proj, dout_b, w_out_g, pw_g, pool_scale, exchanges):
    n = POOL_ROWS + POOL_HALO

    def body(u_ref, gate_ref, d_ref, wo_ref, pw_ref, sc_ref,
             du_ref, dgate_ref, dpw_ref, dsc_ref, dd_ref, ddw_ref):
        g = pl.program_id(0)
        dpw_ref[...] = jnp.zeros_like(dpw_ref)
        dsc_ref[...] = jnp.zeros_like(dsc_ref)

        def first(i, _):
            r = pl.ds(pl.multiple_of(i * POOL_ROWS, POOL_ROWS), POOL_ROWS)
            d, inv_count = _pool_diff(u_ref, i, g)
            db = d.astype(BF16)
            mixed = _dot(db, pw_ref[...])
            gate = gate_ref[r, :]
            sg = _sig(gate)
            silu = gate * sg
            dy = _dot_nt(d_ref[r, :], wo_ref[...])
            sc = sc_ref[...]
            dmixed = dy * silu * sc
            dgate_ref[r, :] = (dy * mixed * sc * (sg * (1.0 + gate * (1.0 - sg)))).astype(BF16)
            dsc_ref[...] += jnp.sum(dy * silu * mixed, axis=0, keepdims=True)
            dmb = dmixed.astype(BF16)
            dpw_ref[...] += _dot_tn(db, dmb)
            dd = _dot_nt(dmb, pw_ref[...])
            dd_ref[r, :] = dd
            ddw_ref[r, :] = dd * inv_count
            return 0
        lax.fori_loop(0, SEQ // POOL_ROWS, first, 0)

        def second(i, _):
            r0 = i * POOL_ROWS
            r = pl.ds(pl.multiple_of(r0, POOL_ROWS), POOL_ROWS)
            last = i == SEQ // POOL_ROWS - 1
            after = ddw_ref[pl.ds(pl.multiple_of(jnp.minimum(r0 + POOL_ROWS, SEQ - POOL_HALO), 8), POOL_HALO), :]
            after = jnp.where(last, 0.0, after)
            ext = jnp.concatenate([ddw_ref[r, :], after], axis=0)
            s = _window_sums(ext, g, lambda k: n - (1 << k))[:POOL_ROWS, :]
            du_ref[r, :] = (s - dd_ref[r, :]).astype(BF16)
            return 0
        lax.fori_loop(0, SEQ // POOL_ROWS, second, 0)

    return _call(
        body, name="pool_bwd", grid=(N_GROUPS,),
        args=(proj, proj, dout_b, w_out_g, pw_g, pool_scale),
        in_specs=[pl.BlockSpec((SEQ, PG), lambda g: (0, g)),
                  pl.BlockSpec((SEQ, PG), lambda g: (0, N_GROUPS + g)),
                  pl.BlockSpec((SEQ, D_MODEL), lambda g: (0, 0)),
                  pl.BlockSpec((PG, D_MODEL), lambda g: (g, 0)),
                  pl.BlockSpec((None, PG, PG), lambda g: (g, 0, 0)),
                  pl.BlockSpec((1, PG), lambda g: (0, g))],
        out_specs=[pl.BlockSpec((SEQ, PG), lambda g: (0, g)),
                   pl.BlockSpec((SEQ, PG), lambda g: (0, g)),
                   pl.BlockSpec((None, PG, PG), lambda g: (g, 0, 0)),
                   pl.BlockSpec((1, PG), lambda g: (0, g))],
        out_shape=[jax.ShapeDtypeStruct((SEQ, D_MODEL), BF16),
                   jax.ShapeDtypeStruct((SEQ, D_MODEL), BF16),
                   jax.ShapeDtypeStruct((N_GROUPS, PG, PG), F32),
                   jax.ShapeDtypeStruct((1, D_MODEL), F32)],
        scratch_shapes=[pltpu.VMEM((SEQ, PG), F32), pltpu.VMEM((SEQ, PG), F32)],
        exchanges=exchanges)


HALF_HEADS = N_HEADS // 2
HALF_COLS = HALF_HEADS * HEAD


def _rec_bwd(proj, o_raw, st_prev, dout_b, w_out_g, lb_logits, rec_g, tri, tri_t, lvl, lvl_t, h0, name, exchanges):
    def body(q_ref, f_ref, i_ref, rg_ref, o_ref, stp_ref, d_ref, wo_ref, lb_ref, g_ref,
             w_ref, lvl_ref, lvlt_ref, tri_ref,
             dq_ref, df_ref, di_ref, drg_ref, part_ref, dst_ref):
        @pl.when(pl.program_id(1) == 0)
        def _():
            dst_ref[...] = jnp.zeros_like(dst_ref)
            part_ref[...] = jnp.zeros_like(part_ref)
        lb = _lower_bound(lb_ref)
        grec = g_ref[...]
        dst = dst_ref[...]
        acc_grec = jnp.zeros((1, HEAD), F32)
        acc_lb = jnp.zeros((1, HEAD), F32)
        for c in reversed(range(REC_CHUNKS)):
            r = pl.ds(c * CHUNK, CHUNK)
            q = q_ref[r, :]
            v = i_ref[r, :]
            rg = rg_ref[r, :]
            o = o_ref[r, :]
            dy = _dot_nt(d_ref[r, :], wo_ref[...])
            sg = _sig(rg)
            silu = rg * sg
            inv = lax.rsqrt(jnp.mean(o * o, axis=-1, keepdims=True) + EPS)
            recn = o * inv
            drg_ref[r, :] = (dy * recn * grec * (sg * (1.0 + rg * (1.0 - sg)))).astype(BF16)
            acc_grec = acc_grec + jnp.sum(dy * silu * recn, axis=0, keepdims=True)
            drecn = dy * silu * grec
            do = inv * drecn - o * (inv * inv * inv) * jnp.mean(drecn * o, axis=-1, keepdims=True)
            qs, sf, f, k, g = _gates(q, f_ref[r, :], lb)
            big_g = _dot3(w_ref[...], g)
            g_last = big_g[CHUNK - 1:CHUNK, :]
            e_g = jnp.exp(big_g)
            e_rev = jnp.exp(g_last - big_g)
            e_last = jnp.exp(g_last)
            levels = _level_factors(big_g, qs, k)
            lv = lvl_ref[...]
            lvt = lvlt_ref[...]
            a_t = jnp.zeros((CHUNK, CHUNK), F32)
            for l, (xl, _, _, _) in enumerate(levels):
                a_t = a_t + jnp.where(lvt == l, _dot_nt(xl, xl), 0.0)
            dob = do.astype(BF16)
            vb = v.astype(BF16)
            d_a = _dot_nt(dob, vb)
            d_at = _dot_nt(vb, dob)
            stp = stp_ref[c]
            dstb = dst.astype(BF16)
            q_g = qs * e_g
            kdec = k * e_rev
            diag = jnp.sum(qs * k, axis=-1, keepdims=True)
            dv = _dot(a_t.astype(BF16), dob) + diag * do + _dot_nt(kdec.astype(BF16), dstb)
            dq_g = _dot(dob, stp.astype(BF16))
            dkdec = _dot(vb, dstb)
            de_last = jnp.sum(stp * dst, axis=0, keepdims=True)
            dst = dst * e_last + _dot_tn(dob, q_g.astype(BF16))
            dqs_i = jnp.zeros((CHUNK, HEAD), F32)
            dk_i = jnp.zeros((CHUNK, HEAD), F32)
            for l, (xl, xlo, e, up) in enumerate(levels):
                z = jnp.where(lv == l, d_a, jnp.where(lvt == l, d_at, 0.0))
                tmp = _dot(z.astype(BF16), jnp.concatenate([xl, xlo], axis=-1))
                tmp = (tmp[:, :HEAD] + tmp[:, HEAD:]) * e
                dqs_i = dqs_i + jnp.where(up, tmp, 0.0)
                dk_i = dk_i + jnp.where(up, 0.0, tmp)
            ddiag = jnp.sum(do * v, axis=-1, keepdims=True)
            dqs = dqs_i + ddiag * k + dq_g * e_g
            dk = dk_i + ddiag * qs + dkdec * e_rev
            dg_rev = dkdec * kdec
            dg_last = jnp.sum(dg_rev, axis=0, keepdims=True) + de_last * e_last
            dbig_g = qs * dqs_i - k * dk_i + dq_g * q_g - dg_rev
            dg = _dot3(tri_ref[...], dbig_g) + dg_last
            df = dg / f - dk
            df_ref[r, :] = (df * (1.0 - lb) * sf * (1.0 - sf)).astype(BF16)
            acc_lb = acc_lb + jnp.sum(df * (1.0 - sf), axis=0, keepdims=True)
            sq = _sig(q)
            dq_ref[r, :] = (dqs * (sq * (1.0 + q * (1.0 - sq)))).astype(BF16)
            di_ref[r, :] = dv.astype(BF16)
        dst_ref[...] = dst
        part_ref[0:1, :] += acc_grec
        part_ref[1:2, :] += acc_lb

    rev = lambda b: N_REC_BLK - 1 - b
    sec = lambda n: pl.BlockSpec((REC_ROWS, HEAD), lambda h, b: (rev(b), n * SEC_BLK + h0 + h))
    col_in = pl.BlockSpec((REC_ROWS, HEAD), lambda h, b: (rev(b), h0 + h))
    col = pl.BlockSpec((REC_ROWS, HEAD), lambda h, b: (rev(b), h))
    vec_in = lambda rows: pl.BlockSpec((rows, HEAD), lambda h, b: (0, h0 + h))
    full = lambda a: pl.BlockSpec(a.shape, lambda h, b: (0,) * a.ndim)
    return _call(
        body, name=name, grid=(HALF_HEADS, N_REC_BLK),
        args=(proj, proj, proj, proj, o_raw, st_prev, dout_b, w_out_g, lb_logits, rec_g, tri, lvl, lvl_t, tri_t),
        in_specs=[sec(2), sec(3), sec(4), sec(5), col_in,
                  pl.BlockSpec((None, REC_CHUNKS, HEAD, HEAD), lambda h, b: (h0 + h, rev(b), 0, 0)),
                  pl.BlockSpec((REC_ROWS, D_MODEL), lambda h, b: (rev(b), 0)),
                  pl.BlockSpec((HEAD, D_MODEL), lambda h, b: (SEC_BLK + h0 + h, 0)),
                  vec_in(2), vec_in(1), full(tri), full(lvl), full(lvl_t), full(tri_t)],
        out_specs=[col, col, col, col, pl.BlockSpec((8, HEAD), lambda h, b: (0, h))],
        out_shape=[jax.ShapeDtypeStruct((SEQ, HALF_COLS), BF16)] * 4
                  + [jax.ShapeDtypeStruct((8, HALF_COLS), F32)],
        scratch_shapes=[pltpu.VMEM((HEAD, HEAD), F32)],
        exchanges=exchanges)


def _w_in_block(w_ref, j):
    per_shard = W_IN_SHARD // COL_BLK
    return w_ref[j // per_shard, :, (j % per_shard) * COL_BLK:(j % per_shard + 1) * COL_BLK]


def _grad_x(dproj, w_in_g, x, g1, dout, exchanges):
    rows = 256
    n_blk = len(dproj)

    def body(*refs):
        dp_refs = refs[:n_blk]
        w_ref, x_ref, g_ref, dout_ref, dx_ref, part_ref = refs[n_blk:]

        @pl.when(pl.program_id(0) == 0)
        def _():
            part_ref[...] = jnp.zeros_like(part_ref)
        dh = jnp.zeros((rows, D_MODEL), F32)
        for j in range(n_blk):
            dh = dh + _dot_nt(dp_refs[j][...], _w_in_block(w_ref, j))
        xv = x_ref[...]
        inv = lax.rsqrt(jnp.mean(xv * xv, axis=-1, keepdims=True) + EPS)
        a = dh * g_ref[...]
        dx_ref[...] = (dout_ref[...] + inv * a
                       - xv * (inv * inv * inv) * jnp.mean(a * xv, axis=-1, keepdims=True))
        part_ref[0:1, :] += jnp.sum(dh * xv * inv, axis=0, keepdims=True)

    row = lambda: pl.BlockSpec((rows, D_MODEL), lambda i: (i, 0))
    dp_spec = lambda cb: pl.BlockSpec((rows, COL_BLK), lambda i: (i, cb))
    return _call(
        body, name="grad_x", grid=(SEQ // rows,),
        args=tuple(a for a, _ in dproj) + (w_in_g, x, g1, dout),
        in_specs=[dp_spec(cb) for _, cb in dproj]
                 + [pl.BlockSpec((N_SHARDS, D_MODEL, W_IN_SHARD), lambda i: (0, 0, 0)),
                    row(), pl.BlockSpec((1, D_MODEL), lambda i: (0, 0)), row()],
        out_specs=[row(), pl.BlockSpec((8, D_MODEL), lambda i: (0, 0))],
        out_shape=[jax.ShapeDtypeStruct((SEQ, D_MODEL), F32),
                   jax.ShapeDtypeStruct((8, D_MODEL), F32)],
        exchanges=exchanges)


def _grad_w_in(h, blocks, name):
    n_blk = len(blocks)
    half = D_MODEL // 2

    def body(h_ref, *refs):
        dp_refs = refs[:n_blk]
        p32_ref, p16_ref, send_ref, recv_ref, send_sems, recv_sems = refs[n_blk:]
        j = pl.program_id(0)
        x, y, c = _place()
        cols = lambda cc: pl.ds(pl.multiple_of(cc * half, half), half)

        def copy(i):
            return _remote(send_ref.at[i], recv_ref.at[i], send_sems, recv_sems, i, (x, y, 1 - c))

        for i in range(n_blk):
            @pl.when(j == i)
            def _(i=i):
                send_ref[i] = _dot_tn(h_ref[:, cols(1 - c)], dp_refs[i][...])
                copy(i).start()
                p32_ref[i] = _dot_tn(h_ref[:, cols(c)], dp_refs[i][...])

        @pl.when(j == n_blk - 1)
        def _():
            for i in range(n_blk):
                copy(i).wait_recv()
                tot = p32_ref[i] + recv_ref[i]
                p32_ref[i] = tot
                p16_ref[i] = tot.astype(BF16)
            for i in range(n_blk):
                copy(i).wait_send()

    whole = pl.BlockSpec((n_blk, half, COL_BLK), lambda j: (0, 0, 0))
    return pl.pallas_call(
        body, name=name, grid=(n_blk,),
        in_specs=[pl.BlockSpec((SEQ, D_MODEL), lambda j: (0, 0))]
                 + [pl.BlockSpec((SEQ, COL_BLK), lambda j, cb=cb: (0, cb)) for _, cb in blocks],
        out_specs=[whole, whole],
        out_shape=[jax.ShapeDtypeStruct((n_blk, half, COL_BLK), F32),
                   jax.ShapeDtypeStruct((n_blk, half, COL_BLK), BF16)],
        scratch_shapes=[pltpu.VMEM((n_blk, half, COL_BLK), F32), pltpu.VMEM((n_blk, half, COL_BLK), F32),
                        pltpu.SemaphoreType.DMA((n_blk,)), pltpu.SemaphoreType.DMA((n_blk,))],
        compiler_params=_params(dimension_semantics=("arbitrary",)),
    )(h, *[a for a, _ in blocks])


def _add_units(grad, recv, place, tile, name):
    n, rows, cols = grad.shape
    per_half = rows // 2 // tile

    def body(place_ref, g_ref, r_ref, o32_ref, o16_ref):
        v = g_ref[...] + r_ref[...]
        o32_ref[...] = v
        o16_ref[...] = v.astype(BF16)

    blk = lambda f: pl.BlockSpec((None, tile, cols), f)
    out = lambda s, i, p: (s, i, 0)
    return pl.pallas_call(
        body, name=name,
        grid_spec=pltpu.PrefetchScalarGridSpec(
            num_scalar_prefetch=1, grid=(n, per_half),
            in_specs=[blk(lambda s, i, p: (s, p[0] * per_half + i, 0)), blk(out)],
            out_specs=[blk(out), blk(out)]),
        out_shape=[jax.ShapeDtypeStruct(recv.shape, F32), jax.ShapeDtypeStruct(recv.shape, BF16)],
        compiler_params=_params(dimension_semantics=("arbitrary", "arbitrary")),
    )(place, grad, recv)


def _sum_units(part32, recv16, place, tile, name):
    n, half, cols = part32.shape
    per_half = half // tile

    def body(place_ref, p_ref, r_ref, o_ref):
        acc = p_ref[...]
        for j in range(len(CHIP_FLIPS)):
            acc = acc + r_ref[j].astype(F32)
        o_ref[...] = acc

    return pl.pallas_call(
        body, name=name,
        grid_spec=pltpu.PrefetchScalarGridSpec(
            num_scalar_prefetch=1, grid=(n, per_half),
            in_specs=[pl.BlockSpec((None, tile, cols), lambda s, i, p: (s, i, 0)),
                      pl.BlockSpec((None, len(CHIP_FLIPS), tile, cols), lambda s, i, p: (s, 0, i, 0))],
            out_specs=pl.BlockSpec((None, tile, cols), lambda s, i, p: (s, p[0] * per_half + i, 0))),
        out_shape=jax.ShapeDtypeStruct((n, 2 * half, cols), F32),
        compiler_params=_params(dimension_semantics=("arbitrary", "arbitrary")),
    )(place, part32, recv16)


def _adamw_math(w, g, m, v):
    m = ADAM_B1 * m + (1.0 - ADAM_B1) * g
    v = ADAM_B2 * v + (1.0 - ADAM_B2) * (g * g)
    m_hat = m / (1.0 - ADAM_B1 ** ADAM_STEP)
    v_hat = v / (1.0 - ADAM_B2 ** ADAM_STEP)
    delta = -ADAM_LR * (m_hat / (jnp.sqrt(v_hat) + ADAM_EPS) + ADAM_WD * w)
    return delta, m, v


def _adamw_units(w, m, v, grads, pick, name):
    rows, cols = w.shape
    bc = grads[0].shape[-1]
    tile = min(rows, 256)
    n_g = len(grads)

    def body(pick_ref, w_ref, m_ref, v_ref, *refs):
        g_refs, (g_out, d_ref, nm_ref, nv_ref) = refs[:n_g], refs[n_g:]
        p = pl.program_id(0)
        for a in range(n_g):
            @pl.when(pick_ref[0, p] == a)
            def _(a=a):
                g = g_refs[a][...]
                g_out[...] = g
                d_ref[...], nm_ref[...], nv_ref[...] = _adamw_math(w_ref[...], g, m_ref[...], v_ref[...])

    blk = pl.BlockSpec((tile, bc), lambda p, i, pick: (i, p))

    def g_spec(a):
        return pl.BlockSpec((None, tile, bc),
                            lambda p, i, pick: (jnp.where(pick[0, p] == a, pick[1, p], 0), i, 0))

    return pl.pallas_call(
        body, name=name,
        grid_spec=pltpu.PrefetchScalarGridSpec(
            num_scalar_prefetch=1, grid=(cols // bc, rows // tile),
            in_specs=[blk] * 3 + [g_spec(a) for a in range(n_g)],
            out_specs=[blk] * 4),
        out_shape=[jax.ShapeDtypeStruct(w.shape, F32)] * 4,
        compiler_params=_params(dimension_semantics=("arbitrary", "arbitrary")),
    )(pick, w, m, v, *grads)


ROW_NORM1, ROW_SCALE, ROW_LB, ROW_REC, ROW_FINAL, ROW_LOSS = 0, 1, 2, 4, 5, 6


def _small_update(gathered, w, m, v):
    def body(p_ref, w_ref, m_ref, v_ref, loss_ref, g_ref, d_ref, nm_ref, nv_ref):
        tot = p_ref[0]
        for d in range(1, 8):
            tot = tot + p_ref[d]
        wv = w_ref[...]
        l0 = wv[ROW_LB:ROW_LB + 1, :]
        l1 = wv[ROW_LB + 1:ROW_LB + 2, :]
        mx = jnp.maximum(l0, l1)
        e0 = jnp.exp(l0 - mx)
        e1 = jnp.exp(l1 - mx)
        lb = e0 / (e0 + e1)
        dl0 = tot[ROW_LB:ROW_LB + 1, :] * lb * (1.0 - lb)
        row = lax.broadcasted_iota(jnp.int32, tot.shape, 0)
        g = jnp.where(row == ROW_LB, dl0, jnp.where(row == ROW_LB + 1, -dl0, tot))
        g = jnp.where(row >= ROW_LOSS, 0.0, g)
        g_ref[...] = g
        d_ref[...], nm_ref[...], nv_ref[...] = _adamw_math(wv, g, m_ref[...], v_ref[...])
        loss_ref[...] = (0.5 / D_MODEL) * jnp.sum(tot[ROW_LOSS:ROW_LOSS + 1, :], axis=-1, keepdims=True)

    return pl.pallas_call(
        body, name="small_update",
        out_shape=[jax.ShapeDtypeStruct((1, 1), F32)] + [jax.ShapeDtypeStruct(w.shape, F32)] * 4,
        compiler_params=_params(),
    )(gathered, w, m, v)


SHARD_OWNERS = tuple(range(N_SHARDS))
BLOCKS_POOL = (0, 1, 2, 3)
BLOCKS_A = (4, 6, 8, 10)
BLOCKS_B = (5, 7, 9, 11)
BLOCK_GROUPS = (BLOCKS_POOL, BLOCKS_A, BLOCKS_B)


def _block_owners(blocks):
    return tuple(j // (W_IN_SHARD // COL_BLK) for j in blocks)


def _small_rows(norm1, scale, lb, rec, final):
    pad = jnp.zeros((2, D_MODEL), F32)
    return jnp.concatenate([norm1, scale, lb, rec, final.reshape(1, D_MODEL), pad], axis=0)


def kernel(x, norm1_g, w_in, pool_w, pool_scale, lb_logits, rec_norm_g, w_out, final_norm_g, loss_target, m_norm1_g, m_w_in, m_pool_w, m_pool_scale, m_lb_logits, m_rec_norm_g, m_w_out, m_final_norm_g, v_norm1_g, v_w_in, v_pool_w, v_pool_scale, v_lb_logits, v_rec_norm_g, v_w_out, v_final_norm_g):
    xi, yi, ci = _place()
    chip = 2 * xi + yi
    place = jnp.stack([ci, chip]).astype(jnp.int32)
    pw_rows = N_GROUPS * PW_SHARD
    flat_pw = lambda a: a.reshape(pw_rows, PG)
    x2, target, gf = x[0], loss_target[0], final_norm_g.reshape(1, D_MODEL)
    tri, tri_t, lvl, lvl_t = _chunk_constants()
    tri, tri_t = jnp.asarray(tri, BF16), jnp.asarray(tri_t, BF16)
    lvl, lvl_t = jnp.asarray(lvl), jnp.asarray(lvl_t)

    proj, h, w_in_g = _in_proj(x2, norm1_g, _cast_own(w_in[0], place, "cast_w_in"), place)
    (y_rec, o_raw, st_prev), ((w_out_g, pw_g),) = _rec_fwd(
        proj, lb_logits, rec_norm_g, tri, lvl,
        [_ex_gather([_cast_own(w_out[0], place, "cast_w_out"), _cast_own(flat_pw(pool_w), place, "cast_pool_w")])])
    w_out_g = w_out_g.reshape(2 * D_MODEL, D_MODEL)
    pw_full = pw_g.reshape(N_SHARDS, N_GROUPS, PW_SHARD, PG).transpose(1, 0, 2, 3).reshape(N_GROUPS, PG, PG)
    y_pool = _pool_fwd(proj, pw_full, pool_scale)
    dout, dout_b, part_out = _out_proj_loss(y_pool, y_rec, w_out_g, x2, target, gf)

    p_out32, p_out16 = _grad_w_out(y_pool, y_rec, dout_b)
    (du, dgate, gpw, dscale), ((rb_out,),) = _pool_bwd(proj, dout_b, w_out_g, pw_full, pool_scale,
                                                       [_ex_send([p_out16], [SHARD_OWNERS])])
    g_out = _sum_units(p_out32, rb_out, place, 256, "sum_w_out")
    gpw = gpw.reshape(N_GROUPS, N_SHARDS, PW_SHARD, PG).transpose(1, 0, 2, 3).reshape(N_SHARDS, pw_rows, PG)
    p_inp32, p_inp16 = _grad_w_in(h, [(du, 0), (du, 1), (dgate, 0), (dgate, 1)], "grad_w_in_pool")

    rec_args = (proj, o_raw, st_prev, dout_b, w_out_g, lb_logits, rec_norm_g, tri, tri_t, lvl, lvl_t)
    (dq_a, df_a, di_a, drg_a, part_a), ((rb_inp,), (ra_pw,)) = _rec_bwd(
        *rec_args, 0, "rec_bwd_a", [_ex_send([p_inp16], [_block_owners(BLOCKS_POOL)]), _ex_swap([gpw])])
    g_inp = _sum_units(p_inp32, rb_inp, place, 256, "sum_w_in_pool")
    p_pw32, p_pw16 = _add_units(gpw, ra_pw, place, 128, "add_pool_w")
    p_ina32, p_ina16 = _grad_w_in(h, [(dq_a, 0), (df_a, 0), (di_a, 0), (drg_a, 0)], "grad_w_in_a")

    (dq_b, df_b, di_b, drg_b, part_b), ((rb_ina, rb_pw),) = _rec_bwd(
        *rec_args, HALF_HEADS, "rec_bwd_b",
        [_ex_send([p_ina16, p_pw16], [_block_owners(BLOCKS_A), SHARD_OWNERS])])
    g_ina = _sum_units(p_ina32, rb_ina, place, 256, "sum_w_in_a")
    g_pw = _sum_units(p_pw32, rb_pw, place, 128, "sum_pool_w")
    p_inb32, p_inb16 = _grad_w_in(h, [(dq_b, 0), (df_b, 0), (di_b, 0), (drg_b, 0)], "grad_w_in_b")

    dproj = [(du, 0), (du, 1), (dgate, 0), (dgate, 1), (dq_a, 0), (dq_b, 0), (df_a, 0), (df_b, 0),
             (di_a, 0), (di_b, 0), (drg_a, 0), (drg_b, 0)]
    (dx, part_x), ((rb_inb,),) = _grad_x(dproj, w_in_g, x2, norm1_g, dout,
                                         [_ex_send([p_inb16], [_block_owners(BLOCKS_B)])])
    g_inb = _sum_units(p_inb32, rb_inb, place, 256, "sum_w_in_b")
    _, ((g_out, g_pw, g_inp, g_ina, g_inb),) = _call(
        None, name="join_halves",
        exchanges=[_ex_join([g_out, g_pw, g_inp, g_ina, g_inb],
                            [SHARD_OWNERS, SHARD_OWNERS] + [_block_owners(b) for b in BLOCK_GROUPS])])

    group_of = np.zeros((D_PROJ // COL_BLK,), np.int32)
    index_of = np.zeros((D_PROJ // COL_BLK,), np.int32)
    for gi, blocks in enumerate(BLOCK_GROUPS):
        for i, j in enumerate(blocks):
            group_of[j], index_of[j] = gi, i
    per_shard = W_IN_SHARD // COL_BLK
    pick_in = jnp.stack([lax.dynamic_slice(jnp.asarray(group_of), (per_shard * chip,), (per_shard,)),
                         lax.dynamic_slice(jnp.asarray(index_of), (per_shard * chip,), (per_shard,))])
    pick_own = jnp.stack([jnp.zeros((1,), jnp.int32), chip.reshape(1).astype(jnp.int32)])
    big = [_adamw_units(w_in[0], m_w_in[0], v_w_in[0], [g_inp, g_ina, g_inb], pick_in, "adamw_w_in"),
           _adamw_units(w_out[0], m_w_out[0], v_w_out[0], [g_out], pick_own, "adamw_w_out"),
           _adamw_units(flat_pw(pool_w), flat_pw(m_pool_w), flat_pw(v_pool_w), [g_pw], pick_own, "adamw_pool_w")]

    zero = jnp.zeros((1, D_MODEL), F32)
    part_rec = jnp.concatenate([part_a, part_b], axis=1)
    parts = jnp.concatenate([part_x[0:1], dscale, part_rec[1:2], zero, part_rec[0:1], part_out[0:1],
                             part_out[1:2], zero], axis=0)
    small_w = _small_rows(norm1_g, pool_scale, lb_logits, rec_norm_g, final_norm_g)
    small_m = _small_rows(m_norm1_g, m_pool_scale, m_lb_logits, m_rec_norm_g, m_final_norm_g)
    small_v = _small_rows(v_norm1_g, v_pool_scale, v_lb_logits, v_rec_norm_g, v_final_norm_g)
    loss, *small = _small_update(_gather_small(parts), small_w, small_m, small_v)

    def leaves(k):
        s = small[k]
        return (s[ROW_NORM1:ROW_NORM1 + 1], big[0][k][None], big[2][k].reshape(pool_w.shape),
                s[ROW_SCALE:ROW_SCALE + 1], s[ROW_LB:ROW_LB + 2], s[ROW_REC:ROW_REC + 1],
                big[1][k][None], s[ROW_FINAL])

    return (loss.reshape(()), dx[None], *leaves(0), *leaves(1), *leaves(2), *leaves(3))
```

```python
import functools

import numpy as np
import jax
import jax.numpy as jnp
from jax import lax
from jax.experimental import pallas as pl
from jax.experimental.pallas import tpu as pltpu

F32 = jnp.float32
BF16 = jnp.bfloat16

SEQ = 2048
D_MODEL = 1024
D_PROJ = 6144
N_SEC = 6
N_GROUPS = 4
PG = 256
N_HEADS = 8
HEAD = 128
CHUNK = 64
N_LEVELS = 6
N_SHARDS = 4
W_IN_SHARD = D_PROJ // N_SHARDS
W_OUT_SHARD = 2048 // N_SHARDS
PW_SHARD = PG // N_SHARDS
COL_BLK = 512
EPS = 1e-6

ADAM_LR = 0.001
ADAM_B1 = 0.9
ADAM_B2 = 0.999
ADAM_EPS = 1e-08
ADAM_WD = 0.01
ADAM_STEP = 10

V7X_VMEM_LIMIT = 56 * 1024 * 1024
MESH = pl.DeviceIdType.MESH


def _params(**kw):
    return pltpu.CompilerParams(vmem_limit_bytes=V7X_VMEM_LIMIT, **kw)


def _sig(x):
    return 1.0 / (1.0 + jnp.exp(-x))


def _dot(a, b):
    return jnp.dot(a, b, preferred_element_type=F32)


def _dot_nt(a, b):
    return lax.dot_general(a, b, (((1,), (1,)), ((), ())), preferred_element_type=F32)


def _dot_tn(a, b):
    return lax.dot_general(a, b, (((0,), (0,)), ((), ())), preferred_element_type=F32)


def _split3(a):
    p1 = a.astype(BF16)
    r1 = a - p1.astype(F32)
    p2 = r1.astype(BF16)
    p3 = (r1 - p2.astype(F32)).astype(BF16)
    return jnp.concatenate([p1, p2, p3], axis=-1)


def _dot3(w01, a):
    n = a.shape[-1]
    r = _dot(w01, _split3(a))
    return r[:, :n] + r[:, n:2 * n] + r[:, 2 * n:]


def _chunk_constants():
    j = np.arange(CHUNK)
    tt, ss = np.meshgrid(j, j, indexing="ij")
    x = tt ^ ss
    hb = np.full((CHUNK, CHUNK), -1, np.int32)
    for l in range(N_LEVELS):
        hb[x >= (1 << l)] = l
    lvl = np.where(tt > ss, hb, -1).astype(np.int32)
    tri = (ss <= tt).astype(np.float32)
    return tri, np.ascontiguousarray(tri.T), lvl, np.ascontiguousarray(lvl.T)


def _in_proj(x, g1, w_slots, place):
    n_col = D_PROJ // COL_BLK
    per_shard = W_IN_SHARD // COL_BLK
    n_fl = len(CHIP_FLIPS)
    rows = 1024

    def shard_at(m, chip):
        return chip ^ jnp.where(m == 0, 0, jnp.where(m == 1, 2, jnp.where(m == 2, 1, 3)))

    def body(place_ref, x_ref, g_ref, w_in_ref, proj_ref, h_ref, w_ref, wbuf, load_sems, send_sems, recv_sems):
        t = pl.program_id(0)
        x_, y_, c = _place()
        chip = 2 * x_ + y_
        piece = lambda shard, half: _half_rows(w_ref.at[shard], half)

        def sends():
            return [_remote(piece(chip, c), piece(chip, c), send_sems, recv_sems, j, (x_ ^ fx, y_ ^ fy, c))
                    for j, (fx, fy) in enumerate(CHIP_FLIPS)]

        def passed_on(j, half, to):
            sj = shard_at(j + 1, chip)
            return _remote(piece(sj, half), piece(sj, half), send_sems, recv_sems, n_fl + j, to)

        def load(shard, p):
            return pltpu.make_async_copy(w_ref.at[shard, :, pl.ds(p * COL_BLK, COL_BLK)], wbuf.at[p],
                                         load_sems.at[p])

        @pl.when(t == 0)
        def _():
            for cp in sends():
                cp.start()

            def norm(i, _):
                r = pl.ds(pl.multiple_of(i * rows, rows), rows)
                xv = x_ref[r, :]
                inv = lax.rsqrt(jnp.mean(xv * xv, axis=-1, keepdims=True) + EPS)
                h_ref[r, :] = (xv * inv * g_ref[...]).astype(BF16)
                return 0
            lax.fori_loop(0, SEQ // rows, norm, 0)

        for m in range(N_SHARDS):
            @pl.when(t == per_shard * m)
            def _(m=m):
                if m > 0:
                    j = m - 1
                    sj = shard_at(m, chip)
                    _remote(piece(sj, c), piece(sj, c), send_sems, recv_sems, j, (x_, y_, c)).wait_recv()
                    passed_on(j, c, (x_, y_, 1 - c)).start()
                    passed_on(j, 1 - c, (x_, y_, c)).wait_recv()
                for p in range(per_shard):
                    load(shard_at(m, chip), p).start()

        for p in range(per_shard):
            @pl.when(t % per_shard == p)
            def _(p=p):
                load(0, p).wait()

                def mm(i, _):
                    r = pl.ds(pl.multiple_of(i * rows, rows), rows)
                    proj_ref[r, :] = _dot(h_ref[r, :], wbuf[p])
                    return 0
                lax.fori_loop(0, SEQ // rows, mm, 0)

        @pl.when(t == n_col - 1)
        def _():
            for cp in sends():
                cp.wait_send()
            for j in range(n_fl):
                passed_on(j, c, (x_, y_, 1 - c)).wait_send()

    return pl.pallas_call(
        body, name="in_proj",
        grid_spec=pltpu.PrefetchScalarGridSpec(
            num_scalar_prefetch=1, grid=(n_col,),
            in_specs=[pl.BlockSpec((SEQ, D_MODEL), lambda t, p: (0, 0)),
                      pl.BlockSpec((1, D_MODEL), lambda t, p: (0, 0)),
                      pl.BlockSpec(memory_space=pl.ANY)],
            out_specs=[pl.BlockSpec((SEQ, COL_BLK),
                                    lambda t, p: (0, per_shard * shard_at(t // per_shard, p[1]) + t % per_shard)),
                       pl.BlockSpec((SEQ, D_MODEL), lambda t, p: (0, 0)),
                       pl.BlockSpec(memory_space=pl.ANY)],
            scratch_shapes=[pltpu.VMEM((per_shard, D_MODEL, COL_BLK), BF16),
                            pltpu.SemaphoreType.DMA((per_shard,)),
                            pltpu.SemaphoreType.DMA((2 * n_fl,)), pltpu.SemaphoreType.DMA((2 * n_fl,))]),
        out_shape=[jax.ShapeDtypeStruct((SEQ, D_PROJ), F32),
                   jax.ShapeDtypeStruct((SEQ, D_MODEL), BF16),
                   jax.ShapeDtypeStruct(w_slots.shape, BF16)],
        input_output_aliases={3: 2},
        compiler_params=_params(dimension_semantics=("arbitrary",)),
    )(place, x, g1, w_slots)


POOL_ROWS = 256
POOL_HALO = 16


def _window_sums(ext, g, shift_of):
    s = ext
    for k in range(N_GROUPS):
        s = jnp.where(k <= g, s + pltpu.roll(s, shift_of(k), 0), s)
    return s


def _pool_diff(u_ref, i, g):
    n = POOL_ROWS + POOL_HALO
    r0 = i * POOL_ROWS
    cur = u_ref[pl.ds(pl.multiple_of(r0, POOL_ROWS), POOL_ROWS), :]
    before = u_ref[pl.ds(pl.multiple_of(jnp.maximum(r0 - POOL_HALO, 0), 8), POOL_HALO), :]
    before = jnp.where(i > 0, before, 0.0)
    ext = jnp.concatenate([before, cur], axis=0)
    s = _window_sums(ext, g, lambda k: 1 << k)[POOL_HALO:, :]
    t = r0 + lax.broadcasted_iota(jnp.int32, (POOL_ROWS, 1), 0)
    width = (2 << g).astype(F32)
    inv_count = 1.0 / jnp.minimum((t + 1).astype(F32), width)
    return s * inv_count - cur, inv_count


def _pool_fwd(proj, pw_g, pool_scale):
    def body(u_ref, gate_ref, pw_ref, sc_ref, y_ref):
        g = pl.program_id(0)

        def step(i, _):
            r = pl.ds(pl.multiple_of(i * POOL_ROWS, POOL_ROWS), POOL_ROWS)
            d, _ = _pool_diff(u_ref, i, g)
            mixed = _dot(d.astype(BF16), pw_ref[...])
            gate = gate_ref[r, :]
            y_ref[r, :] = (mixed * sc_ref[...] * (gate * _sig(gate))).astype(BF16)
            return 0
        lax.fori_loop(0, SEQ // POOL_ROWS, step, 0)

    return pl.pallas_call(
        body, name="pool_fwd", grid=(N_GROUPS,),
        in_specs=[pl.BlockSpec((SEQ, PG), lambda g: (0, g)),
                  pl.BlockSpec((SEQ, PG), lambda g: (0, N_GROUPS + g)),
                  pl.BlockSpec((None, PG, PG), lambda g: (g, 0, 0)),
                  pl.BlockSpec((1, PG), lambda g: (0, g))],
        out_specs=pl.BlockSpec((SEQ, PG), lambda g: (0, g)),
        out_shape=jax.ShapeDtypeStruct((SEQ, D_MODEL), BF16),
        compiler_params=_params(dimension_semantics=("arbitrary",)),
    )(proj, proj, pw_g, pool_scale)


REC_ROWS = 1024
REC_CHUNKS = REC_ROWS // CHUNK
N_REC_BLK = SEQ // REC_ROWS
SEC_BLK = D_MODEL // HEAD


def _lower_bound(lb_ref):
    l0 = lb_ref[0:1, :]
    l1 = lb_ref[1:2, :]
    mx = jnp.maximum(l0, l1)
    e0 = jnp.exp(l0 - mx)
    e1 = jnp.exp(l1 - mx)
    return e0 / (e0 + e1)


def _gates(q, fl, lb):
    qs = q * _sig(q)
    sf = _sig(fl)
    f = lb + (1.0 - lb) * sf
    return qs, sf, f, 1.0 - f, jnp.log(f)


def _level_factors(big_g, qs, k):
    t = lax.broadcasted_iota(jnp.int32, (CHUNK, HEAD), 0)
    row = lambda r, n: jnp.broadcast_to(big_g[r:r + 1, :], (n, HEAD))
    out = []
    for l in range(N_LEVELS):
        m = 1 << l
        if l == 0:
            g_mid = jnp.where((t & 1) == 1, pltpu.roll(big_g, 1, 0), big_g)
        elif l == 1:
            low = (t & 7) < 4
            g_mid = jnp.concatenate([jnp.where(low[:8], row(8 * i + 1, 8), row(8 * i + 5, 8))
                                     for i in range(CHUNK // 8)], axis=0)
        else:
            g_mid = jnp.concatenate([row(b * 2 * m + m - 1, 2 * m) for b in range(CHUNK // (2 * m))], axis=0)
        up = ((t >> l) & 1) == 1
        e = jnp.exp(jnp.where(up, big_g - g_mid, g_mid - big_g))
        x = jnp.where(up, qs, k) * e
        hi = x.astype(BF16)
        out.append((hi, (x - hi.astype(F32)).astype(BF16), e, up))
    return out


CHIP_FLIPS = ((1, 0), (0, 1), (1, 1))
HBM = pl.BlockSpec(memory_space=pl.ANY)


def _place():
    return lax.axis_index("x"), lax.axis_index("y"), lax.axis_index("c")


def _remote(src, dst, send_sems, recv_sems, k, to):
    return pltpu.make_async_remote_copy(src_ref=src, dst_ref=dst, send_sem=send_sems.at[k],
                                        recv_sem=recv_sems.at[k], device_id=to, device_id_type=MESH)


def _half_rows(ref, c):
    half = ref.shape[-2] // 2
    rows = pl.ds(pl.multiple_of(c * half, half), half)
    return ref.at[:, rows, :] if len(ref.shape) == 3 else ref.at[rows, :]


class _Exchange:
    def __init__(self, inputs, out_shapes, n_sems, start, finish, aliases=None):
        self.inputs, self.out_shapes, self.n_sems = list(inputs), list(out_shapes), n_sems
        self.start, self.finish, self.aliases = start, finish, dict(aliases or {})


def _ex_swap(grads):
    def copies(ins, outs, send, recv):
        x, y, c = _place()
        return [_remote(_half_rows(g, 1 - c), o, send, recv, t, (x, y, 1 - c))
                for t, (g, o) in enumerate(zip(ins, outs))]

    def start(*refs):
        for cp in copies(*refs):
            cp.start()

    def finish(*refs):
        cps = copies(*refs)
        for cp in cps:
            cp.wait_recv()
        for cp in cps:
            cp.wait_send()

    shapes = [jax.ShapeDtypeStruct((a.shape[0], a.shape[1] // 2, a.shape[2]), F32) for a in grads]
    return _Exchange(grads, shapes, len(grads), start, finish)


def _ex_send(parts16, owners):
    def each(ins, outs, send, recv, to_sender, to_owner):
        x, y, c = _place()
        k = 0
        for t, own in enumerate(owners):
            for j, o in enumerate(own):
                for r, (fx, fy) in enumerate(CHIP_FLIPS):
                    tx, ty = x ^ fx, y ^ fy
                    cp = _remote(ins[t].at[j], outs[t].at[j, r], send, recv, k, (tx, ty, c))
                    if to_sender is not None:
                        pl.when(2 * tx + ty == o)(functools.partial(to_sender, cp))
                    if to_owner is not None:
                        pl.when(2 * x + y == o)(functools.partial(to_owner, cp))
                    k += 1

    def start(*refs):
        each(*refs, lambda cp: cp.start(), None)

    def finish(*refs):
        each(*refs, None, lambda cp: cp.wait_recv())
        each(*refs, lambda cp: cp.wait_send(), None)

    shapes = [jax.ShapeDtypeStruct((a.shape[0], len(CHIP_FLIPS)) + a.shape[1:], BF16) for a in parts16]
    return _Exchange(parts16, shapes, len(CHIP_FLIPS) * sum(len(o) for o in owners), start, finish)


def _ex_join(units, owners):
    def each(ins, outs, send, recv, fn):
        x, y, c = _place()
        k = 0
        for t, own in enumerate(owners):
            for j, o in enumerate(own):
                def half(cc, to, u=outs[t].at[j], k=k):
                    return _remote(_half_rows(u, cc), _half_rows(u, cc), send, recv, k, to)
                mine = functools.partial(half, c, (x, y, 1 - c))
                theirs = functools.partial(half, 1 - c, (x, y, c))
                pl.when(2 * x + y == o)(functools.partial(fn, mine, theirs))
                k += 1

    def start(*refs):
        each(*refs, lambda mine, theirs: mine().start())

    def finish(*refs):
        each(*refs, lambda mine, theirs: theirs().wait_recv())
        each(*refs, lambda mine, theirs: mine().wait_send())

    shapes = [jax.ShapeDtypeStruct(a.shape, F32) for a in units]
    return _Exchange(units, shapes, sum(len(o) for o in owners), start, finish,
                     aliases={t: t for t in range(len(units))})


def _ex_gather(slots):
    n_t = len(slots)
    n_fl = len(CHIP_FLIPS)

    def piece(ref, shard, half):
        return _half_rows(ref.at[shard], half)

    def first(outs, send, recv):
        x, y, c = _place()
        s = 2 * x + y
        return [_remote(piece(outs[t], s, c), piece(outs[t], s, c), send, recv, n_t * j + t, (x ^ fx, y ^ fy, c))
                for j, (fx, fy) in enumerate(CHIP_FLIPS) for t in range(n_t)]

    def start(ins, outs, send, recv):
        for cp in first(outs, send, recv):
            cp.start()

    def finish(ins, outs, send, recv):
        x, y, c = _place()
        passed = []
        for j, (fx, fy) in enumerate(CHIP_FLIPS):
            sj = 2 * (x ^ fx) + (y ^ fy)
            for t in range(n_t):
                k = n_t * j + t
                _remote(piece(outs[t], sj, c), piece(outs[t], sj, c), send, recv, k, (x, y, c)).wait_recv()
                cp = _remote(piece(outs[t], sj, c), piece(outs[t], sj, c), send, recv, n_t * n_fl + k, (x, y, 1 - c))
                cp.start()
                passed.append(cp)
        for j, (fx, fy) in enumerate(CHIP_FLIPS):
            sj = 2 * (x ^ fx) + (y ^ fy)
            for t in range(n_t):
                k = n_t * n_fl + n_t * j + t
                _remote(piece(outs[t], sj, 1 - c), piece(outs[t], sj, 1 - c), send, recv, k, (x, y, c)).wait_recv()
        for cp in first(outs, send, recv) + passed:
            cp.wait_send()

    shapes = [jax.ShapeDtypeStruct(a.shape, BF16) for a in slots]
    return _Exchange(slots, shapes, 2 * n_t * n_fl, start, finish, aliases={t: t for t in range(n_t)})


def _ex_gather_small(parts):
    def copies(ins, outs, send, recv):
        x, y, c = _place()
        me = 4 * x + 2 * y + c
        return [_remote(ins[0], outs[0].at[me], send, recv, mask - 1,
                        (x ^ (mask >> 2), y ^ ((mask >> 1) & 1), c ^ (mask & 1))) for mask in range(1, 8)]

    def start(*refs):
        for cp in copies(*refs):
            cp.start()

    def finish(ins, outs, send, recv):
        x, y, c = _place()
        me = 4 * x + 2 * y + c
        for mask in range(1, 8):
            _remote(ins[0], outs[0].at[me ^ mask], send, recv, mask - 1, (x, y, c)).wait_recv()
        for cp in copies(ins, outs, send, recv):
            cp.wait_send()

    return _Exchange([parts], [jax.ShapeDtypeStruct((8,) + parts.shape, F32)], 7, start, finish)


def _call(body, *, name, args=(), in_specs=(), out_specs=(), out_shape=(), grid=(), scratch_shapes=(),
          exchanges=()):
    n_in, n_out, n_scr = len(args), len(out_shape), len(scratch_shapes)
    ex_in, ex_out, ex_scr, spans, alias = [], [], [], [], {}
    for ex in exchanges:
        spans.append((len(ex_in), len(ex.inputs), len(ex_out), len(ex.out_shapes)))
        for i, o in ex.aliases.items():
            alias[n_in + len(ex_in) + i] = n_out + len(ex_out) + o
        ex_in += ex.inputs
        ex_out += ex.out_shapes
        ex_scr += [pltpu.SemaphoreType.DMA((ex.n_sems,)), pltpu.SemaphoreType.DMA((ex.n_sems,))]

    def full(*refs):
        ins, x_in = refs[:n_in], refs[n_in:n_in + len(ex_in)]
        outs = refs[n_in + len(ex_in):n_in + len(ex_in) + n_out]
        x_out = refs[n_in + len(ex_in) + n_out:n_in + len(ex_in) + n_out + len(ex_out)]
        scr = refs[len(refs) - n_scr - len(ex_scr):len(refs) - len(ex_scr)]
        sems = refs[len(refs) - len(ex_scr):]

        def run(which):
            for e, (ex, (i0, ni, o0, no)) in enumerate(zip(exchanges, spans)):
                getattr(ex, which)(x_in[i0:i0 + ni], x_out[o0:o0 + no], sems[2 * e], sems[2 * e + 1])

        if grid:
            ids = [pl.program_id(a) for a in range(len(grid))]
            is_first = functools.reduce(jnp.logical_and, [i == 0 for i in ids])
            is_last = functools.reduce(jnp.logical_and, [i == g - 1 for i, g in zip(ids, grid)])
            pl.when(is_first)(lambda: run("start"))
            body(*ins, *outs, *scr)
            pl.when(is_last)(lambda: run("finish"))
        else:
            run("start")
            if body is not None:
                body(*ins, *outs, *scr)
            run("finish")

    kw = dict(grid=grid) if grid else {}
    if grid:
        kw["compiler_params"] = _params(dimension_semantics=("arbitrary",) * len(grid))
    else:
        kw["compiler_params"] = _params()
    res = pl.pallas_call(
        full, name=name,
        in_specs=list(in_specs) + [HBM] * len(ex_in),
        out_specs=list(out_specs) + [HBM] * len(ex_out),
        out_shape=list(out_shape) + ex_out,
        scratch_shapes=list(scratch_shapes) + ex_scr,
        input_output_aliases=alias, **kw,
    )(*args, *ex_in)
    own = list(res[:n_out])
    per_ex = [list(res[n_out + o0:n_out + o0 + no]) for (_, _, o0, no) in spans]
    return own, per_ex


def _cast_own(w, place, name):
    rows, cols = w.shape
    tile = min(rows, 256)

    def body(place_ref, w_ref, o_ref):
        o_ref[...] = w_ref[...].astype(BF16)

    return pl.pallas_call(
        body, name=name,
        grid_spec=pltpu.PrefetchScalarGridSpec(
            num_scalar_prefetch=1, grid=(rows // tile,),
            in_specs=[pl.BlockSpec((tile, cols), lambda i, p: (i, 0))],
            out_specs=pl.BlockSpec((None, tile, cols), lambda i, p: (p[1], i, 0))),
        out_shape=jax.ShapeDtypeStruct((N_SHARDS, rows, cols), BF16),
        compiler_params=_params(dimension_semantics=("arbitrary",)),
    )(place, w)


def _rec_fwd(proj, lb_logits, rec_g, tri, lvl, exchanges):
    def body(q_ref, f_ref, i_ref, rg_ref, lb_ref, g_ref, w_ref, lvl_ref, y_ref, o_ref, stp_ref, st_ref):
        @pl.when(pl.program_id(1) == 0)
        def _():
            st_ref[...] = jnp.zeros_like(st_ref)
        lb = _lower_bound(lb_ref)
        st = st_ref[...]
        for c in range(REC_CHUNKS):
            r = pl.ds(c * CHUNK, CHUNK)
            v = i_ref[r, :]
            rg = rg_ref[r, :]
            qs, _, _, k, g = _gates(q_ref[r, :], f_ref[r, :], lb)
            big_g = _dot3(w_ref[...], g)
            a = jnp.zeros((CHUNK, CHUNK), F32)
            lv = lvl_ref[...]
            for l, (xl, _, _, _) in enumerate(_level_factors(big_g, qs, k)):
                a = a + jnp.where(lv == l, _dot_nt(xl, xl), 0.0)
            stp_ref[c] = st
            vb = v.astype(BF16)
            diag = jnp.sum(qs * k, axis=-1, keepdims=True)
            o = (_dot(a.astype(BF16), vb) + diag * v
                 + _dot_nt((qs * jnp.exp(big_g)).astype(BF16), st.astype(BF16)))
            g_last = big_g[CHUNK - 1:CHUNK, :]
            kdec = (k * jnp.exp(g_last - big_g)).astype(BF16)
            st = st * jnp.exp(g_last) + _dot_tn(vb, kdec)
            o_ref[r, :] = o
            inv = lax.rsqrt(jnp.mean(o * o, axis=-1, keepdims=True) + EPS)
            y_ref[r, :] = (o * inv * g_ref[...] * (rg * _sig(rg))).astype(BF16)
        st_ref[...] = st

    sec = lambda n: pl.BlockSpec((REC_ROWS, HEAD), lambda h, b: (b, n * SEC_BLK + h))
    vec = lambda rows: pl.BlockSpec((rows, HEAD), lambda h, b: (0, h))
    full = lambda a: pl.BlockSpec(a.shape, lambda h, b: (0,) * a.ndim)
    return _call(
        body, name="rec_fwd", grid=(N_HEADS, N_REC_BLK),
        args=(proj, proj, proj, proj, lb_logits, rec_g, tri, lvl),
        in_specs=[sec(2), sec(3), sec(4), sec(5), vec(2), vec(1), full(tri), full(lvl)],
        out_specs=[pl.BlockSpec((REC_ROWS, HEAD), lambda h, b: (b, h)),
                   pl.BlockSpec((REC_ROWS, HEAD), lambda h, b: (b, h)),
                   pl.BlockSpec((None, REC_CHUNKS, HEAD, HEAD), lambda h, b: (h, b, 0, 0))],
        out_shape=[jax.ShapeDtypeStruct((SEQ, D_MODEL), BF16),
                   jax.ShapeDtypeStruct((SEQ, D_MODEL), F32),
                   jax.ShapeDtypeStruct((N_HEADS, SEQ // CHUNK, HEAD, HEAD), F32)],
        scratch_shapes=[pltpu.VMEM((HEAD, HEAD), F32)],
        exchanges=exchanges)


OUT_ROWS = 256


def _out_proj_loss(y_pool, y_rec, w_out_g, x, target, gf):
    def body(yp_ref, yr_ref, w_ref, x_ref, t_ref, gf_ref, dout_ref, doutb_ref, part_ref):
        @pl.when(pl.program_id(0) == 0)
        def _():
            part_ref[...] = jnp.zeros_like(part_ref)
        out = (x_ref[...] + _dot(yp_ref[...], w_ref[0:D_MODEL, :])
               + _dot(yr_ref[...], w_ref[D_MODEL:2 * D_MODEL, :]))
        inv = lax.rsqrt(jnp.mean(out * out, axis=-1, keepdims=True) + EPS)
        gf_v = gf_ref[...]
        diff = out * inv * gf_v - t_ref[...]
        dyf = diff * (1.0 / D_MODEL)
        a = dyf * gf_v
        dout = inv * a - out * (inv * inv * inv) * jnp.mean(a * out, axis=-1, keepdims=True)
        dout_ref[...] = dout
        doutb_ref[...] = dout.astype(BF16)
        part_ref[0:1, :] += jnp.sum(dyf * out * inv, axis=0, keepdims=True)
        part_ref[1:2, :] += jnp.sum(diff * diff, axis=0, keepdims=True)

    row = lambda n: pl.BlockSpec((OUT_ROWS, n), lambda i: (i, 0))
    return pl.pallas_call(
        body, name="out_proj_loss", grid=(SEQ // OUT_ROWS,),
        in_specs=[row(D_MODEL), row(D_MODEL), pl.BlockSpec((2 * D_MODEL, D_MODEL), lambda i: (0, 0)),
                  row(D_MODEL), row(D_MODEL), pl.BlockSpec((1, D_MODEL), lambda i: (0, 0))],
        out_specs=[row(D_MODEL), row(D_MODEL), pl.BlockSpec((8, D_MODEL), lambda i: (0, 0))],
        out_shape=[jax.ShapeDtypeStruct((SEQ, D_MODEL), F32),
                   jax.ShapeDtypeStruct((SEQ, D_MODEL), BF16),
                   jax.ShapeDtypeStruct((8, D_MODEL), F32)],
        compiler_params=_params(dimension_semantics=("arbitrary",)),
    )(y_pool, y_rec, w_out_g, x, target, gf)


def _grad_w_out(y_pool, y_rec, dout_b):
    blk = W_OUT_SHARD // 2
    per = D_MODEL // blk
    n = 2 * per

    def body(yp_ref, yr_ref, d_ref, p32_ref, p16_ref, send_ref, recv_ref, send_sems, recv_sems):
        j = pl.program_id(0)
        x, y, c = _place()

        def copy(u):
            return _remote(send_ref.at[u], recv_ref.at[u], send_sems, recv_sems, u, (x, y, 1 - c))

        for i in range(n):
            @pl.when(j == i)
            def _(i=i):
                res = _dot_tn((yp_ref if i < per else yr_ref)[...], d_ref[...])

                @pl.when(i % 2 == c)
                def _():
                    p32_ref[i // 2] = res

                @pl.when(i % 2 != c)
                def _():
                    send_ref[i // 2] = res
                    copy(i // 2).start()

        @pl.when(j == n - 1)
        def _():
            for u in range(N_SHARDS):
                copy(u).wait_recv()
                tot = p32_ref[u] + recv_ref[u]
                p32_ref[u] = tot
                p16_ref[u] = tot.astype(BF16)
            for u in range(N_SHARDS):
                copy(u).wait_send()

    whole = pl.BlockSpec((N_SHARDS, blk, D_MODEL), lambda j: (0, 0, 0))
    return pl.pallas_call(
        body, name="grad_w_out", grid=(n,),
        in_specs=[pl.BlockSpec((SEQ, blk), lambda j: (0, jnp.minimum(j, per - 1))),
                  pl.BlockSpec((SEQ, blk), lambda j: (0, jnp.maximum(j - per, 0))),
                  pl.BlockSpec((SEQ, D_MODEL), lambda j: (0, 0))],
        out_specs=[whole, whole],
        out_shape=[jax.ShapeDtypeStruct((N_SHARDS, blk, D_MODEL), F32),
                   jax.ShapeDtypeStruct((N_SHARDS, blk, D_MODEL), BF16)],
        scratch_shapes=[pltpu.VMEM((N_SHARDS, blk, D_MODEL), F32), pltpu.VMEM((N_SHARDS, blk, D_MODEL), F32),
                        pltpu.SemaphoreType.DMA((N_SHARDS,)), pltpu.SemaphoreType.DMA((N_SHARDS,))],
        compiler_params=_params(dimension_semantics=("arbitrary",)),
    )(y_pool, y_rec, dout_b)


def _pool_bwd(proj, dout_b, w_out_g, pw_g, pool_scale, exchanges):
    n = POOL_ROWS + POOL_HALO

    def body(u_ref, gate_ref, d_ref, wo_ref, pw_ref, sc_ref,
             dp_ref, dpw_ref, dsc_ref, dd_ref, ddw_ref):
        g = pl.program_id(0)
        dpw_ref[...] = jnp.zeros_like(dpw_ref)
        dsc_ref[...] = jnp.zeros_like(dsc_ref)

        def first(i, _):
            r = pl.ds(pl.multiple_of(i * POOL_ROWS, POOL_ROWS), POOL_ROWS)
            d, inv_count = _pool_diff(u_ref, i, g)
            db = d.astype(BF16)
            mixed = _dot(db, pw_ref[...])
            gate = gate_ref[r, :]
            sg = _sig(gate)
            silu = gate * sg
            dy = _dot_nt(d_ref[r, :], wo_ref[...])
            sc = sc_ref[...]
            dmixed = dy * silu * sc
            dp_ref[1, r, :] = (dy * mixed * sc * (sg * (1.0 + gate * (1.0 - sg)))).astype(BF16)
            dsc_ref[...] += jnp.sum(dy * silu * mixed, axis=0, keepdims=True)
            dmb = dmixed.astype(BF16)
            dpw_ref[...] += _dot_tn(db, dmb)
            dd = _dot_nt(dmb, pw_ref[...])
            dd_ref[r, :] = dd
            ddw_ref[r, :] = dd * inv_count
            return 0
        lax.fori_loop(0, SEQ // POOL_ROWS, first, 0)

        def second(i, _):
            r0 = i * POOL_ROWS
            r = pl.ds(pl.multiple_of(r0, POOL_ROWS), POOL_ROWS)
            last = i == SEQ // POOL_ROWS - 1
            after = ddw_ref[pl.ds(pl.multiple_of(jnp.minimum(r0 + POOL_ROWS, SEQ - POOL_HALO), 8), POOL_HALO), :]
            after = jnp.where(last, 0.0, after)
            ext = jnp.concatenate([ddw_ref[r, :], after], axis=0)
            s = _window_sums(ext, g, lambda k: n - (1 << k))[:POOL_ROWS, :]
            dp_ref[0, r, :] = (s - dd_ref[r, :]).astype(BF16)
            return 0
        lax.fori_loop(0, SEQ // POOL_ROWS, second, 0)

    return _call(
        body, name="pool_bwd", grid=(N_GROUPS,),
        args=(proj, proj, dout_b, w_out_g, pw_g, pool_scale),
        in_specs=[pl.BlockSpec((SEQ, PG), lambda g: (0, g)),
                  pl.BlockSpec((SEQ, PG), lambda g: (0, N_GROUPS + g)),
                  pl.BlockSpec((SEQ, D_MODEL), lambda g: (0, 0)),
                  pl.BlockSpec((PG, D_MODEL), lambda g: (g, 0)),
                  pl.BlockSpec((None, PG, PG), lambda g: (g, 0, 0)),
                  pl.BlockSpec((1, PG), lambda g: (0, g))],
        out_specs=[pl.BlockSpec((2, SEQ, PG), lambda g: (0, 0, g)),
                   pl.BlockSpec((None, PG, PG), lambda g: (g, 0, 0)),
                   pl.BlockSpec((1, PG), lambda g: (0, g))],
        out_shape=[jax.ShapeDtypeStruct((2, SEQ, D_MODEL), BF16),
                   jax.ShapeDtypeStruct((N_GROUPS, PG, PG), F32),
                   jax.ShapeDtypeStruct((1, D_MODEL), F32)],
        scratch_shapes=[pltpu.VMEM((SEQ, PG), F32), pltpu.VMEM((SEQ, PG), F32)],
        exchanges=exchanges)


HALF_HEADS = N_HEADS // 2
HALF_COLS = HALF_HEADS * HEAD


def _rec_bwd(proj, o_raw, st_prev, dout_b, w_out_g, lb_logits, rec_g, tri, tri_t, lvl, lvl_t, h0, name, exchanges):
    def body(q_ref, f_ref, i_ref, rg_ref, o_ref, stp_ref, d_ref, wo_ref, lb_ref, g_ref,
             w_ref, lvl_ref, lvlt_ref, tri_ref,
             dr_ref, part_ref, dst_ref):
        @pl.when(pl.program_id(1) == 0)
        def _():
            dst_ref[...] = jnp.zeros_like(dst_ref)
            part_ref[...] = jnp.zeros_like(part_ref)
        lb = _lower_bound(lb_ref)
        grec = g_ref[...]
        dst = dst_ref[...]
        acc_grec = jnp.zeros((1, HEAD), F32)
        acc_lb = jnp.zeros((1, HEAD), F32)
        for c in reversed(range(REC_CHUNKS)):
            r = pl.ds(c * CHUNK, CHUNK)
            q = q_ref[r, :]
            v = i_ref[r, :]
            rg = rg_ref[r, :]
            o = o_ref[r, :]
            dy = _dot_nt(d_ref[r, :], wo_ref[...])
            sg = _sig(rg)
            silu = rg * sg
            inv = lax.rsqrt(jnp.mean(o * o, axis=-1, keepdims=True) + EPS)
            recn = o * inv
            dr_ref[3, r, :] = (dy * recn * grec * (sg * (1.0 + rg * (1.0 - sg)))).astype(BF16)
            acc_grec = acc_grec + jnp.sum(dy * silu * recn, axis=0, keepdims=True)
            drecn = dy * silu * grec
            do = inv * drecn - o * (inv * inv * inv) * jnp.mean(drecn * o, axis=-1, keepdims=True)
            qs, sf, f, k, g = _gates(q, f_ref[r, :], lb)
            big_g = _dot3(w_ref[...], g)
            g_last = big_g[CHUNK - 1:CHUNK, :]
            e_g = jnp.exp(big_g)
            e_rev = jnp.exp(g_last - big_g)
            e_last = jnp.exp(g_last)
            levels = _level_factors(big_g, qs, k)
            lv = lvl_ref[...]
            lvt = lvlt_ref[...]
            a_t = jnp.zeros((CHUNK, CHUNK), F32)
            for l, (xl, _, _, _) in enumerate(levels):
                a_t = a_t + jnp.where(lvt == l, _dot_nt(xl, xl), 0.0)
            dob = do.astype(BF16)
            vb = v.astype(BF16)
            d_a = _dot_nt(dob, vb)
            d_at = _dot_nt(vb, dob)
            stp = stp_ref[c]
            dstb = dst.astype(BF16)
            q_g = qs * e_g
            kdec = k * e_rev
            diag = jnp.sum(qs * k, axis=-1, keepdims=True)
            dv = _dot(a_t.astype(BF16), dob) + diag * do + _dot_nt(kdec.astype(BF16), dstb)
            dq_g = _dot(dob, stp.astype(BF16))
            dkdec = _dot(vb, dstb)
            de_last = jnp.sum(stp * dst, axis=0, keepdims=True)
            dst = dst * e_last + _dot_tn(dob, q_g.astype(BF16))
            dqs_i = jnp.zeros((CHUNK, HEAD), F32)
            dk_i = jnp.zeros((CHUNK, HEAD), F32)
            for l, (xl, xlo, e, up) in enumerate(levels):
                z = jnp.where(lv == l, d_a, jnp.where(lvt == l, d_at, 0.0))
                tmp = _dot(z.astype(BF16), jnp.concatenate([xl, xlo], axis=-1))
                tmp = (tmp[:, :HEAD] + tmp[:, HEAD:]) * e
                dqs_i = dqs_i + jnp.where(up, tmp, 0.0)
                dk_i = dk_i + jnp.where(up, 0.0, tmp)
            ddiag = jnp.sum(do * v, axis=-1, keepdims=True)
            dqs = dqs_i + ddiag * k + dq_g * e_g
            dk = dk_i + ddiag * qs + dkdec * e_rev
            dg_rev = dkdec * kdec
            dg_last = jnp.sum(dg_rev, axis=0, keepdims=True) + de_last * e_last
            dbig_g = qs * dqs_i - k * dk_i + dq_g * q_g - dg_rev
            dg = _dot3(tri_ref[...], dbig_g) + dg_last
            df = dg / f - dk
            dr_ref[1, r, :] = (df * (1.0 - lb) * sf * (1.0 - sf)).astype(BF16)
            acc_lb = acc_lb + jnp.sum(df * (1.0 - sf), axis=0, keepdims=True)
            sq = _sig(q)
            dr_ref[0, r, :] = (dqs * (sq * (1.0 + q * (1.0 - sq)))).astype(BF16)
            dr_ref[2, r, :] = dv.astype(BF16)
        dst_ref[...] = dst
        part_ref[0:1, :] += acc_grec
        part_ref[1:2, :] += acc_lb

    rev = lambda b: N_REC_BLK - 1 - b
    sec = lambda n: pl.BlockSpec((REC_ROWS, HEAD), lambda h, b: (rev(b), n * SEC_BLK + h0 + h))
    col_in = pl.BlockSpec((REC_ROWS, HEAD), lambda h, b: (rev(b), h0 + h))
    vec_in = lambda rows: pl.BlockSpec((rows, HEAD), lambda h, b: (0, h0 + h))
    full = lambda a: pl.BlockSpec(a.shape, lambda h, b: (0,) * a.ndim)
    return _call(
        body, name=name, grid=(HALF_HEADS, N_REC_BLK),
        args=(proj, proj, proj, proj, o_raw, st_prev, dout_b, w_out_g, lb_logits, rec_g, tri, lvl, lvl_t, tri_t),
        in_specs=[sec(2), sec(3), sec(4), sec(5), col_in,
                  pl.BlockSpec((None, REC_CHUNKS, HEAD, HEAD), lambda h, b: (h0 + h, rev(b), 0, 0)),
                  pl.BlockSpec((REC_ROWS, D_MODEL), lambda h, b: (rev(b), 0)),
                  pl.BlockSpec((HEAD, D_MODEL), lambda h, b: (SEC_BLK + h0 + h, 0)),
                  vec_in(2), vec_in(1), full(tri), full(lvl), full(lvl_t), full(tri_t)],
        out_specs=[pl.BlockSpec((4, REC_ROWS, HEAD), lambda h, b: (0, rev(b), h)),
                   pl.BlockSpec((8, HEAD), lambda h, b: (0, h))],
        out_shape=[jax.ShapeDtypeStruct((4, SEQ, HALF_COLS), BF16),
                   jax.ShapeDtypeStruct((8, HALF_COLS), F32)],
        scratch_shapes=[pltpu.VMEM((HEAD, HEAD), F32)],
        exchanges=exchanges)


def _w_in_block(w_ref, j):
    per_shard = W_IN_SHARD // COL_BLK
    return w_ref[j // per_shard, :, (j % per_shard) * COL_BLK:(j % per_shard + 1) * COL_BLK]


def _grad_x(dproj, w_in_g, x, g1, dout, exchanges):
    rows = 256
    n_blk = len(dproj)

    def body(*refs):
        dp_refs = refs[:n_blk]
        w_ref, x_ref, g_ref, dout_ref, dx_ref, part_ref = refs[n_blk:]

        @pl.when(pl.program_id(0) == 0)
        def _():
            part_ref[...] = jnp.zeros_like(part_ref)
        dh = jnp.zeros((rows, D_MODEL), F32)
        for j in range(n_blk):
            dh = dh + _dot_nt(dp_refs[j][...], _w_in_block(w_ref, j))
        xv = x_ref[...]
        inv = lax.rsqrt(jnp.mean(xv * xv, axis=-1, keepdims=True) + EPS)
        a = dh * g_ref[...]
        dx_ref[...] = (dout_ref[...] + inv * a
                       - xv * (inv * inv * inv) * jnp.mean(a * xv, axis=-1, keepdims=True))
        part_ref[0:1, :] += jnp.sum(dh * xv * inv, axis=0, keepdims=True)

    row = lambda: pl.BlockSpec((rows, D_MODEL), lambda i: (i, 0))
    dp_spec = lambda sec, cb: pl.BlockSpec((None, rows, COL_BLK), lambda i: (sec, i, cb))
    return _call(
        body, name="grad_x", grid=(SEQ // rows,),
        args=tuple(a for a, _, _ in dproj) + (w_in_g, x, g1, dout),
        in_specs=[dp_spec(sec, cb) for _, sec, cb in dproj]
                 + [pl.BlockSpec((N_SHARDS, D_MODEL, W_IN_SHARD), lambda i: (0, 0, 0)),
                    row(), pl.BlockSpec((1, D_MODEL), lambda i: (0, 0)), row()],
        out_specs=[row(), pl.BlockSpec((8, D_MODEL), lambda i: (0, 0))],
        out_shape=[jax.ShapeDtypeStruct((SEQ, D_MODEL), F32),
                   jax.ShapeDtypeStruct((8, D_MODEL), F32)],
        exchanges=exchanges)


def _grad_w_in(h, dp, blocks, name):
    n_blk = len(blocks)
    half = D_MODEL // 2
    pick = lambda vals: (lambda j: functools.reduce(lambda acc, iv: jnp.where(j == iv[0], iv[1], acc),
                                                     list(enumerate(vals))[1:], vals[0]))
    sec_of = pick([sec for sec, _ in blocks])
    cb_of = pick([cb for _, cb in blocks])

    def body(h_ref, dp_ref, p32_ref, p16_ref, send_ref, recv_ref, send_sems, recv_sems):
        j = pl.program_id(0)
        x, y, c = _place()
        cols = lambda cc: pl.ds(pl.multiple_of(cc * half, half), half)

        def copy(i):
            return _remote(send_ref.at[i], recv_ref.at[i], send_sems, recv_sems, i, (x, y, 1 - c))

        for i in range(n_blk):
            @pl.when(j == i)
            def _(i=i):
                send_ref[i] = _dot_tn(h_ref[:, cols(1 - c)], dp_ref[...])
                copy(i).start()
                p32_ref[i] = _dot_tn(h_ref[:, cols(c)], dp_ref[...])

        @pl.when(j == n_blk - 1)
        def _():
            for i in range(n_blk):
                copy(i).wait_recv()
                tot = p32_ref[i] + recv_ref[i]
                p32_ref[i] = tot
                p16_ref[i] = tot.astype(BF16)
            for i in range(n_blk):
                copy(i).wait_send()

    whole = pl.BlockSpec((n_blk, half, COL_BLK), lambda j: (0, 0, 0))
    return pl.pallas_call(
        body, name=name, grid=(n_blk,),
        in_specs=[pl.BlockSpec((SEQ, D_MODEL), lambda j: (0, 0)),
                  pl.BlockSpec((None, SEQ, COL_BLK), lambda j: (sec_of(j), 0, cb_of(j)))],
        out_specs=[whole, whole],
        out_shape=[jax.ShapeDtypeStruct((n_blk, half, COL_BLK), F32),
                   jax.ShapeDtypeStruct((n_blk, half, COL_BLK), BF16)],
        scratch_shapes=[pltpu.VMEM((n_blk, half, COL_BLK), F32), pltpu.VMEM((n_blk, half, COL_BLK), F32),
                        pltpu.SemaphoreType.DMA((n_blk,)), pltpu.SemaphoreType.DMA((n_blk,))],
        compiler_params=_params(dimension_semantics=("arbitrary",)),
    )(h, dp)


def _add_units(grad, recv, place, tile, name):
    n, rows, cols = grad.shape
    per_half = rows // 2 // tile

    def body(place_ref, g_ref, r_ref, o32_ref, o16_ref):
        v = g_ref[...] + r_ref[...]
        o32_ref[...] = v
        o16_ref[...] = v.astype(BF16)

    blk = lambda f: pl.BlockSpec((None, tile, cols), f)
    out = lambda s, i, p: (s, i, 0)
    return pl.pallas_call(
        body, name=name,
        grid_spec=pltpu.PrefetchScalarGridSpec(
            num_scalar_prefetch=1, grid=(n, per_half),
            in_specs=[blk(lambda s, i, p: (s, p[0] * per_half + i, 0)), blk(out)],
            out_specs=[blk(out), blk(out)]),
        out_shape=[jax.ShapeDtypeStruct(recv.shape, F32), jax.ShapeDtypeStruct(recv.shape, BF16)],
        compiler_params=_params(dimension_semantics=("arbitrary", "arbitrary")),
    )(place, grad, recv)


def _sum_units(part32, recv16, place, tile, name):
    n, half, cols = part32.shape
    per_half = half // tile

    def body(place_ref, p_ref, r_ref, o_ref):
        acc = p_ref[...]
        for j in range(len(CHIP_FLIPS)):
            acc = acc + r_ref[j].astype(F32)
        o_ref[...] = acc

    return pl.pallas_call(
        body, name=name,
        grid_spec=pltpu.PrefetchScalarGridSpec(
            num_scalar_prefetch=1, grid=(n, per_half),
            in_specs=[pl.BlockSpec((None, tile, cols), lambda s, i, p: (s, i, 0)),
                      pl.BlockSpec((None, len(CHIP_FLIPS), tile, cols), lambda s, i, p: (s, 0, i, 0))],
            out_specs=pl.BlockSpec((None, tile, cols), lambda s, i, p: (s, p[0] * per_half + i, 0))),
        out_shape=jax.ShapeDtypeStruct((n, 2 * half, cols), F32),
        compiler_params=_params(dimension_semantics=("arbitrary", "arbitrary")),
    )(place, part32, recv16)


def _adamw_math(w, g, m, v):
    m = ADAM_B1 * m + (1.0 - ADAM_B1) * g
    v = ADAM_B2 * v + (1.0 - ADAM_B2) * (g * g)
    m_hat = m / (1.0 - ADAM_B1 ** ADAM_STEP)
    v_hat = v / (1.0 - ADAM_B2 ** ADAM_STEP)
    delta = -ADAM_LR * (m_hat / (jnp.sqrt(v_hat) + ADAM_EPS) + ADAM_WD * w)
    return delta, m, v


def _adamw_units(w, m, v, grads, pick, name):
    rows, cols = w.shape
    bc = grads[0].shape[-1]
    tile = min(rows, 256)
    n_g = len(grads)

    def body(pick_ref, w_ref, m_ref, v_ref, *refs):
        g_refs, (g_out, d_ref, nm_ref, nv_ref) = refs[:n_g], refs[n_g:]
        p = pl.program_id(0)
        for a in range(n_g):
            @pl.when(pick_ref[0, p] == a)
            def _(a=a):
                g = g_refs[a][...]
                g_out[...] = g
                d_ref[...], nm_ref[...], nv_ref[...] = _adamw_math(w_ref[...], g, m_ref[...], v_ref[...])

    blk = pl.BlockSpec((tile, bc), lambda p, i, pick: (i, p))

    def g_spec(a):
        return pl.BlockSpec((None, tile, bc),
                            lambda p, i, pick: (jnp.where(pick[0, p] == a, pick[1, p], 0), i, 0))

    return pl.pallas_call(
        body, name=name,
        grid_spec=pltpu.PrefetchScalarGridSpec(
            num_scalar_prefetch=1, grid=(cols // bc, rows // tile),
            in_specs=[blk] * 3 + [g_spec(a) for a in range(n_g)],
            out_specs=[blk] * 4),
        out_shape=[jax.ShapeDtypeStruct(w.shape, F32)] * 4,
        compiler_params=_params(dimension_semantics=("arbitrary", "arbitrary")),
    )(pick, w, m, v, *grads)


ROW_NORM1, ROW_SCALE, ROW_LB, ROW_REC, ROW_FINAL, ROW_LOSS = 0, 1, 2, 4, 5, 6


def _small_update(parts, gathered, w, m, v):
    def body(own_ref, p_ref, w_ref, m_ref, v_ref, loss_ref, g_ref, d_ref, nm_ref, nv_ref):
        x, y, c = _place()
        me = 4 * x + 2 * y + c
        slot = lambda d: jnp.where(me == d, own_ref[...], p_ref[d])
        tot = slot(0)
        for d in range(1, 8):
            tot = tot + slot(d)
        wv = w_ref[...]
        l0 = wv[ROW_LB:ROW_LB + 1, :]
        l1 = wv[ROW_LB + 1:ROW_LB + 2, :]
        mx = jnp.maximum(l0, l1)
        e0 = jnp.exp(l0 - mx)
        e1 = jnp.exp(l1 - mx)
        lb = e0 / (e0 + e1)
        dl0 = tot[ROW_LB:ROW_LB + 1, :] * lb * (1.0 - lb)
        row = lax.broadcasted_iota(jnp.int32, tot.shape, 0)
        g = jnp.where(row == ROW_LB, dl0, jnp.where(row == ROW_LB + 1, -dl0, tot))
        g = jnp.where(row >= ROW_LOSS, 0.0, g)
        g_ref[...] = g
        d_ref[...], nm_ref[...], nv_ref[...] = _adamw_math(wv, g, m_ref[...], v_ref[...])
        loss_ref[...] = (0.5 / D_MODEL) * jnp.sum(tot[ROW_LOSS:ROW_LOSS + 1, :], axis=-1, keepdims=True)

    return pl.pallas_call(
        body, name="small_update",
        out_shape=[jax.ShapeDtypeStruct((1, 1), F32)] + [jax.ShapeDtypeStruct(w.shape, F32)] * 4,
        compiler_params=_params(),
    )(parts, gathered, w, m, v)


SHARD_OWNERS = tuple(range(N_SHARDS))
BLOCKS_POOL = (0, 1, 2, 3)
BLOCKS_A = (4, 6, 8, 10)
BLOCKS_B = (5, 7, 9, 11)
BLOCK_GROUPS = (BLOCKS_POOL, BLOCKS_A, BLOCKS_B)


def _block_owners(blocks):
    return tuple(j // (W_IN_SHARD // COL_BLK) for j in blocks)


def _small_rows(norm1, scale, lb, rec, final):
    pad = jnp.zeros((2, D_MODEL), F32)
    return jnp.concatenate([norm1, scale, lb, rec, final.reshape(1, D_MODEL), pad], axis=0)


def kernel(x, norm1_g, w_in, pool_w, pool_scale, lb_logits, rec_norm_g, w_out, final_norm_g, loss_target, m_norm1_g, m_w_in, m_pool_w, m_pool_scale, m_lb_logits, m_rec_norm_g, m_w_out, m_final_norm_g, v_norm1_g, v_w_in, v_pool_w, v_pool_scale, v_lb_logits, v_rec_norm_g, v_w_out, v_final_norm_g):
    xi, yi, ci = _place()
    chip = 2 * xi + yi
    place = jnp.stack([ci, chip]).astype(jnp.int32)
    pw_rows = N_GROUPS * PW_SHARD
    flat_pw = lambda a: a.reshape(pw_rows, PG)
    x2, target, gf = x[0], loss_target[0], final_norm_g.reshape(1, D_MODEL)
    tri, tri_t, lvl, lvl_t = _chunk_constants()
    tri, tri_t = jnp.asarray(tri, BF16), jnp.asarray(tri_t, BF16)
    lvl, lvl_t = jnp.asarray(lvl), jnp.asarray(lvl_t)

    proj, h, w_in_g = _in_proj(x2, norm1_g, _cast_own(w_in[0], place, "cast_w_in"), place)
    (y_rec, o_raw, st_prev), ((w_out_g, pw_g),) = _rec_fwd(
        proj, lb_logits, rec_norm_g, tri, lvl,
        [_ex_gather([_cast_own(w_out[0], place, "cast_w_out"), _cast_own(flat_pw(pool_w), place, "cast_pool_w")])])
    w_out_g = w_out_g.reshape(2 * D_MODEL, D_MODEL)
    pw_full = pw_g.reshape(N_SHARDS, N_GROUPS, PW_SHARD, PG).transpose(1, 0, 2, 3).reshape(N_GROUPS, PG, PG)
    y_pool = _pool_fwd(proj, pw_full, pool_scale)
    dout, dout_b, part_out = _out_proj_loss(y_pool, y_rec, w_out_g, x2, target, gf)

    p_out32, p_out16 = _grad_w_out(y_pool, y_rec, dout_b)
    (dpool, gpw, dscale), ((rb_out,),) = _pool_bwd(proj, dout_b, w_out_g, pw_full, pool_scale,
                                                   [_ex_send([p_out16], [SHARD_OWNERS])])
    g_out = _sum_units(p_out32, rb_out, place, 256, "sum_w_out")
    gpw = gpw.reshape(N_GROUPS, N_SHARDS, PW_SHARD, PG).transpose(1, 0, 2, 3).reshape(N_SHARDS, pw_rows, PG)
    p_inp32, p_inp16 = _grad_w_in(h, dpool, [(0, 0), (0, 1), (1, 0), (1, 1)], "grad_w_in_pool")

    rec_args = (proj, o_raw, st_prev, dout_b, w_out_g, lb_logits, rec_norm_g, tri, tri_t, lvl, lvl_t)
    (drec_a, part_a), ((rb_inp,), (ra_pw,)) = _rec_bwd(
        *rec_args, 0, "rec_bwd_a", [_ex_send([p_inp16], [_block_owners(BLOCKS_POOL)]), _ex_swap([gpw])])
    g_inp = _sum_units(p_inp32, rb_inp, place, 256, "sum_w_in_pool")
    p_pw32, p_pw16 = _add_units(gpw, ra_pw, place, 128, "add_pool_w")
    rec_blocks = [(n, 0) for n in range(4)]
    p_ina32, p_ina16 = _grad_w_in(h, drec_a, rec_blocks, "grad_w_in_a")

    (drec_b, part_b), ((rb_ina, rb_pw),) = _rec_bwd(
        *rec_args, HALF_HEADS, "rec_bwd_b",
        [_ex_send([p_ina16, p_pw16], [_block_owners(BLOCKS_A), SHARD_OWNERS])])
    g_ina = _sum_units(p_ina32, rb_ina, place, 256, "sum_w_in_a")
    g_pw = _sum_units(p_pw32, rb_pw, place, 128, "sum_pool_w")
    p_inb32, p_inb16 = _grad_w_in(h, drec_b, rec_blocks, "grad_w_in_b")

    dproj = ([(dpool, 0, 0), (dpool, 0, 1), (dpool, 1, 0), (dpool, 1, 1)]
             + [(d, n, 0) for n in range(4) for d in (drec_a, drec_b)])
    (dx, part_x), ((rb_inb,),) = _grad_x(dproj, w_in_g, x2, norm1_g, dout,
                                         [_ex_send([p_inb16], [_block_owners(BLOCKS_B)])])
    g_inb = _sum_units(p_inb32, rb_inb, place, 256, "sum_w_in_b")
    zero = jnp.zeros((1, D_MODEL), F32)
    part_rec = jnp.concatenate([part_a, part_b], axis=1)
    parts = jnp.concatenate([part_x[0:1], dscale, part_rec[1:2], zero, part_rec[0:1], part_out[0:1],
                             part_out[1:2], zero], axis=0)
    _, ((g_out, g_pw, g_inp, g_ina, g_inb), (gathered,)) = _call(
        None, name="join_halves",
        exchanges=[_ex_join([g_out, g_pw, g_inp, g_ina, g_inb],
                            [SHARD_OWNERS, SHARD_OWNERS] + [_block_owners(b) for b in BLOCK_GROUPS]),
                   _ex_gather_small(parts)])

    group_of = np.zeros((D_PROJ // COL_BLK,), np.int32)
    index_of = np.zeros((D_PROJ // COL_BLK,), np.int32)
    for gi, blocks in enumerate(BLOCK_GROUPS):
        for i, j in enumerate(blocks):
            group_of[j], index_of[j] = gi, i
    per_shard = W_IN_SHARD // COL_BLK
    pick_in = jnp.stack([lax.dynamic_slice(jnp.asarray(group_of), (per_shard * chip,), (per_shard,)),
                         lax.dynamic_slice(jnp.asarray(index_of), (per_shard * chip,), (per_shard,))])
    pick_own = jnp.stack([jnp.zeros((1,), jnp.int32), chip.reshape(1).astype(jnp.int32)])
    big = [_adamw_units(w_in[0], m_w_in[0], v_w_in[0], [g_inp, g_ina, g_inb], pick_in, "adamw_w_in"),
           _adamw_units(w_out[0], m_w_out[0], v_w_out[0], [g_out], pick_own, "adamw_w_out"),
           _adamw_units(flat_pw(pool_w), flat_pw(m_pool_w), flat_pw(v_pool_w), [g_pw], pick_own, "adamw_pool_w")]

    small_w = _small_rows(norm1_g, pool_scale, lb_logits, rec_norm_g, final_norm_g)
    small_m = _small_rows(m_norm1_g, m_pool_scale, m_lb_logits, m_rec_norm_g, m_final_norm_g)
    small_v = _small_rows(v_norm1_g, v_pool_scale, v_lb_logits, v_rec_norm_g, v_final_norm_g)
    loss, *small = _small_update(parts, gathered, small_w, small_m, small_v)

    def leaves(k):
        s = small[k]
        return (s[ROW_NORM1:ROW_NORM1 + 1], big[0][k][None], big[2][k].reshape(pool_w.shape),
                s[ROW_SCALE:ROW_SCALE + 1], s[ROW_LB:ROW_LB + 2], s[ROW_REC:ROW_REC + 1],
                big[1][k][None], s[ROW_FINAL])

    return (loss.reshape(()), dx[None], *leaves(0), *leaves(1), *leaves(2), *leaves(3))
```

```python
import functools

import numpy as np
import jax
import jax.numpy as jnp
from jax import lax
from jax.experimental import pallas as pl
from jax.experimental.pallas import tpu as pltpu

F32 = jnp.float32
BF16 = jnp.bfloat16

SEQ = 2048
D_MODEL = 1024
D_PROJ = 6144
N_SEC = 6
N_GROUPS = 4
PG = 256
N_HEADS = 8
HEAD = 128
CHUNK = 64
N_LEVELS = 6
N_SHARDS = 4
W_IN_SHARD = D_PROJ // N_SHARDS
W_OUT_SHARD = 2048 // N_SHARDS
PW_SHARD = PG // N_SHARDS
COL_BLK = 512
EPS = 1e-6

ADAM_LR = 0.001
ADAM_B1 = 0.9
ADAM_B2 = 0.999
ADAM_EPS = 1e-08
ADAM_WD = 0.01
ADAM_STEP = 10

V7X_VMEM_LIMIT = 56 * 1024 * 1024
MESH = pl.DeviceIdType.MESH


def _params(**kw):
    return pltpu.CompilerParams(vmem_limit_bytes=V7X_VMEM_LIMIT, **kw)


def _sig(x):
    return 1.0 / (1.0 + jnp.exp(-x))


def _dot(a, b):
    return jnp.dot(a, b, preferred_element_type=F32)


def _dot_nt(a, b):
    return lax.dot_general(a, b, (((1,), (1,)), ((), ())), preferred_element_type=F32)


def _dot_tn(a, b):
    return lax.dot_general(a, b, (((0,), (0,)), ((), ())), preferred_element_type=F32)


def _split3(a):
    p1 = a.astype(BF16)
    r1 = a - p1.astype(F32)
    p2 = r1.astype(BF16)
    p3 = (r1 - p2.astype(F32)).astype(BF16)
    return jnp.concatenate([p1, p2, p3], axis=-1)


def _dot3(w01, a):
    n = a.shape[-1]
    r = _dot(w01, _split3(a))
    return r[:, :n] + r[:, n:2 * n] + r[:, 2 * n:]


def _chunk_constants():
    j = np.arange(CHUNK)
    tt, ss = np.meshgrid(j, j, indexing="ij")
    x = tt ^ ss
    hb = np.full((CHUNK, CHUNK), -1, np.int32)
    for l in range(N_LEVELS):
        hb[x >= (1 << l)] = l
    sym = np.stack([(hb == l) for l in range(N_LEVELS)]).astype(np.float32)
    low = sym * (tt > ss)
    sign = np.stack([np.where((j >> l) & 1, 1.0, -1.0) for l in range(N_LEVELS)]).astype(np.float32)
    sign = np.ascontiguousarray(np.broadcast_to(sign[:, :, None], (N_LEVELS, CHUNK, HEAD)))
    tri = (ss <= tt).astype(np.float32)
    return dict(tri=tri, tri_t=np.ascontiguousarray(tri.T), low=low,
                low_t=np.ascontiguousarray(low.transpose(0, 2, 1)), sym=sym, sign=sign)


def _in_proj(x, g1, w_slots, place):
    n_col = D_PROJ // COL_BLK
    per_shard = W_IN_SHARD // COL_BLK
    n_fl = len(CHIP_FLIPS)
    rows = 1024

    def shard_at(m, chip):
        return chip ^ jnp.where(m == 0, 0, jnp.where(m == 1, 2, jnp.where(m == 2, 1, 3)))

    def body(place_ref, x_ref, g_ref, w_in_ref, proj_ref, h_ref, w_ref, wbuf, load_sems, send_sems, recv_sems):
        t = pl.program_id(0)
        x_, y_, c = _place()
        chip = 2 * x_ + y_
        piece = lambda shard, half: _half_rows(w_ref.at[shard], half)

        def sends():
            return [_remote(piece(chip, c), piece(chip, c), send_sems, recv_sems, j, (x_ ^ fx, y_ ^ fy, c))
                    for j, (fx, fy) in enumerate(CHIP_FLIPS)]

        def passed_on(j, half, to):
            sj = shard_at(j + 1, chip)
            return _remote(piece(sj, half), piece(sj, half), send_sems, recv_sems, n_fl + j, to)

        def load(shard, p):
            return pltpu.make_async_copy(w_ref.at[shard, :, pl.ds(p * COL_BLK, COL_BLK)], wbuf.at[p],
                                         load_sems.at[p])

        @pl.when(t == 0)
        def _():
            for cp in sends():
                cp.start()

            def norm(i, _):
                r = pl.ds(pl.multiple_of(i * rows, rows), rows)
                xv = x_ref[r, :]
                inv = lax.rsqrt(jnp.mean(xv * xv, axis=-1, keepdims=True) + EPS)
                h_ref[r, :] = (xv * inv * g_ref[...]).astype(BF16)
                return 0
            lax.fori_loop(0, SEQ // rows, norm, 0)

        for m in range(N_SHARDS):
            @pl.when(t == per_shard * m)
            def _(m=m):
                if m > 0:
                    j = m - 1
                    sj = shard_at(m, chip)
                    _remote(piece(sj, c), piece(sj, c), send_sems, recv_sems, j, (x_, y_, c)).wait_recv()
                    passed_on(j, c, (x_, y_, 1 - c)).start()
                    passed_on(j, 1 - c, (x_, y_, c)).wait_recv()
                for p in range(per_shard):
                    load(shard_at(m, chip), p).start()

        for p in range(per_shard):
            @pl.when(t % per_shard == p)
            def _(p=p):
                load(0, p).wait()

                def mm(i, _):
                    r = pl.ds(pl.multiple_of(i * rows, rows), rows)
                    proj_ref[r, :] = _dot(h_ref[r, :], wbuf[p])
                    return 0
                lax.fori_loop(0, SEQ // rows, mm, 0)

        @pl.when(t == n_col - 1)
        def _():
            for cp in sends():
                cp.wait_send()
            for j in range(n_fl):
                passed_on(j, c, (x_, y_, 1 - c)).wait_send()

    return pl.pallas_call(
        body, name="in_proj",
        grid_spec=pltpu.PrefetchScalarGridSpec(
            num_scalar_prefetch=1, grid=(n_col,),
            in_specs=[pl.BlockSpec((SEQ, D_MODEL), lambda t, p: (0, 0)),
                      pl.BlockSpec((1, D_MODEL), lambda t, p: (0, 0)),
                      pl.BlockSpec(memory_space=pl.ANY)],
            out_specs=[pl.BlockSpec((SEQ, COL_BLK),
                                    lambda t, p: (0, per_shard * shard_at(t // per_shard, p[1]) + t % per_shard)),
                       pl.BlockSpec((SEQ, D_MODEL), lambda t, p: (0, 0)),
                       pl.BlockSpec(memory_space=pl.ANY)],
            scratch_shapes=[pltpu.VMEM((per_shard, D_MODEL, COL_BLK), BF16),
                            pltpu.SemaphoreType.DMA((per_shard,)),
                            pltpu.SemaphoreType.DMA((2 * n_fl,)), pltpu.SemaphoreType.DMA((2 * n_fl,))]),
        out_shape=[jax.ShapeDtypeStruct((SEQ, D_PROJ), F32),
                   jax.ShapeDtypeStruct((SEQ, D_MODEL), BF16),
                   jax.ShapeDtypeStruct(w_slots.shape, BF16)],
        input_output_aliases={3: 2},
        compiler_params=_params(dimension_semantics=("arbitrary",)),
    )(place, x, g1, w_slots)


POOL_ROWS = 256
POOL_HALO = 16


def _window_sums(ext, g, shift_of):
    s = ext
    for k in range(N_GROUPS):
        s = jnp.where(k <= g, s + pltpu.roll(s, shift_of(k), 0), s)
    return s


def _pool_diff(u_ref, i, g):
    n = POOL_ROWS + POOL_HALO
    r0 = i * POOL_ROWS
    cur = u_ref[pl.ds(pl.multiple_of(r0, POOL_ROWS), POOL_ROWS), :]
    before = u_ref[pl.ds(pl.multiple_of(jnp.maximum(r0 - POOL_HALO, 0), 8), POOL_HALO), :]
    before = jnp.where(i > 0, before, 0.0)
    ext = jnp.concatenate([before, cur], axis=0)
    s = _window_sums(ext, g, lambda k: 1 << k)[POOL_HALO:, :]
    t = r0 + lax.broadcasted_iota(jnp.int32, (POOL_ROWS, 1), 0)
    width = (2 << g).astype(F32)
    inv_count = 1.0 / jnp.minimum((t + 1).astype(F32), width)
    return s * inv_count - cur, inv_count


def _pool_fwd(proj, pw_g, pool_scale):
    def body(u_ref, gate_ref, pw_ref, sc_ref, y_ref):
        g = pl.program_id(0)

        def step(i, _):
            r = pl.ds(pl.multiple_of(i * POOL_ROWS, POOL_ROWS), POOL_ROWS)
            d, _ = _pool_diff(u_ref, i, g)
            mixed = _dot(d.astype(BF16), pw_ref[...])
            gate = gate_ref[r, :]
            y_ref[r, :] = (mixed * sc_ref[...] * (gate * _sig(gate))).astype(BF16)
            return 0
        lax.fori_loop(0, SEQ // POOL_ROWS, step, 0)

    return pl.pallas_call(
        body, name="pool_fwd", grid=(N_GROUPS,),
        in_specs=[pl.BlockSpec((SEQ, PG), lambda g: (0, g)),
                  pl.BlockSpec((SEQ, PG), lambda g: (0, N_GROUPS + g)),
                  pl.BlockSpec((None, PG, PG), lambda g: (g, 0, 0)),
                  pl.BlockSpec((1, PG), lambda g: (0, g))],
        out_specs=pl.BlockSpec((SEQ, PG), lambda g: (0, g)),
        out_shape=jax.ShapeDtypeStruct((SEQ, D_MODEL), BF16),
        compiler_params=_params(dimension_semantics=("arbitrary",)),
    )(proj, proj, pw_g, pool_scale)


REC_ROWS = 1024
REC_CHUNKS = REC_ROWS // CHUNK
N_REC_BLK = SEQ // REC_ROWS
REC_GROUP = REC_CHUNKS
SEC_BLK = D_MODEL // HEAD


def _lower_bound(lb_ref):
    l0 = lb_ref[0:1, :]
    l1 = lb_ref[1:2, :]
    mx = jnp.maximum(l0, l1)
    e0 = jnp.exp(l0 - mx)
    e1 = jnp.exp(l1 - mx)
    return e0 / (e0 + e1)


def _gates(q, fl, lb):
    qs = q * _sig(q)
    sf = _sig(fl)
    f = lb + (1.0 - lb) * sf
    return qs, sf, f, 1.0 - f, jnp.log(f)


LOG2E = 1.4426950408889634


def _level_factors(g2, qs, k, sign_ref):
    t = lax.broadcasted_iota(jnp.int32, (CHUNK, HEAD), 0)
    row = lambda r, n: jnp.broadcast_to(g2[r:r + 1, :], (n, HEAD))
    out = []
    for l in range(N_LEVELS):
        m = 1 << l
        if l == 0:
            g_mid = jnp.where((t & 1) == 1, pltpu.roll(g2, 1, 0), g2)
        elif l == 1:
            low = (t & 7) < 4
            g_mid = jnp.concatenate([jnp.where(low[:8], row(8 * i + 1, 8), row(8 * i + 5, 8))
                                     for i in range(CHUNK // 8)], axis=0)
        else:
            g_mid = jnp.concatenate([row(b * 2 * m + m - 1, 2 * m) for b in range(CHUNK // (2 * m))], axis=0)
        sgn = sign_ref[l]
        up = sgn > 0.0
        e = jnp.exp2((g2 - g_mid) * sgn)
        x = jnp.where(up, qs, k) * e
        hi = x.astype(BF16)
        out.append((hi, (x - hi.astype(F32)).astype(BF16), e, up))
    return out


CHIP_FLIPS = ((1, 0), (0, 1), (1, 1))
HBM = pl.BlockSpec(memory_space=pl.ANY)


def _place():
    return lax.axis_index("x"), lax.axis_index("y"), lax.axis_index("c")


def _remote(src, dst, send_sems, recv_sems, k, to):
    return pltpu.make_async_remote_copy(src_ref=src, dst_ref=dst, send_sem=send_sems.at[k],
                                        recv_sem=recv_sems.at[k], device_id=to, device_id_type=MESH)


def _half_rows(ref, c):
    half = ref.shape[-2] // 2
    rows = pl.ds(pl.multiple_of(c * half, half), half)
    return ref.at[:, rows, :] if len(ref.shape) == 3 else ref.at[rows, :]


class _Exchange:
    def __init__(self, inputs, out_shapes, n_sems, start, finish, aliases=None):
        self.inputs, self.out_shapes, self.n_sems = list(inputs), list(out_shapes), n_sems
        self.start, self.finish, self.aliases = start, finish, dict(aliases or {})


def _ex_swap(grads):
    def copies(ins, outs, send, recv):
        x, y, c = _place()
        return [_remote(_half_rows(g, 1 - c), o, send, recv, t, (x, y, 1 - c))
                for t, (g, o) in enumerate(zip(ins, outs))]

    def start(*refs):
        for cp in copies(*refs):
            cp.start()

    def finish(*refs):
        cps = copies(*refs)
        for cp in cps:
            cp.wait_recv()
        for cp in cps:
            cp.wait_send()

    shapes = [jax.ShapeDtypeStruct((a.shape[0], a.shape[1] // 2, a.shape[2]), F32) for a in grads]
    return _Exchange(grads, shapes, len(grads), start, finish)


def _ex_send(parts16, owners):
    def each(ins, outs, send, recv, to_sender, to_owner):
        x, y, c = _place()
        k = 0
        for t, own in enumerate(owners):
            for j, o in enumerate(own):
                for r, (fx, fy) in enumerate(CHIP_FLIPS):
                    tx, ty = x ^ fx, y ^ fy
                    cp = _remote(ins[t].at[j], outs[t].at[j, r], send, recv, k, (tx, ty, c))
                    if to_sender is not None:
                        pl.when(2 * tx + ty == o)(functools.partial(to_sender, cp))
                    if to_owner is not None:
                        pl.when(2 * x + y == o)(functools.partial(to_owner, cp))
                    k += 1

    def start(*refs):
        each(*refs, lambda cp: cp.start(), None)

    def finish(*refs):
        each(*refs, None, lambda cp: cp.wait_recv())
        each(*refs, lambda cp: cp.wait_send(), None)

    shapes = [jax.ShapeDtypeStruct((a.shape[0], len(CHIP_FLIPS)) + a.shape[1:], BF16) for a in parts16]
    return _Exchange(parts16, shapes, len(CHIP_FLIPS) * sum(len(o) for o in owners), start, finish)


def _ex_join(units, owners):
    def each(ins, outs, send, recv, fn):
        x, y, c = _place()
        k = 0
        for t, own in enumerate(owners):
            for j, o in enumerate(own):
                def half(cc, to, u=outs[t].at[j], k=k):
                    return _remote(_half_rows(u, cc), _half_rows(u, cc), send, recv, k, to)
                mine = functools.partial(half, c, (x, y, 1 - c))
                theirs = functools.partial(half, 1 - c, (x, y, c))
                pl.when(2 * x + y == o)(functools.partial(fn, mine, theirs))
                k += 1

    def start(*refs):
        each(*refs, lambda mine, theirs: mine().start())

    def finish(*refs):
        each(*refs, lambda mine, theirs: theirs().wait_recv())
        each(*refs, lambda mine, theirs: mine().wait_send())

    shapes = [jax.ShapeDtypeStruct(a.shape, F32) for a in units]
    return _Exchange(units, shapes, sum(len(o) for o in owners), start, finish,
                     aliases={t: t for t in range(len(units))})


def _ex_gather(slots):
    n_t = len(slots)
    n_fl = len(CHIP_FLIPS)

    def piece(ref, shard, half):
        return _half_rows(ref.at[shard], half)

    def first(outs, send, recv):
        x, y, c = _place()
        s = 2 * x + y
        return [_remote(piece(outs[t], s, c), piece(outs[t], s, c), send, recv, n_t * j + t, (x ^ fx, y ^ fy, c))
                for j, (fx, fy) in enumerate(CHIP_FLIPS) for t in range(n_t)]

    def start(ins, outs, send, recv):
        for cp in first(outs, send, recv):
            cp.start()

    def finish(ins, outs, send, recv):
        x, y, c = _place()
        passed = []
        for j, (fx, fy) in enumerate(CHIP_FLIPS):
            sj = 2 * (x ^ fx) + (y ^ fy)
            for t in range(n_t):
                k = n_t * j + t
                _remote(piece(outs[t], sj, c), piece(outs[t], sj, c), send, recv, k, (x, y, c)).wait_recv()
                cp = _remote(piece(outs[t], sj, c), piece(outs[t], sj, c), send, recv, n_t * n_fl + k, (x, y, 1 - c))
                cp.start()
                passed.append(cp)
        for j, (fx, fy) in enumerate(CHIP_FLIPS):
            sj = 2 * (x ^ fx) + (y ^ fy)
            for t in range(n_t):
                k = n_t * n_fl + n_t * j + t
                _remote(piece(outs[t], sj, 1 - c), piece(outs[t], sj, 1 - c), send, recv, k, (x, y, c)).wait_recv()
        for cp in first(outs, send, recv) + passed:
            cp.wait_send()

    shapes = [jax.ShapeDtypeStruct(a.shape, BF16) for a in slots]
    return _Exchange(slots, shapes, 2 * n_t * n_fl, start, finish, aliases={t: t for t in range(n_t)})


def _ex_gather_small(parts):
    def copies(ins, outs, send, recv):
        x, y, c = _place()
        me = 4 * x + 2 * y + c
        return [_remote(ins[0], outs[0].at[me], send, recv, mask - 1,
                        (x ^ (mask >> 2), y ^ ((mask >> 1) & 1), c ^ (mask & 1))) for mask in range(1, 8)]

    def start(*refs):
        for cp in copies(*refs):
            cp.start()

    def finish(ins, outs, send, recv):
        x, y, c = _place()
        me = 4 * x + 2 * y + c
        for mask in range(1, 8):
            _remote(ins[0], outs[0].at[me ^ mask], send, recv, mask - 1, (x, y, c)).wait_recv()
        for cp in copies(ins, outs, send, recv):
            cp.wait_send()

    return _Exchange([parts], [jax.ShapeDtypeStruct((8,) + parts.shape, F32)], 7, start, finish)


def _call(body, *, name, args=(), in_specs=(), out_specs=(), out_shape=(), grid=(), scratch_shapes=(),
          exchanges=()):
    n_in, n_out, n_scr = len(args), len(out_shape), len(scratch_shapes)
    ex_in, ex_out, ex_scr, spans, alias = [], [], [], [], {}
    for ex in exchanges:
        spans.append((len(ex_in), len(ex.inputs), len(ex_out), len(ex.out_shapes)))
        for i, o in ex.aliases.items():
            alias[n_in + len(ex_in) + i] = n_out + len(ex_out) + o
        ex_in += ex.inputs
        ex_out += ex.out_shapes
        ex_scr += [pltpu.SemaphoreType.DMA((ex.n_sems,)), pltpu.SemaphoreType.DMA((ex.n_sems,))]

    def full(*refs):
        ins, x_in = refs[:n_in], refs[n_in:n_in + len(ex_in)]
        outs = refs[n_in + len(ex_in):n_in + len(ex_in) + n_out]
        x_out = refs[n_in + len(ex_in) + n_out:n_in + len(ex_in) + n_out + len(ex_out)]
        scr = refs[len(refs) - n_scr - len(ex_scr):len(refs) - len(ex_scr)]
        sems = refs[len(refs) - len(ex_scr):]

        def run(which):
            for e, (ex, (i0, ni, o0, no)) in enumerate(zip(exchanges, spans)):
                getattr(ex, which)(x_in[i0:i0 + ni], x_out[o0:o0 + no], sems[2 * e], sems[2 * e + 1])

        if grid:
            ids = [pl.program_id(a) for a in range(len(grid))]
            is_first = functools.reduce(jnp.logical_and, [i == 0 for i in ids])
            is_last = functools.reduce(jnp.logical_and, [i == g - 1 for i, g in zip(ids, grid)])
            pl.when(is_first)(lambda: run("start"))
            body(*ins, *outs, *scr)
            pl.when(is_last)(lambda: run("finish"))
        else:
            run("start")
            if body is not None:
                body(*ins, *outs, *scr)
            run("finish")

    kw = dict(grid=grid) if grid else {}
    if grid:
        kw["compiler_params"] = _params(dimension_semantics=("arbitrary",) * len(grid))
    else:
        kw["compiler_params"] = _params()
    res = pl.pallas_call(
        full, name=name,
        in_specs=list(in_specs) + [HBM] * len(ex_in),
        out_specs=list(out_specs) + [HBM] * len(ex_out),
        out_shape=list(out_shape) + ex_out,
        scratch_shapes=list(scratch_shapes) + ex_scr,
        input_output_aliases=alias, **kw,
    )(*args, *ex_in)
    own = list(res[:n_out])
    per_ex = [list(res[n_out + o0:n_out + o0 + no]) for (_, _, o0, no) in spans]
    return own, per_ex


def _cast_own(w, place, name):
    rows, cols = w.shape
    tile = min(rows, 256)

    def body(place_ref, w_ref, o_ref):
        o_ref[...] = w_ref[...].astype(BF16)

    return pl.pallas_call(
        body, name=name,
        grid_spec=pltpu.PrefetchScalarGridSpec(
            num_scalar_prefetch=1, grid=(rows // tile,),
            in_specs=[pl.BlockSpec((tile, cols), lambda i, p: (i, 0))],
            out_specs=pl.BlockSpec((None, tile, cols), lambda i, p: (p[1], i, 0))),
        out_shape=jax.ShapeDtypeStruct((N_SHARDS, rows, cols), BF16),
        compiler_params=_params(dimension_semantics=("arbitrary",)),
    )(place, w)


def _rec_fwd(proj, lb_logits, rec_g, consts, exchanges):
    tri, low, sign = consts["tri"], consts["low"], consts["sign"]

    def body(q_ref, f_ref, i_ref, rg_ref, lb_ref, g_ref, w_ref, low_ref, sign_ref, y_ref, o_ref, stp_ref, st_ref):
        @pl.when(pl.program_id(1) == 0)
        def _():
            st_ref[...] = jnp.zeros_like(st_ref)
        lb = _lower_bound(lb_ref)
        st = st_ref[...]
        rows = lambda c: pl.ds(c * CHUNK, CHUNK)
        for c0 in range(0, REC_CHUNKS, REC_GROUP):
            group = range(c0, c0 + REC_GROUP)
            gated = [_gates(q_ref[rows(c), :], f_ref[rows(c), :], lb) for c in group]
            g2s = [_dot3(w_ref[...], g) * LOG2E for (_, _, _, _, g) in gated]
            xs = [[xl for xl, _, _, _ in _level_factors(g2, qs, k, sign_ref)]
                  for g2, (qs, _, _, k, _) in zip(g2s, gated)]
            a_s = []
            for x in xs:
                a = jnp.zeros((CHUNK, CHUNK), F32)
                for l, xl in enumerate(x):
                    a = a + _dot_nt(xl, xl) * low_ref[l]
                a_s.append(a.astype(BF16))
            vbs = [i_ref[rows(c), :].astype(BF16) for c in group]
            intra = [_dot(a, vb) for a, vb in zip(a_s, vbs)]
            kvs = [_dot_tn(vb, (k * jnp.exp2(g2[CHUNK - 1:CHUNK, :] - g2)).astype(BF16))
                   for vb, g2, (_, _, _, k, _) in zip(vbs, g2s, gated)]
            for i, c in enumerate(group):
                qs, _, _, k, _ = gated[i]
                g2 = g2s[i]
                stp_ref[c] = st
                v = i_ref[rows(c), :]
                rg = rg_ref[rows(c), :]
                o = (intra[i] + jnp.sum(qs * k, axis=-1, keepdims=True) * v
                     + _dot_nt((qs * jnp.exp2(g2)).astype(BF16), st.astype(BF16)))
                st = st * jnp.exp2(g2[CHUNK - 1:CHUNK, :]) + kvs[i]
                o_ref[rows(c), :] = o
                inv = lax.rsqrt(jnp.mean(o * o, axis=-1, keepdims=True) + EPS)
                y_ref[rows(c), :] = (o * inv * g_ref[...] * (rg * _sig(rg))).astype(BF16)
        st_ref[...] = st

    sec = lambda n: pl.BlockSpec((REC_ROWS, HEAD), lambda h, b: (b, n * SEC_BLK + h))
    vec = lambda rows: pl.BlockSpec((rows, HEAD), lambda h, b: (0, h))
    full = lambda a: pl.BlockSpec(a.shape, lambda h, b: (0,) * a.ndim)
    return _call(
        body, name="rec_fwd", grid=(N_HEADS, N_REC_BLK),
        args=(proj, proj, proj, proj, lb_logits, rec_g, tri, low, sign),
        in_specs=[sec(2), sec(3), sec(4), sec(5), vec(2), vec(1), full(tri), full(low), full(sign)],
        out_specs=[pl.BlockSpec((REC_ROWS, HEAD), lambda h, b: (b, h)),
                   pl.BlockSpec((REC_ROWS, HEAD), lambda h, b: (b, h)),
                   pl.BlockSpec((None, REC_CHUNKS, HEAD, HEAD), lambda h, b: (h, b, 0, 0))],
        out_shape=[jax.ShapeDtypeStruct((SEQ, D_MODEL), BF16),
                   jax.ShapeDtypeStruct((SEQ, D_MODEL), F32),
                   jax.ShapeDtypeStruct((N_HEADS, SEQ // CHUNK, HEAD, HEAD), F32)],
        scratch_shapes=[pltpu.VMEM((HEAD, HEAD), F32)],
        exchanges=exchanges)


OUT_ROWS = 256


def _out_proj_loss(y_pool, y_rec, w_out_g, x, target, gf):
    def body(yp_ref, yr_ref, w_ref, x_ref, t_ref, gf_ref, dout_ref, doutb_ref, part_ref):
        @pl.when(pl.program_id(0) == 0)
        def _():
            part_ref[...] = jnp.zeros_like(part_ref)
        out = (x_ref[...] + _dot(yp_ref[...], w_ref[0:D_MODEL, :])
               + _dot(yr_ref[...], w_ref[D_MODEL:2 * D_MODEL, :]))
        inv = lax.rsqrt(jnp.mean(out * out, axis=-1, keepdims=True) + EPS)
        gf_v = gf_ref[...]
        diff = out * inv * gf_v - t_ref[...]
        dyf = diff * (1.0 / D_MODEL)
        a = dyf * gf_v
        dout = inv * a - out * (inv * inv * inv) * jnp.mean(a * out, axis=-1, keepdims=True)
        dout_ref[...] = dout
        doutb_ref[...] = dout.astype(BF16)
        part_ref[0:1, :] += jnp.sum(dyf * out * inv, axis=0, keepdims=True)
        part_ref[1:2, :] += jnp.sum(diff * diff, axis=0, keepdims=True)

    row = lambda n: pl.BlockSpec((OUT_ROWS, n), lambda i: (i, 0))
    return pl.pallas_call(
        body, name="out_proj_loss", grid=(SEQ // OUT_ROWS,),
        in_specs=[row(D_MODEL), row(D_MODEL), pl.BlockSpec((2 * D_MODEL, D_MODEL), lambda i: (0, 0)),
                  row(D_MODEL), row(D_MODEL), pl.BlockSpec((1, D_MODEL), lambda i: (0, 0))],
        out_specs=[row(D_MODEL), row(D_MODEL), pl.BlockSpec((8, D_MODEL), lambda i: (0, 0))],
        out_shape=[jax.ShapeDtypeStruct((SEQ, D_MODEL), F32),
                   jax.ShapeDtypeStruct((SEQ, D_MODEL), BF16),
                   jax.ShapeDtypeStruct((8, D_MODEL), F32)],
        compiler_params=_params(dimension_semantics=("arbitrary",)),
    )(y_pool, y_rec, w_out_g, x, target, gf)


def _grad_w_out(y_pool, y_rec, dout_b):
    blk = W_OUT_SHARD // 2
    per = D_MODEL // blk
    n = 2 * per

    def body(yp_ref, yr_ref, d_ref, p32_ref, p16_ref, send_ref, recv_ref, send_sems, recv_sems):
        j = pl.program_id(0)
        x, y, c = _place()

        def copy(u):
            return _remote(send_ref.at[u], recv_ref.at[u], send_sems, recv_sems, u, (x, y, 1 - c))

        for i in range(n):
            @pl.when(j == i)
            def _(i=i):
                res = _dot_tn((yp_ref if i < per else yr_ref)[...], d_ref[...])

                @pl.when(i % 2 == c)
                def _():
                    p32_ref[i // 2] = res

                @pl.when(i % 2 != c)
                def _():
                    send_ref[i // 2] = res
                    copy(i // 2).start()

        @pl.when(j == n - 1)
        def _():
            for u in range(N_SHARDS):
                copy(u).wait_recv()
                tot = p32_ref[u] + recv_ref[u]
                p32_ref[u] = tot
                p16_ref[u] = tot.astype(BF16)
            for u in range(N_SHARDS):
                copy(u).wait_send()

    whole = pl.BlockSpec((N_SHARDS, blk, D_MODEL), lambda j: (0, 0, 0))
    return pl.pallas_call(
        body, name="grad_w_out", grid=(n,),
        in_specs=[pl.BlockSpec((SEQ, blk), lambda j: (0, jnp.minimum(j, per - 1))),
                  pl.BlockSpec((SEQ, blk), lambda j: (0, jnp.maximum(j - per, 0))),
                  pl.BlockSpec((SEQ, D_MODEL), lambda j: (0, 0))],
        out_specs=[whole, whole],
        out_shape=[jax.ShapeDtypeStruct((N_SHARDS, blk, D_MODEL), F32),
                   jax.ShapeDtypeStruct((N_SHARDS, blk, D_MODEL), BF16)],
        scratch_shapes=[pltpu.VMEM((N_SHARDS, blk, D_MODEL), F32), pltpu.VMEM((N_SHARDS, blk, D_MODEL), F32),
                        pltpu.SemaphoreType.DMA((N_SHARDS,)), pltpu.SemaphoreType.DMA((N_SHARDS,))],
        compiler_params=_params(dimension_semantics=("arbitrary",)),
    )(y_pool, y_rec, dout_b)


def _pool_bwd(proj, dout_b, w_out_g, pw_g, pool_scale, exchanges):
    n = POOL_ROWS + POOL_HALO

    def body(u_ref, gate_ref, d_ref, wo_ref, pw_ref, sc_ref,
             dp_ref, dpw_ref, dsc_ref, dd_ref, ddw_ref):
        g = pl.program_id(0)
        dpw_ref[...] = jnp.zeros_like(dpw_ref)
        dsc_ref[...] = jnp.zeros_like(dsc_ref)

        def first(i, _):
            r = pl.ds(pl.multiple_of(i * POOL_ROWS, POOL_ROWS), POOL_ROWS)
            d, inv_count = _pool_diff(u_ref, i, g)
            db = d.astype(BF16)
            mixed = _dot(db, pw_ref[...])
            gate = gate_ref[r, :]
            sg = _sig(gate)
            silu = gate * sg
            dy = _dot_nt(d_ref[r, :], wo_ref[...])
            sc = sc_ref[...]
            dmixed = dy * silu * sc
            dp_ref[1, r, :] = (dy * mixed * sc * (sg * (1.0 + gate * (1.0 - sg)))).astype(BF16)
            dsc_ref[...] += jnp.sum(dy * silu * mixed, axis=0, keepdims=True)
            dmb = dmixed.astype(BF16)
            dpw_ref[...] += _dot_tn(db, dmb)
            dd = _dot_nt(dmb, pw_ref[...])
            dd_ref[r, :] = dd
            ddw_ref[r, :] = dd * inv_count
            return 0
        lax.fori_loop(0, SEQ // POOL_ROWS, first, 0)

        def second(i, _):
            r0 = i * POOL_ROWS
            r = pl.ds(pl.multiple_of(r0, POOL_ROWS), POOL_ROWS)
            last = i == SEQ // POOL_ROWS - 1
            after = ddw_ref[pl.ds(pl.multiple_of(jnp.minimum(r0 + POOL_ROWS, SEQ - POOL_HALO), 8), POOL_HALO), :]
            after = jnp.where(last, 0.0, after)
            ext = jnp.concatenate([ddw_ref[r, :], after], axis=0)
            s = _window_sums(ext, g, lambda k: n - (1 << k))[:POOL_ROWS, :]
            dp_ref[0, r, :] = (s - dd_ref[r, :]).astype(BF16)
            return 0
        lax.fori_loop(0, SEQ // POOL_ROWS, second, 0)

    return _call(
        body, name="pool_bwd", grid=(N_GROUPS,),
        args=(proj, proj, dout_b, w_out_g, pw_g, pool_scale),
        in_specs=[pl.BlockSpec((SEQ, PG), lambda g: (0, g)),
                  pl.BlockSpec((SEQ, PG), lambda g: (0, N_GROUPS + g)),
                  pl.BlockSpec((SEQ, D_MODEL), lambda g: (0, 0)),
                  pl.BlockSpec((PG, D_MODEL), lambda g: (g, 0)),
                  pl.BlockSpec((None, PG, PG), lambda g: (g, 0, 0)),
                  pl.BlockSpec((1, PG), lambda g: (0, g))],
        out_specs=[pl.BlockSpec((2, SEQ, PG), lambda g: (0, 0, g)),
                   pl.BlockSpec((None, PG, PG), lambda g: (g, 0, 0)),
                   pl.BlockSpec((1, PG), lambda g: (0, g))],
        out_shape=[jax.ShapeDtypeStruct((2, SEQ, D_MODEL), BF16),
                   jax.ShapeDtypeStruct((N_GROUPS, PG, PG), F32),
                   jax.ShapeDtypeStruct((1, D_MODEL), F32)],
        scratch_shapes=[pltpu.VMEM((SEQ, PG), F32), pltpu.VMEM((SEQ, PG), F32)],
        exchanges=exchanges)


HALF_HEADS = N_HEADS // 2
HALF_COLS = HALF_HEADS * HEAD


def _rec_bwd(proj, o_raw, st_prev, dout_b, w_out_g, lb_logits, rec_g, consts, h0, name, exchanges):
    def body(q_ref, f_ref, i_ref, rg_ref, o_ref, stp_ref, d_ref, wo_ref, lb_ref, g_ref,
             w_ref, lowt_ref, sym_ref, sign_ref, tri_ref,
             dr_ref, part_ref, dst_ref):
        @pl.when(pl.program_id(1) == 0)
        def _():
            dst_ref[...] = jnp.zeros_like(dst_ref)
            part_ref[...] = jnp.zeros_like(part_ref)
        tril = (lax.broadcasted_iota(jnp.int32, (CHUNK, CHUNK), 0)
                > lax.broadcasted_iota(jnp.int32, (CHUNK, CHUNK), 1))
        lb = _lower_bound(lb_ref)
        grec = g_ref[...]
        dst = dst_ref[...]
        acc_grec = jnp.zeros((1, HEAD), F32)
        acc_lb = jnp.zeros((1, HEAD), F32)
        rows = lambda c: pl.ds(c * CHUNK, CHUNK)
        for c0 in reversed(range(0, REC_CHUNKS, REC_GROUP)):
            group = list(reversed(range(c0, c0 + REC_GROUP)))
            dys = [_dot_nt(d_ref[rows(c), :], wo_ref[...]) for c in group]
            dos = []
            for c, dy in zip(group, dys):
                rg = rg_ref[rows(c), :]
                o = o_ref[rows(c), :]
                sg = _sig(rg)
                silu = rg * sg
                inv = lax.rsqrt(jnp.mean(o * o, axis=-1, keepdims=True) + EPS)
                recn = o * inv
                dr_ref[3, rows(c), :] = (dy * recn * grec * (sg * (1.0 + rg * (1.0 - sg)))).astype(BF16)
                acc_grec = acc_grec + jnp.sum(dy * silu * recn, axis=0, keepdims=True)
                drecn = dy * silu * grec
                dos.append(inv * drecn - o * (inv * inv * inv) * jnp.mean(drecn * o, axis=-1, keepdims=True))
            gated = [_gates(q_ref[rows(c), :], f_ref[rows(c), :], lb) for c in group]
            g2s = [_dot3(w_ref[...], g) * LOG2E for (_, _, _, _, g) in gated]
            levels = [_level_factors(g2, qs, k, sign_ref) for g2, (qs, _, _, k, _) in zip(g2s, gated)]
            a_ts = []
            for lev in levels:
                a_t = jnp.zeros((CHUNK, CHUNK), F32)
                for l, (xl, _, _, _) in enumerate(lev):
                    a_t = a_t + _dot_nt(xl, xl) * lowt_ref[l]
                a_ts.append(a_t.astype(BF16))
            dobs = [do.astype(BF16) for do in dos]
            vbs = [i_ref[rows(c), :].astype(BF16) for c in group]
            d_syms = [jnp.where(tril, _dot_nt(dob, vb), _dot_nt(vb, dob)) for dob, vb in zip(dobs, vbs)]
            dqs_is, dk_is = [], []
            for lev, d_sym in zip(levels, d_syms):
                dqs_i = jnp.zeros((CHUNK, HEAD), F32)
                both_i = jnp.zeros((CHUNK, HEAD), F32)
                for l, (xl, xlo, e, up) in enumerate(lev):
                    z = d_sym * sym_ref[l]
                    tmp = _dot(z.astype(BF16), jnp.concatenate([xl, xlo], axis=-1))
                    tmp = (tmp[:, :HEAD] + tmp[:, HEAD:]) * e
                    dqs_i = dqs_i + jnp.where(up, tmp, 0.0)
                    both_i = both_i + tmp
                dqs_is.append(dqs_i)
                dk_is.append(both_i - dqs_i)
            e_gs = [jnp.exp2(g2) for g2 in g2s]
            e_revs = [jnp.exp2(g2[CHUNK - 1:CHUNK, :] - g2) for g2 in g2s]
            e_lasts = [jnp.exp2(g2[CHUNK - 1:CHUNK, :]) for g2 in g2s]
            q_gs = [qs * e_g for (qs, _, _, _, _), e_g in zip(gated, e_gs)]
            kdecs = [k * e_rev for (_, _, _, k, _), e_rev in zip(gated, e_revs)]
            dv12 = [_dot(a_t, dob) + jnp.sum(qs * k, axis=-1, keepdims=True) * do
                    for a_t, dob, do, (qs, _, _, k, _) in zip(a_ts, dobs, dos, gated)]
            dq_gs = [_dot(dob, stp_ref[c].astype(BF16)) for c, dob in zip(group, dobs)]
            steps = [_dot_tn(dob, q_g.astype(BF16)) for dob, q_g in zip(dobs, q_gs)]
            dsts = []
            for e_last, step in zip(e_lasts, steps):
                dsts.append(dst)
                dst = dst * e_last + step
            dstbs = [d.astype(BF16) for d in dsts]
            dv3 = [_dot_nt(kdec.astype(BF16), dstb) for kdec, dstb in zip(kdecs, dstbs)]
            dkdecs = [_dot(vb, dstb) for vb, dstb in zip(vbs, dstbs)]
            dbig_gs, dg_lasts, dqss, dks = [], [], [], []
            for i, c in enumerate(group):
                qs, _, _, k, _ = gated[i]
                de_last = jnp.sum(stp_ref[c] * dsts[i], axis=0, keepdims=True)
                ddiag = jnp.sum(dos[i] * i_ref[rows(c), :], axis=-1, keepdims=True)
                dqss.append(dqs_is[i] + ddiag * k + dq_gs[i] * e_gs[i])
                dks.append(dk_is[i] + ddiag * qs + dkdecs[i] * e_revs[i])
                dg_rev = dkdecs[i] * kdecs[i]
                dg_lasts.append(jnp.sum(dg_rev, axis=0, keepdims=True) + de_last * e_lasts[i])
                dbig_gs.append(qs * dqs_is[i] - k * dk_is[i] + dq_gs[i] * q_gs[i] - dg_rev)
            dgs = [_dot3(tri_ref[...], dbig_g) + dg_last for dbig_g, dg_last in zip(dbig_gs, dg_lasts)]
            for i, c in enumerate(group):
                _, sf, f, _, _ = gated[i]
                q = q_ref[rows(c), :]
                df = dgs[i] / f - dks[i]
                dr_ref[1, rows(c), :] = (df * (1.0 - lb) * sf * (1.0 - sf)).astype(BF16)
                acc_lb = acc_lb + jnp.sum(df * (1.0 - sf), axis=0, keepdims=True)
                sq = _sig(q)
                dr_ref[0, rows(c), :] = (dqss[i] * (sq * (1.0 + q * (1.0 - sq)))).astype(BF16)
                dr_ref[2, rows(c), :] = (dv12[i] + dv3[i]).astype(BF16)
        dst_ref[...] = dst
        part_ref[0:1, :] += acc_grec
        part_ref[1:2, :] += acc_lb

    rev = lambda b: N_REC_BLK - 1 - b
    sec = lambda n: pl.BlockSpec((REC_ROWS, HEAD), lambda h, b: (rev(b), n * SEC_BLK + h0 + h))
    col_in = pl.BlockSpec((REC_ROWS, HEAD), lambda h, b: (rev(b), h0 + h))
    vec_in = lambda rows: pl.BlockSpec((rows, HEAD), lambda h, b: (0, h0 + h))
    full = lambda a: pl.BlockSpec(a.shape, lambda h, b: (0,) * a.ndim)
    return _call(
        body, name=name, grid=(HALF_HEADS, N_REC_BLK),
        args=(proj, proj, proj, proj, o_raw, st_prev, dout_b, w_out_g, lb_logits, rec_g,
              consts["tri"], consts["low_t"], consts["sym"], consts["sign"], consts["tri_t"]),
        in_specs=[sec(2), sec(3), sec(4), sec(5), col_in,
                  pl.BlockSpec((None, REC_CHUNKS, HEAD, HEAD), lambda h, b: (h0 + h, rev(b), 0, 0)),
                  pl.BlockSpec((REC_ROWS, D_MODEL), lambda h, b: (rev(b), 0)),
                  pl.BlockSpec((HEAD, D_MODEL), lambda h, b: (SEC_BLK + h0 + h, 0)),
                  vec_in(2), vec_in(1)] + [full(consts[n]) for n in ("tri", "low_t", "sym", "sign", "tri_t")],
        out_specs=[pl.BlockSpec((4, REC_ROWS, HEAD), lambda h, b: (0, rev(b), h)),
                   pl.BlockSpec((8, HEAD), lambda h, b: (0, h))],
        out_shape=[jax.ShapeDtypeStruct((4, SEQ, HALF_COLS), BF16),
                   jax.ShapeDtypeStruct((8, HALF_COLS), F32)],
        scratch_shapes=[pltpu.VMEM((HEAD, HEAD), F32)],
        exchanges=exchanges)


def _w_in_block(w_ref, j):
    per_shard = W_IN_SHARD // COL_BLK
    return w_ref[j // per_shard, :, (j % per_shard) * COL_BLK:(j % per_shard + 1) * COL_BLK]


def _grad_x(dproj, w_in_g, x, g1, dout, exchanges):
    rows = 256
    n_blk = len(dproj)

    def body(*refs):
        dp_refs = refs[:n_blk]
        w_ref, x_ref, g_ref, dout_ref, dx_ref, part_ref = refs[n_blk:]

        @pl.when(pl.program_id(0) == 0)
        def _():
            part_ref[...] = jnp.zeros_like(part_ref)
        dh = jnp.zeros((rows, D_MODEL), F32)
        for j in range(n_blk):
            dh = dh + _dot_nt(dp_refs[j][...], _w_in_block(w_ref, j))
        xv = x_ref[...]
        inv = lax.rsqrt(jnp.mean(xv * xv, axis=-1, keepdims=True) + EPS)
        a = dh * g_ref[...]
        dx_ref[...] = (dout_ref[...] + inv * a
                       - xv * (inv * inv * inv) * jnp.mean(a * xv, axis=-1, keepdims=True))
        part_ref[0:1, :] += jnp.sum(dh * xv * inv, axis=0, keepdims=True)

    row = lambda: pl.BlockSpec((rows, D_MODEL), lambda i: (i, 0))
    dp_spec = lambda sec, cb: pl.BlockSpec((None, rows, COL_BLK), lambda i: (sec, i, cb))
    return _call(
        body, name="grad_x", grid=(SEQ // rows,),
        args=tuple(a for a, _, _ in dproj) + (w_in_g, x, g1, dout),
        in_specs=[dp_spec(sec, cb) for _, sec, cb in dproj]
                 + [pl.BlockSpec((N_SHARDS, D_MODEL, W_IN_SHARD), lambda i: (0, 0, 0)),
                    row(), pl.BlockSpec((1, D_MODEL), lambda i: (0, 0)), row()],
        out_specs=[row(), pl.BlockSpec((8, D_MODEL), lambda i: (0, 0))],
        out_shape=[jax.ShapeDtypeStruct((SEQ, D_MODEL), F32),
                   jax.ShapeDtypeStruct((8, D_MODEL), F32)],
        exchanges=exchanges)


def _grad_w_in(h, dp, blocks, name):
    n_blk = len(blocks)
    half = D_MODEL // 2
    pick = lambda vals: (lambda j: functools.reduce(lambda acc, iv: jnp.where(j == iv[0], iv[1], acc),
                                                     list(enumerate(vals))[1:], vals[0]))
    sec_of = pick([sec for sec, _ in blocks])
    cb_of = pick([cb for _, cb in blocks])

    def body(h_ref, dp_ref, p32_ref, p16_ref, send_ref, recv_ref, send_sems, recv_sems):
        j = pl.program_id(0)
        x, y, c = _place()
        cols = lambda cc: pl.ds(pl.multiple_of(cc * half, half), half)

        def copy(i):
            return _remote(send_ref.at[i], recv_ref.at[i], send_sems, recv_sems, i, (x, y, 1 - c))

        for i in range(n_blk):
            @pl.when(j == i)
            def _(i=i):
                send_ref[i] = _dot_tn(h_ref[:, cols(1 - c)], dp_ref[...])
                copy(i).start()
                p32_ref[i] = _dot_tn(h_ref[:, cols(c)], dp_ref[...])

        @pl.when(j == n_blk - 1)
        def _():
            for i in range(n_blk):
                copy(i).wait_recv()
                tot = p32_ref[i] + recv_ref[i]
                p32_ref[i] = tot
                p16_ref[i] = tot.astype(BF16)
            for i in range(n_blk):
                copy(i).wait_send()

    whole = pl.BlockSpec((n_blk, half, COL_BLK), lambda j: (0, 0, 0))
    return pl.pallas_call(
        body, name=name, grid=(n_blk,),
        in_specs=[pl.BlockSpec((SEQ, D_MODEL), lambda j: (0, 0)),
                  pl.BlockSpec((None, SEQ, COL_BLK), lambda j: (sec_of(j), 0, cb_of(j)))],
        out_specs=[whole, whole],
        out_shape=[jax.ShapeDtypeStruct((n_blk, half, COL_BLK), F32),
                   jax.ShapeDtypeStruct((n_blk, half, COL_BLK), BF16)],
        scratch_shapes=[pltpu.VMEM((n_blk, half, COL_BLK), F32), pltpu.VMEM((n_blk, half, COL_BLK), F32),
                        pltpu.SemaphoreType.DMA((n_blk,)), pltpu.SemaphoreType.DMA((n_blk,))],
        compiler_params=_params(dimension_semantics=("arbitrary",)),
    )(h, dp)


def _add_units(grad, recv, place, tile, name):
    n, rows, cols = grad.shape
    per_half = rows // 2 // tile

    def body(place_ref, g_ref, r_ref, o32_ref, o16_ref):
        v = g_ref[...] + r_ref[...]
        o32_ref[...] = v
        o16_ref[...] = v.astype(BF16)

    blk = lambda f: pl.BlockSpec((None, tile, cols), f)
    out = lambda s, i, p: (s, i, 0)
    return pl.pallas_call(
        body, name=name,
        grid_spec=pltpu.PrefetchScalarGridSpec(
            num_scalar_prefetch=1, grid=(n, per_half),
            in_specs=[blk(lambda s, i, p: (s, p[0] * per_half + i, 0)), blk(out)],
            out_specs=[blk(out), blk(out)]),
        out_shape=[jax.ShapeDtypeStruct(recv.shape, F32), jax.ShapeDtypeStruct(recv.shape, BF16)],
        compiler_params=_params(dimension_semantics=("arbitrary", "arbitrary")),
    )(place, grad, recv)


def _sum_units(part32, recv16, place, tile, name):
    n, half, cols = part32.shape
    per_half = half // tile

    def body(place_ref, p_ref, r_ref, o_ref):
        acc = p_ref[...]
        for j in range(len(CHIP_FLIPS)):
            acc = acc + r_ref[j].astype(F32)
        o_ref[...] = acc

    return pl.pallas_call(
        body, name=name,
        grid_spec=pltpu.PrefetchScalarGridSpec(
            num_scalar_prefetch=1, grid=(n, per_half),
            in_specs=[pl.BlockSpec((None, tile, cols), lambda s, i, p: (s, i, 0)),
                      pl.BlockSpec((None, len(CHIP_FLIPS), tile, cols), lambda s, i, p: (s, 0, i, 0))],
            out_specs=pl.BlockSpec((None, tile, cols), lambda s, i, p: (s, p[0] * per_half + i, 0))),
        out_shape=jax.ShapeDtypeStruct((n, 2 * half, cols), F32),
        compiler_params=_params(dimension_semantics=("arbitrary", "arbitrary")),
    )(place, part32, recv16)


def _adamw_math(w, g, m, v):
    m = ADAM_B1 * m + (1.0 - ADAM_B1) * g
    v = ADAM_B2 * v + (1.0 - ADAM_B2) * (g * g)
    m_hat = m / (1.0 - ADAM_B1 ** ADAM_STEP)
    v_hat = v / (1.0 - ADAM_B2 ** ADAM_STEP)
    delta = -ADAM_LR * (m_hat / (jnp.sqrt(v_hat) + ADAM_EPS) + ADAM_WD * w)
    return delta, m, v


def _adamw_units(w, m, v, grads, pick, name):
    rows, cols = w.shape
    bc = grads[0].shape[-1]
    tile = min(rows, 256)
    n_g = len(grads)

    def body(pick_ref, w_ref, m_ref, v_ref, *refs):
        g_refs, (g_out, d_ref, nm_ref, nv_ref) = refs[:n_g], refs[n_g:]
        p = pl.program_id(0)
        for a in range(n_g):
            @pl.when(pick_ref[0, p] == a)
            def _(a=a):
                g = g_refs[a][...]
                g_out[...] = g
                d_ref[...], nm_ref[...], nv_ref[...] = _adamw_math(w_ref[...], g, m_ref[...], v_ref[...])

    blk = pl.BlockSpec((tile, bc), lambda p, i, pick: (i, p))

    def g_spec(a):
        return pl.BlockSpec((None, tile, bc),
                            lambda p, i, pick: (jnp.where(pick[0, p] == a, pick[1, p], 0), i, 0))

    return pl.pallas_call(
        body, name=name,
        grid_spec=pltpu.PrefetchScalarGridSpec(
            num_scalar_prefetch=1, grid=(cols // bc, rows // tile),
            in_specs=[blk] * 3 + [g_spec(a) for a in range(n_g)],
            out_specs=[blk] * 4),
        out_shape=[jax.ShapeDtypeStruct(w.shape, F32)] * 4,
        compiler_params=_params(dimension_semantics=("arbitrary", "arbitrary")),
    )(pick, w, m, v, *grads)


ROW_NORM1, ROW_SCALE, ROW_LB, ROW_REC, ROW_FINAL, ROW_LOSS = 0, 1, 2, 4, 5, 6


def _small_update(parts, gathered, w, m, v):
    def body(own_ref, p_ref, w_ref, m_ref, v_ref, loss_ref, g_ref, d_ref, nm_ref, nv_ref):
        x, y, c = _place()
        me = 4 * x + 2 * y + c
        slot = lambda d: jnp.where(me == d, own_ref[...], p_ref[d])
        tot = slot(0)
        for d in range(1, 8):
            tot = tot + slot(d)
        wv = w_ref[...]
        l0 = wv[ROW_LB:ROW_LB + 1, :]
        l1 = wv[ROW_LB + 1:ROW_LB + 2, :]
        mx = jnp.maximum(l0, l1)
        e0 = jnp.exp(l0 - mx)
        e1 = jnp.exp(l1 - mx)
        lb = e0 / (e0 + e1)
        dl0 = tot[ROW_LB:ROW_LB + 1, :] * lb * (1.0 - lb)
        row = lax.broadcasted_iota(jnp.int32, tot.shape, 0)
        g = jnp.where(row == ROW_LB, dl0, jnp.where(row == ROW_LB + 1, -dl0, tot))
        g = jnp.where(row >= ROW_LOSS, 0.0, g)
        g_ref[...] = g
        d_ref[...], nm_ref[...], nv_ref[...] = _adamw_math(wv, g, m_ref[...], v_ref[...])
        loss_ref[...] = (0.5 / D_MODEL) * jnp.sum(tot[ROW_LOSS:ROW_LOSS + 1, :], axis=-1, keepdims=True)

    return pl.pallas_call(
        body, name="small_update",
        out_shape=[jax.ShapeDtypeStruct((1, 1), F32)] + [jax.ShapeDtypeStruct(w.shape, F32)] * 4,
        compiler_params=_params(),
    )(parts, gathered, w, m, v)


SHARD_OWNERS = tuple(range(N_SHARDS))
BLOCKS_POOL = (0, 1, 2, 3)
BLOCKS_A = (4, 6, 8, 10)
BLOCKS_B = (5, 7, 9, 11)
BLOCK_GROUPS = (BLOCKS_POOL, BLOCKS_A, BLOCKS_B)


def _block_owners(blocks):
    return tuple(j // (W_IN_SHARD // COL_BLK) for j in blocks)


def _small_rows(norm1, scale, lb, rec, final):
    pad = jnp.zeros((2, D_MODEL), F32)
    return jnp.concatenate([norm1, scale, lb, rec, final.reshape(1, D_MODEL), pad], axis=0)


def kernel(x, norm1_g, w_in, pool_w, pool_scale, lb_logits, rec_norm_g, w_out, final_norm_g, loss_target, m_norm1_g, m_w_in, m_pool_w, m_pool_scale, m_lb_logits, m_rec_norm_g, m_w_out, m_final_norm_g, v_norm1_g, v_w_in, v_pool_w, v_pool_scale, v_lb_logits, v_rec_norm_g, v_w_out, v_final_norm_g):
    xi, yi, ci = _place()
    chip = 2 * xi + yi
    place = jnp.stack([ci, chip]).astype(jnp.int32)
    pw_rows = N_GROUPS * PW_SHARD
    flat_pw = lambda a: a.reshape(pw_rows, PG)
    x2, target, gf = x[0], loss_target[0], final_norm_g.reshape(1, D_MODEL)
    consts = {n: jnp.asarray(a, BF16 if n.startswith("tri") else F32) for n, a in _chunk_constants().items()}

    proj, h, w_in_g = _in_proj(x2, norm1_g, _cast_own(w_in[0], place, "cast_w_in"), place)
    (y_rec, o_raw, st_prev), ((w_out_g, pw_g),) = _rec_fwd(
        proj, lb_logits, rec_norm_g, consts,
        [_ex_gather([_cast_own(w_out[0], place, "cast_w_out"), _cast_own(flat_pw(pool_w), place, "cast_pool_w")])])
    w_out_g = w_out_g.reshape(2 * D_MODEL, D_MODEL)
    pw_full = pw_g.reshape(N_SHARDS, N_GROUPS, PW_SHARD, PG).transpose(1, 0, 2, 3).reshape(N_GROUPS, PG, PG)
    y_pool = _pool_fwd(proj, pw_full, pool_scale)
    dout, dout_b, part_out = _out_proj_loss(y_pool, y_rec, w_out_g, x2, target, gf)

    p_out32, p_out16 = _grad_w_out(y_pool, y_rec, dout_b)
    (dpool, gpw, dscale), ((rb_out,),) = _pool_bwd(proj, dout_b, w_out_g, pw_full, pool_scale,
                                                   [_ex_send([p_out16], [SHARD_OWNERS])])
    g_out = _sum_units(p_out32, rb_out, place, 256, "sum_w_out")
    gpw = gpw.reshape(N_GROUPS, N_SHARDS, PW_SHARD, PG).transpose(1, 0, 2, 3).reshape(N_SHARDS, pw_rows, PG)
    p_inp32, p_inp16 = _grad_w_in(h, dpool, [(0, 0), (0, 1), (1, 0), (1, 1)], "grad_w_in_pool")

    rec_args = (proj, o_raw, st_prev, dout_b, w_out_g, lb_logits, rec_norm_g, consts)
    (drec_a, part_a), ((rb_inp,), (ra_pw,)) = _rec_bwd(
        *rec_args, 0, "rec_bwd_a", [_ex_send([p_inp16], [_block_owners(BLOCKS_POOL)]), _ex_swap([gpw])])
    g_inp = _sum_units(p_inp32, rb_inp, place, 256, "sum_w_in_pool")
    p_pw32, p_pw16 = _add_units(gpw, ra_pw, place, 128, "add_pool_w")
    rec_blocks = [(n, 0) for n in range(4)]
    p_ina32, p_ina16 = _grad_w_in(h, drec_a, rec_blocks, "grad_w_in_a")

    (drec_b, part_b), ((rb_ina, rb_pw),) = _rec_bwd(
        *rec_args, HALF_HEADS, "rec_bwd_b",
        [_ex_send([p_ina16, p_pw16], [_block_owners(BLOCKS_A), SHARD_OWNERS])])
    g_ina = _sum_units(p_ina32, rb_ina, place, 256, "sum_w_in_a")
    g_pw = _sum_units(p_pw32, rb_pw, place, 128, "sum_pool_w")
    p_inb32, p_inb16 = _grad_w_in(h, drec_b, rec_blocks, "grad_w_in_b")

    dproj = ([(dpool, 0, 0), (dpool, 0, 1), (dpool, 1, 0), (dpool, 1, 1)]
             + [(d, n, 0) for n in range(4) for d in (drec_a, drec_b)])
    (dx, part_x), ((rb_inb,),) = _grad_x(dproj, w_in_g, x2, norm1_g, dout,
                                         [_ex_send([p_inb16], [_block_owners(BLOCKS_B)])])
    g_inb = _sum_units(p_inb32, rb_inb, place, 256, "sum_w_in_b")
    zero = jnp.zeros((1, D_MODEL), F32)
    part_rec = jnp.concatenate([part_a, part_b], axis=1)
    parts = jnp.concatenate([part_x[0:1], dscale, part_rec[1:2], zero, part_rec[0:1], part_out[0:1],
                             part_out[1:2], zero], axis=0)
    _, ((g_out, g_pw, g_inp, g_ina, g_inb), (gathered,)) = _call(
        None, name="join_halves",
        exchanges=[_ex_join([g_out, g_pw, g_inp, g_ina, g_inb],
                            [SHARD_OWNERS, SHARD_OWNERS] + [_block_owners(b) for b in BLOCK_GROUPS]),
                   _ex_gather_small(parts)])

    group_of = np.zeros((D_PROJ // COL_BLK,), np.int32)
    index_of = np.zeros((D_PROJ // COL_BLK,), np.int32)
    for gi, blocks in enumerate(BLOCK_GROUPS):
        for i, j in enumerate(blocks):
            group_of[j], index_of[j] = gi, i
    per_shard = W_IN_SHARD // COL_BLK
    pick_in = jnp.stack([lax.dynamic_slice(jnp.asarray(group_of), (per_shard * chip,), (per_shard,)),
                         lax.dynamic_slice(jnp.asarray(index_of), (per_shard * chip,), (per_shard,))])
    pick_own = jnp.stack([jnp.zeros((1,), jnp.int32), chip.reshape(1).astype(jnp.int32)])
    big = [_adamw_units(w_in[0], m_w_in[0], v_w_in[0], [g_inp, g_ina, g_inb], pick_in, "adamw_w_in"),
           _adamw_units(w_out[0], m_w_out[0], v_w_out[0], [g_out], pick_own, "adamw_w_out"),
           _adamw_units(flat_pw(pool_w), flat_pw(m_pool_w), flat_pw(v_pool_w), [g_pw], pick_own, "adamw_pool_w")]

    small_w = _small_rows(norm1_g, pool_scale, lb_logits, rec_norm_g, final_norm_g)
    small_m = _small_rows(m_norm1_g, m_pool_scale, m_lb_logits, m_rec_norm_g, m_final_norm_g)
    small_v = _small_rows(v_norm1_g, v_pool_scale, v_lb_logits, v_rec_norm_g, v_final_norm_g)
    loss, *small = _small_update(parts, gathered, small_w, small_m, small_v)

    def leaves(k):
        s = small[k]
        return (s[ROW_NORM1:ROW_NORM1 + 1], big[0][k][None], big[2][k].reshape(pool_w.shape),
                s[ROW_SCALE:ROW_SCALE + 1], s[ROW_LB:ROW_LB + 2], s[ROW_REC:ROW_REC + 1],
                big[1][k][None], s[ROW_FINAL])

    return (loss.reshape(()), dx[None], *leaves(0), *leaves(1), *leaves(2), *leaves(3))
```

```python
import functools

import numpy as np
import jax
import jax.numpy as jnp
from jax import lax
from jax.experimental import pallas as pl
from jax.experimental.pallas import tpu as pltpu

F32 = jnp.float32
BF16 = jnp.bfloat16

SEQ = 2048
D_MODEL = 1024
D_PROJ = 6144
N_SEC = 6
N_GROUPS = 4
PG = 256
N_HEADS = 8
HEAD = 128
CHUNK = 64
N_LEVELS = 6
N_SHARDS = 4
W_IN_SHARD = D_PROJ // N_SHARDS
W_OUT_SHARD = 2048 // N_SHARDS
PW_SHARD = PG // N_SHARDS
COL_BLK = 512
EPS = 1e-6

ADAM_LR = 0.001
ADAM_B1 = 0.9
ADAM_B2 = 0.999
ADAM_EPS = 1e-08
ADAM_WD = 0.01
ADAM_STEP = 10

V7X_VMEM_LIMIT = 56 * 1024 * 1024
MESH = pl.DeviceIdType.MESH


def _params(**kw):
    return pltpu.CompilerParams(vmem_limit_bytes=V7X_VMEM_LIMIT, **kw)


def _sig(x):
    return 1.0 / (1.0 + jnp.exp(-x))


def _dot(a, b):
    return jnp.dot(a, b, preferred_element_type=F32)


def _dot_nt(a, b):
    return lax.dot_general(a, b, (((1,), (1,)), ((), ())), preferred_element_type=F32)


def _dot_tn(a, b):
    return lax.dot_general(a, b, (((0,), (0,)), ((), ())), preferred_element_type=F32)


def _split3(a):
    p1 = a.astype(BF16)
    r1 = a - p1.astype(F32)
    p2 = r1.astype(BF16)
    p3 = (r1 - p2.astype(F32)).astype(BF16)
    return jnp.concatenate([p1, p2, p3], axis=-1)


def _dot3(w01, a):
    n = a.shape[-1]
    r = _dot(w01, _split3(a))
    return r[:, :n] + r[:, n:2 * n] + r[:, 2 * n:]


def _chunk_constants():
    j = np.arange(CHUNK)
    tt, ss = np.meshgrid(j, j, indexing="ij")
    x = tt ^ ss
    hb = np.full((CHUNK, CHUNK), -1, np.int32)
    for l in range(N_LEVELS):
        hb[x >= (1 << l)] = l
    sym = np.stack([(hb == l) for l in range(N_LEVELS)]).astype(np.float32)
    low = sym * (tt > ss)
    sign = np.stack([np.where((j >> l) & 1, 1.0, -1.0) for l in range(N_LEVELS)]).astype(np.float32)
    sign = np.ascontiguousarray(np.broadcast_to(sign[:, :, None], (N_LEVELS, CHUNK, HEAD)))
    tri = (ss <= tt).astype(np.float32)
    return dict(tri=tri, tri_t=np.ascontiguousarray(tri.T), low=low,
                low_t=np.ascontiguousarray(low.transpose(0, 2, 1)), sym=sym, sign=sign)


def _in_proj(x, g1, w_slots, place):
    n_col = D_PROJ // COL_BLK
    per_shard = W_IN_SHARD // COL_BLK
    n_fl = len(CHIP_FLIPS)
    rows = 1024

    def shard_at(m, chip):
        return chip ^ jnp.where(m == 0, 0, jnp.where(m == 1, 2, jnp.where(m == 2, 1, 3)))

    def body(place_ref, x_ref, g_ref, w_in_ref, proj_ref, h_ref, w_ref, wbuf, load_sems, send_sems, recv_sems):
        t = pl.program_id(0)
        x_, y_, c = _place()
        chip = 2 * x_ + y_
        piece = lambda shard, half: _half_rows(w_ref.at[shard], half)

        def sends():
            return [_remote(piece(chip, c), piece(chip, c), send_sems, recv_sems, j, (x_ ^ fx, y_ ^ fy, c))
                    for j, (fx, fy) in enumerate(CHIP_FLIPS)]

        def passed_on(j, half, to):
            sj = shard_at(j + 1, chip)
            return _remote(piece(sj, half), piece(sj, half), send_sems, recv_sems, n_fl + j, to)

        def load(shard, p):
            return pltpu.make_async_copy(w_ref.at[shard, :, pl.ds(p * COL_BLK, COL_BLK)], wbuf.at[p],
                                         load_sems.at[p])

        @pl.when(t == 0)
        def _():
            for cp in sends():
                cp.start()

            def norm(i, _):
                r = pl.ds(pl.multiple_of(i * rows, rows), rows)
                xv = x_ref[r, :]
                inv = lax.rsqrt(jnp.mean(xv * xv, axis=-1, keepdims=True) + EPS)
                h_ref[r, :] = (xv * inv * g_ref[...]).astype(BF16)
                return 0
            lax.fori_loop(0, SEQ // rows, norm, 0)

        for m in range(N_SHARDS):
            @pl.when(t == per_shard * m)
            def _(m=m):
                if m > 0:
                    j = m - 1
                    sj = shard_at(m, chip)
                    _remote(piece(sj, c), piece(sj, c), send_sems, recv_sems, j, (x_, y_, c)).wait_recv()
                    passed_on(j, c, (x_, y_, 1 - c)).start()
                    passed_on(j, 1 - c, (x_, y_, c)).wait_recv()
                for p in range(per_shard):
                    load(shard_at(m, chip), p).start()

        for p in range(per_shard):
            @pl.when(t % per_shard == p)
            def _(p=p):
                load(0, p).wait()

                def mm(i, _):
                    r = pl.ds(pl.multiple_of(i * rows, rows), rows)
                    proj_ref[r, :] = _dot(h_ref[r, :], wbuf[p])
                    return 0
                lax.fori_loop(0, SEQ // rows, mm, 0)

        @pl.when(t == n_col - 1)
        def _():
            for cp in sends():
                cp.wait_send()
            for j in range(n_fl):
                passed_on(j, c, (x_, y_, 1 - c)).wait_send()

    return pl.pallas_call(
        body, name="in_proj",
        grid_spec=pltpu.PrefetchScalarGridSpec(
            num_scalar_prefetch=1, grid=(n_col,),
            in_specs=[pl.BlockSpec((SEQ, D_MODEL), lambda t, p: (0, 0)),
                      pl.BlockSpec((1, D_MODEL), lambda t, p: (0, 0)),
                      pl.BlockSpec(memory_space=pl.ANY)],
            out_specs=[pl.BlockSpec((SEQ, COL_BLK),
                                    lambda t, p: (0, per_shard * shard_at(t // per_shard, p[1]) + t % per_shard)),
                       pl.BlockSpec((SEQ, D_MODEL), lambda t, p: (0, 0)),
                       pl.BlockSpec(memory_space=pl.ANY)],
            scratch_shapes=[pltpu.VMEM((per_shard, D_MODEL, COL_BLK), BF16),
                            pltpu.SemaphoreType.DMA((per_shard,)),
                            pltpu.SemaphoreType.DMA((2 * n_fl,)), pltpu.SemaphoreType.DMA((2 * n_fl,))]),
        out_shape=[jax.ShapeDtypeStruct((SEQ, D_PROJ), F32),
                   jax.ShapeDtypeStruct((SEQ, D_MODEL), BF16),
                   jax.ShapeDtypeStruct(w_slots.shape, BF16)],
        input_output_aliases={3: 2},
        compiler_params=_params(dimension_semantics=("arbitrary",)),
    )(place, x, g1, w_slots)


POOL_ROWS = 256
POOL_HALO = 16


def _window_sums(ext, g, shift_of):
    s = ext
    for k in range(N_GROUPS):
        s = jnp.where(k <= g, s + pltpu.roll(s, shift_of(k), 0), s)
    return s


def _pool_diff(u_ref, i, g):
    n = POOL_ROWS + POOL_HALO
    r0 = i * POOL_ROWS
    cur = u_ref[pl.ds(pl.multiple_of(r0, POOL_ROWS), POOL_ROWS), :]
    before = u_ref[pl.ds(pl.multiple_of(jnp.maximum(r0 - POOL_HALO, 0), 8), POOL_HALO), :]
    before = jnp.where(i > 0, before, 0.0)
    ext = jnp.concatenate([before, cur], axis=0)
    s = _window_sums(ext, g, lambda k: 1 << k)[POOL_HALO:, :]
    t = r0 + lax.broadcasted_iota(jnp.int32, (POOL_ROWS, 1), 0)
    width = (2 << g).astype(F32)
    inv_count = 1.0 / jnp.minimum((t + 1).astype(F32), width)
    return s * inv_count - cur, inv_count


def _pool_fwd(proj, pw_g, pool_scale):
    def body(u_ref, gate_ref, pw_ref, sc_ref, y_ref):
        g = pl.program_id(0)

        def step(i, _):
            r = pl.ds(pl.multiple_of(i * POOL_ROWS, POOL_ROWS), POOL_ROWS)
            d, _ = _pool_diff(u_ref, i, g)
            mixed = _dot(d.astype(BF16), pw_ref[...])
            gate = gate_ref[r, :]
            y_ref[r, :] = (mixed * sc_ref[...] * (gate * _sig(gate))).astype(BF16)
            return 0
        lax.fori_loop(0, SEQ // POOL_ROWS, step, 0)

    return pl.pallas_call(
        body, name="pool_fwd", grid=(N_GROUPS,),
        in_specs=[pl.BlockSpec((SEQ, PG), lambda g: (0, g)),
                  pl.BlockSpec((SEQ, PG), lambda g: (0, N_GROUPS + g)),
                  pl.BlockSpec((None, PG, PG), lambda g: (g, 0, 0)),
                  pl.BlockSpec((1, PG), lambda g: (0, g))],
        out_specs=pl.BlockSpec((SEQ, PG), lambda g: (0, g)),
        out_shape=jax.ShapeDtypeStruct((SEQ, D_MODEL), BF16),
        compiler_params=_params(dimension_semantics=("arbitrary",)),
    )(proj, proj, pw_g, pool_scale)


REC_ROWS = 1024
REC_CHUNKS = REC_ROWS // CHUNK
N_REC_BLK = SEQ // REC_ROWS
REC_GROUP = REC_CHUNKS
SEC_BLK = D_MODEL // HEAD


def _lower_bound(lb_ref):
    l0 = lb_ref[0:1, :]
    l1 = lb_ref[1:2, :]
    mx = jnp.maximum(l0, l1)
    e0 = jnp.exp(l0 - mx)
    e1 = jnp.exp(l1 - mx)
    return e0 / (e0 + e1)


def _gates(q, fl, lb):
    qs = q * _sig(q)
    sf = _sig(fl)
    f = lb + (1.0 - lb) * sf
    return qs, sf, f, 1.0 - f, jnp.log(f)


LOG2E = 1.4426950408889634


def _level_factors(g2, qs, k, sign_ref):
    t = lax.broadcasted_iota(jnp.int32, (CHUNK, HEAD), 0)
    row = lambda r, n: jnp.broadcast_to(g2[r:r + 1, :], (n, HEAD))
    out = []
    for l in range(N_LEVELS):
        m = 1 << l
        if l == 0:
            g_mid = jnp.where((t & 1) == 1, pltpu.roll(g2, 1, 0), g2)
        elif l == 1:
            low = (t & 7) < 4
            g_mid = jnp.concatenate([jnp.where(low[:8], row(8 * i + 1, 8), row(8 * i + 5, 8))
                                     for i in range(CHUNK // 8)], axis=0)
        else:
            g_mid = jnp.concatenate([row(b * 2 * m + m - 1, 2 * m) for b in range(CHUNK // (2 * m))], axis=0)
        sgn = sign_ref[l]
        up = sgn > 0.0
        e = jnp.exp2((g2 - g_mid) * sgn)
        x = jnp.where(up, qs, k) * e
        hi = x.astype(BF16)
        out.append((hi, (x - hi.astype(F32)).astype(BF16), e, up))
    return out


CHIP_FLIPS = ((1, 0), (0, 1), (1, 1))
HBM = pl.BlockSpec(memory_space=pl.ANY)


def _place():
    return lax.axis_index("x"), lax.axis_index("y"), lax.axis_index("c")


def _remote(src, dst, send_sems, recv_sems, k, to):
    return pltpu.make_async_remote_copy(src_ref=src, dst_ref=dst, send_sem=send_sems.at[k],
                                        recv_sem=recv_sems.at[k], device_id=to, device_id_type=MESH)


def _half_rows(ref, c):
    half = ref.shape[-2] // 2
    rows = pl.ds(pl.multiple_of(c * half, half), half)
    return ref.at[:, rows, :] if len(ref.shape) == 3 else ref.at[rows, :]


class _Exchange:
    def __init__(self, inputs, out_shapes, n_sems, start, finish, aliases=None):
        self.inputs, self.out_shapes, self.n_sems = list(inputs), list(out_shapes), n_sems
        self.start, self.finish, self.aliases = start, finish, dict(aliases or {})


def _ex_swap(grads):
    def copies(ins, outs, send, recv):
        x, y, c = _place()
        return [_remote(_half_rows(g, 1 - c), o, send, recv, t, (x, y, 1 - c))
                for t, (g, o) in enumerate(zip(ins, outs))]

    def start(*refs):
        for cp in copies(*refs):
            cp.start()

    def finish(*refs):
        cps = copies(*refs)
        for cp in cps:
            cp.wait_recv()
        for cp in cps:
            cp.wait_send()

    shapes = [jax.ShapeDtypeStruct((a.shape[0], a.shape[1] // 2, a.shape[2]), F32) for a in grads]
    return _Exchange(grads, shapes, len(grads), start, finish)


def _ex_send(parts16, owners, units=None, landed=None):
    n_t = len(parts16)
    units = units or [tuple(range(len(o))) for o in owners]
    landed = landed or [None] * n_t
    given = [t for t in range(n_t) if landed[t] is not None]

    def each(ins, outs, send, recv, to_sender, to_owner):
        x, y, c = _place()
        k = 0
        for t, own in enumerate(owners):
            for j in units[t]:
                for r, (fx, fy) in enumerate(CHIP_FLIPS):
                    tx, ty = x ^ fx, y ^ fy
                    cp = _remote(ins[t].at[j], outs[t].at[j, r], send, recv, k, (tx, ty, c))
                    if to_sender is not None:
                        pl.when(2 * tx + ty == own[j])(functools.partial(to_sender, cp))
                    if to_owner is not None:
                        pl.when(2 * x + y == own[j])(functools.partial(to_owner, cp))
                    k += 1

    def start(*refs):
        each(*refs, lambda cp: cp.start(), None)

    def finish(*refs):
        each(*refs, None, lambda cp: cp.wait_recv())
        each(*refs, lambda cp: cp.wait_send(), None)

    shapes = [jax.ShapeDtypeStruct((a.shape[0], len(CHIP_FLIPS)) + a.shape[1:], BF16) for a in parts16]
    return _Exchange(list(parts16) + [landed[t] for t in given], shapes,
                     len(CHIP_FLIPS) * sum(len(u) for u in units), start, finish,
                     aliases={n_t + i: t for i, t in enumerate(given)})


def _ex_join(units, owners):
    def each(ins, outs, send, recv, fn):
        x, y, c = _place()
        k = 0
        for t, own in enumerate(owners):
            for j, o in enumerate(own):
                def half(cc, to, u=outs[t].at[j], k=k):
                    return _remote(_half_rows(u, cc), _half_rows(u, cc), send, recv, k, to)
                mine = functools.partial(half, c, (x, y, 1 - c))
                theirs = functools.partial(half, 1 - c, (x, y, c))
                pl.when(2 * x + y == o)(functools.partial(fn, mine, theirs))
                k += 1

    def start(*refs):
        each(*refs, lambda mine, theirs: mine().start())

    def finish(*refs):
        each(*refs, lambda mine, theirs: theirs().wait_recv())
        each(*refs, lambda mine, theirs: mine().wait_send())

    shapes = [jax.ShapeDtypeStruct(a.shape, F32) for a in units]
    return _Exchange(units, shapes, sum(len(o) for o in owners), start, finish,
                     aliases={t: t for t in range(len(units))})


def _ex_gather(slots):
    n_t = len(slots)
    n_fl = len(CHIP_FLIPS)

    def piece(ref, shard, half):
        return _half_rows(ref.at[shard], half)

    def first(outs, send, recv):
        x, y, c = _place()
        s = 2 * x + y
        return [_remote(piece(outs[t], s, c), piece(outs[t], s, c), send, recv, n_t * j + t, (x ^ fx, y ^ fy, c))
                for j, (fx, fy) in enumerate(CHIP_FLIPS) for t in range(n_t)]

    def start(ins, outs, send, recv):
        for cp in first(outs, send, recv):
            cp.start()

    def finish(ins, outs, send, recv):
        x, y, c = _place()
        passed = []
        for j, (fx, fy) in enumerate(CHIP_FLIPS):
            sj = 2 * (x ^ fx) + (y ^ fy)
            for t in range(n_t):
                k = n_t * j + t
                _remote(piece(outs[t], sj, c), piece(outs[t], sj, c), send, recv, k, (x, y, c)).wait_recv()
                cp = _remote(piece(outs[t], sj, c), piece(outs[t], sj, c), send, recv, n_t * n_fl + k, (x, y, 1 - c))
                cp.start()
                passed.append(cp)
        for j, (fx, fy) in enumerate(CHIP_FLIPS):
            sj = 2 * (x ^ fx) + (y ^ fy)
            for t in range(n_t):
                k = n_t * n_fl + n_t * j + t
                _remote(piece(outs[t], sj, 1 - c), piece(outs[t], sj, 1 - c), send, recv, k, (x, y, c)).wait_recv()
        for cp in first(outs, send, recv) + passed:
            cp.wait_send()

    shapes = [jax.ShapeDtypeStruct(a.shape, BF16) for a in slots]
    return _Exchange(slots, shapes, 2 * n_t * n_fl, start, finish, aliases={t: t for t in range(n_t)})


def _ex_gather_small(parts):
    def copies(ins, outs, send, recv):
        x, y, c = _place()
        me = 4 * x + 2 * y + c
        return [_remote(ins[0], outs[0].at[me], send, recv, mask - 1,
                        (x ^ (mask >> 2), y ^ ((mask >> 1) & 1), c ^ (mask & 1))) for mask in range(1, 8)]

    def start(*refs):
        for cp in copies(*refs):
            cp.start()

    def finish(ins, outs, send, recv):
        x, y, c = _place()
        me = 4 * x + 2 * y + c
        for mask in range(1, 8):
            _remote(ins[0], outs[0].at[me ^ mask], send, recv, mask - 1, (x, y, c)).wait_recv()
        for cp in copies(ins, outs, send, recv):
            cp.wait_send()

    return _Exchange([parts], [jax.ShapeDtypeStruct((8,) + parts.shape, F32)], 7, start, finish)


def _call(body, *, name, args=(), in_specs=(), out_specs=(), out_shape=(), grid=(), scratch_shapes=(),
          exchanges=()):
    n_in, n_out, n_scr = len(args), len(out_shape), len(scratch_shapes)
    ex_in, ex_out, ex_scr, spans, alias = [], [], [], [], {}
    for ex in exchanges:
        spans.append((len(ex_in), len(ex.inputs), len(ex_out), len(ex.out_shapes)))
        for i, o in ex.aliases.items():
            alias[n_in + len(ex_in) + i] = n_out + len(ex_out) + o
        ex_in += ex.inputs
        ex_out += ex.out_shapes
        ex_scr += [pltpu.SemaphoreType.DMA((ex.n_sems,)), pltpu.SemaphoreType.DMA((ex.n_sems,))]

    def full(*refs):
        ins, x_in = refs[:n_in], refs[n_in:n_in + len(ex_in)]
        outs = refs[n_in + len(ex_in):n_in + len(ex_in) + n_out]
        x_out = refs[n_in + len(ex_in) + n_out:n_in + len(ex_in) + n_out + len(ex_out)]
        scr = refs[len(refs) - n_scr - len(ex_scr):len(refs) - len(ex_scr)]
        sems = refs[len(refs) - len(ex_scr):]

        def run(which):
            for e, (ex, (i0, ni, o0, no)) in enumerate(zip(exchanges, spans)):
                getattr(ex, which)(x_in[i0:i0 + ni], x_out[o0:o0 + no], sems[2 * e], sems[2 * e + 1])

        if grid:
            ids = [pl.program_id(a) for a in range(len(grid))]
            is_first = functools.reduce(jnp.logical_and, [i == 0 for i in ids])
            is_last = functools.reduce(jnp.logical_and, [i == g - 1 for i, g in zip(ids, grid)])
            pl.when(is_first)(lambda: run("start"))
            body(*ins, *outs, *scr)
            pl.when(is_last)(lambda: run("finish"))
        else:
            run("start")
            if body is not None:
                body(*ins, *outs, *scr)
            run("finish")

    kw = dict(grid=grid) if grid else {}
    if grid:
        kw["compiler_params"] = _params(dimension_semantics=("arbitrary",) * len(grid))
    else:
        kw["compiler_params"] = _params()
    res = pl.pallas_call(
        full, name=name,
        in_specs=list(in_specs) + [HBM] * len(ex_in),
        out_specs=list(out_specs) + [HBM] * len(ex_out),
        out_shape=list(out_shape) + ex_out,
        scratch_shapes=list(scratch_shapes) + ex_scr,
        input_output_aliases=alias, **kw,
    )(*args, *ex_in)
    own = list(res[:n_out])
    per_ex = [list(res[n_out + o0:n_out + o0 + no]) for (_, _, o0, no) in spans]
    return own, per_ex


def _cast_own(w, place, name):
    rows, cols = w.shape
    tile = min(rows, 256)

    def body(place_ref, w_ref, o_ref):
        o_ref[...] = w_ref[...].astype(BF16)

    return pl.pallas_call(
        body, name=name,
        grid_spec=pltpu.PrefetchScalarGridSpec(
            num_scalar_prefetch=1, grid=(rows // tile,),
            in_specs=[pl.BlockSpec((tile, cols), lambda i, p: (i, 0))],
            out_specs=pl.BlockSpec((None, tile, cols), lambda i, p: (p[1], i, 0))),
        out_shape=jax.ShapeDtypeStruct((N_SHARDS, rows, cols), BF16),
        compiler_params=_params(dimension_semantics=("arbitrary",)),
    )(place, w)


def _rec_fwd(proj, lb_logits, rec_g, consts, exchanges):
    tri, low, sign = consts["tri"], consts["low"], consts["sign"]

    def body(q_ref, f_ref, i_ref, rg_ref, lb_ref, g_ref, w_ref, low_ref, sign_ref, y_ref, o_ref, stp_ref, st_ref):
        @pl.when(pl.program_id(1) == 0)
        def _():
            st_ref[...] = jnp.zeros_like(st_ref)
        lb = _lower_bound(lb_ref)
        st = st_ref[...]
        rows = lambda c: pl.ds(c * CHUNK, CHUNK)
        for c0 in range(0, REC_CHUNKS, REC_GROUP):
            group = range(c0, c0 + REC_GROUP)
            gated = [_gates(q_ref[rows(c), :], f_ref[rows(c), :], lb) for c in group]
            g2s = [_dot3(w_ref[...], g) * LOG2E for (_, _, _, _, g) in gated]
            xs = [[xl for xl, _, _, _ in _level_factors(g2, qs, k, sign_ref)]
                  for g2, (qs, _, _, k, _) in zip(g2s, gated)]
            a_s = []
            for x in xs:
                a = jnp.zeros((CHUNK, CHUNK), F32)
                for l, xl in enumerate(x):
                    a = a + _dot_nt(xl, xl) * low_ref[l]
                a_s.append(a.astype(BF16))
            vbs = [i_ref[rows(c), :].astype(BF16) for c in group]
            intra = [_dot(a, vb) for a, vb in zip(a_s, vbs)]
            kvs = [_dot_tn(vb, (k * jnp.exp2(g2[CHUNK - 1:CHUNK, :] - g2)).astype(BF16))
                   for vb, g2, (_, _, _, k, _) in zip(vbs, g2s, gated)]
            for i, c in enumerate(group):
                qs, _, _, k, _ = gated[i]
                g2 = g2s[i]
                stp_ref[c] = st
                v = i_ref[rows(c), :]
                rg = rg_ref[rows(c), :]
                o = (intra[i] + jnp.sum(qs * k, axis=-1, keepdims=True) * v
                     + _dot_nt((qs * jnp.exp2(g2)).astype(BF16), st.astype(BF16)))
                st = st * jnp.exp2(g2[CHUNK - 1:CHUNK, :]) + kvs[i]
                o_ref[rows(c), :] = o
                inv = lax.rsqrt(jnp.mean(o * o, axis=-1, keepdims=True) + EPS)
                y_ref[rows(c), :] = (o * inv * g_ref[...] * (rg * _sig(rg))).astype(BF16)
        st_ref[...] = st

    sec = lambda n: pl.BlockSpec((REC_ROWS, HEAD), lambda h, b: (b, n * SEC_BLK + h))
    vec = lambda rows: pl.BlockSpec((rows, HEAD), lambda h, b: (0, h))
    full = lambda a: pl.BlockSpec(a.shape, lambda h, b: (0,) * a.ndim)
    return _call(
        body, name="rec_fwd", grid=(N_HEADS, N_REC_BLK),
        args=(proj, proj, proj, proj, lb_logits, rec_g, tri, low, sign),
        in_specs=[sec(2), sec(3), sec(4), sec(5), vec(2), vec(1), full(tri), full(low), full(sign)],
        out_specs=[pl.BlockSpec((REC_ROWS, HEAD), lambda h, b: (b, h)),
                   pl.BlockSpec((REC_ROWS, HEAD), lambda h, b: (b, h)),
                   pl.BlockSpec((None, REC_CHUNKS, HEAD, HEAD), lambda h, b: (h, b, 0, 0))],
        out_shape=[jax.ShapeDtypeStruct((SEQ, D_MODEL), BF16),
                   jax.ShapeDtypeStruct((SEQ, D_MODEL), F32),
                   jax.ShapeDtypeStruct((N_HEADS, SEQ // CHUNK, HEAD, HEAD), F32)],
        scratch_shapes=[pltpu.VMEM((HEAD, HEAD), F32)],
        exchanges=exchanges)


OUT_ROWS = 256


def _out_proj_loss(y_pool, y_rec, w_out_g, x, target, gf):
    def body(yp_ref, yr_ref, w_ref, x_ref, t_ref, gf_ref, dout_ref, doutb_ref, part_ref):
        @pl.when(pl.program_id(0) == 0)
        def _():
            part_ref[...] = jnp.zeros_like(part_ref)
        out = (x_ref[...] + _dot(yp_ref[...], w_ref[0:D_MODEL, :])
               + _dot(yr_ref[...], w_ref[D_MODEL:2 * D_MODEL, :]))
        inv = lax.rsqrt(jnp.mean(out * out, axis=-1, keepdims=True) + EPS)
        gf_v = gf_ref[...]
        diff = out * inv * gf_v - t_ref[...]
        dyf = diff * (1.0 / D_MODEL)
        a = dyf * gf_v
        dout = inv * a - out * (inv * inv * inv) * jnp.mean(a * out, axis=-1, keepdims=True)
        dout_ref[...] = dout
        doutb_ref[...] = dout.astype(BF16)
        part_ref[0:1, :] += jnp.sum(dyf * out * inv, axis=0, keepdims=True)
        part_ref[1:2, :] += jnp.sum(diff * diff, axis=0, keepdims=True)

    row = lambda n: pl.BlockSpec((OUT_ROWS, n), lambda i: (i, 0))
    return pl.pallas_call(
        body, name="out_proj_loss", grid=(SEQ // OUT_ROWS,),
        in_specs=[row(D_MODEL), row(D_MODEL), pl.BlockSpec((2 * D_MODEL, D_MODEL), lambda i: (0, 0)),
                  row(D_MODEL), row(D_MODEL), pl.BlockSpec((1, D_MODEL), lambda i: (0, 0))],
        out_specs=[row(D_MODEL), row(D_MODEL), pl.BlockSpec((8, D_MODEL), lambda i: (0, 0))],
        out_shape=[jax.ShapeDtypeStruct((SEQ, D_MODEL), F32),
                   jax.ShapeDtypeStruct((SEQ, D_MODEL), BF16),
                   jax.ShapeDtypeStruct((8, D_MODEL), F32)],
        compiler_params=_params(dimension_semantics=("arbitrary",)),
    )(y_pool, y_rec, w_out_g, x, target, gf)


def _grad_w_out(y_pool, y_rec, dout_b):
    blk = W_OUT_SHARD // 2
    per = D_MODEL // blk
    n = 2 * per

    def body(yp_ref, yr_ref, d_ref, p32_ref, p16_ref, send_ref, recv_ref, send_sems, recv_sems):
        j = pl.program_id(0)
        x, y, c = _place()

        def copy(u):
            return _remote(send_ref.at[u], recv_ref.at[u], send_sems, recv_sems, u, (x, y, 1 - c))

        for i in range(n):
            @pl.when(j == i)
            def _(i=i):
                res = _dot_tn((yp_ref if i < per else yr_ref)[...], d_ref[...])

                @pl.when(i % 2 == c)
                def _():
                    p32_ref[i // 2] = res

                @pl.when(i % 2 != c)
                def _():
                    send_ref[i // 2] = res
                    copy(i // 2).start()

        @pl.when(j == n - 1)
        def _():
            for u in range(N_SHARDS):
                copy(u).wait_recv()
                tot = p32_ref[u] + recv_ref[u]
                p32_ref[u] = tot
                p16_ref[u] = tot.astype(BF16)
            for u in range(N_SHARDS):
                copy(u).wait_send()

    whole = pl.BlockSpec((N_SHARDS, blk, D_MODEL), lambda j: (0, 0, 0))
    return pl.pallas_call(
        body, name="grad_w_out", grid=(n,),
        in_specs=[pl.BlockSpec((SEQ, blk), lambda j: (0, jnp.minimum(j, per - 1))),
                  pl.BlockSpec((SEQ, blk), lambda j: (0, jnp.maximum(j - per, 0))),
                  pl.BlockSpec((SEQ, D_MODEL), lambda j: (0, 0))],
        out_specs=[whole, whole],
        out_shape=[jax.ShapeDtypeStruct((N_SHARDS, blk, D_MODEL), F32),
                   jax.ShapeDtypeStruct((N_SHARDS, blk, D_MODEL), BF16)],
        scratch_shapes=[pltpu.VMEM((N_SHARDS, blk, D_MODEL), F32), pltpu.VMEM((N_SHARDS, blk, D_MODEL), F32),
                        pltpu.SemaphoreType.DMA((N_SHARDS,)), pltpu.SemaphoreType.DMA((N_SHARDS,))],
        compiler_params=_params(dimension_semantics=("arbitrary",)),
    )(y_pool, y_rec, dout_b)


def _pool_bwd(proj, dout_b, w_out_g, pw_g, pool_scale, exchanges):
    n = POOL_ROWS + POOL_HALO

    def body(u_ref, gate_ref, d_ref, wo_ref, pw_ref, sc_ref,
             dp_ref, dpw_ref, dsc_ref, dd_ref, ddw_ref):
        g = pl.program_id(0)
        dpw_ref[...] = jnp.zeros_like(dpw_ref)
        dsc_ref[...] = jnp.zeros_like(dsc_ref)

        def first(i, _):
            r = pl.ds(pl.multiple_of(i * POOL_ROWS, POOL_ROWS), POOL_ROWS)
            d, inv_count = _pool_diff(u_ref, i, g)
            db = d.astype(BF16)
            mixed = _dot(db, pw_ref[...])
            gate = gate_ref[r, :]
            sg = _sig(gate)
            silu = gate * sg
            dy = _dot_nt(d_ref[r, :], wo_ref[...])
            sc = sc_ref[...]
            dmixed = dy * silu * sc
            dp_ref[1, r, :] = (dy * mixed * sc * (sg * (1.0 + gate * (1.0 - sg)))).astype(BF16)
            dsc_ref[...] += jnp.sum(dy * silu * mixed, axis=0, keepdims=True)
            dmb = dmixed.astype(BF16)
            dpw_ref[...] += _dot_tn(db, dmb)
            dd = _dot_nt(dmb, pw_ref[...])
            dd_ref[r, :] = dd
            ddw_ref[r, :] = dd * inv_count
            return 0
        lax.fori_loop(0, SEQ // POOL_ROWS, first, 0)

        def second(i, _):
            r0 = i * POOL_ROWS
            r = pl.ds(pl.multiple_of(r0, POOL_ROWS), POOL_ROWS)
            last = i == SEQ // POOL_ROWS - 1
            after = ddw_ref[pl.ds(pl.multiple_of(jnp.minimum(r0 + POOL_ROWS, SEQ - POOL_HALO), 8), POOL_HALO), :]
            after = jnp.where(last, 0.0, after)
            ext = jnp.concatenate([ddw_ref[r, :], after], axis=0)
            s = _window_sums(ext, g, lambda k: n - (1 << k))[:POOL_ROWS, :]
            dp_ref[0, r, :] = (s - dd_ref[r, :]).astype(BF16)
            return 0
        lax.fori_loop(0, SEQ // POOL_ROWS, second, 0)

    return _call(
        body, name="pool_bwd", grid=(N_GROUPS,),
        args=(proj, proj, dout_b, w_out_g, pw_g, pool_scale),
        in_specs=[pl.BlockSpec((SEQ, PG), lambda g: (0, g)),
                  pl.BlockSpec((SEQ, PG), lambda g: (0, N_GROUPS + g)),
                  pl.BlockSpec((SEQ, D_MODEL), lambda g: (0, 0)),
                  pl.BlockSpec((PG, D_MODEL), lambda g: (g, 0)),
                  pl.BlockSpec((None, PG, PG), lambda g: (g, 0, 0)),
                  pl.BlockSpec((1, PG), lambda g: (0, g))],
        out_specs=[pl.BlockSpec((2, SEQ, PG), lambda g: (0, 0, g)),
                   pl.BlockSpec((None, PG, PG), lambda g: (g, 0, 0)),
                   pl.BlockSpec((1, PG), lambda g: (0, g))],
        out_shape=[jax.ShapeDtypeStruct((2, SEQ, D_MODEL), BF16),
                   jax.ShapeDtypeStruct((N_GROUPS, PG, PG), F32),
                   jax.ShapeDtypeStruct((1, D_MODEL), F32)],
        scratch_shapes=[pltpu.VMEM((SEQ, PG), F32), pltpu.VMEM((SEQ, PG), F32)],
        exchanges=exchanges)


HALF_HEADS = N_HEADS // 2
HALF_COLS = HALF_HEADS * HEAD


def _rec_bwd(proj, o_raw, st_prev, dout_b, w_out_g, lb_logits, rec_g, consts, h0, name, exchanges):
    def body(q_ref, f_ref, i_ref, rg_ref, o_ref, stp_ref, d_ref, wo_ref, lb_ref, g_ref,
             w_ref, lowt_ref, sym_ref, sign_ref, tri_ref,
             dr_ref, part_ref, dst_ref):
        @pl.when(pl.program_id(1) == 0)
        def _():
            dst_ref[...] = jnp.zeros_like(dst_ref)
            part_ref[...] = jnp.zeros_like(part_ref)
        tril = (lax.broadcasted_iota(jnp.int32, (CHUNK, CHUNK), 0)
                > lax.broadcasted_iota(jnp.int32, (CHUNK, CHUNK), 1))
        lb = _lower_bound(lb_ref)
        grec = g_ref[...]
        dst = dst_ref[...]
        acc_grec = jnp.zeros((1, HEAD), F32)
        acc_lb = jnp.zeros((1, HEAD), F32)
        rows = lambda c: pl.ds(c * CHUNK, CHUNK)
        for c0 in reversed(range(0, REC_CHUNKS, REC_GROUP)):
            group = list(reversed(range(c0, c0 + REC_GROUP)))
            dys = [_dot_nt(d_ref[rows(c), :], wo_ref[...]) for c in group]
            dos = []
            for c, dy in zip(group, dys):
                rg = rg_ref[rows(c), :]
                o = o_ref[rows(c), :]
                sg = _sig(rg)
                silu = rg * sg
                inv = lax.rsqrt(jnp.mean(o * o, axis=-1, keepdims=True) + EPS)
                recn = o * inv
                dr_ref[3, rows(c), :] = (dy * recn * grec * (sg * (1.0 + rg * (1.0 - sg)))).astype(BF16)
                acc_grec = acc_grec + jnp.sum(dy * silu * recn, axis=0, keepdims=True)
                drecn = dy * silu * grec
                dos.append(inv * drecn - o * (inv * inv * inv) * jnp.mean(drecn * o, axis=-1, keepdims=True))
            gated = [_gates(q_ref[rows(c), :], f_ref[rows(c), :], lb) for c in group]
            g2s = [_dot3(w_ref[...], g) * LOG2E for (_, _, _, _, g) in gated]
            levels = [_level_factors(g2, qs, k, sign_ref) for g2, (qs, _, _, k, _) in zip(g2s, gated)]
            a_ts = []
            for lev in levels:
                a_t = jnp.zeros((CHUNK, CHUNK), F32)
                for l, (xl, _, _, _) in enumerate(lev):
                    a_t = a_t + _dot_nt(xl, xl) * lowt_ref[l]
                a_ts.append(a_t.astype(BF16))
            dobs = [do.astype(BF16) for do in dos]
            vbs = [i_ref[rows(c), :].astype(BF16) for c in group]
            d_syms = [jnp.where(tril, _dot_nt(dob, vb), _dot_nt(vb, dob)) for dob, vb in zip(dobs, vbs)]
            dqs_is, dk_is = [], []
            for lev, d_sym in zip(levels, d_syms):
                dqs_i = jnp.zeros((CHUNK, HEAD), F32)
                both_i = jnp.zeros((CHUNK, HEAD), F32)
                for l, (xl, xlo, e, up) in enumerate(lev):
                    z = d_sym * sym_ref[l]
                    tmp = _dot(z.astype(BF16), jnp.concatenate([xl, xlo], axis=-1))
                    tmp = (tmp[:, :HEAD] + tmp[:, HEAD:]) * e
                    dqs_i = dqs_i + jnp.where(up, tmp, 0.0)
                    both_i = both_i + tmp
                dqs_is.append(dqs_i)
                dk_is.append(both_i - dqs_i)
            e_gs = [jnp.exp2(g2) for g2 in g2s]
            e_revs = [jnp.exp2(g2[CHUNK - 1:CHUNK, :] - g2) for g2 in g2s]
            e_lasts = [jnp.exp2(g2[CHUNK - 1:CHUNK, :]) for g2 in g2s]
            q_gs = [qs * e_g for (qs, _, _, _, _), e_g in zip(gated, e_gs)]
            kdecs = [k * e_rev for (_, _, _, k, _), e_rev in zip(gated, e_revs)]
            dv12 = [_dot(a_t, dob) + jnp.sum(qs * k, axis=-1, keepdims=True) * do
                    for a_t, dob, do, (qs, _, _, k, _) in zip(a_ts, dobs, dos, gated)]
            dq_gs = [_dot(dob, stp_ref[c].astype(BF16)) for c, dob in zip(group, dobs)]
            steps = [_dot_tn(dob, q_g.astype(BF16)) for dob, q_g in zip(dobs, q_gs)]
            dsts = []
            for e_last, step in zip(e_lasts, steps):
                dsts.append(dst)
                dst = dst * e_last + step
            dstbs = [d.astype(BF16) for d in dsts]
            dv3 = [_dot_nt(kdec.astype(BF16), dstb) for kdec, dstb in zip(kdecs, dstbs)]
            dkdecs = [_dot(vb, dstb) for vb, dstb in zip(vbs, dstbs)]
            dbig_gs, dg_lasts, dqss, dks = [], [], [], []
            for i, c in enumerate(group):
                qs, _, _, k, _ = gated[i]
                de_last = jnp.sum(stp_ref[c] * dsts[i], axis=0, keepdims=True)
                ddiag = jnp.sum(dos[i] * i_ref[rows(c), :], axis=-1, keepdims=True)
                dqss.append(dqs_is[i] + ddiag * k + dq_gs[i] * e_gs[i])
                dks.append(dk_is[i] + ddiag * qs + dkdecs[i] * e_revs[i])
                dg_rev = dkdecs[i] * kdecs[i]
                dg_lasts.append(jnp.sum(dg_rev, axis=0, keepdims=True) + de_last * e_lasts[i])
                dbig_gs.append(qs * dqs_is[i] - k * dk_is[i] + dq_gs[i] * q_gs[i] - dg_rev)
            dgs = [_dot3(tri_ref[...], dbig_g) + dg_last for dbig_g, dg_last in zip(dbig_gs, dg_lasts)]
            for i, c in enumerate(group):
                _, sf, f, _, _ = gated[i]
                q = q_ref[rows(c), :]
                df = dgs[i] / f - dks[i]
                dr_ref[1, rows(c), :] = (df * (1.0 - lb) * sf * (1.0 - sf)).astype(BF16)
                acc_lb = acc_lb + jnp.sum(df * (1.0 - sf), axis=0, keepdims=True)
                sq = _sig(q)
                dr_ref[0, rows(c), :] = (dqss[i] * (sq * (1.0 + q * (1.0 - sq)))).astype(BF16)
                dr_ref[2, rows(c), :] = (dv12[i] + dv3[i]).astype(BF16)
        dst_ref[...] = dst
        part_ref[0:1, :] += acc_grec
        part_ref[1:2, :] += acc_lb

    rev = lambda b: N_REC_BLK - 1 - b
    sec = lambda n: pl.BlockSpec((REC_ROWS, HEAD), lambda h, b: (rev(b), n * SEC_BLK + h0 + h))
    col_in = pl.BlockSpec((REC_ROWS, HEAD), lambda h, b: (rev(b), h0 + h))
    vec_in = lambda rows: pl.BlockSpec((rows, HEAD), lambda h, b: (0, h0 + h))
    full = lambda a: pl.BlockSpec(a.shape, lambda h, b: (0,) * a.ndim)
    return _call(
        body, name=name, grid=(HALF_HEADS, N_REC_BLK),
        args=(proj, proj, proj, proj, o_raw, st_prev, dout_b, w_out_g, lb_logits, rec_g,
              consts["tri"], consts["low_t"], consts["sym"], consts["sign"], consts["tri_t"]),
        in_specs=[sec(2), sec(3), sec(4), sec(5), col_in,
                  pl.BlockSpec((None, REC_CHUNKS, HEAD, HEAD), lambda h, b: (h0 + h, rev(b), 0, 0)),
                  pl.BlockSpec((REC_ROWS, D_MODEL), lambda h, b: (rev(b), 0)),
                  pl.BlockSpec((HEAD, D_MODEL), lambda h, b: (SEC_BLK + h0 + h, 0)),
                  vec_in(2), vec_in(1)] + [full(consts[n]) for n in ("tri", "low_t", "sym", "sign", "tri_t")],
        out_specs=[pl.BlockSpec((4, REC_ROWS, HEAD), lambda h, b: (0, rev(b), h)),
                   pl.BlockSpec((8, HEAD), lambda h, b: (0, h))],
        out_shape=[jax.ShapeDtypeStruct((4, SEQ, HALF_COLS), BF16),
                   jax.ShapeDtypeStruct((8, HALF_COLS), F32)],
        scratch_shapes=[pltpu.VMEM((HEAD, HEAD), F32)],
        exchanges=exchanges)


def _w_in_block(w_ref, j):
    per_shard = W_IN_SHARD // COL_BLK
    return w_ref[j // per_shard, :, (j % per_shard) * COL_BLK:(j % per_shard + 1) * COL_BLK]


def _grad_x(dproj, w_in_g, x, g1, dout, exchanges):
    rows = 256
    n_blk = len(dproj)

    def body(*refs):
        dp_refs = refs[:n_blk]
        w_ref, x_ref, g_ref, dout_ref, dx_ref, part_ref = refs[n_blk:]

        @pl.when(pl.program_id(0) == 0)
        def _():
            part_ref[...] = jnp.zeros_like(part_ref)
        dh = jnp.zeros((rows, D_MODEL), F32)
        for j in range(n_blk):
            dh = dh + _dot_nt(dp_refs[j][...], _w_in_block(w_ref, j))
        xv = x_ref[...]
        inv = lax.rsqrt(jnp.mean(xv * xv, axis=-1, keepdims=True) + EPS)
        a = dh * g_ref[...]
        dx_ref[...] = (dout_ref[...] + inv * a
                       - xv * (inv * inv * inv) * jnp.mean(a * xv, axis=-1, keepdims=True))
        part_ref[0:1, :] += jnp.sum(dh * xv * inv, axis=0, keepdims=True)

    row = lambda: pl.BlockSpec((rows, D_MODEL), lambda i: (i, 0))
    dp_spec = lambda sec, cb: pl.BlockSpec((None, rows, COL_BLK), lambda i: (sec, i, cb))
    return _call(
        body, name="grad_x", grid=(SEQ // rows,),
        args=tuple(a for a, _, _ in dproj) + (w_in_g, x, g1, dout),
        in_specs=[dp_spec(sec, cb) for _, sec, cb in dproj]
                 + [pl.BlockSpec((N_SHARDS, D_MODEL, W_IN_SHARD), lambda i: (0, 0, 0)),
                    row(), pl.BlockSpec((1, D_MODEL), lambda i: (0, 0)), row()],
        out_specs=[row(), pl.BlockSpec((8, D_MODEL), lambda i: (0, 0))],
        out_shape=[jax.ShapeDtypeStruct((SEQ, D_MODEL), F32),
                   jax.ShapeDtypeStruct((8, D_MODEL), F32)],
        exchanges=exchanges)


def _grad_w_in(h, dp, blocks, name):
    n_blk = len(blocks)
    half = D_MODEL // 2
    pick = lambda vals: (lambda j: functools.reduce(lambda acc, iv: jnp.where(j == iv[0], iv[1], acc),
                                                     list(enumerate(vals))[1:], vals[0]))
    sec_of = pick([sec for sec, _ in blocks])
    cb_of = pick([cb for _, cb in blocks])

    def body(h_ref, dp_ref, p32_ref, p16_ref, send_ref, recv_ref, send_sems, recv_sems):
        j = pl.program_id(0)
        x, y, c = _place()
        cols = lambda cc: pl.ds(pl.multiple_of(cc * half, half), half)

        def copy(i):
            return _remote(send_ref.at[i], recv_ref.at[i], send_sems, recv_sems, i, (x, y, 1 - c))

        for i in range(n_blk):
            @pl.when(j == i)
            def _(i=i):
                send_ref[i] = _dot_tn(h_ref[:, cols(1 - c)], dp_ref[...])
                copy(i).start()
                p32_ref[i] = _dot_tn(h_ref[:, cols(c)], dp_ref[...])

        @pl.when(j == n_blk - 1)
        def _():
            for i in range(n_blk):
                copy(i).wait_recv()
                tot = p32_ref[i] + recv_ref[i]
                p32_ref[i] = tot
                p16_ref[i] = tot.astype(BF16)
            for i in range(n_blk):
                copy(i).wait_send()

    whole = pl.BlockSpec((n_blk, half, COL_BLK), lambda j: (0, 0, 0))
    return pl.pallas_call(
        body, name=name, grid=(n_blk,),
        in_specs=[pl.BlockSpec((SEQ, D_MODEL), lambda j: (0, 0)),
                  pl.BlockSpec((None, SEQ, COL_BLK), lambda j: (sec_of(j), 0, cb_of(j)))],
        out_specs=[whole, whole],
        out_shape=[jax.ShapeDtypeStruct((n_blk, half, COL_BLK), F32),
                   jax.ShapeDtypeStruct((n_blk, half, COL_BLK), BF16)],
        scratch_shapes=[pltpu.VMEM((n_blk, half, COL_BLK), F32), pltpu.VMEM((n_blk, half, COL_BLK), F32),
                        pltpu.SemaphoreType.DMA((n_blk,)), pltpu.SemaphoreType.DMA((n_blk,))],
        compiler_params=_params(dimension_semantics=("arbitrary",)),
    )(h, dp)


def _add_units(grad, recv, place, tile, name):
    n, rows, cols = grad.shape
    per_half = rows // 2 // tile

    def body(place_ref, g_ref, r_ref, o32_ref, o16_ref):
        v = g_ref[...] + r_ref[...]
        o32_ref[...] = v
        o16_ref[...] = v.astype(BF16)

    blk = lambda f: pl.BlockSpec((None, tile, cols), f)
    out = lambda s, i, p: (s, i, 0)
    return pl.pallas_call(
        body, name=name,
        grid_spec=pltpu.PrefetchScalarGridSpec(
            num_scalar_prefetch=1, grid=(n, per_half),
            in_specs=[blk(lambda s, i, p: (s, p[0] * per_half + i, 0)), blk(out)],
            out_specs=[blk(out), blk(out)]),
        out_shape=[jax.ShapeDtypeStruct(recv.shape, F32), jax.ShapeDtypeStruct(recv.shape, BF16)],
        compiler_params=_params(dimension_semantics=("arbitrary", "arbitrary")),
    )(place, grad, recv)


def _sum_units(part32, recv16, place, owners, tile, name):
    n, half, cols = part32.shape
    per_half = half // tile
    own = jnp.asarray(owners, jnp.int32) == place[1]
    order = jnp.argsort(jnp.where(own, 0, 1), stable=True).astype(jnp.int32)
    sched = jnp.concatenate([place[:1], jnp.sum(own).reshape(1).astype(jnp.int32), order])

    def block(k, i, p):
        live = k < p[1]
        unit = p[2 + jnp.minimum(k, jnp.maximum(p[1] - 1, 0))]
        return unit, jnp.where(live, i, per_half - 1)

    def body(sched_ref, p_ref, r_ref, o_ref):
        @pl.when(pl.program_id(0) < sched_ref[1])
        def _():
            acc = p_ref[...]
            for j in range(len(CHIP_FLIPS)):
                acc = acc + r_ref[j].astype(F32)
            o_ref[...] = acc

    return pl.pallas_call(
        body, name=name,
        grid_spec=pltpu.PrefetchScalarGridSpec(
            num_scalar_prefetch=1, grid=(n, per_half),
            in_specs=[pl.BlockSpec((None, tile, cols), lambda k, i, p: (*block(k, i, p), 0)),
                      pl.BlockSpec((None, len(CHIP_FLIPS), tile, cols),
                                   lambda k, i, p: (block(k, i, p)[0], 0, block(k, i, p)[1], 0))],
            out_specs=pl.BlockSpec((None, tile, cols),
                                   lambda k, i, p: (block(k, i, p)[0], p[0] * per_half + block(k, i, p)[1], 0))),
        out_shape=jax.ShapeDtypeStruct((n, 2 * half, cols), F32),
        compiler_params=_params(dimension_semantics=("arbitrary", "arbitrary")),
    )(sched, part32, recv16)


def _adamw_math(w, g, m, v):
    m = ADAM_B1 * m + (1.0 - ADAM_B1) * g
    v = ADAM_B2 * v + (1.0 - ADAM_B2) * (g * g)
    m_hat = m / (1.0 - ADAM_B1 ** ADAM_STEP)
    v_hat = v / (1.0 - ADAM_B2 ** ADAM_STEP)
    delta = -ADAM_LR * (m_hat / (jnp.sqrt(v_hat) + ADAM_EPS) + ADAM_WD * w)
    return delta, m, v


def _adamw_units(w, m, v, grads, pick, name):
    rows, cols = w.shape
    bc = grads[0].shape[-1]
    tile = min(rows, 256)
    n_g = len(grads)

    def body(pick_ref, w_ref, m_ref, v_ref, *refs):
        g_refs, (g_out, d_ref, nm_ref, nv_ref) = refs[:n_g], refs[n_g:]
        p = pl.program_id(0)
        for a in range(n_g):
            @pl.when(pick_ref[0, p] == a)
            def _(a=a):
                g = g_refs[a][...]
                g_out[...] = g
                d_ref[...], nm_ref[...], nv_ref[...] = _adamw_math(w_ref[...], g, m_ref[...], v_ref[...])

    blk = pl.BlockSpec((tile, bc), lambda p, i, pick: (i, p))

    def g_spec(a):
        return pl.BlockSpec((None, tile, bc),
                            lambda p, i, pick: (jnp.where(pick[0, p] == a, pick[1, p], 0),
                                                jnp.where(pick[0, p] == a, i, 0), 0))

    return pl.pallas_call(
        body, name=name,
        grid_spec=pltpu.PrefetchScalarGridSpec(
            num_scalar_prefetch=1, grid=(cols // bc, rows // tile),
            in_specs=[blk] * 3 + [g_spec(a) for a in range(n_g)],
            out_specs=[blk] * 4),
        out_shape=[jax.ShapeDtypeStruct(w.shape, F32)] * 4,
        compiler_params=_params(dimension_semantics=("arbitrary", "arbitrary")),
    )(pick, w, m, v, *grads)


ROW_NORM1, ROW_SCALE, ROW_LB, ROW_REC, ROW_FINAL, ROW_LOSS = 0, 1, 2, 4, 5, 6


def _small_update(parts, gathered, w, m, v):
    def body(own_ref, p_ref, w_ref, m_ref, v_ref, loss_ref, g_ref, d_ref, nm_ref, nv_ref):
        x, y, c = _place()
        me = 4 * x + 2 * y + c
        slot = lambda d: jnp.where(me == d, own_ref[...], p_ref[d])
        tot = slot(0)
        for d in range(1, 8):
            tot = tot + slot(d)
        wv = w_ref[...]
        l0 = wv[ROW_LB:ROW_LB + 1, :]
        l1 = wv[ROW_LB + 1:ROW_LB + 2, :]
        mx = jnp.maximum(l0, l1)
        e0 = jnp.exp(l0 - mx)
        e1 = jnp.exp(l1 - mx)
        lb = e0 / (e0 + e1)
        dl0 = tot[ROW_LB:ROW_LB + 1, :] * lb * (1.0 - lb)
        row = lax.broadcasted_iota(jnp.int32, tot.shape, 0)
        g = jnp.where(row == ROW_LB, dl0, jnp.where(row == ROW_LB + 1, -dl0, tot))
        g = jnp.where(row >= ROW_LOSS, 0.0, g)
        g_ref[...] = g
        d_ref[...], nm_ref[...], nv_ref[...] = _adamw_math(wv, g, m_ref[...], v_ref[...])
        loss_ref[...] = (0.5 / D_MODEL) * jnp.sum(tot[ROW_LOSS:ROW_LOSS + 1, :], axis=-1, keepdims=True)

    return pl.pallas_call(
        body, name="small_update",
        out_shape=[jax.ShapeDtypeStruct((1, 1), F32)] + [jax.ShapeDtypeStruct(w.shape, F32)] * 4,
        compiler_params=_params(),
    )(parts, gathered, w, m, v)


SHARD_OWNERS = tuple(range(N_SHARDS))
BLOCKS_POOL = (0, 1, 2, 3)
BLOCKS_A = (4, 6, 8, 10)
BLOCKS_B = (5, 7, 9, 11)
BLOCK_GROUPS = (BLOCKS_POOL, BLOCKS_A, BLOCKS_B)


def _block_owners(blocks):
    return tuple(j // (W_IN_SHARD // COL_BLK) for j in blocks)


def _small_rows(norm1, scale, lb, rec, final):
    pad = jnp.zeros((2, D_MODEL), F32)
    return jnp.concatenate([norm1, scale, lb, rec, final.reshape(1, D_MODEL), pad], axis=0)


def kernel(x, norm1_g, w_in, pool_w, pool_scale, lb_logits, rec_norm_g, w_out, final_norm_g, loss_target, m_norm1_g, m_w_in, m_pool_w, m_pool_scale, m_lb_logits, m_rec_norm_g, m_w_out, m_final_norm_g, v_norm1_g, v_w_in, v_pool_w, v_pool_scale, v_lb_logits, v_rec_norm_g, v_w_out, v_final_norm_g):
    xi, yi, ci = _place()
    chip = 2 * xi + yi
    place = jnp.stack([ci, chip]).astype(jnp.int32)
    pw_rows = N_GROUPS * PW_SHARD
    flat_pw = lambda a: a.reshape(pw_rows, PG)
    x2, target, gf = x[0], loss_target[0], final_norm_g.reshape(1, D_MODEL)
    consts = {n: jnp.asarray(a, BF16 if n.startswith("tri") else F32) for n, a in _chunk_constants().items()}

    proj, h, w_in_g = _in_proj(x2, norm1_g, _cast_own(w_in[0], place, "cast_w_in"), place)
    (y_rec, o_raw, st_prev), ((w_out_g, pw_g),) = _rec_fwd(
        proj, lb_logits, rec_norm_g, consts,
        [_ex_gather([_cast_own(w_out[0], place, "cast_w_out"), _cast_own(flat_pw(pool_w), place, "cast_pool_w")])])
    w_out_g = w_out_g.reshape(2 * D_MODEL, D_MODEL)
    pw_full = pw_g.reshape(N_SHARDS, N_GROUPS, PW_SHARD, PG).transpose(1, 0, 2, 3).reshape(N_GROUPS, PG, PG)
    y_pool = _pool_fwd(proj, pw_full, pool_scale)
    dout, dout_b, part_out = _out_proj_loss(y_pool, y_rec, w_out_g, x2, target, gf)

    p_out32, p_out16 = _grad_w_out(y_pool, y_rec, dout_b)
    (dpool, gpw, dscale), ((rb_out,),) = _pool_bwd(proj, dout_b, w_out_g, pw_full, pool_scale,
                                                   [_ex_send([p_out16], [SHARD_OWNERS])])
    g_out = _sum_units(p_out32, rb_out, place, SHARD_OWNERS, 256, "sum_w_out")
    gpw = gpw.reshape(N_GROUPS, N_SHARDS, PW_SHARD, PG).transpose(1, 0, 2, 3).reshape(N_SHARDS, pw_rows, PG)
    p_inp32, p_inp16 = _grad_w_in(h, dpool, [(0, 0), (0, 1), (1, 0), (1, 1)], "grad_w_in_pool")

    pool_owners, a_owners, b_owners = (_block_owners(b) for b in BLOCK_GROUPS)
    rec_args = (proj, o_raw, st_prev, dout_b, w_out_g, lb_logits, rec_norm_g, consts)
    (drec_a, part_a), ((rb_inp,), (ra_pw,)) = _rec_bwd(
        *rec_args, 0, "rec_bwd_a", [_ex_send([p_inp16], [pool_owners], units=[(0, 1)]), _ex_swap([gpw])])
    p_pw32, p_pw16 = _add_units(gpw, ra_pw, place, 128, "add_pool_w")
    rec_blocks = [(n, 0) for n in range(4)]
    p_ina32, p_ina16 = _grad_w_in(h, drec_a, rec_blocks, "grad_w_in_a")

    (drec_b, part_b), ((rb_inp, rb_ina, rb_pw),) = _rec_bwd(
        *rec_args, HALF_HEADS, "rec_bwd_b",
        [_ex_send([p_inp16, p_ina16, p_pw16], [pool_owners, a_owners, SHARD_OWNERS],
                  units=[(2, 3), (0, 1, 2, 3), (0, 1, 2, 3)], landed=[rb_inp, None, None])])
    g_inp = _sum_units(p_inp32, rb_inp, place, pool_owners, 256, "sum_w_in_pool")
    g_ina = _sum_units(p_ina32, rb_ina, place, a_owners, 256, "sum_w_in_a")
    g_pw = _sum_units(p_pw32, rb_pw, place, SHARD_OWNERS, 128, "sum_pool_w")
    p_inb32, p_inb16 = _grad_w_in(h, drec_b, rec_blocks, "grad_w_in_b")

    dproj = ([(dpool, 0, 0), (dpool, 0, 1), (dpool, 1, 0), (dpool, 1, 1)]
             + [(d, n, 0) for n in range(4) for d in (drec_a, drec_b)])
    (dx, part_x), ((rb_inb,),) = _grad_x(dproj, w_in_g, x2, norm1_g, dout, [_ex_send([p_inb16], [b_owners])])
    g_inb = _sum_units(p_inb32, rb_inb, place, b_owners, 256, "sum_w_in_b")
    zero = jnp.zeros((1, D_MODEL), F32)
    part_rec = jnp.concatenate([part_a, part_b], axis=1)
    parts = jnp.concatenate([part_x[0:1], dscale, part_rec[1:2], zero, part_rec[0:1], part_out[0:1],
                             part_out[1:2], zero], axis=0)
    _, ((g_out, g_pw, g_inp, g_ina, g_inb), (gathered,)) = _call(
        None, name="join_halves",
        exchanges=[_ex_join([g_out, g_pw, g_inp, g_ina, g_inb],
                            [SHARD_OWNERS, SHARD_OWNERS, pool_owners, a_owners, b_owners]),
                   _ex_gather_small(parts)])

    group_of = np.zeros((D_PROJ // COL_BLK,), np.int32)
    index_of = np.zeros((D_PROJ // COL_BLK,), np.int32)
    for gi, blocks in enumerate(BLOCK_GROUPS):
        for i, j in enumerate(blocks):
            group_of[j], index_of[j] = gi, i
    per_shard = W_IN_SHARD // COL_BLK
    pick_in = jnp.stack([lax.dynamic_slice(jnp.asarray(group_of), (per_shard * chip,), (per_shard,)),
                         lax.dynamic_slice(jnp.asarray(index_of), (per_shard * chip,), (per_shard,))])
    pick_own = jnp.stack([jnp.zeros((1,), jnp.int32), chip.reshape(1).astype(jnp.int32)])
    big = [_adamw_units(w_in[0], m_w_in[0], v_w_in[0], [g_inp, g_ina, g_inb], pick_in, "adamw_w_in"),
           _adamw_units(w_out[0], m_w_out[0], v_w_out[0], [g_out], pick_own, "adamw_w_out"),
           _adamw_units(flat_pw(pool_w), flat_pw(m_pool_w), flat_pw(v_pool_w), [g_pw], pick_own, "adamw_pool_w")]

    small_w = _small_rows(norm1_g, pool_scale, lb_logits, rec_norm_g, final_norm_g)
    small_m = _small_rows(m_norm1_g, m_pool_scale, m_lb_logits, m_rec_norm_g, m_final_norm_g)
    small_v = _small_rows(v_norm1_g, v_pool_scale, v_lb_logits, v_rec_norm_g, v_final_norm_g)
    loss, *small = _small_update(parts, gathered, small_w, small_m, small_v)

    def leaves(k):
        s = small[k]
        return (s[ROW_NORM1:ROW_NORM1 + 1], big[0][k][None], big[2][k].reshape(pool_w.shape),
                s[ROW_SCALE:ROW_SCALE + 1], s[ROW_LB:ROW_LB + 2], s[ROW_REC:ROW_REC + 1],
                big[1][k][None], s[ROW_FINAL])

    return (loss.reshape(()), dx[None], *leaves(0), *leaves(1), *leaves(2), *leaves(3))
```

```python
import functools

import numpy as np
import jax
import jax.numpy as jnp
from jax import lax
from jax.experimental import pallas as pl
from jax.experimental.pallas import tpu as pltpu

F32 = jnp.float32
BF16 = jnp.bfloat16

SEQ = 2048
D_MODEL = 1024
D_PROJ = 6144
N_SEC = 6
N_GROUPS = 4
PG = 256
N_HEADS = 8
HEAD = 128
CHUNK = 64
N_LEVELS = 6
N_SHARDS = 4
W_IN_SHARD = D_PROJ // N_SHARDS
W_OUT_SHARD = 2048 // N_SHARDS
PW_SHARD = PG // N_SHARDS
COL_BLK = 512
EPS = 1e-6

ADAM_LR = 0.001
ADAM_B1 = 0.9
ADAM_B2 = 0.999
ADAM_EPS = 1e-08
ADAM_WD = 0.01
ADAM_STEP = 10

V7X_VMEM_LIMIT = 56 * 1024 * 1024
MESH = pl.DeviceIdType.MESH


def _params(**kw):
    return pltpu.CompilerParams(vmem_limit_bytes=V7X_VMEM_LIMIT, **kw)


def _sig(x):
    return 1.0 / (1.0 + jnp.exp(-x))


def _dot(a, b):
    return jnp.dot(a, b, preferred_element_type=F32)


def _dot_nt(a, b):
    return lax.dot_general(a, b, (((1,), (1,)), ((), ())), preferred_element_type=F32)


def _dot_tn(a, b):
    return lax.dot_general(a, b, (((0,), (0,)), ((), ())), preferred_element_type=F32)


def _split3(a):
    p1 = a.astype(BF16)
    r1 = a - p1.astype(F32)
    p2 = r1.astype(BF16)
    p3 = (r1 - p2.astype(F32)).astype(BF16)
    return jnp.concatenate([p1, p2, p3], axis=-1)


def _dot3(w01, a):
    n = a.shape[-1]
    r = _dot(w01, _split3(a))
    return r[:, :n] + r[:, n:2 * n] + r[:, 2 * n:]


def _chunk_constants():
    j = np.arange(CHUNK)
    tt, ss = np.meshgrid(j, j, indexing="ij")
    x = tt ^ ss
    hb = np.full((CHUNK, CHUNK), -1, np.int32)
    for l in range(N_LEVELS):
        hb[x >= (1 << l)] = l
    sym = np.stack([(hb == l) for l in range(N_LEVELS)]).astype(np.float32)
    low = sym * (tt > ss)
    sign = np.stack([np.where((j >> l) & 1, 1.0, -1.0) for l in range(N_LEVELS)]).astype(np.float32)
    sign = np.ascontiguousarray(np.broadcast_to(sign[:, :, None], (N_LEVELS, CHUNK, HEAD)))
    tri = (ss <= tt).astype(np.float32)
    return dict(tri=tri, tri_t=np.ascontiguousarray(tri.T), low=low,
                low_t=np.ascontiguousarray(low.transpose(0, 2, 1)), sym=sym, sign=sign)


def _in_proj(x, g1, w_slots, place):
    n_col = D_PROJ // COL_BLK
    per_shard = W_IN_SHARD // COL_BLK
    rows = 1024
    half_rows = D_MODEL // 2
    quarter_rows = D_MODEL // 4
    FLIP_X, FLIP_Y, FLIP_XY = 2, 1, 3

    def shard_at(m, chip):
        return chip ^ jnp.where(m == 0, 0, jnp.where(m == 1, FLIP_X, jnp.where(m == 2, FLIP_Y, FLIP_XY)))

    def body(place_ref, x_ref, g_ref, w_in_ref, proj_ref, h_ref, w_ref, wbuf, load_sems, send_sems, recv_sems):
        t = pl.program_id(0)
        x_, y_, c = _place()
        chip = 2 * x_ + y_
        me, other_core = (x_, y_, c), (x_, y_, 1 - c)
        x_nbr, y_nbr = (1 - x_, y_, c), (x_, 1 - y_, c)
        piece = lambda shard, half: _half_rows(w_ref.at[shard], half)

        def quarter(shard, q):
            r = pl.ds(pl.multiple_of(c * half_rows + q * quarter_rows, quarter_rows), quarter_rows)
            return w_ref.at[shard, r, :]

        def copy(k, ref, to):
            return _remote(ref, ref, send_sems, recv_sems, k, to)

        direct = lambda: [copy(0, piece(chip, c), x_nbr), copy(1, piece(chip, c), y_nbr)]
        relays = lambda: [copy(2, quarter(chip ^ FLIP_X, 0), y_nbr), copy(3, quarter(chip ^ FLIP_Y, 1), x_nbr)]
        passed_on = lambda m, half, to: copy(3 + m, piece(shard_at(m, chip), half), to)

        def load(shard, p):
            return pltpu.make_async_copy(w_ref.at[shard, :, pl.ds(p * COL_BLK, COL_BLK)], wbuf.at[p],
                                         load_sems.at[p])

        @pl.when(t == 0)
        def _():
            for cp in direct():
                cp.start()

            def norm(i, _):
                r = pl.ds(pl.multiple_of(i * rows, rows), rows)
                xv = x_ref[r, :]
                inv = lax.rsqrt(jnp.mean(xv * xv, axis=-1, keepdims=True) + EPS)
                h_ref[r, :] = (xv * inv * g_ref[...]).astype(BF16)
                return 0
            lax.fori_loop(0, SEQ // rows, norm, 0)

        @pl.when(t == per_shard)
        def _():
            copy(0, piece(chip ^ FLIP_X, c), me).wait_recv()
            copy(1, piece(chip ^ FLIP_Y, c), me).wait_recv()
            for cp in relays():
                cp.start()
            passed_on(1, c, other_core).start()
            passed_on(2, c, other_core).start()
            passed_on(1, 1 - c, me).wait_recv()

        @pl.when(t == 2 * per_shard)
        def _():
            passed_on(2, 1 - c, me).wait_recv()

        @pl.when(t == 3 * per_shard)
        def _():
            copy(2, quarter(chip ^ FLIP_XY, 0), me).wait_recv()
            copy(3, quarter(chip ^ FLIP_XY, 1), me).wait_recv()
            passed_on(3, c, other_core).start()
            passed_on(3, 1 - c, me).wait_recv()

        for m in range(N_SHARDS):
            @pl.when(t == per_shard * m)
            def _(m=m):
                for p in range(per_shard):
                    load(shard_at(m, chip), p).start()

        for p in range(per_shard):
            @pl.when(t % per_shard == p)
            def _(p=p):
                load(0, p).wait()

                def mm(i, _):
                    r = pl.ds(pl.multiple_of(i * rows, rows), rows)
                    proj_ref[r, :] = _dot(h_ref[r, :], wbuf[p])
                    return 0
                lax.fori_loop(0, SEQ // rows, mm, 0)

        @pl.when(t == n_col - 1)
        def _():
            for cp in direct() + relays() + [passed_on(m, c, other_core) for m in (1, 2, 3)]:
                cp.wait_send()

    return pl.pallas_call(
        body, name="in_proj",
        grid_spec=pltpu.PrefetchScalarGridSpec(
            num_scalar_prefetch=1, grid=(n_col,),
            in_specs=[pl.BlockSpec((SEQ, D_MODEL), lambda t, p: (0, 0)),
                      pl.BlockSpec((1, D_MODEL), lambda t, p: (0, 0)),
                      pl.BlockSpec(memory_space=pl.ANY)],
            out_specs=[pl.BlockSpec((SEQ, COL_BLK),
                                    lambda t, p: (0, per_shard * shard_at(t // per_shard, p[1]) + t % per_shard)),
                       pl.BlockSpec((SEQ, D_MODEL), lambda t, p: (0, 0)),
                       pl.BlockSpec(memory_space=pl.ANY)],
            scratch_shapes=[pltpu.VMEM((per_shard, D_MODEL, COL_BLK), BF16),
                            pltpu.SemaphoreType.DMA((per_shard,)),
                            pltpu.SemaphoreType.DMA((7,)), pltpu.SemaphoreType.DMA((7,))]),
        out_shape=[jax.ShapeDtypeStruct((SEQ, D_PROJ), F32),
                   jax.ShapeDtypeStruct((SEQ, D_MODEL), BF16),
                   jax.ShapeDtypeStruct(w_slots.shape, BF16)],
        input_output_aliases={3: 2},
        compiler_params=_params(dimension_semantics=("arbitrary",)),
    )(place, x, g1, w_slots)


POOL_ROWS = 256
POOL_HALO = 16


def _window_sums(ext, g, shift_of):
    s = ext
    for k in range(N_GROUPS):
        s = jnp.where(k <= g, s + pltpu.roll(s, shift_of(k), 0), s)
    return s


def _pool_diff(u_ref, i, g):
    n = POOL_ROWS + POOL_HALO
    r0 = i * POOL_ROWS
    cur = u_ref[pl.ds(pl.multiple_of(r0, POOL_ROWS), POOL_ROWS), :]
    before = u_ref[pl.ds(pl.multiple_of(jnp.maximum(r0 - POOL_HALO, 0), 8), POOL_HALO), :]
    before = jnp.where(i > 0, before, 0.0)
    ext = jnp.concatenate([before, cur], axis=0)
    s = _window_sums(ext, g, lambda k: 1 << k)[POOL_HALO:, :]
    t = r0 + lax.broadcasted_iota(jnp.int32, (POOL_ROWS, 1), 0)
    width = (2 << g).astype(F32)
    inv_count = 1.0 / jnp.minimum((t + 1).astype(F32), width)
    return s * inv_count - cur, inv_count


def _pool_fwd(proj, pw_g, pool_scale):
    def body(u_ref, gate_ref, pw_ref, sc_ref, y_ref):
        g = pl.program_id(0)

        def step(i, _):
            r = pl.ds(pl.multiple_of(i * POOL_ROWS, POOL_ROWS), POOL_ROWS)
            d, _ = _pool_diff(u_ref, i, g)
            mixed = _dot(d.astype(BF16), pw_ref[...])
            gate = gate_ref[r, :]
            y_ref[r, :] = (mixed * sc_ref[...] * (gate * _sig(gate))).astype(BF16)
            return 0
        lax.fori_loop(0, SEQ // POOL_ROWS, step, 0)

    return pl.pallas_call(
        body, name="pool_fwd", grid=(N_GROUPS,),
        in_specs=[pl.BlockSpec((SEQ, PG), lambda g: (0, g)),
                  pl.BlockSpec((SEQ, PG), lambda g: (0, N_GROUPS + g)),
                  pl.BlockSpec((None, PG, PG), lambda g: (g, 0, 0)),
                  pl.BlockSpec((1, PG), lambda g: (0, g))],
        out_specs=pl.BlockSpec((SEQ, PG), lambda g: (0, g)),
        out_shape=jax.ShapeDtypeStruct((SEQ, D_MODEL), BF16),
        compiler_params=_params(dimension_semantics=("arbitrary",)),
    )(proj, proj, pw_g, pool_scale)


REC_ROWS = 1024
REC_CHUNKS = REC_ROWS // CHUNK
N_REC_BLK = SEQ // REC_ROWS
REC_GROUP = REC_CHUNKS
SEC_BLK = D_MODEL // HEAD


def _lower_bound(lb_ref):
    l0 = lb_ref[0:1, :]
    l1 = lb_ref[1:2, :]
    mx = jnp.maximum(l0, l1)
    e0 = jnp.exp(l0 - mx)
    e1 = jnp.exp(l1 - mx)
    return e0 / (e0 + e1)


def _gates(q, fl, lb):
    qs = q * _sig(q)
    sf = _sig(fl)
    f = lb + (1.0 - lb) * sf
    return qs, sf, f, 1.0 - f, jnp.log(f)


LOG2E = 1.4426950408889634


def _level_factors(g2, qs, k, sign_ref):
    t = lax.broadcasted_iota(jnp.int32, (CHUNK, HEAD), 0)
    row = lambda r, n: jnp.broadcast_to(g2[r:r + 1, :], (n, HEAD))
    out = []
    for l in range(N_LEVELS):
        m = 1 << l
        if l == 0:
            g_mid = jnp.where((t & 1) == 1, pltpu.roll(g2, 1, 0), g2)
        elif l == 1:
            low = (t & 7) < 4
            g_mid = jnp.concatenate([jnp.where(low[:8], row(8 * i + 1, 8), row(8 * i + 5, 8))
                                     for i in range(CHUNK // 8)], axis=0)
        else:
            g_mid = jnp.concatenate([row(b * 2 * m + m - 1, 2 * m) for b in range(CHUNK // (2 * m))], axis=0)
        sgn = sign_ref[l]
        up = sgn > 0.0
        e = jnp.exp2((g2 - g_mid) * sgn)
        x = jnp.where(up, qs, k) * e
        hi = x.astype(BF16)
        out.append((hi, (x - hi.astype(F32)).astype(BF16), e, up))
    return out


CHIP_FLIPS = ((1, 0), (0, 1), (1, 1))
HBM = pl.BlockSpec(memory_space=pl.ANY)


def _place():
    return lax.axis_index("x"), lax.axis_index("y"), lax.axis_index("c")


def _remote(src, dst, send_sems, recv_sems, k, to):
    return pltpu.make_async_remote_copy(src_ref=src, dst_ref=dst, send_sem=send_sems.at[k],
                                        recv_sem=recv_sems.at[k], device_id=to, device_id_type=MESH)


def _half_rows(ref, c):
    half = ref.shape[-2] // 2
    rows = pl.ds(pl.multiple_of(c * half, half), half)
    return ref.at[:, rows, :] if len(ref.shape) == 3 else ref.at[rows, :]


class _Exchange:
    def __init__(self, inputs, out_shapes, n_sems, start, finish, aliases=None):
        self.inputs, self.out_shapes, self.n_sems = list(inputs), list(out_shapes), n_sems
        self.start, self.finish, self.aliases = start, finish, dict(aliases or {})


def _ex_swap(grads):
    def copies(ins, outs, send, recv):
        x, y, c = _place()
        return [_remote(_half_rows(g, 1 - c), o, send, recv, t, (x, y, 1 - c))
                for t, (g, o) in enumerate(zip(ins, outs))]

    def start(*refs):
        for cp in copies(*refs):
            cp.start()

    def finish(*refs):
        cps = copies(*refs)
        for cp in cps:
            cp.wait_recv()
        for cp in cps:
            cp.wait_send()

    shapes = [jax.ShapeDtypeStruct((a.shape[0], a.shape[1] // 2, a.shape[2]), F32) for a in grads]
    return _Exchange(grads, shapes, len(grads), start, finish)


def _ex_send(parts16, owners, units=None, landed=None):
    n_t = len(parts16)
    units = units or [tuple(range(len(o))) for o in owners]
    landed = landed or [None] * n_t
    given = [t for t in range(n_t) if landed[t] is not None]

    def each(ins, outs, send, recv, to_sender, to_owner):
        x, y, c = _place()
        k = 0
        for t, own in enumerate(owners):
            for j in units[t]:
                for r, (fx, fy) in enumerate(CHIP_FLIPS):
                    tx, ty = x ^ fx, y ^ fy
                    cp = _remote(ins[t].at[j], outs[t].at[j, r], send, recv, k, (tx, ty, c))
                    if to_sender is not None:
                        pl.when(2 * tx + ty == own[j])(functools.partial(to_sender, cp))
                    if to_owner is not None:
                        pl.when(2 * x + y == own[j])(functools.partial(to_owner, cp))
                    k += 1

    def start(*refs):
        each(*refs, lambda cp: cp.start(), None)

    def finish(*refs):
        each(*refs, None, lambda cp: cp.wait_recv())
        each(*refs, lambda cp: cp.wait_send(), None)

    shapes = [jax.ShapeDtypeStruct((a.shape[0], len(CHIP_FLIPS)) + a.shape[1:], BF16) for a in parts16]
    return _Exchange(list(parts16) + [landed[t] for t in given], shapes,
                     len(CHIP_FLIPS) * sum(len(u) for u in units), start, finish,
                     aliases={n_t + i: t for i, t in enumerate(given)})


def _ex_join(units, owners):
    def each(ins, outs, send, recv, fn):
        x, y, c = _place()
        k = 0
        for t, own in enumerate(owners):
            for j, o in enumerate(own):
                def half(cc, to, u=outs[t].at[j], k=k):
                    return _remote(_half_rows(u, cc), _half_rows(u, cc), send, recv, k, to)
                mine = functools.partial(half, c, (x, y, 1 - c))
                theirs = functools.partial(half, 1 - c, (x, y, c))
                pl.when(2 * x + y == o)(functools.partial(fn, mine, theirs))
                k += 1

    def start(*refs):
        each(*refs, lambda mine, theirs: mine().start())

    def finish(*refs):
        each(*refs, lambda mine, theirs: theirs().wait_recv())
        each(*refs, lambda mine, theirs: mine().wait_send())

    shapes = [jax.ShapeDtypeStruct(a.shape, F32) for a in units]
    return _Exchange(units, shapes, sum(len(o) for o in owners), start, finish,
                     aliases={t: t for t in range(len(units))})


def _ex_gather(slots):
    n_t = len(slots)
    n_fl = len(CHIP_FLIPS)

    def piece(ref, shard, half):
        return _half_rows(ref.at[shard], half)

    def first(outs, send, recv):
        x, y, c = _place()
        s = 2 * x + y
        return [_remote(piece(outs[t], s, c), piece(outs[t], s, c), send, recv, n_t * j + t, (x ^ fx, y ^ fy, c))
                for j, (fx, fy) in enumerate(CHIP_FLIPS) for t in range(n_t)]

    def start(ins, outs, send, recv):
        for cp in first(outs, send, recv):
            cp.start()

    def finish(ins, outs, send, recv):
        x, y, c = _place()
        passed = []
        for j, (fx, fy) in enumerate(CHIP_FLIPS):
            sj = 2 * (x ^ fx) + (y ^ fy)
            for t in range(n_t):
                k = n_t * j + t
                _remote(piece(outs[t], sj, c), piece(outs[t], sj, c), send, recv, k, (x, y, c)).wait_recv()
                cp = _remote(piece(outs[t], sj, c), piece(outs[t], sj, c), send, recv, n_t * n_fl + k, (x, y, 1 - c))
                cp.start()
                passed.append(cp)
        for j, (fx, fy) in enumerate(CHIP_FLIPS):
            sj = 2 * (x ^ fx) + (y ^ fy)
            for t in range(n_t):
                k = n_t * n_fl + n_t * j + t
                _remote(piece(outs[t], sj, 1 - c), piece(outs[t], sj, 1 - c), send, recv, k, (x, y, c)).wait_recv()
        for cp in first(outs, send, recv) + passed:
            cp.wait_send()

    shapes = [jax.ShapeDtypeStruct(a.shape, BF16) for a in slots]
    return _Exchange(slots, shapes, 2 * n_t * n_fl, start, finish, aliases={t: t for t in range(n_t)})


def _ex_gather_small(parts):
    def copies(ins, outs, send, recv):
        x, y, c = _place()
        me = 4 * x + 2 * y + c
        return [_remote(ins[0], outs[0].at[me], send, recv, mask - 1,
                        (x ^ (mask >> 2), y ^ ((mask >> 1) & 1), c ^ (mask & 1))) for mask in range(1, 8)]

    def start(*refs):
        for cp in copies(*refs):
            cp.start()

    def finish(ins, outs, send, recv):
        x, y, c = _place()
        me = 4 * x + 2 * y + c
        for mask in range(1, 8):
            _remote(ins[0], outs[0].at[me ^ mask], send, recv, mask - 1, (x, y, c)).wait_recv()
        for cp in copies(ins, outs, send, recv):
            cp.wait_send()

    return _Exchange([parts], [jax.ShapeDtypeStruct((8,) + parts.shape, F32)], 7, start, finish)


def _call(body, *, name, args=(), in_specs=(), out_specs=(), out_shape=(), grid=(), scratch_shapes=(),
          exchanges=()):
    n_in, n_out, n_scr = len(args), len(out_shape), len(scratch_shapes)
    ex_in, ex_out, ex_scr, spans, alias = [], [], [], [], {}
    for ex in exchanges:
        spans.append((len(ex_in), len(ex.inputs), len(ex_out), len(ex.out_shapes)))
        for i, o in ex.aliases.items():
            alias[n_in + len(ex_in) + i] = n_out + len(ex_out) + o
        ex_in += ex.inputs
        ex_out += ex.out_shapes
        ex_scr += [pltpu.SemaphoreType.DMA((ex.n_sems,)), pltpu.SemaphoreType.DMA((ex.n_sems,))]

    def full(*refs):
        ins, x_in = refs[:n_in], refs[n_in:n_in + len(ex_in)]
        outs = refs[n_in + len(ex_in):n_in + len(ex_in) + n_out]
        x_out = refs[n_in + len(ex_in) + n_out:n_in + len(ex_in) + n_out + len(ex_out)]
        scr = refs[len(refs) - n_scr - len(ex_scr):len(refs) - len(ex_scr)]
        sems = refs[len(refs) - len(ex_scr):]

        def run(which):
            for e, (ex, (i0, ni, o0, no)) in enumerate(zip(exchanges, spans)):
                getattr(ex, which)(x_in[i0:i0 + ni], x_out[o0:o0 + no], sems[2 * e], sems[2 * e + 1])

        if grid:
            ids = [pl.program_id(a) for a in range(len(grid))]
            is_first = functools.reduce(jnp.logical_and, [i == 0 for i in ids])
            is_last = functools.reduce(jnp.logical_and, [i == g - 1 for i, g in zip(ids, grid)])
            pl.when(is_first)(lambda: run("start"))
            body(*ins, *outs, *scr)
            pl.when(is_last)(lambda: run("finish"))
        else:
            run("start")
            if body is not None:
                body(*ins, *outs, *scr)
            run("finish")

    kw = dict(grid=grid) if grid else {}
    if grid:
        kw["compiler_params"] = _params(dimension_semantics=("arbitrary",) * len(grid))
    else:
        kw["compiler_params"] = _params()
    res = pl.pallas_call(
        full, name=name,
        in_specs=list(in_specs) + [HBM] * len(ex_in),
        out_specs=list(out_specs) + [HBM] * len(ex_out),
        out_shape=list(out_shape) + ex_out,
        scratch_shapes=list(scratch_shapes) + ex_scr,
        input_output_aliases=alias, **kw,
    )(*args, *ex_in)
    own = list(res[:n_out])
    per_ex = [list(res[n_out + o0:n_out + o0 + no]) for (_, _, o0, no) in spans]
    return own, per_ex


def _cast_own(w, place, name):
    rows, cols = w.shape
    tile = min(rows, 256)

    def body(place_ref, w_ref, o_ref):
        o_ref[...] = w_ref[...].astype(BF16)

    return pl.pallas_call(
        body, name=name,
        grid_spec=pltpu.PrefetchScalarGridSpec(
            num_scalar_prefetch=1, grid=(rows // tile,),
            in_specs=[pl.BlockSpec((tile, cols), lambda i, p: (i, 0))],
            out_specs=pl.BlockSpec((None, tile, cols), lambda i, p: (p[1], i, 0))),
        out_shape=jax.ShapeDtypeStruct((N_SHARDS, rows, cols), BF16),
        compiler_params=_params(dimension_semantics=("arbitrary",)),
    )(place, w)


def _rec_fwd(proj, lb_logits, rec_g, consts, exchanges):
    tri, low, sign = consts["tri"], consts["low"], consts["sign"]

    def body(q_ref, f_ref, i_ref, rg_ref, lb_ref, g_ref, w_ref, low_ref, sign_ref, y_ref, o_ref, stp_ref, st_ref):
        @pl.when(pl.program_id(1) == 0)
        def _():
            st_ref[...] = jnp.zeros_like(st_ref)
        lb = _lower_bound(lb_ref)
        st = st_ref[...]
        rows = lambda c: pl.ds(c * CHUNK, CHUNK)
        for c0 in range(0, REC_CHUNKS, REC_GROUP):
            group = range(c0, c0 + REC_GROUP)
            gated = [_gates(q_ref[rows(c), :], f_ref[rows(c), :], lb) for c in group]
            g2s = [_dot3(w_ref[...], g) * LOG2E for (_, _, _, _, g) in gated]
            xs = [[xl for xl, _, _, _ in _level_factors(g2, qs, k, sign_ref)]
                  for g2, (qs, _, _, k, _) in zip(g2s, gated)]
            a_s = []
            for x in xs:
                a = jnp.zeros((CHUNK, CHUNK), F32)
                for l, xl in enumerate(x):
                    a = a + _dot_nt(xl, xl) * low_ref[l]
                a_s.append(a.astype(BF16))
            vbs = [i_ref[rows(c), :].astype(BF16) for c in group]
            intra = [_dot(a, vb) for a, vb in zip(a_s, vbs)]
            kvs = [_dot_tn(vb, (k * jnp.exp2(g2[CHUNK - 1:CHUNK, :] - g2)).astype(BF16))
                   for vb, g2, (_, _, _, k, _) in zip(vbs, g2s, gated)]
            for i, c in enumerate(group):
                qs, _, _, k, _ = gated[i]
                g2 = g2s[i]
                stp_ref[c] = st
                v = i_ref[rows(c), :]
                rg = rg_ref[rows(c), :]
                o = (intra[i] + jnp.sum(qs * k, axis=-1, keepdims=True) * v
                     + _dot_nt((qs * jnp.exp2(g2)).astype(BF16), st.astype(BF16)))
                st = st * jnp.exp2(g2[CHUNK - 1:CHUNK, :]) + kvs[i]
                o_ref[rows(c), :] = o
                inv = lax.rsqrt(jnp.mean(o * o, axis=-1, keepdims=True) + EPS)
                y_ref[rows(c), :] = (o * inv * g_ref[...] * (rg * _sig(rg))).astype(BF16)
        st_ref[...] = st

    sec = lambda n: pl.BlockSpec((REC_ROWS, HEAD), lambda h, b: (b, n * SEC_BLK + h))
    vec = lambda rows: pl.BlockSpec((rows, HEAD), lambda h, b: (0, h))
    full = lambda a: pl.BlockSpec(a.shape, lambda h, b: (0,) * a.ndim)
    return _call(
        body, name="rec_fwd", grid=(N_HEADS, N_REC_BLK),
        args=(proj, proj, proj, proj, lb_logits, rec_g, tri, low, sign),
        in_specs=[sec(2), sec(3), sec(4), sec(5), vec(2), vec(1), full(tri), full(low), full(sign)],
        out_specs=[pl.BlockSpec((REC_ROWS, HEAD), lambda h, b: (b, h)),
                   pl.BlockSpec((REC_ROWS, HEAD), lambda h, b: (b, h)),
                   pl.BlockSpec((None, REC_CHUNKS, HEAD, HEAD), lambda h, b: (h, b, 0, 0))],
        out_shape=[jax.ShapeDtypeStruct((SEQ, D_MODEL), BF16),
                   jax.ShapeDtypeStruct((SEQ, D_MODEL), F32),
                   jax.ShapeDtypeStruct((N_HEADS, SEQ // CHUNK, HEAD, HEAD), F32)],
        scratch_shapes=[pltpu.VMEM((HEAD, HEAD), F32)],
        exchanges=exchanges)


OUT_ROWS = 256


def _out_proj_loss(y_pool, y_rec, w_out_g, x, target, gf):
    def body(yp_ref, yr_ref, w_ref, x_ref, t_ref, gf_ref, dout_ref, doutb_ref, part_ref):
        @pl.when(pl.program_id(0) == 0)
        def _():
            part_ref[...] = jnp.zeros_like(part_ref)
        out = (x_ref[...] + _dot(yp_ref[...], w_ref[0:D_MODEL, :])
               + _dot(yr_ref[...], w_ref[D_MODEL:2 * D_MODEL, :]))
        inv = lax.rsqrt(jnp.mean(out * out, axis=-1, keepdims=True) + EPS)
        gf_v = gf_ref[...]
        diff = out * inv * gf_v - t_ref[...]
        dyf = diff * (1.0 / D_MODEL)
        a = dyf * gf_v
        dout = inv * a - out * (inv * inv * inv) * jnp.mean(a * out, axis=-1, keepdims=True)
        dout_ref[...] = dout
        doutb_ref[...] = dout.astype(BF16)
        part_ref[0:1, :] += jnp.sum(dyf * out * inv, axis=0, keepdims=True)
        part_ref[1:2, :] += jnp.sum(diff * diff, axis=0, keepdims=True)

    row = lambda n: pl.BlockSpec((OUT_ROWS, n), lambda i: (i, 0))
    return pl.pallas_call(
        body, name="out_proj_loss", grid=(SEQ // OUT_ROWS,),
        in_specs=[row(D_MODEL), row(D_MODEL), pl.BlockSpec((2 * D_MODEL, D_MODEL), lambda i: (0, 0)),
                  row(D_MODEL), row(D_MODEL), pl.BlockSpec((1, D_MODEL), lambda i: (0, 0))],
        out_specs=[row(D_MODEL), row(D_MODEL), pl.BlockSpec((8, D_MODEL), lambda i: (0, 0))],
        out_shape=[jax.ShapeDtypeStruct((SEQ, D_MODEL), F32),
                   jax.ShapeDtypeStruct((SEQ, D_MODEL), BF16),
                   jax.ShapeDtypeStruct((8, D_MODEL), F32)],
        compiler_params=_params(dimension_semantics=("arbitrary",)),
    )(y_pool, y_rec, w_out_g, x, target, gf)


def _grad_w_out(y_pool, y_rec, dout_b):
    blk = W_OUT_SHARD // 2
    per = D_MODEL // blk
    n = 2 * per

    def body(yp_ref, yr_ref, d_ref, p32_ref, p16_ref, send_ref, recv_ref, send_sems, recv_sems):
        j = pl.program_id(0)
        x, y, c = _place()

        def copy(u):
            return _remote(send_ref.at[u], recv_ref.at[u], send_sems, recv_sems, u, (x, y, 1 - c))

        for i in range(n):
            @pl.when(j == i)
            def _(i=i):
                res = _dot_tn((yp_ref if i < per else yr_ref)[...], d_ref[...])

                @pl.when(i % 2 == c)
                def _():
                    p32_ref[i // 2] = res

                @pl.when(i % 2 != c)
                def _():
                    send_ref[i // 2] = res
                    copy(i // 2).start()

        @pl.when(j == n - 1)
        def _():
            for u in range(N_SHARDS):
                copy(u).wait_recv()
                tot = p32_ref[u] + recv_ref[u]
                p32_ref[u] = tot
                p16_ref[u] = tot.astype(BF16)
            for u in range(N_SHARDS):
                copy(u).wait_send()

    whole = pl.BlockSpec((N_SHARDS, blk, D_MODEL), lambda j: (0, 0, 0))
    return pl.pallas_call(
        body, name="grad_w_out", grid=(n,),
        in_specs=[pl.BlockSpec((SEQ, blk), lambda j: (0, jnp.minimum(j, per - 1))),
                  pl.BlockSpec((SEQ, blk), lambda j: (0, jnp.maximum(j - per, 0))),
                  pl.BlockSpec((SEQ, D_MODEL), lambda j: (0, 0))],
        out_specs=[whole, whole],
        out_shape=[jax.ShapeDtypeStruct((N_SHARDS, blk, D_MODEL), F32),
                   jax.ShapeDtypeStruct((N_SHARDS, blk, D_MODEL), BF16)],
        scratch_shapes=[pltpu.VMEM((N_SHARDS, blk, D_MODEL), F32), pltpu.VMEM((N_SHARDS, blk, D_MODEL), F32),
                        pltpu.SemaphoreType.DMA((N_SHARDS,)), pltpu.SemaphoreType.DMA((N_SHARDS,))],
        compiler_params=_params(dimension_semantics=("arbitrary",)),
    )(y_pool, y_rec, dout_b)


def _pool_bwd(proj, dout_b, w_out_g, pw_g, pool_scale, exchanges):
    n = POOL_ROWS + POOL_HALO

    def body(u_ref, gate_ref, d_ref, wo_ref, pw_ref, sc_ref,
             dp_ref, dpw_ref, dsc_ref, dd_ref, ddw_ref):
        g = pl.program_id(0)
        dpw_ref[...] = jnp.zeros_like(dpw_ref)
        dsc_ref[...] = jnp.zeros_like(dsc_ref)

        def first(i, _):
            r = pl.ds(pl.multiple_of(i * POOL_ROWS, POOL_ROWS), POOL_ROWS)
            d, inv_count = _pool_diff(u_ref, i, g)
            db = d.astype(BF16)
            mixed = _dot(db, pw_ref[...])
            gate = gate_ref[r, :]
            sg = _sig(gate)
            silu = gate * sg
            dy = _dot_nt(d_ref[r, :], wo_ref[...])
            sc = sc_ref[...]
            dmixed = dy * silu * sc
            dp_ref[1, r, :] = (dy * mixed * sc * (sg * (1.0 + gate * (1.0 - sg)))).astype(BF16)
            dsc_ref[...] += jnp.sum(dy * silu * mixed, axis=0, keepdims=True)
            dmb = dmixed.astype(BF16)
            dpw_ref[...] += _dot_tn(db, dmb)
            dd = _dot_nt(dmb, pw_ref[...])
            dd_ref[r, :] = dd
            ddw_ref[r, :] = dd * inv_count
            return 0
        lax.fori_loop(0, SEQ // POOL_ROWS, first, 0)

        def second(i, _):
            r0 = i * POOL_ROWS
            r = pl.ds(pl.multiple_of(r0, POOL_ROWS), POOL_ROWS)
            last = i == SEQ // POOL_ROWS - 1
            after = ddw_ref[pl.ds(pl.multiple_of(jnp.minimum(r0 + POOL_ROWS, SEQ - POOL_HALO), 8), POOL_HALO), :]
            after = jnp.where(last, 0.0, after)
            ext = jnp.concatenate([ddw_ref[r, :], after], axis=0)
            s = _window_sums(ext, g, lambda k: n - (1 << k))[:POOL_ROWS, :]
            dp_ref[0, r, :] = (s - dd_ref[r, :]).astype(BF16)
            return 0
        lax.fori_loop(0, SEQ // POOL_ROWS, second, 0)

    return _call(
        body, name="pool_bwd", grid=(N_GROUPS,),
        args=(proj, proj, dout_b, w_out_g, pw_g, pool_scale),
        in_specs=[pl.BlockSpec((SEQ, PG), lambda g: (0, g)),
                  pl.BlockSpec((SEQ, PG), lambda g: (0, N_GROUPS + g)),
                  pl.BlockSpec((SEQ, D_MODEL), lambda g: (0, 0)),
                  pl.BlockSpec((PG, D_MODEL), lambda g: (g, 0)),
                  pl.BlockSpec((None, PG, PG), lambda g: (g, 0, 0)),
                  pl.BlockSpec((1, PG), lambda g: (0, g))],
        out_specs=[pl.BlockSpec((2, SEQ, PG), lambda g: (0, 0, g)),
                   pl.BlockSpec((None, PG, PG), lambda g: (g, 0, 0)),
                   pl.BlockSpec((1, PG), lambda g: (0, g))],
        out_shape=[jax.ShapeDtypeStruct((2, SEQ, D_MODEL), BF16),
                   jax.ShapeDtypeStruct((N_GROUPS, PG, PG), F32),
                   jax.ShapeDtypeStruct((1, D_MODEL), F32)],
        scratch_shapes=[pltpu.VMEM((SEQ, PG), F32), pltpu.VMEM((SEQ, PG), F32)],
        exchanges=exchanges)


HALF_HEADS = N_HEADS // 2
HALF_COLS = HALF_HEADS * HEAD


def _rec_bwd(proj, o_raw, st_prev, dout_b, w_out_g, lb_logits, rec_g, consts, h0, name, exchanges):
    def body(q_ref, f_ref, i_ref, rg_ref, o_ref, stp_ref, d_ref, wo_ref, lb_ref, g_ref,
             w_ref, lowt_ref, sym_ref, sign_ref, tri_ref,
             dr_ref, part_ref, dst_ref):
        @pl.when(pl.program_id(1) == 0)
        def _():
            dst_ref[...] = jnp.zeros_like(dst_ref)
            part_ref[...] = jnp.zeros_like(part_ref)
        tril = (lax.broadcasted_iota(jnp.int32, (CHUNK, CHUNK), 0)
                > lax.broadcasted_iota(jnp.int32, (CHUNK, CHUNK), 1))
        lb = _lower_bound(lb_ref)
        grec = g_ref[...]
        dst = dst_ref[...]
        acc_grec = jnp.zeros((1, HEAD), F32)
        acc_lb = jnp.zeros((1, HEAD), F32)
        rows = lambda c: pl.ds(c * CHUNK, CHUNK)
        for c0 in reversed(range(0, REC_CHUNKS, REC_GROUP)):
            group = list(reversed(range(c0, c0 + REC_GROUP)))
            dys = [_dot_nt(d_ref[rows(c), :], wo_ref[...]) for c in group]
            dos = []
            for c, dy in zip(group, dys):
                rg = rg_ref[rows(c), :]
                o = o_ref[rows(c), :]
                sg = _sig(rg)
                silu = rg * sg
                inv = lax.rsqrt(jnp.mean(o * o, axis=-1, keepdims=True) + EPS)
                recn = o * inv
                dr_ref[3, rows(c), :] = (dy * recn * grec * (sg * (1.0 + rg * (1.0 - sg)))).astype(BF16)
                acc_grec = acc_grec + jnp.sum(dy * silu * recn, axis=0, keepdims=True)
                drecn = dy * silu * grec
                dos.append(inv * drecn - o * (inv * inv * inv) * jnp.mean(drecn * o, axis=-1, keepdims=True))
            gated = [_gates(q_ref[rows(c), :], f_ref[rows(c), :], lb) for c in group]
            g2s = [_dot3(w_ref[...], g) * LOG2E for (_, _, _, _, g) in gated]
            levels = [_level_factors(g2, qs, k, sign_ref) for g2, (qs, _, _, k, _) in zip(g2s, gated)]
            a_ts = []
            for lev in levels:
                a_t = jnp.zeros((CHUNK, CHUNK), F32)
                for l, (xl, _, _, _) in enumerate(lev):
                    a_t = a_t + _dot_nt(xl, xl) * lowt_ref[l]
                a_ts.append(a_t.astype(BF16))
            dobs = [do.astype(BF16) for do in dos]
            vbs = [i_ref[rows(c), :].astype(BF16) for c in group]
            d_syms = [jnp.where(tril, _dot_nt(dob, vb), _dot_nt(vb, dob)) for dob, vb in zip(dobs, vbs)]
            dqs_is, dk_is = [], []
            for lev, d_sym in zip(levels, d_syms):
                dqs_i = jnp.zeros((CHUNK, HEAD), F32)
                both_i = jnp.zeros((CHUNK, HEAD), F32)
                for l, (xl, xlo, e, up) in enumerate(lev):
                    z = d_sym * sym_ref[l]
                    tmp = _dot(z.astype(BF16), jnp.concatenate([xl, xlo], axis=-1))
                    tmp = (tmp[:, :HEAD] + tmp[:, HEAD:]) * e
                    dqs_i = dqs_i + jnp.where(up, tmp, 0.0)
                    both_i = both_i + tmp
                dqs_is.append(dqs_i)
                dk_is.append(both_i - dqs_i)
            e_gs = [jnp.exp2(g2) for g2 in g2s]
            e_revs = [jnp.exp2(g2[CHUNK - 1:CHUNK, :] - g2) for g2 in g2s]
            e_lasts = [jnp.exp2(g2[CHUNK - 1:CHUNK, :]) for g2 in g2s]
            q_gs = [qs * e_g for (qs, _, _, _, _), e_g in zip(gated, e_gs)]
            kdecs = [k * e_rev for (_, _, _, k, _), e_rev in zip(gated, e_revs)]
            dv12 = [_dot(a_t, dob) + jnp.sum(qs * k, axis=-1, keepdims=True) * do
                    for a_t, dob, do, (qs, _, _, k, _) in zip(a_ts, dobs, dos, gated)]
            dq_gs = [_dot(dob, stp_ref[c].astype(BF16)) for c, dob in zip(group, dobs)]
            steps = [_dot_tn(dob, q_g.astype(BF16)) for dob, q_g in zip(dobs, q_gs)]
            dsts = []
            for e_last, step in zip(e_lasts, steps):
                dsts.append(dst)
                dst = dst * e_last + step
            dstbs = [d.astype(BF16) for d in dsts]
            dv3 = [_dot_nt(kdec.astype(BF16), dstb) for kdec, dstb in zip(kdecs, dstbs)]
            dkdecs = [_dot(vb, dstb) for vb, dstb in zip(vbs, dstbs)]
            dbig_gs, dg_lasts, dqss, dks = [], [], [], []
            for i, c in enumerate(group):
                qs, _, _, k, _ = gated[i]
                de_last = jnp.sum(stp_ref[c] * dsts[i], axis=0, keepdims=True)
                ddiag = jnp.sum(dos[i] * i_ref[rows(c), :], axis=-1, keepdims=True)
                dqss.append(dqs_is[i] + ddiag * k + dq_gs[i] * e_gs[i])
                dks.append(dk_is[i] + ddiag * qs + dkdecs[i] * e_revs[i])
                dg_rev = dkdecs[i] * kdecs[i]
                dg_lasts.append(jnp.sum(dg_rev, axis=0, keepdims=True) + de_last * e_lasts[i])
                dbig_gs.append(qs * dqs_is[i] - k * dk_is[i] + dq_gs[i] * q_gs[i] - dg_rev)
            dgs = [_dot3(tri_ref[...], dbig_g) + dg_last for dbig_g, dg_last in zip(dbig_gs, dg_lasts)]
            for i, c in enumerate(group):
                _, sf, f, _, _ = gated[i]
                q = q_ref[rows(c), :]
                df = dgs[i] / f - dks[i]
                dr_ref[1, rows(c), :] = (df * (1.0 - lb) * sf * (1.0 - sf)).astype(BF16)
                acc_lb = acc_lb + jnp.sum(df * (1.0 - sf), axis=0, keepdims=True)
                sq = _sig(q)
                dr_ref[0, rows(c), :] = (dqss[i] * (sq * (1.0 + q * (1.0 - sq)))).astype(BF16)
                dr_ref[2, rows(c), :] = (dv12[i] + dv3[i]).astype(BF16)
        dst_ref[...] = dst
        part_ref[0:1, :] += acc_grec
        part_ref[1:2, :] += acc_lb

    rev = lambda b: N_REC_BLK - 1 - b
    sec = lambda n: pl.BlockSpec((REC_ROWS, HEAD), lambda h, b: (rev(b), n * SEC_BLK + h0 + h))
    col_in = pl.BlockSpec((REC_ROWS, HEAD), lambda h, b: (rev(b), h0 + h))
    vec_in = lambda rows: pl.BlockSpec((rows, HEAD), lambda h, b: (0, h0 + h))
    full = lambda a: pl.BlockSpec(a.shape, lambda h, b: (0,) * a.ndim)
    return _call(
        body, name=name, grid=(HALF_HEADS, N_REC_BLK),
        args=(proj, proj, proj, proj, o_raw, st_prev, dout_b, w_out_g, lb_logits, rec_g,
              consts["tri"], consts["low_t"], consts["sym"], consts["sign"], consts["tri_t"]),
        in_specs=[sec(2), sec(3), sec(4), sec(5), col_in,
                  pl.BlockSpec((None, REC_CHUNKS, HEAD, HEAD), lambda h, b: (h0 + h, rev(b), 0, 0)),
                  pl.BlockSpec((REC_ROWS, D_MODEL), lambda h, b: (rev(b), 0)),
                  pl.BlockSpec((HEAD, D_MODEL), lambda h, b: (SEC_BLK + h0 + h, 0)),
                  vec_in(2), vec_in(1)] + [full(consts[n]) for n in ("tri", "low_t", "sym", "sign", "tri_t")],
        out_specs=[pl.BlockSpec((4, REC_ROWS, HEAD), lambda h, b: (0, rev(b), h)),
                   pl.BlockSpec((8, HEAD), lambda h, b: (0, h))],
        out_shape=[jax.ShapeDtypeStruct((4, SEQ, HALF_COLS), BF16),
                   jax.ShapeDtypeStruct((8, HALF_COLS), F32)],
        scratch_shapes=[pltpu.VMEM((HEAD, HEAD), F32)],
        exchanges=exchanges)


def _w_in_block(w_ref, j):
    per_shard = W_IN_SHARD // COL_BLK
    return w_ref[j // per_shard, :, (j % per_shard) * COL_BLK:(j % per_shard + 1) * COL_BLK]


def _grad_x(dproj, w_in_g, x, g1, dout, exchanges):
    rows = 256
    n_blk = len(dproj)

    def body(*refs):
        dp_refs = refs[:n_blk]
        w_ref, x_ref, g_ref, dout_ref, dx_ref, part_ref = refs[n_blk:]

        @pl.when(pl.program_id(0) == 0)
        def _():
            part_ref[...] = jnp.zeros_like(part_ref)
        dh = jnp.zeros((rows, D_MODEL), F32)
        for j in range(n_blk):
            dh = dh + _dot_nt(dp_refs[j][...], _w_in_block(w_ref, j))
        xv = x_ref[...]
        inv = lax.rsqrt(jnp.mean(xv * xv, axis=-1, keepdims=True) + EPS)
        a = dh * g_ref[...]
        dx_ref[...] = (dout_ref[...] + inv * a
                       - xv * (inv * inv * inv) * jnp.mean(a * xv, axis=-1, keepdims=True))
        part_ref[0:1, :] += jnp.sum(dh * xv * inv, axis=0, keepdims=True)

    row = lambda: pl.BlockSpec((rows, D_MODEL), lambda i: (i, 0))
    dp_spec = lambda sec, cb: pl.BlockSpec((None, rows, COL_BLK), lambda i: (sec, i, cb))
    return _call(
        body, name="grad_x", grid=(SEQ // rows,),
        args=tuple(a for a, _, _ in dproj) + (w_in_g, x, g1, dout),
        in_specs=[dp_spec(sec, cb) for _, sec, cb in dproj]
                 + [pl.BlockSpec((N_SHARDS, D_MODEL, W_IN_SHARD), lambda i: (0, 0, 0)),
                    row(), pl.BlockSpec((1, D_MODEL), lambda i: (0, 0)), row()],
        out_specs=[row(), pl.BlockSpec((8, D_MODEL), lambda i: (0, 0))],
        out_shape=[jax.ShapeDtypeStruct((SEQ, D_MODEL), F32),
                   jax.ShapeDtypeStruct((8, D_MODEL), F32)],
        exchanges=exchanges)


def _grad_w_in(h, dp, blocks, name):
    n_blk = len(blocks)
    half = D_MODEL // 2
    pick = lambda vals: (lambda j: functools.reduce(lambda acc, iv: jnp.where(j == iv[0], iv[1], acc),
                                                     list(enumerate(vals))[1:], vals[0]))
    sec_of = pick([sec for sec, _ in blocks])
    cb_of = pick([cb for _, cb in blocks])

    def body(h_ref, dp_ref, p32_ref, p16_ref, send_ref, recv_ref, send_sems, recv_sems):
        j = pl.program_id(0)
        x, y, c = _place()
        cols = lambda cc: pl.ds(pl.multiple_of(cc * half, half), half)

        def copy(i):
            return _remote(send_ref.at[i], recv_ref.at[i], send_sems, recv_sems, i, (x, y, 1 - c))

        for i in range(n_blk):
            @pl.when(j == i)
            def _(i=i):
                send_ref[i] = _dot_tn(h_ref[:, cols(1 - c)], dp_ref[...])
                copy(i).start()
                p32_ref[i] = _dot_tn(h_ref[:, cols(c)], dp_ref[...])

        @pl.when(j == n_blk - 1)
        def _():
            for i in range(n_blk):
                copy(i).wait_recv()
                tot = p32_ref[i] + recv_ref[i]
                p32_ref[i] = tot
                p16_ref[i] = tot.astype(BF16)
            for i in range(n_blk):
                copy(i).wait_send()

    whole = pl.BlockSpec((n_blk, half, COL_BLK), lambda j: (0, 0, 0))
    return pl.pallas_call(
        body, name=name, grid=(n_blk,),
        in_specs=[pl.BlockSpec((SEQ, D_MODEL), lambda j: (0, 0)),
                  pl.BlockSpec((None, SEQ, COL_BLK), lambda j: (sec_of(j), 0, cb_of(j)))],
        out_specs=[whole, whole],
        out_shape=[jax.ShapeDtypeStruct((n_blk, half, COL_BLK), F32),
                   jax.ShapeDtypeStruct((n_blk, half, COL_BLK), BF16)],
        scratch_shapes=[pltpu.VMEM((n_blk, half, COL_BLK), F32), pltpu.VMEM((n_blk, half, COL_BLK), F32),
                        pltpu.SemaphoreType.DMA((n_blk,)), pltpu.SemaphoreType.DMA((n_blk,))],
        compiler_params=_params(dimension_semantics=("arbitrary",)),
    )(h, dp)


def _add_units(grad, recv, place, tile, name):
    n, rows, cols = grad.shape
    per_half = rows // 2 // tile

    def body(place_ref, g_ref, r_ref, o32_ref, o16_ref):
        v = g_ref[...] + r_ref[...]
        o32_ref[...] = v
        o16_ref[...] = v.astype(BF16)

    blk = lambda f: pl.BlockSpec((None, tile, cols), f)
    out = lambda s, i, p: (s, i, 0)
    return pl.pallas_call(
        body, name=name,
        grid_spec=pltpu.PrefetchScalarGridSpec(
            num_scalar_prefetch=1, grid=(n, per_half),
            in_specs=[blk(lambda s, i, p: (s, p[0] * per_half + i, 0)), blk(out)],
            out_specs=[blk(out), blk(out)]),
        out_shape=[jax.ShapeDtypeStruct(recv.shape, F32), jax.ShapeDtypeStruct(recv.shape, BF16)],
        compiler_params=_params(dimension_semantics=("arbitrary", "arbitrary")),
    )(place, grad, recv)


def _sum_units(part32, recv16, place, owners, tile, name):
    n, half, cols = part32.shape
    per_half = half // tile
    own = jnp.asarray(owners, jnp.int32) == place[1]
    order = jnp.argsort(jnp.where(own, 0, 1), stable=True).astype(jnp.int32)
    sched = jnp.concatenate([place[:1], jnp.sum(own).reshape(1).astype(jnp.int32), order])

    def block(k, i, p):
        live = k < p[1]
        unit = p[2 + jnp.minimum(k, jnp.maximum(p[1] - 1, 0))]
        return unit, jnp.where(live, i, per_half - 1)

    def body(sched_ref, p_ref, r_ref, o_ref):
        @pl.when(pl.program_id(0) < sched_ref[1])
        def _():
            acc = p_ref[...]
            for j in range(len(CHIP_FLIPS)):
                acc = acc + r_ref[j].astype(F32)
            o_ref[...] = acc

    return pl.pallas_call(
        body, name=name,
        grid_spec=pltpu.PrefetchScalarGridSpec(
            num_scalar_prefetch=1, grid=(n, per_half),
            in_specs=[pl.BlockSpec((None, tile, cols), lambda k, i, p: (*block(k, i, p), 0)),
                      pl.BlockSpec((None, len(CHIP_FLIPS), tile, cols),
                                   lambda k, i, p: (block(k, i, p)[0], 0, block(k, i, p)[1], 0))],
            out_specs=pl.BlockSpec((None, tile, cols),
                                   lambda k, i, p: (block(k, i, p)[0], p[0] * per_half + block(k, i, p)[1], 0))),
        out_shape=jax.ShapeDtypeStruct((n, 2 * half, cols), F32),
        compiler_params=_params(dimension_semantics=("arbitrary", "arbitrary")),
    )(sched, part32, recv16)


def _adamw_math(w, g, m, v):
    m = ADAM_B1 * m + (1.0 - ADAM_B1) * g
    v = ADAM_B2 * v + (1.0 - ADAM_B2) * (g * g)
    m_hat = m / (1.0 - ADAM_B1 ** ADAM_STEP)
    v_hat = v / (1.0 - ADAM_B2 ** ADAM_STEP)
    delta = -ADAM_LR * (m_hat / (jnp.sqrt(v_hat) + ADAM_EPS) + ADAM_WD * w)
    return delta, m, v


def _adamw_units(w, m, v, grads, pick, name):
    rows, cols = w.shape
    bc = grads[0].shape[-1]
    tile = min(rows, 256)
    n_g = len(grads)

    def body(pick_ref, w_ref, m_ref, v_ref, *refs):
        g_refs, (g_out, d_ref, nm_ref, nv_ref) = refs[:n_g], refs[n_g:]
        p = pl.program_id(0)
        for a in range(n_g):
            @pl.when(pick_ref[0, p] == a)
            def _(a=a):
                g = g_refs[a][...]
                g_out[...] = g
                d_ref[...], nm_ref[...], nv_ref[...] = _adamw_math(w_ref[...], g, m_ref[...], v_ref[...])

    blk = pl.BlockSpec((tile, bc), lambda p, i, pick: (i, p))

    def g_spec(a):
        return pl.BlockSpec((None, tile, bc),
                            lambda p, i, pick: (jnp.where(pick[0, p] == a, pick[1, p], 0),
                                                jnp.where(pick[0, p] == a, i, 0), 0))

    return pl.pallas_call(
        body, name=name,
        grid_spec=pltpu.PrefetchScalarGridSpec(
            num_scalar_prefetch=1, grid=(cols // bc, rows // tile),
            in_specs=[blk] * 3 + [g_spec(a) for a in range(n_g)],
            out_specs=[blk] * 4),
        out_shape=[jax.ShapeDtypeStruct(w.shape, F32)] * 4,
        compiler_params=_params(dimension_semantics=("arbitrary", "arbitrary")),
    )(pick, w, m, v, *grads)


ROW_NORM1, ROW_SCALE, ROW_LB, ROW_REC, ROW_FINAL, ROW_LOSS = 0, 1, 2, 4, 5, 6


def _small_update(parts, gathered, w, m, v):
    def body(own_ref, p_ref, w_ref, m_ref, v_ref, loss_ref, g_ref, d_ref, nm_ref, nv_ref):
        x, y, c = _place()
        me = 4 * x + 2 * y + c
        slot = lambda d: jnp.where(me == d, own_ref[...], p_ref[d])
        tot = slot(0)
        for d in range(1, 8):
            tot = tot + slot(d)
        wv = w_ref[...]
        l0 = wv[ROW_LB:ROW_LB + 1, :]
        l1 = wv[ROW_LB + 1:ROW_LB + 2, :]
        mx = jnp.maximum(l0, l1)
        e0 = jnp.exp(l0 - mx)
        e1 = jnp.exp(l1 - mx)
        lb = e0 / (e0 + e1)
        dl0 = tot[ROW_LB:ROW_LB + 1, :] * lb * (1.0 - lb)
        row = lax.broadcasted_iota(jnp.int32, tot.shape, 0)
        g = jnp.where(row == ROW_LB, dl0, jnp.where(row == ROW_LB + 1, -dl0, tot))
        g = jnp.where(row >= ROW_LOSS, 0.0, g)
        g_ref[...] = g
        d_ref[...], nm_ref[...], nv_ref[...] = _adamw_math(wv, g, m_ref[...], v_ref[...])
        loss_ref[...] = (0.5 / D_MODEL) * jnp.sum(tot[ROW_LOSS:ROW_LOSS + 1, :], axis=-1, keepdims=True)

    return pl.pallas_call(
        body, name="small_update",
        out_shape=[jax.ShapeDtypeStruct((1, 1), F32)] + [jax.ShapeDtypeStruct(w.shape, F32)] * 4,
        compiler_params=_params(),
    )(parts, gathered, w, m, v)


SHARD_OWNERS = tuple(range(N_SHARDS))
BLOCKS_POOL = (0, 1, 2, 3)
BLOCKS_A = (4, 6, 8, 10)
BLOCKS_B = (5, 7, 9, 11)
BLOCK_GROUPS = (BLOCKS_POOL, BLOCKS_A, BLOCKS_B)


def _block_owners(blocks):
    return tuple(j // (W_IN_SHARD // COL_BLK) for j in blocks)


def _small_rows(norm1, scale, lb, rec, final):
    pad = jnp.zeros((2, D_MODEL), F32)
    return jnp.concatenate([norm1, scale, lb, rec, final.reshape(1, D_MODEL), pad], axis=0)


def kernel(x, norm1_g, w_in, pool_w, pool_scale, lb_logits, rec_norm_g, w_out, final_norm_g, loss_target, m_norm1_g, m_w_in, m_pool_w, m_pool_scale, m_lb_logits, m_rec_norm_g, m_w_out, m_final_norm_g, v_norm1_g, v_w_in, v_pool_w, v_pool_scale, v_lb_logits, v_rec_norm_g, v_w_out, v_final_norm_g):
    xi, yi, ci = _place()
    chip = 2 * xi + yi
    place = jnp.stack([ci, chip]).astype(jnp.int32)
    pw_rows = N_GROUPS * PW_SHARD
    flat_pw = lambda a: a.reshape(pw_rows, PG)
    x2, target, gf = x[0], loss_target[0], final_norm_g.reshape(1, D_MODEL)
    consts = {n: jnp.asarray(a, BF16 if n.startswith("tri") else F32) for n, a in _chunk_constants().items()}

    proj, h, w_in_g = _in_proj(x2, norm1_g, _cast_own(w_in[0], place, "cast_w_in"), place)
    (y_rec, o_raw, st_prev), ((w_out_g, pw_g),) = _rec_fwd(
        proj, lb_logits, rec_norm_g, consts,
        [_ex_gather([_cast_own(w_out[0], place, "cast_w_out"), _cast_own(flat_pw(pool_w), place, "cast_pool_w")])])
    w_out_g = w_out_g.reshape(2 * D_MODEL, D_MODEL)
    pw_full = pw_g.reshape(N_SHARDS, N_GROUPS, PW_SHARD, PG).transpose(1, 0, 2, 3).reshape(N_GROUPS, PG, PG)
    y_pool = _pool_fwd(proj, pw_full, pool_scale)
    dout, dout_b, part_out = _out_proj_loss(y_pool, y_rec, w_out_g, x2, target, gf)

    p_out32, p_out16 = _grad_w_out(y_pool, y_rec, dout_b)
    (dpool, gpw, dscale), ((rb_out,),) = _pool_bwd(proj, dout_b, w_out_g, pw_full, pool_scale,
                                                   [_ex_send([p_out16], [SHARD_OWNERS])])
    g_out = _sum_units(p_out32, rb_out, place, SHARD_OWNERS, 256, "sum_w_out")
    gpw = gpw.reshape(N_GROUPS, N_SHARDS, PW_SHARD, PG).transpose(1, 0, 2, 3).reshape(N_SHARDS, pw_rows, PG)
    p_inp32, p_inp16 = _grad_w_in(h, dpool, [(0, 0), (0, 1), (1, 0), (1, 1)], "grad_w_in_pool")

    pool_owners, a_owners, b_owners = (_block_owners(b) for b in BLOCK_GROUPS)
    rec_args = (proj, o_raw, st_prev, dout_b, w_out_g, lb_logits, rec_norm_g, consts)
    (drec_a, part_a), ((rb_inp,), (ra_pw,)) = _rec_bwd(
        *rec_args, 0, "rec_bwd_a", [_ex_send([p_inp16], [pool_owners], units=[(0, 1)]), _ex_swap([gpw])])
    p_pw32, p_pw16 = _add_units(gpw, ra_pw, place, 128, "add_pool_w")
    rec_blocks = [(n, 0) for n in range(4)]
    p_ina32, p_ina16 = _grad_w_in(h, drec_a, rec_blocks, "grad_w_in_a")

    (drec_b, part_b), ((rb_inp, rb_ina, rb_pw),) = _rec_bwd(
        *rec_args, HALF_HEADS, "rec_bwd_b",
        [_ex_send([p_inp16, p_ina16, p_pw16], [pool_owners, a_owners, SHARD_OWNERS],
                  units=[(2, 3), (0, 1, 2, 3), (0, 1, 2, 3)], landed=[rb_inp, None, None])])
    g_inp = _sum_units(p_inp32, rb_inp, place, pool_owners, 256, "sum_w_in_pool")
    g_ina = _sum_units(p_ina32, rb_ina, place, a_owners, 256, "sum_w_in_a")
    g_pw = _sum_units(p_pw32, rb_pw, place, SHARD_OWNERS, 128, "sum_pool_w")
    p_inb32, p_inb16 = _grad_w_in(h, drec_b, rec_blocks, "grad_w_in_b")

    dproj = ([(dpool, 0, 0), (dpool, 0, 1), (dpool, 1, 0), (dpool, 1, 1)]
             + [(d, n, 0) for n in range(4) for d in (drec_a, drec_b)])
    (dx, part_x), ((rb_inb,),) = _grad_x(dproj, w_in_g, x2, norm1_g, dout, [_ex_send([p_inb16], [b_owners])])
    g_inb = _sum_units(p_inb32, rb_inb, place, b_owners, 256, "sum_w_in_b")
    zero = jnp.zeros((1, D_MODEL), F32)
    part_rec = jnp.concatenate([part_a, part_b], axis=1)
    parts = jnp.concatenate([part_x[0:1], dscale, part_rec[1:2], zero, part_rec[0:1], part_out[0:1],
                             part_out[1:2], zero], axis=0)
    _, ((g_out, g_pw, g_inp, g_ina, g_inb), (gathered,)) = _call(
        None, name="join_halves",
        exchanges=[_ex_join([g_out, g_pw, g_inp, g_ina, g_inb],
                            [SHARD_OWNERS, SHARD_OWNERS, pool_owners, a_owners, b_owners]),
                   _ex_gather_small(parts)])

    group_of = np.zeros((D_PROJ // COL_BLK,), np.int32)
    index_of = np.zeros((D_PROJ // COL_BLK,), np.int32)
    for gi, blocks in enumerate(BLOCK_GROUPS):
        for i, j in enumerate(blocks):
            group_of[j], index_of[j] = gi, i
    per_shard = W_IN_SHARD // COL_BLK
    pick_in = jnp.stack([lax.dynamic_slice(jnp.asarray(group_of), (per_shard * chip,), (per_shard,)),
                         lax.dynamic_slice(jnp.asarray(index_of), (per_shard * chip,), (per_shard,))])
    pick_own = jnp.stack([jnp.zeros((1,), jnp.int32), chip.reshape(1).astype(jnp.int32)])
    big = [_adamw_units(w_in[0], m_w_in[0], v_w_in[0], [g_inp, g_ina, g_inb], pick_in, "adamw_w_in"),
           _adamw_units(w_out[0], m_w_out[0], v_w_out[0], [g_out], pick_own, "adamw_w_out"),
           _adamw_units(flat_pw(pool_w), flat_pw(m_pool_w), flat_pw(v_pool_w), [g_pw], pick_own, "adamw_pool_w")]

    small_w = _small_rows(norm1_g, pool_scale, lb_logits, rec_norm_g, final_norm_g)
    small_m = _small_rows(m_norm1_g, m_pool_scale, m_lb_logits, m_rec_norm_g, m_final_norm_g)
    small_v = _small_rows(v_norm1_g, v_pool_scale, v_lb_logits, v_rec_norm_g, v_final_norm_g)
    loss, *small = _small_update(parts, gathered, small_w, small_m, small_v)

    def leaves(k):
        s = small[k]
        return (s[ROW_NORM1:ROW_NORM1 + 1], big[0][k][None], big[2][k].reshape(pool_w.shape),
                s[ROW_SCALE:ROW_SCALE + 1], s[ROW_LB:ROW_LB + 2], s[ROW_REC:ROW_REC + 1],
                big[1][k][None], s[ROW_FINAL])

    return (loss.reshape(()), dx[None], *leaves(0), *leaves(1), *leaves(2), *leaves(3))
```

```python
import functools

import numpy as np
import jax
import jax.numpy as jnp
from jax import lax
from jax.experimental import pallas as pl
from jax.experimental.pallas import tpu as pltpu

F32 = jnp.float32
BF16 = jnp.bfloat16

SEQ = 2048
D_MODEL = 1024
D_PROJ = 6144
N_SEC = 6
N_GROUPS = 4
PG = 256
N_HEADS = 8
HEAD = 128
CHUNK = 64
N_LEVELS = 6
N_SHARDS = 4
W_IN_SHARD = D_PROJ // N_SHARDS
W_OUT_SHARD = 2048 // N_SHARDS
PW_SHARD = PG // N_SHARDS
COL_BLK = 512
EPS = 1e-6

ADAM_LR = 0.001
ADAM_B1 = 0.9
ADAM_B2 = 0.999
ADAM_EPS = 1e-08
ADAM_WD = 0.01
ADAM_STEP = 10

V7X_VMEM_LIMIT = 56 * 1024 * 1024
MESH = pl.DeviceIdType.MESH


def _params(**kw):
    return pltpu.CompilerParams(vmem_limit_bytes=V7X_VMEM_LIMIT, **kw)


def _sig(x):
    return 1.0 / (1.0 + jnp.exp(-x))


def _dot(a, b):
    return jnp.dot(a, b, preferred_element_type=F32)


def _dot_nt(a, b):
    return lax.dot_general(a, b, (((1,), (1,)), ((), ())), preferred_element_type=F32)


def _dot_tn(a, b):
    return lax.dot_general(a, b, (((0,), (0,)), ((), ())), preferred_element_type=F32)


def _split3(a):
    p1 = a.astype(BF16)
    r1 = a - p1.astype(F32)
    p2 = r1.astype(BF16)
    p3 = (r1 - p2.astype(F32)).astype(BF16)
    return jnp.concatenate([p1, p2, p3], axis=-1)


def _dot3(w01, a):
    n = a.shape[-1]
    r = _dot(w01, _split3(a))
    return r[:, :n] + r[:, n:2 * n] + r[:, 2 * n:]


def _chunk_constants():
    j = np.arange(CHUNK)
    tt, ss = np.meshgrid(j, j, indexing="ij")
    x = tt ^ ss
    hb = np.full((CHUNK, CHUNK), -1, np.int32)
    for l in range(N_LEVELS):
        hb[x >= (1 << l)] = l
    sym = np.stack([(hb == l) for l in range(N_LEVELS)]).astype(np.float32)
    low = sym * (tt > ss)
    sign = np.stack([np.where((j >> l) & 1, 1.0, -1.0) for l in range(N_LEVELS)]).astype(np.float32)
    sign = np.ascontiguousarray(np.broadcast_to(sign[:, :, None], (N_LEVELS, CHUNK, HEAD)))
    tri = (ss <= tt).astype(np.float32)
    return dict(tri=tri, tri_t=np.ascontiguousarray(tri.T), low=low,
                low_t=np.ascontiguousarray(low.transpose(0, 2, 1)), sym=sym, sign=sign)


def _in_proj(x, g1, w_slots, place):
    n_col = D_PROJ // COL_BLK
    per_shard = W_IN_SHARD // COL_BLK
    rows = 1024
    half_rows = D_MODEL // 2
    quarter_rows = D_MODEL // 4
    FLIP_X, FLIP_Y, FLIP_XY = 2, 1, 3

    def shard_at(m, chip):
        return chip ^ jnp.where(m == 0, 0, jnp.where(m == 1, FLIP_X, jnp.where(m == 2, FLIP_Y, FLIP_XY)))

    def body(place_ref, x_ref, g_ref, w_in_ref, proj_ref, h_ref, w_ref, wbuf, load_sems, send_sems, recv_sems):
        t = pl.program_id(0)
        x_, y_, c = _place()
        chip = 2 * x_ + y_
        me, other_core = (x_, y_, c), (x_, y_, 1 - c)
        x_nbr, y_nbr = (1 - x_, y_, c), (x_, 1 - y_, c)
        piece = lambda shard, half: _half_rows(w_ref.at[shard], half)

        def quarter(shard, q):
            r = pl.ds(pl.multiple_of(c * half_rows + q * quarter_rows, quarter_rows), quarter_rows)
            return w_ref.at[shard, r, :]

        def copy(k, ref, to):
            return _remote(ref, ref, send_sems, recv_sems, k, to)

        direct = lambda: [copy(0, piece(chip, c), x_nbr), copy(1, piece(chip, c), y_nbr)]
        relays = lambda: [copy(2, quarter(chip ^ FLIP_X, 0), y_nbr), copy(3, quarter(chip ^ FLIP_Y, 1), x_nbr)]
        passed_on = lambda m, half, to: copy(3 + m, piece(shard_at(m, chip), half), to)

        def load(shard, p):
            return pltpu.make_async_copy(w_ref.at[shard, :, pl.ds(p * COL_BLK, COL_BLK)], wbuf.at[p],
                                         load_sems.at[p])

        @pl.when(t == 0)
        def _():
            for cp in direct():
                cp.start()

            def norm(i, _):
                r = pl.ds(pl.multiple_of(i * rows, rows), rows)
                xv = x_ref[r, :]
                inv = lax.rsqrt(jnp.mean(xv * xv, axis=-1, keepdims=True) + EPS)
                h_ref[r, :] = (xv * inv * g_ref[...]).astype(BF16)
                return 0
            lax.fori_loop(0, SEQ // rows, norm, 0)

        @pl.when(t == per_shard)
        def _():
            copy(0, piece(chip ^ FLIP_X, c), me).wait_recv()
            copy(1, piece(chip ^ FLIP_Y, c), me).wait_recv()
            for cp in relays():
                cp.start()
            passed_on(1, c, other_core).start()
            passed_on(2, c, other_core).start()
            passed_on(1, 1 - c, me).wait_recv()

        @pl.when(t == 2 * per_shard)
        def _():
            passed_on(2, 1 - c, me).wait_recv()

        @pl.when(t == 3 * per_shard)
        def _():
            copy(2, quarter(chip ^ FLIP_XY, 0), me).wait_recv()
            copy(3, quarter(chip ^ FLIP_XY, 1), me).wait_recv()
            passed_on(3, c, other_core).start()
            passed_on(3, 1 - c, me).wait_recv()

        for m in range(N_SHARDS):
            @pl.when(t == per_shard * m)
            def _(m=m):
                for p in range(per_shard):
                    load(shard_at(m, chip), p).start()

        for p in range(per_shard):
            @pl.when(t % per_shard == p)
            def _(p=p):
                load(0, p).wait()

                def mm(i, _):
                    r = pl.ds(pl.multiple_of(i * rows, rows), rows)
                    proj_ref[r, :] = _dot(h_ref[r, :], wbuf[p])
                    return 0
                lax.fori_loop(0, SEQ // rows, mm, 0)

        @pl.when(t == n_col - 1)
        def _():
            for cp in direct() + relays() + [passed_on(m, c, other_core) for m in (1, 2, 3)]:
                cp.wait_send()

    return pl.pallas_call(
        body, name="in_proj",
        grid_spec=pltpu.PrefetchScalarGridSpec(
            num_scalar_prefetch=1, grid=(n_col,),
            in_specs=[pl.BlockSpec((SEQ, D_MODEL), lambda t, p: (0, 0)),
                      pl.BlockSpec((1, D_MODEL), lambda t, p: (0, 0)),
                      pl.BlockSpec(memory_space=pl.ANY)],
            out_specs=[pl.BlockSpec((SEQ, COL_BLK),
                                    lambda t, p: (0, per_shard * shard_at(t // per_shard, p[1]) + t % per_shard)),
                       pl.BlockSpec((SEQ, D_MODEL), lambda t, p: (0, 0)),
                       pl.BlockSpec(memory_space=pl.ANY)],
            scratch_shapes=[pltpu.VMEM((per_shard, D_MODEL, COL_BLK), BF16),
                            pltpu.SemaphoreType.DMA((per_shard,)),
                            pltpu.SemaphoreType.DMA((7,)), pltpu.SemaphoreType.DMA((7,))]),
        out_shape=[jax.ShapeDtypeStruct((SEQ, D_PROJ), F32),
                   jax.ShapeDtypeStruct((SEQ, D_MODEL), BF16),
                   jax.ShapeDtypeStruct(w_slots.shape, BF16)],
        input_output_aliases={3: 2},
        compiler_params=_params(dimension_semantics=("arbitrary",)),
    )(place, x, g1, w_slots)


POOL_ROWS = 256
POOL_HALO = 16
POOL_PAIR = 2


def _window_sums(ext, g, shift_of):
    s = ext
    for k in range(N_GROUPS):
        s = jnp.where(k <= g, s + pltpu.roll(s, shift_of(k), 0), s)
    return s


def _pool_diff(u_ref, i, g):
    n = POOL_ROWS + POOL_HALO
    r0 = i * POOL_ROWS
    cur = u_ref[pl.ds(pl.multiple_of(r0, POOL_ROWS), POOL_ROWS), :]
    before = u_ref[pl.ds(pl.multiple_of(jnp.maximum(r0 - POOL_HALO, 0), 8), POOL_HALO), :]
    before = jnp.where(i > 0, before, 0.0)
    ext = jnp.concatenate([before, cur], axis=0)
    s = _window_sums(ext, g, lambda k: 1 << k)[POOL_HALO:, :]
    t = r0 + lax.broadcasted_iota(jnp.int32, (POOL_ROWS, 1), 0)
    width = (2 << g).astype(F32)
    inv_count = 1.0 / jnp.minimum((t + 1).astype(F32), width)
    return s * inv_count - cur, inv_count


def _pool_fwd(proj, pw_g, pool_scale):
    def body(u_ref, gate_ref, pw_ref, sc_ref, y_ref):
        g = pl.program_id(0)

        def step(ii, _):
            chunks = [POOL_PAIR * ii + a for a in range(POOL_PAIR)]
            ds = [_pool_diff(u_ref, i, g)[0].astype(BF16) for i in chunks]
            mixed = [_dot(d, pw_ref[...]) for d in ds]
            for i, m in zip(chunks, mixed):
                r = pl.ds(pl.multiple_of(i * POOL_ROWS, POOL_ROWS), POOL_ROWS)
                gate = gate_ref[r, :]
                y_ref[r, :] = (m * sc_ref[...] * (gate * _sig(gate))).astype(BF16)
            return 0
        lax.fori_loop(0, SEQ // POOL_ROWS // POOL_PAIR, step, 0)

    return pl.pallas_call(
        body, name="pool_fwd", grid=(N_GROUPS,),
        in_specs=[pl.BlockSpec((SEQ, PG), lambda g: (0, g)),
                  pl.BlockSpec((SEQ, PG), lambda g: (0, N_GROUPS + g)),
                  pl.BlockSpec((None, PG, PG), lambda g: (g, 0, 0)),
                  pl.BlockSpec((1, PG), lambda g: (0, g))],
        out_specs=pl.BlockSpec((SEQ, PG), lambda g: (0, g)),
        out_shape=jax.ShapeDtypeStruct((SEQ, D_MODEL), BF16),
        compiler_params=_params(dimension_semantics=("arbitrary",)),
    )(proj, proj, pw_g, pool_scale)


REC_ROWS = 1024
REC_CHUNKS = REC_ROWS // CHUNK
N_REC_BLK = SEQ // REC_ROWS
REC_GROUP = REC_CHUNKS
REC_GROUP_BWD = REC_CHUNKS
SEC_BLK = D_MODEL // HEAD


def _lower_bound(lb_ref):
    l0 = lb_ref[0:1, :]
    l1 = lb_ref[1:2, :]
    mx = jnp.maximum(l0, l1)
    e0 = jnp.exp(l0 - mx)
    e1 = jnp.exp(l1 - mx)
    return e0 / (e0 + e1)


def _gates(q, fl, lb):
    qs = q * _sig(q)
    sf = _sig(fl)
    f = lb + (1.0 - lb) * sf
    return qs, sf, f, 1.0 - f, jnp.log(f)


LOG2E = 1.4426950408889634


def _level_factors(g2, qs, k, sign_ref):
    t = lax.broadcasted_iota(jnp.int32, (CHUNK, HEAD), 0)
    row = lambda r, n: jnp.broadcast_to(g2[r:r + 1, :], (n, HEAD))
    out = []
    for l in range(N_LEVELS):
        m = 1 << l
        if l == 0:
            g_mid = jnp.where((t & 1) == 1, pltpu.roll(g2, 1, 0), g2)
        elif l == 1:
            low = (t & 7) < 4
            g_mid = jnp.concatenate([jnp.where(low[:8], row(8 * i + 1, 8), row(8 * i + 5, 8))
                                     for i in range(CHUNK // 8)], axis=0)
        else:
            g_mid = jnp.concatenate([row(b * 2 * m + m - 1, 2 * m) for b in range(CHUNK // (2 * m))], axis=0)
        sgn = sign_ref[l]
        up = sgn > 0.0
        e = jnp.exp2((g2 - g_mid) * sgn)
        x = jnp.where(up, qs, k) * e
        hi = x.astype(BF16)
        out.append((hi, (x - hi.astype(F32)).astype(BF16), e, up))
    return out


CHIP_FLIPS = ((1, 0), (0, 1), (1, 1))
HBM = pl.BlockSpec(memory_space=pl.ANY)


def _place():
    return lax.axis_index("x"), lax.axis_index("y"), lax.axis_index("c")


def _remote(src, dst, send_sems, recv_sems, k, to):
    return pltpu.make_async_remote_copy(src_ref=src, dst_ref=dst, send_sem=send_sems.at[k],
                                        recv_sem=recv_sems.at[k], device_id=to, device_id_type=MESH)


def _half_rows(ref, c):
    half = ref.shape[-2] // 2
    rows = pl.ds(pl.multiple_of(c * half, half), half)
    return ref.at[:, rows, :] if len(ref.shape) == 3 else ref.at[rows, :]


class _Exchange:
    def __init__(self, inputs, out_shapes, n_sems, start, finish, aliases=None):
        self.inputs, self.out_shapes, self.n_sems = list(inputs), list(out_shapes), n_sems
        self.start, self.finish, self.aliases = start, finish, dict(aliases or {})


def _ex_swap(grads):
    def copies(ins, outs, send, recv):
        x, y, c = _place()
        return [_remote(_half_rows(g, 1 - c), o, send, recv, t, (x, y, 1 - c))
                for t, (g, o) in enumerate(zip(ins, outs))]

    def start(*refs):
        for cp in copies(*refs):
            cp.start()

    def finish(*refs):
        cps = copies(*refs)
        for cp in cps:
            cp.wait_recv()
        for cp in cps:
            cp.wait_send()

    shapes = [jax.ShapeDtypeStruct((a.shape[0], a.shape[1] // 2, a.shape[2]), F32) for a in grads]
    return _Exchange(grads, shapes, len(grads), start, finish)


def _ex_send(parts16, owners, units=None, landed=None):
    n_t = len(parts16)
    units = units or [tuple(range(len(o))) for o in owners]
    landed = landed or [None] * n_t
    given = [t for t in range(n_t) if landed[t] is not None]

    def each(ins, outs, send, recv, to_sender, to_owner):
        x, y, c = _place()
        k = 0
        for t, own in enumerate(owners):
            for j in units[t]:
                for r, (fx, fy) in enumerate(CHIP_FLIPS):
                    tx, ty = x ^ fx, y ^ fy
                    cp = _remote(ins[t].at[j], outs[t].at[j, r], send, recv, k, (tx, ty, c))
                    if to_sender is not None:
                        pl.when(2 * tx + ty == own[j])(functools.partial(to_sender, cp))
                    if to_owner is not None:
                        pl.when(2 * x + y == own[j])(functools.partial(to_owner, cp))
                    k += 1

    def start(*refs):
        each(*refs, lambda cp: cp.start(), None)

    def finish(*refs):
        each(*refs, None, lambda cp: cp.wait_recv())
        each(*refs, lambda cp: cp.wait_send(), None)

    shapes = [jax.ShapeDtypeStruct((a.shape[0], len(CHIP_FLIPS)) + a.shape[1:], BF16) for a in parts16]
    return _Exchange(list(parts16) + [landed[t] for t in given], shapes,
                     len(CHIP_FLIPS) * sum(len(u) for u in units), start, finish,
                     aliases={n_t + i: t for i, t in enumerate(given)})


def _ex_join(units, owners):
    def each(ins, outs, send, recv, fn):
        x, y, c = _place()
        k = 0
        for t, own in enumerate(owners):
            for j, o in enumerate(own):
                def half(cc, to, u=outs[t].at[j], k=k):
                    return _remote(_half_rows(u, cc), _half_rows(u, cc), send, recv, k, to)
                mine = functools.partial(half, c, (x, y, 1 - c))
                theirs = functools.partial(half, 1 - c, (x, y, c))
                pl.when(2 * x + y == o)(functools.partial(fn, mine, theirs))
                k += 1

    def start(*refs):
        each(*refs, lambda mine, theirs: mine().start())

    def finish(*refs):
        each(*refs, lambda mine, theirs: theirs().wait_recv())
        each(*refs, lambda mine, theirs: mine().wait_send())

    shapes = [jax.ShapeDtypeStruct(a.shape, F32) for a in units]
    return _Exchange(units, shapes, sum(len(o) for o in owners), start, finish,
                     aliases={t: t for t in range(len(units))})


def _ex_gather(slots):
    n_t = len(slots)
    n_fl = len(CHIP_FLIPS)

    def piece(ref, shard, half):
        return _half_rows(ref.at[shard], half)

    def first(outs, send, recv):
        x, y, c = _place()
        s = 2 * x + y
        return [_remote(piece(outs[t], s, c), piece(outs[t], s, c), send, recv, n_t * j + t, (x ^ fx, y ^ fy, c))
                for j, (fx, fy) in enumerate(CHIP_FLIPS) for t in range(n_t)]

    def start(ins, outs, send, recv):
        for cp in first(outs, send, recv):
            cp.start()

    def finish(ins, outs, send, recv):
        x, y, c = _place()
        passed = []
        for j, (fx, fy) in enumerate(CHIP_FLIPS):
            sj = 2 * (x ^ fx) + (y ^ fy)
            for t in range(n_t):
                k = n_t * j + t
                _remote(piece(outs[t], sj, c), piece(outs[t], sj, c), send, recv, k, (x, y, c)).wait_recv()
                cp = _remote(piece(outs[t], sj, c), piece(outs[t], sj, c), send, recv, n_t * n_fl + k, (x, y, 1 - c))
                cp.start()
                passed.append(cp)
        for j, (fx, fy) in enumerate(CHIP_FLIPS):
            sj = 2 * (x ^ fx) + (y ^ fy)
            for t in range(n_t):
                k = n_t * n_fl + n_t * j + t
                _remote(piece(outs[t], sj, 1 - c), piece(outs[t], sj, 1 - c), send, recv, k, (x, y, c)).wait_recv()
        for cp in first(outs, send, recv) + passed:
            cp.wait_send()

    shapes = [jax.ShapeDtypeStruct(a.shape, BF16) for a in slots]
    return _Exchange(slots, shapes, 2 * n_t * n_fl, start, finish, aliases={t: t for t in range(n_t)})


def _ex_gather_small(parts):
    def copies(ins, outs, send, recv):
        x, y, c = _place()
        me = 4 * x + 2 * y + c
        return [_remote(ins[0], outs[0].at[me], send, recv, mask - 1,
                        (x ^ (mask >> 2), y ^ ((mask >> 1) & 1), c ^ (mask & 1))) for mask in range(1, 8)]

    def start(*refs):
        for cp in copies(*refs):
            cp.start()

    def finish(ins, outs, send, recv):
        x, y, c = _place()
        me = 4 * x + 2 * y + c
        for mask in range(1, 8):
            _remote(ins[0], outs[0].at[me ^ mask], send, recv, mask - 1, (x, y, c)).wait_recv()
        for cp in copies(ins, outs, send, recv):
            cp.wait_send()

    return _Exchange([parts], [jax.ShapeDtypeStruct((8,) + parts.shape, F32)], 7, start, finish)


def _call(body, *, name, args=(), in_specs=(), out_specs=(), out_shape=(), grid=(), scratch_shapes=(),
          exchanges=()):
    n_in, n_out, n_scr = len(args), len(out_shape), len(scratch_shapes)
    ex_in, ex_out, ex_scr, spans, alias = [], [], [], [], {}
    for ex in exchanges:
        spans.append((len(ex_in), len(ex.inputs), len(ex_out), len(ex.out_shapes)))
        for i, o in ex.aliases.items():
            alias[n_in + len(ex_in) + i] = n_out + len(ex_out) + o
        ex_in += ex.inputs
        ex_out += ex.out_shapes
        ex_scr += [pltpu.SemaphoreType.DMA((ex.n_sems,)), pltpu.SemaphoreType.DMA((ex.n_sems,))]

    def full(*refs):
        ins, x_in = refs[:n_in], refs[n_in:n_in + len(ex_in)]
        outs = refs[n_in + len(ex_in):n_in + len(ex_in) + n_out]
        x_out = refs[n_in + len(ex_in) + n_out:n_in + len(ex_in) + n_out + len(ex_out)]
        scr = refs[len(refs) - n_scr - len(ex_scr):len(refs) - len(ex_scr)]
        sems = refs[len(refs) - len(ex_scr):]

        def run(which):
            for e, (ex, (i0, ni, o0, no)) in enumerate(zip(exchanges, spans)):
                getattr(ex, which)(x_in[i0:i0 + ni], x_out[o0:o0 + no], sems[2 * e], sems[2 * e + 1])

        if grid:
            ids = [pl.program_id(a) for a in range(len(grid))]
            is_first = functools.reduce(jnp.logical_and, [i == 0 for i in ids])
            is_last = functools.reduce(jnp.logical_and, [i == g - 1 for i, g in zip(ids, grid)])
            pl.when(is_first)(lambda: run("start"))
            body(*ins, *outs, *scr)
            pl.when(is_last)(lambda: run("finish"))
        else:
            run("start")
            if body is not None:
                body(*ins, *outs, *scr)
            run("finish")

    kw = dict(grid=grid) if grid else {}
    if grid:
        kw["compiler_params"] = _params(dimension_semantics=("arbitrary",) * len(grid))
    else:
        kw["compiler_params"] = _params()
    res = pl.pallas_call(
        full, name=name,
        in_specs=list(in_specs) + [HBM] * len(ex_in),
        out_specs=list(out_specs) + [HBM] * len(ex_out),
        out_shape=list(out_shape) + ex_out,
        scratch_shapes=list(scratch_shapes) + ex_scr,
        input_output_aliases=alias, **kw,
    )(*args, *ex_in)
    own = list(res[:n_out])
    per_ex = [list(res[n_out + o0:n_out + o0 + no]) for (_, _, o0, no) in spans]
    return own, per_ex


def _cast_own(w, place, name):
    rows, cols = w.shape
    tile = min(rows, 256)

    def body(place_ref, w_ref, o_ref):
        o_ref[...] = w_ref[...].astype(BF16)

    return pl.pallas_call(
        body, name=name,
        grid_spec=pltpu.PrefetchScalarGridSpec(
            num_scalar_prefetch=1, grid=(rows // tile,),
            in_specs=[pl.BlockSpec((tile, cols), lambda i, p: (i, 0))],
            out_specs=pl.BlockSpec((None, tile, cols), lambda i, p: (p[1], i, 0))),
        out_shape=jax.ShapeDtypeStruct((N_SHARDS, rows, cols), BF16),
        compiler_params=_params(dimension_semantics=("arbitrary",)),
    )(place, w)


def _rec_fwd(proj, lb_logits, rec_g, consts, exchanges):
    tri, low, sign = consts["tri"], consts["low"], consts["sign"]

    def body(q_ref, f_ref, i_ref, rg_ref, lb_ref, g_ref, w_ref, low_ref, sign_ref, y_ref, o_ref, stp_ref, st_ref):
        @pl.when(pl.program_id(1) == 0)
        def _():
            st_ref[...] = jnp.zeros_like(st_ref)
        lb = _lower_bound(lb_ref)
        st = st_ref[...]
        rows = lambda c: pl.ds(c * CHUNK, CHUNK)
        for c0 in range(0, REC_CHUNKS, REC_GROUP):
            group = range(c0, c0 + REC_GROUP)
            gated = [_gates(q_ref[rows(c), :], f_ref[rows(c), :], lb) for c in group]
            g2s = [_dot3(w_ref[...], g) * LOG2E for (_, _, _, _, g) in gated]
            xs = [[xl for xl, _, _, _ in _level_factors(g2, qs, k, sign_ref)]
                  for g2, (qs, _, _, k, _) in zip(g2s, gated)]
            a_s = []
            for x in xs:
                a = jnp.zeros((CHUNK, CHUNK), F32)
                for l, xl in enumerate(x):
                    a = a + _dot_nt(xl, xl) * low_ref[l]
                a_s.append(a.astype(BF16))
            vbs = [i_ref[rows(c), :].astype(BF16) for c in group]
            intra = [_dot(a, vb) for a, vb in zip(a_s, vbs)]
            kvs = [_dot_tn(vb, (k * jnp.exp2(g2[CHUNK - 1:CHUNK, :] - g2)).astype(BF16))
                   for vb, g2, (_, _, _, k, _) in zip(vbs, g2s, gated)]
            for i, c in enumerate(group):
                qs, _, _, k, _ = gated[i]
                g2 = g2s[i]
                stp_ref[c] = st
                v = i_ref[rows(c), :]
                rg = rg_ref[rows(c), :]
                o = (intra[i] + jnp.sum(qs * k, axis=-1, keepdims=True) * v
                     + _dot_nt((qs * jnp.exp2(g2)).astype(BF16), st.astype(BF16)))
                st = st * jnp.exp2(g2[CHUNK - 1:CHUNK, :]) + kvs[i]
                o_ref[rows(c), :] = o
                inv = lax.rsqrt(jnp.mean(o * o, axis=-1, keepdims=True) + EPS)
                y_ref[rows(c), :] = (o * inv * g_ref[...] * (rg * _sig(rg))).astype(BF16)
        st_ref[...] = st

    sec = lambda n: pl.BlockSpec((REC_ROWS, HEAD), lambda h, b: (b, n * SEC_BLK + h))
    vec = lambda rows: pl.BlockSpec((rows, HEAD), lambda h, b: (0, h))
    full = lambda a: pl.BlockSpec(a.shape, lambda h, b: (0,) * a.ndim)
    return _call(
        body, name="rec_fwd", grid=(N_HEADS, N_REC_BLK),
        args=(proj, proj, proj, proj, lb_logits, rec_g, tri, low, sign),
        in_specs=[sec(2), sec(3), sec(4), sec(5), vec(2), vec(1), full(tri), full(low), full(sign)],
        out_specs=[pl.BlockSpec((REC_ROWS, HEAD), lambda h, b: (b, h)),
                   pl.BlockSpec((REC_ROWS, HEAD), lambda h, b: (b, h)),
                   pl.BlockSpec((None, REC_CHUNKS, HEAD, HEAD), lambda h, b: (h, b, 0, 0))],
        out_shape=[jax.ShapeDtypeStruct((SEQ, D_MODEL), BF16),
                   jax.ShapeDtypeStruct((SEQ, D_MODEL), F32),
                   jax.ShapeDtypeStruct((N_HEADS, SEQ // CHUNK, HEAD, HEAD), F32)],
        scratch_shapes=[pltpu.VMEM((HEAD, HEAD), F32)],
        exchanges=exchanges)


OUT_ROWS = 256


def _out_proj_loss(y_pool, y_rec, w_out_g, x, target, gf):
    def body(yp_ref, yr_ref, w_ref, x_ref, t_ref, gf_ref, dout_ref, doutb_ref, part_ref):
        @pl.when(pl.program_id(0) == 0)
        def _():
            part_ref[...] = jnp.zeros_like(part_ref)
        out = (x_ref[...] + _dot(yp_ref[...], w_ref[0:D_MODEL, :])
               + _dot(yr_ref[...], w_ref[D_MODEL:2 * D_MODEL, :]))
        inv = lax.rsqrt(jnp.mean(out * out, axis=-1, keepdims=True) + EPS)
        gf_v = gf_ref[...]
        diff = out * inv * gf_v - t_ref[...]
        dyf = diff * (1.0 / D_MODEL)
        a = dyf * gf_v
        dout = inv * a - out * (inv * inv * inv) * jnp.mean(a * out, axis=-1, keepdims=True)
        dout_ref[...] = dout
        doutb_ref[...] = dout.astype(BF16)
        part_ref[0:1, :] += jnp.sum(dyf * out * inv, axis=0, keepdims=True)
        part_ref[1:2, :] += jnp.sum(diff * diff, axis=0, keepdims=True)

    row = lambda n: pl.BlockSpec((OUT_ROWS, n), lambda i: (i, 0))
    return pl.pallas_call(
        body, name="out_proj_loss", grid=(SEQ // OUT_ROWS,),
        in_specs=[row(D_MODEL), row(D_MODEL), pl.BlockSpec((2 * D_MODEL, D_MODEL), lambda i: (0, 0)),
                  row(D_MODEL), row(D_MODEL), pl.BlockSpec((1, D_MODEL), lambda i: (0, 0))],
        out_specs=[row(D_MODEL), row(D_MODEL), pl.BlockSpec((8, D_MODEL), lambda i: (0, 0))],
        out_shape=[jax.ShapeDtypeStruct((SEQ, D_MODEL), F32),
                   jax.ShapeDtypeStruct((SEQ, D_MODEL), BF16),
                   jax.ShapeDtypeStruct((8, D_MODEL), F32)],
        compiler_params=_params(dimension_semantics=("arbitrary",)),
    )(y_pool, y_rec, w_out_g, x, target, gf)


def _grad_w_out(y_pool, y_rec, dout_b):
    blk = W_OUT_SHARD // 2
    per = D_MODEL // blk
    n = 2 * per

    def body(yp_ref, yr_ref, d_ref, p32_ref, p16_ref, send_ref, recv_ref, send_sems, recv_sems):
        j = pl.program_id(0)
        x, y, c = _place()

        def copy(u):
            return _remote(send_ref.at[u], recv_ref.at[u], send_sems, recv_sems, u, (x, y, 1 - c))

        for i in range(n):
            @pl.when(j == i)
            def _(i=i):
                res = _dot_tn((yp_ref if i < per else yr_ref)[...], d_ref[...])

                @pl.when(i % 2 == c)
                def _():
                    p32_ref[i // 2] = res

                @pl.when(i % 2 != c)
                def _():
                    send_ref[i // 2] = res
                    copy(i // 2).start()

        @pl.when(j == n - 1)
        def _():
            for u in range(N_SHARDS):
                copy(u).wait_recv()
                tot = p32_ref[u] + recv_ref[u]
                p32_ref[u] = tot
                p16_ref[u] = tot.astype(BF16)
            for u in range(N_SHARDS):
                copy(u).wait_send()

    whole = pl.BlockSpec((N_SHARDS, blk, D_MODEL), lambda j: (0, 0, 0))
    return pl.pallas_call(
        body, name="grad_w_out", grid=(n,),
        in_specs=[pl.BlockSpec((SEQ, blk), lambda j: (0, jnp.minimum(j, per - 1))),
                  pl.BlockSpec((SEQ, blk), lambda j: (0, jnp.maximum(j - per, 0))),
                  pl.BlockSpec((SEQ, D_MODEL), lambda j: (0, 0))],
        out_specs=[whole, whole],
        out_shape=[jax.ShapeDtypeStruct((N_SHARDS, blk, D_MODEL), F32),
                   jax.ShapeDtypeStruct((N_SHARDS, blk, D_MODEL), BF16)],
        scratch_shapes=[pltpu.VMEM((N_SHARDS, blk, D_MODEL), F32), pltpu.VMEM((N_SHARDS, blk, D_MODEL), F32),
                        pltpu.SemaphoreType.DMA((N_SHARDS,)), pltpu.SemaphoreType.DMA((N_SHARDS,))],
        compiler_params=_params(dimension_semantics=("arbitrary",)),
    )(y_pool, y_rec, dout_b)


def _pool_bwd(proj, dout_b, w_out_g, pw_g, pool_scale, exchanges):
    n = POOL_ROWS + POOL_HALO

    def body(u_ref, gate_ref, d_ref, wo_ref, pw_ref, sc_ref,
             dp_ref, dpw_ref, dsc_ref, dd_ref, ddw_ref):
        g = pl.program_id(0)
        dpw_ref[...] = jnp.zeros_like(dpw_ref)
        dsc_ref[...] = jnp.zeros_like(dsc_ref)

        def first(ii, _):
            chunks = [POOL_PAIR * ii + a for a in range(POOL_PAIR)]
            rs = [pl.ds(pl.multiple_of(i * POOL_ROWS, POOL_ROWS), POOL_ROWS) for i in chunks]
            diffs = [_pool_diff(u_ref, i, g) for i in chunks]
            dbs = [d.astype(BF16) for d, _ in diffs]
            mixed = [_dot(db, pw_ref[...]) for db in dbs]
            dys = [_dot_nt(d_ref[r, :], wo_ref[...]) for r in rs]
            sc = sc_ref[...]
            dmbs = []
            for r, m, dy in zip(rs, mixed, dys):
                gate = gate_ref[r, :]
                sg = _sig(gate)
                silu = gate * sg
                dp_ref[1, r, :] = (dy * m * sc * (sg * (1.0 + gate * (1.0 - sg)))).astype(BF16)
                dsc_ref[...] += jnp.sum(dy * silu * m, axis=0, keepdims=True)
                dmbs.append((dy * silu * sc).astype(BF16))
            for db, dmb in zip(dbs, dmbs):
                dpw_ref[...] += _dot_tn(db, dmb)
            dds = [_dot_nt(dmb, pw_ref[...]) for dmb in dmbs]
            for r, dd, (_, inv_count) in zip(rs, dds, diffs):
                dd_ref[r, :] = dd
                ddw_ref[r, :] = dd * inv_count
            return 0
        lax.fori_loop(0, SEQ // POOL_ROWS // POOL_PAIR, first, 0)

        def second(i, _):
            r0 = i * POOL_ROWS
            r = pl.ds(pl.multiple_of(r0, POOL_ROWS), POOL_ROWS)
            last = i == SEQ // POOL_ROWS - 1
            after = ddw_ref[pl.ds(pl.multiple_of(jnp.minimum(r0 + POOL_ROWS, SEQ - POOL_HALO), 8), POOL_HALO), :]
            after = jnp.where(last, 0.0, after)
            ext = jnp.concatenate([ddw_ref[r, :], after], axis=0)
            s = _window_sums(ext, g, lambda k: n - (1 << k))[:POOL_ROWS, :]
            dp_ref[0, r, :] = (s - dd_ref[r, :]).astype(BF16)
            return 0
        lax.fori_loop(0, SEQ // POOL_ROWS, second, 0)

    return _call(
        body, name="pool_bwd", grid=(N_GROUPS,),
        args=(proj, proj, dout_b, w_out_g, pw_g, pool_scale),
        in_specs=[pl.BlockSpec((SEQ, PG), lambda g: (0, g)),
                  pl.BlockSpec((SEQ, PG), lambda g: (0, N_GROUPS + g)),
                  pl.BlockSpec((SEQ, D_MODEL), lambda g: (0, 0)),
                  pl.BlockSpec((PG, D_MODEL), lambda g: (g, 0)),
                  pl.BlockSpec((None, PG, PG), lambda g: (g, 0, 0)),
                  pl.BlockSpec((1, PG), lambda g: (0, g))],
        out_specs=[pl.BlockSpec((2, SEQ, PG), lambda g: (0, 0, g)),
                   pl.BlockSpec((None, PG, PG), lambda g: (g, 0, 0)),
                   pl.BlockSpec((1, PG), lambda g: (0, g))],
        out_shape=[jax.ShapeDtypeStruct((2, SEQ, D_MODEL), BF16),
                   jax.ShapeDtypeStruct((N_GROUPS, PG, PG), F32),
                   jax.ShapeDtypeStruct((1, D_MODEL), F32)],
        scratch_shapes=[pltpu.VMEM((SEQ, PG), F32), pltpu.VMEM((SEQ, PG), F32)],
        exchanges=exchanges)


HALF_HEADS = N_HEADS // 2
HALF_COLS = HALF_HEADS * HEAD


def _rec_bwd(proj, o_raw, st_prev, dout_b, w_out_g, lb_logits, rec_g, consts, h0, name, exchanges):
    def body(q_ref, f_ref, i_ref, rg_ref, o_ref, stp_ref, d_ref, wo_ref, lb_ref, g_ref,
             w_ref, lowt_ref, sym_ref, sign_ref, tri_ref,
             dr_ref, part_ref, dst_ref):
        @pl.when(pl.program_id(1) == 0)
        def _():
            dst_ref[...] = jnp.zeros_like(dst_ref)
            part_ref[...] = jnp.zeros_like(part_ref)
        tril = (lax.broadcasted_iota(jnp.int32, (CHUNK, CHUNK), 0)
                > lax.broadcasted_iota(jnp.int32, (CHUNK, CHUNK), 1))
        lb = _lower_bound(lb_ref)
        grec = g_ref[...]
        dst = dst_ref[...]
        acc_grec = jnp.zeros((1, HEAD), F32)
        acc_lb = jnp.zeros((1, HEAD), F32)
        rows = lambda c: pl.ds(c * CHUNK, CHUNK)
        for c0 in reversed(range(0, REC_CHUNKS, REC_GROUP_BWD)):
            group = list(reversed(range(c0, c0 + REC_GROUP_BWD)))
            dys = [_dot_nt(d_ref[rows(c), :], wo_ref[...]) for c in group]
            dos = []
            for c, dy in zip(group, dys):
                rg = rg_ref[rows(c), :]
                o = o_ref[rows(c), :]
                sg = _sig(rg)
                silu = rg * sg
                inv = lax.rsqrt(jnp.mean(o * o, axis=-1, keepdims=True) + EPS)
                recn = o * inv
                dr_ref[3, rows(c), :] = (dy * recn * grec * (sg * (1.0 + rg * (1.0 - sg)))).astype(BF16)
                acc_grec = acc_grec + jnp.sum(dy * silu * recn, axis=0, keepdims=True)
                drecn = dy * silu * grec
                dos.append(inv * drecn - o * (inv * inv * inv) * jnp.mean(drecn * o, axis=-1, keepdims=True))
            gated = [_gates(q_ref[rows(c), :], f_ref[rows(c), :], lb) for c in group]
            g2s = [_dot3(w_ref[...], g) * LOG2E for (_, _, _, _, g) in gated]
            levels = [_level_factors(g2, qs, k, sign_ref) for g2, (qs, _, _, k, _) in zip(g2s, gated)]
            a_ts = []
            for lev in levels:
                a_t = jnp.zeros((CHUNK, CHUNK), F32)
                for l, (xl, _, _, _) in enumerate(lev):
                    a_t = a_t + _dot_nt(xl, xl) * lowt_ref[l]
                a_ts.append(a_t.astype(BF16))
            dobs = [do.astype(BF16) for do in dos]
            vbs = [i_ref[rows(c), :].astype(BF16) for c in group]
            d_syms = [jnp.where(tril, _dot_nt(dob, vb), _dot_nt(vb, dob)) for dob, vb in zip(dobs, vbs)]
            dqs_is, dk_is = [], []
            for lev, d_sym in zip(levels, d_syms):
                dqs_i = jnp.zeros((CHUNK, HEAD), F32)
                both_i = jnp.zeros((CHUNK, HEAD), F32)
                for l, (xl, xlo, e, up) in enumerate(lev):
                    z = d_sym * sym_ref[l]
                    tmp = _dot(z.astype(BF16), jnp.concatenate([xl, xlo], axis=-1))
                    tmp = (tmp[:, :HEAD] + tmp[:, HEAD:]) * e
                    dqs_i = dqs_i + jnp.where(up, tmp, 0.0)
                    both_i = both_i + tmp
                dqs_is.append(dqs_i)
                dk_is.append(both_i - dqs_i)
            e_gs = [jnp.exp2(g2) for g2 in g2s]
            e_revs = [jnp.exp2(g2[CHUNK - 1:CHUNK, :] - g2) for g2 in g2s]
            e_lasts = [jnp.exp2(g2[CHUNK - 1:CHUNK, :]) for g2 in g2s]
            q_gs = [qs * e_g for (qs, _, _, _, _), e_g in zip(gated, e_gs)]
            kdecs = [k * e_rev for (_, _, _, k, _), e_rev in zip(gated, e_revs)]
            dv12 = [_dot(a_t, dob) + jnp.sum(qs * k, axis=-1, keepdims=True) * do
                    for a_t, dob, do, (qs, _, _, k, _) in zip(a_ts, dobs, dos, gated)]
            dq_gs = [_dot(dob, stp_ref[c].astype(BF16)) for c, dob in zip(group, dobs)]
            steps = [_dot_tn(dob, q_g.astype(BF16)) for dob, q_g in zip(dobs, q_gs)]
            dsts = []
            for e_last, step in zip(e_lasts, steps):
                dsts.append(dst)
                dst = dst * e_last + step
            dstbs = [d.astype(BF16) for d in dsts]
            dv3 = [_dot_nt(kdec.astype(BF16), dstb) for kdec, dstb in zip(kdecs, dstbs)]
            dkdecs = [_dot(vb, dstb) for vb, dstb in zip(vbs, dstbs)]
            dbig_gs, dg_lasts, dqss, dks = [], [], [], []
            for i, c in enumerate(group):
                qs, _, _, k, _ = gated[i]
                de_last = jnp.sum(stp_ref[c] * dsts[i], axis=0, keepdims=True)
                ddiag = jnp.sum(dos[i] * i_ref[rows(c), :], axis=-1, keepdims=True)
                dqss.append(dqs_is[i] + ddiag * k + dq_gs[i] * e_gs[i])
                dks.append(dk_is[i] + ddiag * qs + dkdecs[i] * e_revs[i])
                dg_rev = dkdecs[i] * kdecs[i]
                dg_lasts.append(jnp.sum(dg_rev, axis=0, keepdims=True) + de_last * e_lasts[i])
                dbig_gs.append(qs * dqs_is[i] - k * dk_is[i] + dq_gs[i] * q_gs[i] - dg_rev)
            dgs = [_dot3(tri_ref[...], dbig_g) + dg_last for dbig_g, dg_last in zip(dbig_gs, dg_lasts)]
            for i, c in enumerate(group):
                _, sf, f, _, _ = gated[i]
                q = q_ref[rows(c), :]
                df = dgs[i] / f - dks[i]
                dr_ref[1, rows(c), :] = (df * (1.0 - lb) * sf * (1.0 - sf)).astype(BF16)
                acc_lb = acc_lb + jnp.sum(df * (1.0 - sf), axis=0, keepdims=True)
                sq = _sig(q)
                dr_ref[0, rows(c), :] = (dqss[i] * (sq * (1.0 + q * (1.0 - sq)))).astype(BF16)
                dr_ref[2, rows(c), :] = (dv12[i] + dv3[i]).astype(BF16)
        dst_ref[...] = dst
        part_ref[0:1, :] += acc_grec
        part_ref[1:2, :] += acc_lb

    rev = lambda b: N_REC_BLK - 1 - b
    sec = lambda n: pl.BlockSpec((REC_ROWS, HEAD), lambda h, b: (rev(b), n * SEC_BLK + h0 + h))
    col_in = pl.BlockSpec((REC_ROWS, HEAD), lambda h, b: (rev(b), h0 + h))
    vec_in = lambda rows: pl.BlockSpec((rows, HEAD), lambda h, b: (0, h0 + h))
    full = lambda a: pl.BlockSpec(a.shape, lambda h, b: (0,) * a.ndim)
    return _call(
        body, name=name, grid=(HALF_HEADS, N_REC_BLK),
        args=(proj, proj, proj, proj, o_raw, st_prev, dout_b, w_out_g, lb_logits, rec_g,
              consts["tri"], consts["low_t"], consts["sym"], consts["sign"], consts["tri_t"]),
        in_specs=[sec(2), sec(3), sec(4), sec(5), col_in,
                  pl.BlockSpec((None, REC_CHUNKS, HEAD, HEAD), lambda h, b: (h0 + h, rev(b), 0, 0)),
                  pl.BlockSpec((REC_ROWS, D_MODEL), lambda h, b: (rev(b), 0)),
                  pl.BlockSpec((HEAD, D_MODEL), lambda h, b: (SEC_BLK + h0 + h, 0)),
                  vec_in(2), vec_in(1)] + [full(consts[n]) for n in ("tri", "low_t", "sym", "sign", "tri_t")],
        out_specs=[pl.BlockSpec((4, REC_ROWS, HEAD), lambda h, b: (0, rev(b), h)),
                   pl.BlockSpec((8, HEAD), lambda h, b: (0, h))],
        out_shape=[jax.ShapeDtypeStruct((4, SEQ, HALF_COLS), BF16),
                   jax.ShapeDtypeStruct((8, HALF_COLS), F32)],
        scratch_shapes=[pltpu.VMEM((HEAD, HEAD), F32)],
        exchanges=exchanges)


def _w_in_block(w_ref, j):
    per_shard = W_IN_SHARD // COL_BLK
    return w_ref[j // per_shard, :, (j % per_shard) * COL_BLK:(j % per_shard + 1) * COL_BLK]


def _grad_x(dproj, w_in_g, x, g1, dout, exchanges):
    rows = 256
    n_blk = len(dproj)

    def body(*refs):
        dp_refs = refs[:n_blk]
        w_ref, x_ref, g_ref, dout_ref, dx_ref, part_ref = refs[n_blk:]

        @pl.when(pl.program_id(0) == 0)
        def _():
            part_ref[...] = jnp.zeros_like(part_ref)
        dh = jnp.zeros((rows, D_MODEL), F32)
        for j in range(n_blk):
            dh = dh + _dot_nt(dp_refs[j][...], _w_in_block(w_ref, j))
        xv = x_ref[...]
        inv = lax.rsqrt(jnp.mean(xv * xv, axis=-1, keepdims=True) + EPS)
        a = dh * g_ref[...]
        dx_ref[...] = (dout_ref[...] + inv * a
                       - xv * (inv * inv * inv) * jnp.mean(a * xv, axis=-1, keepdims=True))
        part_ref[0:1, :] += jnp.sum(dh * xv * inv, axis=0, keepdims=True)

    row = lambda: pl.BlockSpec((rows, D_MODEL), lambda i: (i, 0))
    dp_spec = lambda sec, cb: pl.BlockSpec((None, rows, COL_BLK), lambda i: (sec, i, cb))
    return _call(
        body, name="grad_x", grid=(SEQ // rows,),
        args=tuple(a for a, _, _ in dproj) + (w_in_g, x, g1, dout),
        in_specs=[dp_spec(sec, cb) for _, sec, cb in dproj]
                 + [pl.BlockSpec((N_SHARDS, D_MODEL, W_IN_SHARD), lambda i: (0, 0, 0)),
                    row(), pl.BlockSpec((1, D_MODEL), lambda i: (0, 0)), row()],
        out_specs=[row(), pl.BlockSpec((8, D_MODEL), lambda i: (0, 0))],
        out_shape=[jax.ShapeDtypeStruct((SEQ, D_MODEL), F32),
                   jax.ShapeDtypeStruct((8, D_MODEL), F32)],
        exchanges=exchanges)


def _grad_w_in(h, dp, blocks, name):
    n_blk = len(blocks)
    half = D_MODEL // 2
    pick = lambda vals: (lambda j: functools.reduce(lambda acc, iv: jnp.where(j == iv[0], iv[1], acc),
                                                     list(enumerate(vals))[1:], vals[0]))
    sec_of = pick([sec for sec, _ in blocks])
    cb_of = pick([cb for _, cb in blocks])

    def body(h_ref, dp_ref, p32_ref, p16_ref, send_ref, recv_ref, send_sems, recv_sems):
        j = pl.program_id(0)
        x, y, c = _place()
        cols = lambda cc: pl.ds(pl.multiple_of(cc * half, half), half)

        def copy(i):
            return _remote(send_ref.at[i], recv_ref.at[i], send_sems, recv_sems, i, (x, y, 1 - c))

        for i in range(n_blk):
            @pl.when(j == i)
            def _(i=i):
                send_ref[i] = _dot_tn(h_ref[:, cols(1 - c)], dp_ref[...])
                copy(i).start()
                p32_ref[i] = _dot_tn(h_ref[:, cols(c)], dp_ref[...])

        @pl.when(j == n_blk - 1)
        def _():
            for i in range(n_blk):
                copy(i).wait_recv()
                tot = p32_ref[i] + recv_ref[i]
                p32_ref[i] = tot
                p16_ref[i] = tot.astype(BF16)
            for i in range(n_blk):
                copy(i).wait_send()

    whole = pl.BlockSpec((n_blk, half, COL_BLK), lambda j: (0, 0, 0))
    return pl.pallas_call(
        body, name=name, grid=(n_blk,),
        in_specs=[pl.BlockSpec((SEQ, D_MODEL), lambda j: (0, 0)),
                  pl.BlockSpec((None, SEQ, COL_BLK), lambda j: (sec_of(j), 0, cb_of(j)))],
        out_specs=[whole, whole],
        out_shape=[jax.ShapeDtypeStruct((n_blk, half, COL_BLK), F32),
                   jax.ShapeDtypeStruct((n_blk, half, COL_BLK), BF16)],
        scratch_shapes=[pltpu.VMEM((n_blk, half, COL_BLK), F32), pltpu.VMEM((n_blk, half, COL_BLK), F32),
                        pltpu.SemaphoreType.DMA((n_blk,)), pltpu.SemaphoreType.DMA((n_blk,))],
        compiler_params=_params(dimension_semantics=("arbitrary",)),
    )(h, dp)


def _add_units(grad, recv, place, tile, name):
    n, rows, cols = grad.shape
    per_half = rows // 2 // tile

    def body(place_ref, g_ref, r_ref, o32_ref, o16_ref):
        v = g_ref[...] + r_ref[...]
        o32_ref[...] = v
        o16_ref[...] = v.astype(BF16)

    blk = lambda f: pl.BlockSpec((None, tile, cols), f)
    out = lambda s, i, p: (s, i, 0)
    return pl.pallas_call(
        body, name=name,
        grid_spec=pltpu.PrefetchScalarGridSpec(
            num_scalar_prefetch=1, grid=(n, per_half),
            in_specs=[blk(lambda s, i, p: (s, p[0] * per_half + i, 0)), blk(out)],
            out_specs=[blk(out), blk(out)]),
        out_shape=[jax.ShapeDtypeStruct(recv.shape, F32), jax.ShapeDtypeStruct(recv.shape, BF16)],
        compiler_params=_params(dimension_semantics=("arbitrary", "arbitrary")),
    )(place, grad, recv)


def _sum_units(part32, recv16, place, owners, tile, name):
    n, half, cols = part32.shape
    per_half = half // tile
    table = np.array([[sum(o == chip for o in owners)] + sorted(range(n), key=lambda j: (owners[j] != chip, j))
                      for chip in range(N_SHARDS)], np.int32)
    sched = jnp.concatenate([place[:1], jnp.asarray(table)[place[1]]])

    def block(k, i, p):
        live = k < p[1]
        unit = p[2 + jnp.minimum(k, jnp.maximum(p[1] - 1, 0))]
        return unit, jnp.where(live, i, per_half - 1)

    def body(sched_ref, p_ref, r_ref, o_ref):
        @pl.when(pl.program_id(0) < sched_ref[1])
        def _():
            acc = p_ref[...]
            for j in range(len(CHIP_FLIPS)):
                acc = acc + r_ref[j].astype(F32)
            o_ref[...] = acc

    return pl.pallas_call(
        body, name=name,
        grid_spec=pltpu.PrefetchScalarGridSpec(
            num_scalar_prefetch=1, grid=(n, per_half),
            in_specs=[pl.BlockSpec((None, tile, cols), lambda k, i, p: (*block(k, i, p), 0)),
                      pl.BlockSpec((None, len(CHIP_FLIPS), tile, cols),
                                   lambda k, i, p: (block(k, i, p)[0], 0, block(k, i, p)[1], 0))],
            out_specs=pl.BlockSpec((None, tile, cols),
                                   lambda k, i, p: (block(k, i, p)[0], p[0] * per_half + block(k, i, p)[1], 0))),
        out_shape=jax.ShapeDtypeStruct((n, 2 * half, cols), F32),
        compiler_params=_params(dimension_semantics=("arbitrary", "arbitrary")),
    )(sched, part32, recv16)


def _adamw_math(w, g, m, v):
    m = ADAM_B1 * m + (1.0 - ADAM_B1) * g
    v = ADAM_B2 * v + (1.0 - ADAM_B2) * (g * g)
    m_hat = m / (1.0 - ADAM_B1 ** ADAM_STEP)
    v_hat = v / (1.0 - ADAM_B2 ** ADAM_STEP)
    delta = -ADAM_LR * (m_hat / (jnp.sqrt(v_hat) + ADAM_EPS) + ADAM_WD * w)
    return delta, m, v


def _adamw_units(w, m, v, grads, pick, name):
    rows, cols = w.shape
    bc = grads[0].shape[-1]
    tile = min(rows, 256)
    n_g = len(grads)

    def body(pick_ref, w_ref, m_ref, v_ref, *refs):
        g_refs, (g_out, d_ref, nm_ref, nv_ref) = refs[:n_g], refs[n_g:]
        p = pl.program_id(0)
        for a in range(n_g):
            @pl.when(pick_ref[0, p] == a)
            def _(a=a):
                g = g_refs[a][...]
                g_out[...] = g
                d_ref[...], nm_ref[...], nv_ref[...] = _adamw_math(w_ref[...], g, m_ref[...], v_ref[...])

    blk = pl.BlockSpec((tile, bc), lambda p, i, pick: (i, p))

    def g_spec(a):
        return pl.BlockSpec((None, tile, bc),
                            lambda p, i, pick: (jnp.where(pick[0, p] == a, pick[1, p], 0),
                                                jnp.where(pick[0, p] == a, i, 0), 0))

    return pl.pallas_call(
        body, name=name,
        grid_spec=pltpu.PrefetchScalarGridSpec(
            num_scalar_prefetch=1, grid=(cols // bc, rows // tile),
            in_specs=[blk] * 3 + [g_spec(a) for a in range(n_g)],
            out_specs=[blk] * 4),
        out_shape=[jax.ShapeDtypeStruct(w.shape, F32)] * 4,
        compiler_params=_params(dimension_semantics=("arbitrary", "arbitrary")),
    )(pick, w, m, v, *grads)


ROW_NORM1, ROW_SCALE, ROW_LB, ROW_REC, ROW_FINAL, ROW_LOSS = 0, 1, 2, 4, 5, 6


def _small_update(parts, gathered, w, m, v):
    def body(own_ref, p_ref, w_ref, m_ref, v_ref, loss_ref, g_ref, d_ref, nm_ref, nv_ref):
        x, y, c = _place()
        me = 4 * x + 2 * y + c
        slot = lambda d: jnp.where(me == d, own_ref[...], p_ref[d])
        tot = slot(0)
        for d in range(1, 8):
            tot = tot + slot(d)
        wv = w_ref[...]
        l0 = wv[ROW_LB:ROW_LB + 1, :]
        l1 = wv[ROW_LB + 1:ROW_LB + 2, :]
        mx = jnp.maximum(l0, l1)
        e0 = jnp.exp(l0 - mx)
        e1 = jnp.exp(l1 - mx)
        lb = e0 / (e0 + e1)
        dl0 = tot[ROW_LB:ROW_LB + 1, :] * lb * (1.0 - lb)
        row = lax.broadcasted_iota(jnp.int32, tot.shape, 0)
        g = jnp.where(row == ROW_LB, dl0, jnp.where(row == ROW_LB + 1, -dl0, tot))
        g = jnp.where(row >= ROW_LOSS, 0.0, g)
        g_ref[...] = g
        d_ref[...], nm_ref[...], nv_ref[...] = _adamw_math(wv, g, m_ref[...], v_ref[...])
        loss_ref[...] = (0.5 / D_MODEL) * jnp.sum(tot[ROW_LOSS:ROW_LOSS + 1, :], axis=-1, keepdims=True)

    return pl.pallas_call(
        body, name="small_update",
        out_shape=[jax.ShapeDtypeStruct((1, 1), F32)] + [jax.ShapeDtypeStruct(w.shape, F32)] * 4,
        compiler_params=_params(),
    )(parts, gathered, w, m, v)


SHARD_OWNERS = tuple(range(N_SHARDS))
BLOCKS_POOL = (0, 1, 2, 3)
BLOCKS_A = (4, 6, 8, 10)
BLOCKS_B = (5, 7, 9, 11)
BLOCK_GROUPS = (BLOCKS_POOL, BLOCKS_A, BLOCKS_B)


def _block_owners(blocks):
    return tuple(j // (W_IN_SHARD // COL_BLK) for j in blocks)


def _small_rows(norm1, scale, lb, rec, final):
    pad = jnp.zeros((2, D_MODEL), F32)
    return jnp.concatenate([norm1, scale, lb, rec, final.reshape(1, D_MODEL), pad], axis=0)


def kernel(x, norm1_g, w_in, pool_w, pool_scale, lb_logits, rec_norm_g, w_out, final_norm_g, loss_target, m_norm1_g, m_w_in, m_pool_w, m_pool_scale, m_lb_logits, m_rec_norm_g, m_w_out, m_final_norm_g, v_norm1_g, v_w_in, v_pool_w, v_pool_scale, v_lb_logits, v_rec_norm_g, v_w_out, v_final_norm_g):
    xi, yi, ci = _place()
    chip = 2 * xi + yi
    place = jnp.stack([ci, chip]).astype(jnp.int32)
    pw_rows = N_GROUPS * PW_SHARD
    flat_pw = lambda a: a.reshape(pw_rows, PG)
    x2, target, gf = x[0], loss_target[0], final_norm_g.reshape(1, D_MODEL)
    consts = {n: jnp.asarray(a, BF16 if n.startswith("tri") else F32) for n, a in _chunk_constants().items()}

    proj, h, w_in_g = _in_proj(x2, norm1_g, _cast_own(w_in[0], place, "cast_w_in"), place)
    (y_rec, o_raw, st_prev), ((w_out_g, pw_g),) = _rec_fwd(
        proj, lb_logits, rec_norm_g, consts,
        [_ex_gather([_cast_own(w_out[0], place, "cast_w_out"), _cast_own(flat_pw(pool_w), place, "cast_pool_w")])])
    w_out_g = w_out_g.reshape(2 * D_MODEL, D_MODEL)
    pw_full = pw_g.reshape(N_SHARDS, N_GROUPS, PW_SHARD, PG).transpose(1, 0, 2, 3).reshape(N_GROUPS, PG, PG)
    y_pool = _pool_fwd(proj, pw_full, pool_scale)
    dout, dout_b, part_out = _out_proj_loss(y_pool, y_rec, w_out_g, x2, target, gf)

    p_out32, p_out16 = _grad_w_out(y_pool, y_rec, dout_b)
    (dpool, gpw, dscale), ((rb_out,),) = _pool_bwd(proj, dout_b, w_out_g, pw_full, pool_scale,
                                                   [_ex_send([p_out16], [SHARD_OWNERS])])
    g_out = _sum_units(p_out32, rb_out, place, SHARD_OWNERS, 256, "sum_w_out")
    gpw = gpw.reshape(N_GROUPS, N_SHARDS, PW_SHARD, PG).transpose(1, 0, 2, 3).reshape(N_SHARDS, pw_rows, PG)
    p_inp32, p_inp16 = _grad_w_in(h, dpool, [(0, 0), (0, 1), (1, 0), (1, 1)], "grad_w_in_pool")

    pool_owners, a_owners, b_owners = (_block_owners(b) for b in BLOCK_GROUPS)
    rec_args = (proj, o_raw, st_prev, dout_b, w_out_g, lb_logits, rec_norm_g, consts)
    (drec_a, part_a), ((rb_inp,), (ra_pw,)) = _rec_bwd(
        *rec_args, 0, "rec_bwd_a", [_ex_send([p_inp16], [pool_owners], units=[(0, 1)]), _ex_swap([gpw])])
    p_pw32, p_pw16 = _add_units(gpw, ra_pw, place, 128, "add_pool_w")
    rec_blocks = [(n, 0) for n in range(4)]
    p_ina32, p_ina16 = _grad_w_in(h, drec_a, rec_blocks, "grad_w_in_a")

    (drec_b, part_b), ((rb_inp, rb_ina, rb_pw),) = _rec_bwd(
        *rec_args, HALF_HEADS, "rec_bwd_b",
        [_ex_send([p_inp16, p_ina16, p_pw16], [pool_owners, a_owners, SHARD_OWNERS],
                  units=[(2, 3), (0, 1, 2, 3), (0, 1, 2, 3)], landed=[rb_inp, None, None])])
    g_inp = _sum_units(p_inp32, rb_inp, place, pool_owners, 256, "sum_w_in_pool")
    g_ina = _sum_units(p_ina32, rb_ina, place, a_owners, 256, "sum_w_in_a")
    g_pw = _sum_units(p_pw32, rb_pw, place, SHARD_OWNERS, 128, "sum_pool_w")
    p_inb32, p_inb16 = _grad_w_in(h, drec_b, rec_blocks, "grad_w_in_b")

    dproj = ([(dpool, 0, 0), (dpool, 0, 1), (dpool, 1, 0), (dpool, 1, 1)]
             + [(d, n, 0) for n in range(4) for d in (drec_a, drec_b)])
    (dx, part_x), ((rb_inb,),) = _grad_x(dproj, w_in_g, x2, norm1_g, dout, [_ex_send([p_inb16], [b_owners])])
    g_inb = _sum_units(p_inb32, rb_inb, place, b_owners, 256, "sum_w_in_b")
    zero = jnp.zeros((1, D_MODEL), F32)
    part_rec = jnp.concatenate([part_a, part_b], axis=1)
    parts = jnp.concatenate([part_x[0:1], dscale, part_rec[1:2], zero, part_rec[0:1], part_out[0:1],
                             part_out[1:2], zero], axis=0)
    _, ((g_out, g_pw, g_inp, g_ina, g_inb), (gathered,)) = _call(
        None, name="join_halves",
        exchanges=[_ex_join([g_out, g_pw, g_inp, g_ina, g_inb],
                            [SHARD_OWNERS, SHARD_OWNERS, pool_owners, a_owners, b_owners]),
                   _ex_gather_small(parts)])

    group_of = np.zeros((D_PROJ // COL_BLK,), np.int32)
    index_of = np.zeros((D_PROJ // COL_BLK,), np.int32)
    for gi, blocks in enumerate(BLOCK_GROUPS):
        for i, j in enumerate(blocks):
            group_of[j], index_of[j] = gi, i
    per_shard = W_IN_SHARD // COL_BLK
    pick_in = jnp.stack([lax.dynamic_slice(jnp.asarray(group_of), (per_shard * chip,), (per_shard,)),
                         lax.dynamic_slice(jnp.asarray(index_of), (per_shard * chip,), (per_shard,))])
    pick_own = jnp.stack([jnp.zeros((1,), jnp.int32), chip.reshape(1).astype(jnp.int32)])
    big = [_adamw_units(w_in[0], m_w_in[0], v_w_in[0], [g_inp, g_ina, g_inb], pick_in, "adamw_w_in"),
           _adamw_units(w_out[0], m_w_out[0], v_w_out[0], [g_out], pick_own, "adamw_w_out"),
           _adamw_units(flat_pw(pool_w), flat_pw(m_pool_w), flat_pw(v_pool_w), [g_pw], pick_own, "adamw_pool_w")]

    small_w = _small_rows(norm1_g, pool_scale, lb_logits, rec_norm_g, final_norm_g)
    small_m = _small_rows(m_norm1_g, m_pool_scale, m_lb_logits, m_rec_norm_g, m_final_norm_g)
    small_v = _small_rows(v_norm1_g, v_pool_scale, v_lb_logits, v_rec_norm_g, v_final_norm_g)
    loss, *small = _small_update(parts, gathered, small_w, small_m, small_v)

    def leaves(k):
        s = small[k]
        return (s[ROW_NORM1:ROW_NORM1 + 1], big[0][k][None], big[2][k].reshape(pool_w.shape),
                s[ROW_SCALE:ROW_SCALE + 1], s[ROW_LB:ROW_LB + 2], s[ROW_REC:ROW_REC + 1],
                big[1][k][None], s[ROW_FINAL])

    return (loss.reshape(()), dx[None], *leaves(0), *leaves(1), *leaves(2), *leaves(3))
```

```python
import functools

import numpy as np
import jax
import jax.numpy as jnp
from jax import lax
from jax.experimental import pallas as pl
from jax.experimental.pallas import tpu as pltpu

F32 = jnp.float32
BF16 = jnp.bfloat16

SEQ = 2048
D_MODEL = 1024
D_PROJ = 6144
N_SEC = 6
N_GROUPS = 4
PG = 256
N_HEADS = 8
HEAD = 128
CHUNK = 64
N_LEVELS = 6
N_SHARDS = 4
W_IN_SHARD = D_PROJ // N_SHARDS
W_OUT_SHARD = 2048 // N_SHARDS
PW_SHARD = PG // N_SHARDS
COL_BLK = 512
EPS = 1e-6

ADAM_LR = 0.001
ADAM_B1 = 0.9
ADAM_B2 = 0.999
ADAM_EPS = 1e-08
ADAM_WD = 0.01
ADAM_STEP = 10

V7X_VMEM_LIMIT = 56 * 1024 * 1024
MESH = pl.DeviceIdType.MESH


def _params(**kw):
    return pltpu.CompilerParams(vmem_limit_bytes=V7X_VMEM_LIMIT, **kw)


def _sig(x):
    return 1.0 / (1.0 + jnp.exp(-x))


def _dot(a, b):
    return jnp.dot(a, b, preferred_element_type=F32)


def _dot_nt(a, b):
    return lax.dot_general(a, b, (((1,), (1,)), ((), ())), preferred_element_type=F32)


def _dot_tn(a, b):
    return lax.dot_general(a, b, (((0,), (0,)), ((), ())), preferred_element_type=F32)


def _split3(a):
    p1 = a.astype(BF16)
    r1 = a - p1.astype(F32)
    p2 = r1.astype(BF16)
    p3 = (r1 - p2.astype(F32)).astype(BF16)
    return jnp.concatenate([p1, p2, p3], axis=-1)


def _dot3(w01, a):
    n = a.shape[-1]
    r = _dot(w01, _split3(a))
    return r[:, :n] + r[:, n:2 * n] + r[:, 2 * n:]


def _chunk_constants():
    j = np.arange(CHUNK)
    tt, ss = np.meshgrid(j, j, indexing="ij")
    x = tt ^ ss
    hb = np.full((CHUNK, CHUNK), -1, np.int32)
    for l in range(N_LEVELS):
        hb[x >= (1 << l)] = l
    sym = np.stack([(hb == l) for l in range(N_LEVELS)]).astype(np.float32)
    low = sym * (tt > ss)
    sign = np.stack([np.where((j >> l) & 1, 1.0, -1.0) for l in range(N_LEVELS)]).astype(np.float32)
    sign = np.ascontiguousarray(np.broadcast_to(sign[:, :, None], (N_LEVELS, CHUNK, HEAD)))
    tri = (ss <= tt).astype(np.float32)
    return dict(tri=tri, tri_t=np.ascontiguousarray(tri.T), low=low,
                low_t=np.ascontiguousarray(low.transpose(0, 2, 1)), sym=sym, sign=sign)


def _in_proj(x, g1, w_slots, place):
    n_col = D_PROJ // COL_BLK
    per_shard = W_IN_SHARD // COL_BLK
    rows = 1024
    half_rows = D_MODEL // 2
    quarter_rows = D_MODEL // 4
    FLIPS = (0, 2, 1, 3)
    ORDER = ([(0, p) for p in range(per_shard)] + [(m, p) for p in range(per_shard) for m in (1, 2)]
             + [(3, p) for p in range(per_shard)])

    def shard_at(m, chip):
        return chip ^ FLIPS[m]

    def pick(vals, t):
        return functools.reduce(lambda acc, iv: jnp.where(t == iv[0], iv[1], acc), list(enumerate(vals))[1:], vals[0])

    def body(place_ref, x_ref, g_ref, w_in_ref, proj_ref, h_ref, w_ref, wbuf, load_sems, send_sems, recv_sems):
        t = pl.program_id(0)
        x_, y_, c = _place()
        chip = 2 * x_ + y_
        me, other_core = (x_, y_, c), (x_, y_, 1 - c)
        x_nbr, y_nbr = (1 - x_, y_, c), (x_, 1 - y_, c)

        def rows_of(half, q=None):
            if q is None:
                return pl.ds(pl.multiple_of(half * half_rows, half_rows), half_rows)
            return pl.ds(pl.multiple_of(half * half_rows + q * quarter_rows, quarter_rows), quarter_rows)

        def block(m, p, r):
            return w_ref.at[shard_at(m, chip), r, pl.ds(p * COL_BLK, COL_BLK)]

        def copy(k, ref, to):
            return _remote(ref, ref, send_sems, recv_sems, k, to)

        direct = lambda n, p, to: copy(3 * n + p, block(0, p, rows_of(c)), to)
        relay = lambda n, p, to: copy(6 + 3 * n + p, block(1 + n, p, rows_of(c, n)), to)
        arrived = lambda m, p: ([copy(3 * (m - 1) + p, block(m, p, rows_of(c)), me)] if m < 3 else
                                [copy(6 + 3 * n + p, block(3, p, rows_of(c, n)), me) for n in (0, 1)])
        passed_on = lambda m, p, half, to: copy(9 + 3 * m + p, block(m, p, rows_of(half)), to)

        def load(m, p, slot):
            return pltpu.make_async_copy(w_ref.at[shard_at(m, chip), :, pl.ds(p * COL_BLK, COL_BLK)],
                                         wbuf.at[slot], load_sems.at[slot])

        def prepare(m, p):
            for cp in arrived(m, p):
                cp.wait_recv()
            passed_on(m, p, c, other_core).start()
            if m < 3:
                relay(m - 1, p, y_nbr if m == 1 else x_nbr).start()

        @pl.when(t == 0)
        def _():
            for p in range(per_shard):
                direct(0, p, x_nbr).start()
                direct(1, p, y_nbr).start()
            for p in range(per_shard):
                load(0, p, p).start()

            def norm(i, _):
                r = pl.ds(pl.multiple_of(i * rows, rows), rows)
                xv = x_ref[r, :]
                inv = lax.rsqrt(jnp.mean(xv * xv, axis=-1, keepdims=True) + EPS)
                h_ref[r, :] = (xv * inv * g_ref[...]).astype(BF16)
                return 0
            lax.fori_loop(0, SEQ // rows, norm, 0)

        for step, (m, p) in enumerate(ORDER):
            @pl.when(t == step)
            def _(step=step, m=m, p=p):
                slot = step % per_shard
                if m > 0:
                    passed_on(m, p, 1 - c, me).wait_recv()
                    load(m, p, slot).start()
                if step + 1 < n_col and ORDER[step + 1][0] > 0:
                    prepare(*ORDER[step + 1])
                load(m, p, slot).wait()

                def mm(i, _):
                    r = pl.ds(pl.multiple_of(i * rows, rows), rows)
                    proj_ref[r, :] = _dot(h_ref[r, :], wbuf[slot])
                    return 0
                lax.fori_loop(0, SEQ // rows, mm, 0)

        @pl.when(t == n_col - 1)
        def _():
            for p in range(per_shard):
                sent = [direct(0, p, x_nbr), direct(1, p, y_nbr), relay(0, p, y_nbr), relay(1, p, x_nbr)]
                for cp in sent + [passed_on(m, p, c, other_core) for m in (1, 2, 3)]:
                    cp.wait_send()

    return pl.pallas_call(
        body, name="in_proj",
        grid_spec=pltpu.PrefetchScalarGridSpec(
            num_scalar_prefetch=1, grid=(n_col,),
            in_specs=[pl.BlockSpec((SEQ, D_MODEL), lambda t, p: (0, 0)),
                      pl.BlockSpec((1, D_MODEL), lambda t, p: (0, 0)),
                      pl.BlockSpec(memory_space=pl.ANY)],
            out_specs=[pl.BlockSpec((SEQ, COL_BLK),
                                    lambda t, p: (0, per_shard * (p[1] ^ pick([FLIPS[m] for m, _ in ORDER], t))
                                                  + pick([b for _, b in ORDER], t))),
                       pl.BlockSpec((SEQ, D_MODEL), lambda t, p: (0, 0)),
                       pl.BlockSpec(memory_space=pl.ANY)],
            scratch_shapes=[pltpu.VMEM((per_shard, D_MODEL, COL_BLK), BF16),
                            pltpu.SemaphoreType.DMA((per_shard,)),
                            pltpu.SemaphoreType.DMA((21,)), pltpu.SemaphoreType.DMA((21,))]),
        out_shape=[jax.ShapeDtypeStruct((SEQ, D_PROJ), F32),
                   jax.ShapeDtypeStruct((SEQ, D_MODEL), BF16),
                   jax.ShapeDtypeStruct(w_slots.shape, BF16)],
        input_output_aliases={3: 2},
        compiler_params=_params(dimension_semantics=("arbitrary",)),
    )(place, x, g1, w_slots)


POOL_ROWS = 256
POOL_HALO = 16
POOL_PAIR = 2


def _window_sums(ext, g, shift_of):
    s = ext
    for k in range(N_GROUPS):
        s = jnp.where(k <= g, s + pltpu.roll(s, shift_of(k), 0), s)
    return s


def _pool_diff(u_ref, i, g):
    n = POOL_ROWS + POOL_HALO
    r0 = i * POOL_ROWS
    cur = u_ref[pl.ds(pl.multiple_of(r0, POOL_ROWS), POOL_ROWS), :]
    before = u_ref[pl.ds(pl.multiple_of(jnp.maximum(r0 - POOL_HALO, 0), 8), POOL_HALO), :]
    before = jnp.where(i > 0, before, 0.0)
    ext = jnp.concatenate([before, cur], axis=0)
    s = _window_sums(ext, g, lambda k: 1 << k)[POOL_HALO:, :]
    t = r0 + lax.broadcasted_iota(jnp.int32, (POOL_ROWS, 1), 0)
    width = (2 << g).astype(F32)
    inv_count = 1.0 / jnp.minimum((t + 1).astype(F32), width)
    return s * inv_count - cur, inv_count


def _pool_fwd(proj, pw_g, pool_scale):
    def body(u_ref, gate_ref, pw_ref, sc_ref, y_ref):
        g = pl.program_id(0)

        def step(ii, _):
            chunks = [POOL_PAIR * ii + a for a in range(POOL_PAIR)]
            ds = [_pool_diff(u_ref, i, g)[0].astype(BF16) for i in chunks]
            mixed = [_dot(d, pw_ref[...]) for d in ds]
            for i, m in zip(chunks, mixed):
                r = pl.ds(pl.multiple_of(i * POOL_ROWS, POOL_ROWS), POOL_ROWS)
                gate = gate_ref[r, :]
                y_ref[r, :] = (m * sc_ref[...] * (gate * _sig(gate))).astype(BF16)
            return 0
        lax.fori_loop(0, SEQ // POOL_ROWS // POOL_PAIR, step, 0)

    return pl.pallas_call(
        body, name="pool_fwd", grid=(N_GROUPS,),
        in_specs=[pl.BlockSpec((SEQ, PG), lambda g: (0, g)),
                  pl.BlockSpec((SEQ, PG), lambda g: (0, N_GROUPS + g)),
                  pl.BlockSpec((None, PG, PG), lambda g: (g, 0, 0)),
                  pl.BlockSpec((1, PG), lambda g: (0, g))],
        out_specs=pl.BlockSpec((SEQ, PG), lambda g: (0, g)),
        out_shape=jax.ShapeDtypeStruct((SEQ, D_MODEL), BF16),
        compiler_params=_params(dimension_semantics=("arbitrary",)),
    )(proj, proj, pw_g, pool_scale)


REC_ROWS = 1024
REC_CHUNKS = REC_ROWS // CHUNK
N_REC_BLK = SEQ // REC_ROWS
REC_GROUP = REC_CHUNKS
REC_GROUP_BWD = REC_CHUNKS
SEC_BLK = D_MODEL // HEAD


def _lower_bound(lb_ref):
    l0 = lb_ref[0:1, :]
    l1 = lb_ref[1:2, :]
    mx = jnp.maximum(l0, l1)
    e0 = jnp.exp(l0 - mx)
    e1 = jnp.exp(l1 - mx)
    return e0 / (e0 + e1)


def _gates(q, fl, lb):
    qs = q * _sig(q)
    sf = _sig(fl)
    f = lb + (1.0 - lb) * sf
    return qs, sf, f, 1.0 - f, jnp.log(f)


LOG2E = 1.4426950408889634


def _level_factors(g2, qs, k, sign_ref):
    t = lax.broadcasted_iota(jnp.int32, (CHUNK, HEAD), 0)
    row = lambda r, n: jnp.broadcast_to(g2[r:r + 1, :], (n, HEAD))
    out = []
    for l in range(N_LEVELS):
        m = 1 << l
        if l == 0:
            g_mid = jnp.where((t & 1) == 1, pltpu.roll(g2, 1, 0), g2)
        elif l == 1:
            low = (t & 7) < 4
            g_mid = jnp.concatenate([jnp.where(low[:8], row(8 * i + 1, 8), row(8 * i + 5, 8))
                                     for i in range(CHUNK // 8)], axis=0)
        else:
            g_mid = jnp.concatenate([row(b * 2 * m + m - 1, 2 * m) for b in range(CHUNK // (2 * m))], axis=0)
        sgn = sign_ref[l]
        up = sgn > 0.0
        e = jnp.exp2((g2 - g_mid) * sgn)
        x = jnp.where(up, qs, k) * e
        hi = x.astype(BF16)
        out.append((hi, (x - hi.astype(F32)).astype(BF16), e, up))
    return out


CHIP_FLIPS = ((1, 0), (0, 1), (1, 1))
HBM = pl.BlockSpec(memory_space=pl.ANY)


def _place():
    return lax.axis_index("x"), lax.axis_index("y"), lax.axis_index("c")


def _remote(src, dst, send_sems, recv_sems, k, to):
    return pltpu.make_async_remote_copy(src_ref=src, dst_ref=dst, send_sem=send_sems.at[k],
                                        recv_sem=recv_sems.at[k], device_id=to, device_id_type=MESH)


def _half_rows(ref, c):
    half = ref.shape[-2] // 2
    rows = pl.ds(pl.multiple_of(c * half, half), half)
    return ref.at[:, rows, :] if len(ref.shape) == 3 else ref.at[rows, :]


class _Exchange:
    def __init__(self, inputs, out_shapes, n_sems, start, finish, aliases=None):
        self.inputs, self.out_shapes, self.n_sems = list(inputs), list(out_shapes), n_sems
        self.start, self.finish, self.aliases = start, finish, dict(aliases or {})


def _ex_swap(grads):
    def copies(ins, outs, send, recv):
        x, y, c = _place()
        return [_remote(_half_rows(g, 1 - c), o, send, recv, t, (x, y, 1 - c))
                for t, (g, o) in enumerate(zip(ins, outs))]

    def start(*refs):
        for cp in copies(*refs):
            cp.start()

    def finish(*refs):
        cps = copies(*refs)
        for cp in cps:
            cp.wait_recv()
        for cp in cps:
            cp.wait_send()

    shapes = [jax.ShapeDtypeStruct((a.shape[0], a.shape[1] // 2, a.shape[2]), F32) for a in grads]
    return _Exchange(grads, shapes, len(grads), start, finish)


def _ex_send(parts16, owners, units=None, landed=None):
    n_t = len(parts16)
    units = units or [tuple(range(len(o))) for o in owners]
    landed = landed or [None] * n_t
    given = [t for t in range(n_t) if landed[t] is not None]

    def each(ins, outs, send, recv, to_sender, to_owner):
        x, y, c = _place()
        k = 0
        for t, own in enumerate(owners):
            for j in units[t]:
                for r, (fx, fy) in enumerate(CHIP_FLIPS):
                    tx, ty = x ^ fx, y ^ fy
                    cp = _remote(ins[t].at[j], outs[t].at[j, r], send, recv, k, (tx, ty, c))
                    if to_sender is not None:
                        pl.when(2 * tx + ty == own[j])(functools.partial(to_sender, cp))
                    if to_owner is not None:
                        pl.when(2 * x + y == own[j])(functools.partial(to_owner, cp))
                    k += 1

    def start(*refs):
        each(*refs, lambda cp: cp.start(), None)

    def finish(*refs):
        each(*refs, None, lambda cp: cp.wait_recv())
        each(*refs, lambda cp: cp.wait_send(), None)

    shapes = [jax.ShapeDtypeStruct((a.shape[0], len(CHIP_FLIPS)) + a.shape[1:], BF16) for a in parts16]
    return _Exchange(list(parts16) + [landed[t] for t in given], shapes,
                     len(CHIP_FLIPS) * sum(len(u) for u in units), start, finish,
                     aliases={n_t + i: t for i, t in enumerate(given)})


def _ex_join(units, owners):
    def each(ins, outs, send, recv, fn):
        x, y, c = _place()
        k = 0
        for t, own in enumerate(owners):
            for j, o in enumerate(own):
                def half(cc, to, u=outs[t].at[j], k=k):
                    return _remote(_half_rows(u, cc), _half_rows(u, cc), send, recv, k, to)
                mine = functools.partial(half, c, (x, y, 1 - c))
                theirs = functools.partial(half, 1 - c, (x, y, c))
                pl.when(2 * x + y == o)(functools.partial(fn, mine, theirs))
                k += 1

    def start(*refs):
        each(*refs, lambda mine, theirs: mine().start())

    def finish(*refs):
        each(*refs, lambda mine, theirs: theirs().wait_recv())
        each(*refs, lambda mine, theirs: mine().wait_send())

    shapes = [jax.ShapeDtypeStruct(a.shape, F32) for a in units]
    return _Exchange(units, shapes, sum(len(o) for o in owners), start, finish,
                     aliases={t: t for t in range(len(units))})


def _ex_gather(slots):
    n_t = len(slots)
    n_fl = len(CHIP_FLIPS)

    def piece(ref, shard, half):
        return _half_rows(ref.at[shard], half)

    def first(outs, send, recv):
        x, y, c = _place()
        s = 2 * x + y
        return [_remote(piece(outs[t], s, c), piece(outs[t], s, c), send, recv, n_t * j + t, (x ^ fx, y ^ fy, c))
                for j, (fx, fy) in enumerate(CHIP_FLIPS) for t in range(n_t)]

    def start(ins, outs, send, recv):
        for cp in first(outs, send, recv):
            cp.start()

    def finish(ins, outs, send, recv):
        x, y, c = _place()
        passed = []
        for j, (fx, fy) in enumerate(CHIP_FLIPS):
            sj = 2 * (x ^ fx) + (y ^ fy)
            for t in range(n_t):
                k = n_t * j + t
                _remote(piece(outs[t], sj, c), piece(outs[t], sj, c), send, recv, k, (x, y, c)).wait_recv()
                cp = _remote(piece(outs[t], sj, c), piece(outs[t], sj, c), send, recv, n_t * n_fl + k, (x, y, 1 - c))
                cp.start()
                passed.append(cp)
        for j, (fx, fy) in enumerate(CHIP_FLIPS):
            sj = 2 * (x ^ fx) + (y ^ fy)
            for t in range(n_t):
                k = n_t * n_fl + n_t * j + t
                _remote(piece(outs[t], sj, 1 - c), piece(outs[t], sj, 1 - c), send, recv, k, (x, y, c)).wait_recv()
        for cp in first(outs, send, recv) + passed:
            cp.wait_send()

    shapes = [jax.ShapeDtypeStruct(a.shape, BF16) for a in slots]
    return _Exchange(slots, shapes, 2 * n_t * n_fl, start, finish, aliases={t: t for t in range(n_t)})


def _ex_gather_small(parts):
    def copies(ins, outs, send, recv):
        x, y, c = _place()
        me = 4 * x + 2 * y + c
        return [_remote(ins[0], outs[0].at[me], send, recv, mask - 1,
                        (x ^ (mask >> 2), y ^ ((mask >> 1) & 1), c ^ (mask & 1))) for mask in range(1, 8)]

    def start(*refs):
        for cp in copies(*refs):
            cp.start()

    def finish(ins, outs, send, recv):
        x, y, c = _place()
        me = 4 * x + 2 * y + c
        for mask in range(1, 8):
            _remote(ins[0], outs[0].at[me ^ mask], send, recv, mask - 1, (x, y, c)).wait_recv()
        for cp in copies(ins, outs, send, recv):
            cp.wait_send()

    return _Exchange([parts], [jax.ShapeDtypeStruct((8,) + parts.shape, F32)], 7, start, finish)


def _call(body, *, name, args=(), in_specs=(), out_specs=(), out_shape=(), grid=(), scratch_shapes=(),
          exchanges=()):
    n_in, n_out, n_scr = len(args), len(out_shape), len(scratch_shapes)
    ex_in, ex_out, ex_scr, spans, alias = [], [], [], [], {}
    for ex in exchanges:
        spans.append((len(ex_in), len(ex.inputs), len(ex_out), len(ex.out_shapes)))
        for i, o in ex.aliases.items():
            alias[n_in + len(ex_in) + i] = n_out + len(ex_out) + o
        ex_in += ex.inputs
        ex_out += ex.out_shapes
        ex_scr += [pltpu.SemaphoreType.DMA((ex.n_sems,)), pltpu.SemaphoreType.DMA((ex.n_sems,))]

    def full(*refs):
        ins, x_in = refs[:n_in], refs[n_in:n_in + len(ex_in)]
        outs = refs[n_in + len(ex_in):n_in + len(ex_in) + n_out]
        x_out = refs[n_in + len(ex_in) + n_out:n_in + len(ex_in) + n_out + len(ex_out)]
        scr = refs[len(refs) - n_scr - len(ex_scr):len(refs) - len(ex_scr)]
        sems = refs[len(refs) - len(ex_scr):]

        def run(which):
            for e, (ex, (i0, ni, o0, no)) in enumerate(zip(exchanges, spans)):
                getattr(ex, which)(x_in[i0:i0 + ni], x_out[o0:o0 + no], sems[2 * e], sems[2 * e + 1])

        if grid:
            ids = [pl.program_id(a) for a in range(len(grid))]
            is_first = functools.reduce(jnp.logical_and, [i == 0 for i in ids])
            is_last = functools.reduce(jnp.logical_and, [i == g - 1 for i, g in zip(ids, grid)])
            pl.when(is_first)(lambda: run("start"))
            body(*ins, *outs, *scr)
            pl.when(is_last)(lambda: run("finish"))
        else:
            run("start")
            if body is not None:
                body(*ins, *outs, *scr)
            run("finish")

    kw = dict(grid=grid) if grid else {}
    if grid:
        kw["compiler_params"] = _params(dimension_semantics=("arbitrary",) * len(grid))
    else:
        kw["compiler_params"] = _params()
    res = pl.pallas_call(
        full, name=name,
        in_specs=list(in_specs) + [HBM] * len(ex_in),
        out_specs=list(out_specs) + [HBM] * len(ex_out),
        out_shape=list(out_shape) + ex_out,
        scratch_shapes=list(scratch_shapes) + ex_scr,
        input_output_aliases=alias, **kw,
    )(*args, *ex_in)
    own = list(res[:n_out])
    per_ex = [list(res[n_out + o0:n_out + o0 + no]) for (_, _, o0, no) in spans]
    return own, per_ex


def _cast_own(w, place, name):
    rows, cols = w.shape
    tile = min(rows, 256)

    def body(place_ref, w_ref, o_ref):
        o_ref[...] = w_ref[...].astype(BF16)

    return pl.pallas_call(
        body, name=name,
        grid_spec=pltpu.PrefetchScalarGridSpec(
            num_scalar_prefetch=1, grid=(rows // tile,),
            in_specs=[pl.BlockSpec((tile, cols), lambda i, p: (i, 0))],
            out_specs=pl.BlockSpec((None, tile, cols), lambda i, p: (p[1], i, 0))),
        out_shape=jax.ShapeDtypeStruct((N_SHARDS, rows, cols), BF16),
        compiler_params=_params(dimension_semantics=("arbitrary",)),
    )(place, w)


def _rec_fwd(proj, lb_logits, rec_g, consts, exchanges):
    tri, low, sign = consts["tri"], consts["low"], consts["sign"]

    def body(q_ref, f_ref, i_ref, rg_ref, lb_ref, g_ref, w_ref, low_ref, sign_ref, y_ref, o_ref, stp_ref, st_ref):
        @pl.when(pl.program_id(1) == 0)
        def _():
            st_ref[...] = jnp.zeros_like(st_ref)
        lb = _lower_bound(lb_ref)
        st = st_ref[...]
        rows = lambda c: pl.ds(c * CHUNK, CHUNK)
        for c0 in range(0, REC_CHUNKS, REC_GROUP):
            group = range(c0, c0 + REC_GROUP)
            gated = [_gates(q_ref[rows(c), :], f_ref[rows(c), :], lb) for c in group]
            g2s = [_dot3(w_ref[...], g) * LOG2E for (_, _, _, _, g) in gated]
            xs = [[xl for xl, _, _, _ in _level_factors(g2, qs, k, sign_ref)]
                  for g2, (qs, _, _, k, _) in zip(g2s, gated)]
            a_s = []
            for x in xs:
                a = jnp.zeros((CHUNK, CHUNK), F32)
                for l, xl in enumerate(x):
                    a = a + _dot_nt(xl, xl) * low_ref[l]
                a_s.append(a.astype(BF16))
            vbs = [i_ref[rows(c), :].astype(BF16) for c in group]
            intra = [_dot(a, vb) for a, vb in zip(a_s, vbs)]
            kvs = [_dot_tn(vb, (k * jnp.exp2(g2[CHUNK - 1:CHUNK, :] - g2)).astype(BF16))
                   for vb, g2, (_, _, _, k, _) in zip(vbs, g2s, gated)]
            for i, c in enumerate(group):
                qs, _, _, k, _ = gated[i]
                g2 = g2s[i]
                stp_ref[c] = st
                v = i_ref[rows(c), :]
                rg = rg_ref[rows(c), :]
                o = (intra[i] + jnp.sum(qs * k, axis=-1, keepdims=True) * v
                     + _dot_nt((qs * jnp.exp2(g2)).astype(BF16), st.astype(BF16)))
                st = st * jnp.exp2(g2[CHUNK - 1:CHUNK, :]) + kvs[i]
                o_ref[rows(c), :] = o
                inv = lax.rsqrt(jnp.mean(o * o, axis=-1, keepdims=True) + EPS)
                y_ref[rows(c), :] = (o * inv * g_ref[...] * (rg * _sig(rg))).astype(BF16)
        st_ref[...] = st

    sec = lambda n: pl.BlockSpec((REC_ROWS, HEAD), lambda h, b: (b, n * SEC_BLK + h))
    vec = lambda rows: pl.BlockSpec((rows, HEAD), lambda h, b: (0, h))
    full = lambda a: pl.BlockSpec(a.shape, lambda h, b: (0,) * a.ndim)
    return _call(
        body, name="rec_fwd", grid=(N_HEADS, N_REC_BLK),
        args=(proj, proj, proj, proj, lb_logits, rec_g, tri, low, sign),
        in_specs=[sec(2), sec(3), sec(4), sec(5), vec(2), vec(1), full(tri), full(low), full(sign)],
        out_specs=[pl.BlockSpec((REC_ROWS, HEAD), lambda h, b: (b, h)),
                   pl.BlockSpec((REC_ROWS, HEAD), lambda h, b: (b, h)),
                   pl.BlockSpec((None, REC_CHUNKS, HEAD, HEAD), lambda h, b: (h, b, 0, 0))],
        out_shape=[jax.ShapeDtypeStruct((SEQ, D_MODEL), BF16),
                   jax.ShapeDtypeStruct((SEQ, D_MODEL), F32),
                   jax.ShapeDtypeStruct((N_HEADS, SEQ // CHUNK, HEAD, HEAD), F32)],
        scratch_shapes=[pltpu.VMEM((HEAD, HEAD), F32)],
        exchanges=exchanges)


OUT_ROWS = 512


def _out_proj_loss(y_pool, y_rec, w_out_g, x, target, gf):
    def body(yp_ref, yr_ref, w_ref, x_ref, t_ref, gf_ref, dout_ref, doutb_ref, part_ref):
        @pl.when(pl.program_id(0) == 0)
        def _():
            part_ref[...] = jnp.zeros_like(part_ref)
        halves = [pl.ds(a * (OUT_ROWS // 2), OUT_ROWS // 2) for a in range(2)]
        outs = [x_ref[r, :] + _dot(yp_ref[r, :], w_ref[0:D_MODEL, :])
                + _dot(yr_ref[r, :], w_ref[D_MODEL:2 * D_MODEL, :]) for r in halves]
        gf_v = gf_ref[...]
        for r, out in zip(halves, outs):
            inv = lax.rsqrt(jnp.mean(out * out, axis=-1, keepdims=True) + EPS)
            diff = out * inv * gf_v - t_ref[r, :]
            dyf = diff * (1.0 / D_MODEL)
            a = dyf * gf_v
            dout = inv * a - out * (inv * inv * inv) * jnp.mean(a * out, axis=-1, keepdims=True)
            dout_ref[r, :] = dout
            doutb_ref[r, :] = dout.astype(BF16)
            part_ref[0:1, :] += jnp.sum(dyf * out * inv, axis=0, keepdims=True)
            part_ref[1:2, :] += jnp.sum(diff * diff, axis=0, keepdims=True)

    row = lambda n: pl.BlockSpec((OUT_ROWS, n), lambda i: (i, 0))
    return pl.pallas_call(
        body, name="out_proj_loss", grid=(SEQ // OUT_ROWS,),
        in_specs=[row(D_MODEL), row(D_MODEL), pl.BlockSpec((2 * D_MODEL, D_MODEL), lambda i: (0, 0)),
                  row(D_MODEL), row(D_MODEL), pl.BlockSpec((1, D_MODEL), lambda i: (0, 0))],
        out_specs=[row(D_MODEL), row(D_MODEL), pl.BlockSpec((8, D_MODEL), lambda i: (0, 0))],
        out_shape=[jax.ShapeDtypeStruct((SEQ, D_MODEL), F32),
                   jax.ShapeDtypeStruct((SEQ, D_MODEL), BF16),
                   jax.ShapeDtypeStruct((8, D_MODEL), F32)],
        compiler_params=_params(dimension_semantics=("arbitrary",)),
    )(y_pool, y_rec, w_out_g, x, target, gf)


def _grad_w_out(y_pool, y_rec, dout_b):
    blk = W_OUT_SHARD // 2
    per = D_MODEL // blk
    n = 2 * per

    def body(yp_ref, yr_ref, d_ref, p32_ref, p16_ref, send_ref, recv_ref, send_sems, recv_sems):
        j = pl.program_id(0)
        x, y, c = _place()

        def copy(u):
            return _remote(send_ref.at[u], recv_ref.at[u], send_sems, recv_sems, u, (x, y, 1 - c))

        for i in range(n):
            @pl.when(j == i)
            def _(i=i):
                res = _dot_tn((yp_ref if i < per else yr_ref)[...], d_ref[...])

                @pl.when(i % 2 == c)
                def _():
                    p32_ref[i // 2] = res

                @pl.when(i % 2 != c)
                def _():
                    send_ref[i // 2] = res
                    copy(i // 2).start()

        @pl.when(j == n - 1)
        def _():
            for u in range(N_SHARDS):
                copy(u).wait_recv()
                tot = p32_ref[u] + recv_ref[u]
                p32_ref[u] = tot
                p16_ref[u] = tot.astype(BF16)
            for u in range(N_SHARDS):
                copy(u).wait_send()

    whole = pl.BlockSpec((N_SHARDS, blk, D_MODEL), lambda j: (0, 0, 0))
    return pl.pallas_call(
        body, name="grad_w_out", grid=(n,),
        in_specs=[pl.BlockSpec((SEQ, blk), lambda j: (0, jnp.minimum(j, per - 1))),
                  pl.BlockSpec((SEQ, blk), lambda j: (0, jnp.maximum(j - per, 0))),
                  pl.BlockSpec((SEQ, D_MODEL), lambda j: (0, 0))],
        out_specs=[whole, whole],
        out_shape=[jax.ShapeDtypeStruct((N_SHARDS, blk, D_MODEL), F32),
                   jax.ShapeDtypeStruct((N_SHARDS, blk, D_MODEL), BF16)],
        scratch_shapes=[pltpu.VMEM((N_SHARDS, blk, D_MODEL), F32), pltpu.VMEM((N_SHARDS, blk, D_MODEL), F32),
                        pltpu.SemaphoreType.DMA((N_SHARDS,)), pltpu.SemaphoreType.DMA((N_SHARDS,))],
        compiler_params=_params(dimension_semantics=("arbitrary",)),
    )(y_pool, y_rec, dout_b)


def _pool_bwd(proj, dout_b, w_out_g, pw_g, pool_scale, exchanges):
    n = POOL_ROWS + POOL_HALO

    def body(u_ref, gate_ref, d_ref, wo_ref, pw_ref, sc_ref,
             dp_ref, dpw_ref, dsc_ref, dd_ref, ddw_ref):
        g = pl.program_id(0)
        dpw_ref[...] = jnp.zeros_like(dpw_ref)
        dsc_ref[...] = jnp.zeros_like(dsc_ref)

        def first(ii, _):
            chunks = [POOL_PAIR * ii + a for a in range(POOL_PAIR)]
            rs = [pl.ds(pl.multiple_of(i * POOL_ROWS, POOL_ROWS), POOL_ROWS) for i in chunks]
            diffs = [_pool_diff(u_ref, i, g) for i in chunks]
            dbs = [d.astype(BF16) for d, _ in diffs]
            mixed = [_dot(db, pw_ref[...]) for db in dbs]
            dys = [_dot_nt(d_ref[r, :], wo_ref[...]) for r in rs]
            sc = sc_ref[...]
            dmbs = []
            for r, m, dy in zip(rs, mixed, dys):
                gate = gate_ref[r, :]
                sg = _sig(gate)
                silu = gate * sg
                dp_ref[1, r, :] = (dy * m * sc * (sg * (1.0 + gate * (1.0 - sg)))).astype(BF16)
                dsc_ref[...] += jnp.sum(dy * silu * m, axis=0, keepdims=True)
                dmbs.append((dy * silu * sc).astype(BF16))
            for db, dmb in zip(dbs, dmbs):
                dpw_ref[...] += _dot_tn(db, dmb)
            dds = [_dot_nt(dmb, pw_ref[...]) for dmb in dmbs]
            for r, dd, (_, inv_count) in zip(rs, dds, diffs):
                dd_ref[r, :] = dd
                ddw_ref[r, :] = dd * inv_count
            return 0
        lax.fori_loop(0, SEQ // POOL_ROWS // POOL_PAIR, first, 0)

        def second(i, _):
            r0 = i * POOL_ROWS
            r = pl.ds(pl.multiple_of(r0, POOL_ROWS), POOL_ROWS)
            last = i == SEQ // POOL_ROWS - 1
            after = ddw_ref[pl.ds(pl.multiple_of(jnp.minimum(r0 + POOL_ROWS, SEQ - POOL_HALO), 8), POOL_HALO), :]
            after = jnp.where(last, 0.0, after)
            ext = jnp.concatenate([ddw_ref[r, :], after], axis=0)
            s = _window_sums(ext, g, lambda k: n - (1 << k))[:POOL_ROWS, :]
            dp_ref[0, r, :] = (s - dd_ref[r, :]).astype(BF16)
            return 0
        lax.fori_loop(0, SEQ // POOL_ROWS, second, 0)

    return _call(
        body, name="pool_bwd", grid=(N_GROUPS,),
        args=(proj, proj, dout_b, w_out_g, pw_g, pool_scale),
        in_specs=[pl.BlockSpec((SEQ, PG), lambda g: (0, g)),
                  pl.BlockSpec((SEQ, PG), lambda g: (0, N_GROUPS + g)),
                  pl.BlockSpec((SEQ, D_MODEL), lambda g: (0, 0)),
                  pl.BlockSpec((PG, D_MODEL), lambda g: (g, 0)),
                  pl.BlockSpec((None, PG, PG), lambda g: (g, 0, 0)),
                  pl.BlockSpec((1, PG), lambda g: (0, g))],
        out_specs=[pl.BlockSpec((2, SEQ, PG), lambda g: (0, 0, g)),
                   pl.BlockSpec((None, PG, PG), lambda g: (g, 0, 0)),
                   pl.BlockSpec((1, PG), lambda g: (0, g))],
        out_shape=[jax.ShapeDtypeStruct((2, SEQ, D_MODEL), BF16),
                   jax.ShapeDtypeStruct((N_GROUPS, PG, PG), F32),
                   jax.ShapeDtypeStruct((1, D_MODEL), F32)],
        scratch_shapes=[pltpu.VMEM((SEQ, PG), F32), pltpu.VMEM((SEQ, PG), F32)],
        exchanges=exchanges)


HALF_HEADS = N_HEADS // 2
HALF_COLS = HALF_HEADS * HEAD


def _rec_bwd(proj, o_raw, st_prev, dout_b, w_out_g, lb_logits, rec_g, consts, h0, name, exchanges):
    def body(q_ref, f_ref, i_ref, rg_ref, o_ref, stp_ref, d_ref, wo_ref, lb_ref, g_ref,
             w_ref, lowt_ref, sym_ref, sign_ref, tri_ref,
             dr_ref, part_ref, dst_ref):
        @pl.when(pl.program_id(1) == 0)
        def _():
            dst_ref[...] = jnp.zeros_like(dst_ref)
            part_ref[...] = jnp.zeros_like(part_ref)
        tril = (lax.broadcasted_iota(jnp.int32, (CHUNK, CHUNK), 0)
                > lax.broadcasted_iota(jnp.int32, (CHUNK, CHUNK), 1))
        lb = _lower_bound(lb_ref)
        grec = g_ref[...]
        dst = dst_ref[...]
        acc_grec = jnp.zeros((1, HEAD), F32)
        acc_lb = jnp.zeros((1, HEAD), F32)
        rows = lambda c: pl.ds(c * CHUNK, CHUNK)
        for c0 in reversed(range(0, REC_CHUNKS, REC_GROUP_BWD)):
            group = list(reversed(range(c0, c0 + REC_GROUP_BWD)))
            dys = [_dot_nt(d_ref[rows(c), :], wo_ref[...]) for c in group]
            dos = []
            for c, dy in zip(group, dys):
                rg = rg_ref[rows(c), :]
                o = o_ref[rows(c), :]
                sg = _sig(rg)
                silu = rg * sg
                inv = lax.rsqrt(jnp.mean(o * o, axis=-1, keepdims=True) + EPS)
                recn = o * inv
                dr_ref[3, rows(c), :] = (dy * recn * grec * (sg * (1.0 + rg * (1.0 - sg)))).astype(BF16)
                acc_grec = acc_grec + jnp.sum(dy * silu * recn, axis=0, keepdims=True)
                drecn = dy * silu * grec
                dos.append(inv * drecn - o * (inv * inv * inv) * jnp.mean(drecn * o, axis=-1, keepdims=True))
            gated = [_gates(q_ref[rows(c), :], f_ref[rows(c), :], lb) for c in group]
            g2s = [_dot3(w_ref[...], g) * LOG2E for (_, _, _, _, g) in gated]
            levels = [_level_factors(g2, qs, k, sign_ref) for g2, (qs, _, _, k, _) in zip(g2s, gated)]
            a_ts = []
            for lev in levels:
                a_t = jnp.zeros((CHUNK, CHUNK), F32)
                for l, (xl, _, _, _) in enumerate(lev):
                    a_t = a_t + _dot_nt(xl, xl) * lowt_ref[l]
                a_ts.append(a_t.astype(BF16))
            dobs = [do.astype(BF16) for do in dos]
            vbs = [i_ref[rows(c), :].astype(BF16) for c in group]
            d_syms = [jnp.where(tril, _dot_nt(dob, vb), _dot_nt(vb, dob)) for dob, vb in zip(dobs, vbs)]
            dqs_is, dk_is = [], []
            for lev, d_sym in zip(levels, d_syms):
                dqs_i = jnp.zeros((CHUNK, HEAD), F32)
                both_i = jnp.zeros((CHUNK, HEAD), F32)
                for l, (xl, xlo, e, up) in enumerate(lev):
                    z = d_sym * sym_ref[l]
                    tmp = _dot(z.astype(BF16), jnp.concatenate([xl, xlo], axis=-1))
                    tmp = (tmp[:, :HEAD] + tmp[:, HEAD:]) * e
                    dqs_i = dqs_i + jnp.where(up, tmp, 0.0)
                    both_i = both_i + tmp
                dqs_is.append(dqs_i)
                dk_is.append(both_i - dqs_i)
            e_gs = [jnp.exp2(g2) for g2 in g2s]
            e_revs = [jnp.exp2(g2[CHUNK - 1:CHUNK, :] - g2) for g2 in g2s]
            e_lasts = [jnp.exp2(g2[CHUNK - 1:CHUNK, :]) for g2 in g2s]
            q_gs = [qs * e_g for (qs, _, _, _, _), e_g in zip(gated, e_gs)]
            kdecs = [k * e_rev for (_, _, _, k, _), e_rev in zip(gated, e_revs)]
            dv12 = [_dot(a_t, dob) + jnp.sum(qs * k, axis=-1, keepdims=True) * do
                    for a_t, dob, do, (qs, _, _, k, _) in zip(a_ts, dobs, dos, gated)]
            dq_gs = [_dot(dob, stp_ref[c].astype(BF16)) for c, dob in zip(group, dobs)]
            steps = [_dot_tn(dob, q_g.astype(BF16)) for dob, q_g in zip(dobs, q_gs)]
            dsts = []
            for e_last, step in zip(e_lasts, steps):
                dsts.append(dst)
                dst = dst * e_last + step
            dstbs = [d.astype(BF16) for d in dsts]
            dv3 = [_dot_nt(kdec.astype(BF16), dstb) for kdec, dstb in zip(kdecs, dstbs)]
            dkdecs = [_dot(vb, dstb) for vb, dstb in zip(vbs, dstbs)]
            dbig_gs, dg_lasts, dqss, dks = [], [], [], []
            for i, c in enumerate(group):
                qs, _, _, k, _ = gated[i]
                de_last = jnp.sum(stp_ref[c] * dsts[i], axis=0, keepdims=True)
                ddiag = jnp.sum(dos[i] * i_ref[rows(c), :], axis=-1, keepdims=True)
                dqss.append(dqs_is[i] + ddiag * k + dq_gs[i] * e_gs[i])
                dks.append(dk_is[i] + ddiag * qs + dkdecs[i] * e_revs[i])
                dg_rev = dkdecs[i] * kdecs[i]
                dg_lasts.append(jnp.sum(dg_rev, axis=0, keepdims=True) + de_last * e_lasts[i])
                dbig_gs.append(qs * dqs_is[i] - k * dk_is[i] + dq_gs[i] * q_gs[i] - dg_rev)
            dgs = [_dot3(tri_ref[...], dbig_g) + dg_last for dbig_g, dg_last in zip(dbig_gs, dg_lasts)]
            for i, c in enumerate(group):
                _, sf, f, _, _ = gated[i]
                q = q_ref[rows(c), :]
                df = dgs[i] / f - dks[i]
                dr_ref[1, rows(c), :] = (df * (1.0 - lb) * sf * (1.0 - sf)).astype(BF16)
                acc_lb = acc_lb + jnp.sum(df * (1.0 - sf), axis=0, keepdims=True)
                sq = _sig(q)
                dr_ref[0, rows(c), :] = (dqss[i] * (sq * (1.0 + q * (1.0 - sq)))).astype(BF16)
                dr_ref[2, rows(c), :] = (dv12[i] + dv3[i]).astype(BF16)
        dst_ref[...] = dst
        part_ref[0:1, :] += acc_grec
        part_ref[1:2, :] += acc_lb

    rev = lambda b: N_REC_BLK - 1 - b
    sec = lambda n: pl.BlockSpec((REC_ROWS, HEAD), lambda h, b: (rev(b), n * SEC_BLK + h0 + h))
    col_in = pl.BlockSpec((REC_ROWS, HEAD), lambda h, b: (rev(b), h0 + h))
    vec_in = lambda rows: pl.BlockSpec((rows, HEAD), lambda h, b: (0, h0 + h))
    full = lambda a: pl.BlockSpec(a.shape, lambda h, b: (0,) * a.ndim)
    return _call(
        body, name=name, grid=(HALF_HEADS, N_REC_BLK),
        args=(proj, proj, proj, proj, o_raw, st_prev, dout_b, w_out_g, lb_logits, rec_g,
              consts["tri"], consts["low_t"], consts["sym"], consts["sign"], consts["tri_t"]),
        in_specs=[sec(2), sec(3), sec(4), sec(5), col_in,
                  pl.BlockSpec((None, REC_CHUNKS, HEAD, HEAD), lambda h, b: (h0 + h, rev(b), 0, 0)),
                  pl.BlockSpec((REC_ROWS, D_MODEL), lambda h, b: (rev(b), 0)),
                  pl.BlockSpec((HEAD, D_MODEL), lambda h, b: (SEC_BLK + h0 + h, 0)),
                  vec_in(2), vec_in(1)] + [full(consts[n]) for n in ("tri", "low_t", "sym", "sign", "tri_t")],
        out_specs=[pl.BlockSpec((4, REC_ROWS, HEAD), lambda h, b: (0, rev(b), h)),
                   pl.BlockSpec((8, HEAD), lambda h, b: (0, h))],
        out_shape=[jax.ShapeDtypeStruct((4, SEQ, HALF_COLS), BF16),
                   jax.ShapeDtypeStruct((8, HALF_COLS), F32)],
        scratch_shapes=[pltpu.VMEM((HEAD, HEAD), F32)],
        exchanges=exchanges)


def _w_in_block(w_ref, j):
    per_shard = W_IN_SHARD // COL_BLK
    return w_ref[j // per_shard, :, (j % per_shard) * COL_BLK:(j % per_shard + 1) * COL_BLK]


def _grad_x(dproj, w_in_g, x, g1, dout, exchanges):
    rows = 256
    n_blk = len(dproj)

    def body(*refs):
        dp_refs = refs[:n_blk]
        w_ref, x_ref, g_ref, dout_ref, dx_ref, part_ref = refs[n_blk:]

        @pl.when(pl.program_id(0) == 0)
        def _():
            part_ref[...] = jnp.zeros_like(part_ref)
        dh = jnp.zeros((rows, D_MODEL), F32)
        for j in range(n_blk):
            dh = dh + _dot_nt(dp_refs[j][...], _w_in_block(w_ref, j))
        xv = x_ref[...]
        inv = lax.rsqrt(jnp.mean(xv * xv, axis=-1, keepdims=True) + EPS)
        a = dh * g_ref[...]
        dx_ref[...] = (dout_ref[...] + inv * a
                       - xv * (inv * inv * inv) * jnp.mean(a * xv, axis=-1, keepdims=True))
        part_ref[0:1, :] += jnp.sum(dh * xv * inv, axis=0, keepdims=True)

    row = lambda: pl.BlockSpec((rows, D_MODEL), lambda i: (i, 0))
    dp_spec = lambda sec, cb: pl.BlockSpec((None, rows, COL_BLK), lambda i: (sec, i, cb))
    return _call(
        body, name="grad_x", grid=(SEQ // rows,),
        args=tuple(a for a, _, _ in dproj) + (w_in_g, x, g1, dout),
        in_specs=[dp_spec(sec, cb) for _, sec, cb in dproj]
                 + [pl.BlockSpec((N_SHARDS, D_MODEL, W_IN_SHARD), lambda i: (0, 0, 0)),
                    row(), pl.BlockSpec((1, D_MODEL), lambda i: (0, 0)), row()],
        out_specs=[row(), pl.BlockSpec((8, D_MODEL), lambda i: (0, 0))],
        out_shape=[jax.ShapeDtypeStruct((SEQ, D_MODEL), F32),
                   jax.ShapeDtypeStruct((8, D_MODEL), F32)],
        exchanges=exchanges)


def _grad_w_in(h, dp, blocks, name):
    n_blk = len(blocks)
    half = D_MODEL // 2
    pick = lambda vals: (lambda j: functools.reduce(lambda acc, iv: jnp.where(j == iv[0], iv[1], acc),
                                                     list(enumerate(vals))[1:], vals[0]))
    sec_of = pick([sec for sec, _ in blocks])
    cb_of = pick([cb for _, cb in blocks])

    def body(h_ref, dp_ref, p32_ref, p16_ref, send_ref, recv_ref, send_sems, recv_sems):
        j = pl.program_id(0)
        x, y, c = _place()
        cols = lambda cc: pl.ds(pl.multiple_of(cc * half, half), half)

        def copy(i):
            return _remote(send_ref.at[i], recv_ref.at[i], send_sems, recv_sems, i, (x, y, 1 - c))

        for i in range(n_blk):
            @pl.when(j == i)
            def _(i=i):
                send_ref[i] = _dot_tn(h_ref[:, cols(1 - c)], dp_ref[...])
                copy(i).start()
                p32_ref[i] = _dot_tn(h_ref[:, cols(c)], dp_ref[...])

        @pl.when(j == n_blk - 1)
        def _():
            for i in range(n_blk):
                copy(i).wait_recv()
                tot = p32_ref[i] + recv_ref[i]
                p32_ref[i] = tot
                p16_ref[i] = tot.astype(BF16)
            for i in range(n_blk):
                copy(i).wait_send()

    whole = pl.BlockSpec((n_blk, half, COL_BLK), lambda j: (0, 0, 0))
    return pl.pallas_call(
        body, name=name, grid=(n_blk,),
        in_specs=[pl.BlockSpec((SEQ, D_MODEL), lambda j: (0, 0)),
                  pl.BlockSpec((None, SEQ, COL_BLK), lambda j: (sec_of(j), 0, cb_of(j)))],
        out_specs=[whole, whole],
        out_shape=[jax.ShapeDtypeStruct((n_blk, half, COL_BLK), F32),
                   jax.ShapeDtypeStruct((n_blk, half, COL_BLK), BF16)],
        scratch_shapes=[pltpu.VMEM((n_blk, half, COL_BLK), F32), pltpu.VMEM((n_blk, half, COL_BLK), F32),
                        pltpu.SemaphoreType.DMA((n_blk,)), pltpu.SemaphoreType.DMA((n_blk,))],
        compiler_params=_params(dimension_semantics=("arbitrary",)),
    )(h, dp)


def _add_units(grad, recv, place, tile, name):
    n, rows, cols = grad.shape
    per_half = rows // 2 // tile

    def body(place_ref, g_ref, r_ref, o32_ref, o16_ref):
        v = g_ref[...] + r_ref[...]
        o32_ref[...] = v
        o16_ref[...] = v.astype(BF16)

    blk = lambda f: pl.BlockSpec((None, tile, cols), f)
    out = lambda s, i, p: (s, i, 0)
    return pl.pallas_call(
        body, name=name,
        grid_spec=pltpu.PrefetchScalarGridSpec(
            num_scalar_prefetch=1, grid=(n, per_half),
            in_specs=[blk(lambda s, i, p: (s, p[0] * per_half + i, 0)), blk(out)],
            out_specs=[blk(out), blk(out)]),
        out_shape=[jax.ShapeDtypeStruct(recv.shape, F32), jax.ShapeDtypeStruct(recv.shape, BF16)],
        compiler_params=_params(dimension_semantics=("arbitrary", "arbitrary")),
    )(place, grad, recv)


def _sum_units(part32, recv16, place, owners, tile, name):
    n, half, cols = part32.shape
    per_half = half // tile
    table = np.array([[sum(o == chip for o in owners)] + sorted(range(n), key=lambda j: (owners[j] != chip, j))
                      for chip in range(N_SHARDS)], np.int32)
    sched = jnp.concatenate([place[:1], jnp.asarray(table)[place[1]]])

    def block(k, i, p):
        live = k < p[1]
        unit = p[2 + jnp.minimum(k, jnp.maximum(p[1] - 1, 0))]
        return unit, jnp.where(live, i, per_half - 1)

    def body(sched_ref, p_ref, r_ref, o_ref):
        @pl.when(pl.program_id(0) < sched_ref[1])
        def _():
            acc = p_ref[...]
            for j in range(len(CHIP_FLIPS)):
                acc = acc + r_ref[j].astype(F32)
            o_ref[...] = acc

    return pl.pallas_call(
        body, name=name,
        grid_spec=pltpu.PrefetchScalarGridSpec(
            num_scalar_prefetch=1, grid=(n, per_half),
            in_specs=[pl.BlockSpec((None, tile, cols), lambda k, i, p: (*block(k, i, p), 0)),
                      pl.BlockSpec((None, len(CHIP_FLIPS), tile, cols),
                                   lambda k, i, p: (block(k, i, p)[0], 0, block(k, i, p)[1], 0))],
            out_specs=pl.BlockSpec((None, tile, cols),
                                   lambda k, i, p: (block(k, i, p)[0], p[0] * per_half + block(k, i, p)[1], 0))),
        out_shape=jax.ShapeDtypeStruct((n, 2 * half, cols), F32),
        compiler_params=_params(dimension_semantics=("arbitrary", "arbitrary")),
    )(sched, part32, recv16)


def _adamw_math(w, g, m, v):
    m = ADAM_B1 * m + (1.0 - ADAM_B1) * g
    v = ADAM_B2 * v + (1.0 - ADAM_B2) * (g * g)
    m_hat = m / (1.0 - ADAM_B1 ** ADAM_STEP)
    v_hat = v / (1.0 - ADAM_B2 ** ADAM_STEP)
    delta = -ADAM_LR * (m_hat / (jnp.sqrt(v_hat) + ADAM_EPS) + ADAM_WD * w)
    return delta, m, v


def _adamw_units(w, m, v, grads, pick, name):
    rows, cols = w.shape
    bc = grads[0].shape[-1]
    tile = min(rows, 256)
    n_g = len(grads)

    def body(pick_ref, w_ref, m_ref, v_ref, *refs):
        g_refs, (g_out, d_ref, nm_ref, nv_ref) = refs[:n_g], refs[n_g:]
        p = pl.program_id(0)
        for a in range(n_g):
            @pl.when(pick_ref[0, p] == a)
            def _(a=a):
                g = g_refs[a][...]
                g_out[...] = g
                d_ref[...], nm_ref[...], nv_ref[...] = _adamw_math(w_ref[...], g, m_ref[...], v_ref[...])

    blk = pl.BlockSpec((tile, bc), lambda p, i, pick: (i, p))

    def g_spec(a):
        return pl.BlockSpec((None, tile, bc),
                            lambda p, i, pick: (jnp.where(pick[0, p] == a, pick[1, p], 0),
                                                jnp.where(pick[0, p] == a, i, 0), 0))

    return pl.pallas_call(
        body, name=name,
        grid_spec=pltpu.PrefetchScalarGridSpec(
            num_scalar_prefetch=1, grid=(cols // bc, rows // tile),
            in_specs=[blk] * 3 + [g_spec(a) for a in range(n_g)],
            out_specs=[blk] * 4),
        out_shape=[jax.ShapeDtypeStruct(w.shape, F32)] * 4,
        compiler_params=_params(dimension_semantics=("arbitrary", "arbitrary")),
    )(pick, w, m, v, *grads)


ROW_NORM1, ROW_SCALE, ROW_LB, ROW_REC, ROW_FINAL, ROW_LOSS = 0, 1, 2, 4, 5, 6


SMALL_ROWS = (ROW_NORM1, ROW_SCALE, ROW_LB, ROW_REC, ROW_FINAL)


def _small_update(parts, gathered, params):
    n_p = len(params)

    def body(own_ref, p_ref, *refs):
        ins, loss_ref, outs = refs[:3 * n_p], refs[3 * n_p], refs[3 * n_p + 1:]
        x, y, c = _place()
        me = 4 * x + 2 * y + c
        slot = lambda d: jnp.where(me == d, own_ref[...], p_ref[d])
        tot = slot(0)
        for d in range(1, 8):
            tot = tot + slot(d)
        for i, r in enumerate(SMALL_ROWS):
            w = ins[3 * i][...]
            g = tot[r:r + 1, :]
            if r == ROW_LB:
                mx = jnp.maximum(w[0:1, :], w[1:2, :])
                e0 = jnp.exp(w[0:1, :] - mx)
                e1 = jnp.exp(w[1:2, :] - mx)
                lb = e0 / (e0 + e1)
                g = g * lb * (1.0 - lb)
                g = jnp.concatenate([g, -g], axis=0)
            outs[4 * i][...] = g
            outs[4 * i + 1][...], outs[4 * i + 2][...], outs[4 * i + 3][...] = _adamw_math(
                w, g, ins[3 * i + 1][...], ins[3 * i + 2][...])
        loss_ref[...] = (0.5 / D_MODEL) * jnp.sum(tot[ROW_LOSS:ROW_LOSS + 1, :], axis=-1, keepdims=True)

    flat = [a for wmv in params for a in wmv]
    return pl.pallas_call(
        body, name="small_update",
        out_shape=[jax.ShapeDtypeStruct((1, 1), F32)]
                  + [jax.ShapeDtypeStruct(w.shape, F32) for w, _, _ in params for _ in range(4)],
        compiler_params=_params(),
    )(parts, gathered, *flat)


SHARD_OWNERS = tuple(range(N_SHARDS))
BLOCKS_POOL = (0, 1, 2, 3)
BLOCKS_A = (4, 6, 8, 10)
BLOCKS_B = (5, 7, 9, 11)
BLOCK_GROUPS = (BLOCKS_POOL, BLOCKS_A, BLOCKS_B)


def _block_owners(blocks):
    return tuple(j // (W_IN_SHARD // COL_BLK) for j in blocks)


def kernel(x, norm1_g, w_in, pool_w, pool_scale, lb_logits, rec_norm_g, w_out, final_norm_g, loss_target, m_norm1_g, m_w_in, m_pool_w, m_pool_scale, m_lb_logits, m_rec_norm_g, m_w_out, m_final_norm_g, v_norm1_g, v_w_in, v_pool_w, v_pool_scale, v_lb_logits, v_rec_norm_g, v_w_out, v_final_norm_g):
    xi, yi, ci = _place()
    chip = 2 * xi + yi
    place = jnp.stack([ci, chip]).astype(jnp.int32)
    pw_rows = N_GROUPS * PW_SHARD
    flat_pw = lambda a: a.reshape(pw_rows, PG)
    x2, target, gf = x[0], loss_target[0], final_norm_g.reshape(1, D_MODEL)
    consts = {n: jnp.asarray(a, BF16 if n.startswith("tri") else F32) for n, a in _chunk_constants().items()}

    proj, h, w_in_g = _in_proj(x2, norm1_g, _cast_own(w_in[0], place, "cast_w_in"), place)
    (y_rec, o_raw, st_prev), ((w_out_g, pw_g),) = _rec_fwd(
        proj, lb_logits, rec_norm_g, consts,
        [_ex_gather([_cast_own(w_out[0], place, "cast_w_out"), _cast_own(flat_pw(pool_w), place, "cast_pool_w")])])
    w_out_g = w_out_g.reshape(2 * D_MODEL, D_MODEL)
    pw_full = pw_g.reshape(N_SHARDS, N_GROUPS, PW_SHARD, PG).transpose(1, 0, 2, 3).reshape(N_GROUPS, PG, PG)
    y_pool = _pool_fwd(proj, pw_full, pool_scale)
    dout, dout_b, part_out = _out_proj_loss(y_pool, y_rec, w_out_g, x2, target, gf)

    p_out32, p_out16 = _grad_w_out(y_pool, y_rec, dout_b)
    (dpool, gpw, dscale), ((rb_out,),) = _pool_bwd(proj, dout_b, w_out_g, pw_full, pool_scale,
                                                   [_ex_send([p_out16], [SHARD_OWNERS])])
    g_out = _sum_units(p_out32, rb_out, place, SHARD_OWNERS, 256, "sum_w_out")
    gpw = gpw.reshape(N_GROUPS, N_SHARDS, PW_SHARD, PG).transpose(1, 0, 2, 3).reshape(N_SHARDS, pw_rows, PG)
    p_inp32, p_inp16 = _grad_w_in(h, dpool, [(0, 0), (0, 1), (1, 0), (1, 1)], "grad_w_in_pool")

    pool_owners, a_owners, b_owners = (_block_owners(b) for b in BLOCK_GROUPS)
    rec_args = (proj, o_raw, st_prev, dout_b, w_out_g, lb_logits, rec_norm_g, consts)
    (drec_a, part_a), ((rb_inp,), (ra_pw,)) = _rec_bwd(
        *rec_args, 0, "rec_bwd_a", [_ex_send([p_inp16], [pool_owners], units=[(0, 1)]), _ex_swap([gpw])])
    p_pw32, p_pw16 = _add_units(gpw, ra_pw, place, 128, "add_pool_w")
    rec_blocks = [(n, 0) for n in range(4)]
    p_ina32, p_ina16 = _grad_w_in(h, drec_a, rec_blocks, "grad_w_in_a")

    (drec_b, part_b), ((rb_inp, rb_ina, rb_pw),) = _rec_bwd(
        *rec_args, HALF_HEADS, "rec_bwd_b",
        [_ex_send([p_inp16, p_ina16, p_pw16], [pool_owners, a_owners, SHARD_OWNERS],
                  units=[(2, 3), (0, 1, 2, 3), (0, 1, 2, 3)], landed=[rb_inp, None, None])])
    g_inp = _sum_units(p_inp32, rb_inp, place, pool_owners, 256, "sum_w_in_pool")
    g_ina = _sum_units(p_ina32, rb_ina, place, a_owners, 256, "sum_w_in_a")
    g_pw = _sum_units(p_pw32, rb_pw, place, SHARD_OWNERS, 128, "sum_pool_w")
    p_inb32, p_inb16 = _grad_w_in(h, drec_b, rec_blocks, "grad_w_in_b")

    dproj = ([(dpool, 0, 0), (dpool, 0, 1), (dpool, 1, 0), (dpool, 1, 1)]
             + [(d, n, 0) for n in range(4) for d in (drec_a, drec_b)])
    (dx, part_x), ((rb_inb,),) = _grad_x(dproj, w_in_g, x2, norm1_g, dout, [_ex_send([p_inb16], [b_owners])])
    g_inb = _sum_units(p_inb32, rb_inb, place, b_owners, 256, "sum_w_in_b")
    zero = jnp.zeros((1, D_MODEL), F32)
    part_rec = jnp.concatenate([part_a, part_b], axis=1)
    parts = jnp.concatenate([part_x[0:1], dscale, part_rec[1:2], zero, part_rec[0:1], part_out[0:1],
                             part_out[1:2], zero], axis=0)
    _, ((g_out, g_pw, g_inp, g_ina, g_inb), (gathered,)) = _call(
        None, name="join_halves",
        exchanges=[_ex_join([g_out, g_pw, g_inp, g_ina, g_inb],
                            [SHARD_OWNERS, SHARD_OWNERS, pool_owners, a_owners, b_owners]),
                   _ex_gather_small(parts)])

    group_of = np.zeros((D_PROJ // COL_BLK,), np.int32)
    index_of = np.zeros((D_PROJ // COL_BLK,), np.int32)
    for gi, blocks in enumerate(BLOCK_GROUPS):
        for i, j in enumerate(blocks):
            group_of[j], index_of[j] = gi, i
    per_shard = W_IN_SHARD // COL_BLK
    pick_in = jnp.stack([lax.dynamic_slice(jnp.asarray(group_of), (per_shard * chip,), (per_shard,)),
                         lax.dynamic_slice(jnp.asarray(index_of), (per_shard * chip,), (per_shard,))])
    pick_own = jnp.stack([jnp.zeros((1,), jnp.int32), chip.reshape(1).astype(jnp.int32)])
    big = [_adamw_units(w_in[0], m_w_in[0], v_w_in[0], [g_inp, g_ina, g_inb], pick_in, "adamw_w_in"),
           _adamw_units(w_out[0], m_w_out[0], v_w_out[0], [g_out], pick_own, "adamw_w_out"),
           _adamw_units(flat_pw(pool_w), flat_pw(m_pool_w), flat_pw(v_pool_w), [g_pw], pick_own, "adamw_pool_w")]

    row = lambda a: a.reshape(1, D_MODEL)
    loss, *small = _small_update(parts, gathered, [
        (norm1_g, m_norm1_g, v_norm1_g), (pool_scale, m_pool_scale, v_pool_scale),
        (lb_logits, m_lb_logits, v_lb_logits), (rec_norm_g, m_rec_norm_g, v_rec_norm_g),
        (row(final_norm_g), row(m_final_norm_g), row(v_final_norm_g))])

    def leaves(k):
        norm1, scale, lb, rec, final = (small[4 * i + k] for i in range(len(SMALL_ROWS)))
        return (norm1, big[0][k][None], big[2][k].reshape(pool_w.shape), scale, lb, rec,
                big[1][k][None], final.reshape(D_MODEL))

    return (loss.reshape(()), dx[None], *leaves(0), *leaves(1), *leaves(2), *leaves(3))
```

```python
import functools

import numpy as np
import jax
import jax.numpy as jnp
from jax import lax
from jax.experimental import pallas as pl
from jax.experimental.pallas import tpu as pltpu

F32 = jnp.float32
BF16 = jnp.bfloat16

SEQ = 2048
D_MODEL = 1024
D_PROJ = 6144
N_SEC = 6
N_GROUPS = 4
PG = 256
N_HEADS = 8
HEAD = 128
CHUNK = 64
N_LEVELS = 6
N_SHARDS = 4
W_IN_SHARD = D_PROJ // N_SHARDS
W_OUT_SHARD = 2048 // N_SHARDS
PW_SHARD = PG // N_SHARDS
COL_BLK = 512
EPS = 1e-6

ADAM_LR = 0.001
ADAM_B1 = 0.9
ADAM_B2 = 0.999
ADAM_EPS = 1e-08
ADAM_WD = 0.01
ADAM_STEP = 10

V7X_VMEM_LIMIT = 56 * 1024 * 1024
MESH = pl.DeviceIdType.MESH


def _params(**kw):
    return pltpu.CompilerParams(vmem_limit_bytes=V7X_VMEM_LIMIT, **kw)


def _sig(x):
    return 1.0 / (1.0 + jnp.exp(-x))


def _dot(a, b):
    return jnp.dot(a, b, preferred_element_type=F32)


def _dot_nt(a, b):
    return lax.dot_general(a, b, (((1,), (1,)), ((), ())), preferred_element_type=F32)


def _dot_tn(a, b):
    return lax.dot_general(a, b, (((0,), (0,)), ((), ())), preferred_element_type=F32)


def _split3(a):
    p1 = a.astype(BF16)
    r1 = a - p1.astype(F32)
    p2 = r1.astype(BF16)
    p3 = (r1 - p2.astype(F32)).astype(BF16)
    return jnp.concatenate([p1, p2, p3], axis=-1)


def _dot3(w01, a):
    n = a.shape[-1]
    r = _dot(w01, _split3(a))
    return r[:, :n] + r[:, n:2 * n] + r[:, 2 * n:]


def _chunk_constants():
    j = np.arange(CHUNK)
    tt, ss = np.meshgrid(j, j, indexing="ij")
    x = tt ^ ss
    hb = np.full((CHUNK, CHUNK), -1, np.int32)
    for l in range(N_LEVELS):
        hb[x >= (1 << l)] = l
    sym = np.stack([(hb == l) for l in range(N_LEVELS)]).astype(np.float32)
    low = sym * (tt > ss)
    sign = np.stack([np.where((j >> l) & 1, 1.0, -1.0) for l in range(N_LEVELS)]).astype(np.float32)
    sign = np.ascontiguousarray(np.broadcast_to(sign[:, :, None], (N_LEVELS, CHUNK, HEAD)))
    tri = (ss <= tt).astype(np.float32)
    return dict(tri=tri, tri_t=np.ascontiguousarray(tri.T), low=low,
                low_t=np.ascontiguousarray(low.transpose(0, 2, 1)), sym=sym, sign=sign)


def _in_proj(x, g1, w_slots, place):
    n_col = D_PROJ // COL_BLK
    per_shard = W_IN_SHARD // COL_BLK
    rows = 1024
    half_rows = D_MODEL // 2
    quarter_rows = D_MODEL // 4
    FLIPS = (0, 2, 1, 3)
    ORDER = ([(0, p) for p in range(per_shard)] + [(m, p) for p in range(per_shard) for m in (1, 2)]
             + [(3, p) for p in range(per_shard)])

    def shard_at(m, chip):
        return chip ^ FLIPS[m]

    def pick(vals, t):
        return functools.reduce(lambda acc, iv: jnp.where(t == iv[0], iv[1], acc), list(enumerate(vals))[1:], vals[0])

    def body(place_ref, x_ref, g_ref, w_in_ref, proj_ref, h_ref, w_ref, wbuf, load_sems, send_sems, recv_sems):
        t = pl.program_id(0)
        x_, y_, c = _place()
        chip = 2 * x_ + y_
        me, other_core = (x_, y_, c), (x_, y_, 1 - c)
        x_nbr, y_nbr = (1 - x_, y_, c), (x_, 1 - y_, c)

        def rows_of(half, q=None):
            if q is None:
                return pl.ds(pl.multiple_of(half * half_rows, half_rows), half_rows)
            return pl.ds(pl.multiple_of(half * half_rows + q * quarter_rows, quarter_rows), quarter_rows)

        def block(m, p, r):
            return w_ref.at[shard_at(m, chip), p, r, :]

        def copy(k, ref, to):
            return _remote(ref, ref, send_sems, recv_sems, k, to)

        direct = lambda n, p, to: copy(3 * n + p, block(0, p, rows_of(c)), to)
        relay = lambda n, p, to: copy(6 + 3 * n + p, block(1 + n, p, rows_of(c, n)), to)
        arrived = lambda m, p: ([copy(3 * (m - 1) + p, block(m, p, rows_of(c)), me)] if m < 3 else
                                [copy(6 + 3 * n + p, block(3, p, rows_of(c, n)), me) for n in (0, 1)])
        passed_on = lambda m, p, half, to: copy(9 + 3 * m + p, block(m, p, rows_of(half)), to)

        def load(m, p, slot):
            return pltpu.make_async_copy(w_ref.at[shard_at(m, chip), p], wbuf.at[slot], load_sems.at[slot])

        def prepare(m, p):
            for cp in arrived(m, p):
                cp.wait_recv()
            passed_on(m, p, c, other_core).start()
            if m < 3:
                relay(m - 1, p, y_nbr if m == 1 else x_nbr).start()

        @pl.when(t == 0)
        def _():
            for p in range(per_shard):
                direct(0, p, x_nbr).start()
                direct(1, p, y_nbr).start()
            for p in range(per_shard):
                load(0, p, p).start()

            def norm(i, _):
                r = pl.ds(pl.multiple_of(i * rows, rows), rows)
                xv = x_ref[r, :]
                inv = lax.rsqrt(jnp.mean(xv * xv, axis=-1, keepdims=True) + EPS)
                h_ref[r, :] = (xv * inv * g_ref[...]).astype(BF16)
                return 0
            lax.fori_loop(0, SEQ // rows, norm, 0)

        for step, (m, p) in enumerate(ORDER):
            @pl.when(t == step)
            def _(step=step, m=m, p=p):
                slot = step % per_shard
                if m > 0:
                    passed_on(m, p, 1 - c, me).wait_recv()
                    load(m, p, slot).start()
                if step + 1 < n_col and ORDER[step + 1][0] > 0:
                    prepare(*ORDER[step + 1])
                load(m, p, slot).wait()

                def mm(i, _):
                    r = pl.ds(pl.multiple_of(i * rows, rows), rows)
                    proj_ref[r, :] = _dot(h_ref[r, :], wbuf[slot])
                    return 0
                lax.fori_loop(0, SEQ // rows, mm, 0)

        @pl.when(t == n_col - 1)
        def _():
            for p in range(per_shard):
                sent = [direct(0, p, x_nbr), direct(1, p, y_nbr), relay(0, p, y_nbr), relay(1, p, x_nbr)]
                for cp in sent + [passed_on(m, p, c, other_core) for m in (1, 2, 3)]:
                    cp.wait_send()

    return pl.pallas_call(
        body, name="in_proj",
        grid_spec=pltpu.PrefetchScalarGridSpec(
            num_scalar_prefetch=1, grid=(n_col,),
            in_specs=[pl.BlockSpec((SEQ, D_MODEL), lambda t, p: (0, 0)),
                      pl.BlockSpec((1, D_MODEL), lambda t, p: (0, 0)),
                      pl.BlockSpec(memory_space=pl.ANY)],
            out_specs=[pl.BlockSpec((SEQ, COL_BLK),
                                    lambda t, p: (0, per_shard * (p[1] ^ pick([FLIPS[m] for m, _ in ORDER], t))
                                                  + pick([b for _, b in ORDER], t))),
                       pl.BlockSpec((SEQ, D_MODEL), lambda t, p: (0, 0)),
                       pl.BlockSpec(memory_space=pl.ANY)],
            scratch_shapes=[pltpu.VMEM((per_shard, D_MODEL, COL_BLK), BF16),
                            pltpu.SemaphoreType.DMA((per_shard,)),
                            pltpu.SemaphoreType.DMA((21,)), pltpu.SemaphoreType.DMA((21,))]),
        out_shape=[jax.ShapeDtypeStruct((SEQ, D_PROJ), F32),
                   jax.ShapeDtypeStruct((SEQ, D_MODEL), BF16),
                   jax.ShapeDtypeStruct(w_slots.shape, BF16)],
        input_output_aliases={3: 2},
        compiler_params=_params(dimension_semantics=("arbitrary",)),
    )(place, x, g1, w_slots)


POOL_ROWS = 256
POOL_HALO = 16
POOL_PAIR = 2


def _window_sums(ext, g, shift_of):
    s = ext
    for k in range(N_GROUPS):
        s = jnp.where(k <= g, s + pltpu.roll(s, shift_of(k), 0), s)
    return s


def _pool_diff(u_ref, i, g):
    n = POOL_ROWS + POOL_HALO
    r0 = i * POOL_ROWS
    cur = u_ref[pl.ds(pl.multiple_of(r0, POOL_ROWS), POOL_ROWS), :]
    before = u_ref[pl.ds(pl.multiple_of(jnp.maximum(r0 - POOL_HALO, 0), 8), POOL_HALO), :]
    before = jnp.where(i > 0, before, 0.0)
    ext = jnp.concatenate([before, cur], axis=0)
    s = _window_sums(ext, g, lambda k: 1 << k)[POOL_HALO:, :]
    t = r0 + lax.broadcasted_iota(jnp.int32, (POOL_ROWS, 1), 0)
    width = (2 << g).astype(F32)
    inv_count = 1.0 / jnp.minimum((t + 1).astype(F32), width)
    return s * inv_count - cur, inv_count


def _pool_fwd(proj, pw_g, pool_scale):
    def body(u_ref, gate_ref, pw_ref, sc_ref, y_ref):
        g = pl.program_id(0)

        def step(ii, _):
            chunks = [POOL_PAIR * ii + a for a in range(POOL_PAIR)]
            ds = [_pool_diff(u_ref, i, g)[0].astype(BF16) for i in chunks]
            mixed = [_dot(d, pw_ref[...]) for d in ds]
            for i, m in zip(chunks, mixed):
                r = pl.ds(pl.multiple_of(i * POOL_ROWS, POOL_ROWS), POOL_ROWS)
                gate = gate_ref[r, :]
                y_ref[r, :] = (m * sc_ref[...] * (gate * _sig(gate))).astype(BF16)
            return 0
        lax.fori_loop(0, SEQ // POOL_ROWS // POOL_PAIR, step, 0)

    return pl.pallas_call(
        body, name="pool_fwd", grid=(N_GROUPS,),
        in_specs=[pl.BlockSpec((SEQ, PG), lambda g: (0, g)),
                  pl.BlockSpec((SEQ, PG), lambda g: (0, N_GROUPS + g)),
                  pl.BlockSpec((None, PG, PG), lambda g: (g, 0, 0)),
                  pl.BlockSpec((1, PG), lambda g: (0, g))],
        out_specs=pl.BlockSpec((SEQ, PG), lambda g: (0, g)),
        out_shape=jax.ShapeDtypeStruct((SEQ, D_MODEL), BF16),
        compiler_params=_params(dimension_semantics=("arbitrary",)),
    )(proj, proj, pw_g, pool_scale)


REC_ROWS = 1024
REC_CHUNKS = REC_ROWS // CHUNK
N_REC_BLK = SEQ // REC_ROWS
REC_GROUP = REC_CHUNKS
REC_GROUP_BWD = REC_CHUNKS
SEC_BLK = D_MODEL // HEAD


def _lower_bound(lb_ref):
    l0 = lb_ref[0:1, :]
    l1 = lb_ref[1:2, :]
    mx = jnp.maximum(l0, l1)
    e0 = jnp.exp(l0 - mx)
    e1 = jnp.exp(l1 - mx)
    return e0 / (e0 + e1)


def _gates(q, fl, lb):
    qs = q * _sig(q)
    sf = _sig(fl)
    f = lb + (1.0 - lb) * sf
    return qs, sf, f, 1.0 - f, jnp.log(f)


LOG2E = 1.4426950408889634


def _level_factors(g2, qs, k, sign_ref):
    t = lax.broadcasted_iota(jnp.int32, (CHUNK, HEAD), 0)
    row = lambda r, n: jnp.broadcast_to(g2[r:r + 1, :], (n, HEAD))
    out = []
    for l in range(N_LEVELS):
        m = 1 << l
        if l == 0:
            g_mid = jnp.where((t & 1) == 1, pltpu.roll(g2, 1, 0), g2)
        elif l == 1:
            low = (t & 7) < 4
            g_mid = jnp.concatenate([jnp.where(low[:8], row(8 * i + 1, 8), row(8 * i + 5, 8))
                                     for i in range(CHUNK // 8)], axis=0)
        else:
            g_mid = jnp.concatenate([row(b * 2 * m + m - 1, 2 * m) for b in range(CHUNK // (2 * m))], axis=0)
        sgn = sign_ref[l]
        up = sgn > 0.0
        e = jnp.exp2((g2 - g_mid) * sgn)
        x = jnp.where(up, qs, k) * e
        hi = x.astype(BF16)
        out.append((hi, (x - hi.astype(F32)).astype(BF16), e, up))
    return out


CHIP_FLIPS = ((1, 0), (0, 1), (1, 1))
HBM = pl.BlockSpec(memory_space=pl.ANY)


def _place():
    return lax.axis_index("x"), lax.axis_index("y"), lax.axis_index("c")


def _remote(src, dst, send_sems, recv_sems, k, to):
    return pltpu.make_async_remote_copy(src_ref=src, dst_ref=dst, send_sem=send_sems.at[k],
                                        recv_sem=recv_sems.at[k], device_id=to, device_id_type=MESH)


def _half_rows(ref, c):
    half = ref.shape[-2] // 2
    rows = pl.ds(pl.multiple_of(c * half, half), half)
    return ref.at[:, rows, :] if len(ref.shape) == 3 else ref.at[rows, :]


class _Exchange:
    def __init__(self, inputs, out_shapes, n_sems, start, finish, aliases=None):
        self.inputs, self.out_shapes, self.n_sems = list(inputs), list(out_shapes), n_sems
        self.start, self.finish, self.aliases = start, finish, dict(aliases or {})


def _ex_swap(grads):
    def copies(ins, outs, send, recv):
        x, y, c = _place()
        return [_remote(_half_rows(g, 1 - c), o, send, recv, t, (x, y, 1 - c))
                for t, (g, o) in enumerate(zip(ins, outs))]

    def start(*refs):
        for cp in copies(*refs):
            cp.start()

    def finish(*refs):
        cps = copies(*refs)
        for cp in cps:
            cp.wait_recv()
        for cp in cps:
            cp.wait_send()

    shapes = [jax.ShapeDtypeStruct((a.shape[0], a.shape[1] // 2, a.shape[2]), F32) for a in grads]
    return _Exchange(grads, shapes, len(grads), start, finish)


def _ex_send(parts16, owners, units=None, landed=None):
    n_t = len(parts16)
    units = units or [tuple(range(len(o))) for o in owners]
    landed = landed or [None] * n_t
    given = [t for t in range(n_t) if landed[t] is not None]

    def each(ins, outs, send, recv, to_sender, to_owner):
        x, y, c = _place()
        k = 0
        for t, own in enumerate(owners):
            for j in units[t]:
                for r, (fx, fy) in enumerate(CHIP_FLIPS):
                    tx, ty = x ^ fx, y ^ fy
                    cp = _remote(ins[t].at[j], outs[t].at[j, r], send, recv, k, (tx, ty, c))
                    if to_sender is not None:
                        pl.when(2 * tx + ty == own[j])(functools.partial(to_sender, cp))
                    if to_owner is not None:
                        pl.when(2 * x + y == own[j])(functools.partial(to_owner, cp))
                    k += 1

    def start(*refs):
        each(*refs, lambda cp: cp.start(), None)

    def finish(*refs):
        each(*refs, None, lambda cp: cp.wait_recv())
        each(*refs, lambda cp: cp.wait_send(), None)

    shapes = [jax.ShapeDtypeStruct((a.shape[0], len(CHIP_FLIPS)) + a.shape[1:], BF16) for a in parts16]
    return _Exchange(list(parts16) + [landed[t] for t in given], shapes,
                     len(CHIP_FLIPS) * sum(len(u) for u in units), start, finish,
                     aliases={n_t + i: t for i, t in enumerate(given)})


def _ex_join(units, owners):
    def each(ins, outs, send, recv, fn):
        x, y, c = _place()
        k = 0
        for t, own in enumerate(owners):
            for j, o in enumerate(own):
                def half(cc, to, u=outs[t].at[j], k=k):
                    return _remote(_half_rows(u, cc), _half_rows(u, cc), send, recv, k, to)
                mine = functools.partial(half, c, (x, y, 1 - c))
                theirs = functools.partial(half, 1 - c, (x, y, c))
                pl.when(2 * x + y == o)(functools.partial(fn, mine, theirs))
                k += 1

    def start(*refs):
        each(*refs, lambda mine, theirs: mine().start())

    def finish(*refs):
        each(*refs, lambda mine, theirs: theirs().wait_recv())
        each(*refs, lambda mine, theirs: mine().wait_send())

    shapes = [jax.ShapeDtypeStruct(a.shape, F32) for a in units]
    return _Exchange(units, shapes, sum(len(o) for o in owners), start, finish,
                     aliases={t: t for t in range(len(units))})


def _ex_gather(slots):
    n_t = len(slots)
    n_fl = len(CHIP_FLIPS)

    def piece(ref, shard, half):
        return _half_rows(ref.at[shard], half)

    def first(outs, send, recv):
        x, y, c = _place()
        s = 2 * x + y
        return [_remote(piece(outs[t], s, c), piece(outs[t], s, c), send, recv, n_t * j + t, (x ^ fx, y ^ fy, c))
                for j, (fx, fy) in enumerate(CHIP_FLIPS) for t in range(n_t)]

    def start(ins, outs, send, recv):
        for cp in first(outs, send, recv):
            cp.start()

    def finish(ins, outs, send, recv):
        x, y, c = _place()
        passed = []
        for j, (fx, fy) in enumerate(CHIP_FLIPS):
            sj = 2 * (x ^ fx) + (y ^ fy)
            for t in range(n_t):
                k = n_t * j + t
                _remote(piece(outs[t], sj, c), piece(outs[t], sj, c), send, recv, k, (x, y, c)).wait_recv()
                cp = _remote(piece(outs[t], sj, c), piece(outs[t], sj, c), send, recv, n_t * n_fl + k, (x, y, 1 - c))
                cp.start()
                passed.append(cp)
        for j, (fx, fy) in enumerate(CHIP_FLIPS):
            sj = 2 * (x ^ fx) + (y ^ fy)
            for t in range(n_t):
                k = n_t * n_fl + n_t * j + t
                _remote(piece(outs[t], sj, 1 - c), piece(outs[t], sj, 1 - c), send, recv, k, (x, y, c)).wait_recv()
        for cp in first(outs, send, recv) + passed:
            cp.wait_send()

    shapes = [jax.ShapeDtypeStruct(a.shape, BF16) for a in slots]
    return _Exchange(slots, shapes, 2 * n_t * n_fl, start, finish, aliases={t: t for t in range(n_t)})


def _ex_gather_small(parts):
    def copies(ins, outs, send, recv):
        x, y, c = _place()
        me = 4 * x + 2 * y + c
        return [_remote(ins[0], outs[0].at[me], send, recv, mask - 1,
                        (x ^ (mask >> 2), y ^ ((mask >> 1) & 1), c ^ (mask & 1))) for mask in range(1, 8)]

    def start(*refs):
        for cp in copies(*refs):
            cp.start()

    def finish(ins, outs, send, recv):
        x, y, c = _place()
        me = 4 * x + 2 * y + c
        for mask in range(1, 8):
            _remote(ins[0], outs[0].at[me ^ mask], send, recv, mask - 1, (x, y, c)).wait_recv()
        for cp in copies(ins, outs, send, recv):
            cp.wait_send()

    return _Exchange([parts], [jax.ShapeDtypeStruct((8,) + parts.shape, F32)], 7, start, finish)


def _call(body, *, name, args=(), in_specs=(), out_specs=(), out_shape=(), grid=(), scratch_shapes=(),
          exchanges=()):
    n_in, n_out, n_scr = len(args), len(out_shape), len(scratch_shapes)
    ex_in, ex_out, ex_scr, spans, alias = [], [], [], [], {}
    for ex in exchanges:
        spans.append((len(ex_in), len(ex.inputs), len(ex_out), len(ex.out_shapes)))
        for i, o in ex.aliases.items():
            alias[n_in + len(ex_in) + i] = n_out + len(ex_out) + o
        ex_in += ex.inputs
        ex_out += ex.out_shapes
        ex_scr += [pltpu.SemaphoreType.DMA((ex.n_sems,)), pltpu.SemaphoreType.DMA((ex.n_sems,))]

    def full(*refs):
        ins, x_in = refs[:n_in], refs[n_in:n_in + len(ex_in)]
        outs = refs[n_in + len(ex_in):n_in + len(ex_in) + n_out]
        x_out = refs[n_in + len(ex_in) + n_out:n_in + len(ex_in) + n_out + len(ex_out)]
        scr = refs[len(refs) - n_scr - len(ex_scr):len(refs) - len(ex_scr)]
        sems = refs[len(refs) - len(ex_scr):]

        def run(which):
            for e, (ex, (i0, ni, o0, no)) in enumerate(zip(exchanges, spans)):
                getattr(ex, which)(x_in[i0:i0 + ni], x_out[o0:o0 + no], sems[2 * e], sems[2 * e + 1])

        if grid:
            ids = [pl.program_id(a) for a in range(len(grid))]
            is_first = functools.reduce(jnp.logical_and, [i == 0 for i in ids])
            is_last = functools.reduce(jnp.logical_and, [i == g - 1 for i, g in zip(ids, grid)])
            pl.when(is_first)(lambda: run("start"))
            body(*ins, *outs, *scr)
            pl.when(is_last)(lambda: run("finish"))
        else:
            run("start")
            if body is not None:
                body(*ins, *outs, *scr)
            run("finish")

    kw = dict(grid=grid) if grid else {}
    if grid:
        kw["compiler_params"] = _params(dimension_semantics=("arbitrary",) * len(grid))
    else:
        kw["compiler_params"] = _params()
    res = pl.pallas_call(
        full, name=name,
        in_specs=list(in_specs) + [HBM] * len(ex_in),
        out_specs=list(out_specs) + [HBM] * len(ex_out),
        out_shape=list(out_shape) + ex_out,
        scratch_shapes=list(scratch_shapes) + ex_scr,
        input_output_aliases=alias, **kw,
    )(*args, *ex_in)
    own = list(res[:n_out])
    per_ex = [list(res[n_out + o0:n_out + o0 + no]) for (_, _, o0, no) in spans]
    return own, per_ex


def _cast_own(w, place, name):
    rows, cols = w.shape
    tile = min(rows, 256)

    def body(place_ref, w_ref, o_ref):
        o_ref[...] = w_ref[...].astype(BF16)

    return pl.pallas_call(
        body, name=name,
        grid_spec=pltpu.PrefetchScalarGridSpec(
            num_scalar_prefetch=1, grid=(rows // tile,),
            in_specs=[pl.BlockSpec((tile, cols), lambda i, p: (i, 0))],
            out_specs=pl.BlockSpec((None, tile, cols), lambda i, p: (p[1], i, 0))),
        out_shape=jax.ShapeDtypeStruct((N_SHARDS, rows, cols), BF16),
        compiler_params=_params(dimension_semantics=("arbitrary",)),
    )(place, w)


def _cast_w_in(w, place):
    rows, cols = w.shape
    tile = 256

    def body(place_ref, w_ref, o_ref):
        o_ref[...] = w_ref[...].astype(BF16)

    return pl.pallas_call(
        body, name="cast_w_in",
        grid_spec=pltpu.PrefetchScalarGridSpec(
            num_scalar_prefetch=1, grid=(cols // COL_BLK, rows // tile),
            in_specs=[pl.BlockSpec((tile, COL_BLK), lambda b, i, p: (i, b))],
            out_specs=pl.BlockSpec((None, None, tile, COL_BLK), lambda b, i, p: (p[1], b, i, 0))),
        out_shape=jax.ShapeDtypeStruct((N_SHARDS, cols // COL_BLK, rows, COL_BLK), BF16),
        compiler_params=_params(dimension_semantics=("arbitrary", "arbitrary")),
    )(place, w)


def _rec_fwd(proj, lb_logits, rec_g, consts, exchanges):
    tri, low, sign = consts["tri"], consts["low"], consts["sign"]

    def body(q_ref, f_ref, i_ref, rg_ref, lb_ref, g_ref, w_ref, low_ref, sign_ref, y_ref, o_ref, stp_ref, st_ref):
        @pl.when(pl.program_id(1) == 0)
        def _():
            st_ref[...] = jnp.zeros_like(st_ref)
        lb = _lower_bound(lb_ref)
        st = st_ref[...]
        rows = lambda c: pl.ds(c * CHUNK, CHUNK)
        for c0 in range(0, REC_CHUNKS, REC_GROUP):
            group = range(c0, c0 + REC_GROUP)
            gated = [_gates(q_ref[rows(c), :], f_ref[rows(c), :], lb) for c in group]
            g2s = [_dot3(w_ref[...], g) * LOG2E for (_, _, _, _, g) in gated]
            xs = [[xl for xl, _, _, _ in _level_factors(g2, qs, k, sign_ref)]
                  for g2, (qs, _, _, k, _) in zip(g2s, gated)]
            a_s = []
            for x in xs:
                a = jnp.zeros((CHUNK, CHUNK), F32)
                for l, xl in enumerate(x):
                    a = a + _dot_nt(xl, xl) * low_ref[l]
                a_s.append(a.astype(BF16))
            vbs = [i_ref[rows(c), :].astype(BF16) for c in group]
            intra = [_dot(a, vb) for a, vb in zip(a_s, vbs)]
            kvs = [_dot_tn(vb, (k * jnp.exp2(g2[CHUNK - 1:CHUNK, :] - g2)).astype(BF16))
                   for vb, g2, (_, _, _, k, _) in zip(vbs, g2s, gated)]
            for i, c in enumerate(group):
                qs, _, _, k, _ = gated[i]
                g2 = g2s[i]
                stp_ref[c] = st
                v = i_ref[rows(c), :]
                rg = rg_ref[rows(c), :]
                o = (intra[i] + jnp.sum(qs * k, axis=-1, keepdims=True) * v
                     + _dot_nt((qs * jnp.exp2(g2)).astype(BF16), st.astype(BF16)))
                st = st * jnp.exp2(g2[CHUNK - 1:CHUNK, :]) + kvs[i]
                o_ref[rows(c), :] = o
                inv = lax.rsqrt(jnp.mean(o * o, axis=-1, keepdims=True) + EPS)
                y_ref[rows(c), :] = (o * inv * g_ref[...] * (rg * _sig(rg))).astype(BF16)
        st_ref[...] = st

    sec = lambda n: pl.BlockSpec((REC_ROWS, HEAD), lambda h, b: (b, n * SEC_BLK + h))
    vec = lambda rows: pl.BlockSpec((rows, HEAD), lambda h, b: (0, h))
    full = lambda a: pl.BlockSpec(a.shape, lambda h, b: (0,) * a.ndim)
    return _call(
        body, name="rec_fwd", grid=(N_HEADS, N_REC_BLK),
        args=(proj, proj, proj, proj, lb_logits, rec_g, tri, low, sign),
        in_specs=[sec(2), sec(3), sec(4), sec(5), vec(2), vec(1), full(tri), full(low), full(sign)],
        out_specs=[pl.BlockSpec((REC_ROWS, HEAD), lambda h, b: (b, h)),
                   pl.BlockSpec((REC_ROWS, HEAD), lambda h, b: (b, h)),
                   pl.BlockSpec((None, REC_CHUNKS, HEAD, HEAD), lambda h, b: (h, b, 0, 0))],
        out_shape=[jax.ShapeDtypeStruct((SEQ, D_MODEL), BF16),
                   jax.ShapeDtypeStruct((SEQ, D_MODEL), F32),
                   jax.ShapeDtypeStruct((N_HEADS, SEQ // CHUNK, HEAD, HEAD), F32)],
        scratch_shapes=[pltpu.VMEM((HEAD, HEAD), F32)],
        exchanges=exchanges)


OUT_ROWS = 512


def _out_proj_loss(y_pool, y_rec, w_out_g, x, target, gf):
    def body(yp_ref, yr_ref, w_ref, x_ref, t_ref, gf_ref, dout_ref, doutb_ref, part_ref):
        @pl.when(pl.program_id(0) == 0)
        def _():
            part_ref[...] = jnp.zeros_like(part_ref)
        halves = [pl.ds(a * (OUT_ROWS // 2), OUT_ROWS // 2) for a in range(2)]
        outs = [x_ref[r, :] + _dot(yp_ref[r, :], w_ref[0:D_MODEL, :])
                + _dot(yr_ref[r, :], w_ref[D_MODEL:2 * D_MODEL, :]) for r in halves]
        gf_v = gf_ref[...]
        for r, out in zip(halves, outs):
            inv = lax.rsqrt(jnp.mean(out * out, axis=-1, keepdims=True) + EPS)
            diff = out * inv * gf_v - t_ref[r, :]
            dyf = diff * (1.0 / D_MODEL)
            a = dyf * gf_v
            dout = inv * a - out * (inv * inv * inv) * jnp.mean(a * out, axis=-1, keepdims=True)
            dout_ref[r, :] = dout
            doutb_ref[r, :] = dout.astype(BF16)
            part_ref[0:1, :] += jnp.sum(dyf * out * inv, axis=0, keepdims=True)
            part_ref[1:2, :] += jnp.sum(diff * diff, axis=0, keepdims=True)

    row = lambda n: pl.BlockSpec((OUT_ROWS, n), lambda i: (i, 0))
    return pl.pallas_call(
        body, name="out_proj_loss", grid=(SEQ // OUT_ROWS,),
        in_specs=[row(D_MODEL), row(D_MODEL), pl.BlockSpec((2 * D_MODEL, D_MODEL), lambda i: (0, 0)),
                  row(D_MODEL), row(D_MODEL), pl.BlockSpec((1, D_MODEL), lambda i: (0, 0))],
        out_specs=[row(D_MODEL), row(D_MODEL), pl.BlockSpec((8, D_MODEL), lambda i: (0, 0))],
        out_shape=[jax.ShapeDtypeStruct((SEQ, D_MODEL), F32),
                   jax.ShapeDtypeStruct((SEQ, D_MODEL), BF16),
                   jax.ShapeDtypeStruct((8, D_MODEL), F32)],
        compiler_params=_params(dimension_semantics=("arbitrary",)),
    )(y_pool, y_rec, w_out_g, x, target, gf)


def _grad_w_out(y_pool, y_rec, dout_b):
    blk = W_OUT_SHARD // 2
    per = D_MODEL // blk
    n = 2 * per

    def body(yp_ref, yr_ref, d_ref, p32_ref, p16_ref, send_ref, recv_ref, send_sems, recv_sems):
        j = pl.program_id(0)
        x, y, c = _place()

        def copy(u):
            return _remote(send_ref.at[u], recv_ref.at[u], send_sems, recv_sems, u, (x, y, 1 - c))

        for i in range(n):
            @pl.when(j == i)
            def _(i=i):
                res = _dot_tn((yp_ref if i < per else yr_ref)[...], d_ref[...])

                @pl.when(i % 2 == c)
                def _():
                    p32_ref[i // 2] = res

                @pl.when(i % 2 != c)
                def _():
                    send_ref[i // 2] = res
                    copy(i // 2).start()

        @pl.when(j == n - 1)
        def _():
            for u in range(N_SHARDS):
                copy(u).wait_recv()
                tot = p32_ref[u] + recv_ref[u]
                p32_ref[u] = tot
                p16_ref[u] = tot.astype(BF16)
            for u in range(N_SHARDS):
                copy(u).wait_send()

    whole = pl.BlockSpec((N_SHARDS, blk, D_MODEL), lambda j: (0, 0, 0))
    return pl.pallas_call(
        body, name="grad_w_out", grid=(n,),
        in_specs=[pl.BlockSpec((SEQ, blk), lambda j: (0, jnp.minimum(j, per - 1))),
                  pl.BlockSpec((SEQ, blk), lambda j: (0, jnp.maximum(j - per, 0))),
                  pl.BlockSpec((SEQ, D_MODEL), lambda j: (0, 0))],
        out_specs=[whole, whole],
        out_shape=[jax.ShapeDtypeStruct((N_SHARDS, blk, D_MODEL), F32),
                   jax.ShapeDtypeStruct((N_SHARDS, blk, D_MODEL), BF16)],
        scratch_shapes=[pltpu.VMEM((N_SHARDS, blk, D_MODEL), F32), pltpu.VMEM((N_SHARDS, blk, D_MODEL), F32),
                        pltpu.SemaphoreType.DMA((N_SHARDS,)), pltpu.SemaphoreType.DMA((N_SHARDS,))],
        compiler_params=_params(dimension_semantics=("arbitrary",)),
    )(y_pool, y_rec, dout_b)


def _pool_bwd(proj, dout_b, w_out_g, pw_g, pool_scale, exchanges):
    n = POOL_ROWS + POOL_HALO

    def body(u_ref, gate_ref, d_ref, wo_ref, pw_ref, sc_ref,
             dp_ref, dpw_ref, dsc_ref, dd_ref, ddw_ref):
        g = pl.program_id(0)
        dpw_ref[...] = jnp.zeros_like(dpw_ref)
        dsc_ref[...] = jnp.zeros_like(dsc_ref)

        def first(ii, _):
            chunks = [POOL_PAIR * ii + a for a in range(POOL_PAIR)]
            rs = [pl.ds(pl.multiple_of(i * POOL_ROWS, POOL_ROWS), POOL_ROWS) for i in chunks]
            diffs = [_pool_diff(u_ref, i, g) for i in chunks]
            dbs = [d.astype(BF16) for d, _ in diffs]
            mixed = [_dot(db, pw_ref[...]) for db in dbs]
            dys = [_dot_nt(d_ref[r, :], wo_ref[...]) for r in rs]
            sc = sc_ref[...]
            dmbs = []
            for r, m, dy in zip(rs, mixed, dys):
                gate = gate_ref[r, :]
                sg = _sig(gate)
                silu = gate * sg
                dp_ref[1, r, :] = (dy * m * sc * (sg * (1.0 + gate * (1.0 - sg)))).astype(BF16)
                dsc_ref[...] += jnp.sum(dy * silu * m, axis=0, keepdims=True)
                dmbs.append((dy * silu * sc).astype(BF16))
            for db, dmb in zip(dbs, dmbs):
                dpw_ref[...] += _dot_tn(db, dmb)
            dds = [_dot_nt(dmb, pw_ref[...]) for dmb in dmbs]
            for r, dd, (_, inv_count) in zip(rs, dds, diffs):
                dd_ref[r, :] = dd
                ddw_ref[r, :] = dd * inv_count
            return 0
        lax.fori_loop(0, SEQ // POOL_ROWS // POOL_PAIR, first, 0)

        def second(i, _):
            r0 = i * POOL_ROWS
            r = pl.ds(pl.multiple_of(r0, POOL_ROWS), POOL_ROWS)
            last = i == SEQ // POOL_ROWS - 1
            after = ddw_ref[pl.ds(pl.multiple_of(jnp.minimum(r0 + POOL_ROWS, SEQ - POOL_HALO), 8), POOL_HALO), :]
            after = jnp.where(last, 0.0, after)
            ext = jnp.concatenate([ddw_ref[r, :], after], axis=0)
            s = _window_sums(ext, g, lambda k: n - (1 << k))[:POOL_ROWS, :]
            dp_ref[0, r, :] = (s - dd_ref[r, :]).astype(BF16)
            return 0
        lax.fori_loop(0, SEQ // POOL_ROWS, second, 0)

    return _call(
        body, name="pool_bwd", grid=(N_GROUPS,),
        args=(proj, proj, dout_b, w_out_g, pw_g, pool_scale),
        in_specs=[pl.BlockSpec((SEQ, PG), lambda g: (0, g)),
                  pl.BlockSpec((SEQ, PG), lambda g: (0, N_GROUPS + g)),
                  pl.BlockSpec((SEQ, D_MODEL), lambda g: (0, 0)),
                  pl.BlockSpec((PG, D_MODEL), lambda g: (g, 0)),
                  pl.BlockSpec((None, PG, PG), lambda g: (g, 0, 0)),
                  pl.BlockSpec((1, PG), lambda g: (0, g))],
        out_specs=[pl.BlockSpec((2, SEQ, PG), lambda g: (0, 0, g)),
                   pl.BlockSpec((None, PG, PG), lambda g: (g, 0, 0)),
                   pl.BlockSpec((1, PG), lambda g: (0, g))],
        out_shape=[jax.ShapeDtypeStruct((2, SEQ, D_MODEL), BF16),
                   jax.ShapeDtypeStruct((N_GROUPS, PG, PG), F32),
                   jax.ShapeDtypeStruct((1, D_MODEL), F32)],
        scratch_shapes=[pltpu.VMEM((SEQ, PG), F32), pltpu.VMEM((SEQ, PG), F32)],
        exchanges=exchanges)


HALF_HEADS = N_HEADS // 2
HALF_COLS = HALF_HEADS * HEAD


def _rec_bwd(proj, o_raw, st_prev, dout_b, w_out_g, lb_logits, rec_g, consts, h0, name, exchanges):
    def body(q_ref, f_ref, i_ref, rg_ref, o_ref, stp_ref, d_ref, wo_ref, lb_ref, g_ref,
             w_ref, lowt_ref, sym_ref, sign_ref, tri_ref,
             dr_ref, part_ref, dst_ref):
        @pl.when(pl.program_id(1) == 0)
        def _():
            dst_ref[...] = jnp.zeros_like(dst_ref)
            part_ref[...] = jnp.zeros_like(part_ref)
        tril = (lax.broadcasted_iota(jnp.int32, (CHUNK, CHUNK), 0)
                > lax.broadcasted_iota(jnp.int32, (CHUNK, CHUNK), 1))
        lb = _lower_bound(lb_ref)
        grec = g_ref[...]
        dst = dst_ref[...]
        acc_grec = jnp.zeros((1, HEAD), F32)
        acc_lb = jnp.zeros((1, HEAD), F32)
        rows = lambda c: pl.ds(c * CHUNK, CHUNK)
        for c0 in reversed(range(0, REC_CHUNKS, REC_GROUP_BWD)):
            group = list(reversed(range(c0, c0 + REC_GROUP_BWD)))
            dys = [_dot_nt(d_ref[rows(c), :], wo_ref[...]) for c in group]
            dos = []
            for c, dy in zip(group, dys):
                rg = rg_ref[rows(c), :]
                o = o_ref[rows(c), :]
                sg = _sig(rg)
                silu = rg * sg
                inv = lax.rsqrt(jnp.mean(o * o, axis=-1, keepdims=True) + EPS)
                recn = o * inv
                dr_ref[3, rows(c), :] = (dy * recn * grec * (sg * (1.0 + rg * (1.0 - sg)))).astype(BF16)
                acc_grec = acc_grec + jnp.sum(dy * silu * recn, axis=0, keepdims=True)
                drecn = dy * silu * grec
                dos.append(inv * drecn - o * (inv * inv * inv) * jnp.mean(drecn * o, axis=-1, keepdims=True))
            gated = [_gates(q_ref[rows(c), :], f_ref[rows(c), :], lb) for c in group]
            g2s = [_dot3(w_ref[...], g) * LOG2E for (_, _, _, _, g) in gated]
            levels = [_level_factors(g2, qs, k, sign_ref) for g2, (qs, _, _, k, _) in zip(g2s, gated)]
            a_ts = []
            for lev in levels:
                a_t = jnp.zeros((CHUNK, CHUNK), F32)
                for l, (xl, _, _, _) in enumerate(lev):
                    a_t = a_t + _dot_nt(xl, xl) * lowt_ref[l]
                a_ts.append(a_t.astype(BF16))
            dobs = [do.astype(BF16) for do in dos]
            vbs = [i_ref[rows(c), :].astype(BF16) for c in group]
            d_syms = [jnp.where(tril, _dot_nt(dob, vb), _dot_nt(vb, dob)) for dob, vb in zip(dobs, vbs)]
            dqs_is, dk_is = [], []
            for lev, d_sym in zip(levels, d_syms):
                dqs_i = jnp.zeros((CHUNK, HEAD), F32)
                both_i = jnp.zeros((CHUNK, HEAD), F32)
                for l, (xl, xlo, e, up) in enumerate(lev):
                    z = d_sym * sym_ref[l]
                    tmp = _dot(z.astype(BF16), jnp.concatenate([xl, xlo], axis=-1))
                    tmp = (tmp[:, :HEAD] + tmp[:, HEAD:]) * e
                    dqs_i = dqs_i + jnp.where(up, tmp, 0.0)
                    both_i = both_i + tmp
                dqs_is.append(dqs_i)
                dk_is.append(both_i - dqs_i)
            e_gs = [jnp.exp2(g2) for g2 in g2s]
            e_revs = [jnp.exp2(g2[CHUNK - 1:CHUNK, :] - g2) for g2 in g2s]
            e_lasts = [jnp.exp2(g2[CHUNK - 1:CHUNK, :]) for g2 in g2s]
            q_gs = [qs * e_g for (qs, _, _, _, _), e_g in zip(gated, e_gs)]
            kdecs = [k * e_rev for (_, _, _, k, _), e_rev in zip(gated, e_revs)]
            dv12 = [_dot(a_t, dob) + jnp.sum(qs * k, axis=-1, keepdims=True) * do
                    for a_t, dob, do, (qs, _, _, k, _) in zip(a_ts, dobs, dos, gated)]
            dq_gs = [_dot(dob, stp_ref[c].astype(BF16)) for c, dob in zip(group, dobs)]
            steps = [_dot_tn(dob, q_g.astype(BF16)) for dob, q_g in zip(dobs, q_gs)]
            dsts = []
            for e_last, step in zip(e_lasts, steps):
                dsts.append(dst)
                dst = dst * e_last + step
            dstbs = [d.astype(BF16) for d in dsts]
            dv3 = [_dot_nt(kdec.astype(BF16), dstb) for kdec, dstb in zip(kdecs, dstbs)]
            dkdecs = [_dot(vb, dstb) for vb, dstb in zip(vbs, dstbs)]
            dbig_gs, dg_lasts, dqss, dks = [], [], [], []
            for i, c in enumerate(group):
                qs, _, _, k, _ = gated[i]
                de_last = jnp.sum(stp_ref[c] * dsts[i], axis=0, keepdims=True)
                ddiag = jnp.sum(dos[i] * i_ref[rows(c), :], axis=-1, keepdims=True)
                dqss.append(dqs_is[i] + ddiag * k + dq_gs[i] * e_gs[i])
                dks.append(dk_is[i] + ddiag * qs + dkdecs[i] * e_revs[i])
                dg_rev = dkdecs[i] * kdecs[i]
                dg_lasts.append(jnp.sum(dg_rev, axis=0, keepdims=True) + de_last * e_lasts[i])
                dbig_gs.append(qs * dqs_is[i] - k * dk_is[i] + dq_gs[i] * q_gs[i] - dg_rev)
            dgs = [_dot3(tri_ref[...], dbig_g) + dg_last for dbig_g, dg_last in zip(dbig_gs, dg_lasts)]
            for i, c in enumerate(group):
                _, sf, f, _, _ = gated[i]
                q = q_ref[rows(c), :]
                df = dgs[i] / f - dks[i]
                dr_ref[1, rows(c), :] = (df * (1.0 - lb) * sf * (1.0 - sf)).astype(BF16)
                acc_lb = acc_lb + jnp.sum(df * (1.0 - sf), axis=0, keepdims=True)
                sq = _sig(q)
                dr_ref[0, rows(c), :] = (dqss[i] * (sq * (1.0 + q * (1.0 - sq)))).astype(BF16)
                dr_ref[2, rows(c), :] = (dv12[i] + dv3[i]).astype(BF16)
        dst_ref[...] = dst
        part_ref[0:1, :] += acc_grec
        part_ref[1:2, :] += acc_lb

    rev = lambda b: N_REC_BLK - 1 - b
    sec = lambda n: pl.BlockSpec((REC_ROWS, HEAD), lambda h, b: (rev(b), n * SEC_BLK + h0 + h))
    col_in = pl.BlockSpec((REC_ROWS, HEAD), lambda h, b: (rev(b), h0 + h))
    vec_in = lambda rows: pl.BlockSpec((rows, HEAD), lambda h, b: (0, h0 + h))
    full = lambda a: pl.BlockSpec(a.shape, lambda h, b: (0,) * a.ndim)
    return _call(
        body, name=name, grid=(HALF_HEADS, N_REC_BLK),
        args=(proj, proj, proj, proj, o_raw, st_prev, dout_b, w_out_g, lb_logits, rec_g,
              consts["tri"], consts["low_t"], consts["sym"], consts["sign"], consts["tri_t"]),
        in_specs=[sec(2), sec(3), sec(4), sec(5), col_in,
                  pl.BlockSpec((None, REC_CHUNKS, HEAD, HEAD), lambda h, b: (h0 + h, rev(b), 0, 0)),
                  pl.BlockSpec((REC_ROWS, D_MODEL), lambda h, b: (rev(b), 0)),
                  pl.BlockSpec((HEAD, D_MODEL), lambda h, b: (SEC_BLK + h0 + h, 0)),
                  vec_in(2), vec_in(1)] + [full(consts[n]) for n in ("tri", "low_t", "sym", "sign", "tri_t")],
        out_specs=[pl.BlockSpec((4, REC_ROWS, HEAD), lambda h, b: (0, rev(b), h)),
                   pl.BlockSpec((8, HEAD), lambda h, b: (0, h))],
        out_shape=[jax.ShapeDtypeStruct((4, SEQ, HALF_COLS), BF16),
                   jax.ShapeDtypeStruct((8, HALF_COLS), F32)],
        scratch_shapes=[pltpu.VMEM((HEAD, HEAD), F32)],
        exchanges=exchanges)


def _w_in_block(w_ref, j):
    per_shard = W_IN_SHARD // COL_BLK
    return w_ref[j // per_shard, j % per_shard]


def _grad_x(dproj, w_in_g, x, g1, dout, exchanges):
    rows = 256
    n_blk = len(dproj)

    def body(*refs):
        dp_refs = refs[:n_blk]
        w_ref, x_ref, g_ref, dout_ref, dx_ref, part_ref = refs[n_blk:]

        @pl.when(pl.program_id(0) == 0)
        def _():
            part_ref[...] = jnp.zeros_like(part_ref)
        dh = jnp.zeros((rows, D_MODEL), F32)
        for j in range(n_blk):
            dh = dh + _dot_nt(dp_refs[j][...], _w_in_block(w_ref, j))
        xv = x_ref[...]
        inv = lax.rsqrt(jnp.mean(xv * xv, axis=-1, keepdims=True) + EPS)
        a = dh * g_ref[...]
        dx_ref[...] = (dout_ref[...] + inv * a
                       - xv * (inv * inv * inv) * jnp.mean(a * xv, axis=-1, keepdims=True))
        part_ref[0:1, :] += jnp.sum(dh * xv * inv, axis=0, keepdims=True)

    row = lambda: pl.BlockSpec((rows, D_MODEL), lambda i: (i, 0))
    dp_spec = lambda sec, cb: pl.BlockSpec((None, rows, COL_BLK), lambda i: (sec, i, cb))
    return _call(
        body, name="grad_x", grid=(SEQ // rows,),
        args=tuple(a for a, _, _ in dproj) + (w_in_g, x, g1, dout),
        in_specs=[dp_spec(sec, cb) for _, sec, cb in dproj]
                 + [pl.BlockSpec(w_in_g.shape, lambda i: (0, 0, 0, 0)),
                    row(), pl.BlockSpec((1, D_MODEL), lambda i: (0, 0)), row()],
        out_specs=[row(), pl.BlockSpec((8, D_MODEL), lambda i: (0, 0))],
        out_shape=[jax.ShapeDtypeStruct((SEQ, D_MODEL), F32),
                   jax.ShapeDtypeStruct((8, D_MODEL), F32)],
        exchanges=exchanges)


def _grad_w_in(h, dp, blocks, name):
    n_blk = len(blocks)
    half = D_MODEL // 2
    pick = lambda vals: (lambda j: functools.reduce(lambda acc, iv: jnp.where(j == iv[0], iv[1], acc),
                                                     list(enumerate(vals))[1:], vals[0]))
    sec_of = pick([sec for sec, _ in blocks])
    cb_of = pick([cb for _, cb in blocks])

    def body(h_ref, dp_ref, p32_ref, p16_ref, keep_ref, send_ref, recv_ref, send_sems, recv_sems):
        j = pl.program_id(0)
        x, y, c = _place()
        cols = lambda cc: pl.ds(pl.multiple_of(cc * half, half), half)

        def copy(i):
            return _remote(send_ref.at[i], recv_ref.at[i], send_sems, recv_sems, i, (x, y, 1 - c))

        for i in range(n_blk + 1):
            @pl.when(j == i)
            def _(i=i):
                if i < n_blk:
                    send_ref[i] = _dot_tn(h_ref[:, cols(1 - c)], dp_ref[...])
                    copy(i).start()
                    keep_ref[i] = _dot_tn(h_ref[:, cols(c)], dp_ref[...])
                if i > 0:
                    copy(i - 1).wait_recv()
                    tot = keep_ref[i - 1] + recv_ref[i - 1]
                    p32_ref[...] = tot
                    p16_ref[...] = tot.astype(BF16)

        @pl.when(j == n_blk)
        def _():
            for i in range(n_blk):
                copy(i).wait_send()

    lagged = pl.BlockSpec((None, half, COL_BLK), lambda j: (jnp.maximum(j - 1, 0), 0, 0))
    last = n_blk - 1
    return pl.pallas_call(
        body, name=name, grid=(n_blk + 1,),
        in_specs=[pl.BlockSpec((SEQ, D_MODEL), lambda j: (0, 0)),
                  pl.BlockSpec((None, SEQ, COL_BLK),
                               lambda j: (sec_of(jnp.minimum(j, last)), 0, cb_of(jnp.minimum(j, last))))],
        out_specs=[lagged, lagged],
        out_shape=[jax.ShapeDtypeStruct((n_blk, half, COL_BLK), F32),
                   jax.ShapeDtypeStruct((n_blk, half, COL_BLK), BF16)],
        scratch_shapes=[pltpu.VMEM((n_blk, half, COL_BLK), F32)] * 3
                       + [pltpu.SemaphoreType.DMA((n_blk,)), pltpu.SemaphoreType.DMA((n_blk,))],
        compiler_params=_params(dimension_semantics=("arbitrary",)),
    )(h, dp)


def _add_units(grad, recv, place, tile, name):
    n, rows, cols = grad.shape
    per_half = rows // 2 // tile

    def body(place_ref, g_ref, r_ref, o32_ref, o16_ref):
        v = g_ref[...] + r_ref[...]
        o32_ref[...] = v
        o16_ref[...] = v.astype(BF16)

    blk = lambda f: pl.BlockSpec((None, tile, cols), f)
    out = lambda s, i, p: (s, i, 0)
    return pl.pallas_call(
        body, name=name,
        grid_spec=pltpu.PrefetchScalarGridSpec(
            num_scalar_prefetch=1, grid=(n, per_half),
            in_specs=[blk(lambda s, i, p: (s, p[0] * per_half + i, 0)), blk(out)],
            out_specs=[blk(out), blk(out)]),
        out_shape=[jax.ShapeDtypeStruct(recv.shape, F32), jax.ShapeDtypeStruct(recv.shape, BF16)],
        compiler_params=_params(dimension_semantics=("arbitrary", "arbitrary")),
    )(place, grad, recv)


def _sum_units(part32, recv16, place, owners, tile, name):
    n, half, cols = part32.shape
    per_half = half // tile
    table = np.array([[sum(o == chip for o in owners)] + sorted(range(n), key=lambda j: (owners[j] != chip, j))
                      for chip in range(N_SHARDS)], np.int32)
    sched = jnp.concatenate([place[:1], jnp.asarray(table)[place[1]]])

    def block(k, i, p):
        live = k < p[1]
        unit = p[2 + jnp.minimum(k, jnp.maximum(p[1] - 1, 0))]
        return unit, jnp.where(live, i, per_half - 1)

    def body(sched_ref, p_ref, r_ref, o_ref):
        @pl.when(pl.program_id(0) < sched_ref[1])
        def _():
            acc = p_ref[...]
            for j in range(len(CHIP_FLIPS)):
                acc = acc + r_ref[j].astype(F32)
            o_ref[...] = acc

    return pl.pallas_call(
        body, name=name,
        grid_spec=pltpu.PrefetchScalarGridSpec(
            num_scalar_prefetch=1, grid=(n, per_half),
            in_specs=[pl.BlockSpec((None, tile, cols), lambda k, i, p: (*block(k, i, p), 0)),
                      pl.BlockSpec((None, len(CHIP_FLIPS), tile, cols),
                                   lambda k, i, p: (block(k, i, p)[0], 0, block(k, i, p)[1], 0))],
            out_specs=pl.BlockSpec((None, tile, cols),
                                   lambda k, i, p: (block(k, i, p)[0], p[0] * per_half + block(k, i, p)[1], 0))),
        out_shape=jax.ShapeDtypeStruct((n, 2 * half, cols), F32),
        compiler_params=_params(dimension_semantics=("arbitrary", "arbitrary")),
    )(sched, part32, recv16)


def _adamw_math(w, g, m, v):
    m = ADAM_B1 * m + (1.0 - ADAM_B1) * g
    v = ADAM_B2 * v + (1.0 - ADAM_B2) * (g * g)
    m_hat = m / (1.0 - ADAM_B1 ** ADAM_STEP)
    v_hat = v / (1.0 - ADAM_B2 ** ADAM_STEP)
    delta = -ADAM_LR * (m_hat / (jnp.sqrt(v_hat) + ADAM_EPS) + ADAM_WD * w)
    return delta, m, v


def _adamw_units(w, m, v, grads, pick, name):
    rows, cols = w.shape
    bc = grads[0].shape[-1]
    tile = min(rows, 256)
    n_g = len(grads)

    def body(pick_ref, w_ref, m_ref, v_ref, *refs):
        g_refs, (g_out, d_ref, nm_ref, nv_ref) = refs[:n_g], refs[n_g:]
        p = pl.program_id(0)
        for a in range(n_g):
            @pl.when(pick_ref[0, p] == a)
            def _(a=a):
                g = g_refs[a][...]
                g_out[...] = g
                d_ref[...], nm_ref[...], nv_ref[...] = _adamw_math(w_ref[...], g, m_ref[...], v_ref[...])

    blk = pl.BlockSpec((tile, bc), lambda p, i, pick: (i, p))

    def g_spec(a):
        return pl.BlockSpec((None, tile, bc),
                            lambda p, i, pick: (jnp.where(pick[0, p] == a, pick[1, p], 0),
                                                jnp.where(pick[0, p] == a, i, 0), 0))

    return pl.pallas_call(
        body, name=name,
        grid_spec=pltpu.PrefetchScalarGridSpec(
            num_scalar_prefetch=1, grid=(cols // bc, rows // tile),
            in_specs=[blk] * 3 + [g_spec(a) for a in range(n_g)],
            out_specs=[blk] * 4),
        out_shape=[jax.ShapeDtypeStruct(w.shape, F32)] * 4,
        compiler_params=_params(dimension_semantics=("arbitrary", "arbitrary")),
    )(pick, w, m, v, *grads)


ROW_NORM1, ROW_SCALE, ROW_LB, ROW_REC, ROW_FINAL, ROW_LOSS = 0, 1, 2, 4, 5, 6


SMALL_ROWS = (ROW_NORM1, ROW_SCALE, ROW_LB, ROW_REC, ROW_FINAL)


def _small_update(parts, gathered, params):
    n_p = len(params)

    def body(own_ref, p_ref, *refs):
        ins, loss_ref, outs = refs[:3 * n_p], refs[3 * n_p], refs[3 * n_p + 1:]
        x, y, c = _place()
        me = 4 * x + 2 * y + c
        slot = lambda d: jnp.where(me == d, own_ref[...], p_ref[d])
        tot = slot(0)
        for d in range(1, 8):
            tot = tot + slot(d)
        for i, r in enumerate(SMALL_ROWS):
            w = ins[3 * i][...]
            g = tot[r:r + 1, :]
            if r == ROW_LB:
                mx = jnp.maximum(w[0:1, :], w[1:2, :])
                e0 = jnp.exp(w[0:1, :] - mx)
                e1 = jnp.exp(w[1:2, :] - mx)
                lb = e0 / (e0 + e1)
                g = g * lb * (1.0 - lb)
                g = jnp.concatenate([g, -g], axis=0)
            outs[4 * i][...] = g
            outs[4 * i + 1][...], outs[4 * i + 2][...], outs[4 * i + 3][...] = _adamw_math(
                w, g, ins[3 * i + 1][...], ins[3 * i + 2][...])
        loss_ref[...] = (0.5 / D_MODEL) * jnp.sum(tot[ROW_LOSS:ROW_LOSS + 1, :], axis=-1, keepdims=True)

    flat = [a for wmv in params for a in wmv]
    return pl.pallas_call(
        body, name="small_update",
        out_shape=[jax.ShapeDtypeStruct((1, 1), F32)]
                  + [jax.ShapeDtypeStruct(w.shape, F32) for w, _, _ in params for _ in range(4)],
        compiler_params=_params(),
    )(parts, gathered, *flat)


SHARD_OWNERS = tuple(range(N_SHARDS))
BLOCKS_POOL = (0, 1, 2, 3)
BLOCKS_A = (4, 6, 8, 10)
BLOCKS_B = (5, 7, 9, 11)
BLOCK_GROUPS = (BLOCKS_POOL, BLOCKS_A, BLOCKS_B)


def _block_owners(blocks):
    return tuple(j // (W_IN_SHARD // COL_BLK) for j in blocks)


def kernel(x, norm1_g, w_in, pool_w, pool_scale, lb_logits, rec_norm_g, w_out, final_norm_g, loss_target, m_norm1_g, m_w_in, m_pool_w, m_pool_scale, m_lb_logits, m_rec_norm_g, m_w_out, m_final_norm_g, v_norm1_g, v_w_in, v_pool_w, v_pool_scale, v_lb_logits, v_rec_norm_g, v_w_out, v_final_norm_g):
    xi, yi, ci = _place()
    chip = 2 * xi + yi
    place = jnp.stack([ci, chip]).astype(jnp.int32)
    pw_rows = N_GROUPS * PW_SHARD
    flat_pw = lambda a: a.reshape(pw_rows, PG)
    x2, target, gf = x[0], loss_target[0], final_norm_g.reshape(1, D_MODEL)
    consts = {n: jnp.asarray(a, BF16 if n.startswith("tri") else F32) for n, a in _chunk_constants().items()}

    proj, h, w_in_g = _in_proj(x2, norm1_g, _cast_w_in(w_in[0], place), place)
    (y_rec, o_raw, st_prev), ((w_out_g, pw_g),) = _rec_fwd(
        proj, lb_logits, rec_norm_g, consts,
        [_ex_gather([_cast_own(w_out[0], place, "cast_w_out"), _cast_own(flat_pw(pool_w), place, "cast_pool_w")])])
    w_out_g = w_out_g.reshape(2 * D_MODEL, D_MODEL)
    pw_full = pw_g.reshape(N_SHARDS, N_GROUPS, PW_SHARD, PG).transpose(1, 0, 2, 3).reshape(N_GROUPS, PG, PG)
    y_pool = _pool_fwd(proj, pw_full, pool_scale)
    dout, dout_b, part_out = _out_proj_loss(y_pool, y_rec, w_out_g, x2, target, gf)

    p_out32, p_out16 = _grad_w_out(y_pool, y_rec, dout_b)
    (dpool, gpw, dscale), ((rb_out,),) = _pool_bwd(proj, dout_b, w_out_g, pw_full, pool_scale,
                                                   [_ex_send([p_out16], [SHARD_OWNERS])])
    g_out = _sum_units(p_out32, rb_out, place, SHARD_OWNERS, 256, "sum_w_out")
    gpw = gpw.reshape(N_GROUPS, N_SHARDS, PW_SHARD, PG).transpose(1, 0, 2, 3).reshape(N_SHARDS, pw_rows, PG)
    p_inp32, p_inp16 = _grad_w_in(h, dpool, [(0, 0), (0, 1), (1, 0), (1, 1)], "grad_w_in_pool")

    pool_owners, a_owners, b_owners = (_block_owners(b) for b in BLOCK_GROUPS)
    rec_args = (proj, o_raw, st_prev, dout_b, w_out_g, lb_logits, rec_norm_g, consts)
    (drec_a, part_a), ((rb_inp,), (ra_pw,)) = _rec_bwd(
        *rec_args, 0, "rec_bwd_a", [_ex_send([p_inp16], [pool_owners], units=[(0, 1)]), _ex_swap([gpw])])
    p_pw32, p_pw16 = _add_units(gpw, ra_pw, place, 128, "add_pool_w")
    rec_blocks = [(n, 0) for n in range(4)]
    p_ina32, p_ina16 = _grad_w_in(h, drec_a, rec_blocks, "grad_w_in_a")

    (drec_b, part_b), ((rb_inp, rb_ina, rb_pw),) = _rec_bwd(
        *rec_args, HALF_HEADS, "rec_bwd_b",
        [_ex_send([p_inp16, p_ina16, p_pw16], [pool_owners, a_owners, SHARD_OWNERS],
                  units=[(2, 3), (0, 1, 2, 3), (0, 1, 2, 3)], landed=[rb_inp, None, None])])
    g_inp = _sum_units(p_inp32, rb_inp, place, pool_owners, 256, "sum_w_in_pool")
    g_ina = _sum_units(p_ina32, rb_ina, place, a_owners, 256, "sum_w_in_a")
    g_pw = _sum_units(p_pw32, rb_pw, place, SHARD_OWNERS, 128, "sum_pool_w")
    p_inb32, p_inb16 = _grad_w_in(h, drec_b, rec_blocks, "grad_w_in_b")

    dproj = ([(dpool, 0, 0), (dpool, 0, 1), (dpool, 1, 0), (dpool, 1, 1)]
             + [(d, n, 0) for n in range(4) for d in (drec_a, drec_b)])
    (dx, part_x), ((rb_inb,),) = _grad_x(dproj, w_in_g, x2, norm1_g, dout, [_ex_send([p_inb16], [b_owners])])
    g_inb = _sum_units(p_inb32, rb_inb, place, b_owners, 256, "sum_w_in_b")
    zero = jnp.zeros((1, D_MODEL), F32)
    part_rec = jnp.concatenate([part_a, part_b], axis=1)
    parts = jnp.concatenate([part_x[0:1], dscale, part_rec[1:2], zero, part_rec[0:1], part_out[0:1],
                             part_out[1:2], zero], axis=0)
    _, ((g_out, g_pw, g_inp, g_ina, g_inb), (gathered,)) = _call(
        None, name="join_halves",
        exchanges=[_ex_join([g_out, g_pw, g_inp, g_ina, g_inb],
                            [SHARD_OWNERS, SHARD_OWNERS, pool_owners, a_owners, b_owners]),
                   _ex_gather_small(parts)])

    group_of = np.zeros((D_PROJ // COL_BLK,), np.int32)
    index_of = np.zeros((D_PROJ // COL_BLK,), np.int32)
    for gi, blocks in enumerate(BLOCK_GROUPS):
        for i, j in enumerate(blocks):
            group_of[j], index_of[j] = gi, i
    per_shard = W_IN_SHARD // COL_BLK
    pick_in = jnp.stack([lax.dynamic_slice(jnp.asarray(group_of), (per_shard * chip,), (per_shard,)),
                         lax.dynamic_slice(jnp.asarray(index_of), (per_shard * chip,), (per_shard,))])
    pick_own = jnp.stack([jnp.zeros((1,), jnp.int32), chip.reshape(1).astype(jnp.int32)])
    big = [_adamw_units(w_in[0], m_w_in[0], v_w_in[0], [g_inp, g_ina, g_inb], pick_in, "adamw_w_in"),
           _adamw_units(w_out[0], m_w_out[0], v_w_out[0], [g_out], pick_own, "adamw_w_out"),
           _adamw_units(flat_pw(pool_w), flat_pw(m_pool_w), flat_pw(v_pool_w), [g_pw], pick_own, "adamw_pool_w")]

    row = lambda a: a.reshape(1, D_MODEL)
    loss, *small = _small_update(parts, gathered, [
        (norm1_g, m_norm1_g, v_norm1_g), (pool_scale, m_pool_scale, v_pool_scale),
        (lb_logits, m_lb_logits, v_lb_logits), (rec_norm_g, m_rec_norm_g, v_rec_norm_g),
        (row(final_norm_g), row(m_final_norm_g), row(v_final_norm_g))])

    def leaves(k):
        norm1, scale, lb, rec, final = (small[4 * i + k] for i in range(len(SMALL_ROWS)))
        return (norm1, big[0][k][None], big[2][k].reshape(pool_w.shape), scale, lb, rec,
                big[1][k][None], final.reshape(D_MODEL))

    return (loss.reshape(()), dx[None], *leaves(0), *leaves(1), *leaves(2), *leaves(3))
```

```python
import functools

import numpy as np
import jax
import jax.numpy as jnp
from jax import lax
from jax.experimental import pallas as pl
from jax.experimental.pallas import tpu as pltpu

F32 = jnp.float32
BF16 = jnp.bfloat16

SEQ = 2048
D_MODEL = 1024
D_PROJ = 6144
N_SEC = 6
N_GROUPS = 4
PG = 256
N_HEADS = 8
HEAD = 128
CHUNK = 64
N_LEVELS = 6
N_SHARDS = 4
W_IN_SHARD = D_PROJ // N_SHARDS
W_OUT_SHARD = 2048 // N_SHARDS
PW_SHARD = PG // N_SHARDS
COL_BLK = 512
EPS = 1e-6

ADAM_LR = 0.001
ADAM_B1 = 0.9
ADAM_B2 = 0.999
ADAM_EPS = 1e-08
ADAM_WD = 0.01
ADAM_STEP = 10

V7X_VMEM_LIMIT = 56 * 1024 * 1024
MESH = pl.DeviceIdType.MESH


def _params(**kw):
    return pltpu.CompilerParams(vmem_limit_bytes=V7X_VMEM_LIMIT, **kw)


def _sig(x):
    return 1.0 / (1.0 + jnp.exp(-x))


def _dot(a, b):
    return jnp.dot(a, b, preferred_element_type=F32)


def _dot_nt(a, b):
    return lax.dot_general(a, b, (((1,), (1,)), ((), ())), preferred_element_type=F32)


def _dot_tn(a, b):
    return lax.dot_general(a, b, (((0,), (0,)), ((), ())), preferred_element_type=F32)


def _split3(a):
    p1 = a.astype(BF16)
    r1 = a - p1.astype(F32)
    p2 = r1.astype(BF16)
    p3 = (r1 - p2.astype(F32)).astype(BF16)
    return jnp.concatenate([p1, p2, p3], axis=-1)


def _dot3(w01, a):
    n = a.shape[-1]
    r = _dot(w01, _split3(a))
    return r[:, :n] + r[:, n:2 * n] + r[:, 2 * n:]


def _chunk_constants():
    j = np.arange(CHUNK)
    tt, ss = np.meshgrid(j, j, indexing="ij")
    x = tt ^ ss
    hb = np.full((CHUNK, CHUNK), -1, np.int32)
    for l in range(N_LEVELS):
        hb[x >= (1 << l)] = l
    sym = np.stack([(hb == l) for l in range(N_LEVELS)]).astype(np.float32)
    low = sym * (tt > ss)
    sign = np.stack([np.where((j >> l) & 1, 1.0, -1.0) for l in range(N_LEVELS)]).astype(np.float32)
    sign = np.ascontiguousarray(np.broadcast_to(sign[:, :, None], (N_LEVELS, CHUNK, HEAD)))
    tri = (ss <= tt).astype(np.float32)
    return dict(tri=tri, tri_t=np.ascontiguousarray(tri.T), low=low,
                low_t=np.ascontiguousarray(low.transpose(0, 2, 1)), sym=sym, sign=sign)


def _in_proj(x, g1, w_slots, place):
    n_col = D_PROJ // COL_BLK
    per_shard = W_IN_SHARD // COL_BLK
    rows = 1024
    half_rows = D_MODEL // 2
    quarter_rows = D_MODEL // 4
    FLIPS = (0, 2, 1, 3)
    ORDER = ([(0, p) for p in range(per_shard)] + [(m, p) for p in range(per_shard) for m in (1, 2)]
             + [(3, p) for p in range(per_shard)])

    def shard_at(m, chip):
        return chip ^ FLIPS[m]

    def pick(vals, t):
        return functools.reduce(lambda acc, iv: jnp.where(t == iv[0], iv[1], acc), list(enumerate(vals))[1:], vals[0])

    def body(place_ref, x_ref, g_ref, w_in_ref, proj_ref, h_ref, w_ref, wbuf, load_sems, send_sems, recv_sems):
        t = pl.program_id(0)
        x_, y_, c = _place()
        chip = 2 * x_ + y_
        me, other_core = (x_, y_, c), (x_, y_, 1 - c)
        x_nbr, y_nbr = (1 - x_, y_, c), (x_, 1 - y_, c)

        def rows_of(half, q=None):
            if q is None:
                return pl.ds(pl.multiple_of(half * half_rows, half_rows), half_rows)
            return pl.ds(pl.multiple_of(half * half_rows + q * quarter_rows, quarter_rows), quarter_rows)

        def block(m, p, r):
            return w_ref.at[shard_at(m, chip), p, r, :]

        def copy(k, ref, to):
            return _remote(ref, ref, send_sems, recv_sems, k, to)

        direct = lambda n, p, to: copy(3 * n + p, block(0, p, rows_of(c)), to)
        relay = lambda n, p, to: copy(6 + 3 * n + p, block(1 + n, p, rows_of(c, n)), to)
        arrived = lambda m, p: ([copy(3 * (m - 1) + p, block(m, p, rows_of(c)), me)] if m < 3 else
                                [copy(6 + 3 * n + p, block(3, p, rows_of(c, n)), me) for n in (0, 1)])
        passed_on = lambda m, p, half, to: copy(9 + 3 * m + p, block(m, p, rows_of(half)), to)

        def load(m, p, slot):
            return pltpu.make_async_copy(w_ref.at[shard_at(m, chip), p], wbuf.at[slot], load_sems.at[slot])

        def prepare(m, p):
            for cp in arrived(m, p):
                cp.wait_recv()
            passed_on(m, p, c, other_core).start()
            if m < 3:
                relay(m - 1, p, y_nbr if m == 1 else x_nbr).start()

        @pl.when(t == 0)
        def _():
            for p in range(per_shard):
                direct(0, p, x_nbr).start()
                direct(1, p, y_nbr).start()
            for p in range(per_shard):
                load(0, p, p).start()

            def norm(i, _):
                r = pl.ds(pl.multiple_of(i * rows, rows), rows)
                xv = x_ref[r, :]
                inv = lax.rsqrt(jnp.mean(xv * xv, axis=-1, keepdims=True) + EPS)
                h_ref[r, :] = (xv * inv * g_ref[...]).astype(BF16)
                return 0
            lax.fori_loop(0, SEQ // rows, norm, 0)

        for step, (m, p) in enumerate(ORDER):
            @pl.when(t == step)
            def _(step=step, m=m, p=p):
                slot = step % per_shard
                if m > 0:
                    passed_on(m, p, 1 - c, me).wait_recv()
                    load(m, p, slot).start()
                if step + 1 < n_col and ORDER[step + 1][0] > 0:
                    prepare(*ORDER[step + 1])
                load(m, p, slot).wait()

                def mm(i, _):
                    r = pl.ds(pl.multiple_of(i * rows, rows), rows)
                    proj_ref[r, :] = _dot(h_ref[r, :], wbuf[slot])
                    return 0
                lax.fori_loop(0, SEQ // rows, mm, 0)

        @pl.when(t == n_col - 1)
        def _():
            for p in range(per_shard):
                sent = [direct(0, p, x_nbr), direct(1, p, y_nbr), relay(0, p, y_nbr), relay(1, p, x_nbr)]
                for cp in sent + [passed_on(m, p, c, other_core) for m in (1, 2, 3)]:
                    cp.wait_send()

    return pl.pallas_call(
        body, name="in_proj",
        grid_spec=pltpu.PrefetchScalarGridSpec(
            num_scalar_prefetch=1, grid=(n_col,),
            in_specs=[pl.BlockSpec((SEQ, D_MODEL), lambda t, p: (0, 0)),
                      pl.BlockSpec((1, D_MODEL), lambda t, p: (0, 0)),
                      pl.BlockSpec(memory_space=pl.ANY)],
            out_specs=[pl.BlockSpec((SEQ, COL_BLK),
                                    lambda t, p: (0, per_shard * (p[1] ^ pick([FLIPS[m] for m, _ in ORDER], t))
                                                  + pick([b for _, b in ORDER], t))),
                       pl.BlockSpec((SEQ, D_MODEL), lambda t, p: (0, 0)),
                       pl.BlockSpec(memory_space=pl.ANY)],
            scratch_shapes=[pltpu.VMEM((per_shard, D_MODEL, COL_BLK), BF16),
                            pltpu.SemaphoreType.DMA((per_shard,)),
                            pltpu.SemaphoreType.DMA((21,)), pltpu.SemaphoreType.DMA((21,))]),
        out_shape=[jax.ShapeDtypeStruct((SEQ, D_PROJ), F32),
                   jax.ShapeDtypeStruct((SEQ, D_MODEL), BF16),
                   jax.ShapeDtypeStruct(w_slots.shape, BF16)],
        input_output_aliases={3: 2},
        compiler_params=_params(dimension_semantics=("arbitrary",)),
    )(place, x, g1, w_slots)


POOL_ROWS = 256
POOL_HALO = 16
POOL_PAIR = 2


def _window_sums(ext, g, shift_of):
    s = ext
    for k in range(N_GROUPS):
        s = jnp.where(k <= g, s + pltpu.roll(s, shift_of(k), 0), s)
    return s


def _pool_diff(u_ref, i, g):
    n = POOL_ROWS + POOL_HALO
    r0 = i * POOL_ROWS
    cur = u_ref[pl.ds(pl.multiple_of(r0, POOL_ROWS), POOL_ROWS), :]
    before = u_ref[pl.ds(pl.multiple_of(jnp.maximum(r0 - POOL_HALO, 0), 8), POOL_HALO), :]
    before = jnp.where(i > 0, before, 0.0)
    ext = jnp.concatenate([before, cur], axis=0)
    s = _window_sums(ext, g, lambda k: 1 << k)[POOL_HALO:, :]
    t = r0 + lax.broadcasted_iota(jnp.int32, (POOL_ROWS, 1), 0)
    width = (2 << g).astype(F32)
    inv_count = 1.0 / jnp.minimum((t + 1).astype(F32), width)
    return s * inv_count - cur, inv_count


def _pool_fwd(proj, pw_g, pool_scale):
    def body(u_ref, gate_ref, pw_ref, sc_ref, y_ref):
        g = pl.program_id(0)

        def step(ii, _):
            chunks = [POOL_PAIR * ii + a for a in range(POOL_PAIR)]
            ds = [_pool_diff(u_ref, i, g)[0].astype(BF16) for i in chunks]
            mixed = [_dot(d, pw_ref[...]) for d in ds]
            for i, m in zip(chunks, mixed):
                r = pl.ds(pl.multiple_of(i * POOL_ROWS, POOL_ROWS), POOL_ROWS)
                gate = gate_ref[r, :]
                y_ref[r, :] = (m * sc_ref[...] * (gate * _sig(gate))).astype(BF16)
            return 0
        lax.fori_loop(0, SEQ // POOL_ROWS // POOL_PAIR, step, 0)

    return pl.pallas_call(
        body, name="pool_fwd", grid=(N_GROUPS,),
        in_specs=[pl.BlockSpec((SEQ, PG), lambda g: (0, g)),
                  pl.BlockSpec((SEQ, PG), lambda g: (0, N_GROUPS + g)),
                  pl.BlockSpec((None, PG, PG), lambda g: (g, 0, 0)),
                  pl.BlockSpec((1, PG), lambda g: (0, g))],
        out_specs=pl.BlockSpec((SEQ, PG), lambda g: (0, g)),
        out_shape=jax.ShapeDtypeStruct((SEQ, D_MODEL), BF16),
        compiler_params=_params(dimension_semantics=("arbitrary",)),
    )(proj, proj, pw_g, pool_scale)


REC_ROWS = 1024
REC_CHUNKS = REC_ROWS // CHUNK
N_REC_BLK = SEQ // REC_ROWS
REC_GROUP = REC_CHUNKS
REC_GROUP_BWD = REC_CHUNKS
SEC_BLK = D_MODEL // HEAD


def _lower_bound(lb_ref):
    l0 = lb_ref[0:1, :]
    l1 = lb_ref[1:2, :]
    mx = jnp.maximum(l0, l1)
    e0 = jnp.exp(l0 - mx)
    e1 = jnp.exp(l1 - mx)
    return e0 / (e0 + e1)


def _gates(q, fl, lb):
    qs = q * _sig(q)
    sf = _sig(fl)
    f = lb + (1.0 - lb) * sf
    return qs, sf, f, 1.0 - f, jnp.log(f)


LOG2E = 1.4426950408889634


def _level_factors(g2, qs, k, sign_ref):
    t = lax.broadcasted_iota(jnp.int32, (CHUNK, HEAD), 0)
    row = lambda r, n: jnp.broadcast_to(g2[r:r + 1, :], (n, HEAD))
    out = []
    for l in range(N_LEVELS):
        m = 1 << l
        if l == 0:
            g_mid = jnp.where((t & 1) == 1, pltpu.roll(g2, 1, 0), g2)
        elif l == 1:
            low = (t & 7) < 4
            g_mid = jnp.concatenate([jnp.where(low[:8], row(8 * i + 1, 8), row(8 * i + 5, 8))
                                     for i in range(CHUNK // 8)], axis=0)
        else:
            g_mid = jnp.concatenate([row(b * 2 * m + m - 1, 2 * m) for b in range(CHUNK // (2 * m))], axis=0)
        sgn = sign_ref[l]
        up = sgn > 0.0
        e = jnp.exp2((g2 - g_mid) * sgn)
        x = jnp.where(up, qs, k) * e
        hi = x.astype(BF16)
        out.append((hi, (x - hi.astype(F32)).astype(BF16), e, up))
    return out


CHIP_FLIPS = ((1, 0), (0, 1), (1, 1))
HBM = pl.BlockSpec(memory_space=pl.ANY)


def _place():
    return lax.axis_index("x"), lax.axis_index("y"), lax.axis_index("c")


def _remote(src, dst, send_sems, recv_sems, k, to):
    return pltpu.make_async_remote_copy(src_ref=src, dst_ref=dst, send_sem=send_sems.at[k],
                                        recv_sem=recv_sems.at[k], device_id=to, device_id_type=MESH)


def _half_rows(ref, c):
    half = ref.shape[-2] // 2
    rows = pl.ds(pl.multiple_of(c * half, half), half)
    return ref.at[:, rows, :] if len(ref.shape) == 3 else ref.at[rows, :]


def _other_core_barrier():
    x, y, c = _place()
    sem = pltpu.get_barrier_semaphore()
    pl.semaphore_signal(sem, inc=1, device_id=(x, y, 1 - c), device_id_type=MESH)
    pl.semaphore_wait(sem, 1)


class _Exchange:
    def __init__(self, inputs, out_shapes, n_sems, start, finish, aliases=None):
        self.inputs, self.out_shapes, self.n_sems = list(inputs), list(out_shapes), n_sems
        self.start, self.finish, self.aliases = start, finish, dict(aliases or {})


def _ex_swap(grads):
    def copies(ins, outs, send, recv):
        x, y, c = _place()
        return [_remote(_half_rows(g, 1 - c), o, send, recv, t, (x, y, 1 - c))
                for t, (g, o) in enumerate(zip(ins, outs))]

    def start(*refs):
        for cp in copies(*refs):
            cp.start()

    def finish(*refs):
        cps = copies(*refs)
        for cp in cps:
            cp.wait_recv()
        for cp in cps:
            cp.wait_send()

    shapes = [jax.ShapeDtypeStruct((a.shape[0], a.shape[1] // 2, a.shape[2]), F32) for a in grads]
    return _Exchange(grads, shapes, len(grads), start, finish)


def _ex_send(parts16, owners, units=None, landed=None):
    n_t = len(parts16)
    units = units or [tuple(range(len(o))) for o in owners]
    landed = landed or [None] * n_t
    given = [t for t in range(n_t) if landed[t] is not None]

    def each(ins, outs, send, recv, to_sender, to_owner):
        x, y, c = _place()
        k = 0
        for t, own in enumerate(owners):
            for j in units[t]:
                for r, (fx, fy) in enumerate(CHIP_FLIPS):
                    tx, ty = x ^ fx, y ^ fy
                    cp = _remote(ins[t].at[j], outs[t].at[j, r], send, recv, k, (tx, ty, c))
                    if to_sender is not None:
                        pl.when(2 * tx + ty == own[j])(functools.partial(to_sender, cp))
                    if to_owner is not None:
                        pl.when(2 * x + y == own[j])(functools.partial(to_owner, cp))
                    k += 1

    def start(*refs):
        each(*refs, lambda cp: cp.start(), None)

    def finish(*refs):
        each(*refs, None, lambda cp: cp.wait_recv())
        each(*refs, lambda cp: cp.wait_send(), None)

    shapes = [jax.ShapeDtypeStruct((a.shape[0], len(CHIP_FLIPS)) + a.shape[1:], BF16) for a in parts16]
    return _Exchange(list(parts16) + [landed[t] for t in given], shapes,
                     len(CHIP_FLIPS) * sum(len(u) for u in units), start, finish,
                     aliases={n_t + i: t for i, t in enumerate(given)})


def _ex_join(units, owners):
    def each(ins, outs, send, recv, fn):
        x, y, c = _place()
        k = 0
        for t, own in enumerate(owners):
            for j, o in enumerate(own):
                def half(cc, to, u=outs[t].at[j], k=k):
                    return _remote(_half_rows(u, cc), _half_rows(u, cc), send, recv, k, to)
                mine = functools.partial(half, c, (x, y, 1 - c))
                theirs = functools.partial(half, 1 - c, (x, y, c))
                pl.when(2 * x + y == o)(functools.partial(fn, mine, theirs))
                k += 1

    def start(*refs):
        each(*refs, lambda mine, theirs: mine().start())

    def finish(*refs):
        each(*refs, lambda mine, theirs: theirs().wait_recv())
        each(*refs, lambda mine, theirs: mine().wait_send())

    shapes = [jax.ShapeDtypeStruct(a.shape, F32) for a in units]
    return _Exchange(units, shapes, sum(len(o) for o in owners), start, finish,
                     aliases={t: t for t in range(len(units))})


def _ex_gather(slots):
    n_t = len(slots)
    n_fl = len(CHIP_FLIPS)

    def piece(ref, shard, half):
        return _half_rows(ref.at[shard], half)

    def first(outs, send, recv):
        x, y, c = _place()
        s = 2 * x + y
        return [_remote(piece(outs[t], s, c), piece(outs[t], s, c), send, recv, n_t * j + t, (x ^ fx, y ^ fy, c))
                for j, (fx, fy) in enumerate(CHIP_FLIPS) for t in range(n_t)]

    def start(ins, outs, send, recv):
        for cp in first(outs, send, recv):
            cp.start()

    def finish(ins, outs, send, recv):
        x, y, c = _place()
        passed = []
        for j, (fx, fy) in enumerate(CHIP_FLIPS):
            sj = 2 * (x ^ fx) + (y ^ fy)
            for t in range(n_t):
                k = n_t * j + t
                _remote(piece(outs[t], sj, c), piece(outs[t], sj, c), send, recv, k, (x, y, c)).wait_recv()
                cp = _remote(piece(outs[t], sj, c), piece(outs[t], sj, c), send, recv, n_t * n_fl + k, (x, y, 1 - c))
                cp.start()
                passed.append(cp)
        for j, (fx, fy) in enumerate(CHIP_FLIPS):
            sj = 2 * (x ^ fx) + (y ^ fy)
            for t in range(n_t):
                k = n_t * n_fl + n_t * j + t
                _remote(piece(outs[t], sj, 1 - c), piece(outs[t], sj, 1 - c), send, recv, k, (x, y, c)).wait_recv()
        for cp in first(outs, send, recv) + passed:
            cp.wait_send()

    shapes = [jax.ShapeDtypeStruct(a.shape, BF16) for a in slots]
    return _Exchange(slots, shapes, 2 * n_t * n_fl, start, finish, aliases={t: t for t in range(n_t)})


def _ex_gather_small(parts):
    def copies(ins, outs, send, recv):
        x, y, c = _place()
        me = 4 * x + 2 * y + c
        return [_remote(ins[0], outs[0].at[me], send, recv, mask - 1,
                        (x ^ (mask >> 2), y ^ ((mask >> 1) & 1), c ^ (mask & 1))) for mask in range(1, 8)]

    def start(*refs):
        for cp in copies(*refs):
            cp.start()

    def finish(ins, outs, send, recv):
        x, y, c = _place()
        me = 4 * x + 2 * y + c
        for mask in range(1, 8):
            _remote(ins[0], outs[0].at[me ^ mask], send, recv, mask - 1, (x, y, c)).wait_recv()
        for cp in copies(ins, outs, send, recv):
            cp.wait_send()

    return _Exchange([parts], [jax.ShapeDtypeStruct((8,) + parts.shape, F32)], 7, start, finish)


def _call(body, *, name, args=(), in_specs=(), out_specs=(), out_shape=(), grid=(), scratch_shapes=(),
          exchanges=()):
    n_in, n_out, n_scr = len(args), len(out_shape), len(scratch_shapes)
    ex_in, ex_out, ex_scr, spans, alias = [], [], [], [], {}
    for ex in exchanges:
        spans.append((len(ex_in), len(ex.inputs), len(ex_out), len(ex.out_shapes)))
        for i, o in ex.aliases.items():
            alias[n_in + len(ex_in) + i] = n_out + len(ex_out) + o
        ex_in += ex.inputs
        ex_out += ex.out_shapes
        ex_scr += [pltpu.SemaphoreType.DMA((ex.n_sems,)), pltpu.SemaphoreType.DMA((ex.n_sems,))]

    def full(*refs):
        ins, x_in = refs[:n_in], refs[n_in:n_in + len(ex_in)]
        outs = refs[n_in + len(ex_in):n_in + len(ex_in) + n_out]
        x_out = refs[n_in + len(ex_in) + n_out:n_in + len(ex_in) + n_out + len(ex_out)]
        scr = refs[len(refs) - n_scr - len(ex_scr):len(refs) - len(ex_scr)]
        sems = refs[len(refs) - len(ex_scr):]

        def run(which):
            for e, (ex, (i0, ni, o0, no)) in enumerate(zip(exchanges, spans)):
                getattr(ex, which)(x_in[i0:i0 + ni], x_out[o0:o0 + no], sems[2 * e], sems[2 * e + 1])

        if grid:
            ids = [pl.program_id(a) for a in range(len(grid))]
            is_first = functools.reduce(jnp.logical_and, [i == 0 for i in ids])
            is_last = functools.reduce(jnp.logical_and, [i == g - 1 for i, g in zip(ids, grid)])
            pl.when(is_first)(lambda: run("start"))
            body(*ins, *outs, *scr)
            pl.when(is_last)(lambda: run("finish"))
        else:
            run("start")
            if body is not None:
                body(*ins, *outs, *scr)
            run("finish")

    kw = dict(grid=grid) if grid else {}
    if grid:
        kw["compiler_params"] = _params(dimension_semantics=("arbitrary",) * len(grid))
    else:
        kw["compiler_params"] = _params()
    res = pl.pallas_call(
        full, name=name,
        in_specs=list(in_specs) + [HBM] * len(ex_in),
        out_specs=list(out_specs) + [HBM] * len(ex_out),
        out_shape=list(out_shape) + ex_out,
        scratch_shapes=list(scratch_shapes) + ex_scr,
        input_output_aliases=alias, **kw,
    )(*args, *ex_in)
    own = list(res[:n_out])
    per_ex = [list(res[n_out + o0:n_out + o0 + no]) for (_, _, o0, no) in spans]
    return own, per_ex


def _cast_own(w, place, name):
    rows, cols = w.shape
    tile = min(rows, 256)

    def body(place_ref, w_ref, o_ref):
        o_ref[...] = w_ref[...].astype(BF16)

    return pl.pallas_call(
        body, name=name,
        grid_spec=pltpu.PrefetchScalarGridSpec(
            num_scalar_prefetch=1, grid=(rows // tile,),
            in_specs=[pl.BlockSpec((tile, cols), lambda i, p: (i, 0))],
            out_specs=pl.BlockSpec((None, tile, cols), lambda i, p: (p[1], i, 0))),
        out_shape=jax.ShapeDtypeStruct((N_SHARDS, rows, cols), BF16),
        compiler_params=_params(dimension_semantics=("arbitrary",)),
    )(place, w)


def _cast_w_in(w, place):
    rows, cols = w.shape
    tile = 256

    def body(place_ref, w_ref, o_ref):
        o_ref[...] = w_ref[...].astype(BF16)

    return pl.pallas_call(
        body, name="cast_w_in",
        grid_spec=pltpu.PrefetchScalarGridSpec(
            num_scalar_prefetch=1, grid=(cols // COL_BLK, rows // tile),
            in_specs=[pl.BlockSpec((tile, COL_BLK), lambda b, i, p: (i, b))],
            out_specs=pl.BlockSpec((None, None, tile, COL_BLK), lambda b, i, p: (p[1], b, i, 0))),
        out_shape=jax.ShapeDtypeStruct((N_SHARDS, cols // COL_BLK, rows, COL_BLK), BF16),
        compiler_params=_params(dimension_semantics=("arbitrary", "arbitrary")),
    )(place, w)


def _rec_fwd(proj, lb_logits, rec_g, consts, exchanges):
    tri, low, sign = consts["tri"], consts["low"], consts["sign"]

    def body(q_ref, f_ref, i_ref, rg_ref, lb_ref, g_ref, w_ref, low_ref, sign_ref, y_ref, o_ref, stp_ref, st_ref):
        @pl.when(pl.program_id(1) == 0)
        def _():
            st_ref[...] = jnp.zeros_like(st_ref)
        lb = _lower_bound(lb_ref)
        st = st_ref[...]
        rows = lambda c: pl.ds(c * CHUNK, CHUNK)
        for c0 in range(0, REC_CHUNKS, REC_GROUP):
            group = range(c0, c0 + REC_GROUP)
            gated = [_gates(q_ref[rows(c), :], f_ref[rows(c), :], lb) for c in group]
            g2s = [_dot3(w_ref[...], g) * LOG2E for (_, _, _, _, g) in gated]
            xs = [[xl for xl, _, _, _ in _level_factors(g2, qs, k, sign_ref)]
                  for g2, (qs, _, _, k, _) in zip(g2s, gated)]
            a_s = []
            for x in xs:
                a = jnp.zeros((CHUNK, CHUNK), F32)
                for l, xl in enumerate(x):
                    a = a + _dot_nt(xl, xl) * low_ref[l]
                a_s.append(a.astype(BF16))
            vbs = [i_ref[rows(c), :].astype(BF16) for c in group]
            intra = [_dot(a, vb) for a, vb in zip(a_s, vbs)]
            kvs = [_dot_tn(vb, (k * jnp.exp2(g2[CHUNK - 1:CHUNK, :] - g2)).astype(BF16))
                   for vb, g2, (_, _, _, k, _) in zip(vbs, g2s, gated)]
            for i, c in enumerate(group):
                qs, _, _, k, _ = gated[i]
                g2 = g2s[i]
                stp_ref[c] = st
                v = i_ref[rows(c), :]
                rg = rg_ref[rows(c), :]
                o = (intra[i] + jnp.sum(qs * k, axis=-1, keepdims=True) * v
                     + _dot_nt((qs * jnp.exp2(g2)).astype(BF16), st.astype(BF16)))
                st = st * jnp.exp2(g2[CHUNK - 1:CHUNK, :]) + kvs[i]
                o_ref[rows(c), :] = o
                inv = lax.rsqrt(jnp.mean(o * o, axis=-1, keepdims=True) + EPS)
                y_ref[rows(c), :] = (o * inv * g_ref[...] * (rg * _sig(rg))).astype(BF16)
        st_ref[...] = st

    sec = lambda n: pl.BlockSpec((REC_ROWS, HEAD), lambda h, b: (b, n * SEC_BLK + h))
    vec = lambda rows: pl.BlockSpec((rows, HEAD), lambda h, b: (0, h))
    full = lambda a: pl.BlockSpec(a.shape, lambda h, b: (0,) * a.ndim)
    return _call(
        body, name="rec_fwd", grid=(N_HEADS, N_REC_BLK),
        args=(proj, proj, proj, proj, lb_logits, rec_g, tri, low, sign),
        in_specs=[sec(2), sec(3), sec(4), sec(5), vec(2), vec(1), full(tri), full(low), full(sign)],
        out_specs=[pl.BlockSpec((REC_ROWS, HEAD), lambda h, b: (b, h)),
                   pl.BlockSpec((REC_ROWS, HEAD), lambda h, b: (b, h)),
                   pl.BlockSpec((None, REC_CHUNKS, HEAD, HEAD), lambda h, b: (h, b, 0, 0))],
        out_shape=[jax.ShapeDtypeStruct((SEQ, D_MODEL), BF16),
                   jax.ShapeDtypeStruct((SEQ, D_MODEL), F32),
                   jax.ShapeDtypeStruct((N_HEADS, SEQ // CHUNK, HEAD, HEAD), F32)],
        scratch_shapes=[pltpu.VMEM((HEAD, HEAD), F32)],
        exchanges=exchanges)


OUT_ROWS = 512


def _out_proj_loss(y_pool, y_rec, w_out_g, x, target, gf):
    def body(yp_ref, yr_ref, w_ref, x_ref, t_ref, gf_ref, dout_ref, doutb_ref, part_ref):
        @pl.when(pl.program_id(0) == 0)
        def _():
            part_ref[...] = jnp.zeros_like(part_ref)
        halves = [pl.ds(a * (OUT_ROWS // 2), OUT_ROWS // 2) for a in range(2)]
        outs = [x_ref[r, :] + _dot(yp_ref[r, :], w_ref[0:D_MODEL, :])
                + _dot(yr_ref[r, :], w_ref[D_MODEL:2 * D_MODEL, :]) for r in halves]
        gf_v = gf_ref[...]
        for r, out in zip(halves, outs):
            inv = lax.rsqrt(jnp.mean(out * out, axis=-1, keepdims=True) + EPS)
            diff = out * inv * gf_v - t_ref[r, :]
            dyf = diff * (1.0 / D_MODEL)
            a = dyf * gf_v
            dout = inv * a - out * (inv * inv * inv) * jnp.mean(a * out, axis=-1, keepdims=True)
            dout_ref[r, :] = dout
            doutb_ref[r, :] = dout.astype(BF16)
            part_ref[0:1, :] += jnp.sum(dyf * out * inv, axis=0, keepdims=True)
            part_ref[1:2, :] += jnp.sum(diff * diff, axis=0, keepdims=True)

    row = lambda n: pl.BlockSpec((OUT_ROWS, n), lambda i: (i, 0))
    return pl.pallas_call(
        body, name="out_proj_loss", grid=(SEQ // OUT_ROWS,),
        in_specs=[row(D_MODEL), row(D_MODEL), pl.BlockSpec((2 * D_MODEL, D_MODEL), lambda i: (0, 0)),
                  row(D_MODEL), row(D_MODEL), pl.BlockSpec((1, D_MODEL), lambda i: (0, 0))],
        out_specs=[row(D_MODEL), row(D_MODEL), pl.BlockSpec((8, D_MODEL), lambda i: (0, 0))],
        out_shape=[jax.ShapeDtypeStruct((SEQ, D_MODEL), F32),
                   jax.ShapeDtypeStruct((SEQ, D_MODEL), BF16),
                   jax.ShapeDtypeStruct((8, D_MODEL), F32)],
        compiler_params=_params(dimension_semantics=("arbitrary",)),
    )(y_pool, y_rec, w_out_g, x, target, gf)


def _grad_w_out(y_pool, y_rec, dout_b):
    blk = W_OUT_SHARD // 2
    per = D_MODEL // blk
    n = 2 * per

    def body(yp_ref, yr_ref, d_ref, p32_ref, p16_ref, send_ref, recv_ref, send_sems, recv_sems):
        j = pl.program_id(0)
        x, y, c = _place()

        def copy(u):
            return _remote(send_ref.at[u], recv_ref.at[u], send_sems, recv_sems, u, (x, y, 1 - c))

        pl.when(j == 0)(_other_core_barrier)
        for i in range(n):
            @pl.when(j == i)
            def _(i=i):
                res = _dot_tn((yp_ref if i < per else yr_ref)[...], d_ref[...])

                @pl.when(i % 2 == c)
                def _():
                    p32_ref[i // 2] = res

                @pl.when(i % 2 != c)
                def _():
                    send_ref[i // 2] = res
                    copy(i // 2).start()

        @pl.when(j == n - 1)
        def _():
            for u in range(N_SHARDS):
                copy(u).wait_recv()
                tot = p32_ref[u] + recv_ref[u]
                p32_ref[u] = tot
                p16_ref[u] = tot.astype(BF16)
            for u in range(N_SHARDS):
                copy(u).wait_send()

    whole = pl.BlockSpec((N_SHARDS, blk, D_MODEL), lambda j: (0, 0, 0))
    return pl.pallas_call(
        body, name="grad_w_out", grid=(n,),
        in_specs=[pl.BlockSpec((SEQ, blk), lambda j: (0, jnp.minimum(j, per - 1))),
                  pl.BlockSpec((SEQ, blk), lambda j: (0, jnp.maximum(j - per, 0))),
                  pl.BlockSpec((SEQ, D_MODEL), lambda j: (0, 0))],
        out_specs=[whole, whole],
        out_shape=[jax.ShapeDtypeStruct((N_SHARDS, blk, D_MODEL), F32),
                   jax.ShapeDtypeStruct((N_SHARDS, blk, D_MODEL), BF16)],
        scratch_shapes=[pltpu.VMEM((N_SHARDS, blk, D_MODEL), F32), pltpu.VMEM((N_SHARDS, blk, D_MODEL), F32),
                        pltpu.SemaphoreType.DMA((N_SHARDS,)), pltpu.SemaphoreType.DMA((N_SHARDS,))],
        compiler_params=_params(dimension_semantics=("arbitrary",), collective_id=0),
    )(y_pool, y_rec, dout_b)


def _pool_bwd(proj, dout_b, w_out_g, pw_g, pool_scale, exchanges):
    n = POOL_ROWS + POOL_HALO

    def body(u_ref, gate_ref, d_ref, wo_ref, pw_ref, sc_ref,
             dp_ref, dpw_ref, dsc_ref, dd_ref, ddw_ref):
        g = pl.program_id(0)
        dpw_ref[...] = jnp.zeros_like(dpw_ref)
        dsc_ref[...] = jnp.zeros_like(dsc_ref)

        def first(ii, _):
            chunks = [POOL_PAIR * ii + a for a in range(POOL_PAIR)]
            rs = [pl.ds(pl.multiple_of(i * POOL_ROWS, POOL_ROWS), POOL_ROWS) for i in chunks]
            diffs = [_pool_diff(u_ref, i, g) for i in chunks]
            dbs = [d.astype(BF16) for d, _ in diffs]
            mixed = [_dot(db, pw_ref[...]) for db in dbs]
            dys = [_dot_nt(d_ref[r, :], wo_ref[...]) for r in rs]
            sc = sc_ref[...]
            dmbs = []
            for r, m, dy in zip(rs, mixed, dys):
                gate = gate_ref[r, :]
                sg = _sig(gate)
                silu = gate * sg
                dp_ref[1, r, :] = (dy * m * sc * (sg * (1.0 + gate * (1.0 - sg)))).astype(BF16)
                dsc_ref[...] += jnp.sum(dy * silu * m, axis=0, keepdims=True)
                dmbs.append((dy * silu * sc).astype(BF16))
            for db, dmb in zip(dbs, dmbs):
                dpw_ref[...] += _dot_tn(db, dmb)
            dds = [_dot_nt(dmb, pw_ref[...]) for dmb in dmbs]
            for r, dd, (_, inv_count) in zip(rs, dds, diffs):
                dd_ref[r, :] = dd
                ddw_ref[r, :] = dd * inv_count
            return 0
        lax.fori_loop(0, SEQ // POOL_ROWS // POOL_PAIR, first, 0)

        def second(i, _):
            r0 = i * POOL_ROWS
            r = pl.ds(pl.multiple_of(r0, POOL_ROWS), POOL_ROWS)
            last = i == SEQ // POOL_ROWS - 1
            after = ddw_ref[pl.ds(pl.multiple_of(jnp.minimum(r0 + POOL_ROWS, SEQ - POOL_HALO), 8), POOL_HALO), :]
            after = jnp.where(last, 0.0, after)
            ext = jnp.concatenate([ddw_ref[r, :], after], axis=0)
            s = _window_sums(ext, g, lambda k: n - (1 << k))[:POOL_ROWS, :]
            dp_ref[0, r, :] = (s - dd_ref[r, :]).astype(BF16)
            return 0
        lax.fori_loop(0, SEQ // POOL_ROWS, second, 0)

    return _call(
        body, name="pool_bwd", grid=(N_GROUPS,),
        args=(proj, proj, dout_b, w_out_g, pw_g, pool_scale),
        in_specs=[pl.BlockSpec((SEQ, PG), lambda g: (0, g)),
                  pl.BlockSpec((SEQ, PG), lambda g: (0, N_GROUPS + g)),
                  pl.BlockSpec((SEQ, D_MODEL), lambda g: (0, 0)),
                  pl.BlockSpec((PG, D_MODEL), lambda g: (g, 0)),
                  pl.BlockSpec((None, PG, PG), lambda g: (g, 0, 0)),
                  pl.BlockSpec((1, PG), lambda g: (0, g))],
        out_specs=[pl.BlockSpec((2, SEQ, PG), lambda g: (0, 0, g)),
                   pl.BlockSpec((None, PG, PG), lambda g: (g, 0, 0)),
                   pl.BlockSpec((1, PG), lambda g: (0, g))],
        out_shape=[jax.ShapeDtypeStruct((2, SEQ, D_MODEL), BF16),
                   jax.ShapeDtypeStruct((N_GROUPS, PG, PG), F32),
                   jax.ShapeDtypeStruct((1, D_MODEL), F32)],
        scratch_shapes=[pltpu.VMEM((SEQ, PG), F32), pltpu.VMEM((SEQ, PG), F32)],
        exchanges=exchanges)


HALF_HEADS = N_HEADS // 2
HALF_COLS = HALF_HEADS * HEAD


def _rec_bwd(proj, o_raw, st_prev, dout_b, w_out_g, lb_logits, rec_g, consts, h0, name, exchanges):
    def body(q_ref, f_ref, i_ref, rg_ref, o_ref, stp_ref, d_ref, wo_ref, lb_ref, g_ref,
             w_ref, lowt_ref, sym_ref, sign_ref, tri_ref,
             dr_ref, part_ref, dst_ref):
        @pl.when(pl.program_id(1) == 0)
        def _():
            dst_ref[...] = jnp.zeros_like(dst_ref)
            part_ref[...] = jnp.zeros_like(part_ref)
        tril = (lax.broadcasted_iota(jnp.int32, (CHUNK, CHUNK), 0)
                > lax.broadcasted_iota(jnp.int32, (CHUNK, CHUNK), 1))
        lb = _lower_bound(lb_ref)
        grec = g_ref[...]
        dst = dst_ref[...]
        acc_grec = jnp.zeros((1, HEAD), F32)
        acc_lb = jnp.zeros((1, HEAD), F32)
        rows = lambda c: pl.ds(c * CHUNK, CHUNK)
        for c0 in reversed(range(0, REC_CHUNKS, REC_GROUP_BWD)):
            group = list(reversed(range(c0, c0 + REC_GROUP_BWD)))
            dys = [_dot_nt(d_ref[rows(c), :], wo_ref[...]) for c in group]
            dos = []
            for c, dy in zip(group, dys):
                rg = rg_ref[rows(c), :]
                o = o_ref[rows(c), :]
                sg = _sig(rg)
                silu = rg * sg
                inv = lax.rsqrt(jnp.mean(o * o, axis=-1, keepdims=True) + EPS)
                recn = o * inv
                dr_ref[3, rows(c), :] = (dy * recn * grec * (sg * (1.0 + rg * (1.0 - sg)))).astype(BF16)
                acc_grec = acc_grec + jnp.sum(dy * silu * recn, axis=0, keepdims=True)
                drecn = dy * silu * grec
                dos.append(inv * drecn - o * (inv * inv * inv) * jnp.mean(drecn * o, axis=-1, keepdims=True))
            gated = [_gates(q_ref[rows(c), :], f_ref[rows(c), :], lb) for c in group]
            g2s = [_dot3(w_ref[...], g) * LOG2E for (_, _, _, _, g) in gated]
            levels = [_level_factors(g2, qs, k, sign_ref) for g2, (qs, _, _, k, _) in zip(g2s, gated)]
            a_ts = []
            for lev in levels:
                a_t = jnp.zeros((CHUNK, CHUNK), F32)
                for l, (xl, _, _, _) in enumerate(lev):
                    a_t = a_t + _dot_nt(xl, xl) * lowt_ref[l]
                a_ts.append(a_t.astype(BF16))
            dobs = [do.astype(BF16) for do in dos]
            vbs = [i_ref[rows(c), :].astype(BF16) for c in group]
            d_syms = [jnp.where(tril, _dot_nt(dob, vb), _dot_nt(vb, dob)) for dob, vb in zip(dobs, vbs)]
            dqs_is, dk_is = [], []
            for lev, d_sym in zip(levels, d_syms):
                dqs_i = jnp.zeros((CHUNK, HEAD), F32)
                both_i = jnp.zeros((CHUNK, HEAD), F32)
                for l, (xl, xlo, e, up) in enumerate(lev):
                    z = d_sym * sym_ref[l]
                    tmp = _dot(z.astype(BF16), jnp.concatenate([xl, xlo], axis=-1))
                    tmp = (tmp[:, :HEAD] + tmp[:, HEAD:]) * e
                    dqs_i = dqs_i + jnp.where(up, tmp, 0.0)
                    both_i = both_i + tmp
                dqs_is.append(dqs_i)
                dk_is.append(both_i - dqs_i)
            e_gs = [jnp.exp2(g2) for g2 in g2s]
            e_revs = [jnp.exp2(g2[CHUNK - 1:CHUNK, :] - g2) for g2 in g2s]
            e_lasts = [jnp.exp2(g2[CHUNK - 1:CHUNK, :]) for g2 in g2s]
            q_gs = [qs * e_g for (qs, _, _, _, _), e_g in zip(gated, e_gs)]
            kdecs = [k * e_rev for (_, _, _, k, _), e_rev in zip(gated, e_revs)]
            dv12 = [_dot(a_t, dob) + jnp.sum(qs * k, axis=-1, keepdims=True) * do
                    for a_t, dob, do, (qs, _, _, k, _) in zip(a_ts, dobs, dos, gated)]
            dq_gs = [_dot(dob, stp_ref[c].astype(BF16)) for c, dob in zip(group, dobs)]
            steps = [_dot_tn(dob, q_g.astype(BF16)) for dob, q_g in zip(dobs, q_gs)]
            dsts = []
            for e_last, step in zip(e_lasts, steps):
                dsts.append(dst)
                dst = dst * e_last + step
            dstbs = [d.astype(BF16) for d in dsts]
            dv3 = [_dot_nt(kdec.astype(BF16), dstb) for kdec, dstb in zip(kdecs, dstbs)]
            dkdecs = [_dot(vb, dstb) for vb, dstb in zip(vbs, dstbs)]
            dbig_gs, dg_lasts, dqss, dks = [], [], [], []
            for i, c in enumerate(group):
                qs, _, _, k, _ = gated[i]
                de_last = jnp.sum(stp_ref[c] * dsts[i], axis=0, keepdims=True)
                ddiag = jnp.sum(dos[i] * i_ref[rows(c), :], axis=-1, keepdims=True)
                dqss.append(dqs_is[i] + ddiag * k + dq_gs[i] * e_gs[i])
                dks.append(dk_is[i] + ddiag * qs + dkdecs[i] * e_revs[i])
                dg_rev = dkdecs[i] * kdecs[i]
                dg_lasts.append(jnp.sum(dg_rev, axis=0, keepdims=True) + de_last * e_lasts[i])
                dbig_gs.append(qs * dqs_is[i] - k * dk_is[i] + dq_gs[i] * q_gs[i] - dg_rev)
            dgs = [_dot3(tri_ref[...], dbig_g) + dg_last for dbig_g, dg_last in zip(dbig_gs, dg_lasts)]
            for i, c in enumerate(group):
                _, sf, f, _, _ = gated[i]
                q = q_ref[rows(c), :]
                df = dgs[i] / f - dks[i]
                dr_ref[1, rows(c), :] = (df * (1.0 - lb) * sf * (1.0 - sf)).astype(BF16)
                acc_lb = acc_lb + jnp.sum(df * (1.0 - sf), axis=0, keepdims=True)
                sq = _sig(q)
                dr_ref[0, rows(c), :] = (dqss[i] * (sq * (1.0 + q * (1.0 - sq)))).astype(BF16)
                dr_ref[2, rows(c), :] = (dv12[i] + dv3[i]).astype(BF16)
        dst_ref[...] = dst
        part_ref[0:1, :] += acc_grec
        part_ref[1:2, :] += acc_lb

    rev = lambda b: N_REC_BLK - 1 - b
    sec = lambda n: pl.BlockSpec((REC_ROWS, HEAD), lambda h, b: (rev(b), n * SEC_BLK + h0 + h))
    col_in = pl.BlockSpec((REC_ROWS, HEAD), lambda h, b: (rev(b), h0 + h))
    vec_in = lambda rows: pl.BlockSpec((rows, HEAD), lambda h, b: (0, h0 + h))
    full = lambda a: pl.BlockSpec(a.shape, lambda h, b: (0,) * a.ndim)
    return _call(
        body, name=name, grid=(HALF_HEADS, N_REC_BLK),
        args=(proj, proj, proj, proj, o_raw, st_prev, dout_b, w_out_g, lb_logits, rec_g,
              consts["tri"], consts["low_t"], consts["sym"], consts["sign"], consts["tri_t"]),
        in_specs=[sec(2), sec(3), sec(4), sec(5), col_in,
                  pl.BlockSpec((None, REC_CHUNKS, HEAD, HEAD), lambda h, b: (h0 + h, rev(b), 0, 0)),
                  pl.BlockSpec((REC_ROWS, D_MODEL), lambda h, b: (rev(b), 0)),
                  pl.BlockSpec((HEAD, D_MODEL), lambda h, b: (SEC_BLK + h0 + h, 0)),
                  vec_in(2), vec_in(1)] + [full(consts[n]) for n in ("tri", "low_t", "sym", "sign", "tri_t")],
        out_specs=[pl.BlockSpec((4, REC_ROWS, HEAD), lambda h, b: (0, rev(b), h)),
                   pl.BlockSpec((8, HEAD), lambda h, b: (0, h))],
        out_shape=[jax.ShapeDtypeStruct((4, SEQ, HALF_COLS), BF16),
                   jax.ShapeDtypeStruct((8, HALF_COLS), F32)],
        scratch_shapes=[pltpu.VMEM((HEAD, HEAD), F32)],
        exchanges=exchanges)


def _w_in_block(w_ref, j):
    per_shard = W_IN_SHARD // COL_BLK
    return w_ref[j // per_shard, j % per_shard]


def _grad_x(dproj, w_in_g, x, g1, dout, exchanges):
    rows = 256
    n_blk = len(dproj)

    def body(*refs):
        dp_refs = refs[:n_blk]
        w_ref, x_ref, g_ref, dout_ref, dx_ref, part_ref = refs[n_blk:]

        @pl.when(pl.program_id(0) == 0)
        def _():
            part_ref[...] = jnp.zeros_like(part_ref)
        dh = jnp.zeros((rows, D_MODEL), F32)
        for j in range(n_blk):
            dh = dh + _dot_nt(dp_refs[j][...], _w_in_block(w_ref, j))
        xv = x_ref[...]
        inv = lax.rsqrt(jnp.mean(xv * xv, axis=-1, keepdims=True) + EPS)
        a = dh * g_ref[...]
        dx_ref[...] = (dout_ref[...] + inv * a
                       - xv * (inv * inv * inv) * jnp.mean(a * xv, axis=-1, keepdims=True))
        part_ref[0:1, :] += jnp.sum(dh * xv * inv, axis=0, keepdims=True)

    row = lambda: pl.BlockSpec((rows, D_MODEL), lambda i: (i, 0))
    dp_spec = lambda sec, cb: pl.BlockSpec((None, rows, COL_BLK), lambda i: (sec, i, cb))
    return _call(
        body, name="grad_x", grid=(SEQ // rows,),
        args=tuple(a for a, _, _ in dproj) + (w_in_g, x, g1, dout),
        in_specs=[dp_spec(sec, cb) for _, sec, cb in dproj]
                 + [pl.BlockSpec(w_in_g.shape, lambda i: (0, 0, 0, 0)),
                    row(), pl.BlockSpec((1, D_MODEL), lambda i: (0, 0)), row()],
        out_specs=[row(), pl.BlockSpec((8, D_MODEL), lambda i: (0, 0))],
        out_shape=[jax.ShapeDtypeStruct((SEQ, D_MODEL), F32),
                   jax.ShapeDtypeStruct((8, D_MODEL), F32)],
        exchanges=exchanges)


def _grad_w_in(h, dp, blocks, name, collective_id):
    n_blk = len(blocks)
    half = D_MODEL // 2
    pick = lambda vals: (lambda j: functools.reduce(lambda acc, iv: jnp.where(j == iv[0], iv[1], acc),
                                                     list(enumerate(vals))[1:], vals[0]))
    sec_of = pick([sec for sec, _ in blocks])
    cb_of = pick([cb for _, cb in blocks])

    def body(h_ref, dp_ref, p32_ref, p16_ref, keep_ref, send_ref, recv_ref, send_sems, recv_sems):
        j = pl.program_id(0)
        x, y, c = _place()
        cols = lambda cc: pl.ds(pl.multiple_of(cc * half, half), half)

        def copy(i):
            return _remote(send_ref.at[i], recv_ref.at[i], send_sems, recv_sems, i, (x, y, 1 - c))

        pl.when(j == 0)(_other_core_barrier)
        for i in range(n_blk + 1):
            @pl.when(j == i)
            def _(i=i):
                if i < n_blk:
                    send_ref[i] = _dot_tn(h_ref[:, cols(1 - c)], dp_ref[...])
                    copy(i).start()
                    keep_ref[i] = _dot_tn(h_ref[:, cols(c)], dp_ref[...])
                if i > 0:
                    copy(i - 1).wait_recv()
                    tot = keep_ref[i - 1] + recv_ref[i - 1]
                    p32_ref[...] = tot
                    p16_ref[...] = tot.astype(BF16)

        @pl.when(j == n_blk)
        def _():
            for i in range(n_blk):
                copy(i).wait_send()

    lagged = pl.BlockSpec((None, half, COL_BLK), lambda j: (jnp.maximum(j - 1, 0), 0, 0))
    last = n_blk - 1
    return pl.pallas_call(
        body, name=name, grid=(n_blk + 1,),
        in_specs=[pl.BlockSpec((SEQ, D_MODEL), lambda j: (0, 0)),
                  pl.BlockSpec((None, SEQ, COL_BLK),
                               lambda j: (sec_of(jnp.minimum(j, last)), 0, cb_of(jnp.minimum(j, last))))],
        out_specs=[lagged, lagged],
        out_shape=[jax.ShapeDtypeStruct((n_blk, half, COL_BLK), F32),
                   jax.ShapeDtypeStruct((n_blk, half, COL_BLK), BF16)],
        scratch_shapes=[pltpu.VMEM((n_blk, half, COL_BLK), F32)] * 3
                       + [pltpu.SemaphoreType.DMA((n_blk,)), pltpu.SemaphoreType.DMA((n_blk,))],
        compiler_params=_params(dimension_semantics=("arbitrary",), collective_id=collective_id),
    )(h, dp)


def _add_units(grad, recv, place, tile, name):
    n, rows, cols = grad.shape
    per_half = rows // 2 // tile

    def body(place_ref, g_ref, r_ref, o32_ref, o16_ref):
        v = g_ref[...] + r_ref[...]
        o32_ref[...] = v
        o16_ref[...] = v.astype(BF16)

    blk = lambda f: pl.BlockSpec((None, tile, cols), f)
    out = lambda s, i, p: (s, i, 0)
    return pl.pallas_call(
        body, name=name,
        grid_spec=pltpu.PrefetchScalarGridSpec(
            num_scalar_prefetch=1, grid=(n, per_half),
            in_specs=[blk(lambda s, i, p: (s, p[0] * per_half + i, 0)), blk(out)],
            out_specs=[blk(out), blk(out)]),
        out_shape=[jax.ShapeDtypeStruct(recv.shape, F32), jax.ShapeDtypeStruct(recv.shape, BF16)],
        compiler_params=_params(dimension_semantics=("arbitrary", "arbitrary")),
    )(place, grad, recv)


def _sum_units(part32, recv16, place, owners, tile, name):
    n, half, cols = part32.shape
    per_half = half // tile
    table = np.array([[sum(o == chip for o in owners)] + sorted(range(n), key=lambda j: (owners[j] != chip, j))
                      for chip in range(N_SHARDS)], np.int32)
    sched = jnp.concatenate([place[:1], jnp.asarray(table)[place[1]]])

    def block(k, i, p):
        live = k < p[1]
        unit = p[2 + jnp.minimum(k, jnp.maximum(p[1] - 1, 0))]
        return unit, jnp.where(live, i, per_half - 1)

    def body(sched_ref, p_ref, r_ref, o_ref):
        @pl.when(pl.program_id(0) < sched_ref[1])
        def _():
            acc = p_ref[...]
            for j in range(len(CHIP_FLIPS)):
                acc = acc + r_ref[j].astype(F32)
            o_ref[...] = acc

    return pl.pallas_call(
        body, name=name,
        grid_spec=pltpu.PrefetchScalarGridSpec(
            num_scalar_prefetch=1, grid=(n, per_half),
            in_specs=[pl.BlockSpec((None, tile, cols), lambda k, i, p: (*block(k, i, p), 0)),
                      pl.BlockSpec((None, len(CHIP_FLIPS), tile, cols),
                                   lambda k, i, p: (block(k, i, p)[0], 0, block(k, i, p)[1], 0))],
            out_specs=pl.BlockSpec((None, tile, cols),
                                   lambda k, i, p: (block(k, i, p)[0], p[0] * per_half + block(k, i, p)[1], 0))),
        out_shape=jax.ShapeDtypeStruct((n, 2 * half, cols), F32),
        compiler_params=_params(dimension_semantics=("arbitrary", "arbitrary")),
    )(sched, part32, recv16)


def _adamw_math(w, g, m, v):
    m = ADAM_B1 * m + (1.0 - ADAM_B1) * g
    v = ADAM_B2 * v + (1.0 - ADAM_B2) * (g * g)
    m_hat = m / (1.0 - ADAM_B1 ** ADAM_STEP)
    v_hat = v / (1.0 - ADAM_B2 ** ADAM_STEP)
    delta = -ADAM_LR * (m_hat / (jnp.sqrt(v_hat) + ADAM_EPS) + ADAM_WD * w)
    return delta, m, v


def _adamw_units(w, m, v, grads, pick, name):
    rows, cols = w.shape
    bc = grads[0].shape[-1]
    tile = min(rows, 256)
    n_g = len(grads)

    def body(pick_ref, w_ref, m_ref, v_ref, *refs):
        g_refs, (g_out, d_ref, nm_ref, nv_ref) = refs[:n_g], refs[n_g:]
        p = pl.program_id(0)
        for a in range(n_g):
            @pl.when(pick_ref[0, p] == a)
            def _(a=a):
                g = g_refs[a][...]
                g_out[...] = g
                d_ref[...], nm_ref[...], nv_ref[...] = _adamw_math(w_ref[...], g, m_ref[...], v_ref[...])

    blk = pl.BlockSpec((tile, bc), lambda p, i, pick: (i, p))

    def g_spec(a):
        return pl.BlockSpec((None, tile, bc),
                            lambda p, i, pick: (jnp.where(pick[0, p] == a, pick[1, p], 0),
                                                jnp.where(pick[0, p] == a, i, 0), 0))

    return pl.pallas_call(
        body, name=name,
        grid_spec=pltpu.PrefetchScalarGridSpec(
            num_scalar_prefetch=1, grid=(cols // bc, rows // tile),
            in_specs=[blk] * 3 + [g_spec(a) for a in range(n_g)],
            out_specs=[blk] * 4),
        out_shape=[jax.ShapeDtypeStruct(w.shape, F32)] * 4,
        compiler_params=_params(dimension_semantics=("arbitrary", "arbitrary")),
    )(pick, w, m, v, *grads)


ROW_NORM1, ROW_SCALE, ROW_LB, ROW_REC, ROW_FINAL, ROW_LOSS = 0, 1, 2, 4, 5, 6


SMALL_ROWS = (ROW_NORM1, ROW_SCALE, ROW_LB, ROW_REC, ROW_FINAL)


def _small_update(parts, gathered, params):
    n_p = len(params)

    def body(own_ref, p_ref, *refs):
        ins, loss_ref, outs = refs[:3 * n_p], refs[3 * n_p], refs[3 * n_p + 1:]
        x, y, c = _place()
        me = 4 * x + 2 * y + c
        slot = lambda d: jnp.where(me == d, own_ref[...], p_ref[d])
        tot = slot(0)
        for d in range(1, 8):
            tot = tot + slot(d)
        for i, r in enumerate(SMALL_ROWS):
            w = ins[3 * i][...]
            g = tot[r:r + 1, :]
            if r == ROW_LB:
                mx = jnp.maximum(w[0:1, :], w[1:2, :])
                e0 = jnp.exp(w[0:1, :] - mx)
                e1 = jnp.exp(w[1:2, :] - mx)
                lb = e0 / (e0 + e1)
                g = g * lb * (1.0 - lb)
                g = jnp.concatenate([g, -g], axis=0)
            outs[4 * i][...] = g
            outs[4 * i + 1][...], outs[4 * i + 2][...], outs[4 * i + 3][...] = _adamw_math(
                w, g, ins[3 * i + 1][...], ins[3 * i + 2][...])
        loss_ref[...] = (0.5 / D_MODEL) * jnp.sum(tot[ROW_LOSS:ROW_LOSS + 1, :], axis=-1, keepdims=True)

    flat = [a for wmv in params for a in wmv]
    return pl.pallas_call(
        body, name="small_update",
        out_shape=[jax.ShapeDtypeStruct((1, 1), F32)]
                  + [jax.ShapeDtypeStruct(w.shape, F32) for w, _, _ in params for _ in range(4)],
        compiler_params=_params(),
    )(parts, gathered, *flat)


SHARD_OWNERS = tuple(range(N_SHARDS))
BLOCKS_POOL = (0, 1, 2, 3)
BLOCKS_A = (4, 6, 8, 10)
BLOCKS_B = (5, 7, 9, 11)
BLOCK_GROUPS = (BLOCKS_POOL, BLOCKS_A, BLOCKS_B)


def _block_owners(blocks):
    return tuple(j // (W_IN_SHARD // COL_BLK) for j in blocks)


def kernel(x, norm1_g, w_in, pool_w, pool_scale, lb_logits, rec_norm_g, w_out, final_norm_g, loss_target, m_norm1_g, m_w_in, m_pool_w, m_pool_scale, m_lb_logits, m_rec_norm_g, m_w_out, m_final_norm_g, v_norm1_g, v_w_in, v_pool_w, v_pool_scale, v_lb_logits, v_rec_norm_g, v_w_out, v_final_norm_g):
    xi, yi, ci = _place()
    chip = 2 * xi + yi
    place = jnp.stack([ci, chip]).astype(jnp.int32)
    pw_rows = N_GROUPS * PW_SHARD
    flat_pw = lambda a: a.reshape(pw_rows, PG)
    x2, target, gf = x[0], loss_target[0], final_norm_g.reshape(1, D_MODEL)
    consts = {n: jnp.asarray(a, BF16 if n.startswith("tri") else F32) for n, a in _chunk_constants().items()}

    proj, h, w_in_g = _in_proj(x2, norm1_g, _cast_w_in(w_in[0], place), place)
    (y_rec, o_raw, st_prev), ((w_out_g, pw_g),) = _rec_fwd(
        proj, lb_logits, rec_norm_g, consts,
        [_ex_gather([_cast_own(w_out[0], place, "cast_w_out"), _cast_own(flat_pw(pool_w), place, "cast_pool_w")])])
    w_out_g = w_out_g.reshape(2 * D_MODEL, D_MODEL)
    pw_full = pw_g.reshape(N_SHARDS, N_GROUPS, PW_SHARD, PG).transpose(1, 0, 2, 3).reshape(N_GROUPS, PG, PG)
    y_pool = _pool_fwd(proj, pw_full, pool_scale)
    dout, dout_b, part_out = _out_proj_loss(y_pool, y_rec, w_out_g, x2, target, gf)

    p_out32, p_out16 = _grad_w_out(y_pool, y_rec, dout_b)
    (dpool, gpw, dscale), ((rb_out,),) = _pool_bwd(proj, dout_b, w_out_g, pw_full, pool_scale,
                                                   [_ex_send([p_out16], [SHARD_OWNERS])])
    g_out = _sum_units(p_out32, rb_out, place, SHARD_OWNERS, 256, "sum_w_out")
    gpw = gpw.reshape(N_GROUPS, N_SHARDS, PW_SHARD, PG).transpose(1, 0, 2, 3).reshape(N_SHARDS, pw_rows, PG)
    p_inp32, p_inp16 = _grad_w_in(h, dpool, [(0, 0), (0, 1), (1, 0), (1, 1)], "grad_w_in_pool", 1)

    pool_owners, a_owners, b_owners = (_block_owners(b) for b in BLOCK_GROUPS)
    rec_args = (proj, o_raw, st_prev, dout_b, w_out_g, lb_logits, rec_norm_g, consts)
    (drec_a, part_a), ((rb_inp,), (ra_pw,)) = _rec_bwd(
        *rec_args, 0, "rec_bwd_a", [_ex_send([p_inp16], [pool_owners], units=[(0, 1)]), _ex_swap([gpw])])
    p_pw32, p_pw16 = _add_units(gpw, ra_pw, place, 128, "add_pool_w")
    rec_blocks = [(n, 0) for n in range(4)]
    p_ina32, p_ina16 = _grad_w_in(h, drec_a, rec_blocks, "grad_w_in_a", 2)

    (drec_b, part_b), ((rb_inp, rb_ina, rb_pw),) = _rec_bwd(
        *rec_args, HALF_HEADS, "rec_bwd_b",
        [_ex_send([p_inp16, p_ina16, p_pw16], [pool_owners, a_owners, SHARD_OWNERS],
                  units=[(2, 3), (0, 1, 2, 3), (0, 1, 2, 3)], landed=[rb_inp, None, None])])
    g_inp = _sum_units(p_inp32, rb_inp, place, pool_owners, 256, "sum_w_in_pool")
    g_ina = _sum_units(p_ina32, rb_ina, place, a_owners, 256, "sum_w_in_a")
    g_pw = _sum_units(p_pw32, rb_pw, place, SHARD_OWNERS, 128, "sum_pool_w")
    p_inb32, p_inb16 = _grad_w_in(h, drec_b, rec_blocks, "grad_w_in_b", 3)

    dproj = ([(dpool, 0, 0), (dpool, 0, 1), (dpool, 1, 0), (dpool, 1, 1)]
             + [(d, n, 0) for n in range(4) for d in (drec_a, drec_b)])
    (dx, part_x), ((rb_inb,),) = _grad_x(dproj, w_in_g, x2, norm1_g, dout, [_ex_send([p_inb16], [b_owners])])
    g_inb = _sum_units(p_inb32, rb_inb, place, b_owners, 256, "sum_w_in_b")
    zero = jnp.zeros((1, D_MODEL), F32)
    part_rec = jnp.concatenate([part_a, part_b], axis=1)
    parts = jnp.concatenate([part_x[0:1], dscale, part_rec[1:2], zero, part_rec[0:1], part_out[0:1],
                             part_out[1:2], zero], axis=0)
    _, ((g_out, g_pw, g_inp, g_ina, g_inb), (gathered,)) = _call(
        None, name="join_halves",
        exchanges=[_ex_join([g_out, g_pw, g_inp, g_ina, g_inb],
                            [SHARD_OWNERS, SHARD_OWNERS, pool_owners, a_owners, b_owners]),
                   _ex_gather_small(parts)])

    group_of = np.zeros((D_PROJ // COL_BLK,), np.int32)
    index_of = np.zeros((D_PROJ // COL_BLK,), np.int32)
    for gi, blocks in enumerate(BLOCK_GROUPS):
        for i, j in enumerate(blocks):
            group_of[j], index_of[j] = gi, i
    per_shard = W_IN_SHARD // COL_BLK
    pick_in = jnp.stack([lax.dynamic_slice(jnp.asarray(group_of), (per_shard * chip,), (per_shard,)),
                         lax.dynamic_slice(jnp.asarray(index_of), (per_shard * chip,), (per_shard,))])
    pick_own = jnp.stack([jnp.zeros((1,), jnp.int32), chip.reshape(1).astype(jnp.int32)])
    big = [_adamw_units(w_in[0], m_w_in[0], v_w_in[0], [g_inp, g_ina, g_inb], pick_in, "adamw_w_in"),
           _adamw_units(w_out[0], m_w_out[0], v_w_out[0], [g_out], pick_own, "adamw_w_out"),
           _adamw_units(flat_pw(pool_w), flat_pw(m_pool_w), flat_pw(v_pool_w), [g_pw], pick_own, "adamw_pool_w")]

    row = lambda a: a.reshape(1, D_MODEL)
    loss, *small = _small_update(parts, gathered, [
        (norm1_g, m_norm1_g, v_norm1_g), (pool_scale, m_pool_scale, v_pool_scale),
        (lb_logits, m_lb_logits, v_lb_logits), (rec_norm_g, m_rec_norm_g, v_rec_norm_g),
        (row(final_norm_g), row(m_final_norm_g), row(v_final_norm_g))])

    def leaves(k):
        norm1, scale, lb, rec, final = (small[4 * i + k] for i in range(len(SMALL_ROWS)))
        return (norm1, big[0][k][None], big[2][k].reshape(pool_w.shape), scale, lb, rec,
                big[1][k][None], final.reshape(D_MODEL))

    return (loss.reshape(()), dx[None], *leaves(0), *leaves(1), *leaves(2), *leaves(3))
```

```python
import functools

import numpy as np
import jax
import jax.numpy as jnp
from jax import lax
from jax.experimental import pallas as pl
from jax.experimental.pallas import tpu as pltpu

F32 = jnp.float32
BF16 = jnp.bfloat16

SEQ = 2048
D_MODEL = 1024
D_PROJ = 6144
N_SEC = 6
N_GROUPS = 4
PG = 256
N_HEADS = 8
HEAD = 128
CHUNK = 64
N_LEVELS = 6
N_SHARDS = 4
W_IN_SHARD = D_PROJ // N_SHARDS
W_OUT_SHARD = 2048 // N_SHARDS
PW_SHARD = PG // N_SHARDS
COL_BLK = 512
EPS = 1e-6

ADAM_LR = 0.001
ADAM_B1 = 0.9
ADAM_B2 = 0.999
ADAM_EPS = 1e-08
ADAM_WD = 0.01
ADAM_STEP = 10

V7X_VMEM_LIMIT = 56 * 1024 * 1024
MESH = pl.DeviceIdType.MESH


def _params(**kw):
    return pltpu.CompilerParams(vmem_limit_bytes=V7X_VMEM_LIMIT, **kw)


def _sig(x):
    return 1.0 / (1.0 + jnp.exp(-x))


def _dot(a, b):
    return jnp.dot(a, b, preferred_element_type=F32)


def _dot_nt(a, b):
    return lax.dot_general(a, b, (((1,), (1,)), ((), ())), preferred_element_type=F32)


def _dot_tn(a, b):
    return lax.dot_general(a, b, (((0,), (0,)), ((), ())), preferred_element_type=F32)


def _split3(a):
    p1 = a.astype(BF16)
    r1 = a - p1.astype(F32)
    p2 = r1.astype(BF16)
    p3 = (r1 - p2.astype(F32)).astype(BF16)
    return jnp.concatenate([p1, p2, p3], axis=-1)


def _dot3(w01, a):
    n = a.shape[-1]
    r = _dot(w01, _split3(a))
    return r[:, :n] + r[:, n:2 * n] + r[:, 2 * n:]


def _chunk_constants():
    j = np.arange(CHUNK)
    tt, ss = np.meshgrid(j, j, indexing="ij")
    x = tt ^ ss
    hb = np.full((CHUNK, CHUNK), -1, np.int32)
    for l in range(N_LEVELS):
        hb[x >= (1 << l)] = l
    sym = np.stack([(hb == l) for l in range(N_LEVELS)]).astype(np.float32)
    low = sym * (tt > ss)
    sign = np.stack([np.where((j >> l) & 1, 1.0, -1.0) for l in range(N_LEVELS)]).astype(np.float32)
    sign = np.ascontiguousarray(np.broadcast_to(sign[:, :, None], (N_LEVELS, CHUNK, HEAD)))
    tri = (ss <= tt).astype(np.float32)
    return dict(tri=tri, tri_t=np.ascontiguousarray(tri.T), low=low,
                low_t=np.ascontiguousarray(low.transpose(0, 2, 1)), sym=sym, sign=sign)


def _in_proj(x, g1, w_slots, place):
    n_col = D_PROJ // COL_BLK
    per_shard = W_IN_SHARD // COL_BLK
    rows = 1024
    half_rows = D_MODEL // 2
    quarter_rows = D_MODEL // 4
    FLIPS = (0, 2, 1, 3)
    ORDER = ([(0, p) for p in range(per_shard)] + [(m, p) for p in range(per_shard) for m in (1, 2)]
             + [(3, p) for p in range(per_shard)])

    def shard_at(m, chip):
        return chip ^ FLIPS[m]

    def pick(vals, t):
        return functools.reduce(lambda acc, iv: jnp.where(t == iv[0], iv[1], acc), list(enumerate(vals))[1:], vals[0])

    def body(place_ref, x_ref, g_ref, w_in_ref, proj_ref, h_ref, w_ref, wbuf, load_sems, send_sems, recv_sems):
        t = pl.program_id(0)
        x_, y_, c = _place()
        chip = 2 * x_ + y_
        me, other_core = (x_, y_, c), (x_, y_, 1 - c)
        x_nbr, y_nbr = (1 - x_, y_, c), (x_, 1 - y_, c)

        def rows_of(half, q=None):
            if q is None:
                return pl.ds(pl.multiple_of(half * half_rows, half_rows), half_rows)
            return pl.ds(pl.multiple_of(half * half_rows + q * quarter_rows, quarter_rows), quarter_rows)

        def block(m, p, r):
            return w_ref.at[shard_at(m, chip), p, r, :]

        def copy(k, ref, to):
            return _remote(ref, ref, send_sems, recv_sems, k, to)

        direct = lambda n, p, to: copy(3 * n + p, block(0, p, rows_of(c)), to)
        relay = lambda n, p, to: copy(6 + 3 * n + p, block(1 + n, p, rows_of(c, n)), to)
        arrived = lambda m, p: ([copy(3 * (m - 1) + p, block(m, p, rows_of(c)), me)] if m < 3 else
                                [copy(6 + 3 * n + p, block(3, p, rows_of(c, n)), me) for n in (0, 1)])
        passed_on = lambda m, p, half, to: copy(9 + 3 * m + p, block(m, p, rows_of(half)), to)

        def load(m, p, slot):
            return pltpu.make_async_copy(w_ref.at[shard_at(m, chip), p], wbuf.at[slot], load_sems.at[slot])

        def prepare(m, p):
            for cp in arrived(m, p):
                cp.wait_recv()
            passed_on(m, p, c, other_core).start()
            if m < 3:
                relay(m - 1, p, y_nbr if m == 1 else x_nbr).start()

        @pl.when(t == 0)
        def _():
            for p in range(per_shard):
                direct(0, p, x_nbr).start()
                direct(1, p, y_nbr).start()
            for p in range(per_shard):
                load(0, p, p).start()

            def norm(i, _):
                r = pl.ds(pl.multiple_of(i * rows, rows), rows)
                xv = x_ref[r, :]
                inv = lax.rsqrt(jnp.mean(xv * xv, axis=-1, keepdims=True) + EPS)
                h_ref[r, :] = (xv * inv * g_ref[...]).astype(BF16)
                return 0
            lax.fori_loop(0, SEQ // rows, norm, 0)

        for step, (m, p) in enumerate(ORDER):
            @pl.when(t == step)
            def _(step=step, m=m, p=p):
                slot = step % per_shard
                if m > 0:
                    passed_on(m, p, 1 - c, me).wait_recv()
                    load(m, p, slot).start()
                if step + 1 < n_col and ORDER[step + 1][0] > 0:
                    prepare(*ORDER[step + 1])
                load(m, p, slot).wait()

                def mm(i, _):
                    r = pl.ds(pl.multiple_of(i * rows, rows), rows)
                    proj_ref[r, :] = _dot(h_ref[r, :], wbuf[slot])
                    return 0
                lax.fori_loop(0, SEQ // rows, mm, 0)

        @pl.when(t == n_col - 1)
        def _():
            for p in range(per_shard):
                sent = [direct(0, p, x_nbr), direct(1, p, y_nbr), relay(0, p, y_nbr), relay(1, p, x_nbr)]
                for cp in sent + [passed_on(m, p, c, other_core) for m in (1, 2, 3)]:
                    cp.wait_send()

    return pl.pallas_call(
        body, name="in_proj",
        grid_spec=pltpu.PrefetchScalarGridSpec(
            num_scalar_prefetch=1, grid=(n_col,),
            in_specs=[pl.BlockSpec((SEQ, D_MODEL), lambda t, p: (0, 0)),
                      pl.BlockSpec((1, D_MODEL), lambda t, p: (0, 0)),
                      pl.BlockSpec(memory_space=pl.ANY)],
            out_specs=[pl.BlockSpec((None, SEQ, COL_BLK),
                                    lambda t, p: (per_shard * (p[1] ^ pick([FLIPS[m] for m, _ in ORDER], t))
                                                  + pick([b for _, b in ORDER], t), 0, 0)),
                       pl.BlockSpec((SEQ, D_MODEL), lambda t, p: (0, 0)),
                       pl.BlockSpec(memory_space=pl.ANY)],
            scratch_shapes=[pltpu.VMEM((per_shard, D_MODEL, COL_BLK), BF16),
                            pltpu.SemaphoreType.DMA((per_shard,)),
                            pltpu.SemaphoreType.DMA((21,)), pltpu.SemaphoreType.DMA((21,))]),
        out_shape=[jax.ShapeDtypeStruct((n_col, SEQ, COL_BLK), F32),
                   jax.ShapeDtypeStruct((SEQ, D_MODEL), BF16),
                   jax.ShapeDtypeStruct(w_slots.shape, BF16)],
        input_output_aliases={3: 2},
        compiler_params=_params(dimension_semantics=("arbitrary",)),
    )(place, x, g1, w_slots)


def _proj_cols(width, section, where, rows=SEQ):
    per_blk = COL_BLK // width

    def index(*grid):
        k, r = where(*grid)
        return section * (D_MODEL // COL_BLK) + k // per_blk, r, k % per_blk

    return pl.BlockSpec((None, rows, width), index)


POOL_ROWS = 256
POOL_HALO = 16
POOL_PAIR = 2


def _window_sums(ext, g, shift_of):
    s = ext
    for k in range(N_GROUPS):
        s = jnp.where(k <= g, s + pltpu.roll(s, shift_of(k), 0), s)
    return s


def _pool_diff(u_ref, i, g):
    n = POOL_ROWS + POOL_HALO
    r0 = i * POOL_ROWS
    cur = u_ref[pl.ds(pl.multiple_of(r0, POOL_ROWS), POOL_ROWS), :]
    before = u_ref[pl.ds(pl.multiple_of(jnp.maximum(r0 - POOL_HALO, 0), 8), POOL_HALO), :]
    before = jnp.where(i > 0, before, 0.0)
    ext = jnp.concatenate([before, cur], axis=0)
    s = _window_sums(ext, g, lambda k: 1 << k)[POOL_HALO:, :]
    t = r0 + lax.broadcasted_iota(jnp.int32, (POOL_ROWS, 1), 0)
    width = (2 << g).astype(F32)
    inv_count = 1.0 / jnp.minimum((t + 1).astype(F32), width)
    return s * inv_count - cur, inv_count


def _pool_fwd(proj, pw_g, pool_scale):
    def body(u_ref, gate_ref, pw_ref, sc_ref, y_ref):
        g = pl.program_id(0)

        def step(ii, _):
            chunks = [POOL_PAIR * ii + a for a in range(POOL_PAIR)]
            ds = [_pool_diff(u_ref, i, g)[0].astype(BF16) for i in chunks]
            mixed = [_dot(d, pw_ref[...]) for d in ds]
            for i, m in zip(chunks, mixed):
                r = pl.ds(pl.multiple_of(i * POOL_ROWS, POOL_ROWS), POOL_ROWS)
                gate = gate_ref[r, :]
                y_ref[r, :] = (m * sc_ref[...] * (gate * _sig(gate))).astype(BF16)
            return 0
        lax.fori_loop(0, SEQ // POOL_ROWS // POOL_PAIR, step, 0)

    return pl.pallas_call(
        body, name="pool_fwd", grid=(N_GROUPS,),
        in_specs=[_proj_cols(PG, 0, lambda g: (g, 0)), _proj_cols(PG, 1, lambda g: (g, 0)),
                  pl.BlockSpec((None, PG, PG), lambda g: (g, 0, 0)),
                  pl.BlockSpec((1, PG), lambda g: (0, g))],
        out_specs=pl.BlockSpec((SEQ, PG), lambda g: (0, g)),
        out_shape=jax.ShapeDtypeStruct((SEQ, D_MODEL), BF16),
        compiler_params=_params(dimension_semantics=("arbitrary",)),
    )(proj, proj, pw_g, pool_scale)


REC_ROWS = 1024
REC_CHUNKS = REC_ROWS // CHUNK
N_REC_BLK = SEQ // REC_ROWS
REC_GROUP = REC_CHUNKS
REC_GROUP_BWD = REC_CHUNKS
SEC_BLK = D_MODEL // HEAD


def _lower_bound(lb_ref):
    l0 = lb_ref[0:1, :]
    l1 = lb_ref[1:2, :]
    mx = jnp.maximum(l0, l1)
    e0 = jnp.exp(l0 - mx)
    e1 = jnp.exp(l1 - mx)
    return e0 / (e0 + e1)


def _gates(q, fl, lb):
    qs = q * _sig(q)
    sf = _sig(fl)
    f = lb + (1.0 - lb) * sf
    return qs, sf, f, 1.0 - f, jnp.log(f)


LOG2E = 1.4426950408889634


def _level_factors(g2, qs, k, sign_ref):
    t = lax.broadcasted_iota(jnp.int32, (CHUNK, HEAD), 0)
    row = lambda r, n: jnp.broadcast_to(g2[r:r + 1, :], (n, HEAD))
    out = []
    for l in range(N_LEVELS):
        m = 1 << l
        if l == 0:
            g_mid = jnp.where((t & 1) == 1, pltpu.roll(g2, 1, 0), g2)
        elif l == 1:
            low = (t & 7) < 4
            g_mid = jnp.concatenate([jnp.where(low[:8], row(8 * i + 1, 8), row(8 * i + 5, 8))
                                     for i in range(CHUNK // 8)], axis=0)
        else:
            g_mid = jnp.concatenate([row(b * 2 * m + m - 1, 2 * m) for b in range(CHUNK // (2 * m))], axis=0)
        sgn = sign_ref[l]
        up = sgn > 0.0
        e = jnp.exp2((g2 - g_mid) * sgn)
        x = jnp.where(up, qs, k) * e
        hi = x.astype(BF16)
        out.append((hi, (x - hi.astype(F32)).astype(BF16), e, up))
    return out


CHIP_FLIPS = ((1, 0), (0, 1), (1, 1))
HBM = pl.BlockSpec(memory_space=pl.ANY)


def _place():
    return lax.axis_index("x"), lax.axis_index("y"), lax.axis_index("c")


def _remote(src, dst, send_sems, recv_sems, k, to):
    return pltpu.make_async_remote_copy(src_ref=src, dst_ref=dst, send_sem=send_sems.at[k],
                                        recv_sem=recv_sems.at[k], device_id=to, device_id_type=MESH)


def _half_rows(ref, c):
    half = ref.shape[-2] // 2
    rows = pl.ds(pl.multiple_of(c * half, half), half)
    return ref.at[:, rows, :] if len(ref.shape) == 3 else ref.at[rows, :]


def _other_core_barrier():
    x, y, c = _place()
    sem = pltpu.get_barrier_semaphore()
    pl.semaphore_signal(sem, inc=1, device_id=(x, y, 1 - c), device_id_type=MESH)
    pl.semaphore_wait(sem, 1)


class _Exchange:
    def __init__(self, inputs, out_shapes, n_sems, start, finish, aliases=None):
        self.inputs, self.out_shapes, self.n_sems = list(inputs), list(out_shapes), n_sems
        self.start, self.finish, self.aliases = start, finish, dict(aliases or {})


def _ex_swap(grads):
    def copies(ins, outs, send, recv):
        x, y, c = _place()
        return [_remote(_half_rows(g, 1 - c), o, send, recv, t, (x, y, 1 - c))
                for t, (g, o) in enumerate(zip(ins, outs))]

    def start(*refs):
        for cp in copies(*refs):
            cp.start()

    def finish(*refs):
        cps = copies(*refs)
        for cp in cps:
            cp.wait_recv()
        for cp in cps:
            cp.wait_send()

    shapes = [jax.ShapeDtypeStruct((a.shape[0], a.shape[1] // 2, a.shape[2]), F32) for a in grads]
    return _Exchange(grads, shapes, len(grads), start, finish)


def _ex_send(parts16, owners, units=None, landed=None):
    n_t = len(parts16)
    units = units or [tuple(range(len(o))) for o in owners]
    landed = landed or [None] * n_t
    given = [t for t in range(n_t) if landed[t] is not None]

    def each(ins, outs, send, recv, to_sender, to_owner):
        x, y, c = _place()
        k = 0
        for t, own in enumerate(owners):
            for j in units[t]:
                for r, (fx, fy) in enumerate(CHIP_FLIPS):
                    tx, ty = x ^ fx, y ^ fy
                    cp = _remote(ins[t].at[j], outs[t].at[j, r], send, recv, k, (tx, ty, c))
                    if to_sender is not None:
                        pl.when(2 * tx + ty == own[j])(functools.partial(to_sender, cp))
                    if to_owner is not None:
                        pl.when(2 * x + y == own[j])(functools.partial(to_owner, cp))
                    k += 1

    def start(*refs):
        each(*refs, lambda cp: cp.start(), None)

    def finish(*refs):
        each(*refs, None, lambda cp: cp.wait_recv())
        each(*refs, lambda cp: cp.wait_send(), None)

    shapes = [jax.ShapeDtypeStruct((a.shape[0], len(CHIP_FLIPS)) + a.shape[1:], BF16) for a in parts16]
    return _Exchange(list(parts16) + [landed[t] for t in given], shapes,
                     len(CHIP_FLIPS) * sum(len(u) for u in units), start, finish,
                     aliases={n_t + i: t for i, t in enumerate(given)})


def _ex_join(units, owners):
    def each(ins, outs, send, recv, fn):
        x, y, c = _place()
        k = 0
        for t, own in enumerate(owners):
            for j, o in enumerate(own):
                def half(cc, to, u=outs[t].at[j], k=k):
                    return _remote(_half_rows(u, cc), _half_rows(u, cc), send, recv, k, to)
                mine = functools.partial(half, c, (x, y, 1 - c))
                theirs = functools.partial(half, 1 - c, (x, y, c))
                pl.when(2 * x + y == o)(functools.partial(fn, mine, theirs))
                k += 1

    def start(*refs):
        each(*refs, lambda mine, theirs: mine().start())

    def finish(*refs):
        each(*refs, lambda mine, theirs: theirs().wait_recv())
        each(*refs, lambda mine, theirs: mine().wait_send())

    shapes = [jax.ShapeDtypeStruct(a.shape, F32) for a in units]
    return _Exchange(units, shapes, sum(len(o) for o in owners), start, finish,
                     aliases={t: t for t in range(len(units))})


def _ex_gather(slots):
    n_t = len(slots)
    n_fl = len(CHIP_FLIPS)

    def piece(ref, shard, half):
        return _half_rows(ref.at[shard], half)

    def first(outs, send, recv):
        x, y, c = _place()
        s = 2 * x + y
        return [_remote(piece(outs[t], s, c), piece(outs[t], s, c), send, recv, n_t * j + t, (x ^ fx, y ^ fy, c))
                for j, (fx, fy) in enumerate(CHIP_FLIPS) for t in range(n_t)]

    def start(ins, outs, send, recv):
        for cp in first(outs, send, recv):
            cp.start()

    def finish(ins, outs, send, recv):
        x, y, c = _place()
        passed = []
        for j, (fx, fy) in enumerate(CHIP_FLIPS):
            sj = 2 * (x ^ fx) + (y ^ fy)
            for t in range(n_t):
                k = n_t * j + t
                _remote(piece(outs[t], sj, c), piece(outs[t], sj, c), send, recv, k, (x, y, c)).wait_recv()
                cp = _remote(piece(outs[t], sj, c), piece(outs[t], sj, c), send, recv, n_t * n_fl + k, (x, y, 1 - c))
                cp.start()
                passed.append(cp)
        for j, (fx, fy) in enumerate(CHIP_FLIPS):
            sj = 2 * (x ^ fx) + (y ^ fy)
            for t in range(n_t):
                k = n_t * n_fl + n_t * j + t
                _remote(piece(outs[t], sj, 1 - c), piece(outs[t], sj, 1 - c), send, recv, k, (x, y, c)).wait_recv()
        for cp in first(outs, send, recv) + passed:
            cp.wait_send()

    shapes = [jax.ShapeDtypeStruct(a.shape, BF16) for a in slots]
    return _Exchange(slots, shapes, 2 * n_t * n_fl, start, finish, aliases={t: t for t in range(n_t)})


def _ex_gather_small(parts):
    def copies(ins, outs, send, recv):
        x, y, c = _place()
        me = 4 * x + 2 * y + c
        return [_remote(ins[0], outs[0].at[me], send, recv, mask - 1,
                        (x ^ (mask >> 2), y ^ ((mask >> 1) & 1), c ^ (mask & 1))) for mask in range(1, 8)]

    def start(*refs):
        for cp in copies(*refs):
            cp.start()

    def finish(ins, outs, send, recv):
        x, y, c = _place()
        me = 4 * x + 2 * y + c
        for mask in range(1, 8):
            _remote(ins[0], outs[0].at[me ^ mask], send, recv, mask - 1, (x, y, c)).wait_recv()
        for cp in copies(ins, outs, send, recv):
            cp.wait_send()

    return _Exchange([parts], [jax.ShapeDtypeStruct((8,) + parts.shape, F32)], 7, start, finish)


def _call(body, *, name, args=(), in_specs=(), out_specs=(), out_shape=(), grid=(), scratch_shapes=(),
          exchanges=()):
    n_in, n_out, n_scr = len(args), len(out_shape), len(scratch_shapes)
    ex_in, ex_out, ex_scr, spans, alias = [], [], [], [], {}
    for ex in exchanges:
        spans.append((len(ex_in), len(ex.inputs), len(ex_out), len(ex.out_shapes)))
        for i, o in ex.aliases.items():
            alias[n_in + len(ex_in) + i] = n_out + len(ex_out) + o
        ex_in += ex.inputs
        ex_out += ex.out_shapes
        ex_scr += [pltpu.SemaphoreType.DMA((ex.n_sems,)), pltpu.SemaphoreType.DMA((ex.n_sems,))]

    def full(*refs):
        ins, x_in = refs[:n_in], refs[n_in:n_in + len(ex_in)]
        outs = refs[n_in + len(ex_in):n_in + len(ex_in) + n_out]
        x_out = refs[n_in + len(ex_in) + n_out:n_in + len(ex_in) + n_out + len(ex_out)]
        scr = refs[len(refs) - n_scr - len(ex_scr):len(refs) - len(ex_scr)]
        sems = refs[len(refs) - len(ex_scr):]

        def run(which):
            for e, (ex, (i0, ni, o0, no)) in enumerate(zip(exchanges, spans)):
                getattr(ex, which)(x_in[i0:i0 + ni], x_out[o0:o0 + no], sems[2 * e], sems[2 * e + 1])

        if grid:
            ids = [pl.program_id(a) for a in range(len(grid))]
            is_first = functools.reduce(jnp.logical_and, [i == 0 for i in ids])
            is_last = functools.reduce(jnp.logical_and, [i == g - 1 for i, g in zip(ids, grid)])
            pl.when(is_first)(lambda: run("start"))
            body(*ins, *outs, *scr)
            pl.when(is_last)(lambda: run("finish"))
        else:
            run("start")
            if body is not None:
                body(*ins, *outs, *scr)
            run("finish")

    kw = dict(grid=grid) if grid else {}
    if grid:
        kw["compiler_params"] = _params(dimension_semantics=("arbitrary",) * len(grid))
    else:
        kw["compiler_params"] = _params()
    res = pl.pallas_call(
        full, name=name,
        in_specs=list(in_specs) + [HBM] * len(ex_in),
        out_specs=list(out_specs) + [HBM] * len(ex_out),
        out_shape=list(out_shape) + ex_out,
        scratch_shapes=list(scratch_shapes) + ex_scr,
        input_output_aliases=alias, **kw,
    )(*args, *ex_in)
    own = list(res[:n_out])
    per_ex = [list(res[n_out + o0:n_out + o0 + no]) for (_, _, o0, no) in spans]
    return own, per_ex


def _cast_own(w, place, name):
    rows, cols = w.shape
    tile = min(rows, 256)

    def body(place_ref, w_ref, o_ref):
        o_ref[...] = w_ref[...].astype(BF16)

    return pl.pallas_call(
        body, name=name,
        grid_spec=pltpu.PrefetchScalarGridSpec(
            num_scalar_prefetch=1, grid=(rows // tile,),
            in_specs=[pl.BlockSpec((tile, cols), lambda i, p: (i, 0))],
            out_specs=pl.BlockSpec((None, tile, cols), lambda i, p: (p[1], i, 0))),
        out_shape=jax.ShapeDtypeStruct((N_SHARDS, rows, cols), BF16),
        compiler_params=_params(dimension_semantics=("arbitrary",)),
    )(place, w)


def _cast_w_in(w, place):
    rows, cols = w.shape
    tile = rows

    def body(place_ref, w_ref, o_ref):
        o_ref[...] = w_ref[...].astype(BF16)

    return pl.pallas_call(
        body, name="cast_w_in",
        grid_spec=pltpu.PrefetchScalarGridSpec(
            num_scalar_prefetch=1, grid=(cols // COL_BLK, rows // tile),
            in_specs=[pl.BlockSpec((tile, COL_BLK), lambda b, i, p: (i, b))],
            out_specs=pl.BlockSpec((None, None, tile, COL_BLK), lambda b, i, p: (p[1], b, i, 0))),
        out_shape=jax.ShapeDtypeStruct((N_SHARDS, cols // COL_BLK, rows, COL_BLK), BF16),
        compiler_params=_params(dimension_semantics=("arbitrary", "arbitrary")),
    )(place, w)


def _rec_fwd(proj, lb_logits, rec_g, consts, exchanges):
    tri, low, sign = consts["tri"], consts["low"], consts["sign"]

    def body(q_ref, f_ref, i_ref, rg_ref, lb_ref, g_ref, w_ref, low_ref, sign_ref, y_ref, o_ref, stp_ref, st_ref):
        @pl.when(pl.program_id(1) == 0)
        def _():
            st_ref[...] = jnp.zeros_like(st_ref)
        lb = _lower_bound(lb_ref)
        st = st_ref[...]
        rows = lambda c: pl.ds(c * CHUNK, CHUNK)
        for c0 in range(0, REC_CHUNKS, REC_GROUP):
            group = range(c0, c0 + REC_GROUP)
            gated = [_gates(q_ref[rows(c), :], f_ref[rows(c), :], lb) for c in group]
            g2s = [_dot3(w_ref[...], g) * LOG2E for (_, _, _, _, g) in gated]
            xs = [[xl for xl, _, _, _ in _level_factors(g2, qs, k, sign_ref)]
                  for g2, (qs, _, _, k, _) in zip(g2s, gated)]
            a_s = []
            for x in xs:
                a = jnp.zeros((CHUNK, CHUNK), F32)
                for l, xl in enumerate(x):
                    a = a + _dot_nt(xl, xl) * low_ref[l]
                a_s.append(a.astype(BF16))
            vbs = [i_ref[rows(c), :].astype(BF16) for c in group]
            intra = [_dot(a, vb) for a, vb in zip(a_s, vbs)]
            kvs = [_dot_tn(vb, (k * jnp.exp2(g2[CHUNK - 1:CHUNK, :] - g2)).astype(BF16))
                   for vb, g2, (_, _, _, k, _) in zip(vbs, g2s, gated)]
            for i, c in enumerate(group):
                qs, _, _, k, _ = gated[i]
                g2 = g2s[i]
                stp_ref[c] = st
                v = i_ref[rows(c), :]
                rg = rg_ref[rows(c), :]
                o = (intra[i] + jnp.sum(qs * k, axis=-1, keepdims=True) * v
                     + _dot_nt((qs * jnp.exp2(g2)).astype(BF16), st.astype(BF16)))
                st = st * jnp.exp2(g2[CHUNK - 1:CHUNK, :]) + kvs[i]
                o_ref[rows(c), :] = o
                inv = lax.rsqrt(jnp.mean(o * o, axis=-1, keepdims=True) + EPS)
                y_ref[rows(c), :] = (o * inv * g_ref[...] * (rg * _sig(rg))).astype(BF16)
        st_ref[...] = st

    sec = lambda n: _proj_cols(HEAD, n, lambda h, b: (h, b), REC_ROWS)
    vec = lambda rows: pl.BlockSpec((rows, HEAD), lambda h, b: (0, h))
    full = lambda a: pl.BlockSpec(a.shape, lambda h, b: (0,) * a.ndim)
    return _call(
        body, name="rec_fwd", grid=(N_HEADS, N_REC_BLK),
        args=(proj, proj, proj, proj, lb_logits, rec_g, tri, low, sign),
        in_specs=[sec(2), sec(3), sec(4), sec(5), vec(2), vec(1), full(tri), full(low), full(sign)],
        out_specs=[pl.BlockSpec((REC_ROWS, HEAD), lambda h, b: (b, h)),
                   pl.BlockSpec((REC_ROWS, HEAD), lambda h, b: (b, h)),
                   pl.BlockSpec((None, REC_CHUNKS, HEAD, HEAD), lambda h, b: (h, b, 0, 0))],
        out_shape=[jax.ShapeDtypeStruct((SEQ, D_MODEL), BF16),
                   jax.ShapeDtypeStruct((SEQ, D_MODEL), F32),
                   jax.ShapeDtypeStruct((N_HEADS, SEQ // CHUNK, HEAD, HEAD), F32)],
        scratch_shapes=[pltpu.VMEM((HEAD, HEAD), F32)],
        exchanges=exchanges)


OUT_ROWS = 512


def _out_proj_loss(y_pool, y_rec, w_out_g, x, target, gf):
    def body(yp_ref, yr_ref, w_ref, x_ref, t_ref, gf_ref, dout_ref, doutb_ref, part_ref):
        @pl.when(pl.program_id(0) == 0)
        def _():
            part_ref[...] = jnp.zeros_like(part_ref)
        halves = [pl.ds(a * (OUT_ROWS // 2), OUT_ROWS // 2) for a in range(2)]
        outs = [x_ref[r, :] + _dot(yp_ref[r, :], w_ref[0:D_MODEL, :])
                + _dot(yr_ref[r, :], w_ref[D_MODEL:2 * D_MODEL, :]) for r in halves]
        gf_v = gf_ref[...]
        for r, out in zip(halves, outs):
            inv = lax.rsqrt(jnp.mean(out * out, axis=-1, keepdims=True) + EPS)
            diff = out * inv * gf_v - t_ref[r, :]
            dyf = diff * (1.0 / D_MODEL)
            a = dyf * gf_v
            dout = inv * a - out * (inv * inv * inv) * jnp.mean(a * out, axis=-1, keepdims=True)
            dout_ref[r, :] = dout
            doutb_ref[r, :] = dout.astype(BF16)
            part_ref[0:1, :] += jnp.sum(dyf * out * inv, axis=0, keepdims=True)
            part_ref[1:2, :] += jnp.sum(diff * diff, axis=0, keepdims=True)

    row = lambda n: pl.BlockSpec((OUT_ROWS, n), lambda i: (i, 0))
    return pl.pallas_call(
        body, name="out_proj_loss", grid=(SEQ // OUT_ROWS,),
        in_specs=[row(D_MODEL), row(D_MODEL), pl.BlockSpec((2 * D_MODEL, D_MODEL), lambda i: (0, 0)),
                  row(D_MODEL), row(D_MODEL), pl.BlockSpec((1, D_MODEL), lambda i: (0, 0))],
        out_specs=[row(D_MODEL), row(D_MODEL), pl.BlockSpec((8, D_MODEL), lambda i: (0, 0))],
        out_shape=[jax.ShapeDtypeStruct((SEQ, D_MODEL), F32),
                   jax.ShapeDtypeStruct((SEQ, D_MODEL), BF16),
                   jax.ShapeDtypeStruct((8, D_MODEL), F32)],
        compiler_params=_params(dimension_semantics=("arbitrary",)),
    )(y_pool, y_rec, w_out_g, x, target, gf)


def _grad_w_out(y_pool, y_rec, dout_b):
    blk = W_OUT_SHARD // 2
    per = D_MODEL // blk
    n = 2 * per

    def body(yp_ref, yr_ref, d_ref, p32_ref, p16_ref, send_ref, recv_ref, send_sems, recv_sems):
        j = pl.program_id(0)
        x, y, c = _place()

        def copy(u):
            return _remote(send_ref.at[u], recv_ref.at[u], send_sems, recv_sems, u, (x, y, 1 - c))

        pl.when(j == 0)(_other_core_barrier)
        for i in range(n):
            @pl.when(j == i)
            def _(i=i):
                res = _dot_tn((yp_ref if i < per else yr_ref)[...], d_ref[...])

                @pl.when(i % 2 == c)
                def _():
                    p32_ref[i // 2] = res

                @pl.when(i % 2 != c)
                def _():
                    send_ref[i // 2] = res
                    copy(i // 2).start()

        @pl.when(j == n - 1)
        def _():
            for u in range(N_SHARDS):
                copy(u).wait_recv()
                tot = p32_ref[u] + recv_ref[u]
                p32_ref[u] = tot
                p16_ref[u] = tot.astype(BF16)
            for u in range(N_SHARDS):
                copy(u).wait_send()

    whole = pl.BlockSpec((N_SHARDS, blk, D_MODEL), lambda j: (0, 0, 0))
    return pl.pallas_call(
        body, name="grad_w_out", grid=(n,),
        in_specs=[pl.BlockSpec((SEQ, blk), lambda j: (0, jnp.minimum(j, per - 1))),
                  pl.BlockSpec((SEQ, blk), lambda j: (0, jnp.maximum(j - per, 0))),
                  pl.BlockSpec((SEQ, D_MODEL), lambda j: (0, 0))],
        out_specs=[whole, whole],
        out_shape=[jax.ShapeDtypeStruct((N_SHARDS, blk, D_MODEL), F32),
                   jax.ShapeDtypeStruct((N_SHARDS, blk, D_MODEL), BF16)],
        scratch_shapes=[pltpu.VMEM((N_SHARDS, blk, D_MODEL), F32), pltpu.VMEM((N_SHARDS, blk, D_MODEL), F32),
                        pltpu.SemaphoreType.DMA((N_SHARDS,)), pltpu.SemaphoreType.DMA((N_SHARDS,))],
        compiler_params=_params(dimension_semantics=("arbitrary",), collective_id=0),
    )(y_pool, y_rec, dout_b)


def _pool_bwd(proj, dout_b, w_out_g, pw_g, pool_scale, exchanges):
    n = POOL_ROWS + POOL_HALO

    def body(u_ref, gate_ref, d_ref, wo_ref, pw_ref, sc_ref,
             dp_ref, dpw_ref, dsc_ref, dd_ref, ddw_ref):
        g = pl.program_id(0)
        dpw_ref[...] = jnp.zeros_like(dpw_ref)
        dsc_ref[...] = jnp.zeros_like(dsc_ref)

        def first(ii, _):
            chunks = [POOL_PAIR * ii + a for a in range(POOL_PAIR)]
            rs = [pl.ds(pl.multiple_of(i * POOL_ROWS, POOL_ROWS), POOL_ROWS) for i in chunks]
            diffs = [_pool_diff(u_ref, i, g) for i in chunks]
            dbs = [d.astype(BF16) for d, _ in diffs]
            mixed = [_dot(db, pw_ref[...]) for db in dbs]
            dys = [_dot_nt(d_ref[r, :], wo_ref[...]) for r in rs]
            sc = sc_ref[...]
            dmbs = []
            for r, m, dy in zip(rs, mixed, dys):
                gate = gate_ref[r, :]
                sg = _sig(gate)
                silu = gate * sg
                dp_ref[1, r, :] = (dy * m * sc * (sg * (1.0 + gate * (1.0 - sg)))).astype(BF16)
                dsc_ref[...] += jnp.sum(dy * silu * m, axis=0, keepdims=True)
                dmbs.append((dy * silu * sc).astype(BF16))
            for db, dmb in zip(dbs, dmbs):
                dpw_ref[...] += _dot_tn(db, dmb)
            dds = [_dot_nt(dmb, pw_ref[...]) for dmb in dmbs]
            for r, dd, (_, inv_count) in zip(rs, dds, diffs):
                dd_ref[r, :] = dd
                ddw_ref[r, :] = dd * inv_count
            return 0
        lax.fori_loop(0, SEQ // POOL_ROWS // POOL_PAIR, first, 0)

        def second(i, _):
            r0 = i * POOL_ROWS
            r = pl.ds(pl.multiple_of(r0, POOL_ROWS), POOL_ROWS)
            last = i == SEQ // POOL_ROWS - 1
            after = ddw_ref[pl.ds(pl.multiple_of(jnp.minimum(r0 + POOL_ROWS, SEQ - POOL_HALO), 8), POOL_HALO), :]
            after = jnp.where(last, 0.0, after)
            ext = jnp.concatenate([ddw_ref[r, :], after], axis=0)
            s = _window_sums(ext, g, lambda k: n - (1 << k))[:POOL_ROWS, :]
            dp_ref[0, r, :] = (s - dd_ref[r, :]).astype(BF16)
            return 0
        lax.fori_loop(0, SEQ // POOL_ROWS, second, 0)

    return _call(
        body, name="pool_bwd", grid=(N_GROUPS,),
        args=(proj, proj, dout_b, w_out_g, pw_g, pool_scale),
        in_specs=[_proj_cols(PG, 0, lambda g: (g, 0)), _proj_cols(PG, 1, lambda g: (g, 0)),
                  pl.BlockSpec((SEQ, D_MODEL), lambda g: (0, 0)),
                  pl.BlockSpec((PG, D_MODEL), lambda g: (g, 0)),
                  pl.BlockSpec((None, PG, PG), lambda g: (g, 0, 0)),
                  pl.BlockSpec((1, PG), lambda g: (0, g))],
        out_specs=[pl.BlockSpec((2, SEQ, PG), lambda g: (0, 0, g)),
                   pl.BlockSpec((None, PG, PG), lambda g: (g, 0, 0)),
                   pl.BlockSpec((1, PG), lambda g: (0, g))],
        out_shape=[jax.ShapeDtypeStruct((2, SEQ, D_MODEL), BF16),
                   jax.ShapeDtypeStruct((N_GROUPS, PG, PG), F32),
                   jax.ShapeDtypeStruct((1, D_MODEL), F32)],
        scratch_shapes=[pltpu.VMEM((SEQ, PG), F32), pltpu.VMEM((SEQ, PG), F32)],
        exchanges=exchanges)


HALF_HEADS = N_HEADS // 2
HALF_COLS = HALF_HEADS * HEAD


def _rec_bwd(proj, o_raw, st_prev, dout_b, w_out_g, lb_logits, rec_g, consts, h0, name, exchanges):
    def body(q_ref, f_ref, i_ref, rg_ref, o_ref, stp_ref, d_ref, wo_ref, lb_ref, g_ref,
             w_ref, lowt_ref, sym_ref, sign_ref, tri_ref,
             dr_ref, part_ref, dst_ref):
        @pl.when(pl.program_id(1) == 0)
        def _():
            dst_ref[...] = jnp.zeros_like(dst_ref)
            part_ref[...] = jnp.zeros_like(part_ref)
        tril = (lax.broadcasted_iota(jnp.int32, (CHUNK, CHUNK), 0)
                > lax.broadcasted_iota(jnp.int32, (CHUNK, CHUNK), 1))
        lb = _lower_bound(lb_ref)
        grec = g_ref[...]
        dst = dst_ref[...]
        acc_grec = jnp.zeros((1, HEAD), F32)
        acc_lb = jnp.zeros((1, HEAD), F32)
        rows = lambda c: pl.ds(c * CHUNK, CHUNK)
        for c0 in reversed(range(0, REC_CHUNKS, REC_GROUP_BWD)):
            group = list(reversed(range(c0, c0 + REC_GROUP_BWD)))
            dys = [_dot_nt(d_ref[rows(c), :], wo_ref[...]) for c in group]
            dos = []
            for c, dy in zip(group, dys):
                rg = rg_ref[rows(c), :]
                o = o_ref[rows(c), :]
                sg = _sig(rg)
                silu = rg * sg
                inv = lax.rsqrt(jnp.mean(o * o, axis=-1, keepdims=True) + EPS)
                recn = o * inv
                dr_ref[3, rows(c), :] = (dy * recn * grec * (sg * (1.0 + rg * (1.0 - sg)))).astype(BF16)
                acc_grec = acc_grec + jnp.sum(dy * silu * recn, axis=0, keepdims=True)
                drecn = dy * silu * grec
                dos.append(inv * drecn - o * (inv * inv * inv) * jnp.mean(drecn * o, axis=-1, keepdims=True))
            gated = [_gates(q_ref[rows(c), :], f_ref[rows(c), :], lb) for c in group]
            g2s = [_dot3(w_ref[...], g) * LOG2E for (_, _, _, _, g) in gated]
            levels = [_level_factors(g2, qs, k, sign_ref) for g2, (qs, _, _, k, _) in zip(g2s, gated)]
            a_ts = []
            for lev in levels:
                a_t = jnp.zeros((CHUNK, CHUNK), F32)
                for l, (xl, _, _, _) in enumerate(lev):
                    a_t = a_t + _dot_nt(xl, xl) * lowt_ref[l]
                a_ts.append(a_t.astype(BF16))
            dobs = [do.astype(BF16) for do in dos]
            vbs = [i_ref[rows(c), :].astype(BF16) for c in group]
            d_syms = [jnp.where(tril, _dot_nt(dob, vb), _dot_nt(vb, dob)) for dob, vb in zip(dobs, vbs)]
            dqs_is, dk_is = [], []
            for lev, d_sym in zip(levels, d_syms):
                dqs_i = jnp.zeros((CHUNK, HEAD), F32)
                both_i = jnp.zeros((CHUNK, HEAD), F32)
                for l, (xl, xlo, e, up) in enumerate(lev):
                    z = d_sym * sym_ref[l]
                    tmp = _dot(z.astype(BF16), jnp.concatenate([xl, xlo], axis=-1))
                    tmp = (tmp[:, :HEAD] + tmp[:, HEAD:]) * e
                    dqs_i = dqs_i + jnp.where(up, tmp, 0.0)
                    both_i = both_i + tmp
                dqs_is.append(dqs_i)
                dk_is.append(both_i - dqs_i)
            e_gs = [jnp.exp2(g2) for g2 in g2s]
            e_revs = [jnp.exp2(g2[CHUNK - 1:CHUNK, :] - g2) for g2 in g2s]
            e_lasts = [jnp.exp2(g2[CHUNK - 1:CHUNK, :]) for g2 in g2s]
            q_gs = [qs * e_g for (qs, _, _, _, _), e_g in zip(gated, e_gs)]
            kdecs = [k * e_rev for (_, _, _, k, _), e_rev in zip(gated, e_revs)]
            dv12 = [_dot(a_t, dob) + jnp.sum(qs * k, axis=-1, keepdims=True) * do
                    for a_t, dob, do, (qs, _, _, k, _) in zip(a_ts, dobs, dos, gated)]
            dq_gs = [_dot(dob, stp_ref[c].astype(BF16)) for c, dob in zip(group, dobs)]
            steps = [_dot_tn(dob, q_g.astype(BF16)) for dob, q_g in zip(dobs, q_gs)]
            dsts = []
            for e_last, step in zip(e_lasts, steps):
                dsts.append(dst)
                dst = dst * e_last + step
            dstbs = [d.astype(BF16) for d in dsts]
            dv3 = [_dot_nt(kdec.astype(BF16), dstb) for kdec, dstb in zip(kdecs, dstbs)]
            dkdecs = [_dot(vb, dstb) for vb, dstb in zip(vbs, dstbs)]
            dbig_gs, dg_lasts, dqss, dks = [], [], [], []
            for i, c in enumerate(group):
                qs, _, _, k, _ = gated[i]
                de_last = jnp.sum(stp_ref[c] * dsts[i], axis=0, keepdims=True)
                ddiag = jnp.sum(dos[i] * i_ref[rows(c), :], axis=-1, keepdims=True)
                dqss.append(dqs_is[i] + ddiag * k + dq_gs[i] * e_gs[i])
                dks.append(dk_is[i] + ddiag * qs + dkdecs[i] * e_revs[i])
                dg_rev = dkdecs[i] * kdecs[i]
                dg_lasts.append(jnp.sum(dg_rev, axis=0, keepdims=True) + de_last * e_lasts[i])
                dbig_gs.append(qs * dqs_is[i] - k * dk_is[i] + dq_gs[i] * q_gs[i] - dg_rev)
            dgs = [_dot3(tri_ref[...], dbig_g) + dg_last for dbig_g, dg_last in zip(dbig_gs, dg_lasts)]
            for i, c in enumerate(group):
                _, sf, f, _, _ = gated[i]
                q = q_ref[rows(c), :]
                df = dgs[i] / f - dks[i]
                dr_ref[1, rows(c), :] = (df * (1.0 - lb) * sf * (1.0 - sf)).astype(BF16)
                acc_lb = acc_lb + jnp.sum(df * (1.0 - sf), axis=0, keepdims=True)
                sq = _sig(q)
                dr_ref[0, rows(c), :] = (dqss[i] * (sq * (1.0 + q * (1.0 - sq)))).astype(BF16)
                dr_ref[2, rows(c), :] = (dv12[i] + dv3[i]).astype(BF16)
        dst_ref[...] = dst
        part_ref[0:1, :] += acc_grec
        part_ref[1:2, :] += acc_lb

    rev = lambda b: N_REC_BLK - 1 - b
    sec = lambda n: _proj_cols(HEAD, n, lambda h, b: (h0 + h, rev(b)), REC_ROWS)
    col_in = pl.BlockSpec((REC_ROWS, HEAD), lambda h, b: (rev(b), h0 + h))
    vec_in = lambda rows: pl.BlockSpec((rows, HEAD), lambda h, b: (0, h0 + h))
    full = lambda a: pl.BlockSpec(a.shape, lambda h, b: (0,) * a.ndim)
    return _call(
        body, name=name, grid=(HALF_HEADS, N_REC_BLK),
        args=(proj, proj, proj, proj, o_raw, st_prev, dout_b, w_out_g, lb_logits, rec_g,
              consts["tri"], consts["low_t"], consts["sym"], consts["sign"], consts["tri_t"]),
        in_specs=[sec(2), sec(3), sec(4), sec(5), col_in,
                  pl.BlockSpec((None, REC_CHUNKS, HEAD, HEAD), lambda h, b: (h0 + h, rev(b), 0, 0)),
                  pl.BlockSpec((REC_ROWS, D_MODEL), lambda h, b: (rev(b), 0)),
                  pl.BlockSpec((HEAD, D_MODEL), lambda h, b: (SEC_BLK + h0 + h, 0)),
                  vec_in(2), vec_in(1)] + [full(consts[n]) for n in ("tri", "low_t", "sym", "sign", "tri_t")],
        out_specs=[pl.BlockSpec((4, REC_ROWS, HEAD), lambda h, b: (0, rev(b), h)),
                   pl.BlockSpec((8, HEAD), lambda h, b: (0, h))],
        out_shape=[jax.ShapeDtypeStruct((4, SEQ, HALF_COLS), BF16),
                   jax.ShapeDtypeStruct((8, HALF_COLS), F32)],
        scratch_shapes=[pltpu.VMEM((HEAD, HEAD), F32)],
        exchanges=exchanges)


def _w_in_block(w_ref, j):
    per_shard = W_IN_SHARD // COL_BLK
    return w_ref[j // per_shard, j % per_shard]


def _grad_x(dproj, w_in_g, x, g1, dout, exchanges):
    rows = 256
    n_blk = len(dproj)

    def body(*refs):
        dp_refs = refs[:n_blk]
        w_ref, x_ref, g_ref, dout_ref, dx_ref, part_ref = refs[n_blk:]

        @pl.when(pl.program_id(0) == 0)
        def _():
            part_ref[...] = jnp.zeros_like(part_ref)
        dh = jnp.zeros((rows, D_MODEL), F32)
        for j in range(n_blk):
            dh = dh + _dot_nt(dp_refs[j][...], _w_in_block(w_ref, j))
        xv = x_ref[...]
        inv = lax.rsqrt(jnp.mean(xv * xv, axis=-1, keepdims=True) + EPS)
        a = dh * g_ref[...]
        dx_ref[...] = (dout_ref[...] + inv * a
                       - xv * (inv * inv * inv) * jnp.mean(a * xv, axis=-1, keepdims=True))
        part_ref[0:1, :] += jnp.sum(dh * xv * inv, axis=0, keepdims=True)

    row = lambda: pl.BlockSpec((rows, D_MODEL), lambda i: (i, 0))
    dp_spec = lambda sec, cb: pl.BlockSpec((None, rows, COL_BLK), lambda i: (sec, i, cb))
    return _call(
        body, name="grad_x", grid=(SEQ // rows,),
        args=tuple(a for a, _, _ in dproj) + (w_in_g, x, g1, dout),
        in_specs=[dp_spec(sec, cb) for _, sec, cb in dproj]
                 + [pl.BlockSpec(w_in_g.shape, lambda i: (0, 0, 0, 0)),
                    row(), pl.BlockSpec((1, D_MODEL), lambda i: (0, 0)), row()],
        out_specs=[row(), pl.BlockSpec((8, D_MODEL), lambda i: (0, 0))],
        out_shape=[jax.ShapeDtypeStruct((SEQ, D_MODEL), F32),
                   jax.ShapeDtypeStruct((8, D_MODEL), F32)],
        exchanges=exchanges)


def _grad_w_in(h, dp, blocks, name, collective_id):
    n_blk = len(blocks)
    half = D_MODEL // 2
    pick = lambda vals: (lambda j: functools.reduce(lambda acc, iv: jnp.where(j == iv[0], iv[1], acc),
                                                     list(enumerate(vals))[1:], vals[0]))
    sec_of = pick([sec for sec, _ in blocks])
    cb_of = pick([cb for _, cb in blocks])

    def body(h_ref, dp_ref, p32_ref, p16_ref, keep_ref, send_ref, recv_ref, send_sems, recv_sems):
        j = pl.program_id(0)
        x, y, c = _place()
        cols = lambda cc: pl.ds(pl.multiple_of(cc * half, half), half)

        def copy(i):
            return _remote(send_ref.at[i], recv_ref.at[i], send_sems, recv_sems, i, (x, y, 1 - c))

        pl.when(j == 0)(_other_core_barrier)
        for i in range(n_blk + 1):
            @pl.when(j == i)
            def _(i=i):
                if i < n_blk:
                    send_ref[i] = _dot_tn(h_ref[:, cols(1 - c)], dp_ref[...])
                    copy(i).start()
                    keep_ref[i] = _dot_tn(h_ref[:, cols(c)], dp_ref[...])
                if i > 0:
                    copy(i - 1).wait_recv()
                    tot = keep_ref[i - 1] + recv_ref[i - 1]
                    p32_ref[...] = tot
                    p16_ref[...] = tot.astype(BF16)

        @pl.when(j == n_blk)
        def _():
            for i in range(n_blk):
                copy(i).wait_send()

    lagged = pl.BlockSpec((None, half, COL_BLK), lambda j: (jnp.maximum(j - 1, 0), 0, 0))
    last = n_blk - 1
    return pl.pallas_call(
        body, name=name, grid=(n_blk + 1,),
        in_specs=[pl.BlockSpec((SEQ, D_MODEL), lambda j: (0, 0)),
                  pl.BlockSpec((None, SEQ, COL_BLK),
                               lambda j: (sec_of(jnp.minimum(j, last)), 0, cb_of(jnp.minimum(j, last))))],
        out_specs=[lagged, lagged],
        out_shape=[jax.ShapeDtypeStruct((n_blk, half, COL_BLK), F32),
                   jax.ShapeDtypeStruct((n_blk, half, COL_BLK), BF16)],
        scratch_shapes=[pltpu.VMEM((n_blk, half, COL_BLK), F32)] * 3
                       + [pltpu.SemaphoreType.DMA((n_blk,)), pltpu.SemaphoreType.DMA((n_blk,))],
        compiler_params=_params(dimension_semantics=("arbitrary",), collective_id=collective_id),
    )(h, dp)


def _add_units(grad, recv, place, tile, name):
    n, rows, cols = grad.shape
    per_half = rows // 2 // tile

    def body(place_ref, g_ref, r_ref, o32_ref, o16_ref):
        v = g_ref[...] + r_ref[...]
        o32_ref[...] = v
        o16_ref[...] = v.astype(BF16)

    blk = lambda f: pl.BlockSpec((None, tile, cols), f)
    out = lambda s, i, p: (s, i, 0)
    return pl.pallas_call(
        body, name=name,
        grid_spec=pltpu.PrefetchScalarGridSpec(
            num_scalar_prefetch=1, grid=(n, per_half),
            in_specs=[blk(lambda s, i, p: (s, p[0] * per_half + i, 0)), blk(out)],
            out_specs=[blk(out), blk(out)]),
        out_shape=[jax.ShapeDtypeStruct(recv.shape, F32), jax.ShapeDtypeStruct(recv.shape, BF16)],
        compiler_params=_params(dimension_semantics=("arbitrary", "arbitrary")),
    )(place, grad, recv)


def _sum_units(part32, recv16, place, owners, tile, name):
    n, half, cols = part32.shape
    per_half = half // tile
    table = np.array([[sum(o == chip for o in owners)] + sorted(range(n), key=lambda j: (owners[j] != chip, j))
                      for chip in range(N_SHARDS)], np.int32)
    sched = jnp.concatenate([place[:1], jnp.asarray(table)[place[1]]])

    def block(k, i, p):
        live = k < p[1]
        unit = p[2 + jnp.minimum(k, jnp.maximum(p[1] - 1, 0))]
        return unit, jnp.where(live, i, per_half - 1)

    def body(sched_ref, p_ref, r_ref, o_ref):
        @pl.when(pl.program_id(0) < sched_ref[1])
        def _():
            acc = p_ref[...]
            for j in range(len(CHIP_FLIPS)):
                acc = acc + r_ref[j].astype(F32)
            o_ref[...] = acc

    return pl.pallas_call(
        body, name=name,
        grid_spec=pltpu.PrefetchScalarGridSpec(
            num_scalar_prefetch=1, grid=(n, per_half),
            in_specs=[pl.BlockSpec((None, tile, cols), lambda k, i, p: (*block(k, i, p), 0)),
                      pl.BlockSpec((None, len(CHIP_FLIPS), tile, cols),
                                   lambda k, i, p: (block(k, i, p)[0], 0, block(k, i, p)[1], 0))],
            out_specs=pl.BlockSpec((None, tile, cols),
                                   lambda k, i, p: (block(k, i, p)[0], p[0] * per_half + block(k, i, p)[1], 0))),
        out_shape=jax.ShapeDtypeStruct((n, 2 * half, cols), F32),
        compiler_params=_params(dimension_semantics=("arbitrary", "arbitrary")),
    )(sched, part32, recv16)


def _adamw_math(w, g, m, v):
    m = ADAM_B1 * m + (1.0 - ADAM_B1) * g
    v = ADAM_B2 * v + (1.0 - ADAM_B2) * (g * g)
    m_hat = m / (1.0 - ADAM_B1 ** ADAM_STEP)
    v_hat = v / (1.0 - ADAM_B2 ** ADAM_STEP)
    delta = -ADAM_LR * (m_hat / (jnp.sqrt(v_hat) + ADAM_EPS) + ADAM_WD * w)
    return delta, m, v


def _adamw_units(w, m, v, grads, pick, name):
    rows, cols = w.shape
    bc = grads[0].shape[-1]
    tile = min(rows, 256)
    n_g = len(grads)

    def body(pick_ref, w_ref, m_ref, v_ref, *refs):
        g_refs, (g_out, d_ref, nm_ref, nv_ref) = refs[:n_g], refs[n_g:]
        p = pl.program_id(0)
        for a in range(n_g):
            @pl.when(pick_ref[0, p] == a)
            def _(a=a):
                g = g_refs[a][...]
                g_out[...] = g
                d_ref[...], nm_ref[...], nv_ref[...] = _adamw_math(w_ref[...], g, m_ref[...], v_ref[...])

    blk = pl.BlockSpec((tile, bc), lambda p, i, pick: (i, p))

    def g_spec(a):
        return pl.BlockSpec((None, tile, bc),
                            lambda p, i, pick: (jnp.where(pick[0, p] == a, pick[1, p], 0),
                                                jnp.where(pick[0, p] == a, i, 0), 0))

    return pl.pallas_call(
        body, name=name,
        grid_spec=pltpu.PrefetchScalarGridSpec(
            num_scalar_prefetch=1, grid=(cols // bc, rows // tile),
            in_specs=[blk] * 3 + [g_spec(a) for a in range(n_g)],
            out_specs=[blk] * 4),
        out_shape=[jax.ShapeDtypeStruct(w.shape, F32)] * 4,
        compiler_params=_params(dimension_semantics=("arbitrary", "arbitrary")),
    )(pick, w, m, v, *grads)


ROW_NORM1, ROW_SCALE, ROW_LB, ROW_REC, ROW_FINAL, ROW_LOSS = 0, 1, 2, 4, 5, 6


SMALL_ROWS = (ROW_NORM1, ROW_SCALE, ROW_LB, ROW_REC, ROW_FINAL)


def _small_update(parts, gathered, params):
    n_p = len(params)

    def body(own_ref, p_ref, *refs):
        ins, loss_ref, outs = refs[:3 * n_p], refs[3 * n_p], refs[3 * n_p + 1:]
        x, y, c = _place()
        me = 4 * x + 2 * y + c
        slot = lambda d: jnp.where(me == d, own_ref[...], p_ref[d])
        tot = slot(0)
        for d in range(1, 8):
            tot = tot + slot(d)
        for i, r in enumerate(SMALL_ROWS):
            w = ins[3 * i][...]
            g = tot[r:r + 1, :]
            if r == ROW_LB:
                mx = jnp.maximum(w[0:1, :], w[1:2, :])
                e0 = jnp.exp(w[0:1, :] - mx)
                e1 = jnp.exp(w[1:2, :] - mx)
                lb = e0 / (e0 + e1)
                g = g * lb * (1.0 - lb)
                g = jnp.concatenate([g, -g], axis=0)
            outs[4 * i][...] = g
            outs[4 * i + 1][...], outs[4 * i + 2][...], outs[4 * i + 3][...] = _adamw_math(
                w, g, ins[3 * i + 1][...], ins[3 * i + 2][...])
        loss_ref[...] = (0.5 / D_MODEL) * jnp.sum(tot[ROW_LOSS:ROW_LOSS + 1, :], axis=-1, keepdims=True)

    flat = [a for wmv in params for a in wmv]
    return pl.pallas_call(
        body, name="small_update",
        out_shape=[jax.ShapeDtypeStruct((1, 1), F32)]
                  + [jax.ShapeDtypeStruct(w.shape, F32) for w, _, _ in params for _ in range(4)],
        compiler_params=_params(),
    )(parts, gathered, *flat)


SHARD_OWNERS = tuple(range(N_SHARDS))
BLOCKS_POOL = (0, 1, 2, 3)
BLOCKS_A = (4, 6, 8, 10)
BLOCKS_B = (5, 7, 9, 11)
BLOCK_GROUPS = (BLOCKS_POOL, BLOCKS_A, BLOCKS_B)


def _block_owners(blocks):
    return tuple(j // (W_IN_SHARD // COL_BLK) for j in blocks)


def kernel(x, norm1_g, w_in, pool_w, pool_scale, lb_logits, rec_norm_g, w_out, final_norm_g, loss_target, m_norm1_g, m_w_in, m_pool_w, m_pool_scale, m_lb_logits, m_rec_norm_g, m_w_out, m_final_norm_g, v_norm1_g, v_w_in, v_pool_w, v_pool_scale, v_lb_logits, v_rec_norm_g, v_w_out, v_final_norm_g):
    xi, yi, ci = _place()
    chip = 2 * xi + yi
    place = jnp.stack([ci, chip]).astype(jnp.int32)
    pw_rows = N_GROUPS * PW_SHARD
    flat_pw = lambda a: a.reshape(pw_rows, PG)
    x2, target, gf = x[0], loss_target[0], final_norm_g.reshape(1, D_MODEL)
    consts = {n: jnp.asarray(a, BF16 if n.startswith("tri") else F32) for n, a in _chunk_constants().items()}

    proj, h, w_in_g = _in_proj(x2, norm1_g, _cast_w_in(w_in[0], place), place)
    (y_rec, o_raw, st_prev), ((w_out_g, pw_g),) = _rec_fwd(
        proj, lb_logits, rec_norm_g, consts,
        [_ex_gather([_cast_own(w_out[0], place, "cast_w_out"), _cast_own(flat_pw(pool_w), place, "cast_pool_w")])])
    w_out_g = w_out_g.reshape(2 * D_MODEL, D_MODEL)
    pw_full = pw_g.reshape(N_SHARDS, N_GROUPS, PW_SHARD, PG).transpose(1, 0, 2, 3).reshape(N_GROUPS, PG, PG)
    y_pool = _pool_fwd(proj, pw_full, pool_scale)
    dout, dout_b, part_out = _out_proj_loss(y_pool, y_rec, w_out_g, x2, target, gf)

    p_out32, p_out16 = _grad_w_out(y_pool, y_rec, dout_b)
    (dpool, gpw, dscale), ((rb_out,),) = _pool_bwd(proj, dout_b, w_out_g, pw_full, pool_scale,
                                                   [_ex_send([p_out16], [SHARD_OWNERS])])
    g_out = _sum_units(p_out32, rb_out, place, SHARD_OWNERS, 256, "sum_w_out")
    gpw = gpw.reshape(N_GROUPS, N_SHARDS, PW_SHARD, PG).transpose(1, 0, 2, 3).reshape(N_SHARDS, pw_rows, PG)
    p_inp32, p_inp16 = _grad_w_in(h, dpool, [(0, 0), (0, 1), (1, 0), (1, 1)], "grad_w_in_pool", 1)

    pool_owners, a_owners, b_owners = (_block_owners(b) for b in BLOCK_GROUPS)
    rec_args = (proj, o_raw, st_prev, dout_b, w_out_g, lb_logits, rec_norm_g, consts)
    (drec_a, part_a), ((rb_inp,), (ra_pw,)) = _rec_bwd(
        *rec_args, 0, "rec_bwd_a", [_ex_send([p_inp16], [pool_owners], units=[(0, 1)]), _ex_swap([gpw])])
    p_pw32, p_pw16 = _add_units(gpw, ra_pw, place, 128, "add_pool_w")
    rec_blocks = [(n, 0) for n in range(4)]
    p_ina32, p_ina16 = _grad_w_in(h, drec_a, rec_blocks, "grad_w_in_a", 2)

    (drec_b, part_b), ((rb_inp, rb_ina, rb_pw),) = _rec_bwd(
        *rec_args, HALF_HEADS, "rec_bwd_b",
        [_ex_send([p_inp16, p_ina16, p_pw16], [pool_owners, a_owners, SHARD_OWNERS],
                  units=[(2, 3), (0, 1, 2, 3), (0, 1, 2, 3)], landed=[rb_inp, None, None])])
    g_inp = _sum_units(p_inp32, rb_inp, place, pool_owners, 256, "sum_w_in_pool")
    g_ina = _sum_units(p_ina32, rb_ina, place, a_owners, 256, "sum_w_in_a")
    g_pw = _sum_units(p_pw32, rb_pw, place, SHARD_OWNERS, 128, "sum_pool_w")
    p_inb32, p_inb16 = _grad_w_in(h, drec_b, rec_blocks, "grad_w_in_b", 3)

    dproj = ([(dpool, 0, 0), (dpool, 0, 1), (dpool, 1, 0), (dpool, 1, 1)]
             + [(d, n, 0) for n in range(4) for d in (drec_a, drec_b)])
    (dx, part_x), ((rb_inb,),) = _grad_x(dproj, w_in_g, x2, norm1_g, dout, [_ex_send([p_inb16], [b_owners])])
    g_inb = _sum_units(p_inb32, rb_inb, place, b_owners, 256, "sum_w_in_b")
    zero = jnp.zeros((1, D_MODEL), F32)
    part_rec = jnp.concatenate([part_a, part_b], axis=1)
    parts = jnp.concatenate([part_x[0:1], dscale, part_rec[1:2], zero, part_rec[0:1], part_out[0:1],
                             part_out[1:2], zero], axis=0)
    _, ((g_out, g_pw, g_inp, g_ina, g_inb), (gathered,)) = _call(
        None, name="join_halves",
        exchanges=[_ex_join([g_out, g_pw, g_inp, g_ina, g_inb],
                            [SHARD_OWNERS, SHARD_OWNERS, pool_owners, a_owners, b_owners]),
                   _ex_gather_small(parts)])

    group_of = np.zeros((D_PROJ // COL_BLK,), np.int32)
    index_of = np.zeros((D_PROJ // COL_BLK,), np.int32)
    for gi, blocks in enumerate(BLOCK_GROUPS):
        for i, j in enumerate(blocks):
            group_of[j], index_of[j] = gi, i
    per_shard = W_IN_SHARD // COL_BLK
    pick_in = jnp.stack([lax.dynamic_slice(jnp.asarray(group_of), (per_shard * chip,), (per_shard,)),
                         lax.dynamic_slice(jnp.asarray(index_of), (per_shard * chip,), (per_shard,))])
    pick_own = jnp.stack([jnp.zeros((1,), jnp.int32), chip.reshape(1).astype(jnp.int32)])
    big = [_adamw_units(w_in[0], m_w_in[0], v_w_in[0], [g_inp, g_ina, g_inb], pick_in, "adamw_w_in"),
           _adamw_units(w_out[0], m_w_out[0], v_w_out[0], [g_out], pick_own, "adamw_w_out"),
           _adamw_units(flat_pw(pool_w), flat_pw(m_pool_w), flat_pw(v_pool_w), [g_pw], pick_own, "adamw_pool_w")]

    row = lambda a: a.reshape(1, D_MODEL)
    loss, *small = _small_update(parts, gathered, [
        (norm1_g, m_norm1_g, v_norm1_g), (pool_scale, m_pool_scale, v_pool_scale),
        (lb_logits, m_lb_logits, v_lb_logits), (rec_norm_g, m_rec_norm_g, v_rec_norm_g),
        (row(final_norm_g), row(m_final_norm_g), row(v_final_norm_g))])

    def leaves(k):
        norm1, scale, lb, rec, final = (small[4 * i + k] for i in range(len(SMALL_ROWS)))
        return (norm1, big[0][k][None], big[2][k].reshape(pool_w.shape), scale, lb, rec,
                big[1][k][None], final.reshape(D_MODEL))

    return (loss.reshape(()), dx[None], *leaves(0), *leaves(1), *leaves(2), *leaves(3))
```

```python
import functools

import numpy as np
import jax
import jax.numpy as jnp
from jax import lax
from jax.experimental import pallas as pl
from jax.experimental.pallas import tpu as pltpu

F32 = jnp.float32
BF16 = jnp.bfloat16

SEQ = 2048
D_MODEL = 1024
D_PROJ = 6144
N_SEC = 6
N_GROUPS = 4
PG = 256
N_HEADS = 8
HEAD = 128
CHUNK = 64
N_LEVELS = 6
N_SHARDS = 4
W_IN_SHARD = D_PROJ // N_SHARDS
W_OUT_SHARD = 2048 // N_SHARDS
PW_SHARD = PG // N_SHARDS
COL_BLK = 512
EPS = 1e-6

ADAM_LR = 0.001
ADAM_B1 = 0.9
ADAM_B2 = 0.999
ADAM_EPS = 1e-08
ADAM_WD = 0.01
ADAM_STEP = 10

V7X_VMEM_LIMIT = 56 * 1024 * 1024
MESH = pl.DeviceIdType.MESH


def _params(**kw):
    return pltpu.CompilerParams(vmem_limit_bytes=V7X_VMEM_LIMIT, **kw)


def _sig(x):
    return 1.0 / (1.0 + jnp.exp(-x))


def _dot(a, b):
    return jnp.dot(a, b, preferred_element_type=F32)


def _dot_nt(a, b):
    return lax.dot_general(a, b, (((1,), (1,)), ((), ())), preferred_element_type=F32)


def _dot_tn(a, b):
    return lax.dot_general(a, b, (((0,), (0,)), ((), ())), preferred_element_type=F32)


def _split3(a):
    p1 = a.astype(BF16)
    r1 = a - p1.astype(F32)
    p2 = r1.astype(BF16)
    p3 = (r1 - p2.astype(F32)).astype(BF16)
    return jnp.concatenate([p1, p2, p3], axis=-1)


def _dot3(w01, a):
    n = a.shape[-1]
    r = _dot(w01, _split3(a))
    return r[:, :n] + r[:, n:2 * n] + r[:, 2 * n:]


def _chunk_constants():
    j = np.arange(CHUNK)
    tt, ss = np.meshgrid(j, j, indexing="ij")
    x = tt ^ ss
    hb = np.full((CHUNK, CHUNK), -1, np.int32)
    for l in range(N_LEVELS):
        hb[x >= (1 << l)] = l
    sym = np.stack([(hb == l) for l in range(N_LEVELS)]).astype(np.float32)
    low = sym * (tt > ss)
    sign = np.stack([np.where((j >> l) & 1, 1.0, -1.0) for l in range(N_LEVELS)]).astype(np.float32)
    sign = np.ascontiguousarray(np.broadcast_to(sign[:, :, None], (N_LEVELS, CHUNK, HEAD)))
    tri = (ss <= tt).astype(np.float32)
    return dict(tri=tri, tri_t=np.ascontiguousarray(tri.T), low=low,
                low_t=np.ascontiguousarray(low.transpose(0, 2, 1)), sym=sym, sign=sign)


def _in_proj(x, g1, w_slots, place):
    n_col = D_PROJ // COL_BLK
    per_shard = W_IN_SHARD // COL_BLK
    rows = 1024
    half_rows = D_MODEL // 2
    quarter_rows = D_MODEL // 4
    FLIPS = (0, 2, 1, 3)
    ORDER = ([(0, p) for p in range(per_shard)] + [(m, p) for p in range(per_shard) for m in (1, 2)]
             + [(3, p) for p in range(per_shard)])

    def shard_at(m, chip):
        return chip ^ FLIPS[m]

    def pick(vals, t):
        return functools.reduce(lambda acc, iv: jnp.where(t == iv[0], iv[1], acc), list(enumerate(vals))[1:], vals[0])

    def body(place_ref, x_ref, g_ref, w_in_ref, proj_ref, h_ref, w_ref, wbuf, load_sems, send_sems, recv_sems):
        t = pl.program_id(0)
        x_, y_, c = _place()
        chip = 2 * x_ + y_
        me, other_core = (x_, y_, c), (x_, y_, 1 - c)
        x_nbr, y_nbr = (1 - x_, y_, c), (x_, 1 - y_, c)

        def rows_of(half, q=None):
            if q is None:
                return pl.ds(pl.multiple_of(half * half_rows, half_rows), half_rows)
            return pl.ds(pl.multiple_of(half * half_rows + q * quarter_rows, quarter_rows), quarter_rows)

        def block(m, p, r):
            return w_ref.at[shard_at(m, chip), p, r, :]

        def copy(k, ref, to):
            return _remote(ref, ref, send_sems, recv_sems, k, to)

        direct = lambda n, p, to: copy(3 * n + p, block(0, p, rows_of(c)), to)
        relay = lambda n, p, to: copy(6 + 3 * n + p, block(1 + n, p, rows_of(c, n)), to)
        arrived = lambda m, p: ([copy(3 * (m - 1) + p, block(m, p, rows_of(c)), me)] if m < 3 else
                                [copy(6 + 3 * n + p, block(3, p, rows_of(c, n)), me) for n in (0, 1)])
        passed_on = lambda m, p, half, to: copy(9 + 3 * m + p, block(m, p, rows_of(half)), to)

        def load(m, p, slot):
            return pltpu.make_async_copy(w_ref.at[shard_at(m, chip), p], wbuf.at[slot], load_sems.at[slot])

        def prepare(m, p):
            for cp in arrived(m, p):
                cp.wait_recv()
            passed_on(m, p, c, other_core).start()
            if m < 3:
                relay(m - 1, p, y_nbr if m == 1 else x_nbr).start()

        @pl.when(t == 0)
        def _():
            for p in range(per_shard):
                direct(0, p, x_nbr).start()
                direct(1, p, y_nbr).start()
            for p in range(per_shard):
                load(0, p, p).start()

            def norm(i, _):
                r = pl.ds(pl.multiple_of(i * rows, rows), rows)
                xv = x_ref[r, :]
                inv = lax.rsqrt(jnp.mean(xv * xv, axis=-1, keepdims=True) + EPS)
                h_ref[r, :] = (xv * inv * g_ref[...]).astype(BF16)
                return 0
            lax.fori_loop(0, SEQ // rows, norm, 0)

        for step, (m, p) in enumerate(ORDER):
            @pl.when(t == step)
            def _(step=step, m=m, p=p):
                slot = step % per_shard
                if m > 0:
                    passed_on(m, p, 1 - c, me).wait_recv()
                    load(m, p, slot).start()
                if step + 1 < n_col and ORDER[step + 1][0] > 0:
                    prepare(*ORDER[step + 1])
                load(m, p, slot).wait()

                def mm(i, _):
                    r = pl.ds(pl.multiple_of(i * rows, rows), rows)
                    proj_ref[r, :] = _dot(h_ref[r, :], wbuf[slot])
                    return 0
                lax.fori_loop(0, SEQ // rows, mm, 0)

        @pl.when(t == n_col - 1)
        def _():
            for p in range(per_shard):
                sent = [direct(0, p, x_nbr), direct(1, p, y_nbr), relay(0, p, y_nbr), relay(1, p, x_nbr)]
                for cp in sent + [passed_on(m, p, c, other_core) for m in (1, 2, 3)]:
                    cp.wait_send()

    return pl.pallas_call(
        body, name="in_proj",
        grid_spec=pltpu.PrefetchScalarGridSpec(
            num_scalar_prefetch=1, grid=(n_col,),
            in_specs=[pl.BlockSpec((SEQ, D_MODEL), lambda t, p: (0, 0)),
                      pl.BlockSpec((1, D_MODEL), lambda t, p: (0, 0)),
                      pl.BlockSpec(memory_space=pl.ANY)],
            out_specs=[pl.BlockSpec((None, SEQ, COL_BLK),
                                    lambda t, p: (per_shard * (p[1] ^ pick([FLIPS[m] for m, _ in ORDER], t))
                                                  + pick([b for _, b in ORDER], t), 0, 0)),
                       pl.BlockSpec((SEQ, D_MODEL), lambda t, p: (0, 0)),
                       pl.BlockSpec(memory_space=pl.ANY)],
            scratch_shapes=[pltpu.VMEM((per_shard, D_MODEL, COL_BLK), BF16),
                            pltpu.SemaphoreType.DMA((per_shard,)),
                            pltpu.SemaphoreType.DMA((21,)), pltpu.SemaphoreType.DMA((21,))]),
        out_shape=[jax.ShapeDtypeStruct((n_col, SEQ, COL_BLK), F32),
                   jax.ShapeDtypeStruct((SEQ, D_MODEL), BF16),
                   jax.ShapeDtypeStruct(w_slots.shape, BF16)],
        input_output_aliases={3: 2},
        compiler_params=_params(dimension_semantics=("arbitrary",)),
    )(place, x, g1, w_slots)


def _proj_cols(width, section, where, rows=SEQ):
    per_blk = COL_BLK // width

    def index(*grid):
        k, r = where(*grid)
        return section * (D_MODEL // COL_BLK) + k // per_blk, r, k % per_blk

    return pl.BlockSpec((None, rows, width), index)


POOL_ROWS = 256
POOL_HALO = 16
POOL_PAIR = 2


def _window_sums(ext, g, shift_of):
    s = ext
    for k in range(N_GROUPS):
        s = jnp.where(k <= g, s + pltpu.roll(s, shift_of(k), 0), s)
    return s


def _pool_diff(u_ref, i, g):
    n = POOL_ROWS + POOL_HALO
    r0 = i * POOL_ROWS
    cur = u_ref[pl.ds(pl.multiple_of(r0, POOL_ROWS), POOL_ROWS), :]
    before = u_ref[pl.ds(pl.multiple_of(jnp.maximum(r0 - POOL_HALO, 0), 8), POOL_HALO), :]
    before = jnp.where(i > 0, before, 0.0)
    ext = jnp.concatenate([before, cur], axis=0)
    s = _window_sums(ext, g, lambda k: 1 << k)[POOL_HALO:, :]
    t = r0 + lax.broadcasted_iota(jnp.int32, (POOL_ROWS, 1), 0)
    width = (2 << g).astype(F32)
    inv_count = 1.0 / jnp.minimum((t + 1).astype(F32), width)
    return s * inv_count - cur, inv_count


def _pool_fwd(proj, pw_g, pool_scale):
    def body(u_ref, gate_ref, pw_ref, sc_ref, y_ref):
        g = pl.program_id(0)

        def step(ii, _):
            chunks = [POOL_PAIR * ii + a for a in range(POOL_PAIR)]
            ds = [_pool_diff(u_ref, i, g)[0].astype(BF16) for i in chunks]
            mixed = [_dot(d, pw_ref[...]) for d in ds]
            for i, m in zip(chunks, mixed):
                r = pl.ds(pl.multiple_of(i * POOL_ROWS, POOL_ROWS), POOL_ROWS)
                gate = gate_ref[r, :]
                y_ref[r, :] = (m * sc_ref[...] * (gate * _sig(gate))).astype(BF16)
            return 0
        lax.fori_loop(0, SEQ // POOL_ROWS // POOL_PAIR, step, 0)

    return pl.pallas_call(
        body, name="pool_fwd", grid=(N_GROUPS,),
        in_specs=[_proj_cols(PG, 0, lambda g: (g, 0)), _proj_cols(PG, 1, lambda g: (g, 0)),
                  pl.BlockSpec((None, PG, PG), lambda g: (g, 0, 0)),
                  pl.BlockSpec((1, PG), lambda g: (0, g))],
        out_specs=pl.BlockSpec((SEQ, PG), lambda g: (0, g)),
        out_shape=jax.ShapeDtypeStruct((SEQ, D_MODEL), BF16),
        compiler_params=_params(dimension_semantics=("arbitrary",)),
    )(proj, proj, pw_g, pool_scale)


REC_ROWS = 1024
REC_CHUNKS = REC_ROWS // CHUNK
N_REC_BLK = SEQ // REC_ROWS
REC_GROUP = REC_CHUNKS
REC_GROUP_BWD = REC_CHUNKS
SEC_BLK = D_MODEL // HEAD


def _lower_bound(lb_ref):
    l0 = lb_ref[0:1, :]
    l1 = lb_ref[1:2, :]
    mx = jnp.maximum(l0, l1)
    e0 = jnp.exp(l0 - mx)
    e1 = jnp.exp(l1 - mx)
    return e0 / (e0 + e1)


def _gates(q, fl, lb):
    qs = q * _sig(q)
    sf = _sig(fl)
    f = lb + (1.0 - lb) * sf
    return qs, sf, f, 1.0 - f, jnp.log(f)


LOG2E = 1.4426950408889634


def _level_factors(g2, qs, k, sign_ref):
    t = lax.broadcasted_iota(jnp.int32, (CHUNK, HEAD), 0)
    row = lambda r, n: jnp.broadcast_to(g2[r:r + 1, :], (n, HEAD))
    out = []
    for l in range(N_LEVELS):
        m = 1 << l
        if l == 0:
            g_mid = jnp.where((t & 1) == 1, pltpu.roll(g2, 1, 0), g2)
        elif l == 1:
            low = (t & 7) < 4
            g_mid = jnp.concatenate([jnp.where(low[:8], row(8 * i + 1, 8), row(8 * i + 5, 8))
                                     for i in range(CHUNK // 8)], axis=0)
        else:
            g_mid = jnp.concatenate([row(b * 2 * m + m - 1, 2 * m) for b in range(CHUNK // (2 * m))], axis=0)
        sgn = sign_ref[l]
        up = sgn > 0.0
        e = jnp.exp2((g2 - g_mid) * sgn)
        x = jnp.where(up, qs, k) * e
        hi = x.astype(BF16)
        out.append((hi, (x - hi.astype(F32)).astype(BF16), e, up))
    return out


CHIP_FLIPS = ((1, 0), (0, 1), (1, 1))
HBM = pl.BlockSpec(memory_space=pl.ANY)


def _place():
    return lax.axis_index("x"), lax.axis_index("y"), lax.axis_index("c")


def _remote(src, dst, send_sems, recv_sems, k, to):
    return pltpu.make_async_remote_copy(src_ref=src, dst_ref=dst, send_sem=send_sems.at[k],
                                        recv_sem=recv_sems.at[k], device_id=to, device_id_type=MESH)


def _half_rows(ref, c):
    half = ref.shape[-2] // 2
    rows = pl.ds(pl.multiple_of(c * half, half), half)
    return ref.at[:, rows, :] if len(ref.shape) == 3 else ref.at[rows, :]


def _other_core_barrier():
    x, y, c = _place()
    sem = pltpu.get_barrier_semaphore()
    pl.semaphore_signal(sem, inc=1, device_id=(x, y, 1 - c), device_id_type=MESH)
    pl.semaphore_wait(sem, 1)


class _Exchange:
    def __init__(self, inputs, out_shapes, n_sems, start, finish, aliases=None):
        self.inputs, self.out_shapes, self.n_sems = list(inputs), list(out_shapes), n_sems
        self.start, self.finish, self.aliases = start, finish, dict(aliases or {})


def _ex_swap(grads):
    def copies(ins, outs, send, recv):
        x, y, c = _place()
        return [_remote(_half_rows(g, 1 - c), o, send, recv, t, (x, y, 1 - c))
                for t, (g, o) in enumerate(zip(ins, outs))]

    def start(*refs):
        for cp in copies(*refs):
            cp.start()

    def finish(*refs):
        cps = copies(*refs)
        for cp in cps:
            cp.wait_recv()
        for cp in cps:
            cp.wait_send()

    shapes = [jax.ShapeDtypeStruct((a.shape[0], a.shape[1] // 2, a.shape[2]), F32) for a in grads]
    return _Exchange(grads, shapes, len(grads), start, finish)


def _ex_send(parts16, owners, units=None, landed=None):
    n_t = len(parts16)
    units = units or [tuple(range(len(o))) for o in owners]
    landed = landed or [None] * n_t
    given = [t for t in range(n_t) if landed[t] is not None]

    def each(ins, outs, send, recv, to_sender, to_owner):
        x, y, c = _place()
        k = 0
        for t, own in enumerate(owners):
            for j in units[t]:
                for r, (fx, fy) in enumerate(CHIP_FLIPS):
                    tx, ty = x ^ fx, y ^ fy
                    cp = _remote(ins[t].at[j], outs[t].at[j, r], send, recv, k, (tx, ty, c))
                    if to_sender is not None:
                        pl.when(2 * tx + ty == own[j])(functools.partial(to_sender, cp))
                    if to_owner is not None:
                        pl.when(2 * x + y == own[j])(functools.partial(to_owner, cp))
                    k += 1

    def start(*refs):
        each(*refs, lambda cp: cp.start(), None)

    def finish(*refs):
        each(*refs, None, lambda cp: cp.wait_recv())
        each(*refs, lambda cp: cp.wait_send(), None)

    shapes = [jax.ShapeDtypeStruct((a.shape[0], len(CHIP_FLIPS)) + a.shape[1:], BF16) for a in parts16]
    return _Exchange(list(parts16) + [landed[t] for t in given], shapes,
                     len(CHIP_FLIPS) * sum(len(u) for u in units), start, finish,
                     aliases={n_t + i: t for i, t in enumerate(given)})


def _ex_join(units, owners):
    def each(ins, outs, send, recv, fn):
        x, y, c = _place()
        k = 0
        for t, own in enumerate(owners):
            for j, o in enumerate(own):
                def half(cc, to, u=outs[t].at[j], k=k):
                    return _remote(_half_rows(u, cc), _half_rows(u, cc), send, recv, k, to)
                mine = functools.partial(half, c, (x, y, 1 - c))
                theirs = functools.partial(half, 1 - c, (x, y, c))
                pl.when(2 * x + y == o)(functools.partial(fn, mine, theirs))
                k += 1

    def start(*refs):
        each(*refs, lambda mine, theirs: mine().start())

    def finish(*refs):
        each(*refs, lambda mine, theirs: theirs().wait_recv())
        each(*refs, lambda mine, theirs: mine().wait_send())

    shapes = [jax.ShapeDtypeStruct(a.shape, F32) for a in units]
    return _Exchange(units, shapes, sum(len(o) for o in owners), start, finish,
                     aliases={t: t for t in range(len(units))})


def _ex_gather(slots):
    n_t = len(slots)
    n_fl = len(CHIP_FLIPS)

    def piece(ref, shard, half):
        return _half_rows(ref.at[shard], half)

    def first(outs, send, recv):
        x, y, c = _place()
        s = 2 * x + y
        return [_remote(piece(outs[t], s, c), piece(outs[t], s, c), send, recv, n_t * j + t, (x ^ fx, y ^ fy, c))
                for j, (fx, fy) in enumerate(CHIP_FLIPS) for t in range(n_t)]

    def start(ins, outs, send, recv):
        for cp in first(outs, send, recv):
            cp.start()

    def finish(ins, outs, send, recv):
        x, y, c = _place()
        passed = []
        for j, (fx, fy) in enumerate(CHIP_FLIPS):
            sj = 2 * (x ^ fx) + (y ^ fy)
            for t in range(n_t):
                k = n_t * j + t
                _remote(piece(outs[t], sj, c), piece(outs[t], sj, c), send, recv, k, (x, y, c)).wait_recv()
                cp = _remote(piece(outs[t], sj, c), piece(outs[t], sj, c), send, recv, n_t * n_fl + k, (x, y, 1 - c))
                cp.start()
                passed.append(cp)
        for j, (fx, fy) in enumerate(CHIP_FLIPS):
            sj = 2 * (x ^ fx) + (y ^ fy)
            for t in range(n_t):
                k = n_t * n_fl + n_t * j + t
                _remote(piece(outs[t], sj, 1 - c), piece(outs[t], sj, 1 - c), send, recv, k, (x, y, c)).wait_recv()
        for cp in first(outs, send, recv) + passed:
            cp.wait_send()

    shapes = [jax.ShapeDtypeStruct(a.shape, BF16) for a in slots]
    return _Exchange(slots, shapes, 2 * n_t * n_fl, start, finish, aliases={t: t for t in range(n_t)})


def _ex_gather_small(parts):
    def copies(ins, outs, send, recv):
        x, y, c = _place()
        me = 4 * x + 2 * y + c
        return [_remote(ins[0], outs[0].at[me], send, recv, mask - 1,
                        (x ^ (mask >> 2), y ^ ((mask >> 1) & 1), c ^ (mask & 1))) for mask in range(1, 8)]

    def start(*refs):
        for cp in copies(*refs):
            cp.start()

    def finish(ins, outs, send, recv):
        x, y, c = _place()
        me = 4 * x + 2 * y + c
        for mask in range(1, 8):
            _remote(ins[0], outs[0].at[me ^ mask], send, recv, mask - 1, (x, y, c)).wait_recv()
        for cp in copies(ins, outs, send, recv):
            cp.wait_send()

    return _Exchange([parts], [jax.ShapeDtypeStruct((8,) + parts.shape, F32)], 7, start, finish)


def _call(body, *, name, args=(), in_specs=(), out_specs=(), out_shape=(), grid=(), scratch_shapes=(),
          exchanges=()):
    n_in, n_out, n_scr = len(args), len(out_shape), len(scratch_shapes)
    ex_in, ex_out, ex_scr, spans, alias = [], [], [], [], {}
    for ex in exchanges:
        spans.append((len(ex_in), len(ex.inputs), len(ex_out), len(ex.out_shapes)))
        for i, o in ex.aliases.items():
            alias[n_in + len(ex_in) + i] = n_out + len(ex_out) + o
        ex_in += ex.inputs
        ex_out += ex.out_shapes
        ex_scr += [pltpu.SemaphoreType.DMA((ex.n_sems,)), pltpu.SemaphoreType.DMA((ex.n_sems,))]

    def full(*refs):
        ins, x_in = refs[:n_in], refs[n_in:n_in + len(ex_in)]
        outs = refs[n_in + len(ex_in):n_in + len(ex_in) + n_out]
        x_out = refs[n_in + len(ex_in) + n_out:n_in + len(ex_in) + n_out + len(ex_out)]
        scr = refs[len(refs) - n_scr - len(ex_scr):len(refs) - len(ex_scr)]
        sems = refs[len(refs) - len(ex_scr):]

        def run(which):
            for e, (ex, (i0, ni, o0, no)) in enumerate(zip(exchanges, spans)):
                getattr(ex, which)(x_in[i0:i0 + ni], x_out[o0:o0 + no], sems[2 * e], sems[2 * e + 1])

        if grid:
            ids = [pl.program_id(a) for a in range(len(grid))]
            is_first = functools.reduce(jnp.logical_and, [i == 0 for i in ids])
            is_last = functools.reduce(jnp.logical_and, [i == g - 1 for i, g in zip(ids, grid)])
            pl.when(is_first)(lambda: run("start"))
            body(*ins, *outs, *scr)
            pl.when(is_last)(lambda: run("finish"))
        else:
            run("start")
            if body is not None:
                body(*ins, *outs, *scr)
            run("finish")

    kw = dict(grid=grid) if grid else {}
    if grid:
        kw["compiler_params"] = _params(dimension_semantics=("arbitrary",) * len(grid))
    else:
        kw["compiler_params"] = _params()
    res = pl.pallas_call(
        full, name=name,
        in_specs=list(in_specs) + [HBM] * len(ex_in),
        out_specs=list(out_specs) + [HBM] * len(ex_out),
        out_shape=list(out_shape) + ex_out,
        scratch_shapes=list(scratch_shapes) + ex_scr,
        input_output_aliases=alias, **kw,
    )(*args, *ex_in)
    own = list(res[:n_out])
    per_ex = [list(res[n_out + o0:n_out + o0 + no]) for (_, _, o0, no) in spans]
    return own, per_ex


def _cast_own(w, place, name):
    rows, cols = w.shape
    tile = min(rows, 256)

    def body(place_ref, w_ref, o_ref):
        o_ref[...] = w_ref[...].astype(BF16)

    return pl.pallas_call(
        body, name=name,
        grid_spec=pltpu.PrefetchScalarGridSpec(
            num_scalar_prefetch=1, grid=(rows // tile,),
            in_specs=[pl.BlockSpec((tile, cols), lambda i, p: (i, 0))],
            out_specs=pl.BlockSpec((None, tile, cols), lambda i, p: (p[1], i, 0))),
        out_shape=jax.ShapeDtypeStruct((N_SHARDS, rows, cols), BF16),
        compiler_params=_params(dimension_semantics=("arbitrary",)),
    )(place, w)


def _cast_w_in(w, place):
    rows, cols = w.shape
    tile = rows

    def body(place_ref, w_ref, o_ref):
        o_ref[...] = w_ref[...].astype(BF16)

    return pl.pallas_call(
        body, name="cast_w_in",
        grid_spec=pltpu.PrefetchScalarGridSpec(
            num_scalar_prefetch=1, grid=(cols // COL_BLK, rows // tile),
            in_specs=[pl.BlockSpec((tile, COL_BLK), lambda b, i, p: (i, b))],
            out_specs=pl.BlockSpec((None, None, tile, COL_BLK), lambda b, i, p: (p[1], b, i, 0))),
        out_shape=jax.ShapeDtypeStruct((N_SHARDS, cols // COL_BLK, rows, COL_BLK), BF16),
        compiler_params=_params(dimension_semantics=("arbitrary", "arbitrary")),
    )(place, w)


def _rec_fwd(proj, lb_logits, rec_g, consts, exchanges):
    tri, low, sign = consts["tri"], consts["low"], consts["sign"]

    def body(q_ref, f_ref, i_ref, rg_ref, lb_ref, g_ref, w_ref, low_ref, sign_ref, y_ref, o_ref, stp_ref, st_ref):
        @pl.when(pl.program_id(1) == 0)
        def _():
            st_ref[...] = jnp.zeros_like(st_ref)
        lb = _lower_bound(lb_ref)
        st = st_ref[...]
        rows = lambda c: pl.ds(c * CHUNK, CHUNK)
        for c0 in range(0, REC_CHUNKS, REC_GROUP):
            group = range(c0, c0 + REC_GROUP)
            gated = [_gates(q_ref[rows(c), :], f_ref[rows(c), :], lb) for c in group]
            g2s = [_dot3(w_ref[...], g) * LOG2E for (_, _, _, _, g) in gated]
            xs = [[xl for xl, _, _, _ in _level_factors(g2, qs, k, sign_ref)]
                  for g2, (qs, _, _, k, _) in zip(g2s, gated)]
            a_s = []
            for x in xs:
                a = jnp.zeros((CHUNK, CHUNK), F32)
                for l, xl in enumerate(x):
                    a = a + _dot_nt(xl, xl) * low_ref[l]
                a_s.append(a.astype(BF16))
            vbs = [i_ref[rows(c), :].astype(BF16) for c in group]
            intra = [_dot(a, vb) for a, vb in zip(a_s, vbs)]
            kvs = [_dot_tn(vb, (k * jnp.exp2(g2[CHUNK - 1:CHUNK, :] - g2)).astype(BF16))
                   for vb, g2, (_, _, _, k, _) in zip(vbs, g2s, gated)]
            for i, c in enumerate(group):
                qs, _, _, k, _ = gated[i]
                g2 = g2s[i]
                stp_ref[c] = st
                v = i_ref[rows(c), :]
                rg = rg_ref[rows(c), :]
                o = (intra[i] + jnp.sum(qs * k, axis=-1, keepdims=True) * v
                     + _dot_nt((qs * jnp.exp2(g2)).astype(BF16), st.astype(BF16)))
                st = st * jnp.exp2(g2[CHUNK - 1:CHUNK, :]) + kvs[i]
                o_ref[rows(c), :] = o
                inv = lax.rsqrt(jnp.mean(o * o, axis=-1, keepdims=True) + EPS)
                y_ref[rows(c), :] = (o * inv * g_ref[...] * (rg * _sig(rg))).astype(BF16)
        st_ref[...] = st

    sec = lambda n: _proj_cols(HEAD, n, lambda h, b: (h, b), REC_ROWS)
    vec = lambda rows: pl.BlockSpec((rows, HEAD), lambda h, b: (0, h))
    full = lambda a: pl.BlockSpec(a.shape, lambda h, b: (0,) * a.ndim)
    return _call(
        body, name="rec_fwd", grid=(N_HEADS, N_REC_BLK),
        args=(proj, proj, proj, proj, lb_logits, rec_g, tri, low, sign),
        in_specs=[sec(2), sec(3), sec(4), sec(5), vec(2), vec(1), full(tri), full(low), full(sign)],
        out_specs=[pl.BlockSpec((REC_ROWS, HEAD), lambda h, b: (b, h)),
                   pl.BlockSpec((REC_ROWS, HEAD), lambda h, b: (b, h)),
                   pl.BlockSpec((None, REC_CHUNKS, HEAD, HEAD), lambda h, b: (h, b, 0, 0))],
        out_shape=[jax.ShapeDtypeStruct((SEQ, D_MODEL), BF16),
                   jax.ShapeDtypeStruct((SEQ, D_MODEL), F32),
                   jax.ShapeDtypeStruct((N_HEADS, SEQ // CHUNK, HEAD, HEAD), F32)],
        scratch_shapes=[pltpu.VMEM((HEAD, HEAD), F32)],
        exchanges=exchanges)


OUT_ROWS = 512


def _out_proj_loss(y_pool, y_rec, w_out_g, x, target, gf):
    def body(yp_ref, yr_ref, w_ref, x_ref, t_ref, gf_ref, dout_ref, doutb_ref, part_ref):
        @pl.when(pl.program_id(0) == 0)
        def _():
            part_ref[...] = jnp.zeros_like(part_ref)
        halves = [pl.ds(a * (OUT_ROWS // 2), OUT_ROWS // 2) for a in range(2)]
        outs = [x_ref[r, :] + _dot(yp_ref[r, :], w_ref[0:D_MODEL, :])
                + _dot(yr_ref[r, :], w_ref[D_MODEL:2 * D_MODEL, :]) for r in halves]
        gf_v = gf_ref[...]
        for r, out in zip(halves, outs):
            inv = lax.rsqrt(jnp.mean(out * out, axis=-1, keepdims=True) + EPS)
            diff = out * inv * gf_v - t_ref[r, :]
            dyf = diff * (1.0 / D_MODEL)
            a = dyf * gf_v
            dout = inv * a - out * (inv * inv * inv) * jnp.mean(a * out, axis=-1, keepdims=True)
            dout_ref[r, :] = dout
            doutb_ref[r, :] = dout.astype(BF16)
            part_ref[0:1, :] += jnp.sum(dyf * out * inv, axis=0, keepdims=True)
            part_ref[1:2, :] += jnp.sum(diff * diff, axis=0, keepdims=True)

    row = lambda n: pl.BlockSpec((OUT_ROWS, n), lambda i: (i, 0))
    return pl.pallas_call(
        body, name="out_proj_loss", grid=(SEQ // OUT_ROWS,),
        in_specs=[row(D_MODEL), row(D_MODEL), pl.BlockSpec((2 * D_MODEL, D_MODEL), lambda i: (0, 0)),
                  row(D_MODEL), row(D_MODEL), pl.BlockSpec((1, D_MODEL), lambda i: (0, 0))],
        out_specs=[row(D_MODEL), row(D_MODEL), pl.BlockSpec((8, D_MODEL), lambda i: (0, 0))],
        out_shape=[jax.ShapeDtypeStruct((SEQ, D_MODEL), F32),
                   jax.ShapeDtypeStruct((SEQ, D_MODEL), BF16),
                   jax.ShapeDtypeStruct((8, D_MODEL), F32)],
        compiler_params=_params(dimension_semantics=("arbitrary",)),
    )(y_pool, y_rec, w_out_g, x, target, gf)


def _grad_w_out(y_pool, y_rec, dout_b):
    blk = W_OUT_SHARD // 2
    per = D_MODEL // blk
    n = 2 * per

    def body(yp_ref, yr_ref, d_ref, p32_ref, p16_ref, send_ref, recv_ref, send_sems, recv_sems):
        j = pl.program_id(0)
        x, y, c = _place()

        def copy(u):
            return _remote(send_ref.at[u], recv_ref.at[u], send_sems, recv_sems, u, (x, y, 1 - c))

        pl.when(j == 0)(_other_core_barrier)
        for i in range(n):
            @pl.when(j == i)
            def _(i=i):
                res = _dot_tn((yp_ref if i < per else yr_ref)[...], d_ref[...])

                @pl.when(i % 2 == c)
                def _():
                    p32_ref[i // 2] = res

                @pl.when(i % 2 != c)
                def _():
                    send_ref[i // 2] = res
                    copy(i // 2).start()

        @pl.when(j == n - 1)
        def _():
            for u in range(N_SHARDS):
                copy(u).wait_recv()
                tot = p32_ref[u] + recv_ref[u]
                p32_ref[u] = tot
                p16_ref[u] = tot.astype(BF16)
            for u in range(N_SHARDS):
                copy(u).wait_send()

    whole = pl.BlockSpec((N_SHARDS, blk, D_MODEL), lambda j: (0, 0, 0))
    return pl.pallas_call(
        body, name="grad_w_out", grid=(n,),
        in_specs=[pl.BlockSpec((SEQ, blk), lambda j: (0, jnp.minimum(j, per - 1))),
                  pl.BlockSpec((SEQ, blk), lambda j: (0, jnp.maximum(j - per, 0))),
                  pl.BlockSpec((SEQ, D_MODEL), lambda j: (0, 0))],
        out_specs=[whole, whole],
        out_shape=[jax.ShapeDtypeStruct((N_SHARDS, blk, D_MODEL), F32),
                   jax.ShapeDtypeStruct((N_SHARDS, blk, D_MODEL), BF16)],
        scratch_shapes=[pltpu.VMEM((N_SHARDS, blk, D_MODEL), F32), pltpu.VMEM((N_SHARDS, blk, D_MODEL), F32),
                        pltpu.SemaphoreType.DMA((N_SHARDS,)), pltpu.SemaphoreType.DMA((N_SHARDS,))],
        compiler_params=_params(dimension_semantics=("arbitrary",), collective_id=0),
    )(y_pool, y_rec, dout_b)


def _pool_bwd(proj, dout_b, w_out_g, pw_g, pool_scale, exchanges):
    n = POOL_ROWS + POOL_HALO

    def body(u_ref, gate_ref, d_ref, wo_ref, pw_ref, sc_ref,
             dp_ref, dpw_ref, dsc_ref, dd_ref, ddw_ref):
        g = pl.program_id(0)
        dpw_ref[...] = jnp.zeros_like(dpw_ref)
        dsc_ref[...] = jnp.zeros_like(dsc_ref)

        def first(ii, _):
            chunks = [POOL_PAIR * ii + a for a in range(POOL_PAIR)]
            rs = [pl.ds(pl.multiple_of(i * POOL_ROWS, POOL_ROWS), POOL_ROWS) for i in chunks]
            diffs = [_pool_diff(u_ref, i, g) for i in chunks]
            dbs = [d.astype(BF16) for d, _ in diffs]
            mixed = [_dot(db, pw_ref[...]) for db in dbs]
            dys = [_dot_nt(d_ref[r, :], wo_ref[...]) for r in rs]
            sc = sc_ref[...]
            dmbs = []
            for r, m, dy in zip(rs, mixed, dys):
                gate = gate_ref[r, :]
                sg = _sig(gate)
                silu = gate * sg
                dp_ref[1, r, :] = (dy * m * sc * (sg * (1.0 + gate * (1.0 - sg)))).astype(BF16)
                dsc_ref[...] += jnp.sum(dy * silu * m, axis=0, keepdims=True)
                dmbs.append((dy * silu * sc).astype(BF16))
            for db, dmb in zip(dbs, dmbs):
                dpw_ref[...] += _dot_tn(db, dmb)
            dds = [_dot_nt(dmb, pw_ref[...]) for dmb in dmbs]
            for r, dd, (_, inv_count) in zip(rs, dds, diffs):
                dd_ref[r, :] = dd
                ddw_ref[r, :] = dd * inv_count
            return 0
        lax.fori_loop(0, SEQ // POOL_ROWS // POOL_PAIR, first, 0)

        def second(i, _):
            r0 = i * POOL_ROWS
            r = pl.ds(pl.multiple_of(r0, POOL_ROWS), POOL_ROWS)
            last = i == SEQ // POOL_ROWS - 1
            after = ddw_ref[pl.ds(pl.multiple_of(jnp.minimum(r0 + POOL_ROWS, SEQ - POOL_HALO), 8), POOL_HALO), :]
            after = jnp.where(last, 0.0, after)
            ext = jnp.concatenate([ddw_ref[r, :], after], axis=0)
            s = _window_sums(ext, g, lambda k: n - (1 << k))[:POOL_ROWS, :]
            dp_ref[0, r, :] = (s - dd_ref[r, :]).astype(BF16)
            return 0
        lax.fori_loop(0, SEQ // POOL_ROWS, second, 0)

    return _call(
        body, name="pool_bwd", grid=(N_GROUPS,),
        args=(proj, proj, dout_b, w_out_g, pw_g, pool_scale),
        in_specs=[_proj_cols(PG, 0, lambda g: (g, 0)), _proj_cols(PG, 1, lambda g: (g, 0)),
                  pl.BlockSpec((SEQ, D_MODEL), lambda g: (0, 0)),
                  pl.BlockSpec((PG, D_MODEL), lambda g: (g, 0)),
                  pl.BlockSpec((None, PG, PG), lambda g: (g, 0, 0)),
                  pl.BlockSpec((1, PG), lambda g: (0, g))],
        out_specs=[pl.BlockSpec((2, SEQ, PG), lambda g: (0, 0, g)),
                   pl.BlockSpec((None, PG, PG), lambda g: (g, 0, 0)),
                   pl.BlockSpec((1, PG), lambda g: (0, g))],
        out_shape=[jax.ShapeDtypeStruct((2, SEQ, D_MODEL), BF16),
                   jax.ShapeDtypeStruct((N_GROUPS, PG, PG), F32),
                   jax.ShapeDtypeStruct((1, D_MODEL), F32)],
        scratch_shapes=[pltpu.VMEM((SEQ, PG), F32), pltpu.VMEM((SEQ, PG), F32)],
        exchanges=exchanges)


HALF_HEADS = N_HEADS // 2
HALF_COLS = HALF_HEADS * HEAD


def _rec_bwd(proj, o_raw, st_prev, dout_b, w_out_g, lb_logits, rec_g, consts, h0, name, exchanges,
             gate_of=None):
    n_sec = 4 if gate_of is None else 5

    def body(q_ref, f_ref, i_ref, rg_ref, o_ref, stp_ref, d_ref, wo_ref, lb_ref, g_ref,
             w_ref, lowt_ref, sym_ref, sign_ref, tri_ref, *rest):
        dr_ref, part_ref, dst_ref = rest[-3:]

        @pl.when(pl.program_id(1) == 0)
        def _():
            dst_ref[...] = jnp.zeros_like(dst_ref)
            part_ref[...] = jnp.zeros_like(part_ref)
        if gate_of is not None:
            rg2_ref, o2_ref, wo2_ref, g2_ref = rest[:4]
            for c in range(REC_CHUNKS):
                r = pl.ds(c * CHUNK, CHUNK)
                rg, o = rg2_ref[r, :], o2_ref[r, :]
                sg = _sig(rg)
                inv = lax.rsqrt(jnp.mean(o * o, axis=-1, keepdims=True) + EPS)
                dy = _dot_nt(d_ref[r, :], wo2_ref[...])
                dr_ref[4, r, :] = (dy * (o * inv) * g2_ref[...] * (sg * (1.0 + rg * (1.0 - sg)))).astype(BF16)
        tril = (lax.broadcasted_iota(jnp.int32, (CHUNK, CHUNK), 0)
                > lax.broadcasted_iota(jnp.int32, (CHUNK, CHUNK), 1))
        lb = _lower_bound(lb_ref)
        grec = g_ref[...]
        dst = dst_ref[...]
        acc_grec = jnp.zeros((1, HEAD), F32)
        acc_lb = jnp.zeros((1, HEAD), F32)
        rows = lambda c: pl.ds(c * CHUNK, CHUNK)
        for c0 in reversed(range(0, REC_CHUNKS, REC_GROUP_BWD)):
            group = list(reversed(range(c0, c0 + REC_GROUP_BWD)))
            dys = [_dot_nt(d_ref[rows(c), :], wo_ref[...]) for c in group]
            dos = []
            for c, dy in zip(group, dys):
                rg = rg_ref[rows(c), :]
                o = o_ref[rows(c), :]
                sg = _sig(rg)
                silu = rg * sg
                inv = lax.rsqrt(jnp.mean(o * o, axis=-1, keepdims=True) + EPS)
                recn = o * inv
                dr_ref[3, rows(c), :] = (dy * recn * grec * (sg * (1.0 + rg * (1.0 - sg)))).astype(BF16)
                acc_grec = acc_grec + jnp.sum(dy * silu * recn, axis=0, keepdims=True)
                drecn = dy * silu * grec
                dos.append(inv * drecn - o * (inv * inv * inv) * jnp.mean(drecn * o, axis=-1, keepdims=True))
            gated = [_gates(q_ref[rows(c), :], f_ref[rows(c), :], lb) for c in group]
            g2s = [_dot3(w_ref[...], g) * LOG2E for (_, _, _, _, g) in gated]
            levels = [_level_factors(g2, qs, k, sign_ref) for g2, (qs, _, _, k, _) in zip(g2s, gated)]
            a_ts = []
            for lev in levels:
                a_t = jnp.zeros((CHUNK, CHUNK), F32)
                for l, (xl, _, _, _) in enumerate(lev):
                    a_t = a_t + _dot_nt(xl, xl) * lowt_ref[l]
                a_ts.append(a_t.astype(BF16))
            dobs = [do.astype(BF16) for do in dos]
            vbs = [i_ref[rows(c), :].astype(BF16) for c in group]
            d_syms = [jnp.where(tril, _dot_nt(dob, vb), _dot_nt(vb, dob)) for dob, vb in zip(dobs, vbs)]
            dqs_is, dk_is = [], []
            for lev, d_sym in zip(levels, d_syms):
                dqs_i = jnp.zeros((CHUNK, HEAD), F32)
                both_i = jnp.zeros((CHUNK, HEAD), F32)
                for l, (xl, xlo, e, up) in enumerate(lev):
                    z = d_sym * sym_ref[l]
                    tmp = _dot(z.astype(BF16), jnp.concatenate([xl, xlo], axis=-1))
                    tmp = (tmp[:, :HEAD] + tmp[:, HEAD:]) * e
                    dqs_i = dqs_i + jnp.where(up, tmp, 0.0)
                    both_i = both_i + tmp
                dqs_is.append(dqs_i)
                dk_is.append(both_i - dqs_i)
            e_gs = [jnp.exp2(g2) for g2 in g2s]
            e_revs = [jnp.exp2(g2[CHUNK - 1:CHUNK, :] - g2) for g2 in g2s]
            e_lasts = [jnp.exp2(g2[CHUNK - 1:CHUNK, :]) for g2 in g2s]
            q_gs = [qs * e_g for (qs, _, _, _, _), e_g in zip(gated, e_gs)]
            kdecs = [k * e_rev for (_, _, _, k, _), e_rev in zip(gated, e_revs)]
            dv12 = [_dot(a_t, dob) + jnp.sum(qs * k, axis=-1, keepdims=True) * do
                    for a_t, dob, do, (qs, _, _, k, _) in zip(a_ts, dobs, dos, gated)]
            dq_gs = [_dot(dob, stp_ref[c].astype(BF16)) for c, dob in zip(group, dobs)]
            steps = [_dot_tn(dob, q_g.astype(BF16)) for dob, q_g in zip(dobs, q_gs)]
            dsts = []
            for e_last, step in zip(e_lasts, steps):
                dsts.append(dst)
                dst = dst * e_last + step
            dstbs = [d.astype(BF16) for d in dsts]
            dv3 = [_dot_nt(kdec.astype(BF16), dstb) for kdec, dstb in zip(kdecs, dstbs)]
            dkdecs = [_dot(vb, dstb) for vb, dstb in zip(vbs, dstbs)]
            dbig_gs, dg_lasts, dqss, dks = [], [], [], []
            for i, c in enumerate(group):
                qs, _, _, k, _ = gated[i]
                de_last = jnp.sum(stp_ref[c] * dsts[i], axis=0, keepdims=True)
                ddiag = jnp.sum(dos[i] * i_ref[rows(c), :], axis=-1, keepdims=True)
                dqss.append(dqs_is[i] + ddiag * k + dq_gs[i] * e_gs[i])
                dks.append(dk_is[i] + ddiag * qs + dkdecs[i] * e_revs[i])
                dg_rev = dkdecs[i] * kdecs[i]
                dg_lasts.append(jnp.sum(dg_rev, axis=0, keepdims=True) + de_last * e_lasts[i])
                dbig_gs.append(qs * dqs_is[i] - k * dk_is[i] + dq_gs[i] * q_gs[i] - dg_rev)
            dgs = [_dot3(tri_ref[...], dbig_g) + dg_last for dbig_g, dg_last in zip(dbig_gs, dg_lasts)]
            for i, c in enumerate(group):
                _, sf, f, _, _ = gated[i]
                q = q_ref[rows(c), :]
                df = dgs[i] / f - dks[i]
                dr_ref[1, rows(c), :] = (df * (1.0 - lb) * sf * (1.0 - sf)).astype(BF16)
                acc_lb = acc_lb + jnp.sum(df * (1.0 - sf), axis=0, keepdims=True)
                sq = _sig(q)
                dr_ref[0, rows(c), :] = (dqss[i] * (sq * (1.0 + q * (1.0 - sq)))).astype(BF16)
                dr_ref[2, rows(c), :] = (dv12[i] + dv3[i]).astype(BF16)
        dst_ref[...] = dst
        part_ref[0:1, :] += acc_grec
        part_ref[1:2, :] += acc_lb

    rev = lambda b: N_REC_BLK - 1 - b
    sec = lambda n: _proj_cols(HEAD, n, lambda h, b: (h0 + h, rev(b)), REC_ROWS)
    col_in = pl.BlockSpec((REC_ROWS, HEAD), lambda h, b: (rev(b), h0 + h))
    vec_in = lambda rows: pl.BlockSpec((rows, HEAD), lambda h, b: (0, h0 + h))
    full = lambda a: pl.BlockSpec(a.shape, lambda h, b: (0,) * a.ndim)
    extra_args, extra_specs = (), []
    if gate_of is not None:
        extra_args = (proj, o_raw, w_out_g, rec_g)
        extra_specs = [_proj_cols(HEAD, 5, lambda h, b: (gate_of + h, rev(b)), REC_ROWS),
                       pl.BlockSpec((REC_ROWS, HEAD), lambda h, b: (rev(b), gate_of + h)),
                       pl.BlockSpec((HEAD, D_MODEL), lambda h, b: (SEC_BLK + gate_of + h, 0)),
                       pl.BlockSpec((1, HEAD), lambda h, b: (0, gate_of + h))]
    return _call(
        body, name=name, grid=(HALF_HEADS, N_REC_BLK),
        args=(proj, proj, proj, proj, o_raw, st_prev, dout_b, w_out_g, lb_logits, rec_g,
              consts["tri"], consts["low_t"], consts["sym"], consts["sign"], consts["tri_t"]) + extra_args,
        in_specs=[sec(2), sec(3), sec(4), sec(5), col_in,
                  pl.BlockSpec((None, REC_CHUNKS, HEAD, HEAD), lambda h, b: (h0 + h, rev(b), 0, 0)),
                  pl.BlockSpec((REC_ROWS, D_MODEL), lambda h, b: (rev(b), 0)),
                  pl.BlockSpec((HEAD, D_MODEL), lambda h, b: (SEC_BLK + h0 + h, 0)),
                  vec_in(2), vec_in(1)] + [full(consts[n]) for n in ("tri", "low_t", "sym", "sign", "tri_t")]
                 + extra_specs,
        out_specs=[pl.BlockSpec((n_sec, REC_ROWS, HEAD), lambda h, b: (0, rev(b), h)),
                   pl.BlockSpec((8, HEAD), lambda h, b: (0, h))],
        out_shape=[jax.ShapeDtypeStruct((n_sec, SEQ, HALF_COLS), BF16),
                   jax.ShapeDtypeStruct((8, HALF_COLS), F32)],
        scratch_shapes=[pltpu.VMEM((HEAD, HEAD), F32)],
        exchanges=exchanges)


def _w_in_block(w_ref, j):
    per_shard = W_IN_SHARD // COL_BLK
    return w_ref[j // per_shard, j % per_shard]


def _grad_x(dproj, w_in_g, x, g1, dout, exchanges):
    rows = 256
    n_blk = len(dproj)

    def body(*refs):
        dp_refs = refs[:n_blk]
        w_ref, x_ref, g_ref, dout_ref, dx_ref, part_ref = refs[n_blk:]

        @pl.when(pl.program_id(0) == 0)
        def _():
            part_ref[...] = jnp.zeros_like(part_ref)
        dh = jnp.zeros((rows, D_MODEL), F32)
        for j in range(n_blk):
            dh = dh + _dot_nt(dp_refs[j][...], _w_in_block(w_ref, j))
        xv = x_ref[...]
        inv = lax.rsqrt(jnp.mean(xv * xv, axis=-1, keepdims=True) + EPS)
        a = dh * g_ref[...]
        dx_ref[...] = (dout_ref[...] + inv * a
                       - xv * (inv * inv * inv) * jnp.mean(a * xv, axis=-1, keepdims=True))
        part_ref[0:1, :] += jnp.sum(dh * xv * inv, axis=0, keepdims=True)

    row = lambda: pl.BlockSpec((rows, D_MODEL), lambda i: (i, 0))
    dp_spec = lambda sec, cb: pl.BlockSpec((None, rows, COL_BLK), lambda i: (sec, i, cb))
    return _call(
        body, name="grad_x", grid=(SEQ // rows,),
        args=tuple(a for a, _, _ in dproj) + (w_in_g, x, g1, dout),
        in_specs=[dp_spec(sec, cb) for _, sec, cb in dproj]
                 + [pl.BlockSpec(w_in_g.shape, lambda i: (0, 0, 0, 0)),
                    row(), pl.BlockSpec((1, D_MODEL), lambda i: (0, 0)), row()],
        out_specs=[row(), pl.BlockSpec((8, D_MODEL), lambda i: (0, 0))],
        out_shape=[jax.ShapeDtypeStruct((SEQ, D_MODEL), F32),
                   jax.ShapeDtypeStruct((8, D_MODEL), F32)],
        exchanges=exchanges)


def _grad_w_in(h, dp, blocks, name, collective_id):
    n_blk = len(blocks)
    half = D_MODEL // 2
    pick = lambda vals: (lambda j: functools.reduce(lambda acc, iv: jnp.where(j == iv[0], iv[1], acc),
                                                     list(enumerate(vals))[1:], vals[0]))
    sec_of = pick([sec for sec, _ in blocks])
    cb_of = pick([cb for _, cb in blocks])

    def body(h_ref, dp_ref, p32_ref, p16_ref, keep_ref, send_ref, recv_ref, send_sems, recv_sems):
        j = pl.program_id(0)
        x, y, c = _place()
        cols = lambda cc: pl.ds(pl.multiple_of(cc * half, half), half)

        def copy(i):
            return _remote(send_ref.at[i], recv_ref.at[i], send_sems, recv_sems, i, (x, y, 1 - c))

        pl.when(j == 0)(_other_core_barrier)
        for i in range(n_blk + 1):
            @pl.when(j == i)
            def _(i=i):
                if i < n_blk:
                    send_ref[i] = _dot_tn(h_ref[:, cols(1 - c)], dp_ref[...])
                    copy(i).start()
                    keep_ref[i] = _dot_tn(h_ref[:, cols(c)], dp_ref[...])
                if i > 0:
                    copy(i - 1).wait_recv()
                    tot = keep_ref[i - 1] + recv_ref[i - 1]
                    p32_ref[...] = tot
                    p16_ref[...] = tot.astype(BF16)

        @pl.when(j == n_blk)
        def _():
            for i in range(n_blk):
                copy(i).wait_send()

    lagged = pl.BlockSpec((None, half, COL_BLK), lambda j: (jnp.maximum(j - 1, 0), 0, 0))
    last = n_blk - 1
    return pl.pallas_call(
        body, name=name, grid=(n_blk + 1,),
        in_specs=[pl.BlockSpec((SEQ, D_MODEL), lambda j: (0, 0)),
                  pl.BlockSpec((None, SEQ, COL_BLK),
                               lambda j: (sec_of(jnp.minimum(j, last)), 0, cb_of(jnp.minimum(j, last))))],
        out_specs=[lagged, lagged],
        out_shape=[jax.ShapeDtypeStruct((n_blk, half, COL_BLK), F32),
                   jax.ShapeDtypeStruct((n_blk, half, COL_BLK), BF16)],
        scratch_shapes=[pltpu.VMEM((n_blk, half, COL_BLK), F32)] * 3
                       + [pltpu.SemaphoreType.DMA((n_blk,)), pltpu.SemaphoreType.DMA((n_blk,))],
        compiler_params=_params(dimension_semantics=("arbitrary",), collective_id=collective_id),
    )(h, dp)


def _add_units(grad, recv, place, tile, name):
    n, rows, cols = grad.shape
    per_half = rows // 2 // tile

    def body(place_ref, g_ref, r_ref, o32_ref, o16_ref):
        v = g_ref[...] + r_ref[...]
        o32_ref[...] = v
        o16_ref[...] = v.astype(BF16)

    blk = lambda f: pl.BlockSpec((None, tile, cols), f)
    out = lambda s, i, p: (s, i, 0)
    return pl.pallas_call(
        body, name=name,
        grid_spec=pltpu.PrefetchScalarGridSpec(
            num_scalar_prefetch=1, grid=(n, per_half),
            in_specs=[blk(lambda s, i, p: (s, p[0] * per_half + i, 0)), blk(out)],
            out_specs=[blk(out), blk(out)]),
        out_shape=[jax.ShapeDtypeStruct(recv.shape, F32), jax.ShapeDtypeStruct(recv.shape, BF16)],
        compiler_params=_params(dimension_semantics=("arbitrary", "arbitrary")),
    )(place, grad, recv)


def _sum_units(part32, recv16, place, owners, tile, name):
    n, half, cols = part32.shape
    per_half = half // tile
    table = np.array([[sum(o == chip for o in owners)] + sorted(range(n), key=lambda j: (owners[j] != chip, j))
                      for chip in range(N_SHARDS)], np.int32)
    sched = jnp.concatenate([place[:1], jnp.asarray(table)[place[1]]])

    def block(k, i, p):
        live = k < p[1]
        unit = p[2 + jnp.minimum(k, jnp.maximum(p[1] - 1, 0))]
        return unit, jnp.where(live, i, per_half - 1)

    def body(sched_ref, p_ref, r_ref, o_ref):
        @pl.when(pl.program_id(0) < sched_ref[1])
        def _():
            acc = p_ref[...]
            for j in range(len(CHIP_FLIPS)):
                acc = acc + r_ref[j].astype(F32)
            o_ref[...] = acc

    return pl.pallas_call(
        body, name=name,
        grid_spec=pltpu.PrefetchScalarGridSpec(
            num_scalar_prefetch=1, grid=(n, per_half),
            in_specs=[pl.BlockSpec((None, tile, cols), lambda k, i, p: (*block(k, i, p), 0)),
                      pl.BlockSpec((None, len(CHIP_FLIPS), tile, cols),
                                   lambda k, i, p: (block(k, i, p)[0], 0, block(k, i, p)[1], 0))],
            out_specs=pl.BlockSpec((None, tile, cols),
                                   lambda k, i, p: (block(k, i, p)[0], p[0] * per_half + block(k, i, p)[1], 0))),
        out_shape=jax.ShapeDtypeStruct((n, 2 * half, cols), F32),
        compiler_params=_params(dimension_semantics=("arbitrary", "arbitrary")),
    )(sched, part32, recv16)


def _adamw_math(w, g, m, v):
    m = ADAM_B1 * m + (1.0 - ADAM_B1) * g
    v = ADAM_B2 * v + (1.0 - ADAM_B2) * (g * g)
    m_hat = m / (1.0 - ADAM_B1 ** ADAM_STEP)
    v_hat = v / (1.0 - ADAM_B2 ** ADAM_STEP)
    delta = -ADAM_LR * (m_hat / (jnp.sqrt(v_hat) + ADAM_EPS) + ADAM_WD * w)
    return delta, m, v


def _adamw_units(w, m, v, grads, pick, name):
    rows, cols = w.shape
    bc = grads[0].shape[-1]
    tile = min(rows, 256)
    n_g = len(grads)

    def body(pick_ref, w_ref, m_ref, v_ref, *refs):
        g_refs, (g_out, d_ref, nm_ref, nv_ref) = refs[:n_g], refs[n_g:]
        p = pl.program_id(0)
        for a in range(n_g):
            @pl.when(pick_ref[0, p] == a)
            def _(a=a):
                g = g_refs[a][...]
                g_out[...] = g
                d_ref[...], nm_ref[...], nv_ref[...] = _adamw_math(w_ref[...], g, m_ref[...], v_ref[...])

    blk = pl.BlockSpec((tile, bc), lambda p, i, pick: (i, p))

    def g_spec(a):
        return pl.BlockSpec((None, tile, bc),
                            lambda p, i, pick: (jnp.where(pick[0, p] == a, pick[1, p], 0),
                                                jnp.where(pick[0, p] == a, i, 0), 0))

    return pl.pallas_call(
        body, name=name,
        grid_spec=pltpu.PrefetchScalarGridSpec(
            num_scalar_prefetch=1, grid=(cols // bc, rows // tile),
            in_specs=[blk] * 3 + [g_spec(a) for a in range(n_g)],
            out_specs=[blk] * 4),
        out_shape=[jax.ShapeDtypeStruct(w.shape, F32)] * 4,
        compiler_params=_params(dimension_semantics=("arbitrary", "arbitrary")),
    )(pick, w, m, v, *grads)


ROW_NORM1, ROW_SCALE, ROW_LB, ROW_REC, ROW_FINAL, ROW_LOSS = 0, 1, 2, 4, 5, 6


SMALL_ROWS = (ROW_NORM1, ROW_SCALE, ROW_LB, ROW_REC, ROW_FINAL)


def _small_update(parts, gathered, params):
    n_p = len(params)

    def body(own_ref, p_ref, *refs):
        ins, loss_ref, outs = refs[:3 * n_p], refs[3 * n_p], refs[3 * n_p + 1:]
        x, y, c = _place()
        me = 4 * x + 2 * y + c
        slot = lambda d: jnp.where(me == d, own_ref[...], p_ref[d])
        tot = slot(0)
        for d in range(1, 8):
            tot = tot + slot(d)
        for i, r in enumerate(SMALL_ROWS):
            w = ins[3 * i][...]
            g = tot[r:r + 1, :]
            if r == ROW_LB:
                mx = jnp.maximum(w[0:1, :], w[1:2, :])
                e0 = jnp.exp(w[0:1, :] - mx)
                e1 = jnp.exp(w[1:2, :] - mx)
                lb = e0 / (e0 + e1)
                g = g * lb * (1.0 - lb)
                g = jnp.concatenate([g, -g], axis=0)
            outs[4 * i][...] = g
            outs[4 * i + 1][...], outs[4 * i + 2][...], outs[4 * i + 3][...] = _adamw_math(
                w, g, ins[3 * i + 1][...], ins[3 * i + 2][...])
        loss_ref[...] = (0.5 / D_MODEL) * jnp.sum(tot[ROW_LOSS:ROW_LOSS + 1, :], axis=-1, keepdims=True)

    flat = [a for wmv in params for a in wmv]
    return pl.pallas_call(
        body, name="small_update",
        out_shape=[jax.ShapeDtypeStruct((1, 1), F32)]
                  + [jax.ShapeDtypeStruct(w.shape, F32) for w, _, _ in params for _ in range(4)],
        compiler_params=_params(),
    )(parts, gathered, *flat)


SHARD_OWNERS = tuple(range(N_SHARDS))
BLOCKS_POOL = (0, 1, 2, 3)
BLOCKS_A = (4, 6, 8, 10, 11)
BLOCKS_B = (5, 7, 9)
BLOCK_GROUPS = (BLOCKS_POOL, BLOCKS_A, BLOCKS_B)


def _block_owners(blocks):
    return tuple(j // (W_IN_SHARD // COL_BLK) for j in blocks)


def kernel(x, norm1_g, w_in, pool_w, pool_scale, lb_logits, rec_norm_g, w_out, final_norm_g, loss_target, m_norm1_g, m_w_in, m_pool_w, m_pool_scale, m_lb_logits, m_rec_norm_g, m_w_out, m_final_norm_g, v_norm1_g, v_w_in, v_pool_w, v_pool_scale, v_lb_logits, v_rec_norm_g, v_w_out, v_final_norm_g):
    xi, yi, ci = _place()
    chip = 2 * xi + yi
    place = jnp.stack([ci, chip]).astype(jnp.int32)
    pw_rows = N_GROUPS * PW_SHARD
    flat_pw = lambda a: a.reshape(pw_rows, PG)
    x2, target, gf = x[0], loss_target[0], final_norm_g.reshape(1, D_MODEL)
    consts = {n: jnp.asarray(a, BF16 if n.startswith("tri") else F32) for n, a in _chunk_constants().items()}

    proj, h, w_in_g = _in_proj(x2, norm1_g, _cast_w_in(w_in[0], place), place)
    (y_rec, o_raw, st_prev), ((w_out_g, pw_g),) = _rec_fwd(
        proj, lb_logits, rec_norm_g, consts,
        [_ex_gather([_cast_own(w_out[0], place, "cast_w_out"), _cast_own(flat_pw(pool_w), place, "cast_pool_w")])])
    w_out_g = w_out_g.reshape(2 * D_MODEL, D_MODEL)
    pw_full = pw_g.reshape(N_SHARDS, N_GROUPS, PW_SHARD, PG).transpose(1, 0, 2, 3).reshape(N_GROUPS, PG, PG)
    y_pool = _pool_fwd(proj, pw_full, pool_scale)
    dout, dout_b, part_out = _out_proj_loss(y_pool, y_rec, w_out_g, x2, target, gf)

    p_out32, p_out16 = _grad_w_out(y_pool, y_rec, dout_b)
    (dpool, gpw, dscale), ((rb_out,),) = _pool_bwd(proj, dout_b, w_out_g, pw_full, pool_scale,
                                                   [_ex_send([p_out16], [SHARD_OWNERS])])
    g_out = _sum_units(p_out32, rb_out, place, SHARD_OWNERS, 256, "sum_w_out")
    gpw = gpw.reshape(N_GROUPS, N_SHARDS, PW_SHARD, PG).transpose(1, 0, 2, 3).reshape(N_SHARDS, pw_rows, PG)
    p_inp32, p_inp16 = _grad_w_in(h, dpool, [(0, 0), (0, 1), (1, 0), (1, 1)], "grad_w_in_pool", 1)

    pool_owners, a_owners, b_owners = (_block_owners(b) for b in BLOCK_GROUPS)
    rec_args = (proj, o_raw, st_prev, dout_b, w_out_g, lb_logits, rec_norm_g, consts)
    (drec_a, part_a), ((rb_inp,), (ra_pw,)) = _rec_bwd(
        *rec_args, 0, "rec_bwd_a", [_ex_send([p_inp16], [pool_owners], units=[(0, 1)]), _ex_swap([gpw])],
        gate_of=HALF_HEADS)
    p_pw32, p_pw16 = _add_units(gpw, ra_pw, place, 128, "add_pool_w")
    p_ina32, p_ina16 = _grad_w_in(h, drec_a, [(n, 0) for n in range(5)], "grad_w_in_a", 2)

    (drec_b, part_b), ((rb_inp, rb_ina, rb_pw),) = _rec_bwd(
        *rec_args, HALF_HEADS, "rec_bwd_b",
        [_ex_send([p_inp16, p_ina16, p_pw16], [pool_owners, a_owners, SHARD_OWNERS],
                  units=[(2, 3), tuple(range(len(a_owners))), SHARD_OWNERS], landed=[rb_inp, None, None])])
    g_inp = _sum_units(p_inp32, rb_inp, place, pool_owners, 256, "sum_w_in_pool")
    g_ina = _sum_units(p_ina32, rb_ina, place, a_owners, 256, "sum_w_in_a")
    g_pw = _sum_units(p_pw32, rb_pw, place, SHARD_OWNERS, 128, "sum_pool_w")
    p_inb32, p_inb16 = _grad_w_in(h, drec_b, [(n, 0) for n in range(3)], "grad_w_in_b", 3)

    dproj = ([(dpool, 0, 0), (dpool, 0, 1), (dpool, 1, 0), (dpool, 1, 1)]
             + [(d, n, 0) for n in range(3) for d in (drec_a, drec_b)] + [(drec_a, 3, 0), (drec_a, 4, 0)])
    (dx, part_x), ((rb_inb,),) = _grad_x(dproj, w_in_g, x2, norm1_g, dout, [_ex_send([p_inb16], [b_owners])])
    g_inb = _sum_units(p_inb32, rb_inb, place, b_owners, 256, "sum_w_in_b")
    zero = jnp.zeros((1, D_MODEL), F32)
    part_rec = jnp.concatenate([part_a, part_b], axis=1)
    parts = jnp.concatenate([part_x[0:1], dscale, part_rec[1:2], zero, part_rec[0:1], part_out[0:1],
                             part_out[1:2], zero], axis=0)
    _, ((g_out, g_pw, g_inp, g_ina, g_inb), (gathered,)) = _call(
        None, name="join_halves",
        exchanges=[_ex_join([g_out, g_pw, g_inp, g_ina, g_inb],
                            [SHARD_OWNERS, SHARD_OWNERS, pool_owners, a_owners, b_owners]),
                   _ex_gather_small(parts)])

    group_of = np.zeros((D_PROJ // COL_BLK,), np.int32)
    index_of = np.zeros((D_PROJ // COL_BLK,), np.int32)
    for gi, blocks in enumerate(BLOCK_GROUPS):
        for i, j in enumerate(blocks):
            group_of[j], index_of[j] = gi, i
    per_shard = W_IN_SHARD // COL_BLK
    pick_in = jnp.stack([lax.dynamic_slice(jnp.asarray(group_of), (per_shard * chip,), (per_shard,)),
                         lax.dynamic_slice(jnp.asarray(index_of), (per_shard * chip,), (per_shard,))])
    pick_own = jnp.stack([jnp.zeros((1,), jnp.int32), chip.reshape(1).astype(jnp.int32)])
    big = [_adamw_units(w_in[0], m_w_in[0], v_w_in[0], [g_inp, g_ina, g_inb], pick_in, "adamw_w_in"),
           _adamw_units(w_out[0], m_w_out[0], v_w_out[0], [g_out], pick_own, "adamw_w_out"),
           _adamw_units(flat_pw(pool_w), flat_pw(m_pool_w), flat_pw(v_pool_w), [g_pw], pick_own, "adamw_pool_w")]

    row = lambda a: a.reshape(1, D_MODEL)
    loss, *small = _small_update(parts, gathered, [
        (norm1_g, m_norm1_g, v_norm1_g), (pool_scale, m_pool_scale, v_pool_scale),
        (lb_logits, m_lb_logits, v_lb_logits), (rec_norm_g, m_rec_norm_g, v_rec_norm_g),
        (row(final_norm_g), row(m_final_norm_g), row(v_final_norm_g))])

    def leaves(k):
        norm1, scale, lb, rec, final = (small[4 * i + k] for i in range(len(SMALL_ROWS)))
        return (norm1, big[0][k][None], big[2][k].reshape(pool_w.shape), scale, lb, rec,
                big[1][k][None], final.reshape(D_MODEL))

    return (loss.reshape(()), dx[None], *leaves(0), *leaves(1), *leaves(2), *leaves(3))
```

```python
import functools

import numpy as np
import jax
import jax.numpy as jnp
from jax import lax
from jax.experimental import pallas as pl
from jax.experimental.pallas import tpu as pltpu

F32 = jnp.float32
BF16 = jnp.bfloat16

SEQ = 2048
D_MODEL = 1024
D_PROJ = 6144
N_SEC = 6
N_GROUPS = 4
PG = 256
N_HEADS = 8
HEAD = 128
CHUNK = 64
N_LEVELS = 6
N_SHARDS = 4
W_IN_SHARD = D_PROJ // N_SHARDS
W_OUT_SHARD = 2048 // N_SHARDS
PW_SHARD = PG // N_SHARDS
COL_BLK = 512
EPS = 1e-6

ADAM_LR = 0.001
ADAM_B1 = 0.9
ADAM_B2 = 0.999
ADAM_EPS = 1e-08
ADAM_WD = 0.01
ADAM_STEP = 10

V7X_VMEM_LIMIT = 56 * 1024 * 1024
MESH = pl.DeviceIdType.MESH


def _params(**kw):
    return pltpu.CompilerParams(vmem_limit_bytes=V7X_VMEM_LIMIT, **kw)


def _sig(x):
    return 1.0 / (1.0 + jnp.exp(-x))


def _dot(a, b):
    return jnp.dot(a, b, preferred_element_type=F32)


def _dot_nt(a, b):
    return lax.dot_general(a, b, (((1,), (1,)), ((), ())), preferred_element_type=F32)


def _dot_tn(a, b):
    return lax.dot_general(a, b, (((0,), (0,)), ((), ())), preferred_element_type=F32)


def _split3(a):
    p1 = a.astype(BF16)
    r1 = a - p1.astype(F32)
    p2 = r1.astype(BF16)
    p3 = (r1 - p2.astype(F32)).astype(BF16)
    return jnp.concatenate([p1, p2, p3], axis=-1)


def _dot3(w01, a):
    n = a.shape[-1]
    r = _dot(w01, _split3(a))
    return r[:, :n] + r[:, n:2 * n] + r[:, 2 * n:]


def _chunk_constants():
    j = np.arange(CHUNK)
    tt, ss = np.meshgrid(j, j, indexing="ij")
    x = tt ^ ss
    hb = np.full((CHUNK, CHUNK), -1, np.int32)
    for l in range(N_LEVELS):
        hb[x >= (1 << l)] = l
    sym = np.stack([(hb == l) for l in range(N_LEVELS)]).astype(np.float32)
    low = sym * (tt > ss)
    sign = np.stack([np.where((j >> l) & 1, 1.0, -1.0) for l in range(N_LEVELS)]).astype(np.float32)
    sign = np.ascontiguousarray(np.broadcast_to(sign[:, :, None], (N_LEVELS, CHUNK, HEAD)))
    tri = (ss <= tt).astype(np.float32)
    return dict(tri=tri, tri_t=np.ascontiguousarray(tri.T), low=low,
                low_t=np.ascontiguousarray(low.transpose(0, 2, 1)), sym=sym, sign=sign)


def _in_proj(x, g1, w_slots, place):
    n_col = D_PROJ // COL_BLK
    per_shard = W_IN_SHARD // COL_BLK
    rows = 1024
    half_rows = D_MODEL // 2
    quarter_rows = D_MODEL // 4
    FLIPS = (0, 2, 1, 3)
    ORDER = ([(0, p) for p in range(per_shard)] + [(m, p) for p in range(per_shard) for m in (1, 2)]
             + [(3, p) for p in range(per_shard)])

    def shard_at(m, chip):
        return chip ^ FLIPS[m]

    def pick(vals, t):
        return functools.reduce(lambda acc, iv: jnp.where(t == iv[0], iv[1], acc), list(enumerate(vals))[1:], vals[0])

    def body(place_ref, x_ref, g_ref, w_in_ref, proj_ref, h_ref, w_ref, wbuf, load_sems, send_sems, recv_sems):
        t = pl.program_id(0)
        x_, y_, c = _place()
        chip = 2 * x_ + y_
        me, other_core = (x_, y_, c), (x_, y_, 1 - c)
        x_nbr, y_nbr = (1 - x_, y_, c), (x_, 1 - y_, c)

        def rows_of(half, q=None):
            if q is None:
                return pl.ds(pl.multiple_of(half * half_rows, half_rows), half_rows)
            return pl.ds(pl.multiple_of(half * half_rows + q * quarter_rows, quarter_rows), quarter_rows)

        def block(m, p, r):
            return w_ref.at[shard_at(m, chip), p, r, :]

        def copy(k, ref, to):
            return _remote(ref, ref, send_sems, recv_sems, k, to)

        direct = lambda n, p, to: copy(3 * n + p, block(0, p, rows_of(c)), to)
        relay = lambda n, p, to: copy(6 + 3 * n + p, block(1 + n, p, rows_of(c, n)), to)
        arrived = lambda m, p: ([copy(3 * (m - 1) + p, block(m, p, rows_of(c)), me)] if m < 3 else
                                [copy(6 + 3 * n + p, block(3, p, rows_of(c, n)), me) for n in (0, 1)])
        passed_on = lambda m, p, half, to: copy(9 + 3 * m + p, block(m, p, rows_of(half)), to)

        def load(m, p, slot):
            return pltpu.make_async_copy(w_ref.at[shard_at(m, chip), p], wbuf.at[slot], load_sems.at[slot])

        def prepare(m, p):
            for cp in arrived(m, p):
                cp.wait_recv()
            passed_on(m, p, c, other_core).start()
            if m < 3:
                relay(m - 1, p, y_nbr if m == 1 else x_nbr).start()

        @pl.when(t == 0)
        def _():
            for p in range(per_shard):
                direct(0, p, x_nbr).start()
                direct(1, p, y_nbr).start()
            for p in range(per_shard):
                load(0, p, p).start()

            def norm(i, _):
                r = pl.ds(pl.multiple_of(i * rows, rows), rows)
                xv = x_ref[r, :]
                inv = lax.rsqrt(jnp.mean(xv * xv, axis=-1, keepdims=True) + EPS)
                h_ref[r, :] = (xv * inv * g_ref[...]).astype(BF16)
                return 0
            lax.fori_loop(0, SEQ // rows, norm, 0)

        for step, (m, p) in enumerate(ORDER):
            @pl.when(t == step)
            def _(step=step, m=m, p=p):
                slot = step % per_shard
                if m > 0:
                    passed_on(m, p, 1 - c, me).wait_recv()
                    load(m, p, slot).start()
                if step + 1 < n_col and ORDER[step + 1][0] > 0:
                    prepare(*ORDER[step + 1])
                load(m, p, slot).wait()

                def mm(i, _):
                    r = pl.ds(pl.multiple_of(i * rows, rows), rows)
                    proj_ref[r, :] = _dot(h_ref[r, :], wbuf[slot])
                    return 0
                lax.fori_loop(0, SEQ // rows, mm, 0)

        @pl.when(t == n_col - 1)
        def _():
            for p in range(per_shard):
                sent = [direct(0, p, x_nbr), direct(1, p, y_nbr), relay(0, p, y_nbr), relay(1, p, x_nbr)]
                for cp in sent + [passed_on(m, p, c, other_core) for m in (1, 2, 3)]:
                    cp.wait_send()

    return pl.pallas_call(
        body, name="in_proj",
        grid_spec=pltpu.PrefetchScalarGridSpec(
            num_scalar_prefetch=1, grid=(n_col,),
            in_specs=[pl.BlockSpec((SEQ, D_MODEL), lambda t, p: (0, 0)),
                      pl.BlockSpec((1, D_MODEL), lambda t, p: (0, 0)),
                      pl.BlockSpec(memory_space=pl.ANY)],
            out_specs=[pl.BlockSpec((None, SEQ, COL_BLK),
                                    lambda t, p: (per_shard * (p[1] ^ pick([FLIPS[m] for m, _ in ORDER], t))
                                                  + pick([b for _, b in ORDER], t), 0, 0)),
                       pl.BlockSpec((SEQ, D_MODEL), lambda t, p: (0, 0)),
                       pl.BlockSpec(memory_space=pl.ANY)],
            scratch_shapes=[pltpu.VMEM((per_shard, D_MODEL, COL_BLK), BF16),
                            pltpu.SemaphoreType.DMA((per_shard,)),
                            pltpu.SemaphoreType.DMA((21,)), pltpu.SemaphoreType.DMA((21,))]),
        out_shape=[jax.ShapeDtypeStruct((n_col, SEQ, COL_BLK), F32),
                   jax.ShapeDtypeStruct((SEQ, D_MODEL), BF16),
                   jax.ShapeDtypeStruct(w_slots.shape, BF16)],
        input_output_aliases={3: 2},
        compiler_params=_params(dimension_semantics=("arbitrary",)),
    )(place, x, g1, w_slots)


def _proj_cols(width, section, where, rows=SEQ):
    per_blk = COL_BLK // width

    def index(*grid):
        k, r = where(*grid)
        return section * (D_MODEL // COL_BLK) + k // per_blk, r, k % per_blk

    return pl.BlockSpec((None, rows, width), index)


POOL_ROWS = 256
POOL_HALO = 16
POOL_PAIR = 2


def _window_sums(ext, g, shift_of):
    s = ext
    for k in range(N_GROUPS):
        s = jnp.where(k <= g, s + pltpu.roll(s, shift_of(k), 0), s)
    return s


def _pool_diff(u_ref, i, g):
    n = POOL_ROWS + POOL_HALO
    r0 = i * POOL_ROWS
    cur = u_ref[pl.ds(pl.multiple_of(r0, POOL_ROWS), POOL_ROWS), :]
    before = u_ref[pl.ds(pl.multiple_of(jnp.maximum(r0 - POOL_HALO, 0), 8), POOL_HALO), :]
    before = jnp.where(i > 0, before, 0.0)
    ext = jnp.concatenate([before, cur], axis=0)
    s = _window_sums(ext, g, lambda k: 1 << k)[POOL_HALO:, :]
    t = r0 + lax.broadcasted_iota(jnp.int32, (POOL_ROWS, 1), 0)
    width = (2 << g).astype(F32)
    inv_count = 1.0 / jnp.minimum((t + 1).astype(F32), width)
    return s * inv_count - cur, inv_count


def _pool_fwd(proj, pw_g, pool_scale):
    def body(u_ref, gate_ref, pw_ref, sc_ref, y_ref):
        g = pl.program_id(0)

        def step(ii, _):
            chunks = [POOL_PAIR * ii + a for a in range(POOL_PAIR)]
            ds = [_pool_diff(u_ref, i, g)[0].astype(BF16) for i in chunks]
            mixed = [_dot(d, pw_ref[...]) for d in ds]
            for i, m in zip(chunks, mixed):
                r = pl.ds(pl.multiple_of(i * POOL_ROWS, POOL_ROWS), POOL_ROWS)
                gate = gate_ref[r, :]
                y_ref[r, :] = (m * sc_ref[...] * (gate * _sig(gate))).astype(BF16)
            return 0
        lax.fori_loop(0, SEQ // POOL_ROWS // POOL_PAIR, step, 0)

    return pl.pallas_call(
        body, name="pool_fwd", grid=(N_GROUPS,),
        in_specs=[_proj_cols(PG, 0, lambda g: (g, 0)), _proj_cols(PG, 1, lambda g: (g, 0)),
                  pl.BlockSpec((None, PG, PG), lambda g: (g, 0, 0)),
                  pl.BlockSpec((1, PG), lambda g: (0, g))],
        out_specs=pl.BlockSpec((SEQ, PG), lambda g: (0, g)),
        out_shape=jax.ShapeDtypeStruct((SEQ, D_MODEL), BF16),
        compiler_params=_params(dimension_semantics=("arbitrary",)),
    )(proj, proj, pw_g, pool_scale)


REC_ROWS = 1024
REC_CHUNKS = REC_ROWS // CHUNK
N_REC_BLK = SEQ // REC_ROWS
REC_GROUP = REC_CHUNKS
REC_GROUP_BWD = REC_CHUNKS
SEC_BLK = D_MODEL // HEAD


def _lower_bound(lb_ref):
    l0 = lb_ref[0:1, :]
    l1 = lb_ref[1:2, :]
    mx = jnp.maximum(l0, l1)
    e0 = jnp.exp(l0 - mx)
    e1 = jnp.exp(l1 - mx)
    return e0 / (e0 + e1)


def _gates(q, fl, lb):
    qs = q * _sig(q)
    sf = _sig(fl)
    f = lb + (1.0 - lb) * sf
    return qs, sf, f, 1.0 - f, jnp.log(f)


LOG2E = 1.4426950408889634


def _level_factors(g2, qs, k, sign_ref):
    t = lax.broadcasted_iota(jnp.int32, (CHUNK, HEAD), 0)
    row = lambda r, n: jnp.broadcast_to(g2[r:r + 1, :], (n, HEAD))
    out = []
    for l in range(N_LEVELS):
        m = 1 << l
        if l == 0:
            g_mid = jnp.where((t & 1) == 1, pltpu.roll(g2, 1, 0), g2)
        elif l == 1:
            low = (t & 7) < 4
            g_mid = jnp.concatenate([jnp.where(low[:8], row(8 * i + 1, 8), row(8 * i + 5, 8))
                                     for i in range(CHUNK // 8)], axis=0)
        else:
            g_mid = jnp.concatenate([row(b * 2 * m + m - 1, 2 * m) for b in range(CHUNK // (2 * m))], axis=0)
        sgn = sign_ref[l]
        up = sgn > 0.0
        e = jnp.exp2((g2 - g_mid) * sgn)
        x = jnp.where(up, qs, k) * e
        hi = x.astype(BF16)
        out.append((hi, (x - hi.astype(F32)).astype(BF16), e, up))
    return out


CHIP_FLIPS = ((1, 0), (0, 1), (1, 1))
HBM = pl.BlockSpec(memory_space=pl.ANY)


def _place():
    return lax.axis_index("x"), lax.axis_index("y"), lax.axis_index("c")


def _remote(src, dst, send_sems, recv_sems, k, to):
    return pltpu.make_async_remote_copy(src_ref=src, dst_ref=dst, send_sem=send_sems.at[k],
                                        recv_sem=recv_sems.at[k], device_id=to, device_id_type=MESH)


def _half_rows(ref, c):
    half = ref.shape[-2] // 2
    rows = pl.ds(pl.multiple_of(c * half, half), half)
    return ref.at[:, rows, :] if len(ref.shape) == 3 else ref.at[rows, :]


def _other_core_barrier():
    x, y, c = _place()
    sem = pltpu.get_barrier_semaphore()
    pl.semaphore_signal(sem, inc=1, device_id=(x, y, 1 - c), device_id_type=MESH)
    pl.semaphore_wait(sem, 1)


class _Exchange:
    def __init__(self, inputs, out_shapes, n_sems, start, finish, aliases=None):
        self.inputs, self.out_shapes, self.n_sems = list(inputs), list(out_shapes), n_sems
        self.start, self.finish, self.aliases = start, finish, dict(aliases or {})


def _ex_swap(grads):
    def copies(ins, outs, send, recv):
        x, y, c = _place()
        return [_remote(_half_rows(g, 1 - c), o, send, recv, t, (x, y, 1 - c))
                for t, (g, o) in enumerate(zip(ins, outs))]

    def start(*refs):
        for cp in copies(*refs):
            cp.start()

    def finish(*refs):
        cps = copies(*refs)
        for cp in cps:
            cp.wait_recv()
        for cp in cps:
            cp.wait_send()

    shapes = [jax.ShapeDtypeStruct((a.shape[0], a.shape[1] // 2, a.shape[2]), F32) for a in grads]
    return _Exchange(grads, shapes, len(grads), start, finish)


def _ex_send(parts16, owners, units=None, landed=None):
    n_t = len(parts16)
    units = units or [tuple(range(len(o))) for o in owners]
    landed = landed or [None] * n_t
    given = [t for t in range(n_t) if landed[t] is not None]

    def each(ins, outs, send, recv, to_sender, to_owner):
        x, y, c = _place()
        k = 0
        for t, own in enumerate(owners):
            for j in units[t]:
                for r, (fx, fy) in enumerate(CHIP_FLIPS):
                    tx, ty = x ^ fx, y ^ fy
                    cp = _remote(ins[t].at[j], outs[t].at[j, r], send, recv, k, (tx, ty, c))
                    if to_sender is not None:
                        pl.when(2 * tx + ty == own[j])(functools.partial(to_sender, cp))
                    if to_owner is not None:
                        pl.when(2 * x + y == own[j])(functools.partial(to_owner, cp))
                    k += 1

    def start(*refs):
        each(*refs, lambda cp: cp.start(), None)

    def finish(*refs):
        each(*refs, None, lambda cp: cp.wait_recv())
        each(*refs, lambda cp: cp.wait_send(), None)

    shapes = [jax.ShapeDtypeStruct((a.shape[0], len(CHIP_FLIPS)) + a.shape[1:], BF16) for a in parts16]
    return _Exchange(list(parts16) + [landed[t] for t in given], shapes,
                     len(CHIP_FLIPS) * sum(len(u) for u in units), start, finish,
                     aliases={n_t + i: t for i, t in enumerate(given)})


def _ex_join(units, owners):
    def each(ins, outs, send, recv, fn):
        x, y, c = _place()
        k = 0
        for t, own in enumerate(owners):
            for j, o in enumerate(own):
                def half(cc, to, u=outs[t].at[j], k=k):
                    return _remote(_half_rows(u, cc), _half_rows(u, cc), send, recv, k, to)
                mine = functools.partial(half, c, (x, y, 1 - c))
                theirs = functools.partial(half, 1 - c, (x, y, c))
                pl.when(2 * x + y == o)(functools.partial(fn, mine, theirs))
                k += 1

    def start(*refs):
        each(*refs, lambda mine, theirs: mine().start())

    def finish(*refs):
        each(*refs, lambda mine, theirs: theirs().wait_recv())
        each(*refs, lambda mine, theirs: mine().wait_send())

    shapes = [jax.ShapeDtypeStruct(a.shape, F32) for a in units]
    return _Exchange(units, shapes, sum(len(o) for o in owners), start, finish,
                     aliases={t: t for t in range(len(units))})


def _ex_gather(slots):
    n_t = len(slots)
    n_fl = len(CHIP_FLIPS)

    def piece(ref, shard, half):
        return _half_rows(ref.at[shard], half)

    def first(outs, send, recv):
        x, y, c = _place()
        s = 2 * x + y
        return [_remote(piece(outs[t], s, c), piece(outs[t], s, c), send, recv, n_t * j + t, (x ^ fx, y ^ fy, c))
                for j, (fx, fy) in enumerate(CHIP_FLIPS) for t in range(n_t)]

    def start(ins, outs, send, recv):
        for cp in first(outs, send, recv):
            cp.start()

    def finish(ins, outs, send, recv):
        x, y, c = _place()
        passed = []
        for j, (fx, fy) in enumerate(CHIP_FLIPS):
            sj = 2 * (x ^ fx) + (y ^ fy)
            for t in range(n_t):
                k = n_t * j + t
                _remote(piece(outs[t], sj, c), piece(outs[t], sj, c), send, recv, k, (x, y, c)).wait_recv()
                cp = _remote(piece(outs[t], sj, c), piece(outs[t], sj, c), send, recv, n_t * n_fl + k, (x, y, 1 - c))
                cp.start()
                passed.append(cp)
        for j, (fx, fy) in enumerate(CHIP_FLIPS):
            sj = 2 * (x ^ fx) + (y ^ fy)
            for t in range(n_t):
                k = n_t * n_fl + n_t * j + t
                _remote(piece(outs[t], sj, 1 - c), piece(outs[t], sj, 1 - c), send, recv, k, (x, y, c)).wait_recv()
        for cp in first(outs, send, recv) + passed:
            cp.wait_send()

    shapes = [jax.ShapeDtypeStruct(a.shape, BF16) for a in slots]
    return _Exchange(slots, shapes, 2 * n_t * n_fl, start, finish, aliases={t: t for t in range(n_t)})


def _ex_gather_small(parts):
    def copies(ins, outs, send, recv):
        x, y, c = _place()
        me = 4 * x + 2 * y + c
        return [_remote(ins[0], outs[0].at[me], send, recv, mask - 1,
                        (x ^ (mask >> 2), y ^ ((mask >> 1) & 1), c ^ (mask & 1))) for mask in range(1, 8)]

    def start(*refs):
        for cp in copies(*refs):
            cp.start()

    def finish(ins, outs, send, recv):
        x, y, c = _place()
        me = 4 * x + 2 * y + c
        for mask in range(1, 8):
            _remote(ins[0], outs[0].at[me ^ mask], send, recv, mask - 1, (x, y, c)).wait_recv()
        for cp in copies(ins, outs, send, recv):
            cp.wait_send()

    return _Exchange([parts], [jax.ShapeDtypeStruct((8,) + parts.shape, F32)], 7, start, finish)


def _call(body, *, name, args=(), in_specs=(), out_specs=(), out_shape=(), grid=(), scratch_shapes=(),
          exchanges=()):
    n_in, n_out, n_scr = len(args), len(out_shape), len(scratch_shapes)
    ex_in, ex_out, ex_scr, spans, alias = [], [], [], [], {}
    for ex in exchanges:
        spans.append((len(ex_in), len(ex.inputs), len(ex_out), len(ex.out_shapes)))
        for i, o in ex.aliases.items():
            alias[n_in + len(ex_in) + i] = n_out + len(ex_out) + o
        ex_in += ex.inputs
        ex_out += ex.out_shapes
        ex_scr += [pltpu.SemaphoreType.DMA((ex.n_sems,)), pltpu.SemaphoreType.DMA((ex.n_sems,))]

    def full(*refs):
        ins, x_in = refs[:n_in], refs[n_in:n_in + len(ex_in)]
        outs = refs[n_in + len(ex_in):n_in + len(ex_in) + n_out]
        x_out = refs[n_in + len(ex_in) + n_out:n_in + len(ex_in) + n_out + len(ex_out)]
        scr = refs[len(refs) - n_scr - len(ex_scr):len(refs) - len(ex_scr)]
        sems = refs[len(refs) - len(ex_scr):]

        def run(which):
            for e, (ex, (i0, ni, o0, no)) in enumerate(zip(exchanges, spans)):
                getattr(ex, which)(x_in[i0:i0 + ni], x_out[o0:o0 + no], sems[2 * e], sems[2 * e + 1])

        if grid:
            ids = [pl.program_id(a) for a in range(len(grid))]
            is_first = functools.reduce(jnp.logical_and, [i == 0 for i in ids])
            is_last = functools.reduce(jnp.logical_and, [i == g - 1 for i, g in zip(ids, grid)])
            pl.when(is_first)(lambda: run("start"))
            body(*ins, *outs, *scr)
            pl.when(is_last)(lambda: run("finish"))
        else:
            run("start")
            if body is not None:
                body(*ins, *outs, *scr)
            run("finish")

    kw = dict(grid=grid) if grid else {}
    if grid:
        kw["compiler_params"] = _params(dimension_semantics=("arbitrary",) * len(grid))
    else:
        kw["compiler_params"] = _params()
    res = pl.pallas_call(
        full, name=name,
        in_specs=list(in_specs) + [HBM] * len(ex_in),
        out_specs=list(out_specs) + [HBM] * len(ex_out),
        out_shape=list(out_shape) + ex_out,
        scratch_shapes=list(scratch_shapes) + ex_scr,
        input_output_aliases=alias, **kw,
    )(*args, *ex_in)
    own = list(res[:n_out])
    per_ex = [list(res[n_out + o0:n_out + o0 + no]) for (_, _, o0, no) in spans]
    return own, per_ex


def _cast_own(w, place, name):
    rows, cols = w.shape
    tile = min(rows, 256)

    def body(place_ref, w_ref, o_ref):
        o_ref[...] = w_ref[...].astype(BF16)

    return pl.pallas_call(
        body, name=name,
        grid_spec=pltpu.PrefetchScalarGridSpec(
            num_scalar_prefetch=1, grid=(rows // tile,),
            in_specs=[pl.BlockSpec((tile, cols), lambda i, p: (i, 0))],
            out_specs=pl.BlockSpec((None, tile, cols), lambda i, p: (p[1], i, 0))),
        out_shape=jax.ShapeDtypeStruct((N_SHARDS, rows, cols), BF16),
        compiler_params=_params(dimension_semantics=("arbitrary",)),
    )(place, w)


def _cast_w_in(w, place):
    rows, cols = w.shape
    tile = rows

    def body(place_ref, w_ref, o_ref):
        o_ref[...] = w_ref[...].astype(BF16)

    return pl.pallas_call(
        body, name="cast_w_in",
        grid_spec=pltpu.PrefetchScalarGridSpec(
            num_scalar_prefetch=1, grid=(cols // COL_BLK, rows // tile),
            in_specs=[pl.BlockSpec((tile, COL_BLK), lambda b, i, p: (i, b))],
            out_specs=pl.BlockSpec((None, None, tile, COL_BLK), lambda b, i, p: (p[1], b, i, 0))),
        out_shape=jax.ShapeDtypeStruct((N_SHARDS, cols // COL_BLK, rows, COL_BLK), BF16),
        compiler_params=_params(dimension_semantics=("arbitrary", "arbitrary")),
    )(place, w)


def _rec_fwd(proj, lb_logits, rec_g, consts, exchanges):
    tri, low, sign = consts["tri"], consts["low"], consts["sign"]

    def body(q_ref, f_ref, i_ref, rg_ref, lb_ref, g_ref, w_ref, low_ref, sign_ref, y_ref, o_ref, stp_ref, st_ref):
        @pl.when(pl.program_id(1) == 0)
        def _():
            st_ref[...] = jnp.zeros_like(st_ref)
        lb = _lower_bound(lb_ref)
        st = st_ref[...]
        rows = lambda c: pl.ds(c * CHUNK, CHUNK)
        for c0 in range(0, REC_CHUNKS, REC_GROUP):
            group = range(c0, c0 + REC_GROUP)
            gated = [_gates(q_ref[rows(c), :], f_ref[rows(c), :], lb) for c in group]
            g2s = [_dot3(w_ref[...], g) * LOG2E for (_, _, _, _, g) in gated]
            xs = [[xl for xl, _, _, _ in _level_factors(g2, qs, k, sign_ref)]
                  for g2, (qs, _, _, k, _) in zip(g2s, gated)]
            a_s = []
            for x in xs:
                a = jnp.zeros((CHUNK, CHUNK), F32)
                for l, xl in enumerate(x):
                    a = a + _dot_nt(xl, xl) * low_ref[l]
                a_s.append(a.astype(BF16))
            vbs = [i_ref[rows(c), :].astype(BF16) for c in group]
            intra = [_dot(a, vb) for a, vb in zip(a_s, vbs)]
            kvs = [_dot_tn(vb, (k * jnp.exp2(g2[CHUNK - 1:CHUNK, :] - g2)).astype(BF16))
                   for vb, g2, (_, _, _, k, _) in zip(vbs, g2s, gated)]
            for i, c in enumerate(group):
                qs, _, _, k, _ = gated[i]
                g2 = g2s[i]
                stp_ref[c] = st
                v = i_ref[rows(c), :]
                rg = rg_ref[rows(c), :]
                o = (intra[i] + jnp.sum(qs * k, axis=-1, keepdims=True) * v
                     + _dot_nt((qs * jnp.exp2(g2)).astype(BF16), st.astype(BF16)))
                st = st * jnp.exp2(g2[CHUNK - 1:CHUNK, :]) + kvs[i]
                o_ref[rows(c), :] = o
                inv = lax.rsqrt(jnp.mean(o * o, axis=-1, keepdims=True) + EPS)
                y_ref[rows(c), :] = (o * inv * g_ref[...] * (rg * _sig(rg))).astype(BF16)
        st_ref[...] = st

    sec = lambda n: _proj_cols(HEAD, n, lambda h, b: (h, b), REC_ROWS)
    vec = lambda rows: pl.BlockSpec((rows, HEAD), lambda h, b: (0, h))
    full = lambda a: pl.BlockSpec(a.shape, lambda h, b: (0,) * a.ndim)
    return _call(
        body, name="rec_fwd", grid=(N_HEADS, N_REC_BLK),
        args=(proj, proj, proj, proj, lb_logits, rec_g, tri, low, sign),
        in_specs=[sec(2), sec(3), sec(4), sec(5), vec(2), vec(1), full(tri), full(low), full(sign)],
        out_specs=[pl.BlockSpec((REC_ROWS, HEAD), lambda h, b: (b, h)),
                   pl.BlockSpec((REC_ROWS, HEAD), lambda h, b: (b, h)),
                   pl.BlockSpec((None, REC_CHUNKS, HEAD, HEAD), lambda h, b: (h, b, 0, 0))],
        out_shape=[jax.ShapeDtypeStruct((SEQ, D_MODEL), BF16),
                   jax.ShapeDtypeStruct((SEQ, D_MODEL), F32),
                   jax.ShapeDtypeStruct((N_HEADS, SEQ // CHUNK, HEAD, HEAD), F32)],
        scratch_shapes=[pltpu.VMEM((HEAD, HEAD), F32)],
        exchanges=exchanges)


OUT_ROWS = 512


def _out_proj_loss(y_pool, y_rec, w_out_g, x, target, gf):
    def body(yp_ref, yr_ref, w_ref, x_ref, t_ref, gf_ref, dout_ref, doutb_ref, part_ref):
        @pl.when(pl.program_id(0) == 0)
        def _():
            part_ref[...] = jnp.zeros_like(part_ref)
        halves = [pl.ds(a * (OUT_ROWS // 2), OUT_ROWS // 2) for a in range(2)]
        outs = [x_ref[r, :] + _dot(yp_ref[r, :], w_ref[0:D_MODEL, :])
                + _dot(yr_ref[r, :], w_ref[D_MODEL:2 * D_MODEL, :]) for r in halves]
        gf_v = gf_ref[...]
        for r, out in zip(halves, outs):
            inv = lax.rsqrt(jnp.mean(out * out, axis=-1, keepdims=True) + EPS)
            diff = out * inv * gf_v - t_ref[r, :]
            dyf = diff * (1.0 / D_MODEL)
            a = dyf * gf_v
            dout = inv * a - out * (inv * inv * inv) * jnp.mean(a * out, axis=-1, keepdims=True)
            dout_ref[r, :] = dout
            doutb_ref[r, :] = dout.astype(BF16)
            part_ref[0:1, :] += jnp.sum(dyf * out * inv, axis=0, keepdims=True)
            part_ref[1:2, :] += jnp.sum(diff * diff, axis=0, keepdims=True)

    row = lambda n: pl.BlockSpec((OUT_ROWS, n), lambda i: (i, 0))
    return pl.pallas_call(
        body, name="out_proj_loss", grid=(SEQ // OUT_ROWS,),
        in_specs=[row(D_MODEL), row(D_MODEL), pl.BlockSpec((2 * D_MODEL, D_MODEL), lambda i: (0, 0)),
                  row(D_MODEL), row(D_MODEL), pl.BlockSpec((1, D_MODEL), lambda i: (0, 0))],
        out_specs=[row(D_MODEL), row(D_MODEL), pl.BlockSpec((8, D_MODEL), lambda i: (0, 0))],
        out_shape=[jax.ShapeDtypeStruct((SEQ, D_MODEL), F32),
                   jax.ShapeDtypeStruct((SEQ, D_MODEL), BF16),
                   jax.ShapeDtypeStruct((8, D_MODEL), F32)],
        compiler_params=_params(dimension_semantics=("arbitrary",)),
    )(y_pool, y_rec, w_out_g, x, target, gf)


def _grad_w_out(y_pool, y_rec, dout_b):
    blk = W_OUT_SHARD // 2
    per = D_MODEL // blk
    n = 2 * per

    def body(yp_ref, yr_ref, d_ref, p32_ref, p16_ref, send_ref, recv_ref, send_sems, recv_sems):
        j = pl.program_id(0)
        x, y, c = _place()

        def copy(u):
            return _remote(send_ref.at[u], recv_ref.at[u], send_sems, recv_sems, u, (x, y, 1 - c))

        pl.when(j == 0)(_other_core_barrier)
        for i in range(n):
            @pl.when(j == i)
            def _(i=i):
                res = _dot_tn((yp_ref if i < per else yr_ref)[...], d_ref[...])

                @pl.when(i % 2 == c)
                def _():
                    p32_ref[i // 2] = res

                @pl.when(i % 2 != c)
                def _():
                    send_ref[i // 2] = res
                    copy(i // 2).start()

        @pl.when(j == n - 1)
        def _():
            for u in range(N_SHARDS):
                copy(u).wait_recv()
                tot = p32_ref[u] + recv_ref[u]
                p32_ref[u] = tot
                p16_ref[u] = tot.astype(BF16)
            for u in range(N_SHARDS):
                copy(u).wait_send()

    whole = pl.BlockSpec((N_SHARDS, blk, D_MODEL), lambda j: (0, 0, 0))
    return pl.pallas_call(
        body, name="grad_w_out", grid=(n,),
        in_specs=[pl.BlockSpec((SEQ, blk), lambda j: (0, jnp.minimum(j, per - 1))),
                  pl.BlockSpec((SEQ, blk), lambda j: (0, jnp.maximum(j - per, 0))),
                  pl.BlockSpec((SEQ, D_MODEL), lambda j: (0, 0))],
        out_specs=[whole, whole],
        out_shape=[jax.ShapeDtypeStruct((N_SHARDS, blk, D_MODEL), F32),
                   jax.ShapeDtypeStruct((N_SHARDS, blk, D_MODEL), BF16)],
        scratch_shapes=[pltpu.VMEM((N_SHARDS, blk, D_MODEL), F32), pltpu.VMEM((N_SHARDS, blk, D_MODEL), F32),
                        pltpu.SemaphoreType.DMA((N_SHARDS,)), pltpu.SemaphoreType.DMA((N_SHARDS,))],
        compiler_params=_params(dimension_semantics=("arbitrary",), collective_id=0),
    )(y_pool, y_rec, dout_b)


def _pool_bwd(proj, dout_b, w_out_g, pw_g, pool_scale, exchanges):
    n = POOL_ROWS + POOL_HALO

    def body(u_ref, gate_ref, d_ref, wo_ref, pw_ref, sc_ref,
             dp_ref, dpw_ref, dsc_ref, dd_ref, ddw_ref):
        g = pl.program_id(0)
        dpw_ref[...] = jnp.zeros_like(dpw_ref)
        dsc_ref[...] = jnp.zeros_like(dsc_ref)

        def first(ii, _):
            chunks = [POOL_PAIR * ii + a for a in range(POOL_PAIR)]
            rs = [pl.ds(pl.multiple_of(i * POOL_ROWS, POOL_ROWS), POOL_ROWS) for i in chunks]
            diffs = [_pool_diff(u_ref, i, g) for i in chunks]
            dbs = [d.astype(BF16) for d, _ in diffs]
            mixed = [_dot(db, pw_ref[...]) for db in dbs]
            dys = [_dot_nt(d_ref[r, :], wo_ref[...]) for r in rs]
            sc = sc_ref[...]
            dmbs = []
            for r, m, dy in zip(rs, mixed, dys):
                gate = gate_ref[r, :]
                sg = _sig(gate)
                silu = gate * sg
                dp_ref[1, r, :] = (dy * m * sc * (sg * (1.0 + gate * (1.0 - sg)))).astype(BF16)
                dsc_ref[...] += jnp.sum(dy * silu * m, axis=0, keepdims=True)
                dmbs.append((dy * silu * sc).astype(BF16))
            for db, dmb in zip(dbs, dmbs):
                dpw_ref[...] += _dot_tn(db, dmb)
            dds = [_dot_nt(dmb, pw_ref[...]) for dmb in dmbs]
            for r, dd, (_, inv_count) in zip(rs, dds, diffs):
                dd_ref[r, :] = dd
                ddw_ref[r, :] = dd * inv_count
            return 0
        lax.fori_loop(0, SEQ // POOL_ROWS // POOL_PAIR, first, 0)

        def second(i, _):
            r0 = i * POOL_ROWS
            r = pl.ds(pl.multiple_of(r0, POOL_ROWS), POOL_ROWS)
            last = i == SEQ // POOL_ROWS - 1
            after = ddw_ref[pl.ds(pl.multiple_of(jnp.minimum(r0 + POOL_ROWS, SEQ - POOL_HALO), 8), POOL_HALO), :]
            after = jnp.where(last, 0.0, after)
            ext = jnp.concatenate([ddw_ref[r, :], after], axis=0)
            s = _window_sums(ext, g, lambda k: n - (1 << k))[:POOL_ROWS, :]
            dp_ref[0, r, :] = (s - dd_ref[r, :]).astype(BF16)
            return 0
        lax.fori_loop(0, SEQ // POOL_ROWS, second, 0)

    return _call(
        body, name="pool_bwd", grid=(N_GROUPS,),
        args=(proj, proj, dout_b, w_out_g, pw_g, pool_scale),
        in_specs=[_proj_cols(PG, 0, lambda g: (g, 0)), _proj_cols(PG, 1, lambda g: (g, 0)),
                  pl.BlockSpec((SEQ, D_MODEL), lambda g: (0, 0)),
                  pl.BlockSpec((PG, D_MODEL), lambda g: (g, 0)),
                  pl.BlockSpec((None, PG, PG), lambda g: (g, 0, 0)),
                  pl.BlockSpec((1, PG), lambda g: (0, g))],
        out_specs=[pl.BlockSpec((2, SEQ, PG), lambda g: (0, 0, g)),
                   pl.BlockSpec((None, PG, PG), lambda g: (g, 0, 0)),
                   pl.BlockSpec((1, PG), lambda g: (0, g))],
        out_shape=[jax.ShapeDtypeStruct((2, SEQ, D_MODEL), BF16),
                   jax.ShapeDtypeStruct((N_GROUPS, PG, PG), F32),
                   jax.ShapeDtypeStruct((1, D_MODEL), F32)],
        scratch_shapes=[pltpu.VMEM((SEQ, PG), F32), pltpu.VMEM((SEQ, PG), F32)],
        exchanges=exchanges)


HALF_HEADS = N_HEADS // 2
HALF_COLS = HALF_HEADS * HEAD


def _rec_bwd(proj, o_raw, st_prev, dout_b, w_out_g, lb_logits, rec_g, consts, h0, name, exchanges,
             gate_of=None):
    n_sec = 4 if gate_of is None else 5

    def body(q_ref, f_ref, i_ref, rg_ref, o_ref, stp_ref, d_ref, wo_ref, lb_ref, g_ref,
             w_ref, lowt_ref, sym_ref, sign_ref, tri_ref, *rest):
        dr_ref, part_ref, dst_ref = rest[-3:]

        @pl.when(pl.program_id(1) == 0)
        def _():
            dst_ref[...] = jnp.zeros_like(dst_ref)
            part_ref[...] = jnp.zeros_like(part_ref)
        if gate_of is not None:
            rg2_ref, o2_ref, wo2_ref, g2_ref = rest[:4]
            rg, o = rg2_ref[...], o2_ref[...]
            sg = _sig(rg)
            inv = lax.rsqrt(jnp.mean(o * o, axis=-1, keepdims=True) + EPS)
            dy = _dot_nt(d_ref[...], wo2_ref[...])
            dr_ref[4] = (dy * (o * inv) * g2_ref[...] * (sg * (1.0 + rg * (1.0 - sg)))).astype(BF16)
        tril = (lax.broadcasted_iota(jnp.int32, (CHUNK, CHUNK), 0)
                > lax.broadcasted_iota(jnp.int32, (CHUNK, CHUNK), 1))
        lb = _lower_bound(lb_ref)
        grec = g_ref[...]
        dst = dst_ref[...]
        acc_grec = jnp.zeros((1, HEAD), F32)
        acc_lb = jnp.zeros((1, HEAD), F32)
        rows = lambda c: pl.ds(c * CHUNK, CHUNK)
        for c0 in reversed(range(0, REC_CHUNKS, REC_GROUP_BWD)):
            group = list(reversed(range(c0, c0 + REC_GROUP_BWD)))
            dys = [_dot_nt(d_ref[rows(c), :], wo_ref[...]) for c in group]
            dos = []
            for c, dy in zip(group, dys):
                rg = rg_ref[rows(c), :]
                o = o_ref[rows(c), :]
                sg = _sig(rg)
                silu = rg * sg
                inv = lax.rsqrt(jnp.mean(o * o, axis=-1, keepdims=True) + EPS)
                recn = o * inv
                dr_ref[3, rows(c), :] = (dy * recn * grec * (sg * (1.0 + rg * (1.0 - sg)))).astype(BF16)
                acc_grec = acc_grec + jnp.sum(dy * silu * recn, axis=0, keepdims=True)
                drecn = dy * silu * grec
                dos.append(inv * drecn - o * (inv * inv * inv) * jnp.mean(drecn * o, axis=-1, keepdims=True))
            gated = [_gates(q_ref[rows(c), :], f_ref[rows(c), :], lb) for c in group]
            g2s = [_dot3(w_ref[...], g) * LOG2E for (_, _, _, _, g) in gated]
            levels = [_level_factors(g2, qs, k, sign_ref) for g2, (qs, _, _, k, _) in zip(g2s, gated)]
            a_ts = []
            for lev in levels:
                a_t = jnp.zeros((CHUNK, CHUNK), F32)
                for l, (xl, _, _, _) in enumerate(lev):
                    a_t = a_t + _dot_nt(xl, xl) * lowt_ref[l]
                a_ts.append(a_t.astype(BF16))
            dobs = [do.astype(BF16) for do in dos]
            vbs = [i_ref[rows(c), :].astype(BF16) for c in group]
            d_syms = [jnp.where(tril, _dot_nt(dob, vb), _dot_nt(vb, dob)) for dob, vb in zip(dobs, vbs)]
            dqs_is, dk_is = [], []
            for lev, d_sym in zip(levels, d_syms):
                dqs_i = jnp.zeros((CHUNK, HEAD), F32)
                both_i = jnp.zeros((CHUNK, HEAD), F32)
                for l, (xl, xlo, e, up) in enumerate(lev):
                    z = d_sym * sym_ref[l]
                    tmp = _dot(z.astype(BF16), jnp.concatenate([xl, xlo], axis=-1))
                    tmp = (tmp[:, :HEAD] + tmp[:, HEAD:]) * e
                    dqs_i = dqs_i + jnp.where(up, tmp, 0.0)
                    both_i = both_i + tmp
                dqs_is.append(dqs_i)
                dk_is.append(both_i - dqs_i)
            e_gs = [jnp.exp2(g2) for g2 in g2s]
            e_revs = [jnp.exp2(g2[CHUNK - 1:CHUNK, :] - g2) for g2 in g2s]
            e_lasts = [jnp.exp2(g2[CHUNK - 1:CHUNK, :]) for g2 in g2s]
            q_gs = [qs * e_g for (qs, _, _, _, _), e_g in zip(gated, e_gs)]
            kdecs = [k * e_rev for (_, _, _, k, _), e_rev in zip(gated, e_revs)]
            dv12 = [_dot(a_t, dob) + jnp.sum(qs * k, axis=-1, keepdims=True) * do
                    for a_t, dob, do, (qs, _, _, k, _) in zip(a_ts, dobs, dos, gated)]
            dq_gs = [_dot(dob, stp_ref[c].astype(BF16)) for c, dob in zip(group, dobs)]
            steps = [_dot_tn(dob, q_g.astype(BF16)) for dob, q_g in zip(dobs, q_gs)]
            dsts = []
            for e_last, step in zip(e_lasts, steps):
                dsts.append(dst)
                dst = dst * e_last + step
            dstbs = [d.astype(BF16) for d in dsts]
            dv3 = [_dot_nt(kdec.astype(BF16), dstb) for kdec, dstb in zip(kdecs, dstbs)]
            dkdecs = [_dot(vb, dstb) for vb, dstb in zip(vbs, dstbs)]
            dbig_gs, dg_lasts, dqss, dks = [], [], [], []
            for i, c in enumerate(group):
                qs, _, _, k, _ = gated[i]
                de_last = jnp.sum(stp_ref[c] * dsts[i], axis=0, keepdims=True)
                ddiag = jnp.sum(dos[i] * i_ref[rows(c), :], axis=-1, keepdims=True)
                dqss.append(dqs_is[i] + ddiag * k + dq_gs[i] * e_gs[i])
                dks.append(dk_is[i] + ddiag * qs + dkdecs[i] * e_revs[i])
                dg_rev = dkdecs[i] * kdecs[i]
                dg_lasts.append(jnp.sum(dg_rev, axis=0, keepdims=True) + de_last * e_lasts[i])
                dbig_gs.append(qs * dqs_is[i] - k * dk_is[i] + dq_gs[i] * q_gs[i] - dg_rev)
            dgs = [_dot3(tri_ref[...], dbig_g) + dg_last for dbig_g, dg_last in zip(dbig_gs, dg_lasts)]
            for i, c in enumerate(group):
                _, sf, f, _, _ = gated[i]
                q = q_ref[rows(c), :]
                df = dgs[i] / f - dks[i]
                dr_ref[1, rows(c), :] = (df * (1.0 - lb) * sf * (1.0 - sf)).astype(BF16)
                acc_lb = acc_lb + jnp.sum(df * (1.0 - sf), axis=0, keepdims=True)
                sq = _sig(q)
                dr_ref[0, rows(c), :] = (dqss[i] * (sq * (1.0 + q * (1.0 - sq)))).astype(BF16)
                dr_ref[2, rows(c), :] = (dv12[i] + dv3[i]).astype(BF16)
        dst_ref[...] = dst
        part_ref[0:1, :] += acc_grec
        part_ref[1:2, :] += acc_lb

    rev = lambda b: N_REC_BLK - 1 - b
    sec = lambda n: _proj_cols(HEAD, n, lambda h, b: (h0 + h, rev(b)), REC_ROWS)
    col_in = pl.BlockSpec((REC_ROWS, HEAD), lambda h, b: (rev(b), h0 + h))
    vec_in = lambda rows: pl.BlockSpec((rows, HEAD), lambda h, b: (0, h0 + h))
    full = lambda a: pl.BlockSpec(a.shape, lambda h, b: (0,) * a.ndim)
    extra_args, extra_specs = (), []
    if gate_of is not None:
        extra_args = (proj, o_raw, w_out_g, rec_g)
        extra_specs = [_proj_cols(HEAD, 5, lambda h, b: (gate_of + h, rev(b)), REC_ROWS),
                       pl.BlockSpec((REC_ROWS, HEAD), lambda h, b: (rev(b), gate_of + h)),
                       pl.BlockSpec((HEAD, D_MODEL), lambda h, b: (SEC_BLK + gate_of + h, 0)),
                       pl.BlockSpec((1, HEAD), lambda h, b: (0, gate_of + h))]
    return _call(
        body, name=name, grid=(HALF_HEADS, N_REC_BLK),
        args=(proj, proj, proj, proj, o_raw, st_prev, dout_b, w_out_g, lb_logits, rec_g,
              consts["tri"], consts["low_t"], consts["sym"], consts["sign"], consts["tri_t"]) + extra_args,
        in_specs=[sec(2), sec(3), sec(4), sec(5), col_in,
                  pl.BlockSpec((None, REC_CHUNKS, HEAD, HEAD), lambda h, b: (h0 + h, rev(b), 0, 0)),
                  pl.BlockSpec((REC_ROWS, D_MODEL), lambda h, b: (rev(b), 0)),
                  pl.BlockSpec((HEAD, D_MODEL), lambda h, b: (SEC_BLK + h0 + h, 0)),
                  vec_in(2), vec_in(1)] + [full(consts[n]) for n in ("tri", "low_t", "sym", "sign", "tri_t")]
                 + extra_specs,
        out_specs=[pl.BlockSpec((n_sec, REC_ROWS, HEAD), lambda h, b: (0, rev(b), h)),
                   pl.BlockSpec((8, HEAD), lambda h, b: (0, h))],
        out_shape=[jax.ShapeDtypeStruct((n_sec, SEQ, HALF_COLS), BF16),
                   jax.ShapeDtypeStruct((8, HALF_COLS), F32)],
        scratch_shapes=[pltpu.VMEM((HEAD, HEAD), F32)],
        exchanges=exchanges)


def _w_in_block(w_ref, j):
    per_shard = W_IN_SHARD // COL_BLK
    return w_ref[j // per_shard, j % per_shard]


def _grad_x(dproj, w_in_g, x, g1, dout, exchanges):
    rows = 256
    n_blk = len(dproj)

    def body(*refs):
        dp_refs = refs[:n_blk]
        w_ref, x_ref, g_ref, dout_ref, dx_ref, part_ref = refs[n_blk:]

        @pl.when(pl.program_id(0) == 0)
        def _():
            part_ref[...] = jnp.zeros_like(part_ref)
        dh = jnp.zeros((rows, D_MODEL), F32)
        for j in range(n_blk):
            dh = dh + _dot_nt(dp_refs[j][...], _w_in_block(w_ref, j))
        xv = x_ref[...]
        inv = lax.rsqrt(jnp.mean(xv * xv, axis=-1, keepdims=True) + EPS)
        a = dh * g_ref[...]
        dx_ref[...] = (dout_ref[...] + inv * a
                       - xv * (inv * inv * inv) * jnp.mean(a * xv, axis=-1, keepdims=True))
        part_ref[0:1, :] += jnp.sum(dh * xv * inv, axis=0, keepdims=True)

    row = lambda: pl.BlockSpec((rows, D_MODEL), lambda i: (i, 0))
    dp_spec = lambda sec, cb: pl.BlockSpec((None, rows, COL_BLK), lambda i: (sec, i, cb))
    return _call(
        body, name="grad_x", grid=(SEQ // rows,),
        args=tuple(a for a, _, _ in dproj) + (w_in_g, x, g1, dout),
        in_specs=[dp_spec(sec, cb) for _, sec, cb in dproj]
                 + [pl.BlockSpec(w_in_g.shape, lambda i: (0, 0, 0, 0)),
                    row(), pl.BlockSpec((1, D_MODEL), lambda i: (0, 0)), row()],
        out_specs=[row(), pl.BlockSpec((8, D_MODEL), lambda i: (0, 0))],
        out_shape=[jax.ShapeDtypeStruct((SEQ, D_MODEL), F32),
                   jax.ShapeDtypeStruct((8, D_MODEL), F32)],
        exchanges=exchanges)


def _grad_w_in(h, dp, blocks, name, collective_id):
    n_blk = len(blocks)
    half = D_MODEL // 2
    pick = lambda vals: (lambda j: functools.reduce(lambda acc, iv: jnp.where(j == iv[0], iv[1], acc),
                                                     list(enumerate(vals))[1:], vals[0]))
    sec_of = pick([sec for sec, _ in blocks])
    cb_of = pick([cb for _, cb in blocks])

    def body(h_ref, dp_ref, p32_ref, p16_ref, keep_ref, send_ref, recv_ref, send_sems, recv_sems):
        j = pl.program_id(0)
        x, y, c = _place()
        cols = lambda cc: pl.ds(pl.multiple_of(cc * half, half), half)

        def copy(i):
            return _remote(send_ref.at[i], recv_ref.at[i], send_sems, recv_sems, i, (x, y, 1 - c))

        pl.when(j == 0)(_other_core_barrier)
        for i in range(n_blk + 1):
            @pl.when(j == i)
            def _(i=i):
                if i < n_blk:
                    send_ref[i] = _dot_tn(h_ref[:, cols(1 - c)], dp_ref[...])
                    copy(i).start()
                    keep_ref[i] = _dot_tn(h_ref[:, cols(c)], dp_ref[...])
                if i > 0:
                    copy(i - 1).wait_recv()
                    tot = keep_ref[i - 1] + recv_ref[i - 1]
                    p32_ref[...] = tot
                    p16_ref[...] = tot.astype(BF16)

        @pl.when(j == n_blk)
        def _():
            for i in range(n_blk):
                copy(i).wait_send()

    lagged = pl.BlockSpec((None, half, COL_BLK), lambda j: (jnp.maximum(j - 1, 0), 0, 0))
    last = n_blk - 1
    return pl.pallas_call(
        body, name=name, grid=(n_blk + 1,),
        in_specs=[pl.BlockSpec((SEQ, D_MODEL), lambda j: (0, 0)),
                  pl.BlockSpec((None, SEQ, COL_BLK),
                               lambda j: (sec_of(jnp.minimum(j, last)), 0, cb_of(jnp.minimum(j, last))))],
        out_specs=[lagged, lagged],
        out_shape=[jax.ShapeDtypeStruct((n_blk, half, COL_BLK), F32),
                   jax.ShapeDtypeStruct((n_blk, half, COL_BLK), BF16)],
        scratch_shapes=[pltpu.VMEM((n_blk, half, COL_BLK), F32)] * 3
                       + [pltpu.SemaphoreType.DMA((n_blk,)), pltpu.SemaphoreType.DMA((n_blk,))],
        compiler_params=_params(dimension_semantics=("arbitrary",), collective_id=collective_id),
    )(h, dp)


def _add_units(grad, recv, place, tile, name):
    n, rows, cols = grad.shape
    per_half = rows // 2 // tile

    def body(place_ref, g_ref, r_ref, o32_ref, o16_ref):
        v = g_ref[...] + r_ref[...]
        o32_ref[...] = v
        o16_ref[...] = v.astype(BF16)

    blk = lambda f: pl.BlockSpec((None, tile, cols), f)
    out = lambda s, i, p: (s, i, 0)
    return pl.pallas_call(
        body, name=name,
        grid_spec=pltpu.PrefetchScalarGridSpec(
            num_scalar_prefetch=1, grid=(n, per_half),
            in_specs=[blk(lambda s, i, p: (s, p[0] * per_half + i, 0)), blk(out)],
            out_specs=[blk(out), blk(out)]),
        out_shape=[jax.ShapeDtypeStruct(recv.shape, F32), jax.ShapeDtypeStruct(recv.shape, BF16)],
        compiler_params=_params(dimension_semantics=("arbitrary", "arbitrary")),
    )(place, grad, recv)


def _sum_units(part32, recv16, place, owners, tile, name):
    n, half, cols = part32.shape
    per_half = half // tile
    table = np.array([[sum(o == chip for o in owners)] + sorted(range(n), key=lambda j: (owners[j] != chip, j))
                      for chip in range(N_SHARDS)], np.int32)
    sched = jnp.concatenate([place[:1], jnp.asarray(table)[place[1]]])

    def block(k, i, p):
        live = k < p[1]
        unit = p[2 + jnp.minimum(k, jnp.maximum(p[1] - 1, 0))]
        return unit, jnp.where(live, i, per_half - 1)

    def body(sched_ref, p_ref, r_ref, o_ref):
        @pl.when(pl.program_id(0) < sched_ref[1])
        def _():
            acc = p_ref[...]
            for j in range(len(CHIP_FLIPS)):
                acc = acc + r_ref[j].astype(F32)
            o_ref[...] = acc

    return pl.pallas_call(
        body, name=name,
        grid_spec=pltpu.PrefetchScalarGridSpec(
            num_scalar_prefetch=1, grid=(n, per_half),
            in_specs=[pl.BlockSpec((None, tile, cols), lambda k, i, p: (*block(k, i, p), 0)),
                      pl.BlockSpec((None, len(CHIP_FLIPS), tile, cols),
                                   lambda k, i, p: (block(k, i, p)[0], 0, block(k, i, p)[1], 0))],
            out_specs=pl.BlockSpec((None, tile, cols),
                                   lambda k, i, p: (block(k, i, p)[0], p[0] * per_half + block(k, i, p)[1], 0))),
        out_shape=jax.ShapeDtypeStruct((n, 2 * half, cols), F32),
        compiler_params=_params(dimension_semantics=("arbitrary", "arbitrary")),
    )(sched, part32, recv16)


def _adamw_math(w, g, m, v):
    m = ADAM_B1 * m + (1.0 - ADAM_B1) * g
    v = ADAM_B2 * v + (1.0 - ADAM_B2) * (g * g)
    m_hat = m / (1.0 - ADAM_B1 ** ADAM_STEP)
    v_hat = v / (1.0 - ADAM_B2 ** ADAM_STEP)
    delta = -ADAM_LR * (m_hat / (jnp.sqrt(v_hat) + ADAM_EPS) + ADAM_WD * w)
    return delta, m, v


def _adamw_units(w, m, v, grads, pick, name):
    rows, cols = w.shape
    bc = grads[0].shape[-1]
    tile = min(rows, 256)
    n_g = len(grads)

    def body(pick_ref, w_ref, m_ref, v_ref, *refs):
        g_refs, (g_out, d_ref, nm_ref, nv_ref) = refs[:n_g], refs[n_g:]
        p = pl.program_id(0)
        for a in range(n_g):
            @pl.when(pick_ref[0, p] == a)
            def _(a=a):
                g = g_refs[a][...]
                g_out[...] = g
                d_ref[...], nm_ref[...], nv_ref[...] = _adamw_math(w_ref[...], g, m_ref[...], v_ref[...])

    blk = pl.BlockSpec((tile, bc), lambda p, i, pick: (i, p))

    def g_spec(a):
        return pl.BlockSpec((None, tile, bc),
                            lambda p, i, pick: (jnp.where(pick[0, p] == a, pick[1, p], 0),
                                                jnp.where(pick[0, p] == a, i, 0), 0))

    return pl.pallas_call(
        body, name=name,
        grid_spec=pltpu.PrefetchScalarGridSpec(
            num_scalar_prefetch=1, grid=(cols // bc, rows // tile),
            in_specs=[blk] * 3 + [g_spec(a) for a in range(n_g)],
            out_specs=[blk] * 4),
        out_shape=[jax.ShapeDtypeStruct(w.shape, F32)] * 4,
        compiler_params=_params(dimension_semantics=("arbitrary", "arbitrary")),
    )(pick, w, m, v, *grads)


ROW_NORM1, ROW_SCALE, ROW_LB, ROW_REC, ROW_FINAL, ROW_LOSS = 0, 1, 2, 4, 5, 6


SMALL_ROWS = (ROW_NORM1, ROW_SCALE, ROW_LB, ROW_REC, ROW_FINAL)


def _small_update(parts, gathered, params):
    n_p = len(params)

    def body(own_ref, p_ref, *refs):
        ins, loss_ref, outs = refs[:3 * n_p], refs[3 * n_p], refs[3 * n_p + 1:]
        x, y, c = _place()
        me = 4 * x + 2 * y + c
        slot = lambda d: jnp.where(me == d, own_ref[...], p_ref[d])
        tot = slot(0)
        for d in range(1, 8):
            tot = tot + slot(d)
        for i, r in enumerate(SMALL_ROWS):
            w = ins[3 * i][...]
            g = tot[r:r + 1, :]
            if r == ROW_LB:
                mx = jnp.maximum(w[0:1, :], w[1:2, :])
                e0 = jnp.exp(w[0:1, :] - mx)
                e1 = jnp.exp(w[1:2, :] - mx)
                lb = e0 / (e0 + e1)
                g = g * lb * (1.0 - lb)
                g = jnp.concatenate([g, -g], axis=0)
            outs[4 * i][...] = g
            outs[4 * i + 1][...], outs[4 * i + 2][...], outs[4 * i + 3][...] = _adamw_math(
                w, g, ins[3 * i + 1][...], ins[3 * i + 2][...])
        loss_ref[...] = (0.5 / D_MODEL) * jnp.sum(tot[ROW_LOSS:ROW_LOSS + 1, :], axis=-1, keepdims=True)

    flat = [a for wmv in params for a in wmv]
    return pl.pallas_call(
        body, name="small_update",
        out_shape=[jax.ShapeDtypeStruct((1, 1), F32)]
                  + [jax.ShapeDtypeStruct(w.shape, F32) for w, _, _ in params for _ in range(4)],
        compiler_params=_params(),
    )(parts, gathered, *flat)


SHARD_OWNERS = tuple(range(N_SHARDS))
BLOCKS_POOL = (0, 1, 2, 3)
BLOCKS_A = (4, 6, 8, 10, 11)
BLOCKS_B = (5, 7, 9)
BLOCK_GROUPS = (BLOCKS_POOL, BLOCKS_A, BLOCKS_B)


def _block_owners(blocks):
    return tuple(j // (W_IN_SHARD // COL_BLK) for j in blocks)


def kernel(x, norm1_g, w_in, pool_w, pool_scale, lb_logits, rec_norm_g, w_out, final_norm_g, loss_target, m_norm1_g, m_w_in, m_pool_w, m_pool_scale, m_lb_logits, m_rec_norm_g, m_w_out, m_final_norm_g, v_norm1_g, v_w_in, v_pool_w, v_pool_scale, v_lb_logits, v_rec_norm_g, v_w_out, v_final_norm_g):
    xi, yi, ci = _place()
    chip = 2 * xi + yi
    place = jnp.stack([ci, chip]).astype(jnp.int32)
    pw_rows = N_GROUPS * PW_SHARD
    flat_pw = lambda a: a.reshape(pw_rows, PG)
    x2, target, gf = x[0], loss_target[0], final_norm_g.reshape(1, D_MODEL)
    consts = {n: jnp.asarray(a, BF16 if n.startswith("tri") else F32) for n, a in _chunk_constants().items()}

    proj, h, w_in_g = _in_proj(x2, norm1_g, _cast_w_in(w_in[0], place), place)
    (y_rec, o_raw, st_prev), ((w_out_g, pw_g),) = _rec_fwd(
        proj, lb_logits, rec_norm_g, consts,
        [_ex_gather([_cast_own(w_out[0], place, "cast_w_out"), _cast_own(flat_pw(pool_w), place, "cast_pool_w")])])
    w_out_g = w_out_g.reshape(2 * D_MODEL, D_MODEL)
    pw_full = pw_g.reshape(N_SHARDS, N_GROUPS, PW_SHARD, PG).transpose(1, 0, 2, 3).reshape(N_GROUPS, PG, PG)
    y_pool = _pool_fwd(proj, pw_full, pool_scale)
    dout, dout_b, part_out = _out_proj_loss(y_pool, y_rec, w_out_g, x2, target, gf)

    p_out32, p_out16 = _grad_w_out(y_pool, y_rec, dout_b)
    (dpool, gpw, dscale), ((rb_out,),) = _pool_bwd(proj, dout_b, w_out_g, pw_full, pool_scale,
                                                   [_ex_send([p_out16], [SHARD_OWNERS])])
    g_out = _sum_units(p_out32, rb_out, place, SHARD_OWNERS, 256, "sum_w_out")
    gpw = gpw.reshape(N_GROUPS, N_SHARDS, PW_SHARD, PG).transpose(1, 0, 2, 3).reshape(N_SHARDS, pw_rows, PG)
    p_inp32, p_inp16 = _grad_w_in(h, dpool, [(0, 0), (0, 1), (1, 0), (1, 1)], "grad_w_in_pool", 1)

    pool_owners, a_owners, b_owners = (_block_owners(b) for b in BLOCK_GROUPS)
    rec_args = (proj, o_raw, st_prev, dout_b, w_out_g, lb_logits, rec_norm_g, consts)
    (drec_a, part_a), ((rb_inp,), (ra_pw,)) = _rec_bwd(
        *rec_args, 0, "rec_bwd_a", [_ex_send([p_inp16], [pool_owners], units=[(0, 1)]), _ex_swap([gpw])],
        gate_of=HALF_HEADS)
    p_pw32, p_pw16 = _add_units(gpw, ra_pw, place, 128, "add_pool_w")
    p_ina32, p_ina16 = _grad_w_in(h, drec_a, [(n, 0) for n in range(5)], "grad_w_in_a", 2)

    (drec_b, part_b), ((rb_inp, rb_ina, rb_pw),) = _rec_bwd(
        *rec_args, HALF_HEADS, "rec_bwd_b",
        [_ex_send([p_inp16, p_ina16, p_pw16], [pool_owners, a_owners, SHARD_OWNERS],
                  units=[(2, 3), tuple(range(len(a_owners))), SHARD_OWNERS], landed=[rb_inp, None, None])])
    g_inp = _sum_units(p_inp32, rb_inp, place, pool_owners, 256, "sum_w_in_pool")
    g_ina = _sum_units(p_ina32, rb_ina, place, a_owners, 256, "sum_w_in_a")
    g_pw = _sum_units(p_pw32, rb_pw, place, SHARD_OWNERS, 128, "sum_pool_w")
    p_inb32, p_inb16 = _grad_w_in(h, drec_b, [(n, 0) for n in range(3)], "grad_w_in_b", 3)

    dproj = ([(dpool, 0, 0), (dpool, 0, 1), (dpool, 1, 0), (dpool, 1, 1)]
             + [(d, n, 0) for n in range(3) for d in (drec_a, drec_b)] + [(drec_a, 3, 0), (drec_a, 4, 0)])
    (dx, part_x), ((rb_inb,),) = _grad_x(dproj, w_in_g, x2, norm1_g, dout, [_ex_send([p_inb16], [b_owners])])
    g_inb = _sum_units(p_inb32, rb_inb, place, b_owners, 256, "sum_w_in_b")
    zero = jnp.zeros((1, D_MODEL), F32)
    part_rec = jnp.concatenate([part_a, part_b], axis=1)
    parts = jnp.concatenate([part_x[0:1], dscale, part_rec[1:2], zero, part_rec[0:1], part_out[0:1],
                             part_out[1:2], zero], axis=0)
    _, ((g_out, g_pw, g_inp, g_ina, g_inb), (gathered,)) = _call(
        None, name="join_halves",
        exchanges=[_ex_join([g_out, g_pw, g_inp, g_ina, g_inb],
                            [SHARD_OWNERS, SHARD_OWNERS, pool_owners, a_owners, b_owners]),
                   _ex_gather_small(parts)])

    group_of = np.zeros((D_PROJ // COL_BLK,), np.int32)
    index_of = np.zeros((D_PROJ // COL_BLK,), np.int32)
    for gi, blocks in enumerate(BLOCK_GROUPS):
        for i, j in enumerate(blocks):
            group_of[j], index_of[j] = gi, i
    per_shard = W_IN_SHARD // COL_BLK
    pick_in = jnp.stack([lax.dynamic_slice(jnp.asarray(group_of), (per_shard * chip,), (per_shard,)),
                         lax.dynamic_slice(jnp.asarray(index_of), (per_shard * chip,), (per_shard,))])
    pick_own = jnp.stack([jnp.zeros((1,), jnp.int32), chip.reshape(1).astype(jnp.int32)])
    big = [_adamw_units(w_in[0], m_w_in[0], v_w_in[0], [g_inp, g_ina, g_inb], pick_in, "adamw_w_in"),
           _adamw_units(w_out[0], m_w_out[0], v_w_out[0], [g_out], pick_own, "adamw_w_out"),
           _adamw_units(flat_pw(pool_w), flat_pw(m_pool_w), flat_pw(v_pool_w), [g_pw], pick_own, "adamw_pool_w")]

    row = lambda a: a.reshape(1, D_MODEL)
    loss, *small = _small_update(parts, gathered, [
        (norm1_g, m_norm1_g, v_norm1_g), (pool_scale, m_pool_scale, v_pool_scale),
        (lb_logits, m_lb_logits, v_lb_logits), (rec_norm_g, m_rec_norm_g, v_rec_norm_g),
        (row(final_norm_g), row(m_final_norm_g), row(v_final_norm_g))])

    def leaves(k):
        norm1, scale, lb, rec, final = (small[4 * i + k] for i in range(len(SMALL_ROWS)))
        return (norm1, big[0][k][None], big[2][k].reshape(pool_w.shape), scale, lb, rec,
                big[1][k][None], final.reshape(D_MODEL))

    return (loss.reshape(()), dx[None], *leaves(0), *leaves(1), *leaves(2), *leaves(3))
```

```python
import functools

import numpy as np
import jax
import jax.numpy as jnp
from jax import lax
from jax.experimental import pallas as pl
from jax.experimental.pallas import tpu as pltpu

F32 = jnp.float32
BF16 = jnp.bfloat16

SEQ = 2048
D_MODEL = 1024
D_PROJ = 6144
N_SEC = 6
N_GROUPS = 4
PG = 256
N_HEADS = 8
HEAD = 128
CHUNK = 64
N_LEVELS = 6
N_SHARDS = 4
W_IN_SHARD = D_PROJ // N_SHARDS
W_OUT_SHARD = 2048 // N_SHARDS
PW_SHARD = PG // N_SHARDS
COL_BLK = 512
EPS = 1e-6

ADAM_LR = 0.001
ADAM_B1 = 0.9
ADAM_B2 = 0.999
ADAM_EPS = 1e-08
ADAM_WD = 0.01
ADAM_STEP = 10

V7X_VMEM_LIMIT = 56 * 1024 * 1024
MESH = pl.DeviceIdType.MESH


def _params(**kw):
    return pltpu.CompilerParams(vmem_limit_bytes=V7X_VMEM_LIMIT, **kw)


def _sig(x):
    return 1.0 / (1.0 + jnp.exp(-x))


def _dot(a, b):
    return jnp.dot(a, b, preferred_element_type=F32)


def _dot_nt(a, b):
    return lax.dot_general(a, b, (((1,), (1,)), ((), ())), preferred_element_type=F32)


def _dot_tn(a, b):
    return lax.dot_general(a, b, (((0,), (0,)), ((), ())), preferred_element_type=F32)


def _split3(a):
    p1 = a.astype(BF16)
    r1 = a - p1.astype(F32)
    p2 = r1.astype(BF16)
    p3 = (r1 - p2.astype(F32)).astype(BF16)
    return jnp.concatenate([p1, p2, p3], axis=-1)


def _dot3(w01, a):
    n = a.shape[-1]
    r = _dot(w01, _split3(a))
    return r[:, :n] + r[:, n:2 * n] + r[:, 2 * n:]


def _chunk_constants():
    j = np.arange(CHUNK)
    tt, ss = np.meshgrid(j, j, indexing="ij")
    x = tt ^ ss
    hb = np.full((CHUNK, CHUNK), -1, np.int32)
    for l in range(N_LEVELS):
        hb[x >= (1 << l)] = l
    sym = np.stack([(hb == l) for l in range(N_LEVELS)]).astype(np.float32)
    low = sym * (tt > ss)
    sign = np.stack([np.where((j >> l) & 1, 1.0, -1.0) for l in range(N_LEVELS)]).astype(np.float32)
    sign = np.ascontiguousarray(np.broadcast_to(sign[:, :, None], (N_LEVELS, CHUNK, HEAD)))
    tri = (ss <= tt).astype(np.float32)
    return dict(tri=tri, tri_t=np.ascontiguousarray(tri.T), low=low,
                low_t=np.ascontiguousarray(low.transpose(0, 2, 1)), sym=sym, sign=sign)


def _in_proj(x, g1, w_slots, place, also_cast):
    n_col = D_PROJ // COL_BLK
    per_shard = W_IN_SHARD // COL_BLK
    rows = 1024
    half_rows = D_MODEL // 2
    quarter_rows = D_MODEL // 4
    FLIPS = (0, 2, 1, 3)
    ORDER = ([(0, p) for p in range(per_shard)] + [(m, p) for p in range(per_shard) for m in (1, 2)]
             + [(3, p) for p in range(per_shard)])

    def shard_at(m, chip):
        return chip ^ FLIPS[m]

    def pick(vals, t):
        return functools.reduce(lambda acc, iv: jnp.where(t == iv[0], iv[1], acc), list(enumerate(vals))[1:], vals[0])

    def body(place_ref, x_ref, g_ref, w_in_ref, *rest):
        others, (proj_ref, h_ref, w_ref) = rest[:len(also_cast)], rest[len(also_cast):len(also_cast) + 3]
        slots = rest[len(also_cast) + 3:2 * len(also_cast) + 3]
        wbuf, load_sems, send_sems, recv_sems = rest[2 * len(also_cast) + 3:]
        t = pl.program_id(0)

        @pl.when(t == 1)
        def _():
            for src, dst in zip(others, slots):
                dst[...] = src[...].astype(BF16)
        x_, y_, c = _place()
        chip = 2 * x_ + y_
        me, other_core = (x_, y_, c), (x_, y_, 1 - c)
        x_nbr, y_nbr = (1 - x_, y_, c), (x_, 1 - y_, c)

        def rows_of(half, q=None):
            if q is None:
                return pl.ds(pl.multiple_of(half * half_rows, half_rows), half_rows)
            return pl.ds(pl.multiple_of(half * half_rows + q * quarter_rows, quarter_rows), quarter_rows)

        def block(m, p, r):
            return w_ref.at[shard_at(m, chip), p, r, :]

        def copy(k, ref, to):
            return _remote(ref, ref, send_sems, recv_sems, k, to)

        direct = lambda n, p, to: copy(3 * n + p, block(0, p, rows_of(c)), to)
        relay = lambda n, p, to: copy(6 + 3 * n + p, block(1 + n, p, rows_of(c, n)), to)
        arrived = lambda m, p: ([copy(3 * (m - 1) + p, block(m, p, rows_of(c)), me)] if m < 3 else
                                [copy(6 + 3 * n + p, block(3, p, rows_of(c, n)), me) for n in (0, 1)])
        passed_on = lambda m, p, half, to: copy(9 + 3 * m + p, block(m, p, rows_of(half)), to)

        def load(m, p, slot):
            return pltpu.make_async_copy(w_ref.at[shard_at(m, chip), p], wbuf.at[slot], load_sems.at[slot])

        def prepare(m, p):
            for cp in arrived(m, p):
                cp.wait_recv()
            passed_on(m, p, c, other_core).start()
            if m < 3:
                relay(m - 1, p, y_nbr if m == 1 else x_nbr).start()

        @pl.when(t == 0)
        def _():
            for p in range(per_shard):
                direct(0, p, x_nbr).start()
                direct(1, p, y_nbr).start()
            for p in range(per_shard):
                load(0, p, p).start()

            def norm(i, _):
                r = pl.ds(pl.multiple_of(i * rows, rows), rows)
                xv = x_ref[r, :]
                inv = lax.rsqrt(jnp.mean(xv * xv, axis=-1, keepdims=True) + EPS)
                h_ref[r, :] = (xv * inv * g_ref[...]).astype(BF16)
                return 0
            lax.fori_loop(0, SEQ // rows, norm, 0)

        for step, (m, p) in enumerate(ORDER):
            @pl.when(t == step)
            def _(step=step, m=m, p=p):
                slot = step % per_shard
                if m > 0:
                    passed_on(m, p, 1 - c, me).wait_recv()
                    load(m, p, slot).start()
                if step + 1 < n_col and ORDER[step + 1][0] > 0:
                    prepare(*ORDER[step + 1])
                load(m, p, slot).wait()

                def mm(i, _):
                    r = pl.ds(pl.multiple_of(i * rows, rows), rows)
                    proj_ref[r, :] = _dot(h_ref[r, :], wbuf[slot])
                    return 0
                lax.fori_loop(0, SEQ // rows, mm, 0)

        @pl.when(t == n_col - 1)
        def _():
            for p in range(per_shard):
                sent = [direct(0, p, x_nbr), direct(1, p, y_nbr), relay(0, p, y_nbr), relay(1, p, x_nbr)]
                for cp in sent + [passed_on(m, p, c, other_core) for m in (1, 2, 3)]:
                    cp.wait_send()

    return pl.pallas_call(
        body, name="in_proj",
        grid_spec=pltpu.PrefetchScalarGridSpec(
            num_scalar_prefetch=1, grid=(n_col,),
            in_specs=[pl.BlockSpec((SEQ, D_MODEL), lambda t, p: (0, 0)),
                      pl.BlockSpec((1, D_MODEL), lambda t, p: (0, 0)),
                      pl.BlockSpec(memory_space=pl.ANY)]
                     + [pl.BlockSpec(a.shape, lambda t, p: (0, 0)) for a in also_cast],
            out_specs=[pl.BlockSpec((None, SEQ, COL_BLK),
                                    lambda t, p: (per_shard * (p[1] ^ pick([FLIPS[m] for m, _ in ORDER], t))
                                                  + pick([b for _, b in ORDER], t), 0, 0)),
                       pl.BlockSpec((SEQ, D_MODEL), lambda t, p: (0, 0)),
                       pl.BlockSpec(memory_space=pl.ANY)]
                      + [pl.BlockSpec((None,) + a.shape, lambda t, p: (p[1], 0, 0)) for a in also_cast],
            scratch_shapes=[pltpu.VMEM((per_shard, D_MODEL, COL_BLK), BF16),
                            pltpu.SemaphoreType.DMA((per_shard,)),
                            pltpu.SemaphoreType.DMA((21,)), pltpu.SemaphoreType.DMA((21,))]),
        out_shape=[jax.ShapeDtypeStruct((n_col, SEQ, COL_BLK), F32),
                   jax.ShapeDtypeStruct((SEQ, D_MODEL), BF16),
                   jax.ShapeDtypeStruct(w_slots.shape, BF16)]
                  + [jax.ShapeDtypeStruct((N_SHARDS,) + a.shape, BF16) for a in also_cast],
        input_output_aliases={3: 2},
        compiler_params=_params(dimension_semantics=("arbitrary",)),
    )(place, x, g1, w_slots, *also_cast)


def _proj_cols(width, section, where, rows=SEQ):
    per_blk = COL_BLK // width

    def index(*grid):
        k, r = where(*grid)
        return section * (D_MODEL // COL_BLK) + k // per_blk, r, k % per_blk

    return pl.BlockSpec((None, rows, width), index)


POOL_ROWS = 256
POOL_HALO = 16
POOL_PAIR = 2


def _window_sums(ext, g, shift_of):
    s = ext
    for k in range(N_GROUPS):
        s = jnp.where(k <= g, s + pltpu.roll(s, shift_of(k), 0), s)
    return s


def _pool_diff(u_ref, i, g):
    n = POOL_ROWS + POOL_HALO
    r0 = i * POOL_ROWS
    cur = u_ref[pl.ds(pl.multiple_of(r0, POOL_ROWS), POOL_ROWS), :]
    before = u_ref[pl.ds(pl.multiple_of(jnp.maximum(r0 - POOL_HALO, 0), 8), POOL_HALO), :]
    before = jnp.where(i > 0, before, 0.0)
    ext = jnp.concatenate([before, cur], axis=0)
    s = _window_sums(ext, g, lambda k: 1 << k)[POOL_HALO:, :]
    t = r0 + lax.broadcasted_iota(jnp.int32, (POOL_ROWS, 1), 0)
    width = (2 << g).astype(F32)
    inv_count = 1.0 / jnp.minimum((t + 1).astype(F32), width)
    return s * inv_count - cur, inv_count


def _pool_fwd(proj, pw_g, pool_scale):
    def body(u_ref, gate_ref, pw_ref, sc_ref, y_ref):
        g = pl.program_id(0)

        def step(ii, _):
            chunks = [POOL_PAIR * ii + a for a in range(POOL_PAIR)]
            ds = [_pool_diff(u_ref, i, g)[0].astype(BF16) for i in chunks]
            mixed = [_dot(d, pw_ref[...]) for d in ds]
            for i, m in zip(chunks, mixed):
                r = pl.ds(pl.multiple_of(i * POOL_ROWS, POOL_ROWS), POOL_ROWS)
                gate = gate_ref[r, :]
                y_ref[r, :] = (m * sc_ref[...] * (gate * _sig(gate))).astype(BF16)
            return 0
        lax.fori_loop(0, SEQ // POOL_ROWS // POOL_PAIR, step, 0)

    return pl.pallas_call(
        body, name="pool_fwd", grid=(N_GROUPS,),
        in_specs=[_proj_cols(PG, 0, lambda g: (g, 0)), _proj_cols(PG, 1, lambda g: (g, 0)),
                  pl.BlockSpec((None, PG, PG), lambda g: (g, 0, 0)),
                  pl.BlockSpec((1, PG), lambda g: (0, g))],
        out_specs=pl.BlockSpec((SEQ, PG), lambda g: (0, g)),
        out_shape=jax.ShapeDtypeStruct((SEQ, D_MODEL), BF16),
        compiler_params=_params(dimension_semantics=("arbitrary",)),
    )(proj, proj, pw_g, pool_scale)


REC_ROWS = 1024
REC_CHUNKS = REC_ROWS // CHUNK
N_REC_BLK = SEQ // REC_ROWS
REC_GROUP = REC_CHUNKS
REC_GROUP_BWD = REC_CHUNKS
SEC_BLK = D_MODEL // HEAD


def _lower_bound(lb_ref):
    l0 = lb_ref[0:1, :]
    l1 = lb_ref[1:2, :]
    mx = jnp.maximum(l0, l1)
    e0 = jnp.exp(l0 - mx)
    e1 = jnp.exp(l1 - mx)
    return e0 / (e0 + e1)


def _gates(q, fl, lb):
    qs = q * _sig(q)
    sf = _sig(fl)
    f = lb + (1.0 - lb) * sf
    return qs, sf, f, 1.0 - f, jnp.log(f)


LOG2E = 1.4426950408889634


def _level_factors(g2, qs, k, sign_ref):
    t = lax.broadcasted_iota(jnp.int32, (CHUNK, HEAD), 0)
    row = lambda r, n: jnp.broadcast_to(g2[r:r + 1, :], (n, HEAD))
    out = []
    for l in range(N_LEVELS):
        m = 1 << l
        if l == 0:
            g_mid = jnp.where((t & 1) == 1, pltpu.roll(g2, 1, 0), g2)
        elif l == 1:
            low = (t & 7) < 4
            g_mid = jnp.concatenate([jnp.where(low[:8], row(8 * i + 1, 8), row(8 * i + 5, 8))
                                     for i in range(CHUNK // 8)], axis=0)
        else:
            g_mid = jnp.concatenate([row(b * 2 * m + m - 1, 2 * m) for b in range(CHUNK // (2 * m))], axis=0)
        sgn = sign_ref[l]
        up = sgn > 0.0
        e = jnp.exp2((g2 - g_mid) * sgn)
        x = jnp.where(up, qs, k) * e
        hi = x.astype(BF16)
        out.append((hi, (x - hi.astype(F32)).astype(BF16), e, up))
    return out


CHIP_FLIPS = ((1, 0), (0, 1), (1, 1))
HBM = pl.BlockSpec(memory_space=pl.ANY)


def _place():
    return lax.axis_index("x"), lax.axis_index("y"), lax.axis_index("c")


def _remote(src, dst, send_sems, recv_sems, k, to):
    return pltpu.make_async_remote_copy(src_ref=src, dst_ref=dst, send_sem=send_sems.at[k],
                                        recv_sem=recv_sems.at[k], device_id=to, device_id_type=MESH)


def _half_rows(ref, c):
    half = ref.shape[-2] // 2
    rows = pl.ds(pl.multiple_of(c * half, half), half)
    return ref.at[:, rows, :] if len(ref.shape) == 3 else ref.at[rows, :]


def _other_core_barrier():
    x, y, c = _place()
    sem = pltpu.get_barrier_semaphore()
    pl.semaphore_signal(sem, inc=1, device_id=(x, y, 1 - c), device_id_type=MESH)
    pl.semaphore_wait(sem, 1)


class _Exchange:
    def __init__(self, inputs, out_shapes, n_sems, start, finish, aliases=None):
        self.inputs, self.out_shapes, self.n_sems = list(inputs), list(out_shapes), n_sems
        self.start, self.finish, self.aliases = start, finish, dict(aliases or {})


def _ex_swap(grads):
    def copies(ins, outs, send, recv):
        x, y, c = _place()
        return [_remote(_half_rows(g, 1 - c), o, send, recv, t, (x, y, 1 - c))
                for t, (g, o) in enumerate(zip(ins, outs))]

    def start(*refs):
        for cp in copies(*refs):
            cp.start()

    def finish(*refs):
        cps = copies(*refs)
        for cp in cps:
            cp.wait_recv()
        for cp in cps:
            cp.wait_send()

    shapes = [jax.ShapeDtypeStruct((a.shape[0], a.shape[1] // 2, a.shape[2]), F32) for a in grads]
    return _Exchange(grads, shapes, len(grads), start, finish)


def _ex_send(parts16, owners, units=None, landed=None):
    n_t = len(parts16)
    units = units or [tuple(range(len(o))) for o in owners]
    landed = landed or [None] * n_t
    given = [t for t in range(n_t) if landed[t] is not None]

    def each(ins, outs, send, recv, to_sender, to_owner):
        x, y, c = _place()
        k = 0
        for t, own in enumerate(owners):
            for j in units[t]:
                for r, (fx, fy) in enumerate(CHIP_FLIPS):
                    tx, ty = x ^ fx, y ^ fy
                    cp = _remote(ins[t].at[j], outs[t].at[j, r], send, recv, k, (tx, ty, c))
                    if to_sender is not None:
                        pl.when(2 * tx + ty == own[j])(functools.partial(to_sender, cp))
                    if to_owner is not None:
                        pl.when(2 * x + y == own[j])(functools.partial(to_owner, cp))
                    k += 1

    def start(*refs):
        each(*refs, lambda cp: cp.start(), None)

    def finish(*refs):
        each(*refs, None, lambda cp: cp.wait_recv())
        each(*refs, lambda cp: cp.wait_send(), None)

    shapes = [jax.ShapeDtypeStruct((a.shape[0], len(CHIP_FLIPS)) + a.shape[1:], BF16) for a in parts16]
    return _Exchange(list(parts16) + [landed[t] for t in given], shapes,
                     len(CHIP_FLIPS) * sum(len(u) for u in units), start, finish,
                     aliases={n_t + i: t for i, t in enumerate(given)})


def _ex_join(units, owners):
    def each(ins, outs, send, recv, fn):
        x, y, c = _place()
        k = 0
        for t, own in enumerate(owners):
            for j, o in enumerate(own):
                def half(cc, to, u=outs[t].at[j], k=k):
                    return _remote(_half_rows(u, cc), _half_rows(u, cc), send, recv, k, to)
                mine = functools.partial(half, c, (x, y, 1 - c))
                theirs = functools.partial(half, 1 - c, (x, y, c))
                pl.when(2 * x + y == o)(functools.partial(fn, mine, theirs))
                k += 1

    def start(*refs):
        each(*refs, lambda mine, theirs: mine().start())

    def finish(*refs):
        each(*refs, lambda mine, theirs: theirs().wait_recv())
        each(*refs, lambda mine, theirs: mine().wait_send())

    shapes = [jax.ShapeDtypeStruct(a.shape, F32) for a in units]
    return _Exchange(units, shapes, sum(len(o) for o in owners), start, finish,
                     aliases={t: t for t in range(len(units))})


def _ex_gather(slots):
    n_t = len(slots)
    n_fl = len(CHIP_FLIPS)

    def piece(ref, shard, half):
        return _half_rows(ref.at[shard], half)

    def first(outs, send, recv):
        x, y, c = _place()
        s = 2 * x + y
        return [_remote(piece(outs[t], s, c), piece(outs[t], s, c), send, recv, n_t * j + t, (x ^ fx, y ^ fy, c))
                for j, (fx, fy) in enumerate(CHIP_FLIPS) for t in range(n_t)]

    def start(ins, outs, send, recv):
        for cp in first(outs, send, recv):
            cp.start()

    def finish(ins, outs, send, recv):
        x, y, c = _place()
        passed = []
        for j, (fx, fy) in enumerate(CHIP_FLIPS):
            sj = 2 * (x ^ fx) + (y ^ fy)
            for t in range(n_t):
                k = n_t * j + t
                _remote(piece(outs[t], sj, c), piece(outs[t], sj, c), send, recv, k, (x, y, c)).wait_recv()
                cp = _remote(piece(outs[t], sj, c), piece(outs[t], sj, c), send, recv, n_t * n_fl + k, (x, y, 1 - c))
                cp.start()
                passed.append(cp)
        for j, (fx, fy) in enumerate(CHIP_FLIPS):
            sj = 2 * (x ^ fx) + (y ^ fy)
            for t in range(n_t):
                k = n_t * n_fl + n_t * j + t
                _remote(piece(outs[t], sj, 1 - c), piece(outs[t], sj, 1 - c), send, recv, k, (x, y, c)).wait_recv()
        for cp in first(outs, send, recv) + passed:
            cp.wait_send()

    shapes = [jax.ShapeDtypeStruct(a.shape, BF16) for a in slots]
    return _Exchange(slots, shapes, 2 * n_t * n_fl, start, finish, aliases={t: t for t in range(n_t)})


def _ex_gather_small(parts):
    def copies(ins, outs, send, recv):
        x, y, c = _place()
        me = 4 * x + 2 * y + c
        return [_remote(ins[0], outs[0].at[me], send, recv, mask - 1,
                        (x ^ (mask >> 2), y ^ ((mask >> 1) & 1), c ^ (mask & 1))) for mask in range(1, 8)]

    def start(*refs):
        for cp in copies(*refs):
            cp.start()

    def finish(ins, outs, send, recv):
        x, y, c = _place()
        me = 4 * x + 2 * y + c
        for mask in range(1, 8):
            _remote(ins[0], outs[0].at[me ^ mask], send, recv, mask - 1, (x, y, c)).wait_recv()
        for cp in copies(ins, outs, send, recv):
            cp.wait_send()

    return _Exchange([parts], [jax.ShapeDtypeStruct((8,) + parts.shape, F32)], 7, start, finish)


def _call(body, *, name, args=(), in_specs=(), out_specs=(), out_shape=(), grid=(), scratch_shapes=(),
          exchanges=()):
    n_in, n_out, n_scr = len(args), len(out_shape), len(scratch_shapes)
    ex_in, ex_out, ex_scr, spans, alias = [], [], [], [], {}
    for ex in exchanges:
        spans.append((len(ex_in), len(ex.inputs), len(ex_out), len(ex.out_shapes)))
        for i, o in ex.aliases.items():
            alias[n_in + len(ex_in) + i] = n_out + len(ex_out) + o
        ex_in += ex.inputs
        ex_out += ex.out_shapes
        ex_scr += [pltpu.SemaphoreType.DMA((ex.n_sems,)), pltpu.SemaphoreType.DMA((ex.n_sems,))]

    def full(*refs):
        ins, x_in = refs[:n_in], refs[n_in:n_in + len(ex_in)]
        outs = refs[n_in + len(ex_in):n_in + len(ex_in) + n_out]
        x_out = refs[n_in + len(ex_in) + n_out:n_in + len(ex_in) + n_out + len(ex_out)]
        scr = refs[len(refs) - n_scr - len(ex_scr):len(refs) - len(ex_scr)]
        sems = refs[len(refs) - len(ex_scr):]

        def run(which):
            for e, (ex, (i0, ni, o0, no)) in enumerate(zip(exchanges, spans)):
                getattr(ex, which)(x_in[i0:i0 + ni], x_out[o0:o0 + no], sems[2 * e], sems[2 * e + 1])

        if grid:
            ids = [pl.program_id(a) for a in range(len(grid))]
            is_first = functools.reduce(jnp.logical_and, [i == 0 for i in ids])
            is_last = functools.reduce(jnp.logical_and, [i == g - 1 for i, g in zip(ids, grid)])
            pl.when(is_first)(lambda: run("start"))
            body(*ins, *outs, *scr)
            pl.when(is_last)(lambda: run("finish"))
        else:
            run("start")
            if body is not None:
                body(*ins, *outs, *scr)
            run("finish")

    kw = dict(grid=grid) if grid else {}
    if grid:
        kw["compiler_params"] = _params(dimension_semantics=("arbitrary",) * len(grid))
    else:
        kw["compiler_params"] = _params()
    res = pl.pallas_call(
        full, name=name,
        in_specs=list(in_specs) + [HBM] * len(ex_in),
        out_specs=list(out_specs) + [HBM] * len(ex_out),
        out_shape=list(out_shape) + ex_out,
        scratch_shapes=list(scratch_shapes) + ex_scr,
        input_output_aliases=alias, **kw,
    )(*args, *ex_in)
    own = list(res[:n_out])
    per_ex = [list(res[n_out + o0:n_out + o0 + no]) for (_, _, o0, no) in spans]
    return own, per_ex


def _cast_w_in(w, place):
    rows, cols = w.shape
    tile = rows

    def body(place_ref, w_ref, o_ref):
        o_ref[...] = w_ref[...].astype(BF16)

    return pl.pallas_call(
        body, name="cast_w_in",
        grid_spec=pltpu.PrefetchScalarGridSpec(
            num_scalar_prefetch=1, grid=(cols // COL_BLK, rows // tile),
            in_specs=[pl.BlockSpec((tile, COL_BLK), lambda b, i, p: (i, b))],
            out_specs=pl.BlockSpec((None, None, tile, COL_BLK), lambda b, i, p: (p[1], b, i, 0))),
        out_shape=jax.ShapeDtypeStruct((N_SHARDS, cols // COL_BLK, rows, COL_BLK), BF16),
        compiler_params=_params(dimension_semantics=("arbitrary", "arbitrary")),
    )(place, w)


def _rec_fwd(proj, lb_logits, rec_g, consts, exchanges):
    tri, low, sign = consts["tri"], consts["low"], consts["sign"]

    def body(q_ref, f_ref, i_ref, rg_ref, lb_ref, g_ref, w_ref, low_ref, sign_ref, y_ref, o_ref, stp_ref, st_ref):
        @pl.when(pl.program_id(1) == 0)
        def _():
            st_ref[...] = jnp.zeros_like(st_ref)
        lb = _lower_bound(lb_ref)
        st = st_ref[...]
        rows = lambda c: pl.ds(c * CHUNK, CHUNK)
        for c0 in range(0, REC_CHUNKS, REC_GROUP):
            group = range(c0, c0 + REC_GROUP)
            gated = [_gates(q_ref[rows(c), :], f_ref[rows(c), :], lb) for c in group]
            g2s = [_dot3(w_ref[...], g) * LOG2E for (_, _, _, _, g) in gated]
            xs = [[xl for xl, _, _, _ in _level_factors(g2, qs, k, sign_ref)]
                  for g2, (qs, _, _, k, _) in zip(g2s, gated)]
            a_s = []
            for x in xs:
                a = jnp.zeros((CHUNK, CHUNK), F32)
                for l, xl in enumerate(x):
                    a = a + _dot_nt(xl, xl) * low_ref[l]
                a_s.append(a.astype(BF16))
            vbs = [i_ref[rows(c), :].astype(BF16) for c in group]
            intra = [_dot(a, vb) for a, vb in zip(a_s, vbs)]
            kvs = [_dot_tn(vb, (k * jnp.exp2(g2[CHUNK - 1:CHUNK, :] - g2)).astype(BF16))
                   for vb, g2, (_, _, _, k, _) in zip(vbs, g2s, gated)]
            for i, c in enumerate(group):
                qs, _, _, k, _ = gated[i]
                g2 = g2s[i]
                stp_ref[c] = st
                v = i_ref[rows(c), :]
                rg = rg_ref[rows(c), :]
                o = (intra[i] + jnp.sum(qs * k, axis=-1, keepdims=True) * v
                     + _dot_nt((qs * jnp.exp2(g2)).astype(BF16), st.astype(BF16)))
                st = st * jnp.exp2(g2[CHUNK - 1:CHUNK, :]) + kvs[i]
                o_ref[rows(c), :] = o
                inv = lax.rsqrt(jnp.mean(o * o, axis=-1, keepdims=True) + EPS)
                y_ref[rows(c), :] = (o * inv * g_ref[...] * (rg * _sig(rg))).astype(BF16)
        st_ref[...] = st

    sec = lambda n: _proj_cols(HEAD, n, lambda h, b: (h, b), REC_ROWS)
    vec = lambda rows: pl.BlockSpec((rows, HEAD), lambda h, b: (0, h))
    full = lambda a: pl.BlockSpec(a.shape, lambda h, b: (0,) * a.ndim)
    return _call(
        body, name="rec_fwd", grid=(N_HEADS, N_REC_BLK),
        args=(proj, proj, proj, proj, lb_logits, rec_g, tri, low, sign),
        in_specs=[sec(2), sec(3), sec(4), sec(5), vec(2), vec(1), full(tri), full(low), full(sign)],
        out_specs=[pl.BlockSpec((REC_ROWS, HEAD), lambda h, b: (b, h)),
                   pl.BlockSpec((REC_ROWS, HEAD), lambda h, b: (b, h)),
                   pl.BlockSpec((None, REC_CHUNKS, HEAD, HEAD), lambda h, b: (h, b, 0, 0))],
        out_shape=[jax.ShapeDtypeStruct((SEQ, D_MODEL), BF16),
                   jax.ShapeDtypeStruct((SEQ, D_MODEL), F32),
                   jax.ShapeDtypeStruct((N_HEADS, SEQ // CHUNK, HEAD, HEAD), F32)],
        scratch_shapes=[pltpu.VMEM((HEAD, HEAD), F32)],
        exchanges=exchanges)


OUT_ROWS = 512


def _out_proj_loss(y_pool, y_rec, w_out_g, x, target, gf):
    def body(yp_ref, yr_ref, w_ref, x_ref, t_ref, gf_ref, dout_ref, doutb_ref, part_ref):
        @pl.when(pl.program_id(0) == 0)
        def _():
            part_ref[...] = jnp.zeros_like(part_ref)
        halves = [pl.ds(a * (OUT_ROWS // 2), OUT_ROWS // 2) for a in range(2)]
        outs = [x_ref[r, :] + _dot(yp_ref[r, :], w_ref[0:D_MODEL, :])
                + _dot(yr_ref[r, :], w_ref[D_MODEL:2 * D_MODEL, :]) for r in halves]
        gf_v = gf_ref[...]
        for r, out in zip(halves, outs):
            inv = lax.rsqrt(jnp.mean(out * out, axis=-1, keepdims=True) + EPS)
            diff = out * inv * gf_v - t_ref[r, :]
            dyf = diff * (1.0 / D_MODEL)
            a = dyf * gf_v
            dout = inv * a - out * (inv * inv * inv) * jnp.mean(a * out, axis=-1, keepdims=True)
            dout_ref[r, :] = dout
            doutb_ref[r, :] = dout.astype(BF16)
            part_ref[0:1, :] += jnp.sum(dyf * out * inv, axis=0, keepdims=True)
            part_ref[1:2, :] += jnp.sum(diff * diff, axis=0, keepdims=True)

    row = lambda n: pl.BlockSpec((OUT_ROWS, n), lambda i: (i, 0))
    return pl.pallas_call(
        body, name="out_proj_loss", grid=(SEQ // OUT_ROWS,),
        in_specs=[row(D_MODEL), row(D_MODEL), pl.BlockSpec((2 * D_MODEL, D_MODEL), lambda i: (0, 0)),
                  row(D_MODEL), row(D_MODEL), pl.BlockSpec((1, D_MODEL), lambda i: (0, 0))],
        out_specs=[row(D_MODEL), row(D_MODEL), pl.BlockSpec((8, D_MODEL), lambda i: (0, 0))],
        out_shape=[jax.ShapeDtypeStruct((SEQ, D_MODEL), F32),
                   jax.ShapeDtypeStruct((SEQ, D_MODEL), BF16),
                   jax.ShapeDtypeStruct((8, D_MODEL), F32)],
        compiler_params=_params(dimension_semantics=("arbitrary",)),
    )(y_pool, y_rec, w_out_g, x, target, gf)


def _grad_w_out(y_pool, y_rec, dout_b):
    blk = W_OUT_SHARD // 2
    per = D_MODEL // blk
    n = 2 * per

    def body(yp_ref, yr_ref, d_ref, p32_ref, p16_ref, send_ref, recv_ref, send_sems, recv_sems):
        j = pl.program_id(0)
        x, y, c = _place()

        def copy(u):
            return _remote(send_ref.at[u], recv_ref.at[u], send_sems, recv_sems, u, (x, y, 1 - c))

        pl.when(j == 0)(_other_core_barrier)
        for i in range(n):
            @pl.when(j == i)
            def _(i=i):
                res = _dot_tn((yp_ref if i < per else yr_ref)[...], d_ref[...])

                @pl.when(i % 2 == c)
                def _():
                    p32_ref[i // 2] = res

                @pl.when(i % 2 != c)
                def _():
                    send_ref[i // 2] = res
                    copy(i // 2).start()

        @pl.when(j == n - 1)
        def _():
            for u in range(N_SHARDS):
                copy(u).wait_recv()
                tot = p32_ref[u] + recv_ref[u]
                p32_ref[u] = tot
                p16_ref[u] = tot.astype(BF16)
            for u in range(N_SHARDS):
                copy(u).wait_send()

    whole = pl.BlockSpec((N_SHARDS, blk, D_MODEL), lambda j: (0, 0, 0))
    return pl.pallas_call(
        body, name="grad_w_out", grid=(n,),
        in_specs=[pl.BlockSpec((SEQ, blk), lambda j: (0, jnp.minimum(j, per - 1))),
                  pl.BlockSpec((SEQ, blk), lambda j: (0, jnp.maximum(j - per, 0))),
                  pl.BlockSpec((SEQ, D_MODEL), lambda j: (0, 0))],
        out_specs=[whole, whole],
        out_shape=[jax.ShapeDtypeStruct((N_SHARDS, blk, D_MODEL), F32),
                   jax.ShapeDtypeStruct((N_SHARDS, blk, D_MODEL), BF16)],
        scratch_shapes=[pltpu.VMEM((N_SHARDS, blk, D_MODEL), F32), pltpu.VMEM((N_SHARDS, blk, D_MODEL), F32),
                        pltpu.SemaphoreType.DMA((N_SHARDS,)), pltpu.SemaphoreType.DMA((N_SHARDS,))],
        compiler_params=_params(dimension_semantics=("arbitrary",), collective_id=0),
    )(y_pool, y_rec, dout_b)


def _pool_bwd(proj, dout_b, w_out_g, pw_g, pool_scale, exchanges):
    n = POOL_ROWS + POOL_HALO

    def body(u_ref, gate_ref, d_ref, wo_ref, pw_ref, sc_ref,
             dp_ref, dpw_ref, dsc_ref, dd_ref, ddw_ref):
        g = pl.program_id(0)
        dpw_ref[...] = jnp.zeros_like(dpw_ref)
        dsc_ref[...] = jnp.zeros_like(dsc_ref)

        def first(ii, _):
            chunks = [POOL_PAIR * ii + a for a in range(POOL_PAIR)]
            rs = [pl.ds(pl.multiple_of(i * POOL_ROWS, POOL_ROWS), POOL_ROWS) for i in chunks]
            diffs = [_pool_diff(u_ref, i, g) for i in chunks]
            dbs = [d.astype(BF16) for d, _ in diffs]
            mixed = [_dot(db, pw_ref[...]) for db in dbs]
            dys = [_dot_nt(d_ref[r, :], wo_ref[...]) for r in rs]
            sc = sc_ref[...]
            dmbs = []
            for r, m, dy in zip(rs, mixed, dys):
                gate = gate_ref[r, :]
                sg = _sig(gate)
                silu = gate * sg
                dp_ref[1, r, :] = (dy * m * sc * (sg * (1.0 + gate * (1.0 - sg)))).astype(BF16)
                dsc_ref[...] += jnp.sum(dy * silu * m, axis=0, keepdims=True)
                dmbs.append((dy * silu * sc).astype(BF16))
            for db, dmb in zip(dbs, dmbs):
                dpw_ref[...] += _dot_tn(db, dmb)
            dds = [_dot_nt(dmb, pw_ref[...]) for dmb in dmbs]
            for r, dd, (_, inv_count) in zip(rs, dds, diffs):
                dd_ref[r, :] = dd
                ddw_ref[r, :] = dd * inv_count
            return 0
        lax.fori_loop(0, SEQ // POOL_ROWS // POOL_PAIR, first, 0)

        def second(i, _):
            r0 = i * POOL_ROWS
            r = pl.ds(pl.multiple_of(r0, POOL_ROWS), POOL_ROWS)
            last = i == SEQ // POOL_ROWS - 1
            after = ddw_ref[pl.ds(pl.multiple_of(jnp.minimum(r0 + POOL_ROWS, SEQ - POOL_HALO), 8), POOL_HALO), :]
            after = jnp.where(last, 0.0, after)
            ext = jnp.concatenate([ddw_ref[r, :], after], axis=0)
            s = _window_sums(ext, g, lambda k: n - (1 << k))[:POOL_ROWS, :]
            dp_ref[0, r, :] = (s - dd_ref[r, :]).astype(BF16)
            return 0
        lax.fori_loop(0, SEQ // POOL_ROWS, second, 0)

    return _call(
        body, name="pool_bwd", grid=(N_GROUPS,),
        args=(proj, proj, dout_b, w_out_g, pw_g, pool_scale),
        in_specs=[_proj_cols(PG, 0, lambda g: (g, 0)), _proj_cols(PG, 1, lambda g: (g, 0)),
                  pl.BlockSpec((SEQ, D_MODEL), lambda g: (0, 0)),
                  pl.BlockSpec((PG, D_MODEL), lambda g: (g, 0)),
                  pl.BlockSpec((None, PG, PG), lambda g: (g, 0, 0)),
                  pl.BlockSpec((1, PG), lambda g: (0, g))],
        out_specs=[pl.BlockSpec((2, SEQ, PG), lambda g: (0, 0, g)),
                   pl.BlockSpec((None, PG, PG), lambda g: (g, 0, 0)),
                   pl.BlockSpec((1, PG), lambda g: (0, g))],
        out_shape=[jax.ShapeDtypeStruct((2, SEQ, D_MODEL), BF16),
                   jax.ShapeDtypeStruct((N_GROUPS, PG, PG), F32),
                   jax.ShapeDtypeStruct((1, D_MODEL), F32)],
        scratch_shapes=[pltpu.VMEM((SEQ, PG), F32), pltpu.VMEM((SEQ, PG), F32)],
        exchanges=exchanges)


HALF_HEADS = N_HEADS // 2
HALF_COLS = HALF_HEADS * HEAD


def _rec_bwd(proj, o_raw, st_prev, dout_b, w_out_g, lb_logits, rec_g, consts, h0, name, exchanges,
             gate_of=None, own_gate=True):
    n_sec = 4 if gate_of is None else 5

    def body(q_ref, f_ref, i_ref, rg_ref, o_ref, stp_ref, d_ref, wo_ref, lb_ref, g_ref,
             w_ref, lowt_ref, sym_ref, sign_ref, tri_ref, *rest):
        dr_ref, part_ref, dst_ref = rest[-3:]

        @pl.when(pl.program_id(1) == 0)
        def _():
            dst_ref[...] = jnp.zeros_like(dst_ref)
            part_ref[...] = jnp.zeros_like(part_ref)
        if gate_of is not None:
            rg2_ref, o2_ref, wo2_ref, g2_ref = rest[:4]
            rg, o = rg2_ref[...], o2_ref[...]
            sg = _sig(rg)
            inv = lax.rsqrt(jnp.mean(o * o, axis=-1, keepdims=True) + EPS)
            dy = _dot_nt(d_ref[...], wo2_ref[...])
            dr_ref[4] = (dy * (o * inv) * g2_ref[...] * (sg * (1.0 + rg * (1.0 - sg)))).astype(BF16)
        tril = (lax.broadcasted_iota(jnp.int32, (CHUNK, CHUNK), 0)
                > lax.broadcasted_iota(jnp.int32, (CHUNK, CHUNK), 1))
        lb = _lower_bound(lb_ref)
        grec = g_ref[...]
        dst = dst_ref[...]
        acc_grec = jnp.zeros((1, HEAD), F32)
        acc_lb = jnp.zeros((1, HEAD), F32)
        rows = lambda c: pl.ds(c * CHUNK, CHUNK)
        for c0 in reversed(range(0, REC_CHUNKS, REC_GROUP_BWD)):
            group = list(reversed(range(c0, c0 + REC_GROUP_BWD)))
            dys = [_dot_nt(d_ref[rows(c), :], wo_ref[...]) for c in group]
            dos = []
            for c, dy in zip(group, dys):
                rg = rg_ref[rows(c), :]
                o = o_ref[rows(c), :]
                sg = _sig(rg)
                silu = rg * sg
                inv = lax.rsqrt(jnp.mean(o * o, axis=-1, keepdims=True) + EPS)
                recn = o * inv
                if own_gate:
                    dr_ref[3, rows(c), :] = (dy * recn * grec * (sg * (1.0 + rg * (1.0 - sg)))).astype(BF16)
                acc_grec = acc_grec + jnp.sum(dy * silu * recn, axis=0, keepdims=True)
                drecn = dy * silu * grec
                dos.append(inv * drecn - o * (inv * inv * inv) * jnp.mean(drecn * o, axis=-1, keepdims=True))
            gated = [_gates(q_ref[rows(c), :], f_ref[rows(c), :], lb) for c in group]
            g2s = [_dot3(w_ref[...], g) * LOG2E for (_, _, _, _, g) in gated]
            levels = [_level_factors(g2, qs, k, sign_ref) for g2, (qs, _, _, k, _) in zip(g2s, gated)]
            a_ts = []
            for lev in levels:
                a_t = jnp.zeros((CHUNK, CHUNK), F32)
                for l, (xl, _, _, _) in enumerate(lev):
                    a_t = a_t + _dot_nt(xl, xl) * lowt_ref[l]
                a_ts.append(a_t.astype(BF16))
            dobs = [do.astype(BF16) for do in dos]
            vbs = [i_ref[rows(c), :].astype(BF16) for c in group]
            d_syms = [jnp.where(tril, _dot_nt(dob, vb), _dot_nt(vb, dob)) for dob, vb in zip(dobs, vbs)]
            dqs_is, dk_is = [], []
            for lev, d_sym in zip(levels, d_syms):
                dqs_i = jnp.zeros((CHUNK, HEAD), F32)
                both_i = jnp.zeros((CHUNK, HEAD), F32)
                for l, (xl, xlo, e, up) in enumerate(lev):
                    z = d_sym * sym_ref[l]
                    tmp = _dot(z.astype(BF16), jnp.concatenate([xl, xlo], axis=-1))
                    tmp = (tmp[:, :HEAD] + tmp[:, HEAD:]) * e
                    dqs_i = dqs_i + jnp.where(up, tmp, 0.0)
                    both_i = both_i + tmp
                dqs_is.append(dqs_i)
                dk_is.append(both_i - dqs_i)
            e_gs = [jnp.exp2(g2) for g2 in g2s]
            e_revs = [jnp.exp2(g2[CHUNK - 1:CHUNK, :] - g2) for g2 in g2s]
            e_lasts = [jnp.exp2(g2[CHUNK - 1:CHUNK, :]) for g2 in g2s]
            q_gs = [qs * e_g for (qs, _, _, _, _), e_g in zip(gated, e_gs)]
            kdecs = [k * e_rev for (_, _, _, k, _), e_rev in zip(gated, e_revs)]
            dv12 = [_dot(a_t, dob) + jnp.sum(qs * k, axis=-1, keepdims=True) * do
                    for a_t, dob, do, (qs, _, _, k, _) in zip(a_ts, dobs, dos, gated)]
            dq_gs = [_dot(dob, stp_ref[c].astype(BF16)) for c, dob in zip(group, dobs)]
            steps = [_dot_tn(dob, q_g.astype(BF16)) for dob, q_g in zip(dobs, q_gs)]
            dsts = []
            for e_last, step in zip(e_lasts, steps):
                dsts.append(dst)
                dst = dst * e_last + step
            dstbs = [d.astype(BF16) for d in dsts]
            dv3 = [_dot_nt(kdec.astype(BF16), dstb) for kdec, dstb in zip(kdecs, dstbs)]
            dkdecs = [_dot(vb, dstb) for vb, dstb in zip(vbs, dstbs)]
            dbig_gs, dg_lasts, dqss, dks = [], [], [], []
            for i, c in enumerate(group):
                qs, _, _, k, _ = gated[i]
                de_last = jnp.sum(stp_ref[c] * dsts[i], axis=0, keepdims=True)
                ddiag = jnp.sum(dos[i] * i_ref[rows(c), :], axis=-1, keepdims=True)
                dqss.append(dqs_is[i] + ddiag * k + dq_gs[i] * e_gs[i])
                dks.append(dk_is[i] + ddiag * qs + dkdecs[i] * e_revs[i])
                dg_rev = dkdecs[i] * kdecs[i]
                dg_lasts.append(jnp.sum(dg_rev, axis=0, keepdims=True) + de_last * e_lasts[i])
                dbig_gs.append(qs * dqs_is[i] - k * dk_is[i] + dq_gs[i] * q_gs[i] - dg_rev)
            dgs = [_dot3(tri_ref[...], dbig_g) + dg_last for dbig_g, dg_last in zip(dbig_gs, dg_lasts)]
            for i, c in enumerate(group):
                _, sf, f, _, _ = gated[i]
                q = q_ref[rows(c), :]
                df = dgs[i] / f - dks[i]
                dr_ref[1, rows(c), :] = (df * (1.0 - lb) * sf * (1.0 - sf)).astype(BF16)
                acc_lb = acc_lb + jnp.sum(df * (1.0 - sf), axis=0, keepdims=True)
                sq = _sig(q)
                dr_ref[0, rows(c), :] = (dqss[i] * (sq * (1.0 + q * (1.0 - sq)))).astype(BF16)
                dr_ref[2, rows(c), :] = (dv12[i] + dv3[i]).astype(BF16)
        dst_ref[...] = dst
        part_ref[0:1, :] += acc_grec
        part_ref[1:2, :] += acc_lb

    rev = lambda b: N_REC_BLK - 1 - b
    sec = lambda n: _proj_cols(HEAD, n, lambda h, b: (h0 + h, rev(b)), REC_ROWS)
    col_in = pl.BlockSpec((REC_ROWS, HEAD), lambda h, b: (rev(b), h0 + h))
    vec_in = lambda rows: pl.BlockSpec((rows, HEAD), lambda h, b: (0, h0 + h))
    full = lambda a: pl.BlockSpec(a.shape, lambda h, b: (0,) * a.ndim)
    extra_args, extra_specs = (), []
    if gate_of is not None:
        extra_args = (proj, o_raw, w_out_g, rec_g)
        extra_specs = [_proj_cols(HEAD, 5, lambda h, b: (gate_of + h, rev(b)), REC_ROWS),
                       pl.BlockSpec((REC_ROWS, HEAD), lambda h, b: (rev(b), gate_of + h)),
                       pl.BlockSpec((HEAD, D_MODEL), lambda h, b: (SEC_BLK + gate_of + h, 0)),
                       pl.BlockSpec((1, HEAD), lambda h, b: (0, gate_of + h))]
    return _call(
        body, name=name, grid=(HALF_HEADS, N_REC_BLK),
        args=(proj, proj, proj, proj, o_raw, st_prev, dout_b, w_out_g, lb_logits, rec_g,
              consts["tri"], consts["low_t"], consts["sym"], consts["sign"], consts["tri_t"]) + extra_args,
        in_specs=[sec(2), sec(3), sec(4), sec(5), col_in,
                  pl.BlockSpec((None, REC_CHUNKS, HEAD, HEAD), lambda h, b: (h0 + h, rev(b), 0, 0)),
                  pl.BlockSpec((REC_ROWS, D_MODEL), lambda h, b: (rev(b), 0)),
                  pl.BlockSpec((HEAD, D_MODEL), lambda h, b: (SEC_BLK + h0 + h, 0)),
                  vec_in(2), vec_in(1)] + [full(consts[n]) for n in ("tri", "low_t", "sym", "sign", "tri_t")]
                 + extra_specs,
        out_specs=[pl.BlockSpec((n_sec, REC_ROWS, HEAD), lambda h, b: (0, rev(b), h)),
                   pl.BlockSpec((8, HEAD), lambda h, b: (0, h))],
        out_shape=[jax.ShapeDtypeStruct((n_sec, SEQ, HALF_COLS), BF16),
                   jax.ShapeDtypeStruct((8, HALF_COLS), F32)],
        scratch_shapes=[pltpu.VMEM((HEAD, HEAD), F32)],
        exchanges=exchanges)


def _w_in_block(w_ref, j):
    per_shard = W_IN_SHARD // COL_BLK
    return w_ref[j // per_shard, j % per_shard]


def _grad_x(dproj, w_in_g, x, g1, dout, exchanges):
    rows = 256
    n_blk = len(dproj)

    def body(*refs):
        dp_refs = refs[:n_blk]
        w_ref, x_ref, g_ref, dout_ref, dx_ref, part_ref = refs[n_blk:]

        @pl.when(pl.program_id(0) == 0)
        def _():
            part_ref[...] = jnp.zeros_like(part_ref)
        dh = jnp.zeros((rows, D_MODEL), F32)
        for j in range(n_blk):
            dh = dh + _dot_nt(dp_refs[j][...], _w_in_block(w_ref, j))
        xv = x_ref[...]
        inv = lax.rsqrt(jnp.mean(xv * xv, axis=-1, keepdims=True) + EPS)
        a = dh * g_ref[...]
        dx_ref[...] = (dout_ref[...] + inv * a
                       - xv * (inv * inv * inv) * jnp.mean(a * xv, axis=-1, keepdims=True))
        part_ref[0:1, :] += jnp.sum(dh * xv * inv, axis=0, keepdims=True)

    row = lambda: pl.BlockSpec((rows, D_MODEL), lambda i: (i, 0))
    dp_spec = lambda sec, cb: pl.BlockSpec((None, rows, COL_BLK), lambda i: (sec, i, cb))
    return _call(
        body, name="grad_x", grid=(SEQ // rows,),
        args=tuple(a for a, _, _ in dproj) + (w_in_g, x, g1, dout),
        in_specs=[dp_spec(sec, cb) for _, sec, cb in dproj]
                 + [pl.BlockSpec(w_in_g.shape, lambda i: (0, 0, 0, 0)),
                    row(), pl.BlockSpec((1, D_MODEL), lambda i: (0, 0)), row()],
        out_specs=[row(), pl.BlockSpec((8, D_MODEL), lambda i: (0, 0))],
        out_shape=[jax.ShapeDtypeStruct((SEQ, D_MODEL), F32),
                   jax.ShapeDtypeStruct((8, D_MODEL), F32)],
        exchanges=exchanges)


def _grad_w_in(h, dp, blocks, name, collective_id):
    n_blk = len(blocks)
    half = D_MODEL // 2
    pick = lambda vals: (lambda j: functools.reduce(lambda acc, iv: jnp.where(j == iv[0], iv[1], acc),
                                                     list(enumerate(vals))[1:], vals[0]))
    sec_of = pick([sec for sec, _ in blocks])
    cb_of = pick([cb for _, cb in blocks])

    def body(h_ref, dp_ref, p32_ref, p16_ref, keep_ref, send_ref, recv_ref, send_sems, recv_sems):
        j = pl.program_id(0)
        x, y, c = _place()
        cols = lambda cc: pl.ds(pl.multiple_of(cc * half, half), half)

        def copy(i):
            return _remote(send_ref.at[i], recv_ref.at[i], send_sems, recv_sems, i, (x, y, 1 - c))

        pl.when(j == 0)(_other_core_barrier)
        for i in range(n_blk + 1):
            @pl.when(j == i)
            def _(i=i):
                if i < n_blk:
                    send_ref[i] = _dot_tn(h_ref[:, cols(1 - c)], dp_ref[...])
                    copy(i).start()
                    keep_ref[i] = _dot_tn(h_ref[:, cols(c)], dp_ref[...])
                if i > 0:
                    copy(i - 1).wait_recv()
                    tot = keep_ref[i - 1] + recv_ref[i - 1]
                    p32_ref[...] = tot
                    p16_ref[...] = tot.astype(BF16)

        @pl.when(j == n_blk)
        def _():
            for i in range(n_blk):
                copy(i).wait_send()

    lagged = pl.BlockSpec((None, half, COL_BLK), lambda j: (jnp.maximum(j - 1, 0), 0, 0))
    last = n_blk - 1
    return pl.pallas_call(
        body, name=name, grid=(n_blk + 1,),
        in_specs=[pl.BlockSpec((SEQ, D_MODEL), lambda j: (0, 0)),
                  pl.BlockSpec((None, SEQ, COL_BLK),
                               lambda j: (sec_of(jnp.minimum(j, last)), 0, cb_of(jnp.minimum(j, last))))],
        out_specs=[lagged, lagged],
        out_shape=[jax.ShapeDtypeStruct((n_blk, half, COL_BLK), F32),
                   jax.ShapeDtypeStruct((n_blk, half, COL_BLK), BF16)],
        scratch_shapes=[pltpu.VMEM((n_blk, half, COL_BLK), F32)] * 3
                       + [pltpu.SemaphoreType.DMA((n_blk,)), pltpu.SemaphoreType.DMA((n_blk,))],
        compiler_params=_params(dimension_semantics=("arbitrary",), collective_id=collective_id),
    )(h, dp)


def _add_units(grad, recv, place, tile, name):
    n, rows, cols = grad.shape
    per_half = rows // 2 // tile

    def body(place_ref, g_ref, r_ref, o32_ref, o16_ref):
        v = g_ref[...] + r_ref[...]
        o32_ref[...] = v
        o16_ref[...] = v.astype(BF16)

    blk = lambda f: pl.BlockSpec((None, tile, cols), f)
    out = lambda s, i, p: (s, i, 0)
    return pl.pallas_call(
        body, name=name,
        grid_spec=pltpu.PrefetchScalarGridSpec(
            num_scalar_prefetch=1, grid=(n, per_half),
            in_specs=[blk(lambda s, i, p: (s, p[0] * per_half + i, 0)), blk(out)],
            out_specs=[blk(out), blk(out)]),
        out_shape=[jax.ShapeDtypeStruct(recv.shape, F32), jax.ShapeDtypeStruct(recv.shape, BF16)],
        compiler_params=_params(dimension_semantics=("arbitrary", "arbitrary")),
    )(place, grad, recv)


def _sum_units(part32, recv16, place, owners, tile, name):
    n, half, cols = part32.shape
    per_half = half // tile
    table = np.array([[sum(o == chip for o in owners)] + sorted(range(n), key=lambda j: (owners[j] != chip, j))
                      for chip in range(N_SHARDS)], np.int32)
    sched = jnp.concatenate([place[:1], jnp.asarray(table)[place[1]]])

    def block(k, i, p):
        live = k < p[1]
        unit = p[2 + jnp.minimum(k, jnp.maximum(p[1] - 1, 0))]
        return unit, jnp.where(live, i, per_half - 1)

    def body(sched_ref, p_ref, r_ref, o_ref):
        @pl.when(pl.program_id(0) < sched_ref[1])
        def _():
            acc = p_ref[...]
            for j in range(len(CHIP_FLIPS)):
                acc = acc + r_ref[j].astype(F32)
            o_ref[...] = acc

    return pl.pallas_call(
        body, name=name,
        grid_spec=pltpu.PrefetchScalarGridSpec(
            num_scalar_prefetch=1, grid=(n, per_half),
            in_specs=[pl.BlockSpec((None, tile, cols), lambda k, i, p: (*block(k, i, p), 0)),
                      pl.BlockSpec((None, len(CHIP_FLIPS), tile, cols),
                                   lambda k, i, p: (block(k, i, p)[0], 0, block(k, i, p)[1], 0))],
            out_specs=pl.BlockSpec((None, tile, cols),
                                   lambda k, i, p: (block(k, i, p)[0], p[0] * per_half + block(k, i, p)[1], 0))),
        out_shape=jax.ShapeDtypeStruct((n, 2 * half, cols), F32),
        compiler_params=_params(dimension_semantics=("arbitrary", "arbitrary")),
    )(sched, part32, recv16)


def _adamw_math(w, g, m, v):
    m = ADAM_B1 * m + (1.0 - ADAM_B1) * g
    v = ADAM_B2 * v + (1.0 - ADAM_B2) * (g * g)
    m_hat = m / (1.0 - ADAM_B1 ** ADAM_STEP)
    v_hat = v / (1.0 - ADAM_B2 ** ADAM_STEP)
    delta = -ADAM_LR * (m_hat / (jnp.sqrt(v_hat) + ADAM_EPS) + ADAM_WD * w)
    return delta, m, v


def _adamw_units(w, m, v, grads, pick, name):
    rows, cols = w.shape
    bc = grads[0].shape[-1]
    tile = min(rows, 256)
    n_g = len(grads)

    def body(pick_ref, w_ref, m_ref, v_ref, *refs):
        g_refs, (g_out, d_ref, nm_ref, nv_ref) = refs[:n_g], refs[n_g:]
        p = pl.program_id(0)
        for a in range(n_g):
            @pl.when(pick_ref[0, p] == a)
            def _(a=a):
                g = g_refs[a][...]
                g_out[...] = g
                d_ref[...], nm_ref[...], nv_ref[...] = _adamw_math(w_ref[...], g, m_ref[...], v_ref[...])

    blk = pl.BlockSpec((tile, bc), lambda p, i, pick: (i, p))

    def g_spec(a):
        return pl.BlockSpec((None, tile, bc),
                            lambda p, i, pick: (jnp.where(pick[0, p] == a, pick[1, p], 0),
                                                jnp.where(pick[0, p] == a, i, 0), 0))

    return pl.pallas_call(
        body, name=name,
        grid_spec=pltpu.PrefetchScalarGridSpec(
            num_scalar_prefetch=1, grid=(cols // bc, rows // tile),
            in_specs=[blk] * 3 + [g_spec(a) for a in range(n_g)],
            out_specs=[blk] * 4),
        out_shape=[jax.ShapeDtypeStruct(w.shape, F32)] * 4,
        compiler_params=_params(dimension_semantics=("arbitrary", "arbitrary")),
    )(pick, w, m, v, *grads)


ROW_NORM1, ROW_SCALE, ROW_LB, ROW_REC, ROW_FINAL, ROW_LOSS = 0, 1, 2, 4, 5, 6


SMALL_ROWS = (ROW_NORM1, ROW_SCALE, ROW_LB, ROW_REC, ROW_FINAL)


def _small_update(parts, gathered, params):
    n_p = len(params)

    def body(own_ref, p_ref, *refs):
        ins, loss_ref, outs = refs[:3 * n_p], refs[3 * n_p], refs[3 * n_p + 1:]
        x, y, c = _place()
        me = 4 * x + 2 * y + c
        slot = lambda d: jnp.where(me == d, own_ref[...], p_ref[d])
        tot = slot(0)
        for d in range(1, 8):
            tot = tot + slot(d)
        for i, r in enumerate(SMALL_ROWS):
            w = ins[3 * i][...]
            g = tot[r:r + 1, :]
            if r == ROW_LB:
                mx = jnp.maximum(w[0:1, :], w[1:2, :])
                e0 = jnp.exp(w[0:1, :] - mx)
                e1 = jnp.exp(w[1:2, :] - mx)
                lb = e0 / (e0 + e1)
                g = g * lb * (1.0 - lb)
                g = jnp.concatenate([g, -g], axis=0)
            outs[4 * i][...] = g
            outs[4 * i + 1][...], outs[4 * i + 2][...], outs[4 * i + 3][...] = _adamw_math(
                w, g, ins[3 * i + 1][...], ins[3 * i + 2][...])
        loss_ref[...] = (0.5 / D_MODEL) * jnp.sum(tot[ROW_LOSS:ROW_LOSS + 1, :], axis=-1, keepdims=True)

    flat = [a for wmv in params for a in wmv]
    return pl.pallas_call(
        body, name="small_update",
        out_shape=[jax.ShapeDtypeStruct((1, 1), F32)]
                  + [jax.ShapeDtypeStruct(w.shape, F32) for w, _, _ in params for _ in range(4)],
        compiler_params=_params(),
    )(parts, gathered, *flat)


SHARD_OWNERS = tuple(range(N_SHARDS))
BLOCKS_POOL = (0, 1, 2, 3)
BLOCKS_A = (4, 6, 8, 10, 11)
BLOCKS_B = (5, 7, 9)
BLOCK_GROUPS = (BLOCKS_POOL, BLOCKS_A, BLOCKS_B)


def _block_owners(blocks):
    return tuple(j // (W_IN_SHARD // COL_BLK) for j in blocks)


def kernel(x, norm1_g, w_in, pool_w, pool_scale, lb_logits, rec_norm_g, w_out, final_norm_g, loss_target, m_norm1_g, m_w_in, m_pool_w, m_pool_scale, m_lb_logits, m_rec_norm_g, m_w_out, m_final_norm_g, v_norm1_g, v_w_in, v_pool_w, v_pool_scale, v_lb_logits, v_rec_norm_g, v_w_out, v_final_norm_g):
    xi, yi, ci = _place()
    chip = 2 * xi + yi
    place = jnp.stack([ci, chip]).astype(jnp.int32)
    pw_rows = N_GROUPS * PW_SHARD
    flat_pw = lambda a: a.reshape(pw_rows, PG)
    x2, target, gf = x[0], loss_target[0], final_norm_g.reshape(1, D_MODEL)
    consts = {n: jnp.asarray(a, BF16 if n.startswith("tri") else F32) for n, a in _chunk_constants().items()}

    proj, h, w_in_g, w_out_slots, pw_slots = _in_proj(x2, norm1_g, _cast_w_in(w_in[0], place), place,
                                                      [w_out[0], flat_pw(pool_w)])
    (y_rec, o_raw, st_prev), ((w_out_g, pw_g),) = _rec_fwd(
        proj, lb_logits, rec_norm_g, consts, [_ex_gather([w_out_slots, pw_slots])])
    w_out_g = w_out_g.reshape(2 * D_MODEL, D_MODEL)
    pw_full = pw_g.reshape(N_SHARDS, N_GROUPS, PW_SHARD, PG).transpose(1, 0, 2, 3).reshape(N_GROUPS, PG, PG)
    y_pool = _pool_fwd(proj, pw_full, pool_scale)
    dout, dout_b, part_out = _out_proj_loss(y_pool, y_rec, w_out_g, x2, target, gf)

    p_out32, p_out16 = _grad_w_out(y_pool, y_rec, dout_b)
    (dpool, gpw, dscale), ((rb_out,),) = _pool_bwd(proj, dout_b, w_out_g, pw_full, pool_scale,
                                                   [_ex_send([p_out16], [SHARD_OWNERS])])
    g_out = _sum_units(p_out32, rb_out, place, SHARD_OWNERS, 256, "sum_w_out")
    gpw = gpw.reshape(N_GROUPS, N_SHARDS, PW_SHARD, PG).transpose(1, 0, 2, 3).reshape(N_SHARDS, pw_rows, PG)
    p_inp32, p_inp16 = _grad_w_in(h, dpool, [(0, 0), (0, 1), (1, 0), (1, 1)], "grad_w_in_pool", 1)

    pool_owners, a_owners, b_owners = (_block_owners(b) for b in BLOCK_GROUPS)
    rec_args = (proj, o_raw, st_prev, dout_b, w_out_g, lb_logits, rec_norm_g, consts)
    (drec_a, part_a), ((rb_inp,), (ra_pw,)) = _rec_bwd(
        *rec_args, 0, "rec_bwd_a", [_ex_send([p_inp16], [pool_owners], units=[(0, 1)]), _ex_swap([gpw])],
        gate_of=HALF_HEADS)
    p_pw32, p_pw16 = _add_units(gpw, ra_pw, place, 128, "add_pool_w")
    p_ina32, p_ina16 = _grad_w_in(h, drec_a, [(n, 0) for n in range(5)], "grad_w_in_a", 2)

    (drec_b, part_b), ((rb_inp, rb_ina, rb_pw),) = _rec_bwd(
        *rec_args, HALF_HEADS, "rec_bwd_b",
        [_ex_send([p_inp16, p_ina16, p_pw16], [pool_owners, a_owners, SHARD_OWNERS],
                  units=[(2, 3), tuple(range(len(a_owners))), SHARD_OWNERS], landed=[rb_inp, None, None])],
        own_gate=False)
    g_inp = _sum_units(p_inp32, rb_inp, place, pool_owners, 256, "sum_w_in_pool")
    g_ina = _sum_units(p_ina32, rb_ina, place, a_owners, 256, "sum_w_in_a")
    g_pw = _sum_units(p_pw32, rb_pw, place, SHARD_OWNERS, 128, "sum_pool_w")
    p_inb32, p_inb16 = _grad_w_in(h, drec_b, [(n, 0) for n in range(3)], "grad_w_in_b", 3)

    dproj = ([(dpool, 0, 0), (dpool, 0, 1), (dpool, 1, 0), (dpool, 1, 1)]
             + [(d, n, 0) for n in range(3) for d in (drec_a, drec_b)] + [(drec_a, 3, 0), (drec_a, 4, 0)])
    (dx, part_x), ((rb_inb,),) = _grad_x(dproj, w_in_g, x2, norm1_g, dout, [_ex_send([p_inb16], [b_owners])])
    g_inb = _sum_units(p_inb32, rb_inb, place, b_owners, 256, "sum_w_in_b")
    zero = jnp.zeros((1, D_MODEL), F32)
    part_rec = jnp.concatenate([part_a, part_b], axis=1)
    parts = jnp.concatenate([part_x[0:1], dscale, part_rec[1:2], zero, part_rec[0:1], part_out[0:1],
                             part_out[1:2], zero], axis=0)
    _, ((g_out, g_pw, g_inp, g_ina, g_inb), (gathered,)) = _call(
        None, name="join_halves",
        exchanges=[_ex_join([g_out, g_pw, g_inp, g_ina, g_inb],
                            [SHARD_OWNERS, SHARD_OWNERS, pool_owners, a_owners, b_owners]),
                   _ex_gather_small(parts)])

    group_of = np.zeros((D_PROJ // COL_BLK,), np.int32)
    index_of = np.zeros((D_PROJ // COL_BLK,), np.int32)
    for gi, blocks in enumerate(BLOCK_GROUPS):
        for i, j in enumerate(blocks):
            group_of[j], index_of[j] = gi, i
    per_shard = W_IN_SHARD // COL_BLK
    pick_in = jnp.stack([lax.dynamic_slice(jnp.asarray(group_of), (per_shard * chip,), (per_shard,)),
                         lax.dynamic_slice(jnp.asarray(index_of), (per_shard * chip,), (per_shard,))])
    pick_own = jnp.stack([jnp.zeros((1,), jnp.int32), chip.reshape(1).astype(jnp.int32)])
    big = [_adamw_units(w_in[0], m_w_in[0], v_w_in[0], [g_inp, g_ina, g_inb], pick_in, "adamw_w_in"),
           _adamw_units(w_out[0], m_w_out[0], v_w_out[0], [g_out], pick_own, "adamw_w_out"),
           _adamw_units(flat_pw(pool_w), flat_pw(m_pool_w), flat_pw(v_pool_w), [g_pw], pick_own, "adamw_pool_w")]

    row = lambda a: a.reshape(1, D_MODEL)
    loss, *small = _small_update(parts, gathered, [
        (norm1_g, m_norm1_g, v_norm1_g), (pool_scale, m_pool_scale, v_pool_scale),
        (lb_logits, m_lb_logits, v_lb_logits), (rec_norm_g, m_rec_norm_g, v_rec_norm_g),
        (row(final_norm_g), row(m_final_norm_g), row(v_final_norm_g))])

    def leaves(k):
        norm1, scale, lb, rec, final = (small[4 * i + k] for i in range(len(SMALL_ROWS)))
        return (norm1, big[0][k][None], big[2][k].reshape(pool_w.shape), scale, lb, rec,
                big[1][k][None], final.reshape(D_MODEL))

    return (loss.reshape(()), dx[None], *leaves(0), *leaves(1), *leaves(2), *leaves(3))
```

```python
import functools

import numpy as np
import jax
import jax.numpy as jnp
from jax import lax
from jax.experimental import pallas as pl
from jax.experimental.pallas import tpu as pltpu

F32 = jnp.float32
BF16 = jnp.bfloat16

SEQ = 2048
D_MODEL = 1024
D_PROJ = 6144
N_SEC = 6
N_GROUPS = 4
PG = 256
N_HEADS = 8
HEAD = 128
CHUNK = 64
N_LEVELS = 6
N_SHARDS = 4
W_IN_SHARD = D_PROJ // N_SHARDS
W_OUT_SHARD = 2048 // N_SHARDS
PW_SHARD = PG // N_SHARDS
COL_BLK = 512
EPS = 1e-6

ADAM_LR = 0.001
ADAM_B1 = 0.9
ADAM_B2 = 0.999
ADAM_EPS = 1e-08
ADAM_WD = 0.01
ADAM_STEP = 10

V7X_VMEM_LIMIT = 56 * 1024 * 1024
MESH = pl.DeviceIdType.MESH


def _params(**kw):
    return pltpu.CompilerParams(vmem_limit_bytes=V7X_VMEM_LIMIT, **kw)


def _sig(x):
    return 1.0 / (1.0 + jnp.exp(-x))


def _dot(a, b):
    return jnp.dot(a, b, preferred_element_type=F32)


def _dot_nt(a, b):
    return lax.dot_general(a, b, (((1,), (1,)), ((), ())), preferred_element_type=F32)


def _dot_tn(a, b):
    return lax.dot_general(a, b, (((0,), (0,)), ((), ())), preferred_element_type=F32)


def _split3(a):
    p1 = a.astype(BF16)
    r1 = a - p1.astype(F32)
    p2 = r1.astype(BF16)
    p3 = (r1 - p2.astype(F32)).astype(BF16)
    return jnp.concatenate([p1, p2, p3], axis=-1)


def _dot3(w01, a):
    n = a.shape[-1]
    r = _dot(w01, _split3(a))
    return r[:, :n] + r[:, n:2 * n] + r[:, 2 * n:]


def _chunk_constants():
    j = np.arange(CHUNK)
    tt, ss = np.meshgrid(j, j, indexing="ij")
    x = tt ^ ss
    hb = np.full((CHUNK, CHUNK), -1, np.int32)
    for l in range(N_LEVELS):
        hb[x >= (1 << l)] = l
    sym = np.stack([(hb == l) for l in range(N_LEVELS)]).astype(np.float32)
    low = sym * (tt > ss)
    sign = np.stack([np.where((j >> l) & 1, 1.0, -1.0) for l in range(N_LEVELS)]).astype(np.float32)
    sign = np.ascontiguousarray(np.broadcast_to(sign[:, :, None], (N_LEVELS, CHUNK, HEAD)))
    tri = (ss <= tt).astype(np.float32)
    return dict(tri=tri, tri_t=np.ascontiguousarray(tri.T), low=low,
                low_t=np.ascontiguousarray(low.transpose(0, 2, 1)), sym=sym, sign=sign)


def _in_proj(x, g1, w_slots, place, also_cast):
    n_col = D_PROJ // COL_BLK
    per_shard = W_IN_SHARD // COL_BLK
    rows = 1024
    half_rows = D_MODEL // 2
    quarter_rows = D_MODEL // 4
    FLIPS = (0, 2, 1, 3)
    ORDER = ([(0, p) for p in range(per_shard)] + [(m, p) for p in range(per_shard) for m in (1, 2)]
             + [(3, p) for p in range(per_shard)])

    def shard_at(m, chip):
        return chip ^ FLIPS[m]

    def pick(vals, t):
        return functools.reduce(lambda acc, iv: jnp.where(t == iv[0], iv[1], acc), list(enumerate(vals))[1:], vals[0])

    def body(place_ref, x_ref, g_ref, w_in_ref, *rest):
        others, (proj_ref, h_ref, w_ref) = rest[:len(also_cast)], rest[len(also_cast):len(also_cast) + 3]
        slots = rest[len(also_cast) + 3:2 * len(also_cast) + 3]
        wbuf, load_sems, send_sems, recv_sems = rest[2 * len(also_cast) + 3:]
        t = pl.program_id(0)

        @pl.when(t == 1)
        def _():
            for src, dst in zip(others, slots):
                dst[...] = src[...].astype(BF16)
        x_, y_, c = _place()
        chip = 2 * x_ + y_
        me, other_core = (x_, y_, c), (x_, y_, 1 - c)
        x_nbr, y_nbr = (1 - x_, y_, c), (x_, 1 - y_, c)

        def rows_of(half, q=None):
            if q is None:
                return pl.ds(pl.multiple_of(half * half_rows, half_rows), half_rows)
            return pl.ds(pl.multiple_of(half * half_rows + q * quarter_rows, quarter_rows), quarter_rows)

        def block(m, p, r):
            return w_ref.at[shard_at(m, chip), p, r, :]

        def copy(k, ref, to):
            return _remote(ref, ref, send_sems, recv_sems, k, to)

        direct = lambda n, p, to: copy(3 * n + p, block(0, p, rows_of(c)), to)
        relay = lambda n, p, to: copy(6 + 3 * n + p, block(1 + n, p, rows_of(c, n)), to)
        arrived = lambda m, p: ([copy(3 * (m - 1) + p, block(m, p, rows_of(c)), me)] if m < 3 else
                                [copy(6 + 3 * n + p, block(3, p, rows_of(c, n)), me) for n in (0, 1)])
        passed_on = lambda m, p, half, to: copy(9 + 3 * m + p, block(m, p, rows_of(half)), to)

        def load(m, p, slot):
            return pltpu.make_async_copy(w_ref.at[shard_at(m, chip), p], wbuf.at[slot], load_sems.at[slot])

        def prepare(m, p):
            for cp in arrived(m, p):
                cp.wait_recv()
            passed_on(m, p, c, other_core).start()
            if m < 3:
                relay(m - 1, p, y_nbr if m == 1 else x_nbr).start()

        @pl.when(t == 0)
        def _():
            for p in range(per_shard):
                direct(0, p, x_nbr).start()
                direct(1, p, y_nbr).start()
            for p in range(per_shard):
                load(0, p, p).start()

            def norm(i, _):
                r = pl.ds(pl.multiple_of(i * rows, rows), rows)
                xv = x_ref[r, :]
                inv = lax.rsqrt(jnp.mean(xv * xv, axis=-1, keepdims=True) + EPS)
                h_ref[r, :] = (xv * inv * g_ref[...]).astype(BF16)
                return 0
            lax.fori_loop(0, SEQ // rows, norm, 0)

        for step, (m, p) in enumerate(ORDER):
            @pl.when(t == step)
            def _(step=step, m=m, p=p):
                slot = step % per_shard
                if m > 0:
                    passed_on(m, p, 1 - c, me).wait_recv()
                    load(m, p, slot).start()
                if step + 1 < n_col and ORDER[step + 1][0] > 0:
                    prepare(*ORDER[step + 1])
                load(m, p, slot).wait()

                def mm(i, _):
                    r = pl.ds(pl.multiple_of(i * rows, rows), rows)
                    proj_ref[r, :] = _dot(h_ref[r, :], wbuf[slot])
                    return 0
                lax.fori_loop(0, SEQ // rows, mm, 0)

        @pl.when(t == n_col - 1)
        def _():
            for p in range(per_shard):
                sent = [direct(0, p, x_nbr), direct(1, p, y_nbr), relay(0, p, y_nbr), relay(1, p, x_nbr)]
                for cp in sent + [passed_on(m, p, c, other_core) for m in (1, 2, 3)]:
                    cp.wait_send()

    return pl.pallas_call(
        body, name="in_proj",
        grid_spec=pltpu.PrefetchScalarGridSpec(
            num_scalar_prefetch=1, grid=(n_col,),
            in_specs=[pl.BlockSpec((SEQ, D_MODEL), lambda t, p: (0, 0)),
                      pl.BlockSpec((1, D_MODEL), lambda t, p: (0, 0)),
                      pl.BlockSpec(memory_space=pl.ANY)]
                     + [pl.BlockSpec(a.shape, lambda t, p: (0, 0)) for a in also_cast],
            out_specs=[pl.BlockSpec((None, SEQ, COL_BLK),
                                    lambda t, p: (per_shard * (p[1] ^ pick([FLIPS[m] for m, _ in ORDER], t))
                                                  + pick([b for _, b in ORDER], t), 0, 0)),
                       pl.BlockSpec((SEQ, D_MODEL), lambda t, p: (0, 0)),
                       pl.BlockSpec(memory_space=pl.ANY)]
                      + [pl.BlockSpec((None,) + a.shape, lambda t, p: (p[1], 0, 0)) for a in also_cast],
            scratch_shapes=[pltpu.VMEM((per_shard, D_MODEL, COL_BLK), BF16),
                            pltpu.SemaphoreType.DMA((per_shard,)),
                            pltpu.SemaphoreType.DMA((21,)), pltpu.SemaphoreType.DMA((21,))]),
        out_shape=[jax.ShapeDtypeStruct((n_col, SEQ, COL_BLK), F32),
                   jax.ShapeDtypeStruct((SEQ, D_MODEL), BF16),
                   jax.ShapeDtypeStruct(w_slots.shape, BF16)]
                  + [jax.ShapeDtypeStruct((N_SHARDS,) + a.shape, BF16) for a in also_cast],
        input_output_aliases={3: 2},
        compiler_params=_params(dimension_semantics=("arbitrary",)),
    )(place, x, g1, w_slots, *also_cast)


def _proj_cols(width, section, where, rows=SEQ):
    per_blk = COL_BLK // width

    def index(*grid):
        k, r = where(*grid)
        return section * (D_MODEL // COL_BLK) + k // per_blk, r, k % per_blk

    return pl.BlockSpec((None, rows, width), index)


POOL_ROWS = 256
POOL_HALO = 16
POOL_PAIR = 2


def _window_sums(ext, g, shift_of):
    s = ext
    for k in range(N_GROUPS):
        s = jnp.where(k <= g, s + pltpu.roll(s, shift_of(k), 0), s)
    return s


def _pool_diff(u_ref, i, g):
    n = POOL_ROWS + POOL_HALO
    r0 = i * POOL_ROWS
    cur = u_ref[pl.ds(pl.multiple_of(r0, POOL_ROWS), POOL_ROWS), :]
    before = u_ref[pl.ds(pl.multiple_of(jnp.maximum(r0 - POOL_HALO, 0), 8), POOL_HALO), :]
    before = jnp.where(i > 0, before, 0.0)
    ext = jnp.concatenate([before, cur], axis=0)
    s = _window_sums(ext, g, lambda k: 1 << k)[POOL_HALO:, :]
    t = r0 + lax.broadcasted_iota(jnp.int32, (POOL_ROWS, 1), 0)
    width = (2 << g).astype(F32)
    inv_count = 1.0 / jnp.minimum((t + 1).astype(F32), width)
    return s * inv_count - cur, inv_count


def _pool_fwd(proj, pw_g, pool_scale):
    def body(u_ref, gate_ref, pw_ref, sc_ref, y_ref):
        g = pl.program_id(0)

        def step(ii, _):
            chunks = [POOL_PAIR * ii + a for a in range(POOL_PAIR)]
            ds = [_pool_diff(u_ref, i, g)[0].astype(BF16) for i in chunks]
            mixed = [_dot(d, pw_ref[...]) for d in ds]
            for i, m in zip(chunks, mixed):
                r = pl.ds(pl.multiple_of(i * POOL_ROWS, POOL_ROWS), POOL_ROWS)
                gate = gate_ref[r, :]
                y_ref[r, :] = (m * sc_ref[...] * (gate * _sig(gate))).astype(BF16)
            return 0
        lax.fori_loop(0, SEQ // POOL_ROWS // POOL_PAIR, step, 0)

    return pl.pallas_call(
        body, name="pool_fwd", grid=(N_GROUPS,),
        in_specs=[_proj_cols(PG, 0, lambda g: (g, 0)), _proj_cols(PG, 1, lambda g: (g, 0)),
                  pl.BlockSpec((None, PG, PG), lambda g: (g, 0, 0)),
                  pl.BlockSpec((1, PG), lambda g: (0, g))],
        out_specs=pl.BlockSpec((SEQ, PG), lambda g: (0, g)),
        out_shape=jax.ShapeDtypeStruct((SEQ, D_MODEL), BF16),
        compiler_params=_params(dimension_semantics=("arbitrary",)),
    )(proj, proj, pw_g, pool_scale)


REC_ROWS = 1024
REC_CHUNKS = REC_ROWS // CHUNK
N_REC_BLK = SEQ // REC_ROWS
REC_GROUP = REC_CHUNKS
REC_GROUP_BWD = REC_CHUNKS
SEC_BLK = D_MODEL // HEAD


def _lower_bound(lb_ref):
    l0 = lb_ref[0:1, :]
    l1 = lb_ref[1:2, :]
    mx = jnp.maximum(l0, l1)
    e0 = jnp.exp(l0 - mx)
    e1 = jnp.exp(l1 - mx)
    return e0 / (e0 + e1)


def _gates(q, fl, lb):
    qs = q * _sig(q)
    sf = _sig(fl)
    f = lb + (1.0 - lb) * sf
    return qs, sf, f, 1.0 - f, jnp.log(f)


LOG2E = 1.4426950408889634


def _level_factors(g2, qs, k, sign_ref):
    t = lax.broadcasted_iota(jnp.int32, (CHUNK, HEAD), 0)
    row = lambda r, n: jnp.broadcast_to(g2[r:r + 1, :], (n, HEAD))
    out = []
    for l in range(N_LEVELS):
        m = 1 << l
        if l == 0:
            g_mid = jnp.where((t & 1) == 1, pltpu.roll(g2, 1, 0), g2)
        elif l == 1:
            low = (t & 7) < 4
            g_mid = jnp.concatenate([jnp.where(low[:8], row(8 * i + 1, 8), row(8 * i + 5, 8))
                                     for i in range(CHUNK // 8)], axis=0)
        else:
            g_mid = jnp.concatenate([row(b * 2 * m + m - 1, 2 * m) for b in range(CHUNK // (2 * m))], axis=0)
        sgn = sign_ref[l]
        up = sgn > 0.0
        e = jnp.exp2((g2 - g_mid) * sgn)
        x = jnp.where(up, qs, k) * e
        hi = x.astype(BF16)
        out.append((hi, (x - hi.astype(F32)).astype(BF16), e, up))
    return out


CHIP_FLIPS = ((1, 0), (0, 1), (1, 1))
HBM = pl.BlockSpec(memory_space=pl.ANY)


def _place():
    return lax.axis_index("x"), lax.axis_index("y"), lax.axis_index("c")


def _remote(src, dst, send_sems, recv_sems, k, to):
    return pltpu.make_async_remote_copy(src_ref=src, dst_ref=dst, send_sem=send_sems.at[k],
                                        recv_sem=recv_sems.at[k], device_id=to, device_id_type=MESH)


def _half_rows(ref, c):
    half = ref.shape[-2] // 2
    rows = pl.ds(pl.multiple_of(c * half, half), half)
    return ref.at[:, rows, :] if len(ref.shape) == 3 else ref.at[rows, :]


def _other_core_barrier():
    x, y, c = _place()
    sem = pltpu.get_barrier_semaphore()
    pl.semaphore_signal(sem, inc=1, device_id=(x, y, 1 - c), device_id_type=MESH)
    pl.semaphore_wait(sem, 1)


class _Exchange:
    def __init__(self, inputs, out_shapes, n_sems, start, finish, aliases=None):
        self.inputs, self.out_shapes, self.n_sems = list(inputs), list(out_shapes), n_sems
        self.start, self.finish, self.aliases = start, finish, dict(aliases or {})


def _ex_swap(grads):
    def copies(ins, outs, send, recv):
        x, y, c = _place()
        return [_remote(_half_rows(g, 1 - c), o, send, recv, t, (x, y, 1 - c))
                for t, (g, o) in enumerate(zip(ins, outs))]

    def start(*refs):
        for cp in copies(*refs):
            cp.start()

    def finish(*refs):
        cps = copies(*refs)
        for cp in cps:
            cp.wait_recv()
        for cp in cps:
            cp.wait_send()

    shapes = [jax.ShapeDtypeStruct((a.shape[0], a.shape[1] // 2, a.shape[2]), F32) for a in grads]
    return _Exchange(grads, shapes, len(grads), start, finish)


def _ex_send(parts16, owners, units=None, landed=None):
    n_t = len(parts16)
    units = units or [tuple(range(len(o))) for o in owners]
    landed = landed or [None] * n_t
    given = [t for t in range(n_t) if landed[t] is not None]

    def each(ins, outs, send, recv, to_sender, to_owner):
        x, y, c = _place()
        k = 0
        for t, own in enumerate(owners):
            for j in units[t]:
                for r, (fx, fy) in enumerate(CHIP_FLIPS):
                    tx, ty = x ^ fx, y ^ fy
                    cp = _remote(ins[t].at[j], outs[t].at[j, r], send, recv, k, (tx, ty, c))
                    if to_sender is not None:
                        pl.when(2 * tx + ty == own[j])(functools.partial(to_sender, cp))
                    if to_owner is not None:
                        pl.when(2 * x + y == own[j])(functools.partial(to_owner, cp))
                    k += 1

    def start(*refs):
        each(*refs, lambda cp: cp.start(), None)

    def finish(*refs):
        each(*refs, None, lambda cp: cp.wait_recv())
        each(*refs, lambda cp: cp.wait_send(), None)

    shapes = [jax.ShapeDtypeStruct((a.shape[0], len(CHIP_FLIPS)) + a.shape[1:], BF16) for a in parts16]
    return _Exchange(list(parts16) + [landed[t] for t in given], shapes,
                     len(CHIP_FLIPS) * sum(len(u) for u in units), start, finish,
                     aliases={n_t + i: t for i, t in enumerate(given)})


def _ex_join(units, owners):
    def each(ins, outs, send, recv, fn):
        x, y, c = _place()
        k = 0
        for t, own in enumerate(owners):
            for j, o in enumerate(own):
                def half(cc, to, u=outs[t].at[j], k=k):
                    return _remote(_half_rows(u, cc), _half_rows(u, cc), send, recv, k, to)
                mine = functools.partial(half, c, (x, y, 1 - c))
                theirs = functools.partial(half, 1 - c, (x, y, c))
                pl.when(2 * x + y == o)(functools.partial(fn, mine, theirs))
                k += 1

    def start(*refs):
        each(*refs, lambda mine, theirs: mine().start())

    def finish(*refs):
        each(*refs, lambda mine, theirs: theirs().wait_recv())
        each(*refs, lambda mine, theirs: mine().wait_send())

    shapes = [jax.ShapeDtypeStruct(a.shape, F32) for a in units]
    return _Exchange(units, shapes, sum(len(o) for o in owners), start, finish,
                     aliases={t: t for t in range(len(units))})


def _ex_gather(slots):
    n_t = len(slots)
    n_fl = len(CHIP_FLIPS)

    def piece(ref, shard, half):
        return _half_rows(ref.at[shard], half)

    def first(outs, send, recv):
        x, y, c = _place()
        s = 2 * x + y
        return [_remote(piece(outs[t], s, c), piece(outs[t], s, c), send, recv, n_t * j + t, (x ^ fx, y ^ fy, c))
                for j, (fx, fy) in enumerate(CHIP_FLIPS) for t in range(n_t)]

    def start(ins, outs, send, recv):
        for cp in first(outs, send, recv):
            cp.start()

    def finish(ins, outs, send, recv):
        x, y, c = _place()
        passed = []
        for j, (fx, fy) in enumerate(CHIP_FLIPS):
            sj = 2 * (x ^ fx) + (y ^ fy)
            for t in range(n_t):
                k = n_t * j + t
                _remote(piece(outs[t], sj, c), piece(outs[t], sj, c), send, recv, k, (x, y, c)).wait_recv()
                cp = _remote(piece(outs[t], sj, c), piece(outs[t], sj, c), send, recv, n_t * n_fl + k, (x, y, 1 - c))
                cp.start()
                passed.append(cp)
        for j, (fx, fy) in enumerate(CHIP_FLIPS):
            sj = 2 * (x ^ fx) + (y ^ fy)
            for t in range(n_t):
                k = n_t * n_fl + n_t * j + t
                _remote(piece(outs[t], sj, 1 - c), piece(outs[t], sj, 1 - c), send, recv, k, (x, y, c)).wait_recv()
        for cp in first(outs, send, recv) + passed:
            cp.wait_send()

    shapes = [jax.ShapeDtypeStruct(a.shape, BF16) for a in slots]
    return _Exchange(slots, shapes, 2 * n_t * n_fl, start, finish, aliases={t: t for t in range(n_t)})


def _ex_gather_small(parts):
    def copies(ins, outs, send, recv):
        x, y, c = _place()
        me = 4 * x + 2 * y + c
        return [_remote(ins[0], outs[0].at[me], send, recv, mask - 1,
                        (x ^ (mask >> 2), y ^ ((mask >> 1) & 1), c ^ (mask & 1))) for mask in range(1, 8)]

    def start(*refs):
        for cp in copies(*refs):
            cp.start()

    def finish(ins, outs, send, recv):
        x, y, c = _place()
        me = 4 * x + 2 * y + c
        for mask in range(1, 8):
            _remote(ins[0], outs[0].at[me ^ mask], send, recv, mask - 1, (x, y, c)).wait_recv()
        for cp in copies(ins, outs, send, recv):
            cp.wait_send()

    return _Exchange([parts], [jax.ShapeDtypeStruct((8,) + parts.shape, F32)], 7, start, finish)


def _call(body, *, name, args=(), in_specs=(), out_specs=(), out_shape=(), grid=(), scratch_shapes=(),
          exchanges=()):
    n_in, n_out, n_scr = len(args), len(out_shape), len(scratch_shapes)
    ex_in, ex_out, ex_scr, spans, alias = [], [], [], [], {}
    for ex in exchanges:
        spans.append((len(ex_in), len(ex.inputs), len(ex_out), len(ex.out_shapes)))
        for i, o in ex.aliases.items():
            alias[n_in + len(ex_in) + i] = n_out + len(ex_out) + o
        ex_in += ex.inputs
        ex_out += ex.out_shapes
        ex_scr += [pltpu.SemaphoreType.DMA((ex.n_sems,)), pltpu.SemaphoreType.DMA((ex.n_sems,))]

    def full(*refs):
        ins, x_in = refs[:n_in], refs[n_in:n_in + len(ex_in)]
        outs = refs[n_in + len(ex_in):n_in + len(ex_in) + n_out]
        x_out = refs[n_in + len(ex_in) + n_out:n_in + len(ex_in) + n_out + len(ex_out)]
        scr = refs[len(refs) - n_scr - len(ex_scr):len(refs) - len(ex_scr)]
        sems = refs[len(refs) - len(ex_scr):]

        def run(which):
            for e, (ex, (i0, ni, o0, no)) in enumerate(zip(exchanges, spans)):
                getattr(ex, which)(x_in[i0:i0 + ni], x_out[o0:o0 + no], sems[2 * e], sems[2 * e + 1])

        if grid:
            ids = [pl.program_id(a) for a in range(len(grid))]
            is_first = functools.reduce(jnp.logical_and, [i == 0 for i in ids])
            is_last = functools.reduce(jnp.logical_and, [i == g - 1 for i, g in zip(ids, grid)])
            pl.when(is_first)(lambda: run("start"))
            body(*ins, *outs, *scr)
            pl.when(is_last)(lambda: run("finish"))
        else:
            run("start")
            if body is not None:
                body(*ins, *outs, *scr)
            run("finish")

    kw = dict(grid=grid) if grid else {}
    if grid:
        kw["compiler_params"] = _params(dimension_semantics=("arbitrary",) * len(grid))
    else:
        kw["compiler_params"] = _params()
    res = pl.pallas_call(
        full, name=name,
        in_specs=list(in_specs) + [HBM] * len(ex_in),
        out_specs=list(out_specs) + [HBM] * len(ex_out),
        out_shape=list(out_shape) + ex_out,
        scratch_shapes=list(scratch_shapes) + ex_scr,
        input_output_aliases=alias, **kw,
    )(*args, *ex_in)
    own = list(res[:n_out])
    per_ex = [list(res[n_out + o0:n_out + o0 + no]) for (_, _, o0, no) in spans]
    return own, per_ex


def _cast_w_in(w, place):
    rows, cols = w.shape
    tile = rows

    def body(place_ref, w_ref, o_ref):
        o_ref[...] = w_ref[...].astype(BF16)

    return pl.pallas_call(
        body, name="cast_w_in",
        grid_spec=pltpu.PrefetchScalarGridSpec(
            num_scalar_prefetch=1, grid=(cols // COL_BLK, rows // tile),
            in_specs=[pl.BlockSpec((tile, COL_BLK), lambda b, i, p: (i, b))],
            out_specs=pl.BlockSpec((None, None, tile, COL_BLK), lambda b, i, p: (p[1], b, i, 0))),
        out_shape=jax.ShapeDtypeStruct((N_SHARDS, cols // COL_BLK, rows, COL_BLK), BF16),
        compiler_params=_params(dimension_semantics=("arbitrary", "arbitrary")),
    )(place, w)


def _rec_fwd(proj, lb_logits, rec_g, consts, exchanges):
    tri, low, sign = consts["tri"], consts["low"], consts["sign"]

    def body(q_ref, f_ref, i_ref, rg_ref, lb_ref, g_ref, w_ref, low_ref, sign_ref, y_ref, o_ref, stp_ref, st_ref):
        @pl.when(pl.program_id(1) == 0)
        def _():
            st_ref[...] = jnp.zeros_like(st_ref)
        lb = _lower_bound(lb_ref)
        st = st_ref[...]
        rows = lambda c: pl.ds(c * CHUNK, CHUNK)
        for c0 in range(0, REC_CHUNKS, REC_GROUP):
            group = range(c0, c0 + REC_GROUP)
            gated = [_gates(q_ref[rows(c), :], f_ref[rows(c), :], lb) for c in group]
            g2s = [_dot3(w_ref[...], g) * LOG2E for (_, _, _, _, g) in gated]
            xs = [[xl for xl, _, _, _ in _level_factors(g2, qs, k, sign_ref)]
                  for g2, (qs, _, _, k, _) in zip(g2s, gated)]
            a_s = []
            for x in xs:
                a = jnp.zeros((CHUNK, CHUNK), F32)
                for l, xl in enumerate(x):
                    a = a + _dot_nt(xl, xl) * low_ref[l]
                a_s.append(a.astype(BF16))
            vbs = [i_ref[rows(c), :].astype(BF16) for c in group]
            intra = [_dot(a, vb) for a, vb in zip(a_s, vbs)]
            kvs = [_dot_tn(vb, (k * jnp.exp2(g2[CHUNK - 1:CHUNK, :] - g2)).astype(BF16))
                   for vb, g2, (_, _, _, k, _) in zip(vbs, g2s, gated)]
            for i, c in enumerate(group):
                qs, _, _, k, _ = gated[i]
                g2 = g2s[i]
                stp_ref[c] = st
                v = i_ref[rows(c), :]
                rg = rg_ref[rows(c), :]
                o = (intra[i] + jnp.sum(qs * k, axis=-1, keepdims=True) * v
                     + _dot_nt((qs * jnp.exp2(g2)).astype(BF16), st.astype(BF16)))
                st = st * jnp.exp2(g2[CHUNK - 1:CHUNK, :]) + kvs[i]
                o_ref[rows(c), :] = o
                inv = lax.rsqrt(jnp.mean(o * o, axis=-1, keepdims=True) + EPS)
                y_ref[rows(c), :] = (o * inv * g_ref[...] * (rg * _sig(rg))).astype(BF16)
        st_ref[...] = st

    sec = lambda n: _proj_cols(HEAD, n, lambda h, b: (h, b), REC_ROWS)
    vec = lambda rows: pl.BlockSpec((rows, HEAD), lambda h, b: (0, h))
    full = lambda a: pl.BlockSpec(a.shape, lambda h, b: (0,) * a.ndim)
    return _call(
        body, name="rec_fwd", grid=(N_HEADS, N_REC_BLK),
        args=(proj, proj, proj, proj, lb_logits, rec_g, tri, low, sign),
        in_specs=[sec(2), sec(3), sec(4), sec(5), vec(2), vec(1), full(tri), full(low), full(sign)],
        out_specs=[pl.BlockSpec((REC_ROWS, HEAD), lambda h, b: (b, h)),
                   pl.BlockSpec((REC_ROWS, HEAD), lambda h, b: (b, h)),
                   pl.BlockSpec((None, REC_CHUNKS, HEAD, HEAD), lambda h, b: (h, b, 0, 0))],
        out_shape=[jax.ShapeDtypeStruct((SEQ, D_MODEL), BF16),
                   jax.ShapeDtypeStruct((SEQ, D_MODEL), F32),
                   jax.ShapeDtypeStruct((N_HEADS, SEQ // CHUNK, HEAD, HEAD), F32)],
        scratch_shapes=[pltpu.VMEM((HEAD, HEAD), F32)],
        exchanges=exchanges)


OUT_ROWS = 512


def _out_proj_loss(y_pool, y_rec, w_out_g, x, target, gf):
    def body(yp_ref, yr_ref, w_ref, x_ref, t_ref, gf_ref, dout_ref, doutb_ref, part_ref):
        @pl.when(pl.program_id(0) == 0)
        def _():
            part_ref[...] = jnp.zeros_like(part_ref)
        halves = [pl.ds(a * (OUT_ROWS // 2), OUT_ROWS // 2) for a in range(2)]
        outs = [x_ref[r, :] + _dot(yp_ref[r, :], w_ref[0:D_MODEL, :])
                + _dot(yr_ref[r, :], w_ref[D_MODEL:2 * D_MODEL, :]) for r in halves]
        gf_v = gf_ref[...]
        for r, out in zip(halves, outs):
            inv = lax.rsqrt(jnp.mean(out * out, axis=-1, keepdims=True) + EPS)
            diff = out * inv * gf_v - t_ref[r, :]
            dyf = diff * (1.0 / D_MODEL)
            a = dyf * gf_v
            dout = inv * a - out * (inv * inv * inv) * jnp.mean(a * out, axis=-1, keepdims=True)
            dout_ref[r, :] = dout
            doutb_ref[r, :] = dout.astype(BF16)
            part_ref[0:1, :] += jnp.sum(dyf * out * inv, axis=0, keepdims=True)
            part_ref[1:2, :] += jnp.sum(diff * diff, axis=0, keepdims=True)

    row = lambda n: pl.BlockSpec((OUT_ROWS, n), lambda i: (i, 0))
    return pl.pallas_call(
        body, name="out_proj_loss", grid=(SEQ // OUT_ROWS,),
        in_specs=[row(D_MODEL), row(D_MODEL), pl.BlockSpec((2 * D_MODEL, D_MODEL), lambda i: (0, 0)),
                  row(D_MODEL), row(D_MODEL), pl.BlockSpec((1, D_MODEL), lambda i: (0, 0))],
        out_specs=[row(D_MODEL), row(D_MODEL), pl.BlockSpec((8, D_MODEL), lambda i: (0, 0))],
        out_shape=[jax.ShapeDtypeStruct((SEQ, D_MODEL), F32),
                   jax.ShapeDtypeStruct((SEQ, D_MODEL), BF16),
                   jax.ShapeDtypeStruct((8, D_MODEL), F32)],
        compiler_params=_params(dimension_semantics=("arbitrary",)),
    )(y_pool, y_rec, w_out_g, x, target, gf)


def _grad_w_out(y_pool, y_rec, dout_b):
    blk = W_OUT_SHARD // 2
    per = D_MODEL // blk
    n = 2 * per

    def body(yp_ref, yr_ref, d_ref, p32_ref, p16_ref, send_ref, recv_ref, send_sems, recv_sems):
        j = pl.program_id(0)
        x, y, c = _place()

        def copy(u):
            return _remote(send_ref.at[u], recv_ref.at[u], send_sems, recv_sems, u, (x, y, 1 - c))

        pl.when(j == 0)(_other_core_barrier)
        for i in range(n):
            @pl.when(j == i)
            def _(i=i):
                res = _dot_tn((yp_ref if i < per else yr_ref)[...], d_ref[...])

                @pl.when(i % 2 == c)
                def _():
                    p32_ref[i // 2] = res

                @pl.when(i % 2 != c)
                def _():
                    send_ref[i // 2] = res
                    copy(i // 2).start()

        @pl.when(j == n - 1)
        def _():
            for u in range(N_SHARDS):
                copy(u).wait_recv()
                tot = p32_ref[u] + recv_ref[u]
                p32_ref[u] = tot
                p16_ref[u] = tot.astype(BF16)
            for u in range(N_SHARDS):
                copy(u).wait_send()

    whole = pl.BlockSpec((N_SHARDS, blk, D_MODEL), lambda j: (0, 0, 0))
    return pl.pallas_call(
        body, name="grad_w_out", grid=(n,),
        in_specs=[pl.BlockSpec((SEQ, blk), lambda j: (0, jnp.minimum(j, per - 1))),
                  pl.BlockSpec((SEQ, blk), lambda j: (0, jnp.maximum(j - per, 0))),
                  pl.BlockSpec((SEQ, D_MODEL), lambda j: (0, 0))],
        out_specs=[whole, whole],
        out_shape=[jax.ShapeDtypeStruct((N_SHARDS, blk, D_MODEL), F32),
                   jax.ShapeDtypeStruct((N_SHARDS, blk, D_MODEL), BF16)],
        scratch_shapes=[pltpu.VMEM((N_SHARDS, blk, D_MODEL), F32), pltpu.VMEM((N_SHARDS, blk, D_MODEL), F32),
                        pltpu.SemaphoreType.DMA((N_SHARDS,)), pltpu.SemaphoreType.DMA((N_SHARDS,))],
        compiler_params=_params(dimension_semantics=("arbitrary",), collective_id=0),
    )(y_pool, y_rec, dout_b)


def _pool_bwd(proj, dout_b, w_out_g, pw_g, pool_scale, exchanges):
    n = POOL_ROWS + POOL_HALO

    def body(u_ref, gate_ref, d_ref, wo_ref, pw_ref, sc_ref,
             dp_ref, dpw_ref, dsc_ref, dd_ref, ddw_ref):
        g = pl.program_id(0)
        dpw_ref[...] = jnp.zeros_like(dpw_ref)
        dsc_ref[...] = jnp.zeros_like(dsc_ref)

        def first(ii, _):
            chunks = [POOL_PAIR * ii + a for a in range(POOL_PAIR)]
            rs = [pl.ds(pl.multiple_of(i * POOL_ROWS, POOL_ROWS), POOL_ROWS) for i in chunks]
            diffs = [_pool_diff(u_ref, i, g) for i in chunks]
            dbs = [d.astype(BF16) for d, _ in diffs]
            mixed = [_dot(db, pw_ref[...]) for db in dbs]
            dys = [_dot_nt(d_ref[r, :], wo_ref[...]) for r in rs]
            sc = sc_ref[...]
            dmbs = []
            for r, m, dy in zip(rs, mixed, dys):
                gate = gate_ref[r, :]
                sg = _sig(gate)
                silu = gate * sg
                dp_ref[1, r, :] = (dy * m * sc * (sg * (1.0 + gate * (1.0 - sg)))).astype(BF16)
                dsc_ref[...] += jnp.sum(dy * silu * m, axis=0, keepdims=True)
                dmbs.append((dy * silu * sc).astype(BF16))
            for db, dmb in zip(dbs, dmbs):
                dpw_ref[...] += _dot_tn(db, dmb)
            dds = [_dot_nt(dmb, pw_ref[...]) for dmb in dmbs]
            for r, dd, (_, inv_count) in zip(rs, dds, diffs):
                dd_ref[r, :] = dd
                ddw_ref[r, :] = dd * inv_count
            return 0
        lax.fori_loop(0, SEQ // POOL_ROWS // POOL_PAIR, first, 0)

        def second(i, _):
            r0 = i * POOL_ROWS
            r = pl.ds(pl.multiple_of(r0, POOL_ROWS), POOL_ROWS)
            last = i == SEQ // POOL_ROWS - 1
            after = ddw_ref[pl.ds(pl.multiple_of(jnp.minimum(r0 + POOL_ROWS, SEQ - POOL_HALO), 8), POOL_HALO), :]
            after = jnp.where(last, 0.0, after)
            ext = jnp.concatenate([ddw_ref[r, :], after], axis=0)
            s = _window_sums(ext, g, lambda k: n - (1 << k))[:POOL_ROWS, :]
            dp_ref[0, r, :] = (s - dd_ref[r, :]).astype(BF16)
            return 0
        lax.fori_loop(0, SEQ // POOL_ROWS, second, 0)

    return _call(
        body, name="pool_bwd", grid=(N_GROUPS,),
        args=(proj, proj, dout_b, w_out_g, pw_g, pool_scale),
        in_specs=[_proj_cols(PG, 0, lambda g: (g, 0)), _proj_cols(PG, 1, lambda g: (g, 0)),
                  pl.BlockSpec((SEQ, D_MODEL), lambda g: (0, 0)),
                  pl.BlockSpec((PG, D_MODEL), lambda g: (g, 0)),
                  pl.BlockSpec((None, PG, PG), lambda g: (g, 0, 0)),
                  pl.BlockSpec((1, PG), lambda g: (0, g))],
        out_specs=[pl.BlockSpec((2, SEQ, PG), lambda g: (0, 0, g)),
                   pl.BlockSpec((None, PG, PG), lambda g: (g, 0, 0)),
                   pl.BlockSpec((1, PG), lambda g: (0, g))],
        out_shape=[jax.ShapeDtypeStruct((2, SEQ, D_MODEL), BF16),
                   jax.ShapeDtypeStruct((N_GROUPS, PG, PG), F32),
                   jax.ShapeDtypeStruct((1, D_MODEL), F32)],
        scratch_shapes=[pltpu.VMEM((SEQ, PG), F32), pltpu.VMEM((SEQ, PG), F32)],
        exchanges=exchanges)


HALF_HEADS = N_HEADS // 2
HALF_COLS = HALF_HEADS * HEAD


def _rec_bwd(proj, o_raw, st_prev, dout_b, w_out_g, lb_logits, rec_g, consts, h0, name, exchanges,
             gate_of=None, own_gate=True):
    n_sec = 4 if gate_of is None else 5

    def body(q_ref, f_ref, i_ref, rg_ref, o_ref, stp_ref, d_ref, wo_ref, lb_ref, g_ref,
             w_ref, lowt_ref, sym_ref, sign_ref, tri_ref, *rest):
        dr_ref, part_ref, dst_ref = rest[-3:]

        @pl.when(pl.program_id(1) == 0)
        def _():
            dst_ref[...] = jnp.zeros_like(dst_ref)
            part_ref[...] = jnp.zeros_like(part_ref)
        if gate_of is not None:
            rg2_ref, o2_ref, wo2_ref, g2_ref = rest[:4]
            rg, o = rg2_ref[...], o2_ref[...]
            sg = _sig(rg)
            inv = lax.rsqrt(jnp.mean(o * o, axis=-1, keepdims=True) + EPS)
            dy = _dot_nt(d_ref[...], wo2_ref[...])
            dr_ref[4] = (dy * (o * inv) * g2_ref[...] * (sg * (1.0 + rg * (1.0 - sg)))).astype(BF16)
        tril = (lax.broadcasted_iota(jnp.int32, (CHUNK, CHUNK), 0)
                > lax.broadcasted_iota(jnp.int32, (CHUNK, CHUNK), 1))
        lb = _lower_bound(lb_ref)
        grec = g_ref[...]
        dst = dst_ref[...]
        acc_grec = jnp.zeros((1, HEAD), F32)
        acc_lb = jnp.zeros((1, HEAD), F32)
        rows = lambda c: pl.ds(c * CHUNK, CHUNK)
        for c0 in reversed(range(0, REC_CHUNKS, REC_GROUP_BWD)):
            group = list(reversed(range(c0, c0 + REC_GROUP_BWD)))
            dys = [_dot_nt(d_ref[rows(c), :], wo_ref[...]) for c in group]
            dos = []
            for c, dy in zip(group, dys):
                rg = rg_ref[rows(c), :]
                o = o_ref[rows(c), :]
                sg = _sig(rg)
                silu = rg * sg
                inv = lax.rsqrt(jnp.mean(o * o, axis=-1, keepdims=True) + EPS)
                recn = o * inv
                if own_gate:
                    dr_ref[3, rows(c), :] = (dy * recn * grec * (sg * (1.0 + rg * (1.0 - sg)))).astype(BF16)
                acc_grec = acc_grec + jnp.sum(dy * silu * recn, axis=0, keepdims=True)
                drecn = dy * silu * grec
                dos.append(inv * drecn - o * (inv * inv * inv) * jnp.mean(drecn * o, axis=-1, keepdims=True))
            gated = [_gates(q_ref[rows(c), :], f_ref[rows(c), :], lb) for c in group]
            g2s = [_dot3(w_ref[...], g) * LOG2E for (_, _, _, _, g) in gated]
            levels = [_level_factors(g2, qs, k, sign_ref) for g2, (qs, _, _, k, _) in zip(g2s, gated)]
            a_ts = []
            for lev in levels:
                a_t = jnp.zeros((CHUNK, CHUNK), F32)
                for l, (xl, _, _, _) in enumerate(lev):
                    a_t = a_t + _dot_nt(xl, xl) * lowt_ref[l]
                a_ts.append(a_t.astype(BF16))
            dobs = [do.astype(BF16) for do in dos]
            vbs = [i_ref[rows(c), :].astype(BF16) for c in group]
            d_syms = [jnp.where(tril, _dot_nt(dob, vb), _dot_nt(vb, dob)) for dob, vb in zip(dobs, vbs)]
            dqs_is, dk_is = [], []
            for lev, d_sym in zip(levels, d_syms):
                dqs_i = jnp.zeros((CHUNK, HEAD), F32)
                both_i = jnp.zeros((CHUNK, HEAD), F32)
                for l, (xl, xlo, e, up) in enumerate(lev):
                    z = d_sym * sym_ref[l]
                    tmp = _dot(z.astype(BF16), jnp.concatenate([xl, xlo], axis=-1))
                    tmp = (tmp[:, :HEAD] + tmp[:, HEAD:]) * e
                    dqs_i = dqs_i + jnp.where(up, tmp, 0.0)
                    both_i = both_i + tmp
                dqs_is.append(dqs_i)
                dk_is.append(both_i - dqs_i)
            e_gs = [jnp.exp2(g2) for g2 in g2s]
            e_revs = [jnp.exp2(g2[CHUNK - 1:CHUNK, :] - g2) for g2 in g2s]
            e_lasts = [jnp.exp2(g2[CHUNK - 1:CHUNK, :]) for g2 in g2s]
            q_gs = [qs * e_g for (qs, _, _, _, _), e_g in zip(gated, e_gs)]
            kdecs = [k * e_rev for (_, _, _, k, _), e_rev in zip(gated, e_revs)]
            dv12 = [_dot(a_t, dob) + jnp.sum(qs * k, axis=-1, keepdims=True) * do
                    for a_t, dob, do, (qs, _, _, k, _) in zip(a_ts, dobs, dos, gated)]
            dq_gs = [_dot(dob, stp_ref[c].astype(BF16)) for c, dob in zip(group, dobs)]
            steps = [_dot_tn(dob, q_g.astype(BF16)) for dob, q_g in zip(dobs, q_gs)]
            dsts = []
            for e_last, step in zip(e_lasts, steps):
                dsts.append(dst)
                dst = dst * e_last + step
            dstbs = [d.astype(BF16) for d in dsts]
            dv3 = [_dot_nt(kdec.astype(BF16), dstb) for kdec, dstb in zip(kdecs, dstbs)]
            dkdecs = [_dot(vb, dstb) for vb, dstb in zip(vbs, dstbs)]
            dbig_gs, dg_lasts, dqss, dks = [], [], [], []
            for i, c in enumerate(group):
                qs, _, _, k, _ = gated[i]
                de_last = jnp.sum(stp_ref[c] * dsts[i], axis=0, keepdims=True)
                ddiag = jnp.sum(dos[i] * i_ref[rows(c), :], axis=-1, keepdims=True)
                dqss.append(dqs_is[i] + ddiag * k + dq_gs[i] * e_gs[i])
                dks.append(dk_is[i] + ddiag * qs + dkdecs[i] * e_revs[i])
                dg_rev = dkdecs[i] * kdecs[i]
                dg_lasts.append(jnp.sum(dg_rev, axis=0, keepdims=True) + de_last * e_lasts[i])
                dbig_gs.append(qs * dqs_is[i] - k * dk_is[i] + dq_gs[i] * q_gs[i] - dg_rev)
            dgs = [_dot3(tri_ref[...], dbig_g) + dg_last for dbig_g, dg_last in zip(dbig_gs, dg_lasts)]
            for i, c in enumerate(group):
                _, sf, f, _, _ = gated[i]
                q = q_ref[rows(c), :]
                df = dgs[i] / f - dks[i]
                dr_ref[1, rows(c), :] = (df * (1.0 - lb) * sf * (1.0 - sf)).astype(BF16)
                acc_lb = acc_lb + jnp.sum(df * (1.0 - sf), axis=0, keepdims=True)
                sq = _sig(q)
                dr_ref[0, rows(c), :] = (dqss[i] * (sq * (1.0 + q * (1.0 - sq)))).astype(BF16)
                dr_ref[2, rows(c), :] = (dv12[i] + dv3[i]).astype(BF16)
        dst_ref[...] = dst
        part_ref[0:1, :] += acc_grec
        part_ref[1:2, :] += acc_lb

    rev = lambda b: N_REC_BLK - 1 - b
    sec = lambda n: _proj_cols(HEAD, n, lambda h, b: (h0 + h, rev(b)), REC_ROWS)
    col_in = pl.BlockSpec((REC_ROWS, HEAD), lambda h, b: (rev(b), h0 + h))
    vec_in = lambda rows: pl.BlockSpec((rows, HEAD), lambda h, b: (0, h0 + h))
    full = lambda a: pl.BlockSpec(a.shape, lambda h, b: (0,) * a.ndim)
    extra_args, extra_specs = (), []
    if gate_of is not None:
        extra_args = (proj, o_raw, w_out_g, rec_g)
        extra_specs = [_proj_cols(HEAD, 5, lambda h, b: (gate_of + h, rev(b)), REC_ROWS),
                       pl.BlockSpec((REC_ROWS, HEAD), lambda h, b: (rev(b), gate_of + h)),
                       pl.BlockSpec((HEAD, D_MODEL), lambda h, b: (SEC_BLK + gate_of + h, 0)),
                       pl.BlockSpec((1, HEAD), lambda h, b: (0, gate_of + h))]
    return _call(
        body, name=name, grid=(HALF_HEADS, N_REC_BLK),
        args=(proj, proj, proj, proj, o_raw, st_prev, dout_b, w_out_g, lb_logits, rec_g,
              consts["tri"], consts["low_t"], consts["sym"], consts["sign"], consts["tri_t"]) + extra_args,
        in_specs=[sec(2), sec(3), sec(4), sec(5), col_in,
                  pl.BlockSpec((None, REC_CHUNKS, HEAD, HEAD), lambda h, b: (h0 + h, rev(b), 0, 0)),
                  pl.BlockSpec((REC_ROWS, D_MODEL), lambda h, b: (rev(b), 0)),
                  pl.BlockSpec((HEAD, D_MODEL), lambda h, b: (SEC_BLK + h0 + h, 0)),
                  vec_in(2), vec_in(1)] + [full(consts[n]) for n in ("tri", "low_t", "sym", "sign", "tri_t")]
                 + extra_specs,
        out_specs=[pl.BlockSpec((n_sec, REC_ROWS, HEAD), lambda h, b: (0, rev(b), h)),
                   pl.BlockSpec((8, HEAD), lambda h, b: (0, h))],
        out_shape=[jax.ShapeDtypeStruct((n_sec, SEQ, HALF_COLS), BF16),
                   jax.ShapeDtypeStruct((8, HALF_COLS), F32)],
        scratch_shapes=[pltpu.VMEM((HEAD, HEAD), F32)],
        exchanges=exchanges)


def _w_in_block(w_ref, j):
    per_shard = W_IN_SHARD // COL_BLK
    return w_ref[j // per_shard, j % per_shard]


def _grad_x(dproj, w_in_g, x, g1, dout, exchanges):
    rows = 512
    n_blk = len(dproj)

    def body(*refs):
        dp_refs = refs[:n_blk]
        w_ref, x_ref, g_ref, dout_ref, dx_ref, part_ref = refs[n_blk:]

        @pl.when(pl.program_id(0) == 0)
        def _():
            part_ref[...] = jnp.zeros_like(part_ref)
        dh = jnp.zeros((rows, D_MODEL), F32)
        for j in range(n_blk):
            dh = dh + _dot_nt(dp_refs[j][...], _w_in_block(w_ref, j))
        xv = x_ref[...]
        inv = lax.rsqrt(jnp.mean(xv * xv, axis=-1, keepdims=True) + EPS)
        a = dh * g_ref[...]
        dx_ref[...] = (dout_ref[...] + inv * a
                       - xv * (inv * inv * inv) * jnp.mean(a * xv, axis=-1, keepdims=True))
        part_ref[0:1, :] += jnp.sum(dh * xv * inv, axis=0, keepdims=True)

    row = lambda: pl.BlockSpec((rows, D_MODEL), lambda i: (i, 0))
    dp_spec = lambda sec, cb: pl.BlockSpec((None, rows, COL_BLK), lambda i: (sec, i, cb))
    return _call(
        body, name="grad_x", grid=(SEQ // rows,),
        args=tuple(a for a, _, _ in dproj) + (w_in_g, x, g1, dout),
        in_specs=[dp_spec(sec, cb) for _, sec, cb in dproj]
                 + [pl.BlockSpec(w_in_g.shape, lambda i: (0, 0, 0, 0)),
                    row(), pl.BlockSpec((1, D_MODEL), lambda i: (0, 0)), row()],
        out_specs=[row(), pl.BlockSpec((8, D_MODEL), lambda i: (0, 0))],
        out_shape=[jax.ShapeDtypeStruct((SEQ, D_MODEL), F32),
                   jax.ShapeDtypeStruct((8, D_MODEL), F32)],
        exchanges=exchanges)


def _grad_w_in(h, dp, blocks, name, collective_id):
    n_blk = len(blocks)
    half = D_MODEL // 2
    pick = lambda vals: (lambda j: functools.reduce(lambda acc, iv: jnp.where(j == iv[0], iv[1], acc),
                                                     list(enumerate(vals))[1:], vals[0]))
    sec_of = pick([sec for sec, _ in blocks])
    cb_of = pick([cb for _, cb in blocks])

    def body(h_ref, dp_ref, p32_ref, p16_ref, keep_ref, send_ref, recv_ref, send_sems, recv_sems):
        j = pl.program_id(0)
        x, y, c = _place()
        cols = lambda cc: pl.ds(pl.multiple_of(cc * half, half), half)

        def copy(i):
            return _remote(send_ref.at[i], recv_ref.at[i], send_sems, recv_sems, i, (x, y, 1 - c))

        pl.when(j == 0)(_other_core_barrier)
        for i in range(n_blk + 1):
            @pl.when(j == i)
            def _(i=i):
                if i < n_blk:
                    send_ref[i] = _dot_tn(h_ref[:, cols(1 - c)], dp_ref[...])
                    copy(i).start()
                    keep_ref[i] = _dot_tn(h_ref[:, cols(c)], dp_ref[...])
                if i > 0:
                    copy(i - 1).wait_recv()
                    tot = keep_ref[i - 1] + recv_ref[i - 1]
                    p32_ref[...] = tot
                    p16_ref[...] = tot.astype(BF16)

        @pl.when(j == n_blk)
        def _():
            for i in range(n_blk):
                copy(i).wait_send()

    lagged = pl.BlockSpec((None, half, COL_BLK), lambda j: (jnp.maximum(j - 1, 0), 0, 0))
    last = n_blk - 1
    return pl.pallas_call(
        body, name=name, grid=(n_blk + 1,),
        in_specs=[pl.BlockSpec((SEQ, D_MODEL), lambda j: (0, 0)),
                  pl.BlockSpec((None, SEQ, COL_BLK),
                               lambda j: (sec_of(jnp.minimum(j, last)), 0, cb_of(jnp.minimum(j, last))))],
        out_specs=[lagged, lagged],
        out_shape=[jax.ShapeDtypeStruct((n_blk, half, COL_BLK), F32),
                   jax.ShapeDtypeStruct((n_blk, half, COL_BLK), BF16)],
        scratch_shapes=[pltpu.VMEM((n_blk, half, COL_BLK), F32)] * 3
                       + [pltpu.SemaphoreType.DMA((n_blk,)), pltpu.SemaphoreType.DMA((n_blk,))],
        compiler_params=_params(dimension_semantics=("arbitrary",), collective_id=collective_id),
    )(h, dp)


def _add_units(grad, recv, place, tile, name):
    n, rows, cols = grad.shape
    per_half = rows // 2 // tile

    def body(place_ref, g_ref, r_ref, o32_ref, o16_ref):
        v = g_ref[...] + r_ref[...]
        o32_ref[...] = v
        o16_ref[...] = v.astype(BF16)

    blk = lambda f: pl.BlockSpec((None, tile, cols), f)
    out = lambda s, i, p: (s, i, 0)
    return pl.pallas_call(
        body, name=name,
        grid_spec=pltpu.PrefetchScalarGridSpec(
            num_scalar_prefetch=1, grid=(n, per_half),
            in_specs=[blk(lambda s, i, p: (s, p[0] * per_half + i, 0)), blk(out)],
            out_specs=[blk(out), blk(out)]),
        out_shape=[jax.ShapeDtypeStruct(recv.shape, F32), jax.ShapeDtypeStruct(recv.shape, BF16)],
        compiler_params=_params(dimension_semantics=("arbitrary", "arbitrary")),
    )(place, grad, recv)


def _sum_units(part32, recv16, place, owners, tile, name):
    n, half, cols = part32.shape
    per_half = half // tile
    table = np.array([[sum(o == chip for o in owners)] + sorted(range(n), key=lambda j: (owners[j] != chip, j))
                      for chip in range(N_SHARDS)], np.int32)
    sched = jnp.concatenate([place[:1], jnp.asarray(table)[place[1]]])

    def block(k, i, p):
        live = k < p[1]
        unit = p[2 + jnp.minimum(k, jnp.maximum(p[1] - 1, 0))]
        return unit, jnp.where(live, i, per_half - 1)

    def body(sched_ref, p_ref, r_ref, o_ref):
        @pl.when(pl.program_id(0) < sched_ref[1])
        def _():
            acc = p_ref[...]
            for j in range(len(CHIP_FLIPS)):
                acc = acc + r_ref[j].astype(F32)
            o_ref[...] = acc

    return pl.pallas_call(
        body, name=name,
        grid_spec=pltpu.PrefetchScalarGridSpec(
            num_scalar_prefetch=1, grid=(n, per_half),
            in_specs=[pl.BlockSpec((None, tile, cols), lambda k, i, p: (*block(k, i, p), 0)),
                      pl.BlockSpec((None, len(CHIP_FLIPS), tile, cols),
                                   lambda k, i, p: (block(k, i, p)[0], 0, block(k, i, p)[1], 0))],
            out_specs=pl.BlockSpec((None, tile, cols),
                                   lambda k, i, p: (block(k, i, p)[0], p[0] * per_half + block(k, i, p)[1], 0))),
        out_shape=jax.ShapeDtypeStruct((n, 2 * half, cols), F32),
        compiler_params=_params(dimension_semantics=("arbitrary", "arbitrary")),
    )(sched, part32, recv16)


def _adamw_math(w, g, m, v):
    m = ADAM_B1 * m + (1.0 - ADAM_B1) * g
    v = ADAM_B2 * v + (1.0 - ADAM_B2) * (g * g)
    m_hat = m / (1.0 - ADAM_B1 ** ADAM_STEP)
    v_hat = v / (1.0 - ADAM_B2 ** ADAM_STEP)
    delta = -ADAM_LR * (m_hat / (jnp.sqrt(v_hat) + ADAM_EPS) + ADAM_WD * w)
    return delta, m, v


def _adamw_units(w, m, v, grads, pick, name):
    rows, cols = w.shape
    bc = grads[0].shape[-1]
    tile = min(rows, 256)
    n_g = len(grads)

    def body(pick_ref, w_ref, m_ref, v_ref, *refs):
        g_refs, (g_out, d_ref, nm_ref, nv_ref) = refs[:n_g], refs[n_g:]
        p = pl.program_id(0)
        for a in range(n_g):
            @pl.when(pick_ref[0, p] == a)
            def _(a=a):
                g = g_refs[a][...]
                g_out[...] = g
                d_ref[...], nm_ref[...], nv_ref[...] = _adamw_math(w_ref[...], g, m_ref[...], v_ref[...])

    blk = pl.BlockSpec((tile, bc), lambda p, i, pick: (i, p))

    def g_spec(a):
        return pl.BlockSpec((None, tile, bc),
                            lambda p, i, pick: (jnp.where(pick[0, p] == a, pick[1, p], 0),
                                                jnp.where(pick[0, p] == a, i, 0), 0))

    return pl.pallas_call(
        body, name=name,
        grid_spec=pltpu.PrefetchScalarGridSpec(
            num_scalar_prefetch=1, grid=(cols // bc, rows // tile),
            in_specs=[blk] * 3 + [g_spec(a) for a in range(n_g)],
            out_specs=[blk] * 4),
        out_shape=[jax.ShapeDtypeStruct(w.shape, F32)] * 4,
        compiler_params=_params(dimension_semantics=("arbitrary", "arbitrary")),
    )(pick, w, m, v, *grads)


ROW_NORM1, ROW_SCALE, ROW_LB, ROW_REC, ROW_FINAL, ROW_LOSS = 0, 1, 2, 4, 5, 6


SMALL_ROWS = (ROW_NORM1, ROW_SCALE, ROW_LB, ROW_REC, ROW_FINAL)


def _small_update(parts, gathered, params):
    n_p = len(params)

    def body(own_ref, p_ref, *refs):
        ins, loss_ref, outs = refs[:3 * n_p], refs[3 * n_p], refs[3 * n_p + 1:]
        x, y, c = _place()
        me = 4 * x + 2 * y + c
        slot = lambda d: jnp.where(me == d, own_ref[...], p_ref[d])
        tot = slot(0)
        for d in range(1, 8):
            tot = tot + slot(d)
        for i, r in enumerate(SMALL_ROWS):
            w = ins[3 * i][...]
            g = tot[r:r + 1, :]
            if r == ROW_LB:
                mx = jnp.maximum(w[0:1, :], w[1:2, :])
                e0 = jnp.exp(w[0:1, :] - mx)
                e1 = jnp.exp(w[1:2, :] - mx)
                lb = e0 / (e0 + e1)
                g = g * lb * (1.0 - lb)
                g = jnp.concatenate([g, -g], axis=0)
            outs[4 * i][...] = g
            outs[4 * i + 1][...], outs[4 * i + 2][...], outs[4 * i + 3][...] = _adamw_math(
                w, g, ins[3 * i + 1][...], ins[3 * i + 2][...])
        loss_ref[...] = (0.5 / D_MODEL) * jnp.sum(tot[ROW_LOSS:ROW_LOSS + 1, :], axis=-1, keepdims=True)

    flat = [a for wmv in params for a in wmv]
    return pl.pallas_call(
        body, name="small_update",
        out_shape=[jax.ShapeDtypeStruct((1, 1), F32)]
                  + [jax.ShapeDtypeStruct(w.shape, F32) for w, _, _ in params for _ in range(4)],
        compiler_params=_params(),
    )(parts, gathered, *flat)


SHARD_OWNERS = tuple(range(N_SHARDS))
BLOCKS_POOL = (0, 1, 2, 3)
BLOCKS_A = (4, 6, 8, 10, 11)
BLOCKS_B = (5, 7, 9)
BLOCK_GROUPS = (BLOCKS_POOL, BLOCKS_A, BLOCKS_B)


def _block_owners(blocks):
    return tuple(j // (W_IN_SHARD // COL_BLK) for j in blocks)


def kernel(x, norm1_g, w_in, pool_w, pool_scale, lb_logits, rec_norm_g, w_out, final_norm_g, loss_target, m_norm1_g, m_w_in, m_pool_w, m_pool_scale, m_lb_logits, m_rec_norm_g, m_w_out, m_final_norm_g, v_norm1_g, v_w_in, v_pool_w, v_pool_scale, v_lb_logits, v_rec_norm_g, v_w_out, v_final_norm_g):
    xi, yi, ci = _place()
    chip = 2 * xi + yi
    place = jnp.stack([ci, chip]).astype(jnp.int32)
    pw_rows = N_GROUPS * PW_SHARD
    flat_pw = lambda a: a.reshape(pw_rows, PG)
    x2, target, gf = x[0], loss_target[0], final_norm_g.reshape(1, D_MODEL)
    consts = {n: jnp.asarray(a, BF16 if n.startswith("tri") else F32) for n, a in _chunk_constants().items()}

    proj, h, w_in_g, w_out_slots, pw_slots = _in_proj(x2, norm1_g, _cast_w_in(w_in[0], place), place,
                                                      [w_out[0], flat_pw(pool_w)])
    (y_rec, o_raw, st_prev), ((w_out_g, pw_g),) = _rec_fwd(
        proj, lb_logits, rec_norm_g, consts, [_ex_gather([w_out_slots, pw_slots])])
    w_out_g = w_out_g.reshape(2 * D_MODEL, D_MODEL)
    pw_full = pw_g.reshape(N_SHARDS, N_GROUPS, PW_SHARD, PG).transpose(1, 0, 2, 3).reshape(N_GROUPS, PG, PG)
    y_pool = _pool_fwd(proj, pw_full, pool_scale)
    dout, dout_b, part_out = _out_proj_loss(y_pool, y_rec, w_out_g, x2, target, gf)

    p_out32, p_out16 = _grad_w_out(y_pool, y_rec, dout_b)
    (dpool, gpw, dscale), ((rb_out,),) = _pool_bwd(proj, dout_b, w_out_g, pw_full, pool_scale,
                                                   [_ex_send([p_out16], [SHARD_OWNERS])])
    g_out = _sum_units(p_out32, rb_out, place, SHARD_OWNERS, 256, "sum_w_out")
    gpw = gpw.reshape(N_GROUPS, N_SHARDS, PW_SHARD, PG).transpose(1, 0, 2, 3).reshape(N_SHARDS, pw_rows, PG)
    p_inp32, p_inp16 = _grad_w_in(h, dpool, [(0, 0), (0, 1), (1, 0), (1, 1)], "grad_w_in_pool", 1)

    pool_owners, a_owners, b_owners = (_block_owners(b) for b in BLOCK_GROUPS)
    rec_args = (proj, o_raw, st_prev, dout_b, w_out_g, lb_logits, rec_norm_g, consts)
    (drec_a, part_a), ((rb_inp,), (ra_pw,)) = _rec_bwd(
        *rec_args, 0, "rec_bwd_a", [_ex_send([p_inp16], [pool_owners], units=[(0, 1)]), _ex_swap([gpw])],
        gate_of=HALF_HEADS)
    p_pw32, p_pw16 = _add_units(gpw, ra_pw, place, 128, "add_pool_w")
    p_ina32, p_ina16 = _grad_w_in(h, drec_a, [(n, 0) for n in range(5)], "grad_w_in_a", 2)

    (drec_b, part_b), ((rb_inp, rb_ina, rb_pw),) = _rec_bwd(
        *rec_args, HALF_HEADS, "rec_bwd_b",
        [_ex_send([p_inp16, p_ina16, p_pw16], [pool_owners, a_owners, SHARD_OWNERS],
                  units=[(2, 3), tuple(range(len(a_owners))), SHARD_OWNERS], landed=[rb_inp, None, None])],
        own_gate=False)
    g_inp = _sum_units(p_inp32, rb_inp, place, pool_owners, 256, "sum_w_in_pool")
    g_ina = _sum_units(p_ina32, rb_ina, place, a_owners, 256, "sum_w_in_a")
    g_pw = _sum_units(p_pw32, rb_pw, place, SHARD_OWNERS, 128, "sum_pool_w")
    p_inb32, p_inb16 = _grad_w_in(h, drec_b, [(n, 0) for n in range(3)], "grad_w_in_b", 3)

    dproj = ([(dpool, 0, 0), (dpool, 0, 1), (dpool, 1, 0), (dpool, 1, 1)]
             + [(d, n, 0) for n in range(3) for d in (drec_a, drec_b)] + [(drec_a, 3, 0), (drec_a, 4, 0)])
    (dx, part_x), ((rb_inb,),) = _grad_x(dproj, w_in_g, x2, norm1_g, dout, [_ex_send([p_inb16], [b_owners])])
    g_inb = _sum_units(p_inb32, rb_inb, place, b_owners, 256, "sum_w_in_b")
    zero = jnp.zeros((1, D_MODEL), F32)
    part_rec = jnp.concatenate([part_a, part_b], axis=1)
    parts = jnp.concatenate([part_x[0:1], dscale, part_rec[1:2], zero, part_rec[0:1], part_out[0:1],
                             part_out[1:2], zero], axis=0)
    _, ((g_out, g_pw, g_inp, g_ina, g_inb), (gathered,)) = _call(
        None, name="join_halves",
        exchanges=[_ex_join([g_out, g_pw, g_inp, g_ina, g_inb],
                            [SHARD_OWNERS, SHARD_OWNERS, pool_owners, a_owners, b_owners]),
                   _ex_gather_small(parts)])

    group_of = np.zeros((D_PROJ // COL_BLK,), np.int32)
    index_of = np.zeros((D_PROJ // COL_BLK,), np.int32)
    for gi, blocks in enumerate(BLOCK_GROUPS):
        for i, j in enumerate(blocks):
            group_of[j], index_of[j] = gi, i
    per_shard = W_IN_SHARD // COL_BLK
    pick_in = jnp.stack([lax.dynamic_slice(jnp.asarray(group_of), (per_shard * chip,), (per_shard,)),
                         lax.dynamic_slice(jnp.asarray(index_of), (per_shard * chip,), (per_shard,))])
    pick_own = jnp.stack([jnp.zeros((1,), jnp.int32), chip.reshape(1).astype(jnp.int32)])
    big = [_adamw_units(w_in[0], m_w_in[0], v_w_in[0], [g_inp, g_ina, g_inb], pick_in, "adamw_w_in"),
           _adamw_units(w_out[0], m_w_out[0], v_w_out[0], [g_out], pick_own, "adamw_w_out"),
           _adamw_units(flat_pw(pool_w), flat_pw(m_pool_w), flat_pw(v_pool_w), [g_pw], pick_own, "adamw_pool_w")]

    row = lambda a: a.reshape(1, D_MODEL)
    loss, *small = _small_update(parts, gathered, [
        (norm1_g, m_norm1_g, v_norm1_g), (pool_scale, m_pool_scale, v_pool_scale),
        (lb_logits, m_lb_logits, v_lb_logits), (rec_norm_g, m_rec_norm_g, v_rec_norm_g),
        (row(final_norm_g), row(m_final_norm_g), row(v_final_norm_g))])

    def leaves(k):
        norm1, scale, lb, rec, final = (small[4 * i + k] for i in range(len(SMALL_ROWS)))
        return (norm1, big[0][k][None], big[2][k].reshape(pool_w.shape), scale, lb, rec,
                big[1][k][None], final.reshape(D_MODEL))

    return (loss.reshape(()), dx[None], *leaves(0), *leaves(1), *leaves(2), *leaves(3))
```

```python
import functools

import numpy as np
import jax
import jax.numpy as jnp
from jax import lax
from jax.experimental import pallas as pl
from jax.experimental.pallas import tpu as pltpu

F32 = jnp.float32
BF16 = jnp.bfloat16

SEQ = 2048
D_MODEL = 1024
D_PROJ = 6144
N_SEC = 6
N_GROUPS = 4
PG = 256
N_HEADS = 8
HEAD = 128
CHUNK = 64
N_LEVELS = 6
N_SHARDS = 4
W_IN_SHARD = D_PROJ // N_SHARDS
W_OUT_SHARD = 2048 // N_SHARDS
PW_SHARD = PG // N_SHARDS
COL_BLK = 512
EPS = 1e-6

ADAM_LR = 0.001
ADAM_B1 = 0.9
ADAM_B2 = 0.999
ADAM_EPS = 1e-08
ADAM_WD = 0.01
ADAM_STEP = 10

V7X_VMEM_LIMIT = 56 * 1024 * 1024
MESH = pl.DeviceIdType.MESH


def _params(**kw):
    return pltpu.CompilerParams(vmem_limit_bytes=V7X_VMEM_LIMIT, **kw)


def _sig(x):
    return 1.0 / (1.0 + jnp.exp(-x))


def _dot(a, b):
    return jnp.dot(a, b, preferred_element_type=F32)


def _dot_nt(a, b):
    return lax.dot_general(a, b, (((1,), (1,)), ((), ())), preferred_element_type=F32)


def _dot_tn(a, b):
    return lax.dot_general(a, b, (((0,), (0,)), ((), ())), preferred_element_type=F32)


def _split3(a):
    p1 = a.astype(BF16)
    r1 = a - p1.astype(F32)
    p2 = r1.astype(BF16)
    p3 = (r1 - p2.astype(F32)).astype(BF16)
    return jnp.concatenate([p1, p2, p3], axis=-1)


def _dot3(w01, a):
    n = a.shape[-1]
    r = _dot(w01, _split3(a))
    return r[:, :n] + r[:, n:2 * n] + r[:, 2 * n:]


def _chunk_constants():
    j = np.arange(CHUNK)
    tt, ss = np.meshgrid(j, j, indexing="ij")
    x = tt ^ ss
    hb = np.full((CHUNK, CHUNK), -1, np.int32)
    for l in range(N_LEVELS):
        hb[x >= (1 << l)] = l
    sym = np.stack([(hb == l) for l in range(N_LEVELS)]).astype(np.float32)
    low = sym * (tt > ss)
    sign = np.stack([np.where((j >> l) & 1, 1.0, -1.0) for l in range(N_LEVELS)]).astype(np.float32)
    sign = np.ascontiguousarray(np.broadcast_to(sign[:, :, None], (N_LEVELS, CHUNK, HEAD)))
    tri = (ss <= tt).astype(np.float32)
    return dict(tri=tri, tri_t=np.ascontiguousarray(tri.T), low=low,
                low_t=np.ascontiguousarray(low.transpose(0, 2, 1)), sym=sym, sign=sign)


def _in_proj(x, g1, w_slots, place, also_cast):
    n_col = D_PROJ // COL_BLK
    per_shard = W_IN_SHARD // COL_BLK
    rows = 1024
    half_rows = D_MODEL // 2
    quarter_rows = D_MODEL // 4
    FLIPS = (0, 2, 1, 3)
    ORDER = ([(0, p) for p in range(per_shard)] + [(m, p) for p in range(per_shard) for m in (1, 2)]
             + [(3, p) for p in range(per_shard)])

    def shard_at(m, chip):
        return chip ^ FLIPS[m]

    def pick(vals, t):
        return functools.reduce(lambda acc, iv: jnp.where(t == iv[0], iv[1], acc), list(enumerate(vals))[1:], vals[0])

    def body(place_ref, x_ref, g_ref, w_in_ref, *rest):
        others, (proj_ref, h_ref, w_ref) = rest[:len(also_cast)], rest[len(also_cast):len(also_cast) + 3]
        slots = rest[len(also_cast) + 3:2 * len(also_cast) + 3]
        wbuf, load_sems, send_sems, recv_sems = rest[2 * len(also_cast) + 3:]
        t = pl.program_id(0)

        @pl.when(t == 1)
        def _():
            for src, dst in zip(others, slots):
                dst[...] = src[...].astype(BF16)
        x_, y_, c = _place()
        chip = 2 * x_ + y_
        me, other_core = (x_, y_, c), (x_, y_, 1 - c)
        x_nbr, y_nbr = (1 - x_, y_, c), (x_, 1 - y_, c)

        def rows_of(half, q=None):
            if q is None:
                return pl.ds(pl.multiple_of(half * half_rows, half_rows), half_rows)
            return pl.ds(pl.multiple_of(half * half_rows + q * quarter_rows, quarter_rows), quarter_rows)

        def block(m, p, r):
            return w_ref.at[shard_at(m, chip), p, r, :]

        def copy(k, ref, to):
            return _remote(ref, ref, send_sems, recv_sems, k, to)

        direct = lambda n, p, to: copy(3 * n + p, block(0, p, rows_of(c)), to)
        relay = lambda n, p, to: copy(6 + 3 * n + p, block(1 + n, p, rows_of(c, n)), to)
        arrived = lambda m, p: ([copy(3 * (m - 1) + p, block(m, p, rows_of(c)), me)] if m < 3 else
                                [copy(6 + 3 * n + p, block(3, p, rows_of(c, n)), me) for n in (0, 1)])
        passed_on = lambda m, p, half, to: copy(9 + 3 * m + p, block(m, p, rows_of(half)), to)

        def load(m, p, slot):
            return pltpu.make_async_copy(w_ref.at[shard_at(m, chip), p], wbuf.at[slot], load_sems.at[slot])

        def prepare(m, p):
            for cp in arrived(m, p):
                cp.wait_recv()
            passed_on(m, p, c, other_core).start()
            if m < 3:
                relay(m - 1, p, y_nbr if m == 1 else x_nbr).start()

        @pl.when(t == 0)
        def _():
            for p in range(per_shard):
                direct(0, p, x_nbr).start()
                direct(1, p, y_nbr).start()
            for p in range(per_shard):
                load(0, p, p).start()

            def norm(i, _):
                r = pl.ds(pl.multiple_of(i * rows, rows), rows)
                xv = x_ref[r, :]
                inv = lax.rsqrt(jnp.mean(xv * xv, axis=-1, keepdims=True) + EPS)
                h_ref[r, :] = (xv * inv * g_ref[...]).astype(BF16)
                return 0
            lax.fori_loop(0, SEQ // rows, norm, 0)

        for step, (m, p) in enumerate(ORDER):
            @pl.when(t == step)
            def _(step=step, m=m, p=p):
                slot = step % per_shard
                if m > 0:
                    passed_on(m, p, 1 - c, me).wait_recv()
                    load(m, p, slot).start()
                if step + 1 < n_col and ORDER[step + 1][0] > 0:
                    prepare(*ORDER[step + 1])
                load(m, p, slot).wait()

                def mm(i, _):
                    r = pl.ds(pl.multiple_of(i * rows, rows), rows)
                    proj_ref[r, :] = _dot(h_ref[r, :], wbuf[slot])
                    return 0
                lax.fori_loop(0, SEQ // rows, mm, 0)

        @pl.when(t == n_col - 1)
        def _():
            for p in range(per_shard):
                sent = [direct(0, p, x_nbr), direct(1, p, y_nbr), relay(0, p, y_nbr), relay(1, p, x_nbr)]
                for cp in sent + [passed_on(m, p, c, other_core) for m in (1, 2, 3)]:
                    cp.wait_send()

    return pl.pallas_call(
        body, name="in_proj",
        grid_spec=pltpu.PrefetchScalarGridSpec(
            num_scalar_prefetch=1, grid=(n_col,),
            in_specs=[pl.BlockSpec((SEQ, D_MODEL), lambda t, p: (0, 0)),
                      pl.BlockSpec((1, D_MODEL), lambda t, p: (0, 0)),
                      pl.BlockSpec(memory_space=pl.ANY)]
                     + [pl.BlockSpec(a.shape, lambda t, p: (0, 0)) for a in also_cast],
            out_specs=[pl.BlockSpec((None, SEQ, COL_BLK),
                                    lambda t, p: (per_shard * (p[1] ^ pick([FLIPS[m] for m, _ in ORDER], t))
                                                  + pick([b for _, b in ORDER], t), 0, 0)),
                       pl.BlockSpec((SEQ, D_MODEL), lambda t, p: (0, 0)),
                       pl.BlockSpec(memory_space=pl.ANY)]
                      + [pl.BlockSpec((None,) + a.shape, lambda t, p: (p[1], 0, 0)) for a in also_cast],
            scratch_shapes=[pltpu.VMEM((per_shard, D_MODEL, COL_BLK), BF16),
                            pltpu.SemaphoreType.DMA((per_shard,)),
                            pltpu.SemaphoreType.DMA((21,)), pltpu.SemaphoreType.DMA((21,))]),
        out_shape=[jax.ShapeDtypeStruct((n_col, SEQ, COL_BLK), F32),
                   jax.ShapeDtypeStruct((SEQ, D_MODEL), BF16),
                   jax.ShapeDtypeStruct(w_slots.shape, BF16)]
                  + [jax.ShapeDtypeStruct((N_SHARDS,) + a.shape, BF16) for a in also_cast],
        input_output_aliases={3: 2},
        compiler_params=_params(dimension_semantics=("arbitrary",)),
    )(place, x, g1, w_slots, *also_cast)


def _proj_cols(width, section, where, rows=SEQ):
    per_blk = COL_BLK // width

    def index(*grid):
        k, r = where(*grid)
        return section * (D_MODEL // COL_BLK) + k // per_blk, r, k % per_blk

    return pl.BlockSpec((None, rows, width), index)


POOL_ROWS = 256
POOL_HALO = 16
POOL_PAIR = 2


def _window_sums(ext, g, shift_of):
    s = ext
    for k in range(N_GROUPS):
        s = jnp.where(k <= g, s + pltpu.roll(s, shift_of(k), 0), s)
    return s


def _pool_diff(u_ref, i, g):
    n = POOL_ROWS + POOL_HALO
    r0 = i * POOL_ROWS
    cur = u_ref[pl.ds(pl.multiple_of(r0, POOL_ROWS), POOL_ROWS), :]
    before = u_ref[pl.ds(pl.multiple_of(jnp.maximum(r0 - POOL_HALO, 0), 8), POOL_HALO), :]
    before = jnp.where(i > 0, before, 0.0)
    ext = jnp.concatenate([before, cur], axis=0)
    s = _window_sums(ext, g, lambda k: 1 << k)[POOL_HALO:, :]
    t = r0 + lax.broadcasted_iota(jnp.int32, (POOL_ROWS, 1), 0)
    width = (2 << g).astype(F32)
    inv_count = 1.0 / jnp.minimum((t + 1).astype(F32), width)
    return s * inv_count - cur, inv_count


def _pool_fwd(proj, pw_g, pool_scale):
    def body(u_ref, gate_ref, pw_ref, sc_ref, y_ref):
        g = pl.program_id(0)

        def step(ii, _):
            chunks = [POOL_PAIR * ii + a for a in range(POOL_PAIR)]
            ds = [_pool_diff(u_ref, i, g)[0].astype(BF16) for i in chunks]
            mixed = [_dot(d, pw_ref[...]) for d in ds]
            for i, m in zip(chunks, mixed):
                r = pl.ds(pl.multiple_of(i * POOL_ROWS, POOL_ROWS), POOL_ROWS)
                gate = gate_ref[r, :]
                y_ref[r, :] = (m * sc_ref[...] * (gate * _sig(gate))).astype(BF16)
            return 0
        lax.fori_loop(0, SEQ // POOL_ROWS // POOL_PAIR, step, 0)

    return pl.pallas_call(
        body, name="pool_fwd", grid=(N_GROUPS,),
        in_specs=[_proj_cols(PG, 0, lambda g: (g, 0)), _proj_cols(PG, 1, lambda g: (g, 0)),
                  pl.BlockSpec((None, PG, PG), lambda g: (g, 0, 0)),
                  pl.BlockSpec((1, PG), lambda g: (0, g))],
        out_specs=pl.BlockSpec((SEQ, PG), lambda g: (0, g)),
        out_shape=jax.ShapeDtypeStruct((SEQ, D_MODEL), BF16),
        compiler_params=_params(dimension_semantics=("arbitrary",)),
    )(proj, proj, pw_g, pool_scale)


REC_ROWS = 1024
REC_CHUNKS = REC_ROWS // CHUNK
N_REC_BLK = SEQ // REC_ROWS
REC_GROUP = REC_CHUNKS
REC_GROUP_BWD = REC_CHUNKS
SEC_BLK = D_MODEL // HEAD


def _lower_bound(lb_ref):
    l0 = lb_ref[0:1, :]
    l1 = lb_ref[1:2, :]
    mx = jnp.maximum(l0, l1)
    e0 = jnp.exp(l0 - mx)
    e1 = jnp.exp(l1 - mx)
    return e0 / (e0 + e1)


def _gates(q, fl, lb):
    qs = q * _sig(q)
    sf = _sig(fl)
    f = lb + (1.0 - lb) * sf
    return qs, sf, f, 1.0 - f, jnp.log(f)


LOG2E = 1.4426950408889634


def _level_factors(g2, qs, k, sign_ref):
    t = lax.broadcasted_iota(jnp.int32, (CHUNK, HEAD), 0)
    row = lambda r, n: jnp.broadcast_to(g2[r:r + 1, :], (n, HEAD))
    out = []
    for l in range(N_LEVELS):
        m = 1 << l
        if l == 0:
            g_mid = jnp.where((t & 1) == 1, pltpu.roll(g2, 1, 0), g2)
        elif l == 1:
            low = (t & 7) < 4
            g_mid = jnp.concatenate([jnp.where(low[:8], row(8 * i + 1, 8), row(8 * i + 5, 8))
                                     for i in range(CHUNK // 8)], axis=0)
        else:
            g_mid = jnp.concatenate([row(b * 2 * m + m - 1, 2 * m) for b in range(CHUNK // (2 * m))], axis=0)
        sgn = sign_ref[l]
        up = sgn > 0.0
        e = jnp.exp2((g2 - g_mid) * sgn)
        x = jnp.where(up, qs, k) * e
        hi = x.astype(BF16)
        out.append((hi, (x - hi.astype(F32)).astype(BF16), e, up))
    return out


CHIP_FLIPS = ((1, 0), (0, 1), (1, 1))
HBM = pl.BlockSpec(memory_space=pl.ANY)


def _place():
    return lax.axis_index("x"), lax.axis_index("y"), lax.axis_index("c")


def _remote(src, dst, send_sems, recv_sems, k, to):
    return pltpu.make_async_remote_copy(src_ref=src, dst_ref=dst, send_sem=send_sems.at[k],
                                        recv_sem=recv_sems.at[k], device_id=to, device_id_type=MESH)


def _half_rows(ref, c):
    half = ref.shape[-2] // 2
    rows = pl.ds(pl.multiple_of(c * half, half), half)
    return ref.at[:, rows, :] if len(ref.shape) == 3 else ref.at[rows, :]


def _other_core_barrier():
    x, y, c = _place()
    sem = pltpu.get_barrier_semaphore()
    pl.semaphore_signal(sem, inc=1, device_id=(x, y, 1 - c), device_id_type=MESH)
    pl.semaphore_wait(sem, 1)


class _Exchange:
    def __init__(self, inputs, out_shapes, n_sems, start, finish, aliases=None, middle=None):
        self.inputs, self.out_shapes, self.n_sems = list(inputs), list(out_shapes), n_sems
        self.start, self.finish, self.aliases = start, finish, dict(aliases or {})
        self.middle = middle or (lambda *refs: None)


def _ex_swap(grads):
    def copies(ins, outs, send, recv):
        x, y, c = _place()
        return [_remote(_half_rows(g, 1 - c), o, send, recv, t, (x, y, 1 - c))
                for t, (g, o) in enumerate(zip(ins, outs))]

    def start(*refs):
        for cp in copies(*refs):
            cp.start()

    def finish(*refs):
        cps = copies(*refs)
        for cp in cps:
            cp.wait_recv()
        for cp in cps:
            cp.wait_send()

    shapes = [jax.ShapeDtypeStruct((a.shape[0], a.shape[1] // 2, a.shape[2]), F32) for a in grads]
    return _Exchange(grads, shapes, len(grads), start, finish)


def _ex_send(parts16, owners, units=None, landed=None):
    n_t = len(parts16)
    units = units or [tuple(range(len(o))) for o in owners]
    landed = landed or [None] * n_t
    given = [t for t in range(n_t) if landed[t] is not None]

    def each(ins, outs, send, recv, to_sender, to_owner):
        x, y, c = _place()
        k = 0
        for t, own in enumerate(owners):
            for j in units[t]:
                for r, (fx, fy) in enumerate(CHIP_FLIPS):
                    tx, ty = x ^ fx, y ^ fy
                    cp = _remote(ins[t].at[j], outs[t].at[j, r], send, recv, k, (tx, ty, c))
                    if to_sender is not None:
                        pl.when(2 * tx + ty == own[j])(functools.partial(to_sender, cp))
                    if to_owner is not None:
                        pl.when(2 * x + y == own[j])(functools.partial(to_owner, cp))
                    k += 1

    def start(*refs):
        each(*refs, lambda cp: cp.start(), None)

    def finish(*refs):
        each(*refs, None, lambda cp: cp.wait_recv())
        each(*refs, lambda cp: cp.wait_send(), None)

    shapes = [jax.ShapeDtypeStruct((a.shape[0], len(CHIP_FLIPS)) + a.shape[1:], BF16) for a in parts16]
    return _Exchange(list(parts16) + [landed[t] for t in given], shapes,
                     len(CHIP_FLIPS) * sum(len(u) for u in units), start, finish,
                     aliases={n_t + i: t for i, t in enumerate(given)})


def _ex_join(units, owners):
    def each(ins, outs, send, recv, fn):
        x, y, c = _place()
        k = 0
        for t, own in enumerate(owners):
            for j, o in enumerate(own):
                def half(cc, to, u=outs[t].at[j], k=k):
                    return _remote(_half_rows(u, cc), _half_rows(u, cc), send, recv, k, to)
                mine = functools.partial(half, c, (x, y, 1 - c))
                theirs = functools.partial(half, 1 - c, (x, y, c))
                pl.when(2 * x + y == o)(functools.partial(fn, mine, theirs))
                k += 1

    def start(*refs):
        each(*refs, lambda mine, theirs: mine().start())

    def finish(*refs):
        each(*refs, lambda mine, theirs: theirs().wait_recv())
        each(*refs, lambda mine, theirs: mine().wait_send())

    shapes = [jax.ShapeDtypeStruct(a.shape, F32) for a in units]
    return _Exchange(units, shapes, sum(len(o) for o in owners), start, finish,
                     aliases={t: t for t in range(len(units))})


def _ex_gather(slots):
    n_t = len(slots)
    n_fl = len(CHIP_FLIPS)

    def piece(ref, shard, half):
        return _half_rows(ref.at[shard], half)

    def first(outs, send, recv):
        x, y, c = _place()
        s = 2 * x + y
        return [_remote(piece(outs[t], s, c), piece(outs[t], s, c), send, recv, n_t * j + t, (x ^ fx, y ^ fy, c))
                for j, (fx, fy) in enumerate(CHIP_FLIPS) for t in range(n_t)]

    def start(ins, outs, send, recv):
        for cp in first(outs, send, recv):
            cp.start()

    def passed_on(outs, send, recv):
        x, y, c = _place()
        return [_remote(piece(outs[t], 2 * (x ^ fx) + (y ^ fy), c), piece(outs[t], 2 * (x ^ fx) + (y ^ fy), c),
                        send, recv, n_t * n_fl + n_t * j + t, (x, y, 1 - c))
                for j, (fx, fy) in enumerate(CHIP_FLIPS) for t in range(n_t)]

    def middle(ins, outs, send, recv):
        x, y, c = _place()
        for j, (fx, fy) in enumerate(CHIP_FLIPS):
            sj = 2 * (x ^ fx) + (y ^ fy)
            for t in range(n_t):
                _remote(piece(outs[t], sj, c), piece(outs[t], sj, c), send, recv, n_t * j + t, (x, y, c)).wait_recv()
        for cp in passed_on(outs, send, recv):
            cp.start()

    def finish(ins, outs, send, recv):
        x, y, c = _place()
        for j, (fx, fy) in enumerate(CHIP_FLIPS):
            sj = 2 * (x ^ fx) + (y ^ fy)
            for t in range(n_t):
                k = n_t * n_fl + n_t * j + t
                _remote(piece(outs[t], sj, 1 - c), piece(outs[t], sj, 1 - c), send, recv, k, (x, y, c)).wait_recv()
        for cp in first(outs, send, recv) + passed_on(outs, send, recv):
            cp.wait_send()

    shapes = [jax.ShapeDtypeStruct(a.shape, BF16) for a in slots]
    return _Exchange(slots, shapes, 2 * n_t * n_fl, start, finish, aliases={t: t for t in range(n_t)},
                     middle=middle)


def _ex_gather_small(parts):
    def copies(ins, outs, send, recv):
        x, y, c = _place()
        me = 4 * x + 2 * y + c
        return [_remote(ins[0], outs[0].at[me], send, recv, mask - 1,
                        (x ^ (mask >> 2), y ^ ((mask >> 1) & 1), c ^ (mask & 1))) for mask in range(1, 8)]

    def start(*refs):
        for cp in copies(*refs):
            cp.start()

    def finish(ins, outs, send, recv):
        x, y, c = _place()
        me = 4 * x + 2 * y + c
        for mask in range(1, 8):
            _remote(ins[0], outs[0].at[me ^ mask], send, recv, mask - 1, (x, y, c)).wait_recv()
        for cp in copies(ins, outs, send, recv):
            cp.wait_send()

    return _Exchange([parts], [jax.ShapeDtypeStruct((8,) + parts.shape, F32)], 7, start, finish)


def _call(body, *, name, args=(), in_specs=(), out_specs=(), out_shape=(), grid=(), scratch_shapes=(),
          exchanges=()):
    n_in, n_out, n_scr = len(args), len(out_shape), len(scratch_shapes)
    ex_in, ex_out, ex_scr, spans, alias = [], [], [], [], {}
    for ex in exchanges:
        spans.append((len(ex_in), len(ex.inputs), len(ex_out), len(ex.out_shapes)))
        for i, o in ex.aliases.items():
            alias[n_in + len(ex_in) + i] = n_out + len(ex_out) + o
        ex_in += ex.inputs
        ex_out += ex.out_shapes
        ex_scr += [pltpu.SemaphoreType.DMA((ex.n_sems,)), pltpu.SemaphoreType.DMA((ex.n_sems,))]

    def full(*refs):
        ins, x_in = refs[:n_in], refs[n_in:n_in + len(ex_in)]
        outs = refs[n_in + len(ex_in):n_in + len(ex_in) + n_out]
        x_out = refs[n_in + len(ex_in) + n_out:n_in + len(ex_in) + n_out + len(ex_out)]
        scr = refs[len(refs) - n_scr - len(ex_scr):len(refs) - len(ex_scr)]
        sems = refs[len(refs) - len(ex_scr):]

        def run(which):
            for e, (ex, (i0, ni, o0, no)) in enumerate(zip(exchanges, spans)):
                getattr(ex, which)(x_in[i0:i0 + ni], x_out[o0:o0 + no], sems[2 * e], sems[2 * e + 1])

        if grid:
            ids = [pl.program_id(a) for a in range(len(grid))]
            step = functools.reduce(lambda acc, ig: acc * ig[1] + ig[0], zip(ids, grid), 0)
            n_steps = int(np.prod(grid))
            pl.when(step == 0)(lambda: run("start"))
            if n_steps > 2:
                pl.when(step == n_steps // 2)(lambda: run("middle"))
            body(*ins, *outs, *scr)
            if n_steps <= 2:
                pl.when(step == n_steps - 1)(lambda: run("middle"))
            pl.when(step == n_steps - 1)(lambda: run("finish"))
        else:
            run("start")
            if body is not None:
                body(*ins, *outs, *scr)
            run("middle")
            run("finish")

    kw = dict(grid=grid) if grid else {}
    if grid:
        kw["compiler_params"] = _params(dimension_semantics=("arbitrary",) * len(grid))
    else:
        kw["compiler_params"] = _params()
    res = pl.pallas_call(
        full, name=name,
        in_specs=list(in_specs) + [HBM] * len(ex_in),
        out_specs=list(out_specs) + [HBM] * len(ex_out),
        out_shape=list(out_shape) + ex_out,
        scratch_shapes=list(scratch_shapes) + ex_scr,
        input_output_aliases=alias, **kw,
    )(*args, *ex_in)
    own = list(res[:n_out])
    per_ex = [list(res[n_out + o0:n_out + o0 + no]) for (_, _, o0, no) in spans]
    return own, per_ex


def _cast_w_in(w, place):
    rows, cols = w.shape
    tile = rows

    def body(place_ref, w_ref, o_ref):
        o_ref[...] = w_ref[...].astype(BF16)

    return pl.pallas_call(
        body, name="cast_w_in",
        grid_spec=pltpu.PrefetchScalarGridSpec(
            num_scalar_prefetch=1, grid=(cols // COL_BLK, rows // tile),
            in_specs=[pl.BlockSpec((tile, COL_BLK), lambda b, i, p: (i, b))],
            out_specs=pl.BlockSpec((None, None, tile, COL_BLK), lambda b, i, p: (p[1], b, i, 0))),
        out_shape=jax.ShapeDtypeStruct((N_SHARDS, cols // COL_BLK, rows, COL_BLK), BF16),
        compiler_params=_params(dimension_semantics=("arbitrary", "arbitrary")),
    )(place, w)


def _rec_fwd(proj, lb_logits, rec_g, consts, exchanges):
    tri, low, sign = consts["tri"], consts["low"], consts["sign"]

    def body(q_ref, f_ref, i_ref, rg_ref, lb_ref, g_ref, w_ref, low_ref, sign_ref, y_ref, o_ref, stp_ref, st_ref):
        @pl.when(pl.program_id(1) == 0)
        def _():
            st_ref[...] = jnp.zeros_like(st_ref)
        lb = _lower_bound(lb_ref)
        st = st_ref[...]
        rows = lambda c: pl.ds(c * CHUNK, CHUNK)
        for c0 in range(0, REC_CHUNKS, REC_GROUP):
            group = range(c0, c0 + REC_GROUP)
            gated = [_gates(q_ref[rows(c), :], f_ref[rows(c), :], lb) for c in group]
            g2s = [_dot3(w_ref[...], g) * LOG2E for (_, _, _, _, g) in gated]
            xs = [[xl for xl, _, _, _ in _level_factors(g2, qs, k, sign_ref)]
                  for g2, (qs, _, _, k, _) in zip(g2s, gated)]
            a_s = []
            for x in xs:
                a = jnp.zeros((CHUNK, CHUNK), F32)
                for l, xl in enumerate(x):
                    a = a + _dot_nt(xl, xl) * low_ref[l]
                a_s.append(a.astype(BF16))
            vbs = [i_ref[rows(c), :].astype(BF16) for c in group]
            intra = [_dot(a, vb) for a, vb in zip(a_s, vbs)]
            kvs = [_dot_tn(vb, (k * jnp.exp2(g2[CHUNK - 1:CHUNK, :] - g2)).astype(BF16))
                   for vb, g2, (_, _, _, k, _) in zip(vbs, g2s, gated)]
            for i, c in enumerate(group):
                qs, _, _, k, _ = gated[i]
                g2 = g2s[i]
                stp_ref[c] = st
                v = i_ref[rows(c), :]
                rg = rg_ref[rows(c), :]
                o = (intra[i] + jnp.sum(qs * k, axis=-1, keepdims=True) * v
                     + _dot_nt((qs * jnp.exp2(g2)).astype(BF16), st.astype(BF16)))
                st = st * jnp.exp2(g2[CHUNK - 1:CHUNK, :]) + kvs[i]
                o_ref[rows(c), :] = o
                inv = lax.rsqrt(jnp.mean(o * o, axis=-1, keepdims=True) + EPS)
                y_ref[rows(c), :] = (o * inv * g_ref[...] * (rg * _sig(rg))).astype(BF16)
        st_ref[...] = st

    sec = lambda n: _proj_cols(HEAD, n, lambda h, b: (h, b), REC_ROWS)
    vec = lambda rows: pl.BlockSpec((rows, HEAD), lambda h, b: (0, h))
    full = lambda a: pl.BlockSpec(a.shape, lambda h, b: (0,) * a.ndim)
    return _call(
        body, name="rec_fwd", grid=(N_HEADS, N_REC_BLK),
        args=(proj, proj, proj, proj, lb_logits, rec_g, tri, low, sign),
        in_specs=[sec(2), sec(3), sec(4), sec(5), vec(2), vec(1), full(tri), full(low), full(sign)],
        out_specs=[pl.BlockSpec((REC_ROWS, HEAD), lambda h, b: (b, h)),
                   pl.BlockSpec((REC_ROWS, HEAD), lambda h, b: (b, h)),
                   pl.BlockSpec((None, REC_CHUNKS, HEAD, HEAD), lambda h, b: (h, b, 0, 0))],
        out_shape=[jax.ShapeDtypeStruct((SEQ, D_MODEL), BF16),
                   jax.ShapeDtypeStruct((SEQ, D_MODEL), F32),
                   jax.ShapeDtypeStruct((N_HEADS, SEQ // CHUNK, HEAD, HEAD), F32)],
        scratch_shapes=[pltpu.VMEM((HEAD, HEAD), F32)],
        exchanges=exchanges)


OUT_ROWS = 512


def _out_proj_loss(y_pool, y_rec, w_out_g, x, target, gf):
    def body(yp_ref, yr_ref, w_ref, x_ref, t_ref, gf_ref, dout_ref, doutb_ref, part_ref):
        @pl.when(pl.program_id(0) == 0)
        def _():
            part_ref[...] = jnp.zeros_like(part_ref)
        halves = [pl.ds(a * (OUT_ROWS // 2), OUT_ROWS // 2) for a in range(2)]
        outs = [x_ref[r, :] + _dot(yp_ref[r, :], w_ref[0:D_MODEL, :])
                + _dot(yr_ref[r, :], w_ref[D_MODEL:2 * D_MODEL, :]) for r in halves]
        gf_v = gf_ref[...]
        for r, out in zip(halves, outs):
            inv = lax.rsqrt(jnp.mean(out * out, axis=-1, keepdims=True) + EPS)
            diff = out * inv * gf_v - t_ref[r, :]
            dyf = diff * (1.0 / D_MODEL)
            a = dyf * gf_v
            dout = inv * a - out * (inv * inv * inv) * jnp.mean(a * out, axis=-1, keepdims=True)
            dout_ref[r, :] = dout
            doutb_ref[r, :] = dout.astype(BF16)
            part_ref[0:1, :] += jnp.sum(dyf * out * inv, axis=0, keepdims=True)
            part_ref[1:2, :] += jnp.sum(diff * diff, axis=0, keepdims=True)

    row = lambda n: pl.BlockSpec((OUT_ROWS, n), lambda i: (i, 0))
    return pl.pallas_call(
        body, name="out_proj_loss", grid=(SEQ // OUT_ROWS,),
        in_specs=[row(D_MODEL), row(D_MODEL), pl.BlockSpec((2 * D_MODEL, D_MODEL), lambda i: (0, 0)),
                  row(D_MODEL), row(D_MODEL), pl.BlockSpec((1, D_MODEL), lambda i: (0, 0))],
        out_specs=[row(D_MODEL), row(D_MODEL), pl.BlockSpec((8, D_MODEL), lambda i: (0, 0))],
        out_shape=[jax.ShapeDtypeStruct((SEQ, D_MODEL), F32),
                   jax.ShapeDtypeStruct((SEQ, D_MODEL), BF16),
                   jax.ShapeDtypeStruct((8, D_MODEL), F32)],
        compiler_params=_params(dimension_semantics=("arbitrary",)),
    )(y_pool, y_rec, w_out_g, x, target, gf)


def _grad_w_out(y_pool, y_rec, dout_b):
    blk = W_OUT_SHARD // 2
    per = D_MODEL // blk
    n = 2 * per

    def body(yp_ref, yr_ref, d_ref, p32_ref, p16_ref, send_ref, recv_ref, send_sems, recv_sems):
        j = pl.program_id(0)
        x, y, c = _place()

        def copy(u):
            return _remote(send_ref.at[u], recv_ref.at[u], send_sems, recv_sems, u, (x, y, 1 - c))

        pl.when(j == 0)(_other_core_barrier)
        for i in range(n):
            @pl.when(j == i)
            def _(i=i):
                res = _dot_tn((yp_ref if i < per else yr_ref)[...], d_ref[...])

                @pl.when(i % 2 == c)
                def _():
                    p32_ref[i // 2] = res

                @pl.when(i % 2 != c)
                def _():
                    send_ref[i // 2] = res
                    copy(i // 2).start()

        @pl.when(j == n - 1)
        def _():
            for u in range(N_SHARDS):
                copy(u).wait_recv()
                tot = p32_ref[u] + recv_ref[u]
                p32_ref[u] = tot
                p16_ref[u] = tot.astype(BF16)
            for u in range(N_SHARDS):
                copy(u).wait_send()

    whole = pl.BlockSpec((N_SHARDS, blk, D_MODEL), lambda j: (0, 0, 0))
    return pl.pallas_call(
        body, name="grad_w_out", grid=(n,),
        in_specs=[pl.BlockSpec((SEQ, blk), lambda j: (0, jnp.minimum(j, per - 1))),
                  pl.BlockSpec((SEQ, blk), lambda j: (0, jnp.maximum(j - per, 0))),
                  pl.BlockSpec((SEQ, D_MODEL), lambda j: (0, 0))],
        out_specs=[whole, whole],
        out_shape=[jax.ShapeDtypeStruct((N_SHARDS, blk, D_MODEL), F32),
                   jax.ShapeDtypeStruct((N_SHARDS, blk, D_MODEL), BF16)],
        scratch_shapes=[pltpu.VMEM((N_SHARDS, blk, D_MODEL), F32), pltpu.VMEM((N_SHARDS, blk, D_MODEL), F32),
                        pltpu.SemaphoreType.DMA((N_SHARDS,)), pltpu.SemaphoreType.DMA((N_SHARDS,))],
        compiler_params=_params(dimension_semantics=("arbitrary",), collective_id=0),
    )(y_pool, y_rec, dout_b)


def _pool_bwd(proj, dout_b, w_out_g, pw_g, pool_scale, exchanges):
    n = POOL_ROWS + POOL_HALO

    def body(u_ref, gate_ref, d_ref, wo_ref, pw_ref, sc_ref,
             dp_ref, dpw_ref, dsc_ref, dd_ref, ddw_ref):
        g = pl.program_id(0)
        dpw_ref[...] = jnp.zeros_like(dpw_ref)
        dsc_ref[...] = jnp.zeros_like(dsc_ref)

        def first(ii, _):
            chunks = [POOL_PAIR * ii + a for a in range(POOL_PAIR)]
            rs = [pl.ds(pl.multiple_of(i * POOL_ROWS, POOL_ROWS), POOL_ROWS) for i in chunks]
            diffs = [_pool_diff(u_ref, i, g) for i in chunks]
            dbs = [d.astype(BF16) for d, _ in diffs]
            mixed = [_dot(db, pw_ref[...]) for db in dbs]
            dys = [_dot_nt(d_ref[r, :], wo_ref[...]) for r in rs]
            sc = sc_ref[...]
            dmbs = []
            for r, m, dy in zip(rs, mixed, dys):
                gate = gate_ref[r, :]
                sg = _sig(gate)
                silu = gate * sg
                dp_ref[1, r, :] = (dy * m * sc * (sg * (1.0 + gate * (1.0 - sg)))).astype(BF16)
                dsc_ref[...] += jnp.sum(dy * silu * m, axis=0, keepdims=True)
                dmbs.append((dy * silu * sc).astype(BF16))
            for db, dmb in zip(dbs, dmbs):
                dpw_ref[...] += _dot_tn(db, dmb)
            dds = [_dot_nt(dmb, pw_ref[...]) for dmb in dmbs]
            for r, dd, (_, inv_count) in zip(rs, dds, diffs):
                dd_ref[r, :] = dd
                ddw_ref[r, :] = dd * inv_count
            return 0
        lax.fori_loop(0, SEQ // POOL_ROWS // POOL_PAIR, first, 0)

        def second(i, _):
            r0 = i * POOL_ROWS
            r = pl.ds(pl.multiple_of(r0, POOL_ROWS), POOL_ROWS)
            last = i == SEQ // POOL_ROWS - 1
            after = ddw_ref[pl.ds(pl.multiple_of(jnp.minimum(r0 + POOL_ROWS, SEQ - POOL_HALO), 8), POOL_HALO), :]
            after = jnp.where(last, 0.0, after)
            ext = jnp.concatenate([ddw_ref[r, :], after], axis=0)
            s = _window_sums(ext, g, lambda k: n - (1 << k))[:POOL_ROWS, :]
            dp_ref[0, r, :] = (s - dd_ref[r, :]).astype(BF16)
            return 0
        lax.fori_loop(0, SEQ // POOL_ROWS, second, 0)

    return _call(
        body, name="pool_bwd", grid=(N_GROUPS,),
        args=(proj, proj, dout_b, w_out_g, pw_g, pool_scale),
        in_specs=[_proj_cols(PG, 0, lambda g: (g, 0)), _proj_cols(PG, 1, lambda g: (g, 0)),
                  pl.BlockSpec((SEQ, D_MODEL), lambda g: (0, 0)),
                  pl.BlockSpec((PG, D_MODEL), lambda g: (g, 0)),
                  pl.BlockSpec((None, PG, PG), lambda g: (g, 0, 0)),
                  pl.BlockSpec((1, PG), lambda g: (0, g))],
        out_specs=[pl.BlockSpec((2, SEQ, PG), lambda g: (0, 0, g)),
                   pl.BlockSpec((None, PG, PG), lambda g: (g, 0, 0)),
                   pl.BlockSpec((1, PG), lambda g: (0, g))],
        out_shape=[jax.ShapeDtypeStruct((2, SEQ, D_MODEL), BF16),
                   jax.ShapeDtypeStruct((N_GROUPS, PG, PG), F32),
                   jax.ShapeDtypeStruct((1, D_MODEL), F32)],
        scratch_shapes=[pltpu.VMEM((SEQ, PG), F32), pltpu.VMEM((SEQ, PG), F32)],
        exchanges=exchanges)


HALF_HEADS = N_HEADS // 2
HALF_COLS = HALF_HEADS * HEAD


def _rec_bwd(proj, o_raw, st_prev, dout_b, w_out_g, lb_logits, rec_g, consts, h0, name, exchanges,
             gate_of=None, own_gate=True):
    n_sec = 4 if gate_of is None else 5

    def body(q_ref, f_ref, i_ref, rg_ref, o_ref, stp_ref, d_ref, wo_ref, lb_ref, g_ref,
             w_ref, lowt_ref, sym_ref, sign_ref, tri_ref, *rest):
        dr_ref, part_ref, dst_ref = rest[-3:]

        @pl.when(pl.program_id(1) == 0)
        def _():
            dst_ref[...] = jnp.zeros_like(dst_ref)
            part_ref[...] = jnp.zeros_like(part_ref)
        if gate_of is not None:
            rg2_ref, o2_ref, wo2_ref, g2_ref = rest[:4]
            rg, o = rg2_ref[...], o2_ref[...]
            sg = _sig(rg)
            inv = lax.rsqrt(jnp.mean(o * o, axis=-1, keepdims=True) + EPS)
            dy = _dot_nt(d_ref[...], wo2_ref[...])
            dr_ref[4] = (dy * (o * inv) * g2_ref[...] * (sg * (1.0 + rg * (1.0 - sg)))).astype(BF16)
        tril = (lax.broadcasted_iota(jnp.int32, (CHUNK, CHUNK), 0)
                > lax.broadcasted_iota(jnp.int32, (CHUNK, CHUNK), 1))
        lb = _lower_bound(lb_ref)
        grec = g_ref[...]
        dst = dst_ref[...]
        acc_grec = jnp.zeros((1, HEAD), F32)
        acc_lb = jnp.zeros((1, HEAD), F32)
        rows = lambda c: pl.ds(c * CHUNK, CHUNK)
        for c0 in reversed(range(0, REC_CHUNKS, REC_GROUP_BWD)):
            group = list(reversed(range(c0, c0 + REC_GROUP_BWD)))
            dys = [_dot_nt(d_ref[rows(c), :], wo_ref[...]) for c in group]
            dos = []
            for c, dy in zip(group, dys):
                rg = rg_ref[rows(c), :]
                o = o_ref[rows(c), :]
                sg = _sig(rg)
                silu = rg * sg
                inv = lax.rsqrt(jnp.mean(o * o, axis=-1, keepdims=True) + EPS)
                recn = o * inv
                if own_gate:
                    dr_ref[3, rows(c), :] = (dy * recn * grec * (sg * (1.0 + rg * (1.0 - sg)))).astype(BF16)
                acc_grec = acc_grec + jnp.sum(dy * silu * recn, axis=0, keepdims=True)
                drecn = dy * silu * grec
                dos.append(inv * drecn - o * (inv * inv * inv) * jnp.mean(drecn * o, axis=-1, keepdims=True))
            gated = [_gates(q_ref[rows(c), :], f_ref[rows(c), :], lb) for c in group]
            g2s = [_dot3(w_ref[...], g) * LOG2E for (_, _, _, _, g) in gated]
            levels = [_level_factors(g2, qs, k, sign_ref) for g2, (qs, _, _, k, _) in zip(g2s, gated)]
            a_ts = []
            for lev in levels:
                a_t = jnp.zeros((CHUNK, CHUNK), F32)
                for l, (xl, _, _, _) in enumerate(lev):
                    a_t = a_t + _dot_nt(xl, xl) * lowt_ref[l]
                a_ts.append(a_t.astype(BF16))
            dobs = [do.astype(BF16) for do in dos]
            vbs = [i_ref[rows(c), :].astype(BF16) for c in group]
            d_syms = [jnp.where(tril, _dot_nt(dob, vb), _dot_nt(vb, dob)) for dob, vb in zip(dobs, vbs)]
            dqs_is, dk_is = [], []
            for lev, d_sym in zip(levels, d_syms):
                dqs_i = jnp.zeros((CHUNK, HEAD), F32)
                both_i = jnp.zeros((CHUNK, HEAD), F32)
                for l, (xl, xlo, e, up) in enumerate(lev):
                    z = d_sym * sym_ref[l]
                    tmp = _dot(z.astype(BF16), jnp.concatenate([xl, xlo], axis=-1))
                    tmp = (tmp[:, :HEAD] + tmp[:, HEAD:]) * e
                    dqs_i = dqs_i + jnp.where(up, tmp, 0.0)
                    both_i = both_i + tmp
                dqs_is.append(dqs_i)
                dk_is.append(both_i - dqs_i)
            e_gs = [jnp.exp2(g2) for g2 in g2s]
            e_revs = [jnp.exp2(g2[CHUNK - 1:CHUNK, :] - g2) for g2 in g2s]
            e_lasts = [jnp.exp2(g2[CHUNK - 1:CHUNK, :]) for g2 in g2s]
            q_gs = [qs * e_g for (qs, _, _, _, _), e_g in zip(gated, e_gs)]
            kdecs = [k * e_rev for (_, _, _, k, _), e_rev in zip(gated, e_revs)]
            dv12 = [_dot(a_t, dob) + jnp.sum(qs * k, axis=-1, keepdims=True) * do
                    for a_t, dob, do, (qs, _, _, k, _) in zip(a_ts, dobs, dos, gated)]
            dq_gs = [_dot(dob, stp_ref[c].astype(BF16)) for c, dob in zip(group, dobs)]
            steps = [_dot_tn(dob, q_g.astype(BF16)) for dob, q_g in zip(dobs, q_gs)]
            dsts = []
            for e_last, step in zip(e_lasts, steps):
                dsts.append(dst)
                dst = dst * e_last + step
            dstbs = [d.astype(BF16) for d in dsts]
            dv3 = [_dot_nt(kdec.astype(BF16), dstb) for kdec, dstb in zip(kdecs, dstbs)]
            dkdecs = [_dot(vb, dstb) for vb, dstb in zip(vbs, dstbs)]
            dbig_gs, dg_lasts, dqss, dks = [], [], [], []
            for i, c in enumerate(group):
                qs, _, _, k, _ = gated[i]
                de_last = jnp.sum(stp_ref[c] * dsts[i], axis=0, keepdims=True)
                ddiag = jnp.sum(dos[i] * i_ref[rows(c), :], axis=-1, keepdims=True)
                dqss.append(dqs_is[i] + ddiag * k + dq_gs[i] * e_gs[i])
                dks.append(dk_is[i] + ddiag * qs + dkdecs[i] * e_revs[i])
                dg_rev = dkdecs[i] * kdecs[i]
                dg_lasts.append(jnp.sum(dg_rev, axis=0, keepdims=True) + de_last * e_lasts[i])
                dbig_gs.append(qs * dqs_is[i] - k * dk_is[i] + dq_gs[i] * q_gs[i] - dg_rev)
            dgs = [_dot3(tri_ref[...], dbig_g) + dg_last for dbig_g, dg_last in zip(dbig_gs, dg_lasts)]
            for i, c in enumerate(group):
                _, sf, f, _, _ = gated[i]
                q = q_ref[rows(c), :]
                df = dgs[i] / f - dks[i]
                dr_ref[1, rows(c), :] = (df * (1.0 - lb) * sf * (1.0 - sf)).astype(BF16)
                acc_lb = acc_lb + jnp.sum(df * (1.0 - sf), axis=0, keepdims=True)
                sq = _sig(q)
                dr_ref[0, rows(c), :] = (dqss[i] * (sq * (1.0 + q * (1.0 - sq)))).astype(BF16)
                dr_ref[2, rows(c), :] = (dv12[i] + dv3[i]).astype(BF16)
        dst_ref[...] = dst
        part_ref[0:1, :] += acc_grec
        part_ref[1:2, :] += acc_lb

    rev = lambda b: N_REC_BLK - 1 - b
    sec = lambda n: _proj_cols(HEAD, n, lambda h, b: (h0 + h, rev(b)), REC_ROWS)
    col_in = pl.BlockSpec((REC_ROWS, HEAD), lambda h, b: (rev(b), h0 + h))
    vec_in = lambda rows: pl.BlockSpec((rows, HEAD), lambda h, b: (0, h0 + h))
    full = lambda a: pl.BlockSpec(a.shape, lambda h, b: (0,) * a.ndim)
    extra_args, extra_specs = (), []
    if gate_of is not None:
        extra_args = (proj, o_raw, w_out_g, rec_g)
        extra_specs = [_proj_cols(HEAD, 5, lambda h, b: (gate_of + h, rev(b)), REC_ROWS),
                       pl.BlockSpec((REC_ROWS, HEAD), lambda h, b: (rev(b), gate_of + h)),
                       pl.BlockSpec((HEAD, D_MODEL), lambda h, b: (SEC_BLK + gate_of + h, 0)),
                       pl.BlockSpec((1, HEAD), lambda h, b: (0, gate_of + h))]
    return _call(
        body, name=name, grid=(HALF_HEADS, N_REC_BLK),
        args=(proj, proj, proj, proj, o_raw, st_prev, dout_b, w_out_g, lb_logits, rec_g,
              consts["tri"], consts["low_t"], consts["sym"], consts["sign"], consts["tri_t"]) + extra_args,
        in_specs=[sec(2), sec(3), sec(4), sec(5), col_in,
                  pl.BlockSpec((None, REC_CHUNKS, HEAD, HEAD), lambda h, b: (h0 + h, rev(b), 0, 0)),
                  pl.BlockSpec((REC_ROWS, D_MODEL), lambda h, b: (rev(b), 0)),
                  pl.BlockSpec((HEAD, D_MODEL), lambda h, b: (SEC_BLK + h0 + h, 0)),
                  vec_in(2), vec_in(1)] + [full(consts[n]) for n in ("tri", "low_t", "sym", "sign", "tri_t")]
                 + extra_specs,
        out_specs=[pl.BlockSpec((n_sec, REC_ROWS, HEAD), lambda h, b: (0, rev(b), h)),
                   pl.BlockSpec((8, HEAD), lambda h, b: (0, h))],
        out_shape=[jax.ShapeDtypeStruct((n_sec, SEQ, HALF_COLS), BF16),
                   jax.ShapeDtypeStruct((8, HALF_COLS), F32)],
        scratch_shapes=[pltpu.VMEM((HEAD, HEAD), F32)],
        exchanges=exchanges)


def _w_in_block(w_ref, j):
    per_shard = W_IN_SHARD // COL_BLK
    return w_ref[j // per_shard, j % per_shard]


def _grad_x(dproj, w_in_g, x, g1, dout, exchanges):
    rows = 512
    n_blk = len(dproj)

    def body(*refs):
        dp_refs = refs[:n_blk]
        w_ref, x_ref, g_ref, dout_ref, dx_ref, part_ref = refs[n_blk:]

        @pl.when(pl.program_id(0) == 0)
        def _():
            part_ref[...] = jnp.zeros_like(part_ref)
        dh = jnp.zeros((rows, D_MODEL), F32)
        for j in range(n_blk):
            dh = dh + _dot_nt(dp_refs[j][...], _w_in_block(w_ref, j))
        xv = x_ref[...]
        inv = lax.rsqrt(jnp.mean(xv * xv, axis=-1, keepdims=True) + EPS)
        a = dh * g_ref[...]
        dx_ref[...] = (dout_ref[...] + inv * a
                       - xv * (inv * inv * inv) * jnp.mean(a * xv, axis=-1, keepdims=True))
        part_ref[0:1, :] += jnp.sum(dh * xv * inv, axis=0, keepdims=True)

    row = lambda: pl.BlockSpec((rows, D_MODEL), lambda i: (i, 0))
    dp_spec = lambda sec, cb: pl.BlockSpec((None, rows, COL_BLK), lambda i: (sec, i, cb))
    return _call(
        body, name="grad_x", grid=(SEQ // rows,),
        args=tuple(a for a, _, _ in dproj) + (w_in_g, x, g1, dout),
        in_specs=[dp_spec(sec, cb) for _, sec, cb in dproj]
                 + [pl.BlockSpec(w_in_g.shape, lambda i: (0, 0, 0, 0)),
                    row(), pl.BlockSpec((1, D_MODEL), lambda i: (0, 0)), row()],
        out_specs=[row(), pl.BlockSpec((8, D_MODEL), lambda i: (0, 0))],
        out_shape=[jax.ShapeDtypeStruct((SEQ, D_MODEL), F32),
                   jax.ShapeDtypeStruct((8, D_MODEL), F32)],
        exchanges=exchanges)


def _grad_w_in(h, dp, blocks, name, collective_id):
    n_blk = len(blocks)
    half = D_MODEL // 2
    pick = lambda vals: (lambda j: functools.reduce(lambda acc, iv: jnp.where(j == iv[0], iv[1], acc),
                                                     list(enumerate(vals))[1:], vals[0]))
    sec_of = pick([sec for sec, _ in blocks])
    cb_of = pick([cb for _, cb in blocks])

    def body(h_ref, dp_ref, p32_ref, p16_ref, keep_ref, send_ref, recv_ref, send_sems, recv_sems):
        j = pl.program_id(0)
        x, y, c = _place()
        cols = lambda cc: pl.ds(pl.multiple_of(cc * half, half), half)

        def copy(i):
            return _remote(send_ref.at[i], recv_ref.at[i], send_sems, recv_sems, i, (x, y, 1 - c))

        pl.when(j == 0)(_other_core_barrier)
        for i in range(n_blk + 1):
            @pl.when(j == i)
            def _(i=i):
                if i < n_blk:
                    send_ref[i] = _dot_tn(h_ref[:, cols(1 - c)], dp_ref[...])
                    copy(i).start()
                    keep_ref[i] = _dot_tn(h_ref[:, cols(c)], dp_ref[...])
                if i > 0:
                    copy(i - 1).wait_recv()
                    tot = keep_ref[i - 1] + recv_ref[i - 1]
                    p32_ref[...] = tot
                    p16_ref[...] = tot.astype(BF16)

        @pl.when(j == n_blk)
        def _():
            for i in range(n_blk):
                copy(i).wait_send()

    lagged = pl.BlockSpec((None, half, COL_BLK), lambda j: (jnp.maximum(j - 1, 0), 0, 0))
    last = n_blk - 1
    return pl.pallas_call(
        body, name=name, grid=(n_blk + 1,),
        in_specs=[pl.BlockSpec((SEQ, D_MODEL), lambda j: (0, 0)),
                  pl.BlockSpec((None, SEQ, COL_BLK),
                               lambda j: (sec_of(jnp.minimum(j, last)), 0, cb_of(jnp.minimum(j, last))))],
        out_specs=[lagged, lagged],
        out_shape=[jax.ShapeDtypeStruct((n_blk, half, COL_BLK), F32),
                   jax.ShapeDtypeStruct((n_blk, half, COL_BLK), BF16)],
        scratch_shapes=[pltpu.VMEM((n_blk, half, COL_BLK), F32)] * 3
                       + [pltpu.SemaphoreType.DMA((n_blk,)), pltpu.SemaphoreType.DMA((n_blk,))],
        compiler_params=_params(dimension_semantics=("arbitrary",), collective_id=collective_id),
    )(h, dp)


def _add_units(grad, recv, place, tile, name):
    n, rows, cols = grad.shape
    per_half = rows // 2 // tile

    def body(place_ref, g_ref, r_ref, o32_ref, o16_ref):
        v = g_ref[...] + r_ref[...]
        o32_ref[...] = v
        o16_ref[...] = v.astype(BF16)

    blk = lambda f: pl.BlockSpec((None, tile, cols), f)
    out = lambda s, i, p: (s, i, 0)
    return pl.pallas_call(
        body, name=name,
        grid_spec=pltpu.PrefetchScalarGridSpec(
            num_scalar_prefetch=1, grid=(n, per_half),
            in_specs=[blk(lambda s, i, p: (s, p[0] * per_half + i, 0)), blk(out)],
            out_specs=[blk(out), blk(out)]),
        out_shape=[jax.ShapeDtypeStruct(recv.shape, F32), jax.ShapeDtypeStruct(recv.shape, BF16)],
        compiler_params=_params(dimension_semantics=("arbitrary", "arbitrary")),
    )(place, grad, recv)


def _sum_units(part32, recv16, place, owners, tile, name):
    n, half, cols = part32.shape
    per_half = half // tile
    table = np.array([[sum(o == chip for o in owners)] + sorted(range(n), key=lambda j: (owners[j] != chip, j))
                      for chip in range(N_SHARDS)], np.int32)
    sched = jnp.concatenate([place[:1], jnp.asarray(table)[place[1]]])

    def block(k, i, p):
        live = k < p[1]
        unit = p[2 + jnp.minimum(k, jnp.maximum(p[1] - 1, 0))]
        return unit, jnp.where(live, i, per_half - 1)

    def body(sched_ref, p_ref, r_ref, o_ref):
        @pl.when(pl.program_id(0) < sched_ref[1])
        def _():
            acc = p_ref[...]
            for j in range(len(CHIP_FLIPS)):
                acc = acc + r_ref[j].astype(F32)
            o_ref[...] = acc

    return pl.pallas_call(
        body, name=name,
        grid_spec=pltpu.PrefetchScalarGridSpec(
            num_scalar_prefetch=1, grid=(n, per_half),
            in_specs=[pl.BlockSpec((None, tile, cols), lambda k, i, p: (*block(k, i, p), 0)),
                      pl.BlockSpec((None, len(CHIP_FLIPS), tile, cols),
                                   lambda k, i, p: (block(k, i, p)[0], 0, block(k, i, p)[1], 0))],
            out_specs=pl.BlockSpec((None, tile, cols),
                                   lambda k, i, p: (block(k, i, p)[0], p[0] * per_half + block(k, i, p)[1], 0))),
        out_shape=jax.ShapeDtypeStruct((n, 2 * half, cols), F32),
        compiler_params=_params(dimension_semantics=("arbitrary", "arbitrary")),
    )(sched, part32, recv16)


def _adamw_math(w, g, m, v):
    m = ADAM_B1 * m + (1.0 - ADAM_B1) * g
    v = ADAM_B2 * v + (1.0 - ADAM_B2) * (g * g)
    m_hat = m / (1.0 - ADAM_B1 ** ADAM_STEP)
    v_hat = v / (1.0 - ADAM_B2 ** ADAM_STEP)
    delta = -ADAM_LR * (m_hat / (jnp.sqrt(v_hat) + ADAM_EPS) + ADAM_WD * w)
    return delta, m, v


def _adamw_units(w, m, v, grads, pick, name):
    rows, cols = w.shape
    bc = grads[0].shape[-1]
    tile = min(rows, 256)
    n_g = len(grads)

    def body(pick_ref, w_ref, m_ref, v_ref, *refs):
        g_refs, (g_out, d_ref, nm_ref, nv_ref) = refs[:n_g], refs[n_g:]
        p = pl.program_id(0)
        for a in range(n_g):
            @pl.when(pick_ref[0, p] == a)
            def _(a=a):
                g = g_refs[a][...]
                g_out[...] = g
                d_ref[...], nm_ref[...], nv_ref[...] = _adamw_math(w_ref[...], g, m_ref[...], v_ref[...])

    blk = pl.BlockSpec((tile, bc), lambda p, i, pick: (i, p))

    def g_spec(a):
        return pl.BlockSpec((None, tile, bc),
                            lambda p, i, pick: (jnp.where(pick[0, p] == a, pick[1, p], 0),
                                                jnp.where(pick[0, p] == a, i, 0), 0))

    return pl.pallas_call(
        body, name=name,
        grid_spec=pltpu.PrefetchScalarGridSpec(
            num_scalar_prefetch=1, grid=(cols // bc, rows // tile),
            in_specs=[blk] * 3 + [g_spec(a) for a in range(n_g)],
            out_specs=[blk] * 4),
        out_shape=[jax.ShapeDtypeStruct(w.shape, F32)] * 4,
        compiler_params=_params(dimension_semantics=("arbitrary", "arbitrary")),
    )(pick, w, m, v, *grads)


ROW_NORM1, ROW_SCALE, ROW_LB, ROW_REC, ROW_FINAL, ROW_LOSS = 0, 1, 2, 4, 5, 6


SMALL_ROWS = (ROW_NORM1, ROW_SCALE, ROW_LB, ROW_REC, ROW_FINAL)


def _small_update(parts, gathered, params):
    n_p = len(params)

    def body(own_ref, p_ref, *refs):
        ins, loss_ref, outs = refs[:3 * n_p], refs[3 * n_p], refs[3 * n_p + 1:]
        x, y, c = _place()
        me = 4 * x + 2 * y + c
        slot = lambda d: jnp.where(me == d, own_ref[...], p_ref[d])
        tot = slot(0)
        for d in range(1, 8):
            tot = tot + slot(d)
        for i, r in enumerate(SMALL_ROWS):
            w = ins[3 * i][...]
            g = tot[r:r + 1, :]
            if r == ROW_LB:
                mx = jnp.maximum(w[0:1, :], w[1:2, :])
                e0 = jnp.exp(w[0:1, :] - mx)
                e1 = jnp.exp(w[1:2, :] - mx)
                lb = e0 / (e0 + e1)
                g = g * lb * (1.0 - lb)
                g = jnp.concatenate([g, -g], axis=0)
            outs[4 * i][...] = g
            outs[4 * i + 1][...], outs[4 * i + 2][...], outs[4 * i + 3][...] = _adamw_math(
                w, g, ins[3 * i + 1][...], ins[3 * i + 2][...])
        loss_ref[...] = (0.5 / D_MODEL) * jnp.sum(tot[ROW_LOSS:ROW_LOSS + 1, :], axis=-1, keepdims=True)

    flat = [a for wmv in params for a in wmv]
    return pl.pallas_call(
        body, name="small_update",
        out_shape=[jax.ShapeDtypeStruct((1, 1), F32)]
                  + [jax.ShapeDtypeStruct(w.shape, F32) for w, _, _ in params for _ in range(4)],
        compiler_params=_params(),
    )(parts, gathered, *flat)


SHARD_OWNERS = tuple(range(N_SHARDS))
BLOCKS_POOL = (0, 1, 2, 3)
BLOCKS_A = (4, 6, 8, 10, 11)
BLOCKS_B = (5, 7, 9)
BLOCK_GROUPS = (BLOCKS_POOL, BLOCKS_A, BLOCKS_B)


def _block_owners(blocks):
    return tuple(j // (W_IN_SHARD // COL_BLK) for j in blocks)


def kernel(x, norm1_g, w_in, pool_w, pool_scale, lb_logits, rec_norm_g, w_out, final_norm_g, loss_target, m_norm1_g, m_w_in, m_pool_w, m_pool_scale, m_lb_logits, m_rec_norm_g, m_w_out, m_final_norm_g, v_norm1_g, v_w_in, v_pool_w, v_pool_scale, v_lb_logits, v_rec_norm_g, v_w_out, v_final_norm_g):
    xi, yi, ci = _place()
    chip = 2 * xi + yi
    place = jnp.stack([ci, chip]).astype(jnp.int32)
    pw_rows = N_GROUPS * PW_SHARD
    flat_pw = lambda a: a.reshape(pw_rows, PG)
    x2, target, gf = x[0], loss_target[0], final_norm_g.reshape(1, D_MODEL)
    consts = {n: jnp.asarray(a, BF16 if n.startswith("tri") else F32) for n, a in _chunk_constants().items()}

    proj, h, w_in_g, w_out_slots, pw_slots = _in_proj(x2, norm1_g, _cast_w_in(w_in[0], place), place,
                                                      [w_out[0], flat_pw(pool_w)])
    (y_rec, o_raw, st_prev), ((w_out_g, pw_g),) = _rec_fwd(
        proj, lb_logits, rec_norm_g, consts, [_ex_gather([w_out_slots, pw_slots])])
    w_out_g = w_out_g.reshape(2 * D_MODEL, D_MODEL)
    pw_full = pw_g.reshape(N_SHARDS, N_GROUPS, PW_SHARD, PG).transpose(1, 0, 2, 3).reshape(N_GROUPS, PG, PG)
    y_pool = _pool_fwd(proj, pw_full, pool_scale)
    dout, dout_b, part_out = _out_proj_loss(y_pool, y_rec, w_out_g, x2, target, gf)

    p_out32, p_out16 = _grad_w_out(y_pool, y_rec, dout_b)
    (dpool, gpw, dscale), ((rb_out,),) = _pool_bwd(proj, dout_b, w_out_g, pw_full, pool_scale,
                                                   [_ex_send([p_out16], [SHARD_OWNERS])])
    g_out = _sum_units(p_out32, rb_out, place, SHARD_OWNERS, 256, "sum_w_out")
    gpw = gpw.reshape(N_GROUPS, N_SHARDS, PW_SHARD, PG).transpose(1, 0, 2, 3).reshape(N_SHARDS, pw_rows, PG)
    p_inp32, p_inp16 = _grad_w_in(h, dpool, [(0, 0), (0, 1), (1, 0), (1, 1)], "grad_w_in_pool", 1)

    pool_owners, a_owners, b_owners = (_block_owners(b) for b in BLOCK_GROUPS)
    rec_args = (proj, o_raw, st_prev, dout_b, w_out_g, lb_logits, rec_norm_g, consts)
    (drec_a, part_a), ((rb_inp,), (ra_pw,)) = _rec_bwd(
        *rec_args, 0, "rec_bwd_a", [_ex_send([p_inp16], [pool_owners], units=[(0, 1)]), _ex_swap([gpw])],
        gate_of=HALF_HEADS)
    p_pw32, p_pw16 = _add_units(gpw, ra_pw, place, 128, "add_pool_w")
    p_ina32, p_ina16 = _grad_w_in(h, drec_a, [(n, 0) for n in range(5)], "grad_w_in_a", 2)

    (drec_b, part_b), ((rb_inp, rb_ina, rb_pw),) = _rec_bwd(
        *rec_args, HALF_HEADS, "rec_bwd_b",
        [_ex_send([p_inp16, p_ina16, p_pw16], [pool_owners, a_owners, SHARD_OWNERS],
                  units=[(2, 3), tuple(range(len(a_owners))), SHARD_OWNERS], landed=[rb_inp, None, None])],
        own_gate=False)
    g_inp = _sum_units(p_inp32, rb_inp, place, pool_owners, 256, "sum_w_in_pool")
    g_ina = _sum_units(p_ina32, rb_ina, place, a_owners, 256, "sum_w_in_a")
    g_pw = _sum_units(p_pw32, rb_pw, place, SHARD_OWNERS, 128, "sum_pool_w")
    p_inb32, p_inb16 = _grad_w_in(h, drec_b, [(n, 0) for n in range(3)], "grad_w_in_b", 3)

    dproj = ([(dpool, 0, 0), (dpool, 0, 1), (dpool, 1, 0), (dpool, 1, 1)]
             + [(d, n, 0) for n in range(3) for d in (drec_a, drec_b)] + [(drec_a, 3, 0), (drec_a, 4, 0)])
    (dx, part_x), ((rb_inb,),) = _grad_x(dproj, w_in_g, x2, norm1_g, dout, [_ex_send([p_inb16], [b_owners])])
    g_inb = _sum_units(p_inb32, rb_inb, place, b_owners, 256, "sum_w_in_b")
    zero = jnp.zeros((1, D_MODEL), F32)
    part_rec = jnp.concatenate([part_a, part_b], axis=1)
    parts = jnp.concatenate([part_x[0:1], dscale, part_rec[1:2], zero, part_rec[0:1], part_out[0:1],
                             part_out[1:2], zero], axis=0)
    _, ((g_out, g_pw, g_inp, g_ina, g_inb), (gathered,)) = _call(
        None, name="join_halves",
        exchanges=[_ex_join([g_out, g_pw, g_inp, g_ina, g_inb],
                            [SHARD_OWNERS, SHARD_OWNERS, pool_owners, a_owners, b_owners]),
                   _ex_gather_small(parts)])

    group_of = np.zeros((D_PROJ // COL_BLK,), np.int32)
    index_of = np.zeros((D_PROJ // COL_BLK,), np.int32)
    for gi, blocks in enumerate(BLOCK_GROUPS):
        for i, j in enumerate(blocks):
            group_of[j], index_of[j] = gi, i
    per_shard = W_IN_SHARD // COL_BLK
    pick_in = jnp.stack([lax.dynamic_slice(jnp.asarray(group_of), (per_shard * chip,), (per_shard,)),
                         lax.dynamic_slice(jnp.asarray(index_of), (per_shard * chip,), (per_shard,))])
    pick_own = jnp.stack([jnp.zeros((1,), jnp.int32), chip.reshape(1).astype(jnp.int32)])
    big = [_adamw_units(w_in[0], m_w_in[0], v_w_in[0], [g_inp, g_ina, g_inb], pick_in, "adamw_w_in"),
           _adamw_units(w_out[0], m_w_out[0], v_w_out[0], [g_out], pick_own, "adamw_w_out"),
           _adamw_units(flat_pw(pool_w), flat_pw(m_pool_w), flat_pw(v_pool_w), [g_pw], pick_own, "adamw_pool_w")]

    row = lambda a: a.reshape(1, D_MODEL)
    loss, *small = _small_update(parts, gathered, [
        (norm1_g, m_norm1_g, v_norm1_g), (pool_scale, m_pool_scale, v_pool_scale),
        (lb_logits, m_lb_logits, v_lb_logits), (rec_norm_g, m_rec_norm_g, v_rec_norm_g),
        (row(final_norm_g), row(m_final_norm_g), row(v_final_norm_g))])

    def leaves(k):
        norm1, scale, lb, rec, final = (small[4 * i + k] for i in range(len(SMALL_ROWS)))
        return (norm1, big[0][k][None], big[2][k].reshape(pool_w.shape), scale, lb, rec,
                big[1][k][None], final.reshape(D_MODEL))

    return (loss.reshape(()), dx[None], *leaves(0), *leaves(1), *leaves(2), *leaves(3))
```

```python
import functools

import numpy as np
import jax
import jax.numpy as jnp
from jax import lax
from jax.experimental import pallas as pl
from jax.experimental.pallas import tpu as pltpu

F32 = jnp.float32
BF16 = jnp.bfloat16

SEQ = 2048
D_MODEL = 1024
D_PROJ = 6144
N_SEC = 6
N_GROUPS = 4
PG = 256
N_HEADS = 8
HEAD = 128
CHUNK = 64
N_LEVELS = 6
N_SHARDS = 4
W_IN_SHARD = D_PROJ // N_SHARDS
W_OUT_SHARD = 2048 // N_SHARDS
PW_SHARD = PG // N_SHARDS
COL_BLK = 512
EPS = 1e-6

ADAM_LR = 0.001
ADAM_B1 = 0.9
ADAM_B2 = 0.999
ADAM_EPS = 1e-08
ADAM_WD = 0.01
ADAM_STEP = 10

V7X_VMEM_LIMIT = 56 * 1024 * 1024
MESH = pl.DeviceIdType.MESH


def _params(**kw):
    return pltpu.CompilerParams(vmem_limit_bytes=V7X_VMEM_LIMIT, **kw)


def _sig(x):
    return 1.0 / (1.0 + jnp.exp(-x))


def _dot(a, b):
    return jnp.dot(a, b, preferred_element_type=F32)


def _dot_nt(a, b):
    return lax.dot_general(a, b, (((1,), (1,)), ((), ())), preferred_element_type=F32)


def _dot_tn(a, b):
    return lax.dot_general(a, b, (((0,), (0,)), ((), ())), preferred_element_type=F32)


def _split3(a):
    p1 = a.astype(BF16)
    r1 = a - p1.astype(F32)
    p2 = r1.astype(BF16)
    p3 = (r1 - p2.astype(F32)).astype(BF16)
    return jnp.concatenate([p1, p2, p3], axis=-1)


def _dot3(w01, a):
    n = a.shape[-1]
    r = _dot(w01, _split3(a))
    return r[:, :n] + r[:, n:2 * n] + r[:, 2 * n:]


def _chunk_constants():
    j = np.arange(CHUNK)
    tt, ss = np.meshgrid(j, j, indexing="ij")
    x = tt ^ ss
    hb = np.full((CHUNK, CHUNK), -1, np.int32)
    for l in range(N_LEVELS):
        hb[x >= (1 << l)] = l
    sym = np.stack([(hb == l) for l in range(N_LEVELS)]).astype(np.float32)
    low = sym * (tt > ss)
    sign = np.stack([np.where((j >> l) & 1, 1.0, -1.0) for l in range(N_LEVELS)]).astype(np.float32)
    sign = np.ascontiguousarray(np.broadcast_to(sign[:, :, None], (N_LEVELS, CHUNK, HEAD)))
    tri = (ss <= tt).astype(np.float32)
    return dict(tri=tri, tri_t=np.ascontiguousarray(tri.T), low=low,
                low_t=np.ascontiguousarray(low.transpose(0, 2, 1)), sym=sym, sign=sign)


def _in_proj(x, g1, w_slots, place, also_cast):
    n_col = D_PROJ // COL_BLK
    per_shard = W_IN_SHARD // COL_BLK
    rows = 1024
    half_rows = D_MODEL // 2
    quarter_rows = D_MODEL // 4
    FLIPS = (0, 2, 1, 3)
    ORDER = ([(0, p) for p in range(per_shard)] + [(m, p) for p in range(per_shard) for m in (1, 2)]
             + [(3, p) for p in range(per_shard)])

    def shard_at(m, chip):
        return chip ^ FLIPS[m]

    def pick(vals, t):
        return functools.reduce(lambda acc, iv: jnp.where(t == iv[0], iv[1], acc), list(enumerate(vals))[1:], vals[0])

    def body(place_ref, x_hbm, g_ref, w_in_ref, *rest):
        others, (proj_ref, h_ref, w_ref) = rest[:len(also_cast)], rest[len(also_cast):len(also_cast) + 3]
        slots = rest[len(also_cast) + 3:2 * len(also_cast) + 3]
        wbuf, load_sems, send_sems, recv_sems, x_ref, x_sem = rest[2 * len(also_cast) + 3:]
        t = pl.program_id(0)
        load_x = pltpu.make_async_copy(x_hbm, x_ref, x_sem)

        @pl.when(t == 1)
        def _():
            for src, dst in zip(others, slots):
                dst[...] = src[...].astype(BF16)
        x_, y_, c = _place()
        chip = 2 * x_ + y_
        me, other_core = (x_, y_, c), (x_, y_, 1 - c)
        x_nbr, y_nbr = (1 - x_, y_, c), (x_, 1 - y_, c)

        def rows_of(half, q=None):
            if q is None:
                return pl.ds(pl.multiple_of(half * half_rows, half_rows), half_rows)
            return pl.ds(pl.multiple_of(half * half_rows + q * quarter_rows, quarter_rows), quarter_rows)

        def block(m, p, r):
            return w_ref.at[shard_at(m, chip), p, r, :]

        def copy(k, ref, to):
            return _remote(ref, ref, send_sems, recv_sems, k, to)

        direct = lambda n, p, to: copy(3 * n + p, block(0, p, rows_of(c)), to)
        relay = lambda n, p, to: copy(6 + 3 * n + p, block(1 + n, p, rows_of(c, n)), to)
        arrived = lambda m, p: ([copy(3 * (m - 1) + p, block(m, p, rows_of(c)), me)] if m < 3 else
                                [copy(6 + 3 * n + p, block(3, p, rows_of(c, n)), me) for n in (0, 1)])
        passed_on = lambda m, p, half, to: copy(9 + 3 * m + p, block(m, p, rows_of(half)), to)

        def load(m, p, slot):
            return pltpu.make_async_copy(w_ref.at[shard_at(m, chip), p], wbuf.at[slot], load_sems.at[slot])

        def prepare(m, p):
            for cp in arrived(m, p):
                cp.wait_recv()
            passed_on(m, p, c, other_core).start()
            if m < 3:
                relay(m - 1, p, y_nbr if m == 1 else x_nbr).start()

        @pl.when(t == 0)
        def _():
            for p in range(per_shard):
                direct(0, p, x_nbr).start()
                direct(1, p, y_nbr).start()
            load_x.start()
            for p in range(per_shard):
                load(0, p, p).start()
            load_x.wait()

            def norm(i, _):
                r = pl.ds(pl.multiple_of(i * rows, rows), rows)
                xv = x_ref[r, :]
                inv = lax.rsqrt(jnp.mean(xv * xv, axis=-1, keepdims=True) + EPS)
                h_ref[r, :] = (xv * inv * g_ref[...]).astype(BF16)
                return 0
            lax.fori_loop(0, SEQ // rows, norm, 0)

        for step, (m, p) in enumerate(ORDER):
            @pl.when(t == step)
            def _(step=step, m=m, p=p):
                slot = step % per_shard
                if m > 0:
                    passed_on(m, p, 1 - c, me).wait_recv()
                    load(m, p, slot).start()
                if step + 1 < n_col and ORDER[step + 1][0] > 0:
                    prepare(*ORDER[step + 1])
                load(m, p, slot).wait()

                def mm(i, _):
                    r = pl.ds(pl.multiple_of(i * rows, rows), rows)
                    proj_ref[r, :] = _dot(h_ref[r, :], wbuf[slot])
                    return 0
                lax.fori_loop(0, SEQ // rows, mm, 0)

        @pl.when(t == n_col - 1)
        def _():
            for p in range(per_shard):
                sent = [direct(0, p, x_nbr), direct(1, p, y_nbr), relay(0, p, y_nbr), relay(1, p, x_nbr)]
                for cp in sent + [passed_on(m, p, c, other_core) for m in (1, 2, 3)]:
                    cp.wait_send()

    return pl.pallas_call(
        body, name="in_proj",
        grid_spec=pltpu.PrefetchScalarGridSpec(
            num_scalar_prefetch=1, grid=(n_col,),
            in_specs=[pl.BlockSpec(memory_space=pl.ANY),
                      pl.BlockSpec((1, D_MODEL), lambda t, p: (0, 0)),
                      pl.BlockSpec(memory_space=pl.ANY)]
                     + [pl.BlockSpec(a.shape, lambda t, p: (0, 0)) for a in also_cast],
            out_specs=[pl.BlockSpec((None, SEQ, COL_BLK),
                                    lambda t, p: (per_shard * (p[1] ^ pick([FLIPS[m] for m, _ in ORDER], t))
                                                  + pick([b for _, b in ORDER], t), 0, 0)),
                       pl.BlockSpec((SEQ, D_MODEL), lambda t, p: (0, 0)),
                       pl.BlockSpec(memory_space=pl.ANY)]
                      + [pl.BlockSpec((None,) + a.shape, lambda t, p: (p[1], 0, 0)) for a in also_cast],
            scratch_shapes=[pltpu.VMEM((per_shard, D_MODEL, COL_BLK), BF16),
                            pltpu.SemaphoreType.DMA((per_shard,)),
                            pltpu.SemaphoreType.DMA((21,)), pltpu.SemaphoreType.DMA((21,)),
                            pltpu.VMEM((SEQ, D_MODEL), F32), pltpu.SemaphoreType.DMA]),
        out_shape=[jax.ShapeDtypeStruct((n_col, SEQ, COL_BLK), F32),
                   jax.ShapeDtypeStruct((SEQ, D_MODEL), BF16),
                   jax.ShapeDtypeStruct(w_slots.shape, BF16)]
                  + [jax.ShapeDtypeStruct((N_SHARDS,) + a.shape, BF16) for a in also_cast],
        input_output_aliases={3: 2},
        compiler_params=_params(dimension_semantics=("arbitrary",)),
    )(place, x, g1, w_slots, *also_cast)


def _proj_cols(width, section, where, rows=SEQ):
    per_blk = COL_BLK // width

    def index(*grid):
        k, r = where(*grid)
        return section * (D_MODEL // COL_BLK) + k // per_blk, r, k % per_blk

    return pl.BlockSpec((None, rows, width), index)


POOL_ROWS = 256
POOL_HALO = 16
POOL_PAIR = 2


def _window_sums(ext, g, shift_of):
    s = ext
    for k in range(N_GROUPS):
        s = jnp.where(k <= g, s + pltpu.roll(s, shift_of(k), 0), s)
    return s


def _pool_diff(u_ref, i, g):
    n = POOL_ROWS + POOL_HALO
    r0 = i * POOL_ROWS
    cur = u_ref[pl.ds(pl.multiple_of(r0, POOL_ROWS), POOL_ROWS), :]
    before = u_ref[pl.ds(pl.multiple_of(jnp.maximum(r0 - POOL_HALO, 0), 8), POOL_HALO), :]
    before = jnp.where(i > 0, before, 0.0)
    ext = jnp.concatenate([before, cur], axis=0)
    s = _window_sums(ext, g, lambda k: 1 << k)[POOL_HALO:, :]
    t = r0 + lax.broadcasted_iota(jnp.int32, (POOL_ROWS, 1), 0)
    width = (2 << g).astype(F32)
    inv_count = 1.0 / jnp.minimum((t + 1).astype(F32), width)
    return s * inv_count - cur, inv_count


def _pool_fwd(proj, pw_g, pool_scale):
    def body(u_ref, gate_ref, pw_ref, sc_ref, y_ref):
        g = pl.program_id(0)

        def step(ii, _):
            chunks = [POOL_PAIR * ii + a for a in range(POOL_PAIR)]
            ds = [_pool_diff(u_ref, i, g)[0].astype(BF16) for i in chunks]
            mixed = [_dot(d, pw_ref[...]) for d in ds]
            for i, m in zip(chunks, mixed):
                r = pl.ds(pl.multiple_of(i * POOL_ROWS, POOL_ROWS), POOL_ROWS)
                gate = gate_ref[r, :]
                y_ref[r, :] = (m * sc_ref[...] * (gate * _sig(gate))).astype(BF16)
            return 0
        lax.fori_loop(0, SEQ // POOL_ROWS // POOL_PAIR, step, 0)

    return pl.pallas_call(
        body, name="pool_fwd", grid=(N_GROUPS,),
        in_specs=[_proj_cols(PG, 0, lambda g: (g, 0)), _proj_cols(PG, 1, lambda g: (g, 0)),
                  pl.BlockSpec((None, PG, PG), lambda g: (g, 0, 0)),
                  pl.BlockSpec((1, PG), lambda g: (0, g))],
        out_specs=pl.BlockSpec((SEQ, PG), lambda g: (0, g)),
        out_shape=jax.ShapeDtypeStruct((SEQ, D_MODEL), BF16),
        compiler_params=_params(dimension_semantics=("arbitrary",)),
    )(proj, proj, pw_g, pool_scale)


REC_ROWS = 1024
REC_CHUNKS = REC_ROWS // CHUNK
N_REC_BLK = SEQ // REC_ROWS
REC_GROUP = REC_CHUNKS
REC_GROUP_BWD = REC_CHUNKS
SEC_BLK = D_MODEL // HEAD


def _lower_bound(lb_ref):
    l0 = lb_ref[0:1, :]
    l1 = lb_ref[1:2, :]
    mx = jnp.maximum(l0, l1)
    e0 = jnp.exp(l0 - mx)
    e1 = jnp.exp(l1 - mx)
    return e0 / (e0 + e1)


def _gates(q, fl, lb):
    qs = q * _sig(q)
    sf = _sig(fl)
    f = lb + (1.0 - lb) * sf
    return qs, sf, f, 1.0 - f, jnp.log(f)


LOG2E = 1.4426950408889634


def _level_factors(g2, qs, k, sign_ref):
    t = lax.broadcasted_iota(jnp.int32, (CHUNK, HEAD), 0)
    row = lambda r, n: jnp.broadcast_to(g2[r:r + 1, :], (n, HEAD))
    out = []
    for l in range(N_LEVELS):
        m = 1 << l
        if l == 0:
            g_mid = jnp.where((t & 1) == 1, pltpu.roll(g2, 1, 0), g2)
        elif l == 1:
            low = (t & 7) < 4
            g_mid = jnp.concatenate([jnp.where(low[:8], row(8 * i + 1, 8), row(8 * i + 5, 8))
                                     for i in range(CHUNK // 8)], axis=0)
        else:
            g_mid = jnp.concatenate([row(b * 2 * m + m - 1, 2 * m) for b in range(CHUNK // (2 * m))], axis=0)
        sgn = sign_ref[l]
        up = sgn > 0.0
        e = jnp.exp2((g2 - g_mid) * sgn)
        x = jnp.where(up, qs, k) * e
        hi = x.astype(BF16)
        out.append((hi, (x - hi.astype(F32)).astype(BF16), e, up))
    return out


CHIP_FLIPS = ((1, 0), (0, 1), (1, 1))
HBM = pl.BlockSpec(memory_space=pl.ANY)


def _place():
    return lax.axis_index("x"), lax.axis_index("y"), lax.axis_index("c")


def _remote(src, dst, send_sems, recv_sems, k, to):
    return pltpu.make_async_remote_copy(src_ref=src, dst_ref=dst, send_sem=send_sems.at[k],
                                        recv_sem=recv_sems.at[k], device_id=to, device_id_type=MESH)


def _half_rows(ref, c):
    half = ref.shape[-2] // 2
    rows = pl.ds(pl.multiple_of(c * half, half), half)
    return ref.at[:, rows, :] if len(ref.shape) == 3 else ref.at[rows, :]


def _other_core_barrier():
    x, y, c = _place()
    sem = pltpu.get_barrier_semaphore()
    pl.semaphore_signal(sem, inc=1, device_id=(x, y, 1 - c), device_id_type=MESH)
    pl.semaphore_wait(sem, 1)


class _Exchange:
    def __init__(self, inputs, out_shapes, n_sems, start, finish, aliases=None):
        self.inputs, self.out_shapes, self.n_sems = list(inputs), list(out_shapes), n_sems
        self.start, self.finish, self.aliases = start, finish, dict(aliases or {})


def _ex_swap(grads):
    def copies(ins, outs, send, recv):
        x, y, c = _place()
        return [_remote(_half_rows(g, 1 - c), o, send, recv, t, (x, y, 1 - c))
                for t, (g, o) in enumerate(zip(ins, outs))]

    def start(*refs):
        for cp in copies(*refs):
            cp.start()

    def finish(*refs):
        cps = copies(*refs)
        for cp in cps:
            cp.wait_recv()
        for cp in cps:
            cp.wait_send()

    shapes = [jax.ShapeDtypeStruct((a.shape[0], a.shape[1] // 2, a.shape[2]), F32) for a in grads]
    return _Exchange(grads, shapes, len(grads), start, finish)


def _ex_send(parts16, owners, units=None, landed=None):
    n_t = len(parts16)
    units = units or [tuple(range(len(o))) for o in owners]
    landed = landed or [None] * n_t
    given = [t for t in range(n_t) if landed[t] is not None]

    def each(ins, outs, send, recv, to_sender, to_owner):
        x, y, c = _place()
        k = 0
        for t, own in enumerate(owners):
            for j in units[t]:
                for r, (fx, fy) in enumerate(CHIP_FLIPS):
                    tx, ty = x ^ fx, y ^ fy
                    cp = _remote(ins[t].at[j], outs[t].at[j, r], send, recv, k, (tx, ty, c))
                    if to_sender is not None:
                        pl.when(2 * tx + ty == own[j])(functools.partial(to_sender, cp))
                    if to_owner is not None:
                        pl.when(2 * x + y == own[j])(functools.partial(to_owner, cp))
                    k += 1

    def start(*refs):
        each(*refs, lambda cp: cp.start(), None)

    def finish(*refs):
        each(*refs, None, lambda cp: cp.wait_recv())
        each(*refs, lambda cp: cp.wait_send(), None)

    shapes = [jax.ShapeDtypeStruct((a.shape[0], len(CHIP_FLIPS)) + a.shape[1:], BF16) for a in parts16]
    return _Exchange(list(parts16) + [landed[t] for t in given], shapes,
                     len(CHIP_FLIPS) * sum(len(u) for u in units), start, finish,
                     aliases={n_t + i: t for i, t in enumerate(given)})


def _ex_join(units, owners):
    def each(ins, outs, send, recv, fn):
        x, y, c = _place()
        k = 0
        for t, own in enumerate(owners):
            for j, o in enumerate(own):
                def half(cc, to, u=outs[t].at[j], k=k):
                    return _remote(_half_rows(u, cc), _half_rows(u, cc), send, recv, k, to)
                mine = functools.partial(half, c, (x, y, 1 - c))
                theirs = functools.partial(half, 1 - c, (x, y, c))
                pl.when(2 * x + y == o)(functools.partial(fn, mine, theirs))
                k += 1

    def start(*refs):
        each(*refs, lambda mine, theirs: mine().start())

    def finish(*refs):
        each(*refs, lambda mine, theirs: theirs().wait_recv())
        each(*refs, lambda mine, theirs: mine().wait_send())

    shapes = [jax.ShapeDtypeStruct(a.shape, F32) for a in units]
    return _Exchange(units, shapes, sum(len(o) for o in owners), start, finish,
                     aliases={t: t for t in range(len(units))})


def _ex_gather(slots):
    n_t = len(slots)
    n_fl = len(CHIP_FLIPS)

    def piece(ref, shard, half):
        return _half_rows(ref.at[shard], half)

    def first(outs, send, recv):
        x, y, c = _place()
        s = 2 * x + y
        return [_remote(piece(outs[t], s, c), piece(outs[t], s, c), send, recv, n_t * j + t, (x ^ fx, y ^ fy, c))
                for j, (fx, fy) in enumerate(CHIP_FLIPS) for t in range(n_t)]

    def start(ins, outs, send, recv):
        for cp in first(outs, send, recv):
            cp.start()

    def finish(ins, outs, send, recv):
        x, y, c = _place()
        passed = []
        for j, (fx, fy) in enumerate(CHIP_FLIPS):
            sj = 2 * (x ^ fx) + (y ^ fy)
            for t in range(n_t):
                k = n_t * j + t
                _remote(piece(outs[t], sj, c), piece(outs[t], sj, c), send, recv, k, (x, y, c)).wait_recv()
                cp = _remote(piece(outs[t], sj, c), piece(outs[t], sj, c), send, recv, n_t * n_fl + k, (x, y, 1 - c))
                cp.start()
                passed.append(cp)
        for j, (fx, fy) in enumerate(CHIP_FLIPS):
            sj = 2 * (x ^ fx) + (y ^ fy)
            for t in range(n_t):
                k = n_t * n_fl + n_t * j + t
                _remote(piece(outs[t], sj, 1 - c), piece(outs[t], sj, 1 - c), send, recv, k, (x, y, c)).wait_recv()
        for cp in first(outs, send, recv) + passed:
            cp.wait_send()

    shapes = [jax.ShapeDtypeStruct(a.shape, BF16) for a in slots]
    return _Exchange(slots, shapes, 2 * n_t * n_fl, start, finish, aliases={t: t for t in range(n_t)})


def _ex_gather_small(parts):
    def copies(ins, outs, send, recv):
        x, y, c = _place()
        me = 4 * x + 2 * y + c
        return [_remote(ins[0], outs[0].at[me], send, recv, mask - 1,
                        (x ^ (mask >> 2), y ^ ((mask >> 1) & 1), c ^ (mask & 1))) for mask in range(1, 8)]

    def start(*refs):
        for cp in copies(*refs):
            cp.start()

    def finish(ins, outs, send, recv):
        x, y, c = _place()
        me = 4 * x + 2 * y + c
        for mask in range(1, 8):
            _remote(ins[0], outs[0].at[me ^ mask], send, recv, mask - 1, (x, y, c)).wait_recv()
        for cp in copies(ins, outs, send, recv):
            cp.wait_send()

    return _Exchange([parts], [jax.ShapeDtypeStruct((8,) + parts.shape, F32)], 7, start, finish)


def _call(body, *, name, args=(), in_specs=(), out_specs=(), out_shape=(), grid=(), scratch_shapes=(),
          exchanges=()):
    n_in, n_out, n_scr = len(args), len(out_shape), len(scratch_shapes)
    ex_in, ex_out, ex_scr, spans, alias = [], [], [], [], {}
    for ex in exchanges:
        spans.append((len(ex_in), len(ex.inputs), len(ex_out), len(ex.out_shapes)))
        for i, o in ex.aliases.items():
            alias[n_in + len(ex_in) + i] = n_out + len(ex_out) + o
        ex_in += ex.inputs
        ex_out += ex.out_shapes
        ex_scr += [pltpu.SemaphoreType.DMA((ex.n_sems,)), pltpu.SemaphoreType.DMA((ex.n_sems,))]

    def full(*refs):
        ins, x_in = refs[:n_in], refs[n_in:n_in + len(ex_in)]
        outs = refs[n_in + len(ex_in):n_in + len(ex_in) + n_out]
        x_out = refs[n_in + len(ex_in) + n_out:n_in + len(ex_in) + n_out + len(ex_out)]
        scr = refs[len(refs) - n_scr - len(ex_scr):len(refs) - len(ex_scr)]
        sems = refs[len(refs) - len(ex_scr):]

        def run(which):
            for e, (ex, (i0, ni, o0, no)) in enumerate(zip(exchanges, spans)):
                getattr(ex, which)(x_in[i0:i0 + ni], x_out[o0:o0 + no], sems[2 * e], sems[2 * e + 1])

        if grid:
            ids = [pl.program_id(a) for a in range(len(grid))]
            is_first = functools.reduce(jnp.logical_and, [i == 0 for i in ids])
            is_last = functools.reduce(jnp.logical_and, [i == g - 1 for i, g in zip(ids, grid)])
            pl.when(is_first)(lambda: run("start"))
            body(*ins, *outs, *scr)
            pl.when(is_last)(lambda: run("finish"))
        else:
            run("start")
            if body is not None:
                body(*ins, *outs, *scr)
            run("finish")

    kw = dict(grid=grid) if grid else {}
    if grid:
        kw["compiler_params"] = _params(dimension_semantics=("arbitrary",) * len(grid))
    else:
        kw["compiler_params"] = _params()
    res = pl.pallas_call(
        full, name=name,
        in_specs=list(in_specs) + [HBM] * len(ex_in),
        out_specs=list(out_specs) + [HBM] * len(ex_out),
        out_shape=list(out_shape) + ex_out,
        scratch_shapes=list(scratch_shapes) + ex_scr,
        input_output_aliases=alias, **kw,
    )(*args, *ex_in)
    own = list(res[:n_out])
    per_ex = [list(res[n_out + o0:n_out + o0 + no]) for (_, _, o0, no) in spans]
    return own, per_ex


def _cast_w_in(w, place):
    rows, cols = w.shape
    tile = rows

    def body(place_ref, w_ref, o_ref):
        o_ref[...] = w_ref[...].astype(BF16)

    return pl.pallas_call(
        body, name="cast_w_in",
        grid_spec=pltpu.PrefetchScalarGridSpec(
            num_scalar_prefetch=1, grid=(cols // COL_BLK, rows // tile),
            in_specs=[pl.BlockSpec((tile, COL_BLK), lambda b, i, p: (i, b))],
            out_specs=pl.BlockSpec((None, None, tile, COL_BLK), lambda b, i, p: (p[1], b, i, 0))),
        out_shape=jax.ShapeDtypeStruct((N_SHARDS, cols // COL_BLK, rows, COL_BLK), BF16),
        compiler_params=_params(dimension_semantics=("arbitrary", "arbitrary")),
    )(place, w)


def _rec_fwd(proj, lb_logits, rec_g, consts, exchanges):
    tri, low, sign = consts["tri"], consts["low"], consts["sign"]

    def body(q_ref, f_ref, i_ref, rg_ref, lb_ref, g_ref, w_ref, low_ref, sign_ref, y_ref, o_ref, stp_ref, st_ref):
        @pl.when(pl.program_id(1) == 0)
        def _():
            st_ref[...] = jnp.zeros_like(st_ref)
        lb = _lower_bound(lb_ref)
        st = st_ref[...]
        rows = lambda c: pl.ds(c * CHUNK, CHUNK)
        for c0 in range(0, REC_CHUNKS, REC_GROUP):
            group = range(c0, c0 + REC_GROUP)
            gated = [_gates(q_ref[rows(c), :], f_ref[rows(c), :], lb) for c in group]
            g2s = [_dot3(w_ref[...], g) * LOG2E for (_, _, _, _, g) in gated]
            xs = [[xl for xl, _, _, _ in _level_factors(g2, qs, k, sign_ref)]
                  for g2, (qs, _, _, k, _) in zip(g2s, gated)]
            a_s = []
            for x in xs:
                a = jnp.zeros((CHUNK, CHUNK), F32)
                for l, xl in enumerate(x):
                    a = a + _dot_nt(xl, xl) * low_ref[l]
                a_s.append(a.astype(BF16))
            vbs = [i_ref[rows(c), :].astype(BF16) for c in group]
            intra = [_dot(a, vb) for a, vb in zip(a_s, vbs)]
            kvs = [_dot_tn(vb, (k * jnp.exp2(g2[CHUNK - 1:CHUNK, :] - g2)).astype(BF16))
                   for vb, g2, (_, _, _, k, _) in zip(vbs, g2s, gated)]
            for i, c in enumerate(group):
                qs, _, _, k, _ = gated[i]
                g2 = g2s[i]
                stp_ref[c] = st
                v = i_ref[rows(c), :]
                rg = rg_ref[rows(c), :]
                o = (intra[i] + jnp.sum(qs * k, axis=-1, keepdims=True) * v
                     + _dot_nt((qs * jnp.exp2(g2)).astype(BF16), st.astype(BF16)))
                st = st * jnp.exp2(g2[CHUNK - 1:CHUNK, :]) + kvs[i]
                o_ref[rows(c), :] = o
                inv = lax.rsqrt(jnp.mean(o * o, axis=-1, keepdims=True) + EPS)
                y_ref[rows(c), :] = (o * inv * g_ref[...] * (rg * _sig(rg))).astype(BF16)
        st_ref[...] = st

    sec = lambda n: _proj_cols(HEAD, n, lambda h, b: (h, b), REC_ROWS)
    vec = lambda rows: pl.BlockSpec((rows, HEAD), lambda h, b: (0, h))
    full = lambda a: pl.BlockSpec(a.shape, lambda h, b: (0,) * a.ndim)
    return _call(
        body, name="rec_fwd", grid=(N_HEADS, N_REC_BLK),
        args=(proj, proj, proj, proj, lb_logits, rec_g, tri, low, sign),
        in_specs=[sec(2), sec(3), sec(4), sec(5), vec(2), vec(1), full(tri), full(low), full(sign)],
        out_specs=[pl.BlockSpec((REC_ROWS, HEAD), lambda h, b: (b, h)),
                   pl.BlockSpec((REC_ROWS, HEAD), lambda h, b: (b, h)),
                   pl.BlockSpec((None, REC_CHUNKS, HEAD, HEAD), lambda h, b: (h, b, 0, 0))],
        out_shape=[jax.ShapeDtypeStruct((SEQ, D_MODEL), BF16),
                   jax.ShapeDtypeStruct((SEQ, D_MODEL), F32),
                   jax.ShapeDtypeStruct((N_HEADS, SEQ // CHUNK, HEAD, HEAD), F32)],
        scratch_shapes=[pltpu.VMEM((HEAD, HEAD), F32)],
        exchanges=exchanges)


OUT_ROWS = 512


def _out_proj_loss(y_pool, y_rec, w_out_g, x, target, gf):
    def body(yp_ref, yr_ref, w_ref, x_ref, t_ref, gf_ref, dout_ref, doutb_ref, part_ref):
        @pl.when(pl.program_id(0) == 0)
        def _():
            part_ref[...] = jnp.zeros_like(part_ref)
        halves = [pl.ds(a * (OUT_ROWS // 2), OUT_ROWS // 2) for a in range(2)]
        outs = [x_ref[r, :] + _dot(yp_ref[r, :], w_ref[0:D_MODEL, :])
                + _dot(yr_ref[r, :], w_ref[D_MODEL:2 * D_MODEL, :]) for r in halves]
        gf_v = gf_ref[...]
        for r, out in zip(halves, outs):
            inv = lax.rsqrt(jnp.mean(out * out, axis=-1, keepdims=True) + EPS)
            diff = out * inv * gf_v - t_ref[r, :]
            dyf = diff * (1.0 / D_MODEL)
            a = dyf * gf_v
            dout = inv * a - out * (inv * inv * inv) * jnp.mean(a * out, axis=-1, keepdims=True)
            dout_ref[r, :] = dout
            doutb_ref[r, :] = dout.astype(BF16)
            part_ref[0:1, :] += jnp.sum(dyf * out * inv, axis=0, keepdims=True)
            part_ref[1:2, :] += jnp.sum(diff * diff, axis=0, keepdims=True)

    row = lambda n: pl.BlockSpec((OUT_ROWS, n), lambda i: (i, 0))
    return pl.pallas_call(
        body, name="out_proj_loss", grid=(SEQ // OUT_ROWS,),
        in_specs=[row(D_MODEL), row(D_MODEL), pl.BlockSpec((2 * D_MODEL, D_MODEL), lambda i: (0, 0)),
                  row(D_MODEL), row(D_MODEL), pl.BlockSpec((1, D_MODEL), lambda i: (0, 0))],
        out_specs=[row(D_MODEL), row(D_MODEL), pl.BlockSpec((8, D_MODEL), lambda i: (0, 0))],
        out_shape=[jax.ShapeDtypeStruct((SEQ, D_MODEL), F32),
                   jax.ShapeDtypeStruct((SEQ, D_MODEL), BF16),
                   jax.ShapeDtypeStruct((8, D_MODEL), F32)],
        compiler_params=_params(dimension_semantics=("arbitrary",)),
    )(y_pool, y_rec, w_out_g, x, target, gf)


def _grad_w_out(y_pool, y_rec, dout_b):
    blk = W_OUT_SHARD // 2
    per = D_MODEL // blk
    n = 2 * per

    def body(yp_ref, yr_ref, d_ref, p32_ref, p16_ref, send_ref, recv_ref, send_sems, recv_sems):
        j = pl.program_id(0)
        x, y, c = _place()

        def copy(u):
            return _remote(send_ref.at[u], recv_ref.at[u], send_sems, recv_sems, u, (x, y, 1 - c))

        pl.when(j == 0)(_other_core_barrier)
        for i in range(n):
            @pl.when(j == i)
            def _(i=i):
                res = _dot_tn((yp_ref if i < per else yr_ref)[...], d_ref[...])

                @pl.when(i % 2 == c)
                def _():
                    p32_ref[i // 2] = res

                @pl.when(i % 2 != c)
                def _():
                    send_ref[i // 2] = res
                    copy(i // 2).start()

        @pl.when(j == n - 1)
        def _():
            for u in range(N_SHARDS):
                copy(u).wait_recv()
                tot = p32_ref[u] + recv_ref[u]
                p32_ref[u] = tot
                p16_ref[u] = tot.astype(BF16)
            for u in range(N_SHARDS):
                copy(u).wait_send()

    whole = pl.BlockSpec((N_SHARDS, blk, D_MODEL), lambda j: (0, 0, 0))
    return pl.pallas_call(
        body, name="grad_w_out", grid=(n,),
        in_specs=[pl.BlockSpec((SEQ, blk), lambda j: (0, jnp.minimum(j, per - 1))),
                  pl.BlockSpec((SEQ, blk), lambda j: (0, jnp.maximum(j - per, 0))),
                  pl.BlockSpec((SEQ, D_MODEL), lambda j: (0, 0))],
        out_specs=[whole, whole],
        out_shape=[jax.ShapeDtypeStruct((N_SHARDS, blk, D_MODEL), F32),
                   jax.ShapeDtypeStruct((N_SHARDS, blk, D_MODEL), BF16)],
        scratch_shapes=[pltpu.VMEM((N_SHARDS, blk, D_MODEL), F32), pltpu.VMEM((N_SHARDS, blk, D_MODEL), F32),
                        pltpu.SemaphoreType.DMA((N_SHARDS,)), pltpu.SemaphoreType.DMA((N_SHARDS,))],
        compiler_params=_params(dimension_semantics=("arbitrary",), collective_id=0),
    )(y_pool, y_rec, dout_b)


def _pool_bwd(proj, dout_b, w_out_g, pw_g, pool_scale, exchanges):
    n = POOL_ROWS + POOL_HALO

    def body(u_ref, gate_ref, d_ref, wo_ref, pw_ref, sc_ref,
             dp_ref, dpw_ref, dsc_ref, dd_ref, ddw_ref):
        g = pl.program_id(0)
        dpw_ref[...] = jnp.zeros_like(dpw_ref)
        dsc_ref[...] = jnp.zeros_like(dsc_ref)

        def first(ii, _):
            chunks = [POOL_PAIR * ii + a for a in range(POOL_PAIR)]
            rs = [pl.ds(pl.multiple_of(i * POOL_ROWS, POOL_ROWS), POOL_ROWS) for i in chunks]
            diffs = [_pool_diff(u_ref, i, g) for i in chunks]
            dbs = [d.astype(BF16) for d, _ in diffs]
            mixed = [_dot(db, pw_ref[...]) for db in dbs]
            dys = [_dot_nt(d_ref[r, :], wo_ref[...]) for r in rs]
            sc = sc_ref[...]
            dmbs = []
            for r, m, dy in zip(rs, mixed, dys):
                gate = gate_ref[r, :]
                sg = _sig(gate)
                silu = gate * sg
                dp_ref[1, r, :] = (dy * m * sc * (sg * (1.0 + gate * (1.0 - sg)))).astype(BF16)
                dsc_ref[...] += jnp.sum(dy * silu * m, axis=0, keepdims=True)
                dmbs.append((dy * silu * sc).astype(BF16))
            for db, dmb in zip(dbs, dmbs):
                dpw_ref[...] += _dot_tn(db, dmb)
            dds = [_dot_nt(dmb, pw_ref[...]) for dmb in dmbs]
            for r, dd, (_, inv_count) in zip(rs, dds, diffs):
                dd_ref[r, :] = dd
                ddw_ref[r, :] = dd * inv_count
            return 0
        lax.fori_loop(0, SEQ // POOL_ROWS // POOL_PAIR, first, 0)

        def second(i, _):
            r0 = i * POOL_ROWS
            r = pl.ds(pl.multiple_of(r0, POOL_ROWS), POOL_ROWS)
            last = i == SEQ // POOL_ROWS - 1
            after = ddw_ref[pl.ds(pl.multiple_of(jnp.minimum(r0 + POOL_ROWS, SEQ - POOL_HALO), 8), POOL_HALO), :]
            after = jnp.where(last, 0.0, after)
            ext = jnp.concatenate([ddw_ref[r, :], after], axis=0)
            s = _window_sums(ext, g, lambda k: n - (1 << k))[:POOL_ROWS, :]
            dp_ref[0, r, :] = (s - dd_ref[r, :]).astype(BF16)
            return 0
        lax.fori_loop(0, SEQ // POOL_ROWS, second, 0)

    return _call(
        body, name="pool_bwd", grid=(N_GROUPS,),
        args=(proj, proj, dout_b, w_out_g, pw_g, pool_scale),
        in_specs=[_proj_cols(PG, 0, lambda g: (g, 0)), _proj_cols(PG, 1, lambda g: (g, 0)),
                  pl.BlockSpec((SEQ, D_MODEL), lambda g: (0, 0)),
                  pl.BlockSpec((PG, D_MODEL), lambda g: (g, 0)),
                  pl.BlockSpec((None, PG, PG), lambda g: (g, 0, 0)),
                  pl.BlockSpec((1, PG), lambda g: (0, g))],
        out_specs=[pl.BlockSpec((2, SEQ, PG), lambda g: (0, 0, g)),
                   pl.BlockSpec((None, PG, PG), lambda g: (g, 0, 0)),
                   pl.BlockSpec((1, PG), lambda g: (0, g))],
        out_shape=[jax.ShapeDtypeStruct((2, SEQ, D_MODEL), BF16),
                   jax.ShapeDtypeStruct((N_GROUPS, PG, PG), F32),
                   jax.ShapeDtypeStruct((1, D_MODEL), F32)],
        scratch_shapes=[pltpu.VMEM((SEQ, PG), F32), pltpu.VMEM((SEQ, PG), F32)],
        exchanges=exchanges)


HALF_HEADS = N_HEADS // 2
HALF_COLS = HALF_HEADS * HEAD


def _rec_bwd(proj, o_raw, st_prev, dout_b, w_out_g, lb_logits, rec_g, consts, h0, name, exchanges,
             gate_of=None, own_gate=True):
    n_sec = 4 if gate_of is None else 5

    def body(q_ref, f_ref, i_ref, rg_ref, o_ref, stp_ref, d_ref, wo_ref, lb_ref, g_ref,
             w_ref, lowt_ref, sym_ref, sign_ref, tri_ref, *rest):
        dr_ref, part_ref, dst_ref = rest[-3:]

        @pl.when(pl.program_id(1) == 0)
        def _():
            dst_ref[...] = jnp.zeros_like(dst_ref)
            part_ref[...] = jnp.zeros_like(part_ref)
        if gate_of is not None:
            rg2_ref, o2_ref, wo2_ref, g2_ref = rest[:4]
            rg, o = rg2_ref[...], o2_ref[...]
            sg = _sig(rg)
            inv = lax.rsqrt(jnp.mean(o * o, axis=-1, keepdims=True) + EPS)
            dy = _dot_nt(d_ref[...], wo2_ref[...])
            dr_ref[4] = (dy * (o * inv) * g2_ref[...] * (sg * (1.0 + rg * (1.0 - sg)))).astype(BF16)
        tril = (lax.broadcasted_iota(jnp.int32, (CHUNK, CHUNK), 0)
                > lax.broadcasted_iota(jnp.int32, (CHUNK, CHUNK), 1))
        lb = _lower_bound(lb_ref)
        grec = g_ref[...]
        dst = dst_ref[...]
        acc_grec = jnp.zeros((1, HEAD), F32)
        acc_lb = jnp.zeros((1, HEAD), F32)
        rows = lambda c: pl.ds(c * CHUNK, CHUNK)
        for c0 in reversed(range(0, REC_CHUNKS, REC_GROUP_BWD)):
            group = list(reversed(range(c0, c0 + REC_GROUP_BWD)))
            dys = [_dot_nt(d_ref[rows(c), :], wo_ref[...]) for c in group]
            dos = []
            for c, dy in zip(group, dys):
                rg = rg_ref[rows(c), :]
                o = o_ref[rows(c), :]
                sg = _sig(rg)
                silu = rg * sg
                inv = lax.rsqrt(jnp.mean(o * o, axis=-1, keepdims=True) + EPS)
                recn = o * inv
                if own_gate:
                    dr_ref[3, rows(c), :] = (dy * recn * grec * (sg * (1.0 + rg * (1.0 - sg)))).astype(BF16)
                acc_grec = acc_grec + jnp.sum(dy * silu * recn, axis=0, keepdims=True)
                drecn = dy * silu * grec
                dos.append(inv * drecn - o * (inv * inv * inv) * jnp.mean(drecn * o, axis=-1, keepdims=True))
            gated = [_gates(q_ref[rows(c), :], f_ref[rows(c), :], lb) for c in group]
            g2s = [_dot3(w_ref[...], g) * LOG2E for (_, _, _, _, g) in gated]
            levels = [_level_factors(g2, qs, k, sign_ref) for g2, (qs, _, _, k, _) in zip(g2s, gated)]
            a_ts = []
            for lev in levels:
                a_t = jnp.zeros((CHUNK, CHUNK), F32)
                for l, (xl, _, _, _) in enumerate(lev):
                    a_t = a_t + _dot_nt(xl, xl) * lowt_ref[l]
                a_ts.append(a_t.astype(BF16))
            dobs = [do.astype(BF16) for do in dos]
            vbs = [i_ref[rows(c), :].astype(BF16) for c in group]
            d_syms = [jnp.where(tril, _dot_nt(dob, vb), _dot_nt(vb, dob)) for dob, vb in zip(dobs, vbs)]
            dqs_is, dk_is = [], []
            for lev, d_sym in zip(levels, d_syms):
                dqs_i = jnp.zeros((CHUNK, HEAD), F32)
                both_i = jnp.zeros((CHUNK, HEAD), F32)
                for l, (xl, xlo, e, up) in enumerate(lev):
                    z = d_sym * sym_ref[l]
                    tmp = _dot(z.astype(BF16), jnp.concatenate([xl, xlo], axis=-1))
                    tmp = (tmp[:, :HEAD] + tmp[:, HEAD:]) * e
                    dqs_i = dqs_i + jnp.where(up, tmp, 0.0)
                    both_i = both_i + tmp
                dqs_is.append(dqs_i)
                dk_is.append(both_i - dqs_i)
            e_gs = [jnp.exp2(g2) for g2 in g2s]
            e_revs = [jnp.exp2(g2[CHUNK - 1:CHUNK, :] - g2) for g2 in g2s]
            e_lasts = [jnp.exp2(g2[CHUNK - 1:CHUNK, :]) for g2 in g2s]
            q_gs = [qs * e_g for (qs, _, _, _, _), e_g in zip(gated, e_gs)]
            kdecs = [k * e_rev for (_, _, _, k, _), e_rev in zip(gated, e_revs)]
            dv12 = [_dot(a_t, dob) + jnp.sum(qs * k, axis=-1, keepdims=True) * do
                    for a_t, dob, do, (qs, _, _, k, _) in zip(a_ts, dobs, dos, gated)]
            dq_gs = [_dot(dob, stp_ref[c].astype(BF16)) for c, dob in zip(group, dobs)]
            steps = [_dot_tn(dob, q_g.astype(BF16)) for dob, q_g in zip(dobs, q_gs)]
            dsts = []
            for e_last, step in zip(e_lasts, steps):
                dsts.append(dst)
                dst = dst * e_last + step
            dstbs = [d.astype(BF16) for d in dsts]
            dv3 = [_dot_nt(kdec.astype(BF16), dstb) for kdec, dstb in zip(kdecs, dstbs)]
            dkdecs = [_dot(vb, dstb) for vb, dstb in zip(vbs, dstbs)]
            dbig_gs, dg_lasts, dqss, dks = [], [], [], []
            for i, c in enumerate(group):
                qs, _, _, k, _ = gated[i]
                de_last = jnp.sum(stp_ref[c] * dsts[i], axis=0, keepdims=True)
                ddiag = jnp.sum(dos[i] * i_ref[rows(c), :], axis=-1, keepdims=True)
                dqss.append(dqs_is[i] + ddiag * k + dq_gs[i] * e_gs[i])
                dks.append(dk_is[i] + ddiag * qs + dkdecs[i] * e_revs[i])
                dg_rev = dkdecs[i] * kdecs[i]
                dg_lasts.append(jnp.sum(dg_rev, axis=0, keepdims=True) + de_last * e_lasts[i])
                dbig_gs.append(qs * dqs_is[i] - k * dk_is[i] + dq_gs[i] * q_gs[i] - dg_rev)
            dgs = [_dot3(tri_ref[...], dbig_g) + dg_last for dbig_g, dg_last in zip(dbig_gs, dg_lasts)]
            for i, c in enumerate(group):
                _, sf, f, _, _ = gated[i]
                q = q_ref[rows(c), :]
                df = dgs[i] / f - dks[i]
                dr_ref[1, rows(c), :] = (df * (1.0 - lb) * sf * (1.0 - sf)).astype(BF16)
                acc_lb = acc_lb + jnp.sum(df * (1.0 - sf), axis=0, keepdims=True)
                sq = _sig(q)
                dr_ref[0, rows(c), :] = (dqss[i] * (sq * (1.0 + q * (1.0 - sq)))).astype(BF16)
                dr_ref[2, rows(c), :] = (dv12[i] + dv3[i]).astype(BF16)
        dst_ref[...] = dst
        part_ref[0:1, :] += acc_grec
        part_ref[1:2, :] += acc_lb

    rev = lambda b: N_REC_BLK - 1 - b
    sec = lambda n: _proj_cols(HEAD, n, lambda h, b: (h0 + h, rev(b)), REC_ROWS)
    col_in = pl.BlockSpec((REC_ROWS, HEAD), lambda h, b: (rev(b), h0 + h))
    vec_in = lambda rows: pl.BlockSpec((rows, HEAD), lambda h, b: (0, h0 + h))
    full = lambda a: pl.BlockSpec(a.shape, lambda h, b: (0,) * a.ndim)
    extra_args, extra_specs = (), []
    if gate_of is not None:
        extra_args = (proj, o_raw, w_out_g, rec_g)
        extra_specs = [_proj_cols(HEAD, 5, lambda h, b: (gate_of + h, rev(b)), REC_ROWS),
                       pl.BlockSpec((REC_ROWS, HEAD), lambda h, b: (rev(b), gate_of + h)),
                       pl.BlockSpec((HEAD, D_MODEL), lambda h, b: (SEC_BLK + gate_of + h, 0)),
                       pl.BlockSpec((1, HEAD), lambda h, b: (0, gate_of + h))]
    return _call(
        body, name=name, grid=(HALF_HEADS, N_REC_BLK),
        args=(proj, proj, proj, proj, o_raw, st_prev, dout_b, w_out_g, lb_logits, rec_g,
              consts["tri"], consts["low_t"], consts["sym"], consts["sign"], consts["tri_t"]) + extra_args,
        in_specs=[sec(2), sec(3), sec(4), sec(5), col_in,
                  pl.BlockSpec((None, REC_CHUNKS, HEAD, HEAD), lambda h, b: (h0 + h, rev(b), 0, 0)),
                  pl.BlockSpec((REC_ROWS, D_MODEL), lambda h, b: (rev(b), 0)),
                  pl.BlockSpec((HEAD, D_MODEL), lambda h, b: (SEC_BLK + h0 + h, 0)),
                  vec_in(2), vec_in(1)] + [full(consts[n]) for n in ("tri", "low_t", "sym", "sign", "tri_t")]
                 + extra_specs,
        out_specs=[pl.BlockSpec((n_sec, REC_ROWS, HEAD), lambda h, b: (0, rev(b), h)),
                   pl.BlockSpec((8, HEAD), lambda h, b: (0, h))],
        out_shape=[jax.ShapeDtypeStruct((n_sec, SEQ, HALF_COLS), BF16),
                   jax.ShapeDtypeStruct((8, HALF_COLS), F32)],
        scratch_shapes=[pltpu.VMEM((HEAD, HEAD), F32)],
        exchanges=exchanges)


def _w_in_block(w_ref, j):
    per_shard = W_IN_SHARD // COL_BLK
    return w_ref[j // per_shard, j % per_shard]


def _grad_x(dproj, w_in_g, x, g1, dout, exchanges):
    rows = 512
    n_blk = len(dproj)

    def body(*refs):
        dp_refs = refs[:n_blk]
        w_ref, x_ref, g_ref, dout_ref, dx_ref, part_ref = refs[n_blk:]

        @pl.when(pl.program_id(0) == 0)
        def _():
            part_ref[...] = jnp.zeros_like(part_ref)
        dh = jnp.zeros((rows, D_MODEL), F32)
        for j in range(n_blk):
            dh = dh + _dot_nt(dp_refs[j][...], _w_in_block(w_ref, j))
        xv = x_ref[...]
        inv = lax.rsqrt(jnp.mean(xv * xv, axis=-1, keepdims=True) + EPS)
        a = dh * g_ref[...]
        dx_ref[...] = (dout_ref[...] + inv * a
                       - xv * (inv * inv * inv) * jnp.mean(a * xv, axis=-1, keepdims=True))
        part_ref[0:1, :] += jnp.sum(dh * xv * inv, axis=0, keepdims=True)

    row = lambda: pl.BlockSpec((rows, D_MODEL), lambda i: (i, 0))
    dp_spec = lambda sec, cb: pl.BlockSpec((None, rows, COL_BLK), lambda i: (sec, i, cb))
    return _call(
        body, name="grad_x", grid=(SEQ // rows,),
        args=tuple(a for a, _, _ in dproj) + (w_in_g, x, g1, dout),
        in_specs=[dp_spec(sec, cb) for _, sec, cb in dproj]
                 + [pl.BlockSpec(w_in_g.shape, lambda i: (0, 0, 0, 0)),
                    row(), pl.BlockSpec((1, D_MODEL), lambda i: (0, 0)), row()],
        out_specs=[row(), pl.BlockSpec((8, D_MODEL), lambda i: (0, 0))],
        out_shape=[jax.ShapeDtypeStruct((SEQ, D_MODEL), F32),
                   jax.ShapeDtypeStruct((8, D_MODEL), F32)],
        exchanges=exchanges)


def _grad_w_in(h, dp, blocks, name, collective_id):
    n_blk = len(blocks)
    half = D_MODEL // 2
    pick = lambda vals: (lambda j: functools.reduce(lambda acc, iv: jnp.where(j == iv[0], iv[1], acc),
                                                     list(enumerate(vals))[1:], vals[0]))
    sec_of = pick([sec for sec, _ in blocks])
    cb_of = pick([cb for _, cb in blocks])

    def body(h_ref, dp_ref, p32_ref, p16_ref, keep_ref, send_ref, recv_ref, send_sems, recv_sems):
        j = pl.program_id(0)
        x, y, c = _place()
        cols = lambda cc: pl.ds(pl.multiple_of(cc * half, half), half)

        def copy(i):
            return _remote(send_ref.at[i], recv_ref.at[i], send_sems, recv_sems, i, (x, y, 1 - c))

        pl.when(j == 0)(_other_core_barrier)
        for i in range(n_blk + 1):
            @pl.when(j == i)
            def _(i=i):
                if i < n_blk:
                    send_ref[i] = _dot_tn(h_ref[:, cols(1 - c)], dp_ref[...])
                    copy(i).start()
                    keep_ref[i] = _dot_tn(h_ref[:, cols(c)], dp_ref[...])
                if i > 0:
                    copy(i - 1).wait_recv()
                    tot = keep_ref[i - 1] + recv_ref[i - 1]
                    p32_ref[...] = tot
                    p16_ref[...] = tot.astype(BF16)

        @pl.when(j == n_blk)
        def _():
            for i in range(n_blk):
                copy(i).wait_send()

    lagged = pl.BlockSpec((None, half, COL_BLK), lambda j: (jnp.maximum(j - 1, 0), 0, 0))
    last = n_blk - 1
    return pl.pallas_call(
        body, name=name, grid=(n_blk + 1,),
        in_specs=[pl.BlockSpec((SEQ, D_MODEL), lambda j: (0, 0)),
                  pl.BlockSpec((None, SEQ, COL_BLK),
                               lambda j: (sec_of(jnp.minimum(j, last)), 0, cb_of(jnp.minimum(j, last))))],
        out_specs=[lagged, lagged],
        out_shape=[jax.ShapeDtypeStruct((n_blk, half, COL_BLK), F32),
                   jax.ShapeDtypeStruct((n_blk, half, COL_BLK), BF16)],
        scratch_shapes=[pltpu.VMEM((n_blk, half, COL_BLK), F32)] * 3
                       + [pltpu.SemaphoreType.DMA((n_blk,)), pltpu.SemaphoreType.DMA((n_blk,))],
        compiler_params=_params(dimension_semantics=("arbitrary",), collective_id=collective_id),
    )(h, dp)


def _add_units(grad, recv, place, tile, name):
    n, rows, cols = grad.shape
    per_half = rows // 2 // tile

    def body(place_ref, g_ref, r_ref, o32_ref, o16_ref):
        v = g_ref[...] + r_ref[...]
        o32_ref[...] = v
        o16_ref[...] = v.astype(BF16)

    blk = lambda f: pl.BlockSpec((None, tile, cols), f)
    out = lambda s, i, p: (s, i, 0)
    return pl.pallas_call(
        body, name=name,
        grid_spec=pltpu.PrefetchScalarGridSpec(
            num_scalar_prefetch=1, grid=(n, per_half),
            in_specs=[blk(lambda s, i, p: (s, p[0] * per_half + i, 0)), blk(out)],
            out_specs=[blk(out), blk(out)]),
        out_shape=[jax.ShapeDtypeStruct(recv.shape, F32), jax.ShapeDtypeStruct(recv.shape, BF16)],
        compiler_params=_params(dimension_semantics=("arbitrary", "arbitrary")),
    )(place, grad, recv)


def _sum_units(part32, recv16, place, owners, tile, name):
    n, half, cols = part32.shape
    per_half = half // tile
    table = np.array([[sum(o == chip for o in owners)] + sorted(range(n), key=lambda j: (owners[j] != chip, j))
                      for chip in range(N_SHARDS)], np.int32)
    sched = jnp.concatenate([place[:1], jnp.asarray(table)[place[1]]])

    def block(k, i, p):
        live = k < p[1]
        unit = p[2 + jnp.minimum(k, jnp.maximum(p[1] - 1, 0))]
        return unit, jnp.where(live, i, per_half - 1)

    def body(sched_ref, p_ref, r_ref, o_ref):
        @pl.when(pl.program_id(0) < sched_ref[1])
        def _():
            acc = p_ref[...]
            for j in range(len(CHIP_FLIPS)):
                acc = acc + r_ref[j].astype(F32)
            o_ref[...] = acc

    return pl.pallas_call(
        body, name=name,
        grid_spec=pltpu.PrefetchScalarGridSpec(
            num_scalar_prefetch=1, grid=(n, per_half),
            in_specs=[pl.BlockSpec((None, tile, cols), lambda k, i, p: (*block(k, i, p), 0)),
                      pl.BlockSpec((None, len(CHIP_FLIPS), tile, cols),
                                   lambda k, i, p: (block(k, i, p)[0], 0, block(k, i, p)[1], 0))],
            out_specs=pl.BlockSpec((None, tile, cols),
                                   lambda k, i, p: (block(k, i, p)[0], p[0] * per_half + block(k, i, p)[1], 0))),
        out_shape=jax.ShapeDtypeStruct((n, 2 * half, cols), F32),
        compiler_params=_params(dimension_semantics=("arbitrary", "arbitrary")),
    )(sched, part32, recv16)


def _adamw_math(w, g, m, v):
    m = ADAM_B1 * m + (1.0 - ADAM_B1) * g
    v = ADAM_B2 * v + (1.0 - ADAM_B2) * (g * g)
    m_hat = m / (1.0 - ADAM_B1 ** ADAM_STEP)
    v_hat = v / (1.0 - ADAM_B2 ** ADAM_STEP)
    delta = -ADAM_LR * (m_hat / (jnp.sqrt(v_hat) + ADAM_EPS) + ADAM_WD * w)
    return delta, m, v


def _adamw_units(w, m, v, grads, pick, name):
    rows, cols = w.shape
    bc = grads[0].shape[-1]
    tile = min(rows, 256)
    n_g = len(grads)

    def body(pick_ref, w_ref, m_ref, v_ref, *refs):
        g_refs, (g_out, d_ref, nm_ref, nv_ref) = refs[:n_g], refs[n_g:]
        p = pl.program_id(0)
        for a in range(n_g):
            @pl.when(pick_ref[0, p] == a)
            def _(a=a):
                g = g_refs[a][...]
                g_out[...] = g
                d_ref[...], nm_ref[...], nv_ref[...] = _adamw_math(w_ref[...], g, m_ref[...], v_ref[...])

    blk = pl.BlockSpec((tile, bc), lambda p, i, pick: (i, p))

    def g_spec(a):
        return pl.BlockSpec((None, tile, bc),
                            lambda p, i, pick: (jnp.where(pick[0, p] == a, pick[1, p], 0),
                                                jnp.where(pick[0, p] == a, i, 0), 0))

    return pl.pallas_call(
        body, name=name,
        grid_spec=pltpu.PrefetchScalarGridSpec(
            num_scalar_prefetch=1, grid=(cols // bc, rows // tile),
            in_specs=[blk] * 3 + [g_spec(a) for a in range(n_g)],
            out_specs=[blk] * 4),
        out_shape=[jax.ShapeDtypeStruct(w.shape, F32)] * 4,
        compiler_params=_params(dimension_semantics=("arbitrary", "arbitrary")),
    )(pick, w, m, v, *grads)


ROW_NORM1, ROW_SCALE, ROW_LB, ROW_REC, ROW_FINAL, ROW_LOSS = 0, 1, 2, 4, 5, 6


SMALL_ROWS = (ROW_NORM1, ROW_SCALE, ROW_LB, ROW_REC, ROW_FINAL)


def _small_update(parts, gathered, params):
    n_p = len(params)

    def body(own_ref, p_ref, *refs):
        ins, loss_ref, outs = refs[:3 * n_p], refs[3 * n_p], refs[3 * n_p + 1:]
        x, y, c = _place()
        me = 4 * x + 2 * y + c
        slot = lambda d: jnp.where(me == d, own_ref[...], p_ref[d])
        tot = slot(0)
        for d in range(1, 8):
            tot = tot + slot(d)
        for i, r in enumerate(SMALL_ROWS):
            w = ins[3 * i][...]
            g = tot[r:r + 1, :]
            if r == ROW_LB:
                mx = jnp.maximum(w[0:1, :], w[1:2, :])
                e0 = jnp.exp(w[0:1, :] - mx)
                e1 = jnp.exp(w[1:2, :] - mx)
                lb = e0 / (e0 + e1)
                g = g * lb * (1.0 - lb)
                g = jnp.concatenate([g, -g], axis=0)
            outs[4 * i][...] = g
            outs[4 * i + 1][...], outs[4 * i + 2][...], outs[4 * i + 3][...] = _adamw_math(
                w, g, ins[3 * i + 1][...], ins[3 * i + 2][...])
        loss_ref[...] = (0.5 / D_MODEL) * jnp.sum(tot[ROW_LOSS:ROW_LOSS + 1, :], axis=-1, keepdims=True)

    flat = [a for wmv in params for a in wmv]
    return pl.pallas_call(
        body, name="small_update",
        out_shape=[jax.ShapeDtypeStruct((1, 1), F32)]
                  + [jax.ShapeDtypeStruct(w.shape, F32) for w, _, _ in params for _ in range(4)],
        compiler_params=_params(),
    )(parts, gathered, *flat)


SHARD_OWNERS = tuple(range(N_SHARDS))
BLOCKS_POOL = (0, 1, 2, 3)
BLOCKS_A = (4, 6, 8, 10, 11)
BLOCKS_B = (5, 7, 9)
BLOCK_GROUPS = (BLOCKS_POOL, BLOCKS_A, BLOCKS_B)


def _block_owners(blocks):
    return tuple(j // (W_IN_SHARD // COL_BLK) for j in blocks)


def kernel(x, norm1_g, w_in, pool_w, pool_scale, lb_logits, rec_norm_g, w_out, final_norm_g, loss_target, m_norm1_g, m_w_in, m_pool_w, m_pool_scale, m_lb_logits, m_rec_norm_g, m_w_out, m_final_norm_g, v_norm1_g, v_w_in, v_pool_w, v_pool_scale, v_lb_logits, v_rec_norm_g, v_w_out, v_final_norm_g):
    xi, yi, ci = _place()
    chip = 2 * xi + yi
    place = jnp.stack([ci, chip]).astype(jnp.int32)
    pw_rows = N_GROUPS * PW_SHARD
    flat_pw = lambda a: a.reshape(pw_rows, PG)
    x2, target, gf = x[0], loss_target[0], final_norm_g.reshape(1, D_MODEL)
    consts = {n: jnp.asarray(a, BF16 if n.startswith("tri") else F32) for n, a in _chunk_constants().items()}

    proj, h, w_in_g, w_out_slots, pw_slots = _in_proj(x2, norm1_g, _cast_w_in(w_in[0], place), place,
                                                      [w_out[0], flat_pw(pool_w)])
    (y_rec, o_raw, st_prev), ((w_out_g, pw_g),) = _rec_fwd(
        proj, lb_logits, rec_norm_g, consts, [_ex_gather([w_out_slots, pw_slots])])
    w_out_g = w_out_g.reshape(2 * D_MODEL, D_MODEL)
    pw_full = pw_g.reshape(N_SHARDS, N_GROUPS, PW_SHARD, PG).transpose(1, 0, 2, 3).reshape(N_GROUPS, PG, PG)
    y_pool = _pool_fwd(proj, pw_full, pool_scale)
    dout, dout_b, part_out = _out_proj_loss(y_pool, y_rec, w_out_g, x2, target, gf)

    p_out32, p_out16 = _grad_w_out(y_pool, y_rec, dout_b)
    (dpool, gpw, dscale), ((rb_out,),) = _pool_bwd(proj, dout_b, w_out_g, pw_full, pool_scale,
                                                   [_ex_send([p_out16], [SHARD_OWNERS])])
    g_out = _sum_units(p_out32, rb_out, place, SHARD_OWNERS, 256, "sum_w_out")
    gpw = gpw.reshape(N_GROUPS, N_SHARDS, PW_SHARD, PG).transpose(1, 0, 2, 3).reshape(N_SHARDS, pw_rows, PG)
    p_inp32, p_inp16 = _grad_w_in(h, dpool, [(0, 0), (0, 1), (1, 0), (1, 1)], "grad_w_in_pool", 1)

    pool_owners, a_owners, b_owners = (_block_owners(b) for b in BLOCK_GROUPS)
    rec_args = (proj, o_raw, st_prev, dout_b, w_out_g, lb_logits, rec_norm_g, consts)
    (drec_a, part_a), ((rb_inp,), (ra_pw,)) = _rec_bwd(
        *rec_args, 0, "rec_bwd_a", [_ex_send([p_inp16], [pool_owners], units=[(0, 1)]), _ex_swap([gpw])],
        gate_of=HALF_HEADS)
    p_pw32, p_pw16 = _add_units(gpw, ra_pw, place, 128, "add_pool_w")
    p_ina32, p_ina16 = _grad_w_in(h, drec_a, [(n, 0) for n in range(5)], "grad_w_in_a", 2)

    (drec_b, part_b), ((rb_inp, rb_ina, rb_pw),) = _rec_bwd(
        *rec_args, HALF_HEADS, "rec_bwd_b",
        [_ex_send([p_inp16, p_ina16, p_pw16], [pool_owners, a_owners, SHARD_OWNERS],
                  units=[(2, 3), tuple(range(len(a_owners))), SHARD_OWNERS], landed=[rb_inp, None, None])],
        own_gate=False)
    g_inp = _sum_units(p_inp32, rb_inp, place, pool_owners, 256, "sum_w_in_pool")
    g_ina = _sum_units(p_ina32, rb_ina, place, a_owners, 256, "sum_w_in_a")
    g_pw = _sum_units(p_pw32, rb_pw, place, SHARD_OWNERS, 128, "sum_pool_w")
    p_inb32, p_inb16 = _grad_w_in(h, drec_b, [(n, 0) for n in range(3)], "grad_w_in_b", 3)

    dproj = ([(dpool, 0, 0), (dpool, 0, 1), (dpool, 1, 0), (dpool, 1, 1)]
             + [(d, n, 0) for n in range(3) for d in (drec_a, drec_b)] + [(drec_a, 3, 0), (drec_a, 4, 0)])
    (dx, part_x), ((rb_inb,),) = _grad_x(dproj, w_in_g, x2, norm1_g, dout, [_ex_send([p_inb16], [b_owners])])
    g_inb = _sum_units(p_inb32, rb_inb, place, b_owners, 256, "sum_w_in_b")
    zero = jnp.zeros((1, D_MODEL), F32)
    part_rec = jnp.concatenate([part_a, part_b], axis=1)
    parts = jnp.concatenate([part_x[0:1], dscale, part_rec[1:2], zero, part_rec[0:1], part_out[0:1],
                             part_out[1:2], zero], axis=0)
    _, ((g_out, g_pw, g_inp, g_ina, g_inb), (gathered,)) = _call(
        None, name="join_halves",
        exchanges=[_ex_join([g_out, g_pw, g_inp, g_ina, g_inb],
                            [SHARD_OWNERS, SHARD_OWNERS, pool_owners, a_owners, b_owners]),
                   _ex_gather_small(parts)])

    group_of = np.zeros((D_PROJ // COL_BLK,), np.int32)
    index_of = np.zeros((D_PROJ // COL_BLK,), np.int32)
    for gi, blocks in enumerate(BLOCK_GROUPS):
        for i, j in enumerate(blocks):
            group_of[j], index_of[j] = gi, i
    per_shard = W_IN_SHARD // COL_BLK
    pick_in = jnp.stack([lax.dynamic_slice(jnp.asarray(group_of), (per_shard * chip,), (per_shard,)),
                         lax.dynamic_slice(jnp.asarray(index_of), (per_shard * chip,), (per_shard,))])
    pick_own = jnp.stack([jnp.zeros((1,), jnp.int32), chip.reshape(1).astype(jnp.int32)])
    big = [_adamw_units(w_in[0], m_w_in[0], v_w_in[0], [g_inp, g_ina, g_inb], pick_in, "adamw_w_in"),
           _adamw_units(w_out[0], m_w_out[0], v_w_out[0], [g_out], pick_own, "adamw_w_out"),
           _adamw_units(flat_pw(pool_w), flat_pw(m_pool_w), flat_pw(v_pool_w), [g_pw], pick_own, "adamw_pool_w")]

    row = lambda a: a.reshape(1, D_MODEL)
    loss, *small = _small_update(parts, gathered, [
        (norm1_g, m_norm1_g, v_norm1_g), (pool_scale, m_pool_scale, v_pool_scale),
        (lb_logits, m_lb_logits, v_lb_logits), (rec_norm_g, m_rec_norm_g, v_rec_norm_g),
        (row(final_norm_g), row(m_final_norm_g), row(v_final_norm_g))])

    def leaves(k):
        norm1, scale, lb, rec, final = (small[4 * i + k] for i in range(len(SMALL_ROWS)))
        return (norm1, big[0][k][None], big[2][k].reshape(pool_w.shape), scale, lb, rec,
                big[1][k][None], final.reshape(D_MODEL))

    return (loss.reshape(()), dx[None], *leaves(0), *leaves(1), *leaves(2), *leaves(3))
```

```python
import functools

import numpy as np
import jax
import jax.numpy as jnp
from jax import lax
from jax.experimental import pallas as pl
from jax.experimental.pallas import tpu as pltpu

F32 = jnp.float32
BF16 = jnp.bfloat16

SEQ = 2048
D_MODEL = 1024
D_PROJ = 6144
N_SEC = 6
N_GROUPS = 4
PG = 256
N_HEADS = 8
HEAD = 128
CHUNK = 64
N_LEVELS = 6
N_SHARDS = 4
W_IN_SHARD = D_PROJ // N_SHARDS
W_OUT_SHARD = 2048 // N_SHARDS
PW_SHARD = PG // N_SHARDS
COL_BLK = 512
EPS = 1e-6

ADAM_LR = 0.001
ADAM_B1 = 0.9
ADAM_B2 = 0.999
ADAM_EPS = 1e-08
ADAM_WD = 0.01
ADAM_STEP = 10

V7X_VMEM_LIMIT = 56 * 1024 * 1024
MESH = pl.DeviceIdType.MESH


def _params(**kw):
    return pltpu.CompilerParams(vmem_limit_bytes=V7X_VMEM_LIMIT, **kw)


def _sig(x):
    return 1.0 / (1.0 + jnp.exp(-x))


def _dot(a, b):
    return jnp.dot(a, b, preferred_element_type=F32)


def _dot_nt(a, b):
    return lax.dot_general(a, b, (((1,), (1,)), ((), ())), preferred_element_type=F32)


def _dot_tn(a, b):
    return lax.dot_general(a, b, (((0,), (0,)), ((), ())), preferred_element_type=F32)


def _split3(a):
    p1 = a.astype(BF16)
    r1 = a - p1.astype(F32)
    p2 = r1.astype(BF16)
    p3 = (r1 - p2.astype(F32)).astype(BF16)
    return jnp.concatenate([p1, p2, p3], axis=-1)


def _dot3(w01, a):
    n = a.shape[-1]
    r = _dot(w01, _split3(a))
    return r[:, :n] + r[:, n:2 * n] + r[:, 2 * n:]


def _chunk_constants():
    j = np.arange(CHUNK)
    tt, ss = np.meshgrid(j, j, indexing="ij")
    x = tt ^ ss
    hb = np.full((CHUNK, CHUNK), -1, np.int32)
    for l in range(N_LEVELS):
        hb[x >= (1 << l)] = l
    sym = np.stack([(hb == l) for l in range(N_LEVELS)]).astype(np.float32)
    low = sym * (tt > ss)
    sign = np.stack([np.where((j >> l) & 1, 1.0, -1.0) for l in range(N_LEVELS)]).astype(np.float32)
    sign = np.ascontiguousarray(np.broadcast_to(sign[:, :, None], (N_LEVELS, CHUNK, HEAD)))
    tri = (ss <= tt).astype(np.float32)
    return dict(tri=tri, tri_t=np.ascontiguousarray(tri.T), low=low,
                low_t=np.ascontiguousarray(low.transpose(0, 2, 1)), sym=sym, sign=sign)


def _in_proj(x, g1, w_slots, place, also_cast):
    n_col = D_PROJ // COL_BLK
    per_shard = W_IN_SHARD // COL_BLK
    rows = 1024
    half_rows = D_MODEL // 2
    quarter_rows = D_MODEL // 4
    FLIPS = (0, 2, 1, 3)
    ORDER = ([(0, p) for p in range(per_shard)] + [(m, p) for p in range(per_shard) for m in (1, 2)]
             + [(3, p) for p in range(per_shard)])

    def shard_at(m, chip):
        return chip ^ FLIPS[m]

    def pick(vals, t):
        return functools.reduce(lambda acc, iv: jnp.where(t == iv[0], iv[1], acc), list(enumerate(vals))[1:], vals[0])

    def body(place_ref, x_hbm, g_ref, w_in_ref, *rest):
        others, (proj_ref, h_ref, w_ref) = rest[:len(also_cast)], rest[len(also_cast):len(also_cast) + 3]
        slots = rest[len(also_cast) + 3:2 * len(also_cast) + 3]
        wbuf, load_sems, send_sems, recv_sems, x_ref, x_sem = rest[2 * len(also_cast) + 3:]
        t = pl.program_id(0)
        load_x = pltpu.make_async_copy(x_hbm, x_ref, x_sem)

        @pl.when(t == 1)
        def _():
            for src, dst in zip(others, slots):
                dst[...] = src[...].astype(BF16)
        x_, y_, c = _place()
        chip = 2 * x_ + y_
        me, other_core = (x_, y_, c), (x_, y_, 1 - c)
        x_nbr, y_nbr = (1 - x_, y_, c), (x_, 1 - y_, c)

        def rows_of(half, q=None):
            if q is None:
                return pl.ds(pl.multiple_of(half * half_rows, half_rows), half_rows)
            return pl.ds(pl.multiple_of(half * half_rows + q * quarter_rows, quarter_rows), quarter_rows)

        def block(m, p, r):
            return w_ref.at[shard_at(m, chip), p, r, :]

        def copy(k, ref, to):
            return _remote(ref, ref, send_sems, recv_sems, k, to)

        direct = lambda n, p, to: copy(3 * n + p, block(0, p, rows_of(c)), to)
        relay = lambda n, p, to: copy(6 + 3 * n + p, block(1 + n, p, rows_of(c, n)), to)
        arrived = lambda m, p: ([copy(3 * (m - 1) + p, block(m, p, rows_of(c)), me)] if m < 3 else
                                [copy(6 + 3 * n + p, block(3, p, rows_of(c, n)), me) for n in (0, 1)])
        passed_on = lambda m, p, half, to: copy(9 + 3 * m + p, block(m, p, rows_of(half)), to)

        def load(m, p, slot):
            return pltpu.make_async_copy(w_ref.at[shard_at(m, chip), p], wbuf.at[slot], load_sems.at[slot])

        def prepare(m, p):
            for cp in arrived(m, p):
                cp.wait_recv()
            passed_on(m, p, c, other_core).start()
            if m < 3:
                relay(m - 1, p, y_nbr if m == 1 else x_nbr).start()

        @pl.when(t == 0)
        def _():
            for p in range(per_shard):
                direct(0, p, x_nbr).start()
                direct(1, p, y_nbr).start()
            load_x.start()
            for p in range(per_shard):
                load(0, p, p).start()
            load_x.wait()

            def norm(i, _):
                r = pl.ds(pl.multiple_of(i * rows, rows), rows)
                xv = x_ref[r, :]
                inv = lax.rsqrt(jnp.mean(xv * xv, axis=-1, keepdims=True) + EPS)
                h_ref[r, :] = (xv * inv * g_ref[...]).astype(BF16)
                return 0
            lax.fori_loop(0, SEQ // rows, norm, 0)

        for step, (m, p) in enumerate(ORDER):
            @pl.when(t == step)
            def _(step=step, m=m, p=p):
                slot = step % per_shard
                if m > 0:
                    passed_on(m, p, 1 - c, me).wait_recv()
                    load(m, p, slot).start()
                if step + 1 < n_col and ORDER[step + 1][0] > 0:
                    prepare(*ORDER[step + 1])
                load(m, p, slot).wait()

                def mm(i, _):
                    r = pl.ds(pl.multiple_of(i * rows, rows), rows)
                    proj_ref[r, :] = _dot(h_ref[r, :], wbuf[slot])
                    return 0
                lax.fori_loop(0, SEQ // rows, mm, 0)

        @pl.when(t == n_col - 1)
        def _():
            for p in range(per_shard):
                sent = [direct(0, p, x_nbr), direct(1, p, y_nbr), relay(0, p, y_nbr), relay(1, p, x_nbr)]
                for cp in sent + [passed_on(m, p, c, other_core) for m in (1, 2, 3)]:
                    cp.wait_send()

    return pl.pallas_call(
        body, name="in_proj",
        grid_spec=pltpu.PrefetchScalarGridSpec(
            num_scalar_prefetch=1, grid=(n_col,),
            in_specs=[pl.BlockSpec(memory_space=pl.ANY),
                      pl.BlockSpec((1, D_MODEL), lambda t, p: (0, 0)),
                      pl.BlockSpec(memory_space=pl.ANY)]
                     + [pl.BlockSpec(a.shape, lambda t, p: (0, 0)) for a in also_cast],
            out_specs=[pl.BlockSpec((None, SEQ, COL_BLK),
                                    lambda t, p: (per_shard * (p[1] ^ pick([FLIPS[m] for m, _ in ORDER], t))
                                                  + pick([b for _, b in ORDER], t), 0, 0)),
                       pl.BlockSpec((SEQ, D_MODEL), lambda t, p: (0, 0)),
                       pl.BlockSpec(memory_space=pl.ANY)]
                      + [pl.BlockSpec((None,) + a.shape, lambda t, p: (p[1], 0, 0)) for a in also_cast],
            scratch_shapes=[pltpu.VMEM((per_shard, D_MODEL, COL_BLK), BF16),
                            pltpu.SemaphoreType.DMA((per_shard,)),
                            pltpu.SemaphoreType.DMA((21,)), pltpu.SemaphoreType.DMA((21,)),
                            pltpu.VMEM((SEQ, D_MODEL), F32), pltpu.SemaphoreType.DMA]),
        out_shape=[jax.ShapeDtypeStruct((n_col, SEQ, COL_BLK), F32),
                   jax.ShapeDtypeStruct((SEQ, D_MODEL), BF16),
                   jax.ShapeDtypeStruct(w_slots.shape, BF16)]
                  + [jax.ShapeDtypeStruct((N_SHARDS,) + a.shape, BF16) for a in also_cast],
        input_output_aliases={3: 2},
        compiler_params=_params(dimension_semantics=("arbitrary",)),
    )(place, x, g1, w_slots, *also_cast)


def _proj_cols(width, section, where, rows=SEQ):
    per_blk = COL_BLK // width

    def index(*grid):
        k, r = where(*grid)
        return section * (D_MODEL // COL_BLK) + k // per_blk, r, k % per_blk

    return pl.BlockSpec((None, rows, width), index)


POOL_ROWS = 256
POOL_HALO = 16
POOL_PAIR = 2


def _window_sums(ext, g, shift_of):
    s = ext
    for k in range(N_GROUPS):
        s = jnp.where(k <= g, s + pltpu.roll(s, shift_of(k), 0), s)
    return s


def _pool_diff(u_ref, i, g):
    n = POOL_ROWS + POOL_HALO
    r0 = i * POOL_ROWS
    cur = u_ref[pl.ds(pl.multiple_of(r0, POOL_ROWS), POOL_ROWS), :]
    before = u_ref[pl.ds(pl.multiple_of(jnp.maximum(r0 - POOL_HALO, 0), 8), POOL_HALO), :]
    before = jnp.where(i > 0, before, 0.0)
    ext = jnp.concatenate([before, cur], axis=0)
    s = _window_sums(ext, g, lambda k: 1 << k)[POOL_HALO:, :]
    t = r0 + lax.broadcasted_iota(jnp.int32, (POOL_ROWS, 1), 0)
    width = (2 << g).astype(F32)
    inv_count = 1.0 / jnp.minimum((t + 1).astype(F32), width)
    return s * inv_count - cur, inv_count


def _pool_fwd(proj, pw_g, pool_scale):
    def body(u_ref, gate_ref, pw_ref, sc_ref, y_ref):
        g = pl.program_id(0)

        def step(ii, _):
            chunks = [POOL_PAIR * ii + a for a in range(POOL_PAIR)]
            ds = [_pool_diff(u_ref, i, g)[0].astype(BF16) for i in chunks]
            mixed = [_dot(d, pw_ref[...]) for d in ds]
            for i, m in zip(chunks, mixed):
                r = pl.ds(pl.multiple_of(i * POOL_ROWS, POOL_ROWS), POOL_ROWS)
                gate = gate_ref[r, :]
                y_ref[r, :] = (m * sc_ref[...] * (gate * _sig(gate))).astype(BF16)
            return 0
        lax.fori_loop(0, SEQ // POOL_ROWS // POOL_PAIR, step, 0)

    return pl.pallas_call(
        body, name="pool_fwd", grid=(N_GROUPS,),
        in_specs=[_proj_cols(PG, 0, lambda g: (g, 0)), _proj_cols(PG, 1, lambda g: (g, 0)),
                  pl.BlockSpec((None, PG, PG), lambda g: (g, 0, 0)),
                  pl.BlockSpec((1, PG), lambda g: (0, g))],
        out_specs=pl.BlockSpec((SEQ, PG), lambda g: (0, g)),
        out_shape=jax.ShapeDtypeStruct((SEQ, D_MODEL), BF16),
        compiler_params=_params(dimension_semantics=("arbitrary",)),
    )(proj, proj, pw_g, pool_scale)


REC_ROWS = 1024
REC_CHUNKS = REC_ROWS // CHUNK
N_REC_BLK = SEQ // REC_ROWS
REC_GROUP = REC_CHUNKS
REC_GROUP_BWD = REC_CHUNKS
SEC_BLK = D_MODEL // HEAD


def _lower_bound(lb_ref):
    l0 = lb_ref[0:1, :]
    l1 = lb_ref[1:2, :]
    mx = jnp.maximum(l0, l1)
    e0 = jnp.exp(l0 - mx)
    e1 = jnp.exp(l1 - mx)
    return e0 / (e0 + e1)


def _gates(q, fl, lb):
    qs = q * _sig(q)
    sf = _sig(fl)
    f = lb + (1.0 - lb) * sf
    return qs, sf, f, 1.0 - f, jnp.log(f)


LOG2E = 1.4426950408889634


def _level_factors(g2, qs, k, sign_ref):
    t = lax.broadcasted_iota(jnp.int32, (CHUNK, HEAD), 0)
    row = lambda r, n: jnp.broadcast_to(g2[r:r + 1, :], (n, HEAD))
    out = []
    for l in range(N_LEVELS):
        m = 1 << l
        if l == 0:
            g_mid = jnp.where((t & 1) == 1, pltpu.roll(g2, 1, 0), g2)
        elif l == 1:
            low = (t & 7) < 4
            g_mid = jnp.concatenate([jnp.where(low[:8], row(8 * i + 1, 8), row(8 * i + 5, 8))
                                     for i in range(CHUNK // 8)], axis=0)
        else:
            g_mid = jnp.concatenate([row(b * 2 * m + m - 1, 2 * m) for b in range(CHUNK // (2 * m))], axis=0)
        sgn = sign_ref[l]
        up = sgn > 0.0
        e = jnp.exp2((g2 - g_mid) * sgn)
        x = jnp.where(up, qs, k) * e
        hi = x.astype(BF16)
        out.append((hi, (x - hi.astype(F32)).astype(BF16), e, up))
    return out


CHIP_FLIPS = ((1, 0), (0, 1), (1, 1))
HBM = pl.BlockSpec(memory_space=pl.ANY)


def _place():
    return lax.axis_index("x"), lax.axis_index("y"), lax.axis_index("c")


def _remote(src, dst, send_sems, recv_sems, k, to):
    return pltpu.make_async_remote_copy(src_ref=src, dst_ref=dst, send_sem=send_sems.at[k],
                                        recv_sem=recv_sems.at[k], device_id=to, device_id_type=MESH)


def _half_rows(ref, c):
    half = ref.shape[-2] // 2
    rows = pl.ds(pl.multiple_of(c * half, half), half)
    return ref.at[:, rows, :] if len(ref.shape) == 3 else ref.at[rows, :]


def _other_core_barrier():
    x, y, c = _place()
    sem = pltpu.get_barrier_semaphore()
    pl.semaphore_signal(sem, inc=1, device_id=(x, y, 1 - c), device_id_type=MESH)
    pl.semaphore_wait(sem, 1)


class _Exchange:
    def __init__(self, inputs, out_shapes, n_sems, start, finish, aliases=None):
        self.inputs, self.out_shapes, self.n_sems = list(inputs), list(out_shapes), n_sems
        self.start, self.finish, self.aliases = start, finish, dict(aliases or {})


def _ex_send(parts16, owners, units=None, landed=None):
    n_t = len(parts16)
    units = units or [tuple(range(len(o))) for o in owners]
    landed = landed or [None] * n_t
    given = [t for t in range(n_t) if landed[t] is not None]

    def each(ins, outs, send, recv, to_sender, to_owner):
        x, y, c = _place()
        k = 0
        for t, own in enumerate(owners):
            for j in units[t]:
                for r, (fx, fy) in enumerate(CHIP_FLIPS):
                    tx, ty = x ^ fx, y ^ fy
                    cp = _remote(ins[t].at[j], outs[t].at[j, r], send, recv, k, (tx, ty, c))
                    if to_sender is not None:
                        pl.when(2 * tx + ty == own[j])(functools.partial(to_sender, cp))
                    if to_owner is not None:
                        pl.when(2 * x + y == own[j])(functools.partial(to_owner, cp))
                    k += 1

    def start(*refs):
        each(*refs, lambda cp: cp.start(), None)

    def finish(*refs):
        each(*refs, None, lambda cp: cp.wait_recv())
        each(*refs, lambda cp: cp.wait_send(), None)

    shapes = [jax.ShapeDtypeStruct((a.shape[0], len(CHIP_FLIPS)) + a.shape[1:], BF16) for a in parts16]
    return _Exchange(list(parts16) + [landed[t] for t in given], shapes,
                     len(CHIP_FLIPS) * sum(len(u) for u in units), start, finish,
                     aliases={n_t + i: t for i, t in enumerate(given)})


def _ex_join(units, owners):
    def each(ins, outs, send, recv, fn):
        x, y, c = _place()
        k = 0
        for t, own in enumerate(owners):
            for j, o in enumerate(own):
                def half(cc, to, u=outs[t].at[j], k=k):
                    return _remote(_half_rows(u, cc), _half_rows(u, cc), send, recv, k, to)
                mine = functools.partial(half, c, (x, y, 1 - c))
                theirs = functools.partial(half, 1 - c, (x, y, c))
                pl.when(2 * x + y == o)(functools.partial(fn, mine, theirs))
                k += 1

    def start(*refs):
        each(*refs, lambda mine, theirs: mine().start())

    def finish(*refs):
        each(*refs, lambda mine, theirs: theirs().wait_recv())
        each(*refs, lambda mine, theirs: mine().wait_send())

    shapes = [jax.ShapeDtypeStruct(a.shape, F32) for a in units]
    return _Exchange(units, shapes, sum(len(o) for o in owners), start, finish,
                     aliases={t: t for t in range(len(units))})


def _ex_gather(slots):
    n_t = len(slots)
    n_fl = len(CHIP_FLIPS)

    def piece(ref, shard, half):
        return _half_rows(ref.at[shard], half)

    def first(outs, send, recv):
        x, y, c = _place()
        s = 2 * x + y
        return [_remote(piece(outs[t], s, c), piece(outs[t], s, c), send, recv, n_t * j + t, (x ^ fx, y ^ fy, c))
                for j, (fx, fy) in enumerate(CHIP_FLIPS) for t in range(n_t)]

    def start(ins, outs, send, recv):
        for cp in first(outs, send, recv):
            cp.start()

    def finish(ins, outs, send, recv):
        x, y, c = _place()
        passed = []
        for j, (fx, fy) in enumerate(CHIP_FLIPS):
            sj = 2 * (x ^ fx) + (y ^ fy)
            for t in range(n_t):
                k = n_t * j + t
                _remote(piece(outs[t], sj, c), piece(outs[t], sj, c), send, recv, k, (x, y, c)).wait_recv()
                cp = _remote(piece(outs[t], sj, c), piece(outs[t], sj, c), send, recv, n_t * n_fl + k, (x, y, 1 - c))
                cp.start()
                passed.append(cp)
        for j, (fx, fy) in enumerate(CHIP_FLIPS):
            sj = 2 * (x ^ fx) + (y ^ fy)
            for t in range(n_t):
                k = n_t * n_fl + n_t * j + t
                _remote(piece(outs[t], sj, 1 - c), piece(outs[t], sj, 1 - c), send, recv, k, (x, y, c)).wait_recv()
        for cp in first(outs, send, recv) + passed:
            cp.wait_send()

    shapes = [jax.ShapeDtypeStruct(a.shape, BF16) for a in slots]
    return _Exchange(slots, shapes, 2 * n_t * n_fl, start, finish, aliases={t: t for t in range(n_t)})


def _ex_gather_small(parts):
    def copies(ins, outs, send, recv):
        x, y, c = _place()
        me = 4 * x + 2 * y + c
        return [_remote(ins[0], outs[0].at[me], send, recv, mask - 1,
                        (x ^ (mask >> 2), y ^ ((mask >> 1) & 1), c ^ (mask & 1))) for mask in range(1, 8)]

    def start(*refs):
        for cp in copies(*refs):
            cp.start()

    def finish(ins, outs, send, recv):
        x, y, c = _place()
        me = 4 * x + 2 * y + c
        for mask in range(1, 8):
            _remote(ins[0], outs[0].at[me ^ mask], send, recv, mask - 1, (x, y, c)).wait_recv()
        for cp in copies(ins, outs, send, recv):
            cp.wait_send()

    return _Exchange([parts], [jax.ShapeDtypeStruct((8,) + parts.shape, F32)], 7, start, finish)


def _call(body, *, name, args=(), in_specs=(), out_specs=(), out_shape=(), grid=(), scratch_shapes=(),
          exchanges=()):
    n_in, n_out, n_scr = len(args), len(out_shape), len(scratch_shapes)
    ex_in, ex_out, ex_scr, spans, alias = [], [], [], [], {}
    for ex in exchanges:
        spans.append((len(ex_in), len(ex.inputs), len(ex_out), len(ex.out_shapes)))
        for i, o in ex.aliases.items():
            alias[n_in + len(ex_in) + i] = n_out + len(ex_out) + o
        ex_in += ex.inputs
        ex_out += ex.out_shapes
        ex_scr += [pltpu.SemaphoreType.DMA((ex.n_sems,)), pltpu.SemaphoreType.DMA((ex.n_sems,))]

    def full(*refs):
        ins, x_in = refs[:n_in], refs[n_in:n_in + len(ex_in)]
        outs = refs[n_in + len(ex_in):n_in + len(ex_in) + n_out]
        x_out = refs[n_in + len(ex_in) + n_out:n_in + len(ex_in) + n_out + len(ex_out)]
        scr = refs[len(refs) - n_scr - len(ex_scr):len(refs) - len(ex_scr)]
        sems = refs[len(refs) - len(ex_scr):]

        def run(which):
            for e, (ex, (i0, ni, o0, no)) in enumerate(zip(exchanges, spans)):
                getattr(ex, which)(x_in[i0:i0 + ni], x_out[o0:o0 + no], sems[2 * e], sems[2 * e + 1])

        if grid:
            ids = [pl.program_id(a) for a in range(len(grid))]
            is_first = functools.reduce(jnp.logical_and, [i == 0 for i in ids])
            is_last = functools.reduce(jnp.logical_and, [i == g - 1 for i, g in zip(ids, grid)])
            pl.when(is_first)(lambda: run("start"))
            body(*ins, *outs, *scr)
            pl.when(is_last)(lambda: run("finish"))
        else:
            run("start")
            if body is not None:
                body(*ins, *outs, *scr)
            run("finish")

    kw = dict(grid=grid) if grid else {}
    if grid:
        kw["compiler_params"] = _params(dimension_semantics=("arbitrary",) * len(grid))
    else:
        kw["compiler_params"] = _params()
    res = pl.pallas_call(
        full, name=name,
        in_specs=list(in_specs) + [HBM] * len(ex_in),
        out_specs=list(out_specs) + [HBM] * len(ex_out),
        out_shape=list(out_shape) + ex_out,
        scratch_shapes=list(scratch_shapes) + ex_scr,
        input_output_aliases=alias, **kw,
    )(*args, *ex_in)
    own = list(res[:n_out])
    per_ex = [list(res[n_out + o0:n_out + o0 + no]) for (_, _, o0, no) in spans]
    return own, per_ex


def _cast_w_in(w, place):
    rows, cols = w.shape
    tile = rows

    def body(place_ref, w_ref, o_ref):
        o_ref[...] = w_ref[...].astype(BF16)

    return pl.pallas_call(
        body, name="cast_w_in",
        grid_spec=pltpu.PrefetchScalarGridSpec(
            num_scalar_prefetch=1, grid=(cols // COL_BLK, rows // tile),
            in_specs=[pl.BlockSpec((tile, COL_BLK), lambda b, i, p: (i, b))],
            out_specs=pl.BlockSpec((None, None, tile, COL_BLK), lambda b, i, p: (p[1], b, i, 0))),
        out_shape=jax.ShapeDtypeStruct((N_SHARDS, cols // COL_BLK, rows, COL_BLK), BF16),
        compiler_params=_params(dimension_semantics=("arbitrary", "arbitrary")),
    )(place, w)


def _rec_fwd(proj, lb_logits, rec_g, consts, exchanges):
    tri, low, sign = consts["tri"], consts["low"], consts["sign"]

    def body(q_ref, f_ref, i_ref, rg_ref, lb_ref, g_ref, w_ref, low_ref, sign_ref, y_ref, o_ref, stp_ref, st_ref):
        @pl.when(pl.program_id(1) == 0)
        def _():
            st_ref[...] = jnp.zeros_like(st_ref)
        lb = _lower_bound(lb_ref)
        st = st_ref[...]
        rows = lambda c: pl.ds(c * CHUNK, CHUNK)
        for c0 in range(0, REC_CHUNKS, REC_GROUP):
            group = range(c0, c0 + REC_GROUP)
            gated = [_gates(q_ref[rows(c), :], f_ref[rows(c), :], lb) for c in group]
            g2s = [_dot3(w_ref[...], g) * LOG2E for (_, _, _, _, g) in gated]
            xs = [[xl for xl, _, _, _ in _level_factors(g2, qs, k, sign_ref)]
                  for g2, (qs, _, _, k, _) in zip(g2s, gated)]
            a_s = []
            for x in xs:
                a = jnp.zeros((CHUNK, CHUNK), F32)
                for l, xl in enumerate(x):
                    a = a + _dot_nt(xl, xl) * low_ref[l]
                a_s.append(a.astype(BF16))
            vbs = [i_ref[rows(c), :].astype(BF16) for c in group]
            intra = [_dot(a, vb) for a, vb in zip(a_s, vbs)]
            kvs = [_dot_tn(vb, (k * jnp.exp2(g2[CHUNK - 1:CHUNK, :] - g2)).astype(BF16))
                   for vb, g2, (_, _, _, k, _) in zip(vbs, g2s, gated)]
            for i, c in enumerate(group):
                qs, _, _, k, _ = gated[i]
                g2 = g2s[i]
                stp_ref[c] = st
                v = i_ref[rows(c), :]
                rg = rg_ref[rows(c), :]
                o = (intra[i] + jnp.sum(qs * k, axis=-1, keepdims=True) * v
                     + _dot_nt((qs * jnp.exp2(g2)).astype(BF16), st.astype(BF16)))
                st = st * jnp.exp2(g2[CHUNK - 1:CHUNK, :]) + kvs[i]
                o_ref[rows(c), :] = o
                inv = lax.rsqrt(jnp.mean(o * o, axis=-1, keepdims=True) + EPS)
                y_ref[rows(c), :] = (o * inv * g_ref[...] * (rg * _sig(rg))).astype(BF16)
        st_ref[...] = st

    sec = lambda n: _proj_cols(HEAD, n, lambda h, b: (h, b), REC_ROWS)
    vec = lambda rows: pl.BlockSpec((rows, HEAD), lambda h, b: (0, h))
    full = lambda a: pl.BlockSpec(a.shape, lambda h, b: (0,) * a.ndim)
    return _call(
        body, name="rec_fwd", grid=(N_HEADS, N_REC_BLK),
        args=(proj, proj, proj, proj, lb_logits, rec_g, tri, low, sign),
        in_specs=[sec(2), sec(3), sec(4), sec(5), vec(2), vec(1), full(tri), full(low), full(sign)],
        out_specs=[pl.BlockSpec((REC_ROWS, HEAD), lambda h, b: (b, h)),
                   pl.BlockSpec((REC_ROWS, HEAD), lambda h, b: (b, h)),
                   pl.BlockSpec((None, REC_CHUNKS, HEAD, HEAD), lambda h, b: (h, b, 0, 0))],
        out_shape=[jax.ShapeDtypeStruct((SEQ, D_MODEL), BF16),
                   jax.ShapeDtypeStruct((SEQ, D_MODEL), F32),
                   jax.ShapeDtypeStruct((N_HEADS, SEQ // CHUNK, HEAD, HEAD), F32)],
        scratch_shapes=[pltpu.VMEM((HEAD, HEAD), F32)],
        exchanges=exchanges)


OUT_ROWS = 512


def _out_proj_loss(y_pool, y_rec, w_out_g, x, target, gf):
    def body(yp_ref, yr_ref, w_ref, x_ref, t_ref, gf_ref, dout_ref, doutb_ref, part_ref):
        @pl.when(pl.program_id(0) == 0)
        def _():
            part_ref[...] = jnp.zeros_like(part_ref)
        halves = [pl.ds(a * (OUT_ROWS // 2), OUT_ROWS // 2) for a in range(2)]
        outs = [x_ref[r, :] + _dot(yp_ref[r, :], w_ref[0:D_MODEL, :])
                + _dot(yr_ref[r, :], w_ref[D_MODEL:2 * D_MODEL, :]) for r in halves]
        gf_v = gf_ref[...]
        for r, out in zip(halves, outs):
            inv = lax.rsqrt(jnp.mean(out * out, axis=-1, keepdims=True) + EPS)
            diff = out * inv * gf_v - t_ref[r, :]
            dyf = diff * (1.0 / D_MODEL)
            a = dyf * gf_v
            dout = inv * a - out * (inv * inv * inv) * jnp.mean(a * out, axis=-1, keepdims=True)
            dout_ref[r, :] = dout
            doutb_ref[r, :] = dout.astype(BF16)
            part_ref[0:1, :] += jnp.sum(dyf * out * inv, axis=0, keepdims=True)
            part_ref[1:2, :] += jnp.sum(diff * diff, axis=0, keepdims=True)

    row = lambda n: pl.BlockSpec((OUT_ROWS, n), lambda i: (i, 0))
    return pl.pallas_call(
        body, name="out_proj_loss", grid=(SEQ // OUT_ROWS,),
        in_specs=[row(D_MODEL), row(D_MODEL), pl.BlockSpec((2 * D_MODEL, D_MODEL), lambda i: (0, 0)),
                  row(D_MODEL), row(D_MODEL), pl.BlockSpec((1, D_MODEL), lambda i: (0, 0))],
        out_specs=[row(D_MODEL), row(D_MODEL), pl.BlockSpec((8, D_MODEL), lambda i: (0, 0))],
        out_shape=[jax.ShapeDtypeStruct((SEQ, D_MODEL), F32),
                   jax.ShapeDtypeStruct((SEQ, D_MODEL), BF16),
                   jax.ShapeDtypeStruct((8, D_MODEL), F32)],
        compiler_params=_params(dimension_semantics=("arbitrary",)),
    )(y_pool, y_rec, w_out_g, x, target, gf)


def _grad_w_out(y_pool, y_rec, dout_b):
    blk = W_OUT_SHARD // 2
    per = D_MODEL // blk
    n = 2 * per

    def body(yp_ref, yr_ref, d_ref, p32_ref, p16_ref, send_ref, recv_ref, send_sems, recv_sems):
        j = pl.program_id(0)
        x, y, c = _place()

        def copy(u):
            return _remote(send_ref.at[u], recv_ref.at[u], send_sems, recv_sems, u, (x, y, 1 - c))

        pl.when(j == 0)(_other_core_barrier)
        for i in range(n):
            @pl.when(j == i)
            def _(i=i):
                res = _dot_tn((yp_ref if i < per else yr_ref)[...], d_ref[...])

                @pl.when(i % 2 == c)
                def _():
                    p32_ref[i // 2] = res

                @pl.when(i % 2 != c)
                def _():
                    send_ref[i // 2] = res
                    copy(i // 2).start()

        @pl.when(j == n - 1)
        def _():
            for u in range(N_SHARDS):
                copy(u).wait_recv()
                tot = p32_ref[u] + recv_ref[u]
                p32_ref[u] = tot
                p16_ref[u] = tot.astype(BF16)
            for u in range(N_SHARDS):
                copy(u).wait_send()

    whole = pl.BlockSpec((N_SHARDS, blk, D_MODEL), lambda j: (0, 0, 0))
    return pl.pallas_call(
        body, name="grad_w_out", grid=(n,),
        in_specs=[pl.BlockSpec((SEQ, blk), lambda j: (0, jnp.minimum(j, per - 1))),
                  pl.BlockSpec((SEQ, blk), lambda j: (0, jnp.maximum(j - per, 0))),
                  pl.BlockSpec((SEQ, D_MODEL), lambda j: (0, 0))],
        out_specs=[whole, whole],
        out_shape=[jax.ShapeDtypeStruct((N_SHARDS, blk, D_MODEL), F32),
                   jax.ShapeDtypeStruct((N_SHARDS, blk, D_MODEL), BF16)],
        scratch_shapes=[pltpu.VMEM((N_SHARDS, blk, D_MODEL), F32), pltpu.VMEM((N_SHARDS, blk, D_MODEL), F32),
                        pltpu.SemaphoreType.DMA((N_SHARDS,)), pltpu.SemaphoreType.DMA((N_SHARDS,))],
        compiler_params=_params(dimension_semantics=("arbitrary",), collective_id=0),
    )(y_pool, y_rec, dout_b)


def _pool_bwd(proj, dout_b, w_out_g, pw_g, pool_scale, exchanges):
    n = POOL_ROWS + POOL_HALO
    per_half = N_GROUPS // 2

    def body(u_ref, gate_ref, d_ref, wo_ref, pw_ref, sc_ref,
             dp_ref, p32_ref, p16_ref, dsc_ref, dd_ref, ddw_ref, dpw_ref, keep_ref, send_ref, recv_ref,
             send_sems, recv_sems):
        g = pl.program_id(0)
        x_, y_, c = _place()

        def to_other_core(a):
            return _remote(send_ref.at[a], recv_ref.at[a], send_sems, recv_sems, a, (x_, y_, 1 - c))

        dpw_ref[...] = jnp.zeros_like(dpw_ref)
        dsc_ref[...] = jnp.zeros_like(dsc_ref)

        def first(ii, _):
            chunks = [POOL_PAIR * ii + a for a in range(POOL_PAIR)]
            rs = [pl.ds(pl.multiple_of(i * POOL_ROWS, POOL_ROWS), POOL_ROWS) for i in chunks]
            diffs = [_pool_diff(u_ref, i, g) for i in chunks]
            dbs = [d.astype(BF16) for d, _ in diffs]
            mixed = [_dot(db, pw_ref[...]) for db in dbs]
            dys = [_dot_nt(d_ref[r, :], wo_ref[...]) for r in rs]
            sc = sc_ref[...]
            dmbs = []
            for r, m, dy in zip(rs, mixed, dys):
                gate = gate_ref[r, :]
                sg = _sig(gate)
                silu = gate * sg
                dp_ref[1, r, :] = (dy * m * sc * (sg * (1.0 + gate * (1.0 - sg)))).astype(BF16)
                dsc_ref[...] += jnp.sum(dy * silu * m, axis=0, keepdims=True)
                dmbs.append((dy * silu * sc).astype(BF16))
            for db, dmb in zip(dbs, dmbs):
                dpw_ref[...] += _dot_tn(db, dmb)
            dds = [_dot_nt(dmb, pw_ref[...]) for dmb in dmbs]
            for r, dd, (_, inv_count) in zip(rs, dds, diffs):
                dd_ref[r, :] = dd
                ddw_ref[r, :] = dd * inv_count
            return 0
        lax.fori_loop(0, SEQ // POOL_ROWS // POOL_PAIR, first, 0)

        def second(i, _):
            r0 = i * POOL_ROWS
            r = pl.ds(pl.multiple_of(r0, POOL_ROWS), POOL_ROWS)
            last = i == SEQ // POOL_ROWS - 1
            after = ddw_ref[pl.ds(pl.multiple_of(jnp.minimum(r0 + POOL_ROWS, SEQ - POOL_HALO), 8), POOL_HALO), :]
            after = jnp.where(last, 0.0, after)
            ext = jnp.concatenate([ddw_ref[r, :], after], axis=0)
            s = _window_sums(ext, g, lambda k: n - (1 << k))[:POOL_ROWS, :]
            dp_ref[0, r, :] = (s - dd_ref[r, :]).astype(BF16)
            return 0
        lax.fori_loop(0, SEQ // POOL_ROWS, second, 0)

        for a in range(per_half):
            @pl.when((g % per_half == a) & (g // per_half == c))
            def _(a=a):
                keep_ref[a] = dpw_ref[...]

            @pl.when((g % per_half == a) & (g // per_half != c))
            def _(a=a):
                send_ref[a] = dpw_ref[...]
                to_other_core(a).start()

        @pl.when(g == N_GROUPS - 1)
        def _():
            for a in range(per_half):
                to_other_core(a).wait_recv()
                tot = keep_ref[a] + recv_ref[a]
                for s in range(N_SHARDS):
                    rows_s = tot[s * PW_SHARD:(s + 1) * PW_SHARD, :]
                    p32_ref[s, a * PW_SHARD:(a + 1) * PW_SHARD, :] = rows_s
                    p16_ref[s, a * PW_SHARD:(a + 1) * PW_SHARD, :] = rows_s.astype(BF16)
            for a in range(per_half):
                to_other_core(a).wait_send()

    pw_half = (N_SHARDS, per_half * PW_SHARD, PG)
    return _call(
        body, name="pool_bwd", grid=(N_GROUPS,),
        args=(proj, proj, dout_b, w_out_g, pw_g, pool_scale),
        in_specs=[_proj_cols(PG, 0, lambda g: (g, 0)), _proj_cols(PG, 1, lambda g: (g, 0)),
                  pl.BlockSpec((SEQ, D_MODEL), lambda g: (0, 0)),
                  pl.BlockSpec((PG, D_MODEL), lambda g: (g, 0)),
                  pl.BlockSpec((None, PG, PG), lambda g: (g, 0, 0)),
                  pl.BlockSpec((1, PG), lambda g: (0, g))],
        out_specs=[pl.BlockSpec((2, SEQ, PG), lambda g: (0, 0, g)),
                   pl.BlockSpec(pw_half, lambda g: (0, 0, 0)),
                   pl.BlockSpec(pw_half, lambda g: (0, 0, 0)),
                   pl.BlockSpec((1, PG), lambda g: (0, g))],
        out_shape=[jax.ShapeDtypeStruct((2, SEQ, D_MODEL), BF16),
                   jax.ShapeDtypeStruct(pw_half, F32),
                   jax.ShapeDtypeStruct(pw_half, BF16),
                   jax.ShapeDtypeStruct((1, D_MODEL), F32)],
        scratch_shapes=[pltpu.VMEM((SEQ, PG), F32), pltpu.VMEM((SEQ, PG), F32), pltpu.VMEM((PG, PG), F32)]
                       + [pltpu.VMEM((per_half, PG, PG), F32)] * 3
                       + [pltpu.SemaphoreType.DMA((per_half,)), pltpu.SemaphoreType.DMA((per_half,))],
        exchanges=exchanges)


HALF_HEADS = N_HEADS // 2
HALF_COLS = HALF_HEADS * HEAD


def _rec_bwd(proj, o_raw, st_prev, dout_b, w_out_g, lb_logits, rec_g, consts, h0, name, exchanges,
             gate_of=None, own_gate=True):
    n_sec = 4 if gate_of is None else 5

    def body(q_ref, f_ref, i_ref, rg_ref, o_ref, stp_ref, d_ref, wo_ref, lb_ref, g_ref,
             w_ref, lowt_ref, sym_ref, sign_ref, tri_ref, *rest):
        dr_ref, part_ref, dst_ref = rest[-3:]

        @pl.when(pl.program_id(1) == 0)
        def _():
            dst_ref[...] = jnp.zeros_like(dst_ref)
            part_ref[...] = jnp.zeros_like(part_ref)
        if gate_of is not None:
            rg2_ref, o2_ref, wo2_ref, g2_ref = rest[:4]
            rg, o = rg2_ref[...], o2_ref[...]
            sg = _sig(rg)
            inv = lax.rsqrt(jnp.mean(o * o, axis=-1, keepdims=True) + EPS)
            dy = _dot_nt(d_ref[...], wo2_ref[...])
            dr_ref[4] = (dy * (o * inv) * g2_ref[...] * (sg * (1.0 + rg * (1.0 - sg)))).astype(BF16)
        tril = (lax.broadcasted_iota(jnp.int32, (CHUNK, CHUNK), 0)
                > lax.broadcasted_iota(jnp.int32, (CHUNK, CHUNK), 1))
        lb = _lower_bound(lb_ref)
        grec = g_ref[...]
        dst = dst_ref[...]
        acc_grec = jnp.zeros((1, HEAD), F32)
        acc_lb = jnp.zeros((1, HEAD), F32)
        rows = lambda c: pl.ds(c * CHUNK, CHUNK)
        for c0 in reversed(range(0, REC_CHUNKS, REC_GROUP_BWD)):
            group = list(reversed(range(c0, c0 + REC_GROUP_BWD)))
            dys = [_dot_nt(d_ref[rows(c), :], wo_ref[...]) for c in group]
            dos = []
            for c, dy in zip(group, dys):
                rg = rg_ref[rows(c), :]
                o = o_ref[rows(c), :]
                sg = _sig(rg)
                silu = rg * sg
                inv = lax.rsqrt(jnp.mean(o * o, axis=-1, keepdims=True) + EPS)
                recn = o * inv
                if own_gate:
                    dr_ref[3, rows(c), :] = (dy * recn * grec * (sg * (1.0 + rg * (1.0 - sg)))).astype(BF16)
                acc_grec = acc_grec + jnp.sum(dy * silu * recn, axis=0, keepdims=True)
                drecn = dy * silu * grec
                dos.append(inv * drecn - o * (inv * inv * inv) * jnp.mean(drecn * o, axis=-1, keepdims=True))
            gated = [_gates(q_ref[rows(c), :], f_ref[rows(c), :], lb) for c in group]
            g2s = [_dot3(w_ref[...], g) * LOG2E for (_, _, _, _, g) in gated]
            levels = [_level_factors(g2, qs, k, sign_ref) for g2, (qs, _, _, k, _) in zip(g2s, gated)]
            a_ts = []
            for lev in levels:
                a_t = jnp.zeros((CHUNK, CHUNK), F32)
                for l, (xl, _, _, _) in enumerate(lev):
                    a_t = a_t + _dot_nt(xl, xl) * lowt_ref[l]
                a_ts.append(a_t.astype(BF16))
            dobs = [do.astype(BF16) for do in dos]
            vbs = [i_ref[rows(c), :].astype(BF16) for c in group]
            d_syms = [jnp.where(tril, _dot_nt(dob, vb), _dot_nt(vb, dob)) for dob, vb in zip(dobs, vbs)]
            dqs_is, dk_is = [], []
            for lev, d_sym in zip(levels, d_syms):
                dqs_i = jnp.zeros((CHUNK, HEAD), F32)
                both_i = jnp.zeros((CHUNK, HEAD), F32)
                for l, (xl, xlo, e, up) in enumerate(lev):
                    z = d_sym * sym_ref[l]
                    tmp = _dot(z.astype(BF16), jnp.concatenate([xl, xlo], axis=-1))
                    tmp = (tmp[:, :HEAD] + tmp[:, HEAD:]) * e
                    dqs_i = dqs_i + jnp.where(up, tmp, 0.0)
                    both_i = both_i + tmp
                dqs_is.append(dqs_i)
                dk_is.append(both_i - dqs_i)
            e_gs = [jnp.exp2(g2) for g2 in g2s]
            e_revs = [jnp.exp2(g2[CHUNK - 1:CHUNK, :] - g2) for g2 in g2s]
            e_lasts = [jnp.exp2(g2[CHUNK - 1:CHUNK, :]) for g2 in g2s]
            q_gs = [qs * e_g for (qs, _, _, _, _), e_g in zip(gated, e_gs)]
            kdecs = [k * e_rev for (_, _, _, k, _), e_rev in zip(gated, e_revs)]
            dv12 = [_dot(a_t, dob) + jnp.sum(qs * k, axis=-1, keepdims=True) * do
                    for a_t, dob, do, (qs, _, _, k, _) in zip(a_ts, dobs, dos, gated)]
            dq_gs = [_dot(dob, stp_ref[c].astype(BF16)) for c, dob in zip(group, dobs)]
            steps = [_dot_tn(dob, q_g.astype(BF16)) for dob, q_g in zip(dobs, q_gs)]
            dsts = []
            for e_last, step in zip(e_lasts, steps):
                dsts.append(dst)
                dst = dst * e_last + step
            dstbs = [d.astype(BF16) for d in dsts]
            dv3 = [_dot_nt(kdec.astype(BF16), dstb) for kdec, dstb in zip(kdecs, dstbs)]
            dkdecs = [_dot(vb, dstb) for vb, dstb in zip(vbs, dstbs)]
            dbig_gs, dg_lasts, dqss, dks = [], [], [], []
            for i, c in enumerate(group):
                qs, _, _, k, _ = gated[i]
                de_last = jnp.sum(stp_ref[c] * dsts[i], axis=0, keepdims=True)
                ddiag = jnp.sum(dos[i] * i_ref[rows(c), :], axis=-1, keepdims=True)
                dqss.append(dqs_is[i] + ddiag * k + dq_gs[i] * e_gs[i])
                dks.append(dk_is[i] + ddiag * qs + dkdecs[i] * e_revs[i])
                dg_rev = dkdecs[i] * kdecs[i]
                dg_lasts.append(jnp.sum(dg_rev, axis=0, keepdims=True) + de_last * e_lasts[i])
                dbig_gs.append(qs * dqs_is[i] - k * dk_is[i] + dq_gs[i] * q_gs[i] - dg_rev)
            dgs = [_dot3(tri_ref[...], dbig_g) + dg_last for dbig_g, dg_last in zip(dbig_gs, dg_lasts)]
            for i, c in enumerate(group):
                _, sf, f, _, _ = gated[i]
                q = q_ref[rows(c), :]
                df = dgs[i] / f - dks[i]
                dr_ref[1, rows(c), :] = (df * (1.0 - lb) * sf * (1.0 - sf)).astype(BF16)
                acc_lb = acc_lb + jnp.sum(df * (1.0 - sf), axis=0, keepdims=True)
                sq = _sig(q)
                dr_ref[0, rows(c), :] = (dqss[i] * (sq * (1.0 + q * (1.0 - sq)))).astype(BF16)
                dr_ref[2, rows(c), :] = (dv12[i] + dv3[i]).astype(BF16)
        dst_ref[...] = dst
        part_ref[0:1, :] += acc_grec
        part_ref[1:2, :] += acc_lb

    rev = lambda b: N_REC_BLK - 1 - b
    sec = lambda n: _proj_cols(HEAD, n, lambda h, b: (h0 + h, rev(b)), REC_ROWS)
    col_in = pl.BlockSpec((REC_ROWS, HEAD), lambda h, b: (rev(b), h0 + h))
    vec_in = lambda rows: pl.BlockSpec((rows, HEAD), lambda h, b: (0, h0 + h))
    full = lambda a: pl.BlockSpec(a.shape, lambda h, b: (0,) * a.ndim)
    extra_args, extra_specs = (), []
    if gate_of is not None:
        extra_args = (proj, o_raw, w_out_g, rec_g)
        extra_specs = [_proj_cols(HEAD, 5, lambda h, b: (gate_of + h, rev(b)), REC_ROWS),
                       pl.BlockSpec((REC_ROWS, HEAD), lambda h, b: (rev(b), gate_of + h)),
                       pl.BlockSpec((HEAD, D_MODEL), lambda h, b: (SEC_BLK + gate_of + h, 0)),
                       pl.BlockSpec((1, HEAD), lambda h, b: (0, gate_of + h))]
    return _call(
        body, name=name, grid=(HALF_HEADS, N_REC_BLK),
        args=(proj, proj, proj, proj, o_raw, st_prev, dout_b, w_out_g, lb_logits, rec_g,
              consts["tri"], consts["low_t"], consts["sym"], consts["sign"], consts["tri_t"]) + extra_args,
        in_specs=[sec(2), sec(3), sec(4), sec(5), col_in,
                  pl.BlockSpec((None, REC_CHUNKS, HEAD, HEAD), lambda h, b: (h0 + h, rev(b), 0, 0)),
                  pl.BlockSpec((REC_ROWS, D_MODEL), lambda h, b: (rev(b), 0)),
                  pl.BlockSpec((HEAD, D_MODEL), lambda h, b: (SEC_BLK + h0 + h, 0)),
                  vec_in(2), vec_in(1)] + [full(consts[n]) for n in ("tri", "low_t", "sym", "sign", "tri_t")]
                 + extra_specs,
        out_specs=[pl.BlockSpec((n_sec, REC_ROWS, HEAD), lambda h, b: (0, rev(b), h)),
                   pl.BlockSpec((8, HEAD), lambda h, b: (0, h))],
        out_shape=[jax.ShapeDtypeStruct((n_sec, SEQ, HALF_COLS), BF16),
                   jax.ShapeDtypeStruct((8, HALF_COLS), F32)],
        scratch_shapes=[pltpu.VMEM((HEAD, HEAD), F32)],
        exchanges=exchanges)


def _w_in_block(w_ref, j):
    per_shard = W_IN_SHARD // COL_BLK
    return w_ref[j // per_shard, j % per_shard]


def _grad_x(dproj, w_in_g, x, g1, dout, exchanges):
    rows = 512
    n_blk = len(dproj)

    def body(*refs):
        dp_refs = refs[:n_blk]
        w_ref, x_ref, g_ref, dout_ref, dx_ref, part_ref = refs[n_blk:]

        @pl.when(pl.program_id(0) == 0)
        def _():
            part_ref[...] = jnp.zeros_like(part_ref)
        dh = jnp.zeros((rows, D_MODEL), F32)
        for j in range(n_blk):
            dh = dh + _dot_nt(dp_refs[j][...], _w_in_block(w_ref, j))
        xv = x_ref[...]
        inv = lax.rsqrt(jnp.mean(xv * xv, axis=-1, keepdims=True) + EPS)
        a = dh * g_ref[...]
        dx_ref[...] = (dout_ref[...] + inv * a
                       - xv * (inv * inv * inv) * jnp.mean(a * xv, axis=-1, keepdims=True))
        part_ref[0:1, :] += jnp.sum(dh * xv * inv, axis=0, keepdims=True)

    row = lambda: pl.BlockSpec((rows, D_MODEL), lambda i: (i, 0))
    dp_spec = lambda sec, cb: pl.BlockSpec((None, rows, COL_BLK), lambda i: (sec, i, cb))
    return _call(
        body, name="grad_x", grid=(SEQ // rows,),
        args=tuple(a for a, _, _ in dproj) + (w_in_g, x, g1, dout),
        in_specs=[dp_spec(sec, cb) for _, sec, cb in dproj]
                 + [pl.BlockSpec(w_in_g.shape, lambda i: (0, 0, 0, 0)),
                    row(), pl.BlockSpec((1, D_MODEL), lambda i: (0, 0)), row()],
        out_specs=[row(), pl.BlockSpec((8, D_MODEL), lambda i: (0, 0))],
        out_shape=[jax.ShapeDtypeStruct((SEQ, D_MODEL), F32),
                   jax.ShapeDtypeStruct((8, D_MODEL), F32)],
        exchanges=exchanges)


def _grad_w_in(h, dp, blocks, name, collective_id):
    n_blk = len(blocks)
    half = D_MODEL // 2
    pick = lambda vals: (lambda j: functools.reduce(lambda acc, iv: jnp.where(j == iv[0], iv[1], acc),
                                                     list(enumerate(vals))[1:], vals[0]))
    sec_of = pick([sec for sec, _ in blocks])
    cb_of = pick([cb for _, cb in blocks])

    def body(h_ref, dp_ref, p32_ref, p16_ref, keep_ref, send_ref, recv_ref, send_sems, recv_sems):
        j = pl.program_id(0)
        x, y, c = _place()
        cols = lambda cc: pl.ds(pl.multiple_of(cc * half, half), half)

        def copy(i):
            return _remote(send_ref.at[i], recv_ref.at[i], send_sems, recv_sems, i, (x, y, 1 - c))

        pl.when(j == 0)(_other_core_barrier)
        for i in range(n_blk + 1):
            @pl.when(j == i)
            def _(i=i):
                if i < n_blk:
                    send_ref[i] = _dot_tn(h_ref[:, cols(1 - c)], dp_ref[...])
                    copy(i).start()
                    keep_ref[i] = _dot_tn(h_ref[:, cols(c)], dp_ref[...])
                if i > 0:
                    copy(i - 1).wait_recv()
                    tot = keep_ref[i - 1] + recv_ref[i - 1]
                    p32_ref[...] = tot
                    p16_ref[...] = tot.astype(BF16)

        @pl.when(j == n_blk)
        def _():
            for i in range(n_blk):
                copy(i).wait_send()

    lagged = pl.BlockSpec((None, half, COL_BLK), lambda j: (jnp.maximum(j - 1, 0), 0, 0))
    last = n_blk - 1
    return pl.pallas_call(
        body, name=name, grid=(n_blk + 1,),
        in_specs=[pl.BlockSpec((SEQ, D_MODEL), lambda j: (0, 0)),
                  pl.BlockSpec((None, SEQ, COL_BLK),
                               lambda j: (sec_of(jnp.minimum(j, last)), 0, cb_of(jnp.minimum(j, last))))],
        out_specs=[lagged, lagged],
        out_shape=[jax.ShapeDtypeStruct((n_blk, half, COL_BLK), F32),
                   jax.ShapeDtypeStruct((n_blk, half, COL_BLK), BF16)],
        scratch_shapes=[pltpu.VMEM((n_blk, half, COL_BLK), F32)] * 3
                       + [pltpu.SemaphoreType.DMA((n_blk,)), pltpu.SemaphoreType.DMA((n_blk,))],
        compiler_params=_params(dimension_semantics=("arbitrary",), collective_id=collective_id),
    )(h, dp)


def _sum_units(part32, recv16, place, owners, tile, name):
    n, half, cols = part32.shape
    per_half = half // tile
    table = np.array([[sum(o == chip for o in owners)] + sorted(range(n), key=lambda j: (owners[j] != chip, j))
                      for chip in range(N_SHARDS)], np.int32)
    sched = jnp.concatenate([place[:1], jnp.asarray(table)[place[1]]])

    def block(k, i, p):
        live = k < p[1]
        unit = p[2 + jnp.minimum(k, jnp.maximum(p[1] - 1, 0))]
        return unit, jnp.where(live, i, per_half - 1)

    def body(sched_ref, p_ref, r_ref, o_ref):
        @pl.when(pl.program_id(0) < sched_ref[1])
        def _():
            acc = p_ref[...]
            for j in range(len(CHIP_FLIPS)):
                acc = acc + r_ref[j].astype(F32)
            o_ref[...] = acc

    return pl.pallas_call(
        body, name=name,
        grid_spec=pltpu.PrefetchScalarGridSpec(
            num_scalar_prefetch=1, grid=(n, per_half),
            in_specs=[pl.BlockSpec((None, tile, cols), lambda k, i, p: (*block(k, i, p), 0)),
                      pl.BlockSpec((None, len(CHIP_FLIPS), tile, cols),
                                   lambda k, i, p: (block(k, i, p)[0], 0, block(k, i, p)[1], 0))],
            out_specs=pl.BlockSpec((None, tile, cols),
                                   lambda k, i, p: (block(k, i, p)[0], p[0] * per_half + block(k, i, p)[1], 0))),
        out_shape=jax.ShapeDtypeStruct((n, 2 * half, cols), F32),
        compiler_params=_params(dimension_semantics=("arbitrary", "arbitrary")),
    )(sched, part32, recv16)


def _adamw_math(w, g, m, v):
    m = ADAM_B1 * m + (1.0 - ADAM_B1) * g
    v = ADAM_B2 * v + (1.0 - ADAM_B2) * (g * g)
    m_hat = m / (1.0 - ADAM_B1 ** ADAM_STEP)
    v_hat = v / (1.0 - ADAM_B2 ** ADAM_STEP)
    delta = -ADAM_LR * (m_hat / (jnp.sqrt(v_hat) + ADAM_EPS) + ADAM_WD * w)
    return delta, m, v


def _adamw_units(w, m, v, grads, pick, name):
    rows, cols = w.shape
    bc = grads[0].shape[-1]
    tile = min(rows, 256)
    n_g = len(grads)

    def body(pick_ref, w_ref, m_ref, v_ref, *refs):
        g_refs, (g_out, d_ref, nm_ref, nv_ref) = refs[:n_g], refs[n_g:]
        p = pl.program_id(0)
        for a in range(n_g):
            @pl.when(pick_ref[0, p] == a)
            def _(a=a):
                g = g_refs[a][...]
                g_out[...] = g
                d_ref[...], nm_ref[...], nv_ref[...] = _adamw_math(w_ref[...], g, m_ref[...], v_ref[...])

    blk = pl.BlockSpec((tile, bc), lambda p, i, pick: (i, p))

    def g_spec(a):
        return pl.BlockSpec((None, tile, bc),
                            lambda p, i, pick: (jnp.where(pick[0, p] == a, pick[1, p], 0),
                                                jnp.where(pick[0, p] == a, i, 0), 0))

    return pl.pallas_call(
        body, name=name,
        grid_spec=pltpu.PrefetchScalarGridSpec(
            num_scalar_prefetch=1, grid=(cols // bc, rows // tile),
            in_specs=[blk] * 3 + [g_spec(a) for a in range(n_g)],
            out_specs=[blk] * 4),
        out_shape=[jax.ShapeDtypeStruct(w.shape, F32)] * 4,
        compiler_params=_params(dimension_semantics=("arbitrary", "arbitrary")),
    )(pick, w, m, v, *grads)


ROW_NORM1, ROW_SCALE, ROW_LB, ROW_REC, ROW_FINAL, ROW_LOSS = 0, 1, 2, 4, 5, 6


SMALL_ROWS = (ROW_NORM1, ROW_SCALE, ROW_LB, ROW_REC, ROW_FINAL)


def _small_update(parts, gathered, params):
    n_p = len(params)

    def body(own_ref, p_ref, *refs):
        ins, loss_ref, outs = refs[:3 * n_p], refs[3 * n_p], refs[3 * n_p + 1:]
        x, y, c = _place()
        me = 4 * x + 2 * y + c
        slot = lambda d: jnp.where(me == d, own_ref[...], p_ref[d])
        tot = slot(0)
        for d in range(1, 8):
            tot = tot + slot(d)
        for i, r in enumerate(SMALL_ROWS):
            w = ins[3 * i][...]
            g = tot[r:r + 1, :]
            if r == ROW_LB:
                mx = jnp.maximum(w[0:1, :], w[1:2, :])
                e0 = jnp.exp(w[0:1, :] - mx)
                e1 = jnp.exp(w[1:2, :] - mx)
                lb = e0 / (e0 + e1)
                g = g * lb * (1.0 - lb)
                g = jnp.concatenate([g, -g], axis=0)
            outs[4 * i][...] = g
            outs[4 * i + 1][...], outs[4 * i + 2][...], outs[4 * i + 3][...] = _adamw_math(
                w, g, ins[3 * i + 1][...], ins[3 * i + 2][...])
        loss_ref[...] = (0.5 / D_MODEL) * jnp.sum(tot[ROW_LOSS:ROW_LOSS + 1, :], axis=-1, keepdims=True)

    flat = [a for wmv in params for a in wmv]
    return pl.pallas_call(
        body, name="small_update",
        out_shape=[jax.ShapeDtypeStruct((1, 1), F32)]
                  + [jax.ShapeDtypeStruct(w.shape, F32) for w, _, _ in params for _ in range(4)],
        compiler_params=_params(),
    )(parts, gathered, *flat)


SHARD_OWNERS = tuple(range(N_SHARDS))
BLOCKS_POOL = (0, 1, 2, 3)
BLOCKS_A = (4, 6, 8, 10, 11)
BLOCKS_B = (5, 7, 9)
BLOCK_GROUPS = (BLOCKS_POOL, BLOCKS_A, BLOCKS_B)


def _block_owners(blocks):
    return tuple(j // (W_IN_SHARD // COL_BLK) for j in blocks)


def kernel(x, norm1_g, w_in, pool_w, pool_scale, lb_logits, rec_norm_g, w_out, final_norm_g, loss_target, m_norm1_g, m_w_in, m_pool_w, m_pool_scale, m_lb_logits, m_rec_norm_g, m_w_out, m_final_norm_g, v_norm1_g, v_w_in, v_pool_w, v_pool_scale, v_lb_logits, v_rec_norm_g, v_w_out, v_final_norm_g):
    xi, yi, ci = _place()
    chip = 2 * xi + yi
    place = jnp.stack([ci, chip]).astype(jnp.int32)
    pw_rows = N_GROUPS * PW_SHARD
    flat_pw = lambda a: a.reshape(pw_rows, PG)
    x2, target, gf = x[0], loss_target[0], final_norm_g.reshape(1, D_MODEL)
    consts = {n: jnp.asarray(a, BF16 if n.startswith("tri") else F32) for n, a in _chunk_constants().items()}

    proj, h, w_in_g, w_out_slots, pw_slots = _in_proj(x2, norm1_g, _cast_w_in(w_in[0], place), place,
                                                      [w_out[0], flat_pw(pool_w)])
    (y_rec, o_raw, st_prev), ((w_out_g, pw_g),) = _rec_fwd(
        proj, lb_logits, rec_norm_g, consts, [_ex_gather([w_out_slots, pw_slots])])
    w_out_g = w_out_g.reshape(2 * D_MODEL, D_MODEL)
    pw_full = pw_g.reshape(N_SHARDS, N_GROUPS, PW_SHARD, PG).transpose(1, 0, 2, 3).reshape(N_GROUPS, PG, PG)
    y_pool = _pool_fwd(proj, pw_full, pool_scale)
    dout, dout_b, part_out = _out_proj_loss(y_pool, y_rec, w_out_g, x2, target, gf)

    p_out32, p_out16 = _grad_w_out(y_pool, y_rec, dout_b)
    (dpool, p_pw32, p_pw16, dscale), ((rb_out,),) = _pool_bwd(proj, dout_b, w_out_g, pw_full, pool_scale,
                                                              [_ex_send([p_out16], [SHARD_OWNERS])])
    g_out = _sum_units(p_out32, rb_out, place, SHARD_OWNERS, 256, "sum_w_out")
    p_inp32, p_inp16 = _grad_w_in(h, dpool, [(0, 0), (0, 1), (1, 0), (1, 1)], "grad_w_in_pool", 1)

    pool_owners, a_owners, b_owners = (_block_owners(b) for b in BLOCK_GROUPS)
    rec_args = (proj, o_raw, st_prev, dout_b, w_out_g, lb_logits, rec_norm_g, consts)
    (drec_a, part_a), ((rb_inp,),) = _rec_bwd(
        *rec_args, 0, "rec_bwd_a", [_ex_send([p_inp16], [pool_owners], units=[(0, 1)])], gate_of=HALF_HEADS)
    p_ina32, p_ina16 = _grad_w_in(h, drec_a, [(n, 0) for n in range(5)], "grad_w_in_a", 2)

    (drec_b, part_b), ((rb_inp, rb_ina, rb_pw),) = _rec_bwd(
        *rec_args, HALF_HEADS, "rec_bwd_b",
        [_ex_send([p_inp16, p_ina16, p_pw16], [pool_owners, a_owners, SHARD_OWNERS],
                  units=[(2, 3), tuple(range(len(a_owners))), SHARD_OWNERS], landed=[rb_inp, None, None])],
        own_gate=False)
    g_inp = _sum_units(p_inp32, rb_inp, place, pool_owners, 256, "sum_w_in_pool")
    g_ina = _sum_units(p_ina32, rb_ina, place, a_owners, 256, "sum_w_in_a")
    g_pw = _sum_units(p_pw32, rb_pw, place, SHARD_OWNERS, 128, "sum_pool_w")
    p_inb32, p_inb16 = _grad_w_in(h, drec_b, [(n, 0) for n in range(3)], "grad_w_in_b", 3)

    dproj = ([(dpool, 0, 0), (dpool, 0, 1), (dpool, 1, 0), (dpool, 1, 1)]
             + [(d, n, 0) for n in range(3) for d in (drec_a, drec_b)] + [(drec_a, 3, 0), (drec_a, 4, 0)])
    (dx, part_x), ((rb_inb,),) = _grad_x(dproj, w_in_g, x2, norm1_g, dout, [_ex_send([p_inb16], [b_owners])])
    g_inb = _sum_units(p_inb32, rb_inb, place, b_owners, 256, "sum_w_in_b")
    zero = jnp.zeros((1, D_MODEL), F32)
    part_rec = jnp.concatenate([part_a, part_b], axis=1)
    parts = jnp.concatenate([part_x[0:1], dscale, part_rec[1:2], zero, part_rec[0:1], part_out[0:1],
                             part_out[1:2], zero], axis=0)
    _, ((g_out, g_pw, g_inp, g_ina, g_inb), (gathered,)) = _call(
        None, name="join_halves",
        exchanges=[_ex_join([g_out, g_pw, g_inp, g_ina, g_inb],
                            [SHARD_OWNERS, SHARD_OWNERS, pool_owners, a_owners, b_owners]),
                   _ex_gather_small(parts)])

    group_of = np.zeros((D_PROJ // COL_BLK,), np.int32)
    index_of = np.zeros((D_PROJ // COL_BLK,), np.int32)
    for gi, blocks in enumerate(BLOCK_GROUPS):
        for i, j in enumerate(blocks):
            group_of[j], index_of[j] = gi, i
    per_shard = W_IN_SHARD // COL_BLK
    pick_in = jnp.stack([lax.dynamic_slice(jnp.asarray(group_of), (per_shard * chip,), (per_shard,)),
                         lax.dynamic_slice(jnp.asarray(index_of), (per_shard * chip,), (per_shard,))])
    pick_own = jnp.stack([jnp.zeros((1,), jnp.int32), chip.reshape(1).astype(jnp.int32)])
    big = [_adamw_units(w_in[0], m_w_in[0], v_w_in[0], [g_inp, g_ina, g_inb], pick_in, "adamw_w_in"),
           _adamw_units(w_out[0], m_w_out[0], v_w_out[0], [g_out], pick_own, "adamw_w_out"),
           _adamw_units(flat_pw(pool_w), flat_pw(m_pool_w), flat_pw(v_pool_w), [g_pw], pick_own, "adamw_pool_w")]

    row = lambda a: a.reshape(1, D_MODEL)
    loss, *small = _small_update(parts, gathered, [
        (norm1_g, m_norm1_g, v_norm1_g), (pool_scale, m_pool_scale, v_pool_scale),
        (lb_logits, m_lb_logits, v_lb_logits), (rec_norm_g, m_rec_norm_g, v_rec_norm_g),
        (row(final_norm_g), row(m_final_norm_g), row(v_final_norm_g))])

    def leaves(k):
        norm1, scale, lb, rec, final = (small[4 * i + k] for i in range(len(SMALL_ROWS)))
        return (norm1, big[0][k][None], big[2][k].reshape(pool_w.shape), scale, lb, rec,
                big[1][k][None], final.reshape(D_MODEL))

    return (loss.reshape(()), dx[None], *leaves(0), *leaves(1), *leaves(2), *leaves(3))
```

```python
import functools

import numpy as np
import jax
import jax.numpy as jnp
from jax import lax
from jax.experimental import pallas as pl
from jax.experimental.pallas import tpu as pltpu

F32 = jnp.float32
BF16 = jnp.bfloat16

SEQ = 2048
D_MODEL = 1024
D_PROJ = 6144
N_SEC = 6
N_GROUPS = 4
PG = 256
N_HEADS = 8
HEAD = 128
CHUNK = 64
N_LEVELS = 6
N_SHARDS = 4
W_IN_SHARD = D_PROJ // N_SHARDS
W_OUT_SHARD = 2048 // N_SHARDS
PW_SHARD = PG // N_SHARDS
COL_BLK = 512
EPS = 1e-6

ADAM_LR = 0.001
ADAM_B1 = 0.9
ADAM_B2 = 0.999
ADAM_EPS = 1e-08
ADAM_WD = 0.01
ADAM_STEP = 10

V7X_VMEM_LIMIT = 56 * 1024 * 1024
MESH = pl.DeviceIdType.MESH


def _params(**kw):
    return pltpu.CompilerParams(vmem_limit_bytes=V7X_VMEM_LIMIT, **kw)


def _sig(x):
    return 1.0 / (1.0 + jnp.exp(-x))


def _dot(a, b):
    return jnp.dot(a, b, preferred_element_type=F32)


def _dot_nt(a, b):
    return lax.dot_general(a, b, (((1,), (1,)), ((), ())), preferred_element_type=F32)


def _dot_tn(a, b):
    return lax.dot_general(a, b, (((0,), (0,)), ((), ())), preferred_element_type=F32)


def _split3(a):
    p1 = a.astype(BF16)
    r1 = a - p1.astype(F32)
    p2 = r1.astype(BF16)
    p3 = (r1 - p2.astype(F32)).astype(BF16)
    return jnp.concatenate([p1, p2, p3], axis=-1)


def _dot3(w01, a):
    n = a.shape[-1]
    r = _dot(w01, _split3(a))
    return r[:, :n] + r[:, n:2 * n] + r[:, 2 * n:]


def _chunk_constants():
    j = np.arange(CHUNK)
    tt, ss = np.meshgrid(j, j, indexing="ij")
    x = tt ^ ss
    hb = np.full((CHUNK, CHUNK), -1, np.int32)
    for l in range(N_LEVELS):
        hb[x >= (1 << l)] = l
    sym = np.stack([(hb == l) for l in range(N_LEVELS)]).astype(np.float32)
    low = sym * (tt > ss)
    sign = np.stack([np.where((j >> l) & 1, 1.0, -1.0) for l in range(N_LEVELS)]).astype(np.float32)
    sign = np.ascontiguousarray(np.broadcast_to(sign[:, :, None], (N_LEVELS, CHUNK, HEAD)))
    tri = (ss <= tt).astype(np.float32)
    return dict(tri=tri, tri_t=np.ascontiguousarray(tri.T), low=low,
                low_t=np.ascontiguousarray(low.transpose(0, 2, 1)), sym=sym, sign=sign)


def _in_proj(x, g1, w_slots, place, also_cast):
    n_col = D_PROJ // COL_BLK
    per_shard = W_IN_SHARD // COL_BLK
    rows = 1024
    half_rows = D_MODEL // 2
    quarter_rows = D_MODEL // 4
    FLIPS = (0, 2, 1, 3)
    ORDER = ([(0, p) for p in range(per_shard)] + [(m, p) for p in range(per_shard) for m in (1, 2)]
             + [(3, p) for p in range(per_shard)])

    def shard_at(m, chip):
        return chip ^ FLIPS[m]

    def pick(vals, t):
        return functools.reduce(lambda acc, iv: jnp.where(t == iv[0], iv[1], acc), list(enumerate(vals))[1:], vals[0])

    def body(place_ref, x_hbm, g_ref, w_in_ref, *rest):
        others, (proj_ref, h_ref, w_ref) = rest[:len(also_cast)], rest[len(also_cast):len(also_cast) + 3]
        slots = rest[len(also_cast) + 3:2 * len(also_cast) + 3]
        wbuf, load_sems, send_sems, recv_sems, x_ref, x_sem = rest[2 * len(also_cast) + 3:]
        t = pl.program_id(0)
        load_x = pltpu.make_async_copy(x_hbm, x_ref, x_sem)

        @pl.when(t == 1)
        def _():
            for src, dst in zip(others, slots):
                dst[...] = src[...].astype(BF16)
        x_, y_, c = _place()
        chip = 2 * x_ + y_
        me, other_core = (x_, y_, c), (x_, y_, 1 - c)
        x_nbr, y_nbr = (1 - x_, y_, c), (x_, 1 - y_, c)

        def rows_of(half, q=None):
            if q is None:
                return pl.ds(pl.multiple_of(half * half_rows, half_rows), half_rows)
            return pl.ds(pl.multiple_of(half * half_rows + q * quarter_rows, quarter_rows), quarter_rows)

        def block(m, p, r):
            return w_ref.at[shard_at(m, chip), p, r, :]

        def copy(k, ref, to):
            return _remote(ref, ref, send_sems, recv_sems, k, to)

        direct = lambda n, p, to: copy(3 * n + p, block(0, p, rows_of(c)), to)
        relay = lambda n, p, to: copy(6 + 3 * n + p, block(1 + n, p, rows_of(c, n)), to)
        arrived = lambda m, p: ([copy(3 * (m - 1) + p, block(m, p, rows_of(c)), me)] if m < 3 else
                                [copy(6 + 3 * n + p, block(3, p, rows_of(c, n)), me) for n in (0, 1)])
        passed_on = lambda m, p, half, to: copy(9 + 3 * m + p, block(m, p, rows_of(half)), to)

        def load(m, p, slot):
            return pltpu.make_async_copy(w_ref.at[shard_at(m, chip), p], wbuf.at[slot], load_sems.at[slot])

        def prepare(m, p):
            for cp in arrived(m, p):
                cp.wait_recv()
            passed_on(m, p, c, other_core).start()
            if m < 3:
                relay(m - 1, p, y_nbr if m == 1 else x_nbr).start()

        @pl.when(t == 0)
        def _():
            for p in range(per_shard):
                direct(0, p, x_nbr).start()
                direct(1, p, y_nbr).start()
            load_x.start()
            for p in range(per_shard):
                load(0, p, p).start()
            load_x.wait()

            def norm(i, _):
                r = pl.ds(pl.multiple_of(i * rows, rows), rows)
                xv = x_ref[r, :]
                inv = lax.rsqrt(jnp.mean(xv * xv, axis=-1, keepdims=True) + EPS)
                h_ref[r, :] = (xv * inv * g_ref[...]).astype(BF16)
                return 0
            lax.fori_loop(0, SEQ // rows, norm, 0)

        for step, (m, p) in enumerate(ORDER):
            @pl.when(t == step)
            def _(step=step, m=m, p=p):
                slot = step % per_shard
                if m > 0:
                    passed_on(m, p, 1 - c, me).wait_recv()
                    load(m, p, slot).start()
                if step + 1 < n_col and ORDER[step + 1][0] > 0:
                    prepare(*ORDER[step + 1])
                load(m, p, slot).wait()

                def mm(i, _):
                    r = pl.ds(pl.multiple_of(i * rows, rows), rows)
                    proj_ref[r, :] = _dot(h_ref[r, :], wbuf[slot])
                    return 0
                lax.fori_loop(0, SEQ // rows, mm, 0)

        @pl.when(t == n_col - 1)
        def _():
            for p in range(per_shard):
                sent = [direct(0, p, x_nbr), direct(1, p, y_nbr), relay(0, p, y_nbr), relay(1, p, x_nbr)]
                for cp in sent + [passed_on(m, p, c, other_core) for m in (1, 2, 3)]:
                    cp.wait_send()

    return pl.pallas_call(
        body, name="in_proj",
        grid_spec=pltpu.PrefetchScalarGridSpec(
            num_scalar_prefetch=1, grid=(n_col,),
            in_specs=[pl.BlockSpec(memory_space=pl.ANY),
                      pl.BlockSpec((1, D_MODEL), lambda t, p: (0, 0)),
                      pl.BlockSpec(memory_space=pl.ANY)]
                     + [pl.BlockSpec(a.shape, lambda t, p: (0, 0)) for a in also_cast],
            out_specs=[pl.BlockSpec((None, SEQ, COL_BLK),
                                    lambda t, p: (per_shard * (p[1] ^ pick([FLIPS[m] for m, _ in ORDER], t))
                                                  + pick([b for _, b in ORDER], t), 0, 0)),
                       pl.BlockSpec((SEQ, D_MODEL), lambda t, p: (0, 0)),
                       pl.BlockSpec(memory_space=pl.ANY)]
                      + [pl.BlockSpec((None,) + a.shape, lambda t, p: (p[1], 0, 0)) for a in also_cast],
            scratch_shapes=[pltpu.VMEM((per_shard, D_MODEL, COL_BLK), BF16),
                            pltpu.SemaphoreType.DMA((per_shard,)),
                            pltpu.SemaphoreType.DMA((21,)), pltpu.SemaphoreType.DMA((21,)),
                            pltpu.VMEM((SEQ, D_MODEL), F32), pltpu.SemaphoreType.DMA]),
        out_shape=[jax.ShapeDtypeStruct((n_col, SEQ, COL_BLK), F32),
                   jax.ShapeDtypeStruct((SEQ, D_MODEL), BF16),
                   jax.ShapeDtypeStruct(w_slots.shape, BF16)]
                  + [jax.ShapeDtypeStruct((N_SHARDS,) + a.shape, BF16) for a in also_cast],
        input_output_aliases={3: 2},
        compiler_params=_params(dimension_semantics=("arbitrary",)),
    )(place, x, g1, w_slots, *also_cast)


def _proj_cols(width, section, where, rows=SEQ):
    per_blk = COL_BLK // width

    def index(*grid):
        k, r = where(*grid)
        return section * (D_MODEL // COL_BLK) + k // per_blk, r, k % per_blk

    return pl.BlockSpec((None, rows, width), index)


POOL_ROWS = 256
POOL_HALO = 16
POOL_PAIR = 4


def _window_sums(ext, g, shift_of):
    s = ext
    for k in range(N_GROUPS):
        s = jnp.where(k <= g, s + pltpu.roll(s, shift_of(k), 0), s)
    return s


def _pool_diff(u_ref, i, g):
    n = POOL_ROWS + POOL_HALO
    r0 = i * POOL_ROWS
    cur = u_ref[pl.ds(pl.multiple_of(r0, POOL_ROWS), POOL_ROWS), :]
    before = u_ref[pl.ds(pl.multiple_of(jnp.maximum(r0 - POOL_HALO, 0), 8), POOL_HALO), :]
    before = jnp.where(i > 0, before, 0.0)
    ext = jnp.concatenate([before, cur], axis=0)
    s = _window_sums(ext, g, lambda k: 1 << k)[POOL_HALO:, :]
    t = r0 + lax.broadcasted_iota(jnp.int32, (POOL_ROWS, 1), 0)
    width = (2 << g).astype(F32)
    inv_count = 1.0 / jnp.minimum((t + 1).astype(F32), width)
    return s * inv_count - cur, inv_count


def _pool_fwd(proj, pw_g, pool_scale):
    def body(u_ref, gate_ref, pw_ref, sc_ref, y_ref):
        g = pl.program_id(0)

        def step(ii, _):
            chunks = [POOL_PAIR * ii + a for a in range(POOL_PAIR)]
            ds = [_pool_diff(u_ref, i, g)[0].astype(BF16) for i in chunks]
            mixed = [_dot(d, pw_ref[...]) for d in ds]
            for i, m in zip(chunks, mixed):
                r = pl.ds(pl.multiple_of(i * POOL_ROWS, POOL_ROWS), POOL_ROWS)
                gate = gate_ref[r, :]
                y_ref[r, :] = (m * sc_ref[...] * (gate * _sig(gate))).astype(BF16)
            return 0
        lax.fori_loop(0, SEQ // POOL_ROWS // POOL_PAIR, step, 0)

    return pl.pallas_call(
        body, name="pool_fwd", grid=(N_GROUPS,),
        in_specs=[_proj_cols(PG, 0, lambda g: (g, 0)), _proj_cols(PG, 1, lambda g: (g, 0)),
                  pl.BlockSpec((None, PG, PG), lambda g: (g, 0, 0)),
                  pl.BlockSpec((1, PG), lambda g: (0, g))],
        out_specs=pl.BlockSpec((SEQ, PG), lambda g: (0, g)),
        out_shape=jax.ShapeDtypeStruct((SEQ, D_MODEL), BF16),
        compiler_params=_params(dimension_semantics=("arbitrary",)),
    )(proj, proj, pw_g, pool_scale)


REC_ROWS = 1024
REC_CHUNKS = REC_ROWS // CHUNK
N_REC_BLK = SEQ // REC_ROWS
REC_GROUP = REC_CHUNKS
REC_GROUP_BWD = REC_CHUNKS
SEC_BLK = D_MODEL // HEAD


def _lower_bound(lb_ref):
    l0 = lb_ref[0:1, :]
    l1 = lb_ref[1:2, :]
    mx = jnp.maximum(l0, l1)
    e0 = jnp.exp(l0 - mx)
    e1 = jnp.exp(l1 - mx)
    return e0 / (e0 + e1)


def _gates(q, fl, lb):
    qs = q * _sig(q)
    sf = _sig(fl)
    f = lb + (1.0 - lb) * sf
    return qs, sf, f, 1.0 - f, jnp.log(f)


LOG2E = 1.4426950408889634


def _level_factors(g2, qs, k, sign_ref):
    t = lax.broadcasted_iota(jnp.int32, (CHUNK, HEAD), 0)
    row = lambda r, n: jnp.broadcast_to(g2[r:r + 1, :], (n, HEAD))
    out = []
    for l in range(N_LEVELS):
        m = 1 << l
        if l == 0:
            g_mid = jnp.where((t & 1) == 1, pltpu.roll(g2, 1, 0), g2)
        elif l == 1:
            low = (t & 7) < 4
            g_mid = jnp.concatenate([jnp.where(low[:8], row(8 * i + 1, 8), row(8 * i + 5, 8))
                                     for i in range(CHUNK // 8)], axis=0)
        else:
            g_mid = jnp.concatenate([row(b * 2 * m + m - 1, 2 * m) for b in range(CHUNK // (2 * m))], axis=0)
        sgn = sign_ref[l]
        up = sgn > 0.0
        e = jnp.exp2((g2 - g_mid) * sgn)
        x = jnp.where(up, qs, k) * e
        hi = x.astype(BF16)
        out.append((hi, (x - hi.astype(F32)).astype(BF16), e, up))
    return out


CHIP_FLIPS = ((1, 0), (0, 1), (1, 1))
HBM = pl.BlockSpec(memory_space=pl.ANY)


def _place():
    return lax.axis_index("x"), lax.axis_index("y"), lax.axis_index("c")


def _remote(src, dst, send_sems, recv_sems, k, to):
    return pltpu.make_async_remote_copy(src_ref=src, dst_ref=dst, send_sem=send_sems.at[k],
                                        recv_sem=recv_sems.at[k], device_id=to, device_id_type=MESH)


def _half_rows(ref, c):
    half = ref.shape[-2] // 2
    rows = pl.ds(pl.multiple_of(c * half, half), half)
    return ref.at[:, rows, :] if len(ref.shape) == 3 else ref.at[rows, :]


def _other_core_barrier():
    x, y, c = _place()
    sem = pltpu.get_barrier_semaphore()
    pl.semaphore_signal(sem, inc=1, device_id=(x, y, 1 - c), device_id_type=MESH)
    pl.semaphore_wait(sem, 1)


class _Exchange:
    def __init__(self, inputs, out_shapes, n_sems, start, finish, aliases=None):
        self.inputs, self.out_shapes, self.n_sems = list(inputs), list(out_shapes), n_sems
        self.start, self.finish, self.aliases = start, finish, dict(aliases or {})


def _ex_send(parts16, owners, units=None, landed=None):
    n_t = len(parts16)
    units = units or [tuple(range(len(o))) for o in owners]
    landed = landed or [None] * n_t
    given = [t for t in range(n_t) if landed[t] is not None]

    def each(ins, outs, send, recv, to_sender, to_owner):
        x, y, c = _place()
        k = 0
        for t, own in enumerate(owners):
            for j in units[t]:
                for r, (fx, fy) in enumerate(CHIP_FLIPS):
                    tx, ty = x ^ fx, y ^ fy
                    cp = _remote(ins[t].at[j], outs[t].at[j, r], send, recv, k, (tx, ty, c))
                    if to_sender is not None:
                        pl.when(2 * tx + ty == own[j])(functools.partial(to_sender, cp))
                    if to_owner is not None:
                        pl.when(2 * x + y == own[j])(functools.partial(to_owner, cp))
                    k += 1

    def start(*refs):
        each(*refs, lambda cp: cp.start(), None)

    def finish(*refs):
        each(*refs, None, lambda cp: cp.wait_recv())
        each(*refs, lambda cp: cp.wait_send(), None)

    shapes = [jax.ShapeDtypeStruct((a.shape[0], len(CHIP_FLIPS)) + a.shape[1:], BF16) for a in parts16]
    return _Exchange(list(parts16) + [landed[t] for t in given], shapes,
                     len(CHIP_FLIPS) * sum(len(u) for u in units), start, finish,
                     aliases={n_t + i: t for i, t in enumerate(given)})


def _ex_join(units, owners):
    def each(ins, outs, send, recv, fn):
        x, y, c = _place()
        k = 0
        for t, own in enumerate(owners):
            for j, o in enumerate(own):
                def half(cc, to, u=outs[t].at[j], k=k):
                    return _remote(_half_rows(u, cc), _half_rows(u, cc), send, recv, k, to)
                mine = functools.partial(half, c, (x, y, 1 - c))
                theirs = functools.partial(half, 1 - c, (x, y, c))
                pl.when(2 * x + y == o)(functools.partial(fn, mine, theirs))
                k += 1

    def start(*refs):
        each(*refs, lambda mine, theirs: mine().start())

    def finish(*refs):
        each(*refs, lambda mine, theirs: theirs().wait_recv())
        each(*refs, lambda mine, theirs: mine().wait_send())

    shapes = [jax.ShapeDtypeStruct(a.shape, F32) for a in units]
    return _Exchange(units, shapes, sum(len(o) for o in owners), start, finish,
                     aliases={t: t for t in range(len(units))})


def _ex_gather(slots):
    n_t = len(slots)
    n_fl = len(CHIP_FLIPS)

    def piece(ref, shard, half):
        return _half_rows(ref.at[shard], half)

    def first(outs, send, recv):
        x, y, c = _place()
        s = 2 * x + y
        return [_remote(piece(outs[t], s, c), piece(outs[t], s, c), send, recv, n_t * j + t, (x ^ fx, y ^ fy, c))
                for j, (fx, fy) in enumerate(CHIP_FLIPS) for t in range(n_t)]

    def start(ins, outs, send, recv):
        for cp in first(outs, send, recv):
            cp.start()

    def finish(ins, outs, send, recv):
        x, y, c = _place()
        passed = []
        for j, (fx, fy) in enumerate(CHIP_FLIPS):
            sj = 2 * (x ^ fx) + (y ^ fy)
            for t in range(n_t):
                k = n_t * j + t
                _remote(piece(outs[t], sj, c), piece(outs[t], sj, c), send, recv, k, (x, y, c)).wait_recv()
                cp = _remote(piece(outs[t], sj, c), piece(outs[t], sj, c), send, recv, n_t * n_fl + k, (x, y, 1 - c))
                cp.start()
                passed.append(cp)
        for j, (fx, fy) in enumerate(CHIP_FLIPS):
            sj = 2 * (x ^ fx) + (y ^ fy)
            for t in range(n_t):
                k = n_t * n_fl + n_t * j + t
                _remote(piece(outs[t], sj, 1 - c), piece(outs[t], sj, 1 - c), send, recv, k, (x, y, c)).wait_recv()
        for cp in first(outs, send, recv) + passed:
            cp.wait_send()

    shapes = [jax.ShapeDtypeStruct(a.shape, BF16) for a in slots]
    return _Exchange(slots, shapes, 2 * n_t * n_fl, start, finish, aliases={t: t for t in range(n_t)})


def _ex_gather_small(parts):
    def copies(ins, outs, send, recv):
        x, y, c = _place()
        me = 4 * x + 2 * y + c
        return [_remote(ins[0], outs[0].at[me], send, recv, mask - 1,
                        (x ^ (mask >> 2), y ^ ((mask >> 1) & 1), c ^ (mask & 1))) for mask in range(1, 8)]

    def start(*refs):
        for cp in copies(*refs):
            cp.start()

    def finish(ins, outs, send, recv):
        x, y, c = _place()
        me = 4 * x + 2 * y + c
        for mask in range(1, 8):
            _remote(ins[0], outs[0].at[me ^ mask], send, recv, mask - 1, (x, y, c)).wait_recv()
        for cp in copies(ins, outs, send, recv):
            cp.wait_send()

    return _Exchange([parts], [jax.ShapeDtypeStruct((8,) + parts.shape, F32)], 7, start, finish)


def _call(body, *, name, args=(), in_specs=(), out_specs=(), out_shape=(), grid=(), scratch_shapes=(),
          exchanges=()):
    n_in, n_out, n_scr = len(args), len(out_shape), len(scratch_shapes)
    ex_in, ex_out, ex_scr, spans, alias = [], [], [], [], {}
    for ex in exchanges:
        spans.append((len(ex_in), len(ex.inputs), len(ex_out), len(ex.out_shapes)))
        for i, o in ex.aliases.items():
            alias[n_in + len(ex_in) + i] = n_out + len(ex_out) + o
        ex_in += ex.inputs
        ex_out += ex.out_shapes
        ex_scr += [pltpu.SemaphoreType.DMA((ex.n_sems,)), pltpu.SemaphoreType.DMA((ex.n_sems,))]

    def full(*refs):
        ins, x_in = refs[:n_in], refs[n_in:n_in + len(ex_in)]
        outs = refs[n_in + len(ex_in):n_in + len(ex_in) + n_out]
        x_out = refs[n_in + len(ex_in) + n_out:n_in + len(ex_in) + n_out + len(ex_out)]
        scr = refs[len(refs) - n_scr - len(ex_scr):len(refs) - len(ex_scr)]
        sems = refs[len(refs) - len(ex_scr):]

        def run(which):
            for e, (ex, (i0, ni, o0, no)) in enumerate(zip(exchanges, spans)):
                getattr(ex, which)(x_in[i0:i0 + ni], x_out[o0:o0 + no], sems[2 * e], sems[2 * e + 1])

        if grid:
            ids = [pl.program_id(a) for a in range(len(grid))]
            is_first = functools.reduce(jnp.logical_and, [i == 0 for i in ids])
            is_last = functools.reduce(jnp.logical_and, [i == g - 1 for i, g in zip(ids, grid)])
            pl.when(is_first)(lambda: run("start"))
            body(*ins, *outs, *scr)
            pl.when(is_last)(lambda: run("finish"))
        else:
            run("start")
            if body is not None:
                body(*ins, *outs, *scr)
            run("finish")

    kw = dict(grid=grid) if grid else {}
    if grid:
        kw["compiler_params"] = _params(dimension_semantics=("arbitrary",) * len(grid))
    else:
        kw["compiler_params"] = _params()
    res = pl.pallas_call(
        full, name=name,
        in_specs=list(in_specs) + [HBM] * len(ex_in),
        out_specs=list(out_specs) + [HBM] * len(ex_out),
        out_shape=list(out_shape) + ex_out,
        scratch_shapes=list(scratch_shapes) + ex_scr,
        input_output_aliases=alias, **kw,
    )(*args, *ex_in)
    own = list(res[:n_out])
    per_ex = [list(res[n_out + o0:n_out + o0 + no]) for (_, _, o0, no) in spans]
    return own, per_ex


def _cast_w_in(w, place):
    rows, cols = w.shape
    tile = rows

    def body(place_ref, w_ref, o_ref):
        o_ref[...] = w_ref[...].astype(BF16)

    return pl.pallas_call(
        body, name="cast_w_in",
        grid_spec=pltpu.PrefetchScalarGridSpec(
            num_scalar_prefetch=1, grid=(cols // COL_BLK, rows // tile),
            in_specs=[pl.BlockSpec((tile, COL_BLK), lambda b, i, p: (i, b))],
            out_specs=pl.BlockSpec((None, None, tile, COL_BLK), lambda b, i, p: (p[1], b, i, 0))),
        out_shape=jax.ShapeDtypeStruct((N_SHARDS, cols // COL_BLK, rows, COL_BLK), BF16),
        compiler_params=_params(dimension_semantics=("arbitrary", "arbitrary")),
    )(place, w)


def _rec_fwd(proj, lb_logits, rec_g, consts, exchanges):
    tri, low, sign = consts["tri"], consts["low"], consts["sign"]

    def body(q_ref, f_ref, i_ref, rg_ref, lb_ref, g_ref, w_ref, low_ref, sign_ref, y_ref, o_ref, stp_ref, st_ref):
        @pl.when(pl.program_id(1) == 0)
        def _():
            st_ref[...] = jnp.zeros_like(st_ref)
        lb = _lower_bound(lb_ref)
        st = st_ref[...]
        rows = lambda c: pl.ds(c * CHUNK, CHUNK)
        for c0 in range(0, REC_CHUNKS, REC_GROUP):
            group = range(c0, c0 + REC_GROUP)
            gated = [_gates(q_ref[rows(c), :], f_ref[rows(c), :], lb) for c in group]
            g2s = [_dot3(w_ref[...], g) * LOG2E for (_, _, _, _, g) in gated]
            xs = [[xl for xl, _, _, _ in _level_factors(g2, qs, k, sign_ref)]
                  for g2, (qs, _, _, k, _) in zip(g2s, gated)]
            a_s = []
            for x in xs:
                a = jnp.zeros((CHUNK, CHUNK), F32)
                for l, xl in enumerate(x):
                    a = a + _dot_nt(xl, xl) * low_ref[l]
                a_s.append(a.astype(BF16))
            vbs = [i_ref[rows(c), :].astype(BF16) for c in group]
            intra = [_dot(a, vb) for a, vb in zip(a_s, vbs)]
            kvs = [_dot_tn(vb, (k * jnp.exp2(g2[CHUNK - 1:CHUNK, :] - g2)).astype(BF16))
                   for vb, g2, (_, _, _, k, _) in zip(vbs, g2s, gated)]
            for i, c in enumerate(group):
                qs, _, _, k, _ = gated[i]
                g2 = g2s[i]
                stp_ref[c] = st
                v = i_ref[rows(c), :]
                rg = rg_ref[rows(c), :]
                o = (intra[i] + jnp.sum(qs * k, axis=-1, keepdims=True) * v
                     + _dot_nt((qs * jnp.exp2(g2)).astype(BF16), st.astype(BF16)))
                st = st * jnp.exp2(g2[CHUNK - 1:CHUNK, :]) + kvs[i]
                o_ref[rows(c), :] = o
                inv = lax.rsqrt(jnp.mean(o * o, axis=-1, keepdims=True) + EPS)
                y_ref[rows(c), :] = (o * inv * g_ref[...] * (rg * _sig(rg))).astype(BF16)
        st_ref[...] = st

    sec = lambda n: _proj_cols(HEAD, n, lambda h, b: (h, b), REC_ROWS)
    vec = lambda rows: pl.BlockSpec((rows, HEAD), lambda h, b: (0, h))
    full = lambda a: pl.BlockSpec(a.shape, lambda h, b: (0,) * a.ndim)
    return _call(
        body, name="rec_fwd", grid=(N_HEADS, N_REC_BLK),
        args=(proj, proj, proj, proj, lb_logits, rec_g, tri, low, sign),
        in_specs=[sec(2), sec(3), sec(4), sec(5), vec(2), vec(1), full(tri), full(low), full(sign)],
        out_specs=[pl.BlockSpec((REC_ROWS, HEAD), lambda h, b: (b, h)),
                   pl.BlockSpec((REC_ROWS, HEAD), lambda h, b: (b, h)),
                   pl.BlockSpec((None, REC_CHUNKS, HEAD, HEAD), lambda h, b: (h, b, 0, 0))],
        out_shape=[jax.ShapeDtypeStruct((SEQ, D_MODEL), BF16),
                   jax.ShapeDtypeStruct((SEQ, D_MODEL), F32),
                   jax.ShapeDtypeStruct((N_HEADS, SEQ // CHUNK, HEAD, HEAD), F32)],
        scratch_shapes=[pltpu.VMEM((HEAD, HEAD), F32)],
        exchanges=exchanges)


OUT_ROWS = 512


def _out_proj_loss(y_pool, y_rec, w_out_g, x, target, gf):
    def body(yp_ref, yr_ref, w_ref, x_ref, t_ref, gf_ref, dout_ref, doutb_ref, part_ref):
        @pl.when(pl.program_id(0) == 0)
        def _():
            part_ref[...] = jnp.zeros_like(part_ref)
        halves = [pl.ds(a * (OUT_ROWS // 2), OUT_ROWS // 2) for a in range(2)]
        outs = [x_ref[r, :] + _dot(yp_ref[r, :], w_ref[0:D_MODEL, :])
                + _dot(yr_ref[r, :], w_ref[D_MODEL:2 * D_MODEL, :]) for r in halves]
        gf_v = gf_ref[...]
        for r, out in zip(halves, outs):
            inv = lax.rsqrt(jnp.mean(out * out, axis=-1, keepdims=True) + EPS)
            diff = out * inv * gf_v - t_ref[r, :]
            dyf = diff * (1.0 / D_MODEL)
            a = dyf * gf_v
            dout = inv * a - out * (inv * inv * inv) * jnp.mean(a * out, axis=-1, keepdims=True)
            dout_ref[r, :] = dout
            doutb_ref[r, :] = dout.astype(BF16)
            part_ref[0:1, :] += jnp.sum(dyf * out * inv, axis=0, keepdims=True)
            part_ref[1:2, :] += jnp.sum(diff * diff, axis=0, keepdims=True)

    row = lambda n: pl.BlockSpec((OUT_ROWS, n), lambda i: (i, 0))
    return pl.pallas_call(
        body, name="out_proj_loss", grid=(SEQ // OUT_ROWS,),
        in_specs=[row(D_MODEL), row(D_MODEL), pl.BlockSpec((2 * D_MODEL, D_MODEL), lambda i: (0, 0)),
                  row(D_MODEL), row(D_MODEL), pl.BlockSpec((1, D_MODEL), lambda i: (0, 0))],
        out_specs=[row(D_MODEL), row(D_MODEL), pl.BlockSpec((8, D_MODEL), lambda i: (0, 0))],
        out_shape=[jax.ShapeDtypeStruct((SEQ, D_MODEL), F32),
                   jax.ShapeDtypeStruct((SEQ, D_MODEL), BF16),
                   jax.ShapeDtypeStruct((8, D_MODEL), F32)],
        compiler_params=_params(dimension_semantics=("arbitrary",)),
    )(y_pool, y_rec, w_out_g, x, target, gf)


def _grad_w_out(y_pool, y_rec, dout_b):
    blk = W_OUT_SHARD // 2
    per = D_MODEL // blk
    n = 2 * per

    def body(yp_ref, yr_ref, d_ref, p32_ref, p16_ref, send_ref, recv_ref, send_sems, recv_sems):
        j = pl.program_id(0)
        x, y, c = _place()

        def copy(u):
            return _remote(send_ref.at[u], recv_ref.at[u], send_sems, recv_sems, u, (x, y, 1 - c))

        pl.when(j == 0)(_other_core_barrier)
        for i in range(n):
            @pl.when(j == i)
            def _(i=i):
                res = _dot_tn((yp_ref if i < per else yr_ref)[...], d_ref[...])

                @pl.when(i % 2 == c)
                def _():
                    p32_ref[i // 2] = res

                @pl.when(i % 2 != c)
                def _():
                    send_ref[i // 2] = res
                    copy(i // 2).start()

        @pl.when(j == n - 1)
        def _():
            for u in range(N_SHARDS):
                copy(u).wait_recv()
                tot = p32_ref[u] + recv_ref[u]
                p32_ref[u] = tot
                p16_ref[u] = tot.astype(BF16)
            for u in range(N_SHARDS):
                copy(u).wait_send()

    whole = pl.BlockSpec((N_SHARDS, blk, D_MODEL), lambda j: (0, 0, 0))
    return pl.pallas_call(
        body, name="grad_w_out", grid=(n,),
        in_specs=[pl.BlockSpec((SEQ, blk), lambda j: (0, jnp.minimum(j, per - 1))),
                  pl.BlockSpec((SEQ, blk), lambda j: (0, jnp.maximum(j - per, 0))),
                  pl.BlockSpec((SEQ, D_MODEL), lambda j: (0, 0))],
        out_specs=[whole, whole],
        out_shape=[jax.ShapeDtypeStruct((N_SHARDS, blk, D_MODEL), F32),
                   jax.ShapeDtypeStruct((N_SHARDS, blk, D_MODEL), BF16)],
        scratch_shapes=[pltpu.VMEM((N_SHARDS, blk, D_MODEL), F32), pltpu.VMEM((N_SHARDS, blk, D_MODEL), F32),
                        pltpu.SemaphoreType.DMA((N_SHARDS,)), pltpu.SemaphoreType.DMA((N_SHARDS,))],
        compiler_params=_params(dimension_semantics=("arbitrary",), collective_id=0),
    )(y_pool, y_rec, dout_b)


def _pool_bwd(proj, dout_b, w_out_g, pw_g, pool_scale, exchanges):
    n = POOL_ROWS + POOL_HALO
    per_half = N_GROUPS // 2

    def body(u_ref, gate_ref, d_ref, wo_ref, pw_ref, sc_ref,
             dp_ref, p32_ref, p16_ref, dsc_ref, dd_ref, ddw_ref, dpw_ref, keep_ref, send_ref, recv_ref,
             send_sems, recv_sems):
        g = pl.program_id(0)
        x_, y_, c = _place()

        def to_other_core(a):
            return _remote(send_ref.at[a], recv_ref.at[a], send_sems, recv_sems, a, (x_, y_, 1 - c))

        dpw_ref[...] = jnp.zeros_like(dpw_ref)
        dsc_ref[...] = jnp.zeros_like(dsc_ref)

        def first(ii, _):
            chunks = [POOL_PAIR * ii + a for a in range(POOL_PAIR)]
            rs = [pl.ds(pl.multiple_of(i * POOL_ROWS, POOL_ROWS), POOL_ROWS) for i in chunks]
            diffs = [_pool_diff(u_ref, i, g) for i in chunks]
            dbs = [d.astype(BF16) for d, _ in diffs]
            mixed = [_dot(db, pw_ref[...]) for db in dbs]
            dys = [_dot_nt(d_ref[r, :], wo_ref[...]) for r in rs]
            sc = sc_ref[...]
            dmbs = []
            for r, m, dy in zip(rs, mixed, dys):
                gate = gate_ref[r, :]
                sg = _sig(gate)
                silu = gate * sg
                dp_ref[1, r, :] = (dy * m * sc * (sg * (1.0 + gate * (1.0 - sg)))).astype(BF16)
                dsc_ref[...] += jnp.sum(dy * silu * m, axis=0, keepdims=True)
                dmbs.append((dy * silu * sc).astype(BF16))
            for db, dmb in zip(dbs, dmbs):
                dpw_ref[...] += _dot_tn(db, dmb)
            dds = [_dot_nt(dmb, pw_ref[...]) for dmb in dmbs]
            for r, dd, (_, inv_count) in zip(rs, dds, diffs):
                dd_ref[r, :] = dd
                ddw_ref[r, :] = dd * inv_count
            return 0
        lax.fori_loop(0, SEQ // POOL_ROWS // POOL_PAIR, first, 0)

        def second(i, _):
            r0 = i * POOL_ROWS
            r = pl.ds(pl.multiple_of(r0, POOL_ROWS), POOL_ROWS)
            last = i == SEQ // POOL_ROWS - 1
            after = ddw_ref[pl.ds(pl.multiple_of(jnp.minimum(r0 + POOL_ROWS, SEQ - POOL_HALO), 8), POOL_HALO), :]
            after = jnp.where(last, 0.0, after)
            ext = jnp.concatenate([ddw_ref[r, :], after], axis=0)
            s = _window_sums(ext, g, lambda k: n - (1 << k))[:POOL_ROWS, :]
            dp_ref[0, r, :] = (s - dd_ref[r, :]).astype(BF16)
            return 0
        lax.fori_loop(0, SEQ // POOL_ROWS, second, 0)

        for a in range(per_half):
            @pl.when((g % per_half == a) & (g // per_half == c))
            def _(a=a):
                keep_ref[a] = dpw_ref[...]

            @pl.when((g % per_half == a) & (g // per_half != c))
            def _(a=a):
                send_ref[a] = dpw_ref[...]
                to_other_core(a).start()

        @pl.when(g == N_GROUPS - 1)
        def _():
            for a in range(per_half):
                to_other_core(a).wait_recv()
                tot = keep_ref[a] + recv_ref[a]
                for s in range(N_SHARDS):
                    rows_s = tot[s * PW_SHARD:(s + 1) * PW_SHARD, :]
                    p32_ref[s, a * PW_SHARD:(a + 1) * PW_SHARD, :] = rows_s
                    p16_ref[s, a * PW_SHARD:(a + 1) * PW_SHARD, :] = rows_s.astype(BF16)
            for a in range(per_half):
                to_other_core(a).wait_send()

    pw_half = (N_SHARDS, per_half * PW_SHARD, PG)
    return _call(
        body, name="pool_bwd", grid=(N_GROUPS,),
        args=(proj, proj, dout_b, w_out_g, pw_g, pool_scale),
        in_specs=[_proj_cols(PG, 0, lambda g: (g, 0)), _proj_cols(PG, 1, lambda g: (g, 0)),
                  pl.BlockSpec((SEQ, D_MODEL), lambda g: (0, 0)),
                  pl.BlockSpec((PG, D_MODEL), lambda g: (g, 0)),
                  pl.BlockSpec((None, PG, PG), lambda g: (g, 0, 0)),
                  pl.BlockSpec((1, PG), lambda g: (0, g))],
        out_specs=[pl.BlockSpec((2, SEQ, PG), lambda g: (0, 0, g)),
                   pl.BlockSpec(pw_half, lambda g: (0, 0, 0)),
                   pl.BlockSpec(pw_half, lambda g: (0, 0, 0)),
                   pl.BlockSpec((1, PG), lambda g: (0, g))],
        out_shape=[jax.ShapeDtypeStruct((2, SEQ, D_MODEL), BF16),
                   jax.ShapeDtypeStruct(pw_half, F32),
                   jax.ShapeDtypeStruct(pw_half, BF16),
                   jax.ShapeDtypeStruct((1, D_MODEL), F32)],
        scratch_shapes=[pltpu.VMEM((SEQ, PG), F32), pltpu.VMEM((SEQ, PG), F32), pltpu.VMEM((PG, PG), F32)]
                       + [pltpu.VMEM((per_half, PG, PG), F32)] * 3
                       + [pltpu.SemaphoreType.DMA((per_half,)), pltpu.SemaphoreType.DMA((per_half,))],
        exchanges=exchanges)


HALF_HEADS = N_HEADS // 2
HALF_COLS = HALF_HEADS * HEAD


def _rec_bwd(proj, o_raw, st_prev, dout_b, w_out_g, lb_logits, rec_g, consts, h0, name, exchanges,
             gate_of=None, own_gate=True):
    n_sec = 4 if gate_of is None else 5

    def body(q_ref, f_ref, i_ref, rg_ref, o_ref, stp_ref, d_ref, wo_ref, lb_ref, g_ref,
             w_ref, lowt_ref, sym_ref, sign_ref, tri_ref, *rest):
        dr_ref, part_ref, dst_ref = rest[-3:]

        @pl.when(pl.program_id(1) == 0)
        def _():
            dst_ref[...] = jnp.zeros_like(dst_ref)
            part_ref[...] = jnp.zeros_like(part_ref)
        if gate_of is not None:
            rg2_ref, o2_ref, wo2_ref, g2_ref = rest[:4]
            rg, o = rg2_ref[...], o2_ref[...]
            sg = _sig(rg)
            inv = lax.rsqrt(jnp.mean(o * o, axis=-1, keepdims=True) + EPS)
            dy = _dot_nt(d_ref[...], wo2_ref[...])
            dr_ref[4] = (dy * (o * inv) * g2_ref[...] * (sg * (1.0 + rg * (1.0 - sg)))).astype(BF16)
        tril = (lax.broadcasted_iota(jnp.int32, (CHUNK, CHUNK), 0)
                > lax.broadcasted_iota(jnp.int32, (CHUNK, CHUNK), 1))
        lb = _lower_bound(lb_ref)
        grec = g_ref[...]
        dst = dst_ref[...]
        acc_grec = jnp.zeros((1, HEAD), F32)
        acc_lb = jnp.zeros((1, HEAD), F32)
        rows = lambda c: pl.ds(c * CHUNK, CHUNK)
        for c0 in reversed(range(0, REC_CHUNKS, REC_GROUP_BWD)):
            group = list(reversed(range(c0, c0 + REC_GROUP_BWD)))
            dys = [_dot_nt(d_ref[rows(c), :], wo_ref[...]) for c in group]
            dos = []
            for c, dy in zip(group, dys):
                rg = rg_ref[rows(c), :]
                o = o_ref[rows(c), :]
                sg = _sig(rg)
                silu = rg * sg
                inv = lax.rsqrt(jnp.mean(o * o, axis=-1, keepdims=True) + EPS)
                recn = o * inv
                if own_gate:
                    dr_ref[3, rows(c), :] = (dy * recn * grec * (sg * (1.0 + rg * (1.0 - sg)))).astype(BF16)
                acc_grec = acc_grec + jnp.sum(dy * silu * recn, axis=0, keepdims=True)
                drecn = dy * silu * grec
                dos.append(inv * drecn - o * (inv * inv * inv) * jnp.mean(drecn * o, axis=-1, keepdims=True))
            gated = [_gates(q_ref[rows(c), :], f_ref[rows(c), :], lb) for c in group]
            g2s = [_dot3(w_ref[...], g) * LOG2E for (_, _, _, _, g) in gated]
            levels = [_level_factors(g2, qs, k, sign_ref) for g2, (qs, _, _, k, _) in zip(g2s, gated)]
            a_ts = []
            for lev in levels:
                a_t = jnp.zeros((CHUNK, CHUNK), F32)
                for l, (xl, _, _, _) in enumerate(lev):
                    a_t = a_t + _dot_nt(xl, xl) * lowt_ref[l]
                a_ts.append(a_t.astype(BF16))
            dobs = [do.astype(BF16) for do in dos]
            vbs = [i_ref[rows(c), :].astype(BF16) for c in group]
            d_syms = [jnp.where(tril, _dot_nt(dob, vb), _dot_nt(vb, dob)) for dob, vb in zip(dobs, vbs)]
            dqs_is, dk_is = [], []
            for lev, d_sym in zip(levels, d_syms):
                dqs_i = jnp.zeros((CHUNK, HEAD), F32)
                both_i = jnp.zeros((CHUNK, HEAD), F32)
                for l, (xl, xlo, e, up) in enumerate(lev):
                    z = d_sym * sym_ref[l]
                    tmp = _dot(z.astype(BF16), jnp.concatenate([xl, xlo], axis=-1))
                    tmp = (tmp[:, :HEAD] + tmp[:, HEAD:]) * e
                    dqs_i = dqs_i + jnp.where(up, tmp, 0.0)
                    both_i = both_i + tmp
                dqs_is.append(dqs_i)
                dk_is.append(both_i - dqs_i)
            e_gs = [jnp.exp2(g2) for g2 in g2s]
            e_revs = [jnp.exp2(g2[CHUNK - 1:CHUNK, :] - g2) for g2 in g2s]
            e_lasts = [jnp.exp2(g2[CHUNK - 1:CHUNK, :]) for g2 in g2s]
            q_gs = [qs * e_g for (qs, _, _, _, _), e_g in zip(gated, e_gs)]
            kdecs = [k * e_rev for (_, _, _, k, _), e_rev in zip(gated, e_revs)]
            dv12 = [_dot(a_t, dob) + jnp.sum(qs * k, axis=-1, keepdims=True) * do
                    for a_t, dob, do, (qs, _, _, k, _) in zip(a_ts, dobs, dos, gated)]
            dq_gs = [_dot(dob, stp_ref[c].astype(BF16)) for c, dob in zip(group, dobs)]
            steps = [_dot_tn(dob, q_g.astype(BF16)) for dob, q_g in zip(dobs, q_gs)]
            dsts = []
            for e_last, step in zip(e_lasts, steps):
                dsts.append(dst)
                dst = dst * e_last + step
            dstbs = [d.astype(BF16) for d in dsts]
            dv3 = [_dot_nt(kdec.astype(BF16), dstb) for kdec, dstb in zip(kdecs, dstbs)]
            dkdecs = [_dot(vb, dstb) for vb, dstb in zip(vbs, dstbs)]
            dbig_gs, dg_lasts, dqss, dks = [], [], [], []
            for i, c in enumerate(group):
                qs, _, _, k, _ = gated[i]
                de_last = jnp.sum(stp_ref[c] * dsts[i], axis=0, keepdims=True)
                ddiag = jnp.sum(dos[i] * i_ref[rows(c), :], axis=-1, keepdims=True)
                dqss.append(dqs_is[i] + ddiag * k + dq_gs[i] * e_gs[i])
                dks.append(dk_is[i] + ddiag * qs + dkdecs[i] * e_revs[i])
                dg_rev = dkdecs[i] * kdecs[i]
                dg_lasts.append(jnp.sum(dg_rev, axis=0, keepdims=True) + de_last * e_lasts[i])
                dbig_gs.append(qs * dqs_is[i] - k * dk_is[i] + dq_gs[i] * q_gs[i] - dg_rev)
            dgs = [_dot3(tri_ref[...], dbig_g) + dg_last for dbig_g, dg_last in zip(dbig_gs, dg_lasts)]
            for i, c in enumerate(group):
                _, sf, f, _, _ = gated[i]
                q = q_ref[rows(c), :]
                df = dgs[i] / f - dks[i]
                dr_ref[1, rows(c), :] = (df * (1.0 - lb) * sf * (1.0 - sf)).astype(BF16)
                acc_lb = acc_lb + jnp.sum(df * (1.0 - sf), axis=0, keepdims=True)
                sq = _sig(q)
                dr_ref[0, rows(c), :] = (dqss[i] * (sq * (1.0 + q * (1.0 - sq)))).astype(BF16)
                dr_ref[2, rows(c), :] = (dv12[i] + dv3[i]).astype(BF16)
        dst_ref[...] = dst
        part_ref[0:1, :] += acc_grec
        part_ref[1:2, :] += acc_lb

    rev = lambda b: N_REC_BLK - 1 - b
    sec = lambda n: _proj_cols(HEAD, n, lambda h, b: (h0 + h, rev(b)), REC_ROWS)
    col_in = pl.BlockSpec((REC_ROWS, HEAD), lambda h, b: (rev(b), h0 + h))
    vec_in = lambda rows: pl.BlockSpec((rows, HEAD), lambda h, b: (0, h0 + h))
    full = lambda a: pl.BlockSpec(a.shape, lambda h, b: (0,) * a.ndim)
    extra_args, extra_specs = (), []
    if gate_of is not None:
        extra_args = (proj, o_raw, w_out_g, rec_g)
        extra_specs = [_proj_cols(HEAD, 5, lambda h, b: (gate_of + h, rev(b)), REC_ROWS),
                       pl.BlockSpec((REC_ROWS, HEAD), lambda h, b: (rev(b), gate_of + h)),
                       pl.BlockSpec((HEAD, D_MODEL), lambda h, b: (SEC_BLK + gate_of + h, 0)),
                       pl.BlockSpec((1, HEAD), lambda h, b: (0, gate_of + h))]
    return _call(
        body, name=name, grid=(HALF_HEADS, N_REC_BLK),
        args=(proj, proj, proj, proj, o_raw, st_prev, dout_b, w_out_g, lb_logits, rec_g,
              consts["tri"], consts["low_t"], consts["sym"], consts["sign"], consts["tri_t"]) + extra_args,
        in_specs=[sec(2), sec(3), sec(4), sec(5), col_in,
                  pl.BlockSpec((None, REC_CHUNKS, HEAD, HEAD), lambda h, b: (h0 + h, rev(b), 0, 0)),
                  pl.BlockSpec((REC_ROWS, D_MODEL), lambda h, b: (rev(b), 0)),
                  pl.BlockSpec((HEAD, D_MODEL), lambda h, b: (SEC_BLK + h0 + h, 0)),
                  vec_in(2), vec_in(1)] + [full(consts[n]) for n in ("tri", "low_t", "sym", "sign", "tri_t")]
                 + extra_specs,
        out_specs=[pl.BlockSpec((n_sec, REC_ROWS, HEAD), lambda h, b: (0, rev(b), h)),
                   pl.BlockSpec((8, HEAD), lambda h, b: (0, h))],
        out_shape=[jax.ShapeDtypeStruct((n_sec, SEQ, HALF_COLS), BF16),
                   jax.ShapeDtypeStruct((8, HALF_COLS), F32)],
        scratch_shapes=[pltpu.VMEM((HEAD, HEAD), F32)],
        exchanges=exchanges)


def _w_in_block(w_ref, j):
    per_shard = W_IN_SHARD // COL_BLK
    return w_ref[j // per_shard, j % per_shard]


def _grad_x(dproj, w_in_g, x, g1, dout, exchanges):
    rows = 512
    n_blk = len(dproj)

    def body(*refs):
        dp_refs = refs[:n_blk]
        w_ref, x_ref, g_ref, dout_ref, dx_ref, part_ref = refs[n_blk:]

        @pl.when(pl.program_id(0) == 0)
        def _():
            part_ref[...] = jnp.zeros_like(part_ref)
        dh = jnp.zeros((rows, D_MODEL), F32)
        for j in range(n_blk):
            dh = dh + _dot_nt(dp_refs[j][...], _w_in_block(w_ref, j))
        xv = x_ref[...]
        inv = lax.rsqrt(jnp.mean(xv * xv, axis=-1, keepdims=True) + EPS)
        a = dh * g_ref[...]
        dx_ref[...] = (dout_ref[...] + inv * a
                       - xv * (inv * inv * inv) * jnp.mean(a * xv, axis=-1, keepdims=True))
        part_ref[0:1, :] += jnp.sum(dh * xv * inv, axis=0, keepdims=True)

    row = lambda: pl.BlockSpec((rows, D_MODEL), lambda i: (i, 0))
    dp_spec = lambda sec, cb: pl.BlockSpec((None, rows, COL_BLK), lambda i: (sec, i, cb))
    return _call(
        body, name="grad_x", grid=(SEQ // rows,),
        args=tuple(a for a, _, _ in dproj) + (w_in_g, x, g1, dout),
        in_specs=[dp_spec(sec, cb) for _, sec, cb in dproj]
                 + [pl.BlockSpec(w_in_g.shape, lambda i: (0, 0, 0, 0)),
                    row(), pl.BlockSpec((1, D_MODEL), lambda i: (0, 0)), row()],
        out_specs=[row(), pl.BlockSpec((8, D_MODEL), lambda i: (0, 0))],
        out_shape=[jax.ShapeDtypeStruct((SEQ, D_MODEL), F32),
                   jax.ShapeDtypeStruct((8, D_MODEL), F32)],
        exchanges=exchanges)


def _grad_w_in(h, dp, blocks, name, collective_id):
    n_blk = len(blocks)
    half = D_MODEL // 2
    pick = lambda vals: (lambda j: functools.reduce(lambda acc, iv: jnp.where(j == iv[0], iv[1], acc),
                                                     list(enumerate(vals))[1:], vals[0]))
    sec_of = pick([sec for sec, _ in blocks])
    cb_of = pick([cb for _, cb in blocks])

    def body(h_ref, dp_ref, p32_ref, p16_ref, keep_ref, send_ref, recv_ref, send_sems, recv_sems):
        j = pl.program_id(0)
        x, y, c = _place()
        cols = lambda cc: pl.ds(pl.multiple_of(cc * half, half), half)

        def copy(i):
            return _remote(send_ref.at[i], recv_ref.at[i], send_sems, recv_sems, i, (x, y, 1 - c))

        pl.when(j == 0)(_other_core_barrier)
        for i in range(n_blk + 1):
            @pl.when(j == i)
            def _(i=i):
                if i < n_blk:
                    send_ref[i] = _dot_tn(h_ref[:, cols(1 - c)], dp_ref[...])
                    copy(i).start()
                    keep_ref[i] = _dot_tn(h_ref[:, cols(c)], dp_ref[...])
                if i > 0:
                    copy(i - 1).wait_recv()
                    tot = keep_ref[i - 1] + recv_ref[i - 1]
                    p32_ref[...] = tot
                    p16_ref[...] = tot.astype(BF16)

        @pl.when(j == n_blk)
        def _():
            for i in range(n_blk):
                copy(i).wait_send()

    lagged = pl.BlockSpec((None, half, COL_BLK), lambda j: (jnp.maximum(j - 1, 0), 0, 0))
    last = n_blk - 1
    return pl.pallas_call(
        body, name=name, grid=(n_blk + 1,),
        in_specs=[pl.BlockSpec((SEQ, D_MODEL), lambda j: (0, 0)),
                  pl.BlockSpec((None, SEQ, COL_BLK),
                               lambda j: (sec_of(jnp.minimum(j, last)), 0, cb_of(jnp.minimum(j, last))))],
        out_specs=[lagged, lagged],
        out_shape=[jax.ShapeDtypeStruct((n_blk, half, COL_BLK), F32),
                   jax.ShapeDtypeStruct((n_blk, half, COL_BLK), BF16)],
        scratch_shapes=[pltpu.VMEM((n_blk, half, COL_BLK), F32)] * 3
                       + [pltpu.SemaphoreType.DMA((n_blk,)), pltpu.SemaphoreType.DMA((n_blk,))],
        compiler_params=_params(dimension_semantics=("arbitrary",), collective_id=collective_id),
    )(h, dp)


def _sum_units(part32, recv16, place, owners, tile, name):
    n, half, cols = part32.shape
    per_half = half // tile
    table = np.array([[sum(o == chip for o in owners)] + sorted(range(n), key=lambda j: (owners[j] != chip, j))
                      for chip in range(N_SHARDS)], np.int32)
    sched = jnp.concatenate([place[:1], jnp.asarray(table)[place[1]]])

    def block(k, i, p):
        live = k < p[1]
        unit = p[2 + jnp.minimum(k, jnp.maximum(p[1] - 1, 0))]
        return unit, jnp.where(live, i, per_half - 1)

    def body(sched_ref, p_ref, r_ref, o_ref):
        @pl.when(pl.program_id(0) < sched_ref[1])
        def _():
            acc = p_ref[...]
            for j in range(len(CHIP_FLIPS)):
                acc = acc + r_ref[j].astype(F32)
            o_ref[...] = acc

    return pl.pallas_call(
        body, name=name,
        grid_spec=pltpu.PrefetchScalarGridSpec(
            num_scalar_prefetch=1, grid=(n, per_half),
            in_specs=[pl.BlockSpec((None, tile, cols), lambda k, i, p: (*block(k, i, p), 0)),
                      pl.BlockSpec((None, len(CHIP_FLIPS), tile, cols),
                                   lambda k, i, p: (block(k, i, p)[0], 0, block(k, i, p)[1], 0))],
            out_specs=pl.BlockSpec((None, tile, cols),
                                   lambda k, i, p: (block(k, i, p)[0], p[0] * per_half + block(k, i, p)[1], 0))),
        out_shape=jax.ShapeDtypeStruct((n, 2 * half, cols), F32),
        compiler_params=_params(dimension_semantics=("arbitrary", "arbitrary")),
    )(sched, part32, recv16)


def _adamw_math(w, g, m, v):
    m = ADAM_B1 * m + (1.0 - ADAM_B1) * g
    v = ADAM_B2 * v + (1.0 - ADAM_B2) * (g * g)
    m_hat = m / (1.0 - ADAM_B1 ** ADAM_STEP)
    v_hat = v / (1.0 - ADAM_B2 ** ADAM_STEP)
    delta = -ADAM_LR * (m_hat / (jnp.sqrt(v_hat) + ADAM_EPS) + ADAM_WD * w)
    return delta, m, v


def _adamw_units(w, m, v, grads, pick, name):
    rows, cols = w.shape
    bc = grads[0].shape[-1]
    tile = min(rows, 256)
    n_g = len(grads)

    def body(pick_ref, w_ref, m_ref, v_ref, *refs):
        g_refs, (g_out, d_ref, nm_ref, nv_ref) = refs[:n_g], refs[n_g:]
        p = pl.program_id(0)
        for a in range(n_g):
            @pl.when(pick_ref[0, p] == a)
            def _(a=a):
                g = g_refs[a][...]
                g_out[...] = g
                d_ref[...], nm_ref[...], nv_ref[...] = _adamw_math(w_ref[...], g, m_ref[...], v_ref[...])

    blk = pl.BlockSpec((tile, bc), lambda p, i, pick: (i, p))

    def g_spec(a):
        return pl.BlockSpec((None, tile, bc),
                            lambda p, i, pick: (jnp.where(pick[0, p] == a, pick[1, p], 0),
                                                jnp.where(pick[0, p] == a, i, 0), 0))

    return pl.pallas_call(
        body, name=name,
        grid_spec=pltpu.PrefetchScalarGridSpec(
            num_scalar_prefetch=1, grid=(cols // bc, rows // tile),
            in_specs=[blk] * 3 + [g_spec(a) for a in range(n_g)],
            out_specs=[blk] * 4),
        out_shape=[jax.ShapeDtypeStruct(w.shape, F32)] * 4,
        compiler_params=_params(dimension_semantics=("arbitrary", "arbitrary")),
    )(pick, w, m, v, *grads)


ROW_NORM1, ROW_SCALE, ROW_LB, ROW_REC, ROW_FINAL, ROW_LOSS = 0, 1, 2, 4, 5, 6


SMALL_ROWS = (ROW_NORM1, ROW_SCALE, ROW_LB, ROW_REC, ROW_FINAL)


def _small_update(parts, gathered, params):
    n_p = len(params)

    def body(own_ref, p_ref, *refs):
        ins, loss_ref, outs = refs[:3 * n_p], refs[3 * n_p], refs[3 * n_p + 1:]
        x, y, c = _place()
        me = 4 * x + 2 * y + c
        slot = lambda d: jnp.where(me == d, own_ref[...], p_ref[d])
        tot = slot(0)
        for d in range(1, 8):
            tot = tot + slot(d)
        for i, r in enumerate(SMALL_ROWS):
            w = ins[3 * i][...]
            g = tot[r:r + 1, :]
            if r == ROW_LB:
                mx = jnp.maximum(w[0:1, :], w[1:2, :])
                e0 = jnp.exp(w[0:1, :] - mx)
                e1 = jnp.exp(w[1:2, :] - mx)
                lb = e0 / (e0 + e1)
                g = g * lb * (1.0 - lb)
                g = jnp.concatenate([g, -g], axis=0)
            outs[4 * i][...] = g
            outs[4 * i + 1][...], outs[4 * i + 2][...], outs[4 * i + 3][...] = _adamw_math(
                w, g, ins[3 * i + 1][...], ins[3 * i + 2][...])
        loss_ref[...] = (0.5 / D_MODEL) * jnp.sum(tot[ROW_LOSS:ROW_LOSS + 1, :], axis=-1, keepdims=True)

    flat = [a for wmv in params for a in wmv]
    return pl.pallas_call(
        body, name="small_update",
        out_shape=[jax.ShapeDtypeStruct((1, 1), F32)]
                  + [jax.ShapeDtypeStruct(w.shape, F32) for w, _, _ in params for _ in range(4)],
        compiler_params=_params(),
    )(parts, gathered, *flat)


SHARD_OWNERS = tuple(range(N_SHARDS))
BLOCKS_POOL = (0, 1, 2, 3)
BLOCKS_A = (4, 6, 8, 10, 11)
BLOCKS_B = (5, 7, 9)
BLOCK_GROUPS = (BLOCKS_POOL, BLOCKS_A, BLOCKS_B)


def _block_owners(blocks):
    return tuple(j // (W_IN_SHARD // COL_BLK) for j in blocks)


def kernel(x, norm1_g, w_in, pool_w, pool_scale, lb_logits, rec_norm_g, w_out, final_norm_g, loss_target, m_norm1_g, m_w_in, m_pool_w, m_pool_scale, m_lb_logits, m_rec_norm_g, m_w_out, m_final_norm_g, v_norm1_g, v_w_in, v_pool_w, v_pool_scale, v_lb_logits, v_rec_norm_g, v_w_out, v_final_norm_g):
    xi, yi, ci = _place()
    chip = 2 * xi + yi
    place = jnp.stack([ci, chip]).astype(jnp.int32)
    pw_rows = N_GROUPS * PW_SHARD
    flat_pw = lambda a: a.reshape(pw_rows, PG)
    x2, target, gf = x[0], loss_target[0], final_norm_g.reshape(1, D_MODEL)
    consts = {n: jnp.asarray(a, BF16 if n.startswith("tri") else F32) for n, a in _chunk_constants().items()}

    proj, h, w_in_g, w_out_slots, pw_slots = _in_proj(x2, norm1_g, _cast_w_in(w_in[0], place), place,
                                                      [w_out[0], flat_pw(pool_w)])
    (y_rec, o_raw, st_prev), ((w_out_g, pw_g),) = _rec_fwd(
        proj, lb_logits, rec_norm_g, consts, [_ex_gather([w_out_slots, pw_slots])])
    w_out_g = w_out_g.reshape(2 * D_MODEL, D_MODEL)
    pw_full = pw_g.reshape(N_SHARDS, N_GROUPS, PW_SHARD, PG).transpose(1, 0, 2, 3).reshape(N_GROUPS, PG, PG)
    y_pool = _pool_fwd(proj, pw_full, pool_scale)
    dout, dout_b, part_out = _out_proj_loss(y_pool, y_rec, w_out_g, x2, target, gf)

    p_out32, p_out16 = _grad_w_out(y_pool, y_rec, dout_b)
    (dpool, p_pw32, p_pw16, dscale), ((rb_out,),) = _pool_bwd(proj, dout_b, w_out_g, pw_full, pool_scale,
                                                              [_ex_send([p_out16], [SHARD_OWNERS])])
    g_out = _sum_units(p_out32, rb_out, place, SHARD_OWNERS, 256, "sum_w_out")
    p_inp32, p_inp16 = _grad_w_in(h, dpool, [(0, 0), (0, 1), (1, 0), (1, 1)], "grad_w_in_pool", 1)

    pool_owners, a_owners, b_owners = (_block_owners(b) for b in BLOCK_GROUPS)
    rec_args = (proj, o_raw, st_prev, dout_b, w_out_g, lb_logits, rec_norm_g, consts)
    (drec_a, part_a), ((rb_inp,),) = _rec_bwd(
        *rec_args, 0, "rec_bwd_a", [_ex_send([p_inp16], [pool_owners], units=[(0, 1)])], gate_of=HALF_HEADS)
    p_ina32, p_ina16 = _grad_w_in(h, drec_a, [(n, 0) for n in range(5)], "grad_w_in_a", 2)

    (drec_b, part_b), ((rb_inp, rb_ina, rb_pw),) = _rec_bwd(
        *rec_args, HALF_HEADS, "rec_bwd_b",
        [_ex_send([p_inp16, p_ina16, p_pw16], [pool_owners, a_owners, SHARD_OWNERS],
                  units=[(2, 3), tuple(range(len(a_owners))), SHARD_OWNERS], landed=[rb_inp, None, None])],
        own_gate=False)
    g_inp = _sum_units(p_inp32, rb_inp, place, pool_owners, 256, "sum_w_in_pool")
    g_ina = _sum_units(p_ina32, rb_ina, place, a_owners, 256, "sum_w_in_a")
    g_pw = _sum_units(p_pw32, rb_pw, place, SHARD_OWNERS, 128, "sum_pool_w")
    p_inb32, p_inb16 = _grad_w_in(h, drec_b, [(n, 0) for n in range(3)], "grad_w_in_b", 3)

    dproj = ([(dpool, 0, 0), (dpool, 0, 1), (dpool, 1, 0), (dpool, 1, 1)]
             + [(d, n, 0) for n in range(3) for d in (drec_a, drec_b)] + [(drec_a, 3, 0), (drec_a, 4, 0)])
    (dx, part_x), ((rb_inb,),) = _grad_x(dproj, w_in_g, x2, norm1_g, dout, [_ex_send([p_inb16], [b_owners])])
    g_inb = _sum_units(p_inb32, rb_inb, place, b_owners, 256, "sum_w_in_b")
    zero = jnp.zeros((1, D_MODEL), F32)
    part_rec = jnp.concatenate([part_a, part_b], axis=1)
    parts = jnp.concatenate([part_x[0:1], dscale, part_rec[1:2], zero, part_rec[0:1], part_out[0:1],
                             part_out[1:2], zero], axis=0)
    _, ((g_out, g_pw, g_inp, g_ina, g_inb), (gathered,)) = _call(
        None, name="join_halves",
        exchanges=[_ex_join([g_out, g_pw, g_inp, g_ina, g_inb],
                            [SHARD_OWNERS, SHARD_OWNERS, pool_owners, a_owners, b_owners]),
                   _ex_gather_small(parts)])

    group_of = np.zeros((D_PROJ // COL_BLK,), np.int32)
    index_of = np.zeros((D_PROJ // COL_BLK,), np.int32)
    for gi, blocks in enumerate(BLOCK_GROUPS):
        for i, j in enumerate(blocks):
            group_of[j], index_of[j] = gi, i
    per_shard = W_IN_SHARD // COL_BLK
    pick_in = jnp.stack([lax.dynamic_slice(jnp.asarray(group_of), (per_shard * chip,), (per_shard,)),
                         lax.dynamic_slice(jnp.asarray(index_of), (per_shard * chip,), (per_shard,))])
    pick_own = jnp.stack([jnp.zeros((1,), jnp.int32), chip.reshape(1).astype(jnp.int32)])
    big = [_adamw_units(w_in[0], m_w_in[0], v_w_in[0], [g_inp, g_ina, g_inb], pick_in, "adamw_w_in"),
           _adamw_units(w_out[0], m_w_out[0], v_w_out[0], [g_out], pick_own, "adamw_w_out"),
           _adamw_units(flat_pw(pool_w), flat_pw(m_pool_w), flat_pw(v_pool_w), [g_pw], pick_own, "adamw_pool_w")]

    row = lambda a: a.reshape(1, D_MODEL)
    loss, *small = _small_update(parts, gathered, [
        (norm1_g, m_norm1_g, v_norm1_g), (pool_scale, m_pool_scale, v_pool_scale),
        (lb_logits, m_lb_logits, v_lb_logits), (rec_norm_g, m_rec_norm_g, v_rec_norm_g),
        (row(final_norm_g), row(m_final_norm_g), row(v_final_norm_g))])

    def leaves(k):
        norm1, scale, lb, rec, final = (small[4 * i + k] for i in range(len(SMALL_ROWS)))
        return (norm1, big[0][k][None], big[2][k].reshape(pool_w.shape), scale, lb, rec,
                big[1][k][None], final.reshape(D_MODEL))

    return (loss.reshape(()), dx[None], *leaves(0), *leaves(1), *leaves(2), *leaves(3))
```

```python
import functools

import numpy as np
import jax
import jax.numpy as jnp
from jax import lax
from jax.experimental import pallas as pl
from jax.experimental.pallas import tpu as pltpu

F32 = jnp.float32
BF16 = jnp.bfloat16

SEQ = 2048
D_MODEL = 1024
D_PROJ = 6144
N_SEC = 6
N_GROUPS = 4
PG = 256
N_HEADS = 8
HEAD = 128
CHUNK = 64
N_LEVELS = 6
N_SHARDS = 4
W_IN_SHARD = D_PROJ // N_SHARDS
W_OUT_SHARD = 2048 // N_SHARDS
PW_SHARD = PG // N_SHARDS
COL_BLK = 512
EPS = 1e-6

ADAM_LR = 0.001
ADAM_B1 = 0.9
ADAM_B2 = 0.999
ADAM_EPS = 1e-08
ADAM_WD = 0.01
ADAM_STEP = 10

V7X_VMEM_LIMIT = 56 * 1024 * 1024
MESH = pl.DeviceIdType.MESH


def _params(**kw):
    return pltpu.CompilerParams(vmem_limit_bytes=V7X_VMEM_LIMIT, **kw)


def _sig(x):
    return 1.0 / (1.0 + jnp.exp(-x))


def _dot(a, b):
    return jnp.dot(a, b, preferred_element_type=F32)


def _dot_nt(a, b):
    return lax.dot_general(a, b, (((1,), (1,)), ((), ())), preferred_element_type=F32)


def _dot_tn(a, b):
    return lax.dot_general(a, b, (((0,), (0,)), ((), ())), preferred_element_type=F32)


def _split3(a):
    p1 = a.astype(BF16)
    r1 = a - p1.astype(F32)
    p2 = r1.astype(BF16)
    p3 = (r1 - p2.astype(F32)).astype(BF16)
    return jnp.concatenate([p1, p2, p3], axis=-1)


def _dot3(w01, a):
    n = a.shape[-1]
    r = _dot(w01, _split3(a))
    return r[:, :n] + r[:, n:2 * n] + r[:, 2 * n:]


def _chunk_constants():
    j = np.arange(CHUNK)
    tt, ss = np.meshgrid(j, j, indexing="ij")
    x = tt ^ ss
    hb = np.full((CHUNK, CHUNK), -1, np.int32)
    for l in range(N_LEVELS):
        hb[x >= (1 << l)] = l
    sym = np.stack([(hb == l) for l in range(N_LEVELS)]).astype(np.float32)
    low = sym * (tt > ss)
    sign = np.stack([np.where((j >> l) & 1, 1.0, -1.0) for l in range(N_LEVELS)]).astype(np.float32)
    sign = np.ascontiguousarray(np.broadcast_to(sign[:, :, None], (N_LEVELS, CHUNK, HEAD)))
    tri = (ss <= tt).astype(np.float32)
    return dict(tri=tri, tri_t=np.ascontiguousarray(tri.T), low=low,
                low_t=np.ascontiguousarray(low.transpose(0, 2, 1)), sym=sym, sign=sign)


def _in_proj(x, g1, w_slots, place, also_cast):
    n_col = D_PROJ // COL_BLK
    per_shard = W_IN_SHARD // COL_BLK
    rows = 1024
    half_rows = D_MODEL // 2
    quarter_rows = D_MODEL // 4
    FLIPS = (0, 2, 1, 3)
    ORDER = ([(0, p) for p in range(per_shard)] + [(m, p) for p in range(per_shard) for m in (1, 2)]
             + [(3, p) for p in range(per_shard)])

    def shard_at(m, chip):
        return chip ^ FLIPS[m]

    def pick(vals, t):
        return functools.reduce(lambda acc, iv: jnp.where(t == iv[0], iv[1], acc), list(enumerate(vals))[1:], vals[0])

    def body(place_ref, x_hbm, g_ref, w_in_ref, *rest):
        others, (proj_ref, h_ref, w_ref) = rest[:len(also_cast)], rest[len(also_cast):len(also_cast) + 3]
        slots = rest[len(also_cast) + 3:2 * len(also_cast) + 3]
        wbuf, load_sems, send_sems, recv_sems, x_ref, x_sem = rest[2 * len(also_cast) + 3:]
        t = pl.program_id(0)
        load_x = pltpu.make_async_copy(x_hbm, x_ref, x_sem)

        @pl.when(t == 1)
        def _():
            for src, dst in zip(others, slots):
                dst[...] = src[...].astype(BF16)
        x_, y_, c = _place()
        chip = 2 * x_ + y_
        me, other_core = (x_, y_, c), (x_, y_, 1 - c)
        x_nbr, y_nbr = (1 - x_, y_, c), (x_, 1 - y_, c)

        def rows_of(half, q=None):
            if q is None:
                return pl.ds(pl.multiple_of(half * half_rows, half_rows), half_rows)
            return pl.ds(pl.multiple_of(half * half_rows + q * quarter_rows, quarter_rows), quarter_rows)

        def block(m, p, r):
            return w_ref.at[shard_at(m, chip), p, r, :]

        def copy(k, ref, to):
            return _remote(ref, ref, send_sems, recv_sems, k, to)

        direct = lambda n, p, to: copy(3 * n + p, block(0, p, rows_of(c)), to)
        relay = lambda n, p, to: copy(6 + 3 * n + p, block(1 + n, p, rows_of(c, n)), to)
        arrived = lambda m, p: ([copy(3 * (m - 1) + p, block(m, p, rows_of(c)), me)] if m < 3 else
                                [copy(6 + 3 * n + p, block(3, p, rows_of(c, n)), me) for n in (0, 1)])
        passed_on = lambda m, p, half, to: copy(9 + 3 * m + p, block(m, p, rows_of(half)), to)

        def load(m, p, slot):
            return pltpu.make_async_copy(w_ref.at[shard_at(m, chip), p], wbuf.at[slot], load_sems.at[slot])

        def prepare(m, p):
            for cp in arrived(m, p):
                cp.wait_recv()
            passed_on(m, p, c, other_core).start()
            if m < 3:
                relay(m - 1, p, y_nbr if m == 1 else x_nbr).start()

        @pl.when(t == 0)
        def _():
            for p in range(per_shard):
                direct(0, p, x_nbr).start()
                direct(1, p, y_nbr).start()
            load_x.start()
            for p in range(per_shard):
                load(0, p, p).start()
            load_x.wait()

            def norm(i, _):
                r = pl.ds(pl.multiple_of(i * rows, rows), rows)
                xv = x_ref[r, :]
                inv = lax.rsqrt(jnp.mean(xv * xv, axis=-1, keepdims=True) + EPS)
                h_ref[r, :] = (xv * inv * g_ref[...]).astype(BF16)
                return 0
            lax.fori_loop(0, SEQ // rows, norm, 0)

        for step, (m, p) in enumerate(ORDER):
            @pl.when(t == step)
            def _(step=step, m=m, p=p):
                slot = step % per_shard
                if m > 0:
                    passed_on(m, p, 1 - c, me).wait_recv()
                    load(m, p, slot).start()
                if step + 1 < n_col and ORDER[step + 1][0] > 0:
                    prepare(*ORDER[step + 1])
                load(m, p, slot).wait()

                def mm(i, _):
                    r = pl.ds(pl.multiple_of(i * rows, rows), rows)
                    proj_ref[r, :] = _dot(h_ref[r, :], wbuf[slot])
                    return 0
                lax.fori_loop(0, SEQ // rows, mm, 0)

        @pl.when(t == n_col - 1)
        def _():
            for p in range(per_shard):
                sent = [direct(0, p, x_nbr), direct(1, p, y_nbr), relay(0, p, y_nbr), relay(1, p, x_nbr)]
                for cp in sent + [passed_on(m, p, c, other_core) for m in (1, 2, 3)]:
                    cp.wait_send()

    return pl.pallas_call(
        body, name="in_proj",
        grid_spec=pltpu.PrefetchScalarGridSpec(
            num_scalar_prefetch=1, grid=(n_col,),
            in_specs=[pl.BlockSpec(memory_space=pl.ANY),
                      pl.BlockSpec((1, D_MODEL), lambda t, p: (0, 0)),
                      pl.BlockSpec(memory_space=pl.ANY)]
                     + [pl.BlockSpec(a.shape, lambda t, p: (0, 0)) for a in also_cast],
            out_specs=[pl.BlockSpec((None, SEQ, COL_BLK),
                                    lambda t, p: (per_shard * (p[1] ^ pick([FLIPS[m] for m, _ in ORDER], t))
                                                  + pick([b for _, b in ORDER], t), 0, 0)),
                       pl.BlockSpec((SEQ, D_MODEL), lambda t, p: (0, 0)),
                       pl.BlockSpec(memory_space=pl.ANY)]
                      + [pl.BlockSpec((None,) + a.shape, lambda t, p: (p[1], 0, 0)) for a in also_cast],
            scratch_shapes=[pltpu.VMEM((per_shard, D_MODEL, COL_BLK), BF16),
                            pltpu.SemaphoreType.DMA((per_shard,)),
                            pltpu.SemaphoreType.DMA((21,)), pltpu.SemaphoreType.DMA((21,)),
                            pltpu.VMEM((SEQ, D_MODEL), F32), pltpu.SemaphoreType.DMA]),
        out_shape=[jax.ShapeDtypeStruct((n_col, SEQ, COL_BLK), F32),
                   jax.ShapeDtypeStruct((SEQ, D_MODEL), BF16),
                   jax.ShapeDtypeStruct(w_slots.shape, BF16)]
                  + [jax.ShapeDtypeStruct((N_SHARDS,) + a.shape, BF16) for a in also_cast],
        input_output_aliases={3: 2},
        compiler_params=_params(dimension_semantics=("arbitrary",)),
    )(place, x, g1, w_slots, *also_cast)


def _proj_cols(width, section, where, rows=SEQ):
    per_blk = COL_BLK // width

    def index(*grid):
        k, r = where(*grid)
        return section * (D_MODEL // COL_BLK) + k // per_blk, r, k % per_blk

    return pl.BlockSpec((None, rows, width), index)


POOL_ROWS = 256
POOL_HALO = 16
POOL_PAIR = 4


def _window_sums(ext, g, shift_of):
    s = ext
    for k in range(N_GROUPS):
        s = jnp.where(k <= g, s + pltpu.roll(s, shift_of(k), 0), s)
    return s


def _pool_diff(u_ref, i, g):
    n = POOL_ROWS + POOL_HALO
    r0 = i * POOL_ROWS
    cur = u_ref[pl.ds(pl.multiple_of(r0, POOL_ROWS), POOL_ROWS), :]
    before = u_ref[pl.ds(pl.multiple_of(jnp.maximum(r0 - POOL_HALO, 0), 8), POOL_HALO), :]
    before = jnp.where(i > 0, before, 0.0)
    ext = jnp.concatenate([before, cur], axis=0)
    s = _window_sums(ext, g, lambda k: 1 << k)[POOL_HALO:, :]
    t = r0 + lax.broadcasted_iota(jnp.int32, (POOL_ROWS, 1), 0)
    width = (2 << g).astype(F32)
    inv_count = 1.0 / jnp.minimum((t + 1).astype(F32), width)
    return s * inv_count - cur, inv_count


def _pool_fwd(proj, pw_g, pool_scale):
    def body(u_ref, gate_ref, pw_ref, sc_ref, y_ref):
        g = pl.program_id(0)

        def step(ii, _):
            chunks = [POOL_PAIR * ii + a for a in range(POOL_PAIR)]
            ds = [_pool_diff(u_ref, i, g)[0].astype(BF16) for i in chunks]
            mixed = [_dot(d, pw_ref[...]) for d in ds]
            for i, m in zip(chunks, mixed):
                r = pl.ds(pl.multiple_of(i * POOL_ROWS, POOL_ROWS), POOL_ROWS)
                gate = gate_ref[r, :]
                y_ref[r, :] = (m * sc_ref[...] * (gate * _sig(gate))).astype(BF16)
            return 0
        lax.fori_loop(0, SEQ // POOL_ROWS // POOL_PAIR, step, 0)

    return pl.pallas_call(
        body, name="pool_fwd", grid=(N_GROUPS,),
        in_specs=[_proj_cols(PG, 0, lambda g: (g, 0)), _proj_cols(PG, 1, lambda g: (g, 0)),
                  pl.BlockSpec((None, PG, PG), lambda g: (g, 0, 0)),
                  pl.BlockSpec((1, PG), lambda g: (0, g))],
        out_specs=pl.BlockSpec((SEQ, PG), lambda g: (0, g)),
        out_shape=jax.ShapeDtypeStruct((SEQ, D_MODEL), BF16),
        compiler_params=_params(dimension_semantics=("arbitrary",)),
    )(proj, proj, pw_g, pool_scale)


REC_ROWS = 1024
REC_CHUNKS = REC_ROWS // CHUNK
N_REC_BLK = SEQ // REC_ROWS
REC_GROUP = REC_CHUNKS
REC_GROUP_BWD = REC_CHUNKS
SEC_BLK = D_MODEL // HEAD


def _lower_bound(lb_ref):
    l0 = lb_ref[0:1, :]
    l1 = lb_ref[1:2, :]
    mx = jnp.maximum(l0, l1)
    e0 = jnp.exp(l0 - mx)
    e1 = jnp.exp(l1 - mx)
    return e0 / (e0 + e1)


def _gates(q, fl, lb):
    qs = q * _sig(q)
    sf = _sig(fl)
    f = lb + (1.0 - lb) * sf
    return qs, sf, f, 1.0 - f, jnp.log(f)


LOG2E = 1.4426950408889634


def _level_factors(g2, qs, k, sign_ref):
    t = lax.broadcasted_iota(jnp.int32, (CHUNK, HEAD), 0)
    row = lambda r, n: jnp.broadcast_to(g2[r:r + 1, :], (n, HEAD))
    out = []
    for l in range(N_LEVELS):
        m = 1 << l
        if l == 0:
            g_mid = jnp.where((t & 1) == 1, pltpu.roll(g2, 1, 0), g2)
        elif l == 1:
            low = (t & 7) < 4
            g_mid = jnp.concatenate([jnp.where(low[:8], row(8 * i + 1, 8), row(8 * i + 5, 8))
                                     for i in range(CHUNK // 8)], axis=0)
        else:
            g_mid = jnp.concatenate([row(b * 2 * m + m - 1, 2 * m) for b in range(CHUNK // (2 * m))], axis=0)
        sgn = sign_ref[l]
        up = sgn > 0.0
        e = jnp.exp2((g2 - g_mid) * sgn)
        x = jnp.where(up, qs, k) * e
        hi = x.astype(BF16)
        out.append((hi, (x - hi.astype(F32)).astype(BF16), e, up))
    return out


CHIP_FLIPS = ((1, 0), (0, 1), (1, 1))
HBM = pl.BlockSpec(memory_space=pl.ANY)


def _place():
    return lax.axis_index("x"), lax.axis_index("y"), lax.axis_index("c")


def _remote(src, dst, send_sems, recv_sems, k, to):
    return pltpu.make_async_remote_copy(src_ref=src, dst_ref=dst, send_sem=send_sems.at[k],
                                        recv_sem=recv_sems.at[k], device_id=to, device_id_type=MESH)


def _half_rows(ref, c):
    half = ref.shape[-2] // 2
    rows = pl.ds(pl.multiple_of(c * half, half), half)
    return ref.at[:, rows, :] if len(ref.shape) == 3 else ref.at[rows, :]


def _other_core_barrier():
    x, y, c = _place()
    sem = pltpu.get_barrier_semaphore()
    pl.semaphore_signal(sem, inc=1, device_id=(x, y, 1 - c), device_id_type=MESH)
    pl.semaphore_wait(sem, 1)


class _Exchange:
    def __init__(self, inputs, out_shapes, n_sems, start, finish, aliases=None):
        self.inputs, self.out_shapes, self.n_sems = list(inputs), list(out_shapes), n_sems
        self.start, self.finish, self.aliases = start, finish, dict(aliases or {})


def _ex_send(parts16, owners, units=None, landed=None):
    n_t = len(parts16)
    units = units or [tuple(range(len(o))) for o in owners]
    landed = landed or [None] * n_t
    given = [t for t in range(n_t) if landed[t] is not None]

    def each(ins, outs, send, recv, to_sender, to_owner):
        x, y, c = _place()
        k = 0
        for t, own in enumerate(owners):
            for j in units[t]:
                for r, (fx, fy) in enumerate(CHIP_FLIPS):
                    tx, ty = x ^ fx, y ^ fy
                    cp = _remote(ins[t].at[j], outs[t].at[j, r], send, recv, k, (tx, ty, c))
                    if to_sender is not None:
                        pl.when(2 * tx + ty == own[j])(functools.partial(to_sender, cp))
                    if to_owner is not None:
                        pl.when(2 * x + y == own[j])(functools.partial(to_owner, cp))
                    k += 1

    def start(*refs):
        each(*refs, lambda cp: cp.start(), None)

    def finish(*refs):
        each(*refs, None, lambda cp: cp.wait_recv())
        each(*refs, lambda cp: cp.wait_send(), None)

    shapes = [jax.ShapeDtypeStruct((a.shape[0], len(CHIP_FLIPS)) + a.shape[1:], BF16) for a in parts16]
    return _Exchange(list(parts16) + [landed[t] for t in given], shapes,
                     len(CHIP_FLIPS) * sum(len(u) for u in units), start, finish,
                     aliases={n_t + i: t for i, t in enumerate(given)})


def _ex_join(units, owners):
    def each(ins, outs, send, recv, fn):
        x, y, c = _place()
        k = 0
        for t, own in enumerate(owners):
            for j, o in enumerate(own):
                def half(cc, to, u=outs[t].at[j], k=k):
                    return _remote(_half_rows(u, cc), _half_rows(u, cc), send, recv, k, to)
                mine = functools.partial(half, c, (x, y, 1 - c))
                theirs = functools.partial(half, 1 - c, (x, y, c))
                pl.when(2 * x + y == o)(functools.partial(fn, mine, theirs))
                k += 1

    def start(*refs):
        each(*refs, lambda mine, theirs: mine().start())

    def finish(*refs):
        each(*refs, lambda mine, theirs: theirs().wait_recv())
        each(*refs, lambda mine, theirs: mine().wait_send())

    shapes = [jax.ShapeDtypeStruct(a.shape, F32) for a in units]
    return _Exchange(units, shapes, sum(len(o) for o in owners), start, finish,
                     aliases={t: t for t in range(len(units))})


def _ex_gather(slots):
    n_t = len(slots)
    n_fl = len(CHIP_FLIPS)

    def piece(ref, shard, half):
        return _half_rows(ref.at[shard], half)

    def first(outs, send, recv):
        x, y, c = _place()
        s = 2 * x + y
        return [_remote(piece(outs[t], s, c), piece(outs[t], s, c), send, recv, n_t * j + t, (x ^ fx, y ^ fy, c))
                for j, (fx, fy) in enumerate(CHIP_FLIPS) for t in range(n_t)]

    def start(ins, outs, send, recv):
        for cp in first(outs, send, recv):
            cp.start()

    def finish(ins, outs, send, recv):
        x, y, c = _place()
        passed = []
        for j, (fx, fy) in enumerate(CHIP_FLIPS):
            sj = 2 * (x ^ fx) + (y ^ fy)
            for t in range(n_t):
                k = n_t * j + t
                _remote(piece(outs[t], sj, c), piece(outs[t], sj, c), send, recv, k, (x, y, c)).wait_recv()
                cp = _remote(piece(outs[t], sj, c), piece(outs[t], sj, c), send, recv, n_t * n_fl + k, (x, y, 1 - c))
                cp.start()
                passed.append(cp)
        for j, (fx, fy) in enumerate(CHIP_FLIPS):
            sj = 2 * (x ^ fx) + (y ^ fy)
            for t in range(n_t):
                k = n_t * n_fl + n_t * j + t
                _remote(piece(outs[t], sj, 1 - c), piece(outs[t], sj, 1 - c), send, recv, k, (x, y, c)).wait_recv()
        for cp in first(outs, send, recv) + passed:
            cp.wait_send()

    shapes = [jax.ShapeDtypeStruct(a.shape, BF16) for a in slots]
    return _Exchange(slots, shapes, 2 * n_t * n_fl, start, finish, aliases={t: t for t in range(n_t)})


def _ex_gather_small(parts):
    def copies(ins, outs, send, recv):
        x, y, c = _place()
        me = 4 * x + 2 * y + c
        return [_remote(ins[0], outs[0].at[me], send, recv, mask - 1,
                        (x ^ (mask >> 2), y ^ ((mask >> 1) & 1), c ^ (mask & 1))) for mask in range(1, 8)]

    def start(*refs):
        for cp in copies(*refs):
            cp.start()

    def finish(ins, outs, send, recv):
        x, y, c = _place()
        me = 4 * x + 2 * y + c
        for mask in range(1, 8):
            _remote(ins[0], outs[0].at[me ^ mask], send, recv, mask - 1, (x, y, c)).wait_recv()
        for cp in copies(ins, outs, send, recv):
            cp.wait_send()

    return _Exchange([parts], [jax.ShapeDtypeStruct((8,) + parts.shape, F32)], 7, start, finish)


def _call(body, *, name, args=(), in_specs=(), out_specs=(), out_shape=(), grid=(), scratch_shapes=(),
          exchanges=()):
    n_in, n_out, n_scr = len(args), len(out_shape), len(scratch_shapes)
    ex_in, ex_out, ex_scr, spans, alias = [], [], [], [], {}
    for ex in exchanges:
        spans.append((len(ex_in), len(ex.inputs), len(ex_out), len(ex.out_shapes)))
        for i, o in ex.aliases.items():
            alias[n_in + len(ex_in) + i] = n_out + len(ex_out) + o
        ex_in += ex.inputs
        ex_out += ex.out_shapes
        ex_scr += [pltpu.SemaphoreType.DMA((ex.n_sems,)), pltpu.SemaphoreType.DMA((ex.n_sems,))]

    def full(*refs):
        ins, x_in = refs[:n_in], refs[n_in:n_in + len(ex_in)]
        outs = refs[n_in + len(ex_in):n_in + len(ex_in) + n_out]
        x_out = refs[n_in + len(ex_in) + n_out:n_in + len(ex_in) + n_out + len(ex_out)]
        scr = refs[len(refs) - n_scr - len(ex_scr):len(refs) - len(ex_scr)]
        sems = refs[len(refs) - len(ex_scr):]

        def run(which):
            for e, (ex, (i0, ni, o0, no)) in enumerate(zip(exchanges, spans)):
                getattr(ex, which)(x_in[i0:i0 + ni], x_out[o0:o0 + no], sems[2 * e], sems[2 * e + 1])

        if grid:
            ids = [pl.program_id(a) for a in range(len(grid))]
            is_first = functools.reduce(jnp.logical_and, [i == 0 for i in ids])
            is_last = functools.reduce(jnp.logical_and, [i == g - 1 for i, g in zip(ids, grid)])
            pl.when(is_first)(lambda: run("start"))
            body(*ins, *outs, *scr)
            pl.when(is_last)(lambda: run("finish"))
        else:
            run("start")
            if body is not None:
                body(*ins, *outs, *scr)
            run("finish")

    kw = dict(grid=grid) if grid else {}
    if grid:
        kw["compiler_params"] = _params(dimension_semantics=("arbitrary",) * len(grid))
    else:
        kw["compiler_params"] = _params()
    res = pl.pallas_call(
        full, name=name,
        in_specs=list(in_specs) + [HBM] * len(ex_in),
        out_specs=list(out_specs) + [HBM] * len(ex_out),
        out_shape=list(out_shape) + ex_out,
        scratch_shapes=list(scratch_shapes) + ex_scr,
        input_output_aliases=alias, **kw,
    )(*args, *ex_in)
    own = list(res[:n_out])
    per_ex = [list(res[n_out + o0:n_out + o0 + no]) for (_, _, o0, no) in spans]
    return own, per_ex


def _cast_w_in(w, place):
    rows, cols = w.shape
    tile = rows

    def body(place_ref, w_ref, o_ref):
        o_ref[...] = w_ref[...].astype(BF16)

    return pl.pallas_call(
        body, name="cast_w_in",
        grid_spec=pltpu.PrefetchScalarGridSpec(
            num_scalar_prefetch=1, grid=(cols // COL_BLK, rows // tile),
            in_specs=[pl.BlockSpec((tile, COL_BLK), lambda b, i, p: (i, b))],
            out_specs=pl.BlockSpec((None, None, tile, COL_BLK), lambda b, i, p: (p[1], b, i, 0))),
        out_shape=jax.ShapeDtypeStruct((N_SHARDS, cols // COL_BLK, rows, COL_BLK), BF16),
        compiler_params=_params(dimension_semantics=("arbitrary", "arbitrary")),
    )(place, w)


def _rec_fwd(proj, lb_logits, rec_g, consts, exchanges):
    tri, low, sign = consts["tri"], consts["low"], consts["sign"]

    def body(q_ref, f_ref, i_ref, rg_ref, lb_ref, g_ref, w_ref, low_ref, sign_ref, y_ref, o_ref, stp_ref, st_ref):
        @pl.when(pl.program_id(1) == 0)
        def _():
            st_ref[...] = jnp.zeros_like(st_ref)
        lb = _lower_bound(lb_ref)
        st = st_ref[...]
        rows = lambda c: pl.ds(c * CHUNK, CHUNK)
        for c0 in range(0, REC_CHUNKS, REC_GROUP):
            group = range(c0, c0 + REC_GROUP)
            gated = [_gates(q_ref[rows(c), :], f_ref[rows(c), :], lb) for c in group]
            g2s = [_dot3(w_ref[...], g) * LOG2E for (_, _, _, _, g) in gated]
            xs = [[xl for xl, _, _, _ in _level_factors(g2, qs, k, sign_ref)]
                  for g2, (qs, _, _, k, _) in zip(g2s, gated)]
            a_s = []
            for x in xs:
                a = jnp.zeros((CHUNK, CHUNK), F32)
                for l, xl in enumerate(x):
                    a = a + _dot_nt(xl, xl) * low_ref[l]
                a_s.append(a.astype(BF16))
            vbs = [i_ref[rows(c), :].astype(BF16) for c in group]
            intra = [_dot(a, vb) for a, vb in zip(a_s, vbs)]
            kvs = [_dot_tn(vb, (k * jnp.exp2(g2[CHUNK - 1:CHUNK, :] - g2)).astype(BF16))
                   for vb, g2, (_, _, _, k, _) in zip(vbs, g2s, gated)]
            for i, c in enumerate(group):
                qs, _, _, k, _ = gated[i]
                g2 = g2s[i]
                stp_ref[c] = st
                v = i_ref[rows(c), :]
                rg = rg_ref[rows(c), :]
                o = (intra[i] + jnp.sum(qs * k, axis=-1, keepdims=True) * v
                     + _dot_nt((qs * jnp.exp2(g2)).astype(BF16), st.astype(BF16)))
                st = st * jnp.exp2(g2[CHUNK - 1:CHUNK, :]) + kvs[i]
                o_ref[rows(c), :] = o
                inv = lax.rsqrt(jnp.mean(o * o, axis=-1, keepdims=True) + EPS)
                y_ref[rows(c), :] = (o * inv * g_ref[...] * (rg * _sig(rg))).astype(BF16)
        st_ref[...] = st

    sec = lambda n: _proj_cols(HEAD, n, lambda h, b: (h, b), REC_ROWS)
    vec = lambda rows: pl.BlockSpec((rows, HEAD), lambda h, b: (0, h))
    full = lambda a: pl.BlockSpec(a.shape, lambda h, b: (0,) * a.ndim)
    return _call(
        body, name="rec_fwd", grid=(N_HEADS, N_REC_BLK),
        args=(proj, proj, proj, proj, lb_logits, rec_g, tri, low, sign),
        in_specs=[sec(2), sec(3), sec(4), sec(5), vec(2), vec(1), full(tri), full(low), full(sign)],
        out_specs=[pl.BlockSpec((REC_ROWS, HEAD), lambda h, b: (b, h)),
                   pl.BlockSpec((REC_ROWS, HEAD), lambda h, b: (b, h)),
                   pl.BlockSpec((None, REC_CHUNKS, HEAD, HEAD), lambda h, b: (h, b, 0, 0))],
        out_shape=[jax.ShapeDtypeStruct((SEQ, D_MODEL), BF16),
                   jax.ShapeDtypeStruct((SEQ, D_MODEL), F32),
                   jax.ShapeDtypeStruct((N_HEADS, SEQ // CHUNK, HEAD, HEAD), F32)],
        scratch_shapes=[pltpu.VMEM((HEAD, HEAD), F32)],
        exchanges=exchanges)


OUT_ROWS = 512


def _out_proj_loss(y_pool, y_rec, w_out_g, x, target, gf):
    def body(yp_ref, yr_ref, w_ref, x_ref, t_ref, gf_ref, dout_ref, doutb_ref, part_ref):
        @pl.when(pl.program_id(0) == 0)
        def _():
            part_ref[...] = jnp.zeros_like(part_ref)
        halves = [pl.ds(a * (OUT_ROWS // 2), OUT_ROWS // 2) for a in range(2)]
        outs = [x_ref[r, :] + _dot(yp_ref[r, :], w_ref[0:D_MODEL, :])
                + _dot(yr_ref[r, :], w_ref[D_MODEL:2 * D_MODEL, :]) for r in halves]
        gf_v = gf_ref[...]
        for r, out in zip(halves, outs):
            inv = lax.rsqrt(jnp.mean(out * out, axis=-1, keepdims=True) + EPS)
            diff = out * inv * gf_v - t_ref[r, :]
            dyf = diff * (1.0 / D_MODEL)
            a = dyf * gf_v
            dout = inv * a - out * (inv * inv * inv) * jnp.mean(a * out, axis=-1, keepdims=True)
            dout_ref[r, :] = dout
            doutb_ref[r, :] = dout.astype(BF16)
            part_ref[0:1, :] += jnp.sum(dyf * out * inv, axis=0, keepdims=True)
            part_ref[1:2, :] += jnp.sum(diff * diff, axis=0, keepdims=True)

    row = lambda n: pl.BlockSpec((OUT_ROWS, n), lambda i: (i, 0))
    return pl.pallas_call(
        body, name="out_proj_loss", grid=(SEQ // OUT_ROWS,),
        in_specs=[row(D_MODEL), row(D_MODEL), pl.BlockSpec((2 * D_MODEL, D_MODEL), lambda i: (0, 0)),
                  row(D_MODEL), row(D_MODEL), pl.BlockSpec((1, D_MODEL), lambda i: (0, 0))],
        out_specs=[row(D_MODEL), row(D_MODEL), pl.BlockSpec((8, D_MODEL), lambda i: (0, 0))],
        out_shape=[jax.ShapeDtypeStruct((SEQ, D_MODEL), F32),
                   jax.ShapeDtypeStruct((SEQ, D_MODEL), BF16),
                   jax.ShapeDtypeStruct((8, D_MODEL), F32)],
        compiler_params=_params(dimension_semantics=("arbitrary",)),
    )(y_pool, y_rec, w_out_g, x, target, gf)


def _grad_w_out(y_pool, y_rec, dout_b):
    blk = W_OUT_SHARD // 2
    per = D_MODEL // blk
    n = 2 * per

    def body(yp_ref, yr_ref, d_ref, p32_ref, p16_ref, send_ref, recv_ref, send_sems, recv_sems):
        j = pl.program_id(0)
        x, y, c = _place()

        def copy(u):
            return _remote(send_ref.at[u], recv_ref.at[u], send_sems, recv_sems, u, (x, y, 1 - c))

        pl.when(j == 0)(_other_core_barrier)
        for i in range(n):
            @pl.when(j == i)
            def _(i=i):
                res = _dot_tn((yp_ref if i < per else yr_ref)[...], d_ref[...])

                @pl.when(i % 2 == c)
                def _():
                    p32_ref[i // 2] = res

                @pl.when(i % 2 != c)
                def _():
                    send_ref[i // 2] = res
                    copy(i // 2).start()

        @pl.when(j == n - 1)
        def _():
            for u in range(N_SHARDS):
                copy(u).wait_recv()
                tot = p32_ref[u] + recv_ref[u]
                p32_ref[u] = tot
                p16_ref[u] = tot.astype(BF16)
            for u in range(N_SHARDS):
                copy(u).wait_send()

    whole = pl.BlockSpec((N_SHARDS, blk, D_MODEL), lambda j: (0, 0, 0))
    return pl.pallas_call(
        body, name="grad_w_out", grid=(n,),
        in_specs=[pl.BlockSpec((SEQ, blk), lambda j: (0, jnp.minimum(j, per - 1))),
                  pl.BlockSpec((SEQ, blk), lambda j: (0, jnp.maximum(j - per, 0))),
                  pl.BlockSpec((SEQ, D_MODEL), lambda j: (0, 0))],
        out_specs=[whole, whole],
        out_shape=[jax.ShapeDtypeStruct((N_SHARDS, blk, D_MODEL), F32),
                   jax.ShapeDtypeStruct((N_SHARDS, blk, D_MODEL), BF16)],
        scratch_shapes=[pltpu.VMEM((N_SHARDS, blk, D_MODEL), F32), pltpu.VMEM((N_SHARDS, blk, D_MODEL), F32),
                        pltpu.SemaphoreType.DMA((N_SHARDS,)), pltpu.SemaphoreType.DMA((N_SHARDS,))],
        compiler_params=_params(dimension_semantics=("arbitrary",), collective_id=0),
    )(y_pool, y_rec, dout_b)


def _pool_bwd(proj, dout_b, w_out_g, pw_g, pool_scale, exchanges):
    n = POOL_ROWS + POOL_HALO
    per_half = N_GROUPS // 2

    def body(u_ref, gate_ref, d_ref, wo_ref, pw_ref, sc_ref,
             dp_ref, p32_ref, p16_ref, dsc_ref, dd_ref, ddw_ref, dpw_ref, keep_ref, send_ref, recv_ref,
             send_sems, recv_sems):
        g = pl.program_id(0)
        x_, y_, c = _place()

        def to_other_core(a):
            return _remote(send_ref.at[a], recv_ref.at[a], send_sems, recv_sems, a, (x_, y_, 1 - c))

        dpw_ref[...] = jnp.zeros_like(dpw_ref)
        dsc_ref[...] = jnp.zeros_like(dsc_ref)

        def first(ii, _):
            chunks = [POOL_PAIR * ii + a for a in range(POOL_PAIR)]
            rs = [pl.ds(pl.multiple_of(i * POOL_ROWS, POOL_ROWS), POOL_ROWS) for i in chunks]
            diffs = [_pool_diff(u_ref, i, g) for i in chunks]
            dbs = [d.astype(BF16) for d, _ in diffs]
            mixed = [_dot(db, pw_ref[...]) for db in dbs]
            dys = [_dot_nt(d_ref[r, :], wo_ref[...]) for r in rs]
            sc = sc_ref[...]
            dmbs = []
            for r, m, dy in zip(rs, mixed, dys):
                gate = gate_ref[r, :]
                sg = _sig(gate)
                silu = gate * sg
                dp_ref[1, r, :] = (dy * m * sc * (sg * (1.0 + gate * (1.0 - sg)))).astype(BF16)
                dsc_ref[...] += jnp.sum(dy * silu * m, axis=0, keepdims=True)
                dmbs.append((dy * silu * sc).astype(BF16))
            for db, dmb in zip(dbs, dmbs):
                dpw_ref[...] += _dot_tn(db, dmb)
            dds = [_dot_nt(dmb, pw_ref[...]) for dmb in dmbs]
            for r, dd, (_, inv_count) in zip(rs, dds, diffs):
                dd_ref[r, :] = dd
                ddw_ref[r, :] = dd * inv_count
            return 0
        lax.fori_loop(0, SEQ // POOL_ROWS // POOL_PAIR, first, 0)

        def second(i, _):
            r0 = i * POOL_ROWS
            r = pl.ds(pl.multiple_of(r0, POOL_ROWS), POOL_ROWS)
            last = i == SEQ // POOL_ROWS - 1
            after = ddw_ref[pl.ds(pl.multiple_of(jnp.minimum(r0 + POOL_ROWS, SEQ - POOL_HALO), 8), POOL_HALO), :]
            after = jnp.where(last, 0.0, after)
            ext = jnp.concatenate([ddw_ref[r, :], after], axis=0)
            s = _window_sums(ext, g, lambda k: n - (1 << k))[:POOL_ROWS, :]
            dp_ref[0, r, :] = (s - dd_ref[r, :]).astype(BF16)
            return 0
        lax.fori_loop(0, SEQ // POOL_ROWS, second, 0)

        for a in range(per_half):
            @pl.when((g % per_half == a) & (g // per_half == c))
            def _(a=a):
                keep_ref[a] = dpw_ref[...]

            @pl.when((g % per_half == a) & (g // per_half != c))
            def _(a=a):
                send_ref[a] = dpw_ref[...]
                to_other_core(a).start()

        @pl.when(g == N_GROUPS - 1)
        def _():
            for a in range(per_half):
                to_other_core(a).wait_recv()
                tot = keep_ref[a] + recv_ref[a]
                for s in range(N_SHARDS):
                    rows_s = tot[s * PW_SHARD:(s + 1) * PW_SHARD, :]
                    p32_ref[s, a * PW_SHARD:(a + 1) * PW_SHARD, :] = rows_s
                    p16_ref[s, a * PW_SHARD:(a + 1) * PW_SHARD, :] = rows_s.astype(BF16)
            for a in range(per_half):
                to_other_core(a).wait_send()

    pw_half = (N_SHARDS, per_half * PW_SHARD, PG)
    return _call(
        body, name="pool_bwd", grid=(N_GROUPS,),
        args=(proj, proj, dout_b, w_out_g, pw_g, pool_scale),
        in_specs=[_proj_cols(PG, 0, lambda g: (g, 0)), _proj_cols(PG, 1, lambda g: (g, 0)),
                  pl.BlockSpec((SEQ, D_MODEL), lambda g: (0, 0)),
                  pl.BlockSpec((PG, D_MODEL), lambda g: (g, 0)),
                  pl.BlockSpec((None, PG, PG), lambda g: (g, 0, 0)),
                  pl.BlockSpec((1, PG), lambda g: (0, g))],
        out_specs=[pl.BlockSpec((2, SEQ, PG), lambda g: (0, 0, g)),
                   pl.BlockSpec(pw_half, lambda g: (0, 0, 0)),
                   pl.BlockSpec(pw_half, lambda g: (0, 0, 0)),
                   pl.BlockSpec((1, PG), lambda g: (0, g))],
        out_shape=[jax.ShapeDtypeStruct((2, SEQ, D_MODEL), BF16),
                   jax.ShapeDtypeStruct(pw_half, F32),
                   jax.ShapeDtypeStruct(pw_half, BF16),
                   jax.ShapeDtypeStruct((1, D_MODEL), F32)],
        scratch_shapes=[pltpu.VMEM((SEQ, PG), F32), pltpu.VMEM((SEQ, PG), F32), pltpu.VMEM((PG, PG), F32)]
                       + [pltpu.VMEM((per_half, PG, PG), F32)] * 3
                       + [pltpu.SemaphoreType.DMA((per_half,)), pltpu.SemaphoreType.DMA((per_half,))],
        exchanges=exchanges)


HALF_HEADS = N_HEADS // 2
HALF_COLS = HALF_HEADS * HEAD


def _rec_bwd(proj, o_raw, st_prev, dout_b, w_out_g, lb_logits, rec_g, consts, h0, name, exchanges,
             gate_of=None, own_gate=True):
    n_sec = 4 if gate_of is None else 5

    def body(q_ref, f_ref, i_ref, rg_ref, o_ref, stp_ref, d_ref, wo_ref, lb_ref, g_ref,
             w_ref, lowt_ref, sym_ref, sign_ref, tri_ref, *rest):
        dr_ref, part_ref, dst_ref = rest[-3:]

        @pl.when(pl.program_id(1) == 0)
        def _():
            dst_ref[...] = jnp.zeros_like(dst_ref)
            part_ref[...] = jnp.zeros_like(part_ref)
        if gate_of is not None:
            rg2_ref, o2_ref, wo2_ref, g2_ref = rest[:4]
            rg, o = rg2_ref[...], o2_ref[...]
            sg = _sig(rg)
            inv = lax.rsqrt(jnp.mean(o * o, axis=-1, keepdims=True) + EPS)
            dy = _dot_nt(d_ref[...], wo2_ref[...])
            dr_ref[4] = (dy * (o * inv) * g2_ref[...] * (sg * (1.0 + rg * (1.0 - sg)))).astype(BF16)
        tril = (lax.broadcasted_iota(jnp.int32, (CHUNK, CHUNK), 0)
                > lax.broadcasted_iota(jnp.int32, (CHUNK, CHUNK), 1))
        lb = _lower_bound(lb_ref)
        grec = g_ref[...]
        dst = dst_ref[...]
        acc_grec = jnp.zeros((1, HEAD), F32)
        acc_lb = jnp.zeros((1, HEAD), F32)
        rows = lambda c: pl.ds(c * CHUNK, CHUNK)
        for c0 in reversed(range(0, REC_CHUNKS, REC_GROUP_BWD)):
            group = list(reversed(range(c0, c0 + REC_GROUP_BWD)))
            dys = [_dot_nt(d_ref[rows(c), :], wo_ref[...]) for c in group]
            dos = []
            for c, dy in zip(group, dys):
                rg = rg_ref[rows(c), :]
                o = o_ref[rows(c), :]
                sg = _sig(rg)
                silu = rg * sg
                inv = lax.rsqrt(jnp.mean(o * o, axis=-1, keepdims=True) + EPS)
                recn = o * inv
                if own_gate:
                    dr_ref[3, rows(c), :] = (dy * recn * grec * (sg * (1.0 + rg * (1.0 - sg)))).astype(BF16)
                acc_grec = acc_grec + jnp.sum(dy * silu * recn, axis=0, keepdims=True)
                drecn = dy * silu * grec
                dos.append(inv * drecn - o * (inv * inv * inv) * jnp.mean(drecn * o, axis=-1, keepdims=True))
            gated = [_gates(q_ref[rows(c), :], f_ref[rows(c), :], lb) for c in group]
            g2s = [_dot3(w_ref[...], g) * LOG2E for (_, _, _, _, g) in gated]
            levels = [_level_factors(g2, qs, k, sign_ref) for g2, (qs, _, _, k, _) in zip(g2s, gated)]
            a_ts = []
            for lev in levels:
                a_t = jnp.zeros((CHUNK, CHUNK), F32)
                for l, (xl, _, _, _) in enumerate(lev):
                    a_t = a_t + _dot_nt(xl, xl) * lowt_ref[l]
                a_ts.append(a_t.astype(BF16))
            dobs = [do.astype(BF16) for do in dos]
            vbs = [i_ref[rows(c), :].astype(BF16) for c in group]
            d_syms = [jnp.where(tril, _dot_nt(dob, vb), _dot_nt(vb, dob)) for dob, vb in zip(dobs, vbs)]
            dqs_is, dk_is = [], []
            for lev, d_sym in zip(levels, d_syms):
                dqs_i = jnp.zeros((CHUNK, HEAD), F32)
                both_i = jnp.zeros((CHUNK, HEAD), F32)
                for l, (xl, xlo, e, up) in enumerate(lev):
                    z = d_sym * sym_ref[l]
                    tmp = _dot(z.astype(BF16), jnp.concatenate([xl, xlo], axis=-1))
                    tmp = (tmp[:, :HEAD] + tmp[:, HEAD:]) * e
                    dqs_i = dqs_i + jnp.where(up, tmp, 0.0)
                    both_i = both_i + tmp
                dqs_is.append(dqs_i)
                dk_is.append(both_i - dqs_i)
            e_gs = [jnp.exp2(g2) for g2 in g2s]
            e_revs = [jnp.exp2(g2[CHUNK - 1:CHUNK, :] - g2) for g2 in g2s]
            e_lasts = [jnp.exp2(g2[CHUNK - 1:CHUNK, :]) for g2 in g2s]
            q_gs = [qs * e_g for (qs, _, _, _, _), e_g in zip(gated, e_gs)]
            kdecs = [k * e_rev for (_, _, _, k, _), e_rev in zip(gated, e_revs)]
            dv12 = [_dot(a_t, dob) + jnp.sum(qs * k, axis=-1, keepdims=True) * do
                    for a_t, dob, do, (qs, _, _, k, _) in zip(a_ts, dobs, dos, gated)]
            dq_gs = [_dot(dob, stp_ref[c].astype(BF16)) for c, dob in zip(group, dobs)]
            steps = [_dot_tn(dob, q_g.astype(BF16)) for dob, q_g in zip(dobs, q_gs)]
            dsts = []
            for e_last, step in zip(e_lasts, steps):
                dsts.append(dst)
                dst = dst * e_last + step
            dstbs = [d.astype(BF16) for d in dsts]
            dv3 = [_dot_nt(kdec.astype(BF16), dstb) for kdec, dstb in zip(kdecs, dstbs)]
            dkdecs = [_dot(vb, dstb) for vb, dstb in zip(vbs, dstbs)]
            dbig_gs, dg_lasts, dqss, dks = [], [], [], []
            for i, c in enumerate(group):
                qs, _, _, k, _ = gated[i]
                de_last = jnp.sum(stp_ref[c] * dsts[i], axis=0, keepdims=True)
                ddiag = jnp.sum(dos[i] * i_ref[rows(c), :], axis=-1, keepdims=True)
                dqss.append(dqs_is[i] + ddiag * k + dq_gs[i] * e_gs[i])
                dks.append(dk_is[i] + ddiag * qs + dkdecs[i] * e_revs[i])
                dg_rev = dkdecs[i] * kdecs[i]
                dg_lasts.append(jnp.sum(dg_rev, axis=0, keepdims=True) + de_last * e_lasts[i])
                dbig_gs.append(qs * dqs_is[i] - k * dk_is[i] + dq_gs[i] * q_gs[i] - dg_rev)
            dgs = [_dot3(tri_ref[...], dbig_g) + dg_last for dbig_g, dg_last in zip(dbig_gs, dg_lasts)]
            for i, c in enumerate(group):
                _, sf, f, _, _ = gated[i]
                q = q_ref[rows(c), :]
                df = dgs[i] / f - dks[i]
                dr_ref[1, rows(c), :] = (df * (1.0 - lb) * sf * (1.0 - sf)).astype(BF16)
                acc_lb = acc_lb + jnp.sum(df * (1.0 - sf), axis=0, keepdims=True)
                sq = _sig(q)
                dr_ref[0, rows(c), :] = (dqss[i] * (sq * (1.0 + q * (1.0 - sq)))).astype(BF16)
                dr_ref[2, rows(c), :] = (dv12[i] + dv3[i]).astype(BF16)
        dst_ref[...] = dst
        part_ref[0:1, :] += acc_grec
        part_ref[1:2, :] += acc_lb

    rev = lambda b: N_REC_BLK - 1 - b
    sec = lambda n: _proj_cols(HEAD, n, lambda h, b: (h0 + h, rev(b)), REC_ROWS)
    col_in = pl.BlockSpec((REC_ROWS, HEAD), lambda h, b: (rev(b), h0 + h))
    vec_in = lambda rows: pl.BlockSpec((rows, HEAD), lambda h, b: (0, h0 + h))
    full = lambda a: pl.BlockSpec(a.shape, lambda h, b: (0,) * a.ndim)
    extra_args, extra_specs = (), []
    if gate_of is not None:
        extra_args = (proj, o_raw, w_out_g, rec_g)
        extra_specs = [_proj_cols(HEAD, 5, lambda h, b: (gate_of + h, rev(b)), REC_ROWS),
                       pl.BlockSpec((REC_ROWS, HEAD), lambda h, b: (rev(b), gate_of + h)),
                       pl.BlockSpec((HEAD, D_MODEL), lambda h, b: (SEC_BLK + gate_of + h, 0)),
                       pl.BlockSpec((1, HEAD), lambda h, b: (0, gate_of + h))]
    return _call(
        body, name=name, grid=(HALF_HEADS, N_REC_BLK),
        args=(proj, proj, proj, proj, o_raw, st_prev, dout_b, w_out_g, lb_logits, rec_g,
              consts["tri"], consts["low_t"], consts["sym"], consts["sign"], consts["tri_t"]) + extra_args,
        in_specs=[sec(2), sec(3), sec(4), sec(5), col_in,
                  pl.BlockSpec((None, REC_CHUNKS, HEAD, HEAD), lambda h, b: (h0 + h, rev(b), 0, 0)),
                  pl.BlockSpec((REC_ROWS, D_MODEL), lambda h, b: (rev(b), 0)),
                  pl.BlockSpec((HEAD, D_MODEL), lambda h, b: (SEC_BLK + h0 + h, 0)),
                  vec_in(2), vec_in(1)] + [full(consts[n]) for n in ("tri", "low_t", "sym", "sign", "tri_t")]
                 + extra_specs,
        out_specs=[pl.BlockSpec((n_sec, REC_ROWS, HEAD), lambda h, b: (0, rev(b), h)),
                   pl.BlockSpec((8, HEAD), lambda h, b: (0, h))],
        out_shape=[jax.ShapeDtypeStruct((n_sec, SEQ, HALF_COLS), BF16),
                   jax.ShapeDtypeStruct((8, HALF_COLS), F32)],
        scratch_shapes=[pltpu.VMEM((HEAD, HEAD), F32)],
        exchanges=exchanges)


def _w_in_block(w_ref, j):
    per_shard = W_IN_SHARD // COL_BLK
    return w_ref[j // per_shard, j % per_shard]


def _grad_x(dproj, w_in_g, x, g1, dout, exchanges):
    rows = 512
    n_blk = len(dproj)

    def body(*refs):
        dp_refs = refs[:n_blk]
        w_ref, x_ref, g_ref, dout_ref, dx_ref, part_ref = refs[n_blk:]

        @pl.when(pl.program_id(0) == 0)
        def _():
            part_ref[...] = jnp.zeros_like(part_ref)
        dh = jnp.zeros((rows, D_MODEL), F32)
        for j in range(n_blk):
            dh = dh + _dot_nt(dp_refs[j][...], _w_in_block(w_ref, j))
        xv = x_ref[...]
        inv = lax.rsqrt(jnp.mean(xv * xv, axis=-1, keepdims=True) + EPS)
        a = dh * g_ref[...]
        dx_ref[...] = (dout_ref[...] + inv * a
                       - xv * (inv * inv * inv) * jnp.mean(a * xv, axis=-1, keepdims=True))
        part_ref[0:1, :] += jnp.sum(dh * xv * inv, axis=0, keepdims=True)

    row = lambda: pl.BlockSpec((rows, D_MODEL), lambda i: (i, 0))
    dp_spec = lambda sec, cb: pl.BlockSpec((None, rows, COL_BLK), lambda i: (sec, i, cb))
    return _call(
        body, name="grad_x", grid=(SEQ // rows,),
        args=tuple(a for a, _, _ in dproj) + (w_in_g, x, g1, dout),
        in_specs=[dp_spec(sec, cb) for _, sec, cb in dproj]
                 + [pl.BlockSpec(w_in_g.shape, lambda i: (0, 0, 0, 0)),
                    row(), pl.BlockSpec((1, D_MODEL), lambda i: (0, 0)), row()],
        out_specs=[row(), pl.BlockSpec((8, D_MODEL), lambda i: (0, 0))],
        out_shape=[jax.ShapeDtypeStruct((SEQ, D_MODEL), F32),
                   jax.ShapeDtypeStruct((8, D_MODEL), F32)],
        exchanges=exchanges)


def _grad_w_in(h, dp, blocks, name, collective_id):
    n_blk = len(blocks)
    half = D_MODEL // 2
    pick = lambda vals: (lambda j: functools.reduce(lambda acc, iv: jnp.where(j == iv[0], iv[1], acc),
                                                     list(enumerate(vals))[1:], vals[0]))
    sec_of = pick([sec for sec, _ in blocks])
    cb_of = pick([cb for _, cb in blocks])

    def body(h_ref, dp_ref, p32_ref, p16_ref, keep_ref, send_ref, recv_ref, send_sems, recv_sems):
        j = pl.program_id(0)
        x, y, c = _place()
        cols = lambda cc: pl.ds(pl.multiple_of(cc * half, half), half)

        def copy(i):
            return _remote(send_ref.at[i], recv_ref.at[i], send_sems, recv_sems, i, (x, y, 1 - c))

        pl.when(j == 0)(_other_core_barrier)
        for i in range(n_blk + 1):
            @pl.when(j == i)
            def _(i=i):
                if i < n_blk:
                    send_ref[i] = _dot_tn(h_ref[:, cols(1 - c)], dp_ref[...])
                    copy(i).start()
                    keep_ref[i] = _dot_tn(h_ref[:, cols(c)], dp_ref[...])
                if i > 0:
                    copy(i - 1).wait_recv()
                    tot = keep_ref[i - 1] + recv_ref[i - 1]
                    p32_ref[...] = tot
                    p16_ref[...] = tot.astype(BF16)

        @pl.when(j == n_blk)
        def _():
            for i in range(n_blk):
                copy(i).wait_send()

    lagged = pl.BlockSpec((None, half, COL_BLK), lambda j: (jnp.maximum(j - 1, 0), 0, 0))
    last = n_blk - 1
    return pl.pallas_call(
        body, name=name, grid=(n_blk + 1,),
        in_specs=[pl.BlockSpec((SEQ, D_MODEL), lambda j: (0, 0)),
                  pl.BlockSpec((None, SEQ, COL_BLK),
                               lambda j: (sec_of(jnp.minimum(j, last)), 0, cb_of(jnp.minimum(j, last))))],
        out_specs=[lagged, lagged],
        out_shape=[jax.ShapeDtypeStruct((n_blk, half, COL_BLK), F32),
                   jax.ShapeDtypeStruct((n_blk, half, COL_BLK), BF16)],
        scratch_shapes=[pltpu.VMEM((n_blk, half, COL_BLK), F32)] * 3
                       + [pltpu.SemaphoreType.DMA((n_blk,)), pltpu.SemaphoreType.DMA((n_blk,))],
        compiler_params=_params(dimension_semantics=("arbitrary",), collective_id=collective_id),
    )(h, dp)


def _sum_units(part32, recv16, place, owners, tile, name):
    n, half, cols = part32.shape
    per_half = half // tile
    table = np.array([[sum(o == chip for o in owners)] + sorted(range(n), key=lambda j: (owners[j] != chip, j))
                      for chip in range(N_SHARDS)], np.int32)
    sched = jnp.concatenate([place[:1], jnp.asarray(table)[place[1]]])

    def block(k, i, p):
        live = k < p[1]
        unit = p[2 + jnp.minimum(k, jnp.maximum(p[1] - 1, 0))]
        return unit, jnp.where(live, i, per_half - 1)

    def body(sched_ref, p_ref, r_ref, o_ref):
        @pl.when(pl.program_id(0) < sched_ref[1])
        def _():
            acc = p_ref[...]
            for j in range(len(CHIP_FLIPS)):
                acc = acc + r_ref[j].astype(F32)
            o_ref[...] = acc

    return pl.pallas_call(
        body, name=name,
        grid_spec=pltpu.PrefetchScalarGridSpec(
            num_scalar_prefetch=1, grid=(n, per_half),
            in_specs=[pl.BlockSpec((None, tile, cols), lambda k, i, p: (*block(k, i, p), 0)),
                      pl.BlockSpec((None, len(CHIP_FLIPS), tile, cols),
                                   lambda k, i, p: (block(k, i, p)[0], 0, block(k, i, p)[1], 0))],
            out_specs=pl.BlockSpec((None, tile, cols),
                                   lambda k, i, p: (block(k, i, p)[0], p[0] * per_half + block(k, i, p)[1], 0))),
        out_shape=jax.ShapeDtypeStruct((n, 2 * half, cols), F32),
        compiler_params=_params(dimension_semantics=("arbitrary", "arbitrary")),
    )(sched, part32, recv16)


def _adamw_math(w, g, m, v):
    m = ADAM_B1 * m + (1.0 - ADAM_B1) * g
    v = ADAM_B2 * v + (1.0 - ADAM_B2) * (g * g)
    m_hat = m / (1.0 - ADAM_B1 ** ADAM_STEP)
    v_hat = v / (1.0 - ADAM_B2 ** ADAM_STEP)
    delta = -ADAM_LR * (m_hat / (jnp.sqrt(v_hat) + ADAM_EPS) + ADAM_WD * w)
    return delta, m, v


def _adamw_units(w, m, v, grads, pick, name):
    rows, cols = w.shape
    bc = grads[0].shape[-1]
    tile = min(rows, 512)
    n_g = len(grads)

    def body(pick_ref, w_ref, m_ref, v_ref, *refs):
        g_refs, (g_out, d_ref, nm_ref, nv_ref) = refs[:n_g], refs[n_g:]
        p = pl.program_id(0)
        for a in range(n_g):
            @pl.when(pick_ref[0, p] == a)
            def _(a=a):
                g = g_refs[a][...]
                g_out[...] = g
                d_ref[...], nm_ref[...], nv_ref[...] = _adamw_math(w_ref[...], g, m_ref[...], v_ref[...])

    blk = pl.BlockSpec((tile, bc), lambda p, i, pick: (i, p))

    def g_spec(a):
        return pl.BlockSpec((None, tile, bc),
                            lambda p, i, pick: (jnp.where(pick[0, p] == a, pick[1, p], 0),
                                                jnp.where(pick[0, p] == a, i, 0), 0))

    return pl.pallas_call(
        body, name=name,
        grid_spec=pltpu.PrefetchScalarGridSpec(
            num_scalar_prefetch=1, grid=(cols // bc, rows // tile),
            in_specs=[blk] * 3 + [g_spec(a) for a in range(n_g)],
            out_specs=[blk] * 4),
        out_shape=[jax.ShapeDtypeStruct(w.shape, F32)] * 4,
        compiler_params=_params(dimension_semantics=("arbitrary", "arbitrary")),
    )(pick, w, m, v, *grads)


ROW_NORM1, ROW_SCALE, ROW_LB, ROW_REC, ROW_FINAL, ROW_LOSS = 0, 1, 2, 4, 5, 6


SMALL_ROWS = (ROW_NORM1, ROW_SCALE, ROW_LB, ROW_REC, ROW_FINAL)


def _small_update(parts, gathered, params):
    n_p = len(params)

    def body(own_ref, p_ref, *refs):
        ins, loss_ref, outs = refs[:3 * n_p], refs[3 * n_p], refs[3 * n_p + 1:]
        x, y, c = _place()
        me = 4 * x + 2 * y + c
        slot = lambda d: jnp.where(me == d, own_ref[...], p_ref[d])
        tot = slot(0)
        for d in range(1, 8):
            tot = tot + slot(d)
        for i, r in enumerate(SMALL_ROWS):
            w = ins[3 * i][...]
            g = tot[r:r + 1, :]
            if r == ROW_LB:
                mx = jnp.maximum(w[0:1, :], w[1:2, :])
                e0 = jnp.exp(w[0:1, :] - mx)
                e1 = jnp.exp(w[1:2, :] - mx)
                lb = e0 / (e0 + e1)
                g = g * lb * (1.0 - lb)
                g = jnp.concatenate([g, -g], axis=0)
            outs[4 * i][...] = g
            outs[4 * i + 1][...], outs[4 * i + 2][...], outs[4 * i + 3][...] = _adamw_math(
                w, g, ins[3 * i + 1][...], ins[3 * i + 2][...])
        loss_ref[...] = (0.5 / D_MODEL) * jnp.sum(tot[ROW_LOSS:ROW_LOSS + 1, :], axis=-1, keepdims=True)

    flat = [a for wmv in params for a in wmv]
    return pl.pallas_call(
        body, name="small_update",
        out_shape=[jax.ShapeDtypeStruct((1, 1), F32)]
                  + [jax.ShapeDtypeStruct(w.shape, F32) for w, _, _ in params for _ in range(4)],
        compiler_params=_params(),
    )(parts, gathered, *flat)


SHARD_OWNERS = tuple(range(N_SHARDS))
BLOCKS_POOL = (0, 1, 2, 3)
BLOCKS_A = (4, 6, 8, 10, 11)
BLOCKS_B = (5, 7, 9)
BLOCK_GROUPS = (BLOCKS_POOL, BLOCKS_A, BLOCKS_B)


def _block_owners(blocks):
    return tuple(j // (W_IN_SHARD // COL_BLK) for j in blocks)


def kernel(x, norm1_g, w_in, pool_w, pool_scale, lb_logits, rec_norm_g, w_out, final_norm_g, loss_target, m_norm1_g, m_w_in, m_pool_w, m_pool_scale, m_lb_logits, m_rec_norm_g, m_w_out, m_final_norm_g, v_norm1_g, v_w_in, v_pool_w, v_pool_scale, v_lb_logits, v_rec_norm_g, v_w_out, v_final_norm_g):
    xi, yi, ci = _place()
    chip = 2 * xi + yi
    place = jnp.stack([ci, chip]).astype(jnp.int32)
    pw_rows = N_GROUPS * PW_SHARD
    flat_pw = lambda a: a.reshape(pw_rows, PG)
    x2, target, gf = x[0], loss_target[0], final_norm_g.reshape(1, D_MODEL)
    consts = {n: jnp.asarray(a, BF16 if n.startswith("tri") else F32) for n, a in _chunk_constants().items()}

    proj, h, w_in_g, w_out_slots, pw_slots = _in_proj(x2, norm1_g, _cast_w_in(w_in[0], place), place,
                                                      [w_out[0], flat_pw(pool_w)])
    (y_rec, o_raw, st_prev), ((w_out_g, pw_g),) = _rec_fwd(
        proj, lb_logits, rec_norm_g, consts, [_ex_gather([w_out_slots, pw_slots])])
    w_out_g = w_out_g.reshape(2 * D_MODEL, D_MODEL)
    pw_full = pw_g.reshape(N_SHARDS, N_GROUPS, PW_SHARD, PG).transpose(1, 0, 2, 3).reshape(N_GROUPS, PG, PG)
    y_pool = _pool_fwd(proj, pw_full, pool_scale)
    dout, dout_b, part_out = _out_proj_loss(y_pool, y_rec, w_out_g, x2, target, gf)

    p_out32, p_out16 = _grad_w_out(y_pool, y_rec, dout_b)
    (dpool, p_pw32, p_pw16, dscale), ((rb_out,),) = _pool_bwd(proj, dout_b, w_out_g, pw_full, pool_scale,
                                                              [_ex_send([p_out16], [SHARD_OWNERS])])
    g_out = _sum_units(p_out32, rb_out, place, SHARD_OWNERS, 256, "sum_w_out")
    p_inp32, p_inp16 = _grad_w_in(h, dpool, [(0, 0), (0, 1), (1, 0), (1, 1)], "grad_w_in_pool", 1)

    pool_owners, a_owners, b_owners = (_block_owners(b) for b in BLOCK_GROUPS)
    rec_args = (proj, o_raw, st_prev, dout_b, w_out_g, lb_logits, rec_norm_g, consts)
    (drec_a, part_a), ((rb_inp,),) = _rec_bwd(
        *rec_args, 0, "rec_bwd_a", [_ex_send([p_inp16], [pool_owners], units=[(0, 1)])], gate_of=HALF_HEADS)
    p_ina32, p_ina16 = _grad_w_in(h, drec_a, [(n, 0) for n in range(5)], "grad_w_in_a", 2)

    (drec_b, part_b), ((rb_inp, rb_ina, rb_pw),) = _rec_bwd(
        *rec_args, HALF_HEADS, "rec_bwd_b",
        [_ex_send([p_inp16, p_ina16, p_pw16], [pool_owners, a_owners, SHARD_OWNERS],
                  units=[(2, 3), tuple(range(len(a_owners))), SHARD_OWNERS], landed=[rb_inp, None, None])],
        own_gate=False)
    g_inp = _sum_units(p_inp32, rb_inp, place, pool_owners, 512, "sum_w_in_pool")
    g_ina = _sum_units(p_ina32, rb_ina, place, a_owners, 512, "sum_w_in_a")
    g_pw = _sum_units(p_pw32, rb_pw, place, SHARD_OWNERS, 128, "sum_pool_w")
    p_inb32, p_inb16 = _grad_w_in(h, drec_b, [(n, 0) for n in range(3)], "grad_w_in_b", 3)

    dproj = ([(dpool, 0, 0), (dpool, 0, 1), (dpool, 1, 0), (dpool, 1, 1)]
             + [(d, n, 0) for n in range(3) for d in (drec_a, drec_b)] + [(drec_a, 3, 0), (drec_a, 4, 0)])
    (dx, part_x), ((rb_inb,),) = _grad_x(dproj, w_in_g, x2, norm1_g, dout, [_ex_send([p_inb16], [b_owners])])
    g_inb = _sum_units(p_inb32, rb_inb, place, b_owners, 512, "sum_w_in_b")
    zero = jnp.zeros((1, D_MODEL), F32)
    part_rec = jnp.concatenate([part_a, part_b], axis=1)
    parts = jnp.concatenate([part_x[0:1], dscale, part_rec[1:2], zero, part_rec[0:1], part_out[0:1],
                             part_out[1:2], zero], axis=0)
    _, ((g_out, g_pw, g_inp, g_ina, g_inb), (gathered,)) = _call(
        None, name="join_halves",
        exchanges=[_ex_join([g_out, g_pw, g_inp, g_ina, g_inb],
                            [SHARD_OWNERS, SHARD_OWNERS, pool_owners, a_owners, b_owners]),
                   _ex_gather_small(parts)])

    group_of = np.zeros((D_PROJ // COL_BLK,), np.int32)
    index_of = np.zeros((D_PROJ // COL_BLK,), np.int32)
    for gi, blocks in enumerate(BLOCK_GROUPS):
        for i, j in enumerate(blocks):
            group_of[j], index_of[j] = gi, i
    per_shard = W_IN_SHARD // COL_BLK
    pick_in = jnp.stack([lax.dynamic_slice(jnp.asarray(group_of), (per_shard * chip,), (per_shard,)),
                         lax.dynamic_slice(jnp.asarray(index_of), (per_shard * chip,), (per_shard,))])
    pick_own = jnp.stack([jnp.zeros((1,), jnp.int32), chip.reshape(1).astype(jnp.int32)])
    big = [_adamw_units(w_in[0], m_w_in[0], v_w_in[0], [g_inp, g_ina, g_inb], pick_in, "adamw_w_in"),
           _adamw_units(w_out[0], m_w_out[0], v_w_out[0], [g_out], pick_own, "adamw_w_out"),
           _adamw_units(flat_pw(pool_w), flat_pw(m_pool_w), flat_pw(v_pool_w), [g_pw], pick_own, "adamw_pool_w")]

    row = lambda a: a.reshape(1, D_MODEL)
    loss, *small = _small_update(parts, gathered, [
        (norm1_g, m_norm1_g, v_norm1_g), (pool_scale, m_pool_scale, v_pool_scale),
        (lb_logits, m_lb_logits, v_lb_logits), (rec_norm_g, m_rec_norm_g, v_rec_norm_g),
        (row(final_norm_g), row(m_final_norm_g), row(v_final_norm_g))])

    def leaves(k):
        norm1, scale, lb, rec, final = (small[4 * i + k] for i in range(len(SMALL_ROWS)))
        return (norm1, big[0][k][None], big[2][k].reshape(pool_w.shape), scale, lb, rec,
                big[1][k][None], final.reshape(D_MODEL))

    return (loss.reshape(()), dx[None], *leaves(0), *leaves(1), *leaves(2), *leaves(3))
```

```python
import functools

import numpy as np
import jax
import jax.numpy as jnp
from jax import lax
from jax.experimental import pallas as pl
from jax.experimental.pallas import tpu as pltpu

F32 = jnp.float32
BF16 = jnp.bfloat16

SEQ = 2048
D_MODEL = 1024
D_PROJ = 6144
N_SEC = 6
N_GROUPS = 4
PG = 256
N_HEADS = 8
HEAD = 128
CHUNK = 64
N_LEVELS = 6
N_SHARDS = 4
W_IN_SHARD = D_PROJ // N_SHARDS
W_OUT_SHARD = 2048 // N_SHARDS
PW_SHARD = PG // N_SHARDS
COL_BLK = 512
EPS = 1e-6

ADAM_LR = 0.001
ADAM_B1 = 0.9
ADAM_B2 = 0.999
ADAM_EPS = 1e-08
ADAM_WD = 0.01
ADAM_STEP = 10

V7X_VMEM_LIMIT = 56 * 1024 * 1024
MESH = pl.DeviceIdType.MESH


def _params(**kw):
    return pltpu.CompilerParams(vmem_limit_bytes=V7X_VMEM_LIMIT, **kw)


def _sig(x):
    return 1.0 / (1.0 + jnp.exp(-x))


def _dot(a, b):
    return jnp.dot(a, b, preferred_element_type=F32)


def _dot_nt(a, b):
    return lax.dot_general(a, b, (((1,), (1,)), ((), ())), preferred_element_type=F32)


def _dot_tn(a, b):
    return lax.dot_general(a, b, (((0,), (0,)), ((), ())), preferred_element_type=F32)


def _split3(a):
    p1 = a.astype(BF16)
    r1 = a - p1.astype(F32)
    p2 = r1.astype(BF16)
    p3 = (r1 - p2.astype(F32)).astype(BF16)
    return jnp.concatenate([p1, p2, p3], axis=-1)


def _dot3(w01, a):
    n = a.shape[-1]
    r = _dot(w01, _split3(a))
    return r[:, :n] + r[:, n:2 * n] + r[:, 2 * n:]


def _chunk_constants():
    j = np.arange(CHUNK)
    tt, ss = np.meshgrid(j, j, indexing="ij")
    x = tt ^ ss
    hb = np.full((CHUNK, CHUNK), -1, np.int32)
    for l in range(N_LEVELS):
        hb[x >= (1 << l)] = l
    sym = np.stack([(hb == l) for l in range(N_LEVELS)]).astype(np.float32)
    low = sym * (tt > ss)
    sign = np.stack([np.where((j >> l) & 1, 1.0, -1.0) for l in range(N_LEVELS)]).astype(np.float32)
    sign = np.ascontiguousarray(np.broadcast_to(sign[:, :, None], (N_LEVELS, CHUNK, HEAD)))
    tri = (ss <= tt).astype(np.float32)
    return dict(tri=tri, tri_t=np.ascontiguousarray(tri.T), low=low,
                low_t=np.ascontiguousarray(low.transpose(0, 2, 1)), sym=sym, sign=sign)


def _in_proj(x, g1, w_slots, place, also_cast):
    n_col = D_PROJ // COL_BLK
    per_shard = W_IN_SHARD // COL_BLK
    rows = 1024
    half_rows = D_MODEL // 2
    quarter_rows = D_MODEL // 4
    FLIPS = (0, 2, 1, 3)
    ORDER = ([(0, p) for p in range(per_shard)] + [(m, p) for p in range(per_shard) for m in (1, 2)]
             + [(3, p) for p in range(per_shard)])

    def shard_at(m, chip):
        return chip ^ FLIPS[m]

    def pick(vals, t):
        return functools.reduce(lambda acc, iv: jnp.where(t == iv[0], iv[1], acc), list(enumerate(vals))[1:], vals[0])

    def body(place_ref, x_hbm, g_ref, w_in_ref, *rest):
        others, (proj_ref, h_ref, w_ref) = rest[:len(also_cast)], rest[len(also_cast):len(also_cast) + 3]
        slots = rest[len(also_cast) + 3:2 * len(also_cast) + 3]
        wbuf, load_sems, send_sems, recv_sems, x_ref, x_sem = rest[2 * len(also_cast) + 3:]
        t = pl.program_id(0)
        load_x = pltpu.make_async_copy(x_hbm, x_ref, x_sem)

        @pl.when(t == 1)
        def _():
            for src, dst in zip(others, slots):
                dst[...] = src[...].astype(BF16)
        x_, y_, c = _place()
        chip = 2 * x_ + y_
        me, other_core = (x_, y_, c), (x_, y_, 1 - c)
        x_nbr, y_nbr = (1 - x_, y_, c), (x_, 1 - y_, c)

        def rows_of(half, q=None):
            if q is None:
                return pl.ds(pl.multiple_of(half * half_rows, half_rows), half_rows)
            return pl.ds(pl.multiple_of(half * half_rows + q * quarter_rows, quarter_rows), quarter_rows)

        def block(m, p, r):
            return w_ref.at[shard_at(m, chip), p, r, :]

        def copy(k, ref, to):
            return _remote(ref, ref, send_sems, recv_sems, k, to)

        direct = lambda n, p, to: copy(3 * n + p, block(0, p, rows_of(c)), to)
        relay = lambda n, p, to: copy(6 + 3 * n + p, block(1 + n, p, rows_of(c, n)), to)
        arrived = lambda m, p: ([copy(3 * (m - 1) + p, block(m, p, rows_of(c)), me)] if m < 3 else
                                [copy(6 + 3 * n + p, block(3, p, rows_of(c, n)), me) for n in (0, 1)])
        passed_on = lambda m, p, half, to: copy(9 + 3 * m + p, block(m, p, rows_of(half)), to)

        def load(m, p, slot):
            return pltpu.make_async_copy(w_ref.at[shard_at(m, chip), p], wbuf.at[slot], load_sems.at[slot])

        def prepare(m, p):
            for cp in arrived(m, p):
                cp.wait_recv()
            passed_on(m, p, c, other_core).start()
            if m < 3:
                relay(m - 1, p, y_nbr if m == 1 else x_nbr).start()

        @pl.when(t == 0)
        def _():
            for p in range(per_shard):
                direct(0, p, x_nbr).start()
                direct(1, p, y_nbr).start()
            load_x.start()
            for p in range(per_shard):
                load(0, p, p).start()
            load_x.wait()

            def norm(i, _):
                r = pl.ds(pl.multiple_of(i * rows, rows), rows)
                xv = x_ref[r, :]
                inv = lax.rsqrt(jnp.mean(xv * xv, axis=-1, keepdims=True) + EPS)
                h_ref[r, :] = (xv * inv * g_ref[...]).astype(BF16)
                return 0
            lax.fori_loop(0, SEQ // rows, norm, 0)

        for step, (m, p) in enumerate(ORDER):
            @pl.when(t == step)
            def _(step=step, m=m, p=p):
                slot = step % per_shard
                if m > 0:
                    passed_on(m, p, 1 - c, me).wait_recv()
                    load(m, p, slot).start()
                if step + 1 < n_col and ORDER[step + 1][0] > 0:
                    prepare(*ORDER[step + 1])
                load(m, p, slot).wait()

                def mm(i, _):
                    r = pl.ds(pl.multiple_of(i * rows, rows), rows)
                    proj_ref[r, :] = _dot(h_ref[r, :], wbuf[slot])
                    return 0
                lax.fori_loop(0, SEQ // rows, mm, 0)

        @pl.when(t == n_col - 1)
        def _():
            for p in range(per_shard):
                sent = [direct(0, p, x_nbr), direct(1, p, y_nbr), relay(0, p, y_nbr), relay(1, p, x_nbr)]
                for cp in sent + [passed_on(m, p, c, other_core) for m in (1, 2, 3)]:
                    cp.wait_send()

    return pl.pallas_call(
        body, name="in_proj",
        grid_spec=pltpu.PrefetchScalarGridSpec(
            num_scalar_prefetch=1, grid=(n_col,),
            in_specs=[pl.BlockSpec(memory_space=pl.ANY),
                      pl.BlockSpec((1, D_MODEL), lambda t, p: (0, 0)),
                      pl.BlockSpec(memory_space=pl.ANY)]
                     + [pl.BlockSpec(a.shape, lambda t, p: (0, 0)) for a in also_cast],
            out_specs=[pl.BlockSpec((None, SEQ, COL_BLK),
                                    lambda t, p: (per_shard * (p[1] ^ pick([FLIPS[m] for m, _ in ORDER], t))
                                                  + pick([b for _, b in ORDER], t), 0, 0)),
                       pl.BlockSpec((SEQ, D_MODEL), lambda t, p: (0, 0)),
                       pl.BlockSpec(memory_space=pl.ANY)]
                      + [pl.BlockSpec((None,) + a.shape, lambda t, p: (p[1], 0, 0)) for a in also_cast],
            scratch_shapes=[pltpu.VMEM((per_shard, D_MODEL, COL_BLK), BF16),
                            pltpu.SemaphoreType.DMA((per_shard,)),
                            pltpu.SemaphoreType.DMA((21,)), pltpu.SemaphoreType.DMA((21,)),
                            pltpu.VMEM((SEQ, D_MODEL), F32), pltpu.SemaphoreType.DMA]),
        out_shape=[jax.ShapeDtypeStruct((n_col, SEQ, COL_BLK), F32),
                   jax.ShapeDtypeStruct((SEQ, D_MODEL), BF16),
                   jax.ShapeDtypeStruct(w_slots.shape, BF16)]
                  + [jax.ShapeDtypeStruct((N_SHARDS,) + a.shape, BF16) for a in also_cast],
        input_output_aliases={3: 2},
        compiler_params=_params(dimension_semantics=("arbitrary",)),
    )(place, x, g1, w_slots, *also_cast)


def _proj_cols(width, section, where, rows=SEQ):
    per_blk = COL_BLK // width

    def index(*grid):
        k, r = where(*grid)
        return section * (D_MODEL // COL_BLK) + k // per_blk, r, k % per_blk

    return pl.BlockSpec((None, rows, width), index)


POOL_ROWS = 256
POOL_HALO = 16
POOL_PAIR = 4


def _window_sums(ext, g, shift_of):
    s = ext
    for k in range(N_GROUPS):
        s = jnp.where(k <= g, s + pltpu.roll(s, shift_of(k), 0), s)
    return s


def _pool_diff(u_ref, i, g):
    n = POOL_ROWS + POOL_HALO
    r0 = i * POOL_ROWS
    cur = u_ref[pl.ds(pl.multiple_of(r0, POOL_ROWS), POOL_ROWS), :]
    before = u_ref[pl.ds(pl.multiple_of(jnp.maximum(r0 - POOL_HALO, 0), 8), POOL_HALO), :]
    before = jnp.where(i > 0, before, 0.0)
    ext = jnp.concatenate([before, cur], axis=0)
    s = _window_sums(ext, g, lambda k: 1 << k)[POOL_HALO:, :]
    t = r0 + lax.broadcasted_iota(jnp.int32, (POOL_ROWS, 1), 0)
    width = (2 << g).astype(F32)
    inv_count = 1.0 / jnp.minimum((t + 1).astype(F32), width)
    return s * inv_count - cur, inv_count


def _pool_fwd(proj, pw_g, pool_scale):
    def body(u_ref, gate_ref, pw_ref, sc_ref, y_ref):
        g = pl.program_id(0)

        def step(ii, _):
            chunks = [POOL_PAIR * ii + a for a in range(POOL_PAIR)]
            ds = [_pool_diff(u_ref, i, g)[0].astype(BF16) for i in chunks]
            mixed = [_dot(d, pw_ref[...]) for d in ds]
            for i, m in zip(chunks, mixed):
                r = pl.ds(pl.multiple_of(i * POOL_ROWS, POOL_ROWS), POOL_ROWS)
                gate = gate_ref[r, :]
                y_ref[r, :] = (m * sc_ref[...] * (gate * _sig(gate))).astype(BF16)
            return 0
        lax.fori_loop(0, SEQ // POOL_ROWS // POOL_PAIR, step, 0)

    return pl.pallas_call(
        body, name="pool_fwd", grid=(N_GROUPS,),
        in_specs=[_proj_cols(PG, 0, lambda g: (g, 0)), _proj_cols(PG, 1, lambda g: (g, 0)),
                  pl.BlockSpec((None, PG, PG), lambda g: (g, 0, 0)),
                  pl.BlockSpec((1, PG), lambda g: (0, g))],
        out_specs=pl.BlockSpec((SEQ, PG), lambda g: (0, g)),
        out_shape=jax.ShapeDtypeStruct((SEQ, D_MODEL), BF16),
        compiler_params=_params(dimension_semantics=("arbitrary",)),
    )(proj, proj, pw_g, pool_scale)


REC_ROWS = 1024
REC_CHUNKS = REC_ROWS // CHUNK
N_REC_BLK = SEQ // REC_ROWS
REC_GROUP = REC_CHUNKS
REC_GROUP_BWD = REC_CHUNKS
SEC_BLK = D_MODEL // HEAD


def _lower_bound(lb_ref):
    l0 = lb_ref[0:1, :]
    l1 = lb_ref[1:2, :]
    mx = jnp.maximum(l0, l1)
    e0 = jnp.exp(l0 - mx)
    e1 = jnp.exp(l1 - mx)
    return e0 / (e0 + e1)


def _gates(q, fl, lb):
    qs = q * _sig(q)
    sf = _sig(fl)
    f = lb + (1.0 - lb) * sf
    return qs, sf, f, 1.0 - f, jnp.log(f)


LOG2E = 1.4426950408889634


def _level_factors(g2, qs, k, sign_ref):
    t = lax.broadcasted_iota(jnp.int32, (CHUNK, HEAD), 0)
    row = lambda r, n: jnp.broadcast_to(g2[r:r + 1, :], (n, HEAD))
    out = []
    for l in range(N_LEVELS):
        m = 1 << l
        if l == 0:
            g_mid = jnp.where((t & 1) == 1, pltpu.roll(g2, 1, 0), g2)
        elif l == 1:
            low = (t & 7) < 4
            g_mid = jnp.concatenate([jnp.where(low[:8], row(8 * i + 1, 8), row(8 * i + 5, 8))
                                     for i in range(CHUNK // 8)], axis=0)
        else:
            g_mid = jnp.concatenate([row(b * 2 * m + m - 1, 2 * m) for b in range(CHUNK // (2 * m))], axis=0)
        sgn = sign_ref[l]
        up = sgn > 0.0
        e = jnp.exp2((g2 - g_mid) * sgn)
        x = jnp.where(up, qs, k) * e
        hi = x.astype(BF16)
        out.append((hi, (x - hi.astype(F32)).astype(BF16), e, up))
    return out


CHIP_FLIPS = ((1, 0), (0, 1), (1, 1))
HBM = pl.BlockSpec(memory_space=pl.ANY)


def _place():
    return lax.axis_index("x"), lax.axis_index("y"), lax.axis_index("c")


def _remote(src, dst, send_sems, recv_sems, k, to):
    return pltpu.make_async_remote_copy(src_ref=src, dst_ref=dst, send_sem=send_sems.at[k],
                                        recv_sem=recv_sems.at[k], device_id=to, device_id_type=MESH)


def _half_rows(ref, c):
    half = ref.shape[-2] // 2
    rows = pl.ds(pl.multiple_of(c * half, half), half)
    return ref.at[:, rows, :] if len(ref.shape) == 3 else ref.at[rows, :]


def _other_core_barrier():
    x, y, c = _place()
    sem = pltpu.get_barrier_semaphore()
    pl.semaphore_signal(sem, inc=1, device_id=(x, y, 1 - c), device_id_type=MESH)
    pl.semaphore_wait(sem, 1)


class _Exchange:
    def __init__(self, inputs, out_shapes, n_sems, start, finish, aliases=None):
        self.inputs, self.out_shapes, self.n_sems = list(inputs), list(out_shapes), n_sems
        self.start, self.finish, self.aliases = start, finish, dict(aliases or {})


def _ex_send(parts16, owners, units=None, landed=None):
    n_t = len(parts16)
    units = units or [tuple(range(len(o))) for o in owners]
    landed = landed or [None] * n_t
    given = [t for t in range(n_t) if landed[t] is not None]

    def each(ins, outs, send, recv, to_sender, to_owner):
        x, y, c = _place()
        k = 0
        for t, own in enumerate(owners):
            for j in units[t]:
                for r, (fx, fy) in enumerate(CHIP_FLIPS):
                    tx, ty = x ^ fx, y ^ fy
                    cp = _remote(ins[t].at[j], outs[t].at[j, r], send, recv, k, (tx, ty, c))
                    if to_sender is not None:
                        pl.when(2 * tx + ty == own[j])(functools.partial(to_sender, cp))
                    if to_owner is not None:
                        pl.when(2 * x + y == own[j])(functools.partial(to_owner, cp))
                    k += 1

    def start(*refs):
        each(*refs, lambda cp: cp.start(), None)

    def finish(*refs):
        each(*refs, None, lambda cp: cp.wait_recv())
        each(*refs, lambda cp: cp.wait_send(), None)

    shapes = [jax.ShapeDtypeStruct((a.shape[0], len(CHIP_FLIPS)) + a.shape[1:], BF16) for a in parts16]
    return _Exchange(list(parts16) + [landed[t] for t in given], shapes,
                     len(CHIP_FLIPS) * sum(len(u) for u in units), start, finish,
                     aliases={n_t + i: t for i, t in enumerate(given)})


def _ex_join(units, owners):
    def each(ins, outs, send, recv, fn):
        x, y, c = _place()
        k = 0
        for t, own in enumerate(owners):
            for j, o in enumerate(own):
                def half(cc, to, u=outs[t].at[j], k=k):
                    return _remote(_half_rows(u, cc), _half_rows(u, cc), send, recv, k, to)
                mine = functools.partial(half, c, (x, y, 1 - c))
                theirs = functools.partial(half, 1 - c, (x, y, c))
                pl.when(2 * x + y == o)(functools.partial(fn, mine, theirs))
                k += 1

    def start(*refs):
        each(*refs, lambda mine, theirs: mine().start())

    def finish(*refs):
        each(*refs, lambda mine, theirs: theirs().wait_recv())
        each(*refs, lambda mine, theirs: mine().wait_send())

    shapes = [jax.ShapeDtypeStruct(a.shape, F32) for a in units]
    return _Exchange(units, shapes, sum(len(o) for o in owners), start, finish,
                     aliases={t: t for t in range(len(units))})


def _ex_gather(slots):
    n_t = len(slots)
    n_fl = len(CHIP_FLIPS)

    def piece(ref, shard, half):
        return _half_rows(ref.at[shard], half)

    def first(outs, send, recv):
        x, y, c = _place()
        s = 2 * x + y
        return [_remote(piece(outs[t], s, c), piece(outs[t], s, c), send, recv, n_t * j + t, (x ^ fx, y ^ fy, c))
                for j, (fx, fy) in enumerate(CHIP_FLIPS) for t in range(n_t)]

    def start(ins, outs, send, recv):
        for cp in first(outs, send, recv):
            cp.start()

    def finish(ins, outs, send, recv):
        x, y, c = _place()
        passed = []
        for j, (fx, fy) in enumerate(CHIP_FLIPS):
            sj = 2 * (x ^ fx) + (y ^ fy)
            for t in range(n_t):
                k = n_t * j + t
                _remote(piece(outs[t], sj, c), piece(outs[t], sj, c), send, recv, k, (x, y, c)).wait_recv()
                cp = _remote(piece(outs[t], sj, c), piece(outs[t], sj, c), send, recv, n_t * n_fl + k, (x, y, 1 - c))
                cp.start()
                passed.append(cp)
        for j, (fx, fy) in enumerate(CHIP_FLIPS):
            sj = 2 * (x ^ fx) + (y ^ fy)
            for t in range(n_t):
                k = n_t * n_fl + n_t * j + t
                _remote(piece(outs[t], sj, 1 - c), piece(outs[t], sj, 1 - c), send, recv, k, (x, y, c)).wait_recv()
        for cp in first(outs, send, recv) + passed:
            cp.wait_send()

    shapes = [jax.ShapeDtypeStruct(a.shape, BF16) for a in slots]
    return _Exchange(slots, shapes, 2 * n_t * n_fl, start, finish, aliases={t: t for t in range(n_t)})


def _ex_gather_small(parts):
    def copies(ins, outs, send, recv):
        x, y, c = _place()
        me = 4 * x + 2 * y + c
        return [_remote(ins[0], outs[0].at[me], send, recv, mask - 1,
                        (x ^ (mask >> 2), y ^ ((mask >> 1) & 1), c ^ (mask & 1))) for mask in range(1, 8)]

    def start(*refs):
        for cp in copies(*refs):
            cp.start()

    def finish(ins, outs, send, recv):
        x, y, c = _place()
        me = 4 * x + 2 * y + c
        for mask in range(1, 8):
            _remote(ins[0], outs[0].at[me ^ mask], send, recv, mask - 1, (x, y, c)).wait_recv()
        for cp in copies(ins, outs, send, recv):
            cp.wait_send()

    return _Exchange([parts], [jax.ShapeDtypeStruct((8,) + parts.shape, F32)], 7, start, finish)


def _call(body, *, name, args=(), in_specs=(), out_specs=(), out_shape=(), grid=(), scratch_shapes=(),
          exchanges=()):
    n_in, n_out, n_scr = len(args), len(out_shape), len(scratch_shapes)
    ex_in, ex_out, ex_scr, spans, alias = [], [], [], [], {}
    for ex in exchanges:
        spans.append((len(ex_in), len(ex.inputs), len(ex_out), len(ex.out_shapes)))
        for i, o in ex.aliases.items():
            alias[n_in + len(ex_in) + i] = n_out + len(ex_out) + o
        ex_in += ex.inputs
        ex_out += ex.out_shapes
        ex_scr += [pltpu.SemaphoreType.DMA((ex.n_sems,)), pltpu.SemaphoreType.DMA((ex.n_sems,))]

    def full(*refs):
        ins, x_in = refs[:n_in], refs[n_in:n_in + len(ex_in)]
        outs = refs[n_in + len(ex_in):n_in + len(ex_in) + n_out]
        x_out = refs[n_in + len(ex_in) + n_out:n_in + len(ex_in) + n_out + len(ex_out)]
        scr = refs[len(refs) - n_scr - len(ex_scr):len(refs) - len(ex_scr)]
        sems = refs[len(refs) - len(ex_scr):]

        def run(which):
            for e, (ex, (i0, ni, o0, no)) in enumerate(zip(exchanges, spans)):
                getattr(ex, which)(x_in[i0:i0 + ni], x_out[o0:o0 + no], sems[2 * e], sems[2 * e + 1])

        if grid:
            ids = [pl.program_id(a) for a in range(len(grid))]
            is_first = functools.reduce(jnp.logical_and, [i == 0 for i in ids])
            is_last = functools.reduce(jnp.logical_and, [i == g - 1 for i, g in zip(ids, grid)])
            pl.when(is_first)(lambda: run("start"))
            body(*ins, *outs, *scr)
            pl.when(is_last)(lambda: run("finish"))
        else:
            run("start")
            if body is not None:
                body(*ins, *outs, *scr)
            run("finish")

    kw = dict(grid=grid) if grid else {}
    if grid:
        kw["compiler_params"] = _params(dimension_semantics=("arbitrary",) * len(grid))
    else:
        kw["compiler_params"] = _params()
    res = pl.pallas_call(
        full, name=name,
        in_specs=list(in_specs) + [HBM] * len(ex_in),
        out_specs=list(out_specs) + [HBM] * len(ex_out),
        out_shape=list(out_shape) + ex_out,
        scratch_shapes=list(scratch_shapes) + ex_scr,
        input_output_aliases=alias, **kw,
    )(*args, *ex_in)
    own = list(res[:n_out])
    per_ex = [list(res[n_out + o0:n_out + o0 + no]) for (_, _, o0, no) in spans]
    return own, per_ex


def _cast_w_in(w, place):
    rows, cols = w.shape
    tile = rows

    def body(place_ref, w_ref, o_ref):
        o_ref[...] = w_ref[...].astype(BF16)

    return pl.pallas_call(
        body, name="cast_w_in",
        grid_spec=pltpu.PrefetchScalarGridSpec(
            num_scalar_prefetch=1, grid=(cols // COL_BLK, rows // tile),
            in_specs=[pl.BlockSpec((tile, COL_BLK), lambda b, i, p: (i, b))],
            out_specs=pl.BlockSpec((None, None, tile, COL_BLK), lambda b, i, p: (p[1], b, i, 0))),
        out_shape=jax.ShapeDtypeStruct((N_SHARDS, cols // COL_BLK, rows, COL_BLK), BF16),
        compiler_params=_params(dimension_semantics=("arbitrary", "arbitrary")),
    )(place, w)


def _rec_fwd(proj, lb_logits, rec_g, consts, exchanges):
    tri, low, sign = consts["tri"], consts["low"], consts["sign"]

    def body(q_ref, f_ref, i_ref, rg_ref, lb_ref, g_ref, w_ref, low_ref, sign_ref, y_ref, o_ref, stp_ref, st_ref):
        @pl.when(pl.program_id(1) == 0)
        def _():
            st_ref[...] = jnp.zeros_like(st_ref)
        lb = _lower_bound(lb_ref)
        st = st_ref[...]
        rows = lambda c: pl.ds(c * CHUNK, CHUNK)
        for c0 in range(0, REC_CHUNKS, REC_GROUP):
            group = range(c0, c0 + REC_GROUP)
            gated = [_gates(q_ref[rows(c), :], f_ref[rows(c), :], lb) for c in group]
            g2s = [_dot3(w_ref[...], g) * LOG2E for (_, _, _, _, g) in gated]
            xs = [[xl for xl, _, _, _ in _level_factors(g2, qs, k, sign_ref)]
                  for g2, (qs, _, _, k, _) in zip(g2s, gated)]
            a_s = []
            for x in xs:
                a = jnp.zeros((CHUNK, CHUNK), F32)
                for l, xl in enumerate(x):
                    a = a + _dot_nt(xl, xl) * low_ref[l]
                a_s.append(a.astype(BF16))
            vbs = [i_ref[rows(c), :].astype(BF16) for c in group]
            intra = [_dot(a, vb) for a, vb in zip(a_s, vbs)]
            kvs = [_dot_tn(vb, (k * jnp.exp2(g2[CHUNK - 1:CHUNK, :] - g2)).astype(BF16))
                   for vb, g2, (_, _, _, k, _) in zip(vbs, g2s, gated)]
            for i, c in enumerate(group):
                qs, _, _, k, _ = gated[i]
                g2 = g2s[i]
                stp_ref[c] = st
                v = i_ref[rows(c), :]
                rg = rg_ref[rows(c), :]
                o = (intra[i] + jnp.sum(qs * k, axis=-1, keepdims=True) * v
                     + _dot_nt((qs * jnp.exp2(g2)).astype(BF16), st.astype(BF16)))
                st = st * jnp.exp2(g2[CHUNK - 1:CHUNK, :]) + kvs[i]
                o_ref[rows(c), :] = o
                inv = lax.rsqrt(jnp.mean(o * o, axis=-1, keepdims=True) + EPS)
                y_ref[rows(c), :] = (o * inv * g_ref[...] * (rg * _sig(rg))).astype(BF16)
        st_ref[...] = st

    sec = lambda n: _proj_cols(HEAD, n, lambda h, b: (h, b), REC_ROWS)
    vec = lambda rows: pl.BlockSpec((rows, HEAD), lambda h, b: (0, h))
    full = lambda a: pl.BlockSpec(a.shape, lambda h, b: (0,) * a.ndim)
    return _call(
        body, name="rec_fwd", grid=(N_HEADS, N_REC_BLK),
        args=(proj, proj, proj, proj, lb_logits, rec_g, tri, low, sign),
        in_specs=[sec(2), sec(3), sec(4), sec(5), vec(2), vec(1), full(tri), full(low), full(sign)],
        out_specs=[pl.BlockSpec((REC_ROWS, HEAD), lambda h, b: (b, h)),
                   pl.BlockSpec((REC_ROWS, HEAD), lambda h, b: (b, h)),
                   pl.BlockSpec((None, REC_CHUNKS, HEAD, HEAD), lambda h, b: (h, b, 0, 0))],
        out_shape=[jax.ShapeDtypeStruct((SEQ, D_MODEL), BF16),
                   jax.ShapeDtypeStruct((SEQ, D_MODEL), F32),
                   jax.ShapeDtypeStruct((N_HEADS, SEQ // CHUNK, HEAD, HEAD), F32)],
        scratch_shapes=[pltpu.VMEM((HEAD, HEAD), F32)],
        exchanges=exchanges)


OUT_ROWS = 512


def _out_proj_loss(y_pool, y_rec, w_out_g, x, target, gf):
    def body(yp_ref, yr_ref, w_ref, x_ref, t_ref, gf_ref, dout_ref, doutb_ref, part_ref):
        @pl.when(pl.program_id(0) == 0)
        def _():
            part_ref[...] = jnp.zeros_like(part_ref)
        halves = [pl.ds(a * (OUT_ROWS // 2), OUT_ROWS // 2) for a in range(2)]
        outs = [x_ref[r, :] + _dot(yp_ref[r, :], w_ref[0:D_MODEL, :])
                + _dot(yr_ref[r, :], w_ref[D_MODEL:2 * D_MODEL, :]) for r in halves]
        gf_v = gf_ref[...]
        for r, out in zip(halves, outs):
            inv = lax.rsqrt(jnp.mean(out * out, axis=-1, keepdims=True) + EPS)
            diff = out * inv * gf_v - t_ref[r, :]
            dyf = diff * (1.0 / D_MODEL)
            a = dyf * gf_v
            dout = inv * a - out * (inv * inv * inv) * jnp.mean(a * out, axis=-1, keepdims=True)
            dout_ref[r, :] = dout
            doutb_ref[r, :] = dout.astype(BF16)
            part_ref[0:1, :] += jnp.sum(dyf * out * inv, axis=0, keepdims=True)
            part_ref[1:2, :] += jnp.sum(diff * diff, axis=0, keepdims=True)

    row = lambda n: pl.BlockSpec((OUT_ROWS, n), lambda i: (i, 0))
    return pl.pallas_call(
        body, name="out_proj_loss", grid=(SEQ // OUT_ROWS,),
        in_specs=[row(D_MODEL), row(D_MODEL), pl.BlockSpec((2 * D_MODEL, D_MODEL), lambda i: (0, 0)),
                  row(D_MODEL), row(D_MODEL), pl.BlockSpec((1, D_MODEL), lambda i: (0, 0))],
        out_specs=[row(D_MODEL), row(D_MODEL), pl.BlockSpec((8, D_MODEL), lambda i: (0, 0))],
        out_shape=[jax.ShapeDtypeStruct((SEQ, D_MODEL), F32),
                   jax.ShapeDtypeStruct((SEQ, D_MODEL), BF16),
                   jax.ShapeDtypeStruct((8, D_MODEL), F32)],
        compiler_params=_params(dimension_semantics=("arbitrary",)),
    )(y_pool, y_rec, w_out_g, x, target, gf)


def _grad_w_out(y_pool, y_rec, dout_b):
    blk = W_OUT_SHARD // 2
    per = D_MODEL // blk
    n = 2 * per

    def body(yp_ref, yr_ref, d_ref, p32_ref, p16_ref, send_ref, recv_ref, send_sems, recv_sems):
        j = pl.program_id(0)
        x, y, c = _place()

        def copy(u):
            return _remote(send_ref.at[u], recv_ref.at[u], send_sems, recv_sems, u, (x, y, 1 - c))

        pl.when(j == 0)(_other_core_barrier)
        for i in range(n):
            @pl.when(j == i)
            def _(i=i):
                res = _dot_tn((yp_ref if i < per else yr_ref)[...], d_ref[...])

                @pl.when(i % 2 == c)
                def _():
                    p32_ref[i // 2] = res

                @pl.when(i % 2 != c)
                def _():
                    send_ref[i // 2] = res
                    copy(i // 2).start()

        @pl.when(j == n - 1)
        def _():
            for u in range(N_SHARDS):
                copy(u).wait_recv()
                tot = p32_ref[u] + recv_ref[u]
                p32_ref[u] = tot
                p16_ref[u] = tot.astype(BF16)
            for u in range(N_SHARDS):
                copy(u).wait_send()

    whole = pl.BlockSpec((N_SHARDS, blk, D_MODEL), lambda j: (0, 0, 0))
    return pl.pallas_call(
        body, name="grad_w_out", grid=(n,),
        in_specs=[pl.BlockSpec((SEQ, blk), lambda j: (0, jnp.minimum(j, per - 1))),
                  pl.BlockSpec((SEQ, blk), lambda j: (0, jnp.maximum(j - per, 0))),
                  pl.BlockSpec((SEQ, D_MODEL), lambda j: (0, 0))],
        out_specs=[whole, whole],
        out_shape=[jax.ShapeDtypeStruct((N_SHARDS, blk, D_MODEL), F32),
                   jax.ShapeDtypeStruct((N_SHARDS, blk, D_MODEL), BF16)],
        scratch_shapes=[pltpu.VMEM((N_SHARDS, blk, D_MODEL), F32), pltpu.VMEM((N_SHARDS, blk, D_MODEL), F32),
                        pltpu.SemaphoreType.DMA((N_SHARDS,)), pltpu.SemaphoreType.DMA((N_SHARDS,))],
        compiler_params=_params(dimension_semantics=("arbitrary",), collective_id=0),
    )(y_pool, y_rec, dout_b)


def _pool_bwd(proj, dout_b, w_out_g, pw_g, pool_scale, exchanges):
    n = POOL_ROWS + POOL_HALO
    per_half = N_GROUPS // 2

    def body(u_ref, gate_ref, d_ref, wo_ref, pw_ref, sc_ref,
             dp_ref, p32_ref, p16_ref, dsc_ref, dd_ref, ddw_ref, dpw_ref, keep_ref, send_ref, recv_ref,
             send_sems, recv_sems):
        g = pl.program_id(0)
        x_, y_, c = _place()

        def to_other_core(a):
            return _remote(send_ref.at[a], recv_ref.at[a], send_sems, recv_sems, a, (x_, y_, 1 - c))

        dpw_ref[...] = jnp.zeros_like(dpw_ref)
        dsc_ref[...] = jnp.zeros_like(dsc_ref)

        def first(ii, _):
            chunks = [POOL_PAIR * ii + a for a in range(POOL_PAIR)]
            rs = [pl.ds(pl.multiple_of(i * POOL_ROWS, POOL_ROWS), POOL_ROWS) for i in chunks]
            diffs = [_pool_diff(u_ref, i, g) for i in chunks]
            dbs = [d.astype(BF16) for d, _ in diffs]
            mixed = [_dot(db, pw_ref[...]) for db in dbs]
            dys = [_dot_nt(d_ref[r, :], wo_ref[...]) for r in rs]
            sc = sc_ref[...]
            dmbs = []
            for r, m, dy in zip(rs, mixed, dys):
                gate = gate_ref[r, :]
                sg = _sig(gate)
                silu = gate * sg
                dp_ref[1, r, :] = (dy * m * sc * (sg * (1.0 + gate * (1.0 - sg)))).astype(BF16)
                dsc_ref[...] += jnp.sum(dy * silu * m, axis=0, keepdims=True)
                dmbs.append((dy * silu * sc).astype(BF16))
            for db, dmb in zip(dbs, dmbs):
                dpw_ref[...] += _dot_tn(db, dmb)
            dds = [_dot_nt(dmb, pw_ref[...]) for dmb in dmbs]
            for r, dd, (_, inv_count) in zip(rs, dds, diffs):
                dd_ref[r, :] = dd
                ddw_ref[r, :] = dd * inv_count
            return 0
        lax.fori_loop(0, SEQ // POOL_ROWS // POOL_PAIR, first, 0)

        def second(i, _):
            r0 = i * POOL_ROWS
            r = pl.ds(pl.multiple_of(r0, POOL_ROWS), POOL_ROWS)
            last = i == SEQ // POOL_ROWS - 1
            after = ddw_ref[pl.ds(pl.multiple_of(jnp.minimum(r0 + POOL_ROWS, SEQ - POOL_HALO), 8), POOL_HALO), :]
            after = jnp.where(last, 0.0, after)
            ext = jnp.concatenate([ddw_ref[r, :], after], axis=0)
            s = _window_sums(ext, g, lambda k: n - (1 << k))[:POOL_ROWS, :]
            dp_ref[0, r, :] = (s - dd_ref[r, :]).astype(BF16)
            return 0
        lax.fori_loop(0, SEQ // POOL_ROWS, second, 0)

        for a in range(per_half):
            @pl.when((g % per_half == a) & (g // per_half == c))
            def _(a=a):
                keep_ref[a] = dpw_ref[...]

            @pl.when((g % per_half == a) & (g // per_half != c))
            def _(a=a):
                send_ref[a] = dpw_ref[...]
                to_other_core(a).start()

        @pl.when(g == N_GROUPS - 1)
        def _():
            for a in range(per_half):
                to_other_core(a).wait_recv()
                tot = keep_ref[a] + recv_ref[a]
                for s in range(N_SHARDS):
                    rows_s = tot[s * PW_SHARD:(s + 1) * PW_SHARD, :]
                    p32_ref[s, a * PW_SHARD:(a + 1) * PW_SHARD, :] = rows_s
                    p16_ref[s, a * PW_SHARD:(a + 1) * PW_SHARD, :] = rows_s.astype(BF16)
            for a in range(per_half):
                to_other_core(a).wait_send()

    pw_half = (N_SHARDS, per_half * PW_SHARD, PG)
    return _call(
        body, name="pool_bwd", grid=(N_GROUPS,),
        args=(proj, proj, dout_b, w_out_g, pw_g, pool_scale),
        in_specs=[_proj_cols(PG, 0, lambda g: (g, 0)), _proj_cols(PG, 1, lambda g: (g, 0)),
                  pl.BlockSpec((SEQ, D_MODEL), lambda g: (0, 0)),
                  pl.BlockSpec((PG, D_MODEL), lambda g: (g, 0)),
                  pl.BlockSpec((None, PG, PG), lambda g: (g, 0, 0)),
                  pl.BlockSpec((1, PG), lambda g: (0, g))],
        out_specs=[pl.BlockSpec((2, SEQ, PG), lambda g: (0, 0, g)),
                   pl.BlockSpec(pw_half, lambda g: (0, 0, 0)),
                   pl.BlockSpec(pw_half, lambda g: (0, 0, 0)),
                   pl.BlockSpec((1, PG), lambda g: (0, g))],
        out_shape=[jax.ShapeDtypeStruct((2, SEQ, D_MODEL), BF16),
                   jax.ShapeDtypeStruct(pw_half, F32),
                   jax.ShapeDtypeStruct(pw_half, BF16),
                   jax.ShapeDtypeStruct((1, D_MODEL), F32)],
        scratch_shapes=[pltpu.VMEM((SEQ, PG), F32), pltpu.VMEM((SEQ, PG), F32), pltpu.VMEM((PG, PG), F32)]
                       + [pltpu.VMEM((per_half, PG, PG), F32)] * 3
                       + [pltpu.SemaphoreType.DMA((per_half,)), pltpu.SemaphoreType.DMA((per_half,))],
        exchanges=exchanges)


HALF_HEADS = N_HEADS // 2
HALF_COLS = HALF_HEADS * HEAD


def _rec_bwd(proj, o_raw, st_prev, dout_b, w_out_g, lb_logits, rec_g, consts, h0, name, exchanges,
             gate_of=None, own_gate=True):
    n_sec = 4 if gate_of is None else 5

    def body(q_ref, f_ref, i_ref, rg_ref, o_ref, stp_ref, d_ref, wo_ref, lb_ref, g_ref,
             w_ref, lowt_ref, sym_ref, sign_ref, tri_ref, *rest):
        dr_ref, part_ref, dst_ref = rest[-3:]

        @pl.when(pl.program_id(1) == 0)
        def _():
            dst_ref[...] = jnp.zeros_like(dst_ref)
            part_ref[...] = jnp.zeros_like(part_ref)
        if gate_of is not None:
            rg2_ref, o2_ref, wo2_ref, g2_ref = rest[:4]
            rg, o = rg2_ref[...], o2_ref[...]
            sg = _sig(rg)
            inv = lax.rsqrt(jnp.mean(o * o, axis=-1, keepdims=True) + EPS)
            dy = _dot_nt(d_ref[...], wo2_ref[...])
            dr_ref[4] = (dy * (o * inv) * g2_ref[...] * (sg * (1.0 + rg * (1.0 - sg)))).astype(BF16)
        tril = (lax.broadcasted_iota(jnp.int32, (CHUNK, CHUNK), 0)
                > lax.broadcasted_iota(jnp.int32, (CHUNK, CHUNK), 1))
        lb = _lower_bound(lb_ref)
        grec = g_ref[...]
        dst = dst_ref[...]
        acc_grec = jnp.zeros((1, HEAD), F32)
        acc_lb = jnp.zeros((1, HEAD), F32)
        rows = lambda c: pl.ds(c * CHUNK, CHUNK)
        for c0 in reversed(range(0, REC_CHUNKS, REC_GROUP_BWD)):
            group = list(reversed(range(c0, c0 + REC_GROUP_BWD)))
            dys = [_dot_nt(d_ref[rows(c), :], wo_ref[...]) for c in group]
            dos = []
            for c, dy in zip(group, dys):
                rg = rg_ref[rows(c), :]
                o = o_ref[rows(c), :]
                sg = _sig(rg)
                silu = rg * sg
                inv = lax.rsqrt(jnp.mean(o * o, axis=-1, keepdims=True) + EPS)
                recn = o * inv
                if own_gate:
                    dr_ref[3, rows(c), :] = (dy * recn * grec * (sg * (1.0 + rg * (1.0 - sg)))).astype(BF16)
                acc_grec = acc_grec + jnp.sum(dy * silu * recn, axis=0, keepdims=True)
                drecn = dy * silu * grec
                dos.append(inv * drecn - o * (inv * inv * inv) * jnp.mean(drecn * o, axis=-1, keepdims=True))
            gated = [_gates(q_ref[rows(c), :], f_ref[rows(c), :], lb) for c in group]
            g2s = [_dot3(w_ref[...], g) * LOG2E for (_, _, _, _, g) in gated]
            levels = [_level_factors(g2, qs, k, sign_ref) for g2, (qs, _, _, k, _) in zip(g2s, gated)]
            a_ts = []
            for lev in levels:
                a_t = jnp.zeros((CHUNK, CHUNK), F32)
                for l, (xl, _, _, _) in enumerate(lev):
                    a_t = a_t + _dot_nt(xl, xl) * lowt_ref[l]
                a_ts.append(a_t.astype(BF16))
            dobs = [do.astype(BF16) for do in dos]
            vbs = [i_ref[rows(c), :].astype(BF16) for c in group]
            d_syms = [jnp.where(tril, _dot_nt(dob, vb), _dot_nt(vb, dob)) for dob, vb in zip(dobs, vbs)]
            dqs_is, dk_is = [], []
            for lev, d_sym in zip(levels, d_syms):
                dqs_i = jnp.zeros((CHUNK, HEAD), F32)
                both_i = jnp.zeros((CHUNK, HEAD), F32)
                for l, (xl, xlo, e, up) in enumerate(lev):
                    z = d_sym * sym_ref[l]
                    tmp = _dot(z.astype(BF16), jnp.concatenate([xl, xlo], axis=-1))
                    tmp = (tmp[:, :HEAD] + tmp[:, HEAD:]) * e
                    dqs_i = dqs_i + jnp.where(up, tmp, 0.0)
                    both_i = both_i + tmp
                dqs_is.append(dqs_i)
                dk_is.append(both_i - dqs_i)
            e_gs = [jnp.exp2(g2) for g2 in g2s]
            e_revs = [jnp.exp2(g2[CHUNK - 1:CHUNK, :] - g2) for g2 in g2s]
            e_lasts = [jnp.exp2(g2[CHUNK - 1:CHUNK, :]) for g2 in g2s]
            q_gs = [qs * e_g for (qs, _, _, _, _), e_g in zip(gated, e_gs)]
            kdecs = [k * e_rev for (_, _, _, k, _), e_rev in zip(gated, e_revs)]
            dv12 = [_dot(a_t, dob) + jnp.sum(qs * k, axis=-1, keepdims=True) * do
                    for a_t, dob, do, (qs, _, _, k, _) in zip(a_ts, dobs, dos, gated)]
            dq_gs = [_dot(dob, stp_ref[c].astype(BF16)) for c, dob in zip(group, dobs)]
            steps = [_dot_tn(dob, q_g.astype(BF16)) for dob, q_g in zip(dobs, q_gs)]
            dsts = []
            for e_last, step in zip(e_lasts, steps):
                dsts.append(dst)
                dst = dst * e_last + step
            dstbs = [d.astype(BF16) for d in dsts]
            dv3 = [_dot_nt(kdec.astype(BF16), dstb) for kdec, dstb in zip(kdecs, dstbs)]
            dkdecs = [_dot(vb, dstb) for vb, dstb in zip(vbs, dstbs)]
            dbig_gs, dg_lasts, dqss, dks = [], [], [], []
            for i, c in enumerate(group):
                qs, _, _, k, _ = gated[i]
                de_last = jnp.sum(stp_ref[c] * dsts[i], axis=0, keepdims=True)
                ddiag = jnp.sum(dos[i] * i_ref[rows(c), :], axis=-1, keepdims=True)
                dqss.append(dqs_is[i] + ddiag * k + dq_gs[i] * e_gs[i])
                dks.append(dk_is[i] + ddiag * qs + dkdecs[i] * e_revs[i])
                dg_rev = dkdecs[i] * kdecs[i]
                dg_lasts.append(jnp.sum(dg_rev, axis=0, keepdims=True) + de_last * e_lasts[i])
                dbig_gs.append(qs * dqs_is[i] - k * dk_is[i] + dq_gs[i] * q_gs[i] - dg_rev)
            dgs = [_dot3(tri_ref[...], dbig_g) + dg_last for dbig_g, dg_last in zip(dbig_gs, dg_lasts)]
            for i, c in enumerate(group):
                _, sf, f, _, _ = gated[i]
                q = q_ref[rows(c), :]
                df = dgs[i] / f - dks[i]
                dr_ref[1, rows(c), :] = (df * (1.0 - lb) * sf * (1.0 - sf)).astype(BF16)
                acc_lb = acc_lb + jnp.sum(df * (1.0 - sf), axis=0, keepdims=True)
                sq = _sig(q)
                dr_ref[0, rows(c), :] = (dqss[i] * (sq * (1.0 + q * (1.0 - sq)))).astype(BF16)
                dr_ref[2, rows(c), :] = (dv12[i] + dv3[i]).astype(BF16)
        dst_ref[...] = dst
        part_ref[0:1, :] += acc_grec
        part_ref[1:2, :] += acc_lb

    rev = lambda b: N_REC_BLK - 1 - b
    sec = lambda n: _proj_cols(HEAD, n, lambda h, b: (h0 + h, rev(b)), REC_ROWS)
    col_in = pl.BlockSpec((REC_ROWS, HEAD), lambda h, b: (rev(b), h0 + h))
    vec_in = lambda rows: pl.BlockSpec((rows, HEAD), lambda h, b: (0, h0 + h))
    full = lambda a: pl.BlockSpec(a.shape, lambda h, b: (0,) * a.ndim)
    extra_args, extra_specs = (), []
    if gate_of is not None:
        extra_args = (proj, o_raw, w_out_g, rec_g)
        extra_specs = [_proj_cols(HEAD, 5, lambda h, b: (gate_of + h, rev(b)), REC_ROWS),
                       pl.BlockSpec((REC_ROWS, HEAD), lambda h, b: (rev(b), gate_of + h)),
                       pl.BlockSpec((HEAD, D_MODEL), lambda h, b: (SEC_BLK + gate_of + h, 0)),
                       pl.BlockSpec((1, HEAD), lambda h, b: (0, gate_of + h))]
    return _call(
        body, name=name, grid=(HALF_HEADS, N_REC_BLK),
        args=(proj, proj, proj, proj, o_raw, st_prev, dout_b, w_out_g, lb_logits, rec_g,
              consts["tri"], consts["low_t"], consts["sym"], consts["sign"], consts["tri_t"]) + extra_args,
        in_specs=[sec(2), sec(3), sec(4), sec(5), col_in,
                  pl.BlockSpec((None, REC_CHUNKS, HEAD, HEAD), lambda h, b: (h0 + h, rev(b), 0, 0)),
                  pl.BlockSpec((REC_ROWS, D_MODEL), lambda h, b: (rev(b), 0)),
                  pl.BlockSpec((HEAD, D_MODEL), lambda h, b: (SEC_BLK + h0 + h, 0)),
                  vec_in(2), vec_in(1)] + [full(consts[n]) for n in ("tri", "low_t", "sym", "sign", "tri_t")]
                 + extra_specs,
        out_specs=[pl.BlockSpec((n_sec, REC_ROWS, HEAD), lambda h, b: (0, rev(b), h)),
                   pl.BlockSpec((8, HEAD), lambda h, b: (0, h))],
        out_shape=[jax.ShapeDtypeStruct((n_sec, SEQ, HALF_COLS), BF16),
                   jax.ShapeDtypeStruct((8, HALF_COLS), F32)],
        scratch_shapes=[pltpu.VMEM((HEAD, HEAD), F32)],
        exchanges=exchanges)


def _w_in_block(w_ref, j):
    per_shard = W_IN_SHARD // COL_BLK
    return w_ref[j // per_shard, j % per_shard]


def _grad_x(dproj, w_in_g, x, g1, dout, exchanges):
    rows = 512
    n_blk = len(dproj)

    def body(*refs):
        dp_refs = refs[:n_blk]
        w_ref, x_ref, g_ref, dout_ref, dx_ref, part_ref = refs[n_blk:]

        @pl.when(pl.program_id(0) == 0)
        def _():
            part_ref[...] = jnp.zeros_like(part_ref)
        dh = jnp.zeros((rows, D_MODEL), F32)
        for j in range(n_blk):
            dh = dh + _dot_nt(dp_refs[j][...], _w_in_block(w_ref, j))
        xv = x_ref[...]
        inv = lax.rsqrt(jnp.mean(xv * xv, axis=-1, keepdims=True) + EPS)
        a = dh * g_ref[...]
        dx_ref[...] = (dout_ref[...] + inv * a
                       - xv * (inv * inv * inv) * jnp.mean(a * xv, axis=-1, keepdims=True))
        part_ref[0:1, :] += jnp.sum(dh * xv * inv, axis=0, keepdims=True)

    row = lambda: pl.BlockSpec((rows, D_MODEL), lambda i: (i, 0))
    dp_spec = lambda sec, cb: pl.BlockSpec((None, rows, COL_BLK), lambda i: (sec, i, cb))
    return _call(
        body, name="grad_x", grid=(SEQ // rows,),
        args=tuple(a for a, _, _ in dproj) + (w_in_g, x, g1, dout),
        in_specs=[dp_spec(sec, cb) for _, sec, cb in dproj]
                 + [pl.BlockSpec(w_in_g.shape, lambda i: (0, 0, 0, 0)),
                    row(), pl.BlockSpec((1, D_MODEL), lambda i: (0, 0)), row()],
        out_specs=[row(), pl.BlockSpec((8, D_MODEL), lambda i: (0, 0))],
        out_shape=[jax.ShapeDtypeStruct((SEQ, D_MODEL), F32),
                   jax.ShapeDtypeStruct((8, D_MODEL), F32)],
        exchanges=exchanges)


def _grad_w_in(h, dp, blocks, name, collective_id):
    n_blk = len(blocks)
    half = D_MODEL // 2
    pick = lambda vals: (lambda j: functools.reduce(lambda acc, iv: jnp.where(j == iv[0], iv[1], acc),
                                                     list(enumerate(vals))[1:], vals[0]))
    sec_of = pick([sec for sec, _ in blocks])
    cb_of = pick([cb for _, cb in blocks])

    def body(h_ref, dp_ref, p32_ref, p16_ref, keep_ref, send_ref, recv_ref, send_sems, recv_sems):
        j = pl.program_id(0)
        x, y, c = _place()
        cols = lambda cc: pl.ds(pl.multiple_of(cc * half, half), half)

        def copy(i):
            return _remote(send_ref.at[i], recv_ref.at[i], send_sems, recv_sems, i, (x, y, 1 - c))

        pl.when(j == 0)(_other_core_barrier)
        for i in range(n_blk + 1):
            @pl.when(j == i)
            def _(i=i):
                if i < n_blk:
                    send_ref[i] = _dot_tn(h_ref[:, cols(1 - c)], dp_ref[...])
                    copy(i).start()
                    keep_ref[i] = _dot_tn(h_ref[:, cols(c)], dp_ref[...])
                if i > 0:
                    copy(i - 1).wait_recv()
                    tot = keep_ref[i - 1] + recv_ref[i - 1]
                    p32_ref[...] = tot
                    p16_ref[...] = tot.astype(BF16)

        @pl.when(j == n_blk)
        def _():
            for i in range(n_blk):
                copy(i).wait_send()

    lagged = pl.BlockSpec((None, half, COL_BLK), lambda j: (jnp.maximum(j - 1, 0), 0, 0))
    last = n_blk - 1
    return pl.pallas_call(
        body, name=name, grid=(n_blk + 1,),
        in_specs=[pl.BlockSpec((SEQ, D_MODEL), lambda j: (0, 0)),
                  pl.BlockSpec((None, SEQ, COL_BLK),
                               lambda j: (sec_of(jnp.minimum(j, last)), 0, cb_of(jnp.minimum(j, last))))],
        out_specs=[lagged, lagged],
        out_shape=[jax.ShapeDtypeStruct((n_blk, half, COL_BLK), F32),
                   jax.ShapeDtypeStruct((n_blk, half, COL_BLK), BF16)],
        scratch_shapes=[pltpu.VMEM((n_blk, half, COL_BLK), F32)] * 3
                       + [pltpu.SemaphoreType.DMA((n_blk,)), pltpu.SemaphoreType.DMA((n_blk,))],
        compiler_params=_params(dimension_semantics=("arbitrary",), collective_id=collective_id),
    )(h, dp)


def _sum_units(part32, recv16, place, owners, tile, name):
    n, half, cols = part32.shape
    per_half = half // tile
    table = np.array([[sum(o == chip for o in owners)] + sorted(range(n), key=lambda j: (owners[j] != chip, j))
                      for chip in range(N_SHARDS)], np.int32)
    sched = jnp.concatenate([place[:1], jnp.asarray(table)[place[1]]])

    def block(k, i, p):
        live = k < p[1]
        unit = p[2 + jnp.minimum(k, jnp.maximum(p[1] - 1, 0))]
        return unit, jnp.where(live, i, per_half - 1)

    def body(sched_ref, p_ref, r_ref, o_ref):
        @pl.when(pl.program_id(0) < sched_ref[1])
        def _():
            acc = p_ref[...]
            for j in range(len(CHIP_FLIPS)):
                acc = acc + r_ref[j].astype(F32)
            o_ref[...] = acc

    return pl.pallas_call(
        body, name=name,
        grid_spec=pltpu.PrefetchScalarGridSpec(
            num_scalar_prefetch=1, grid=(n, per_half),
            in_specs=[pl.BlockSpec((None, tile, cols), lambda k, i, p: (*block(k, i, p), 0)),
                      pl.BlockSpec((None, len(CHIP_FLIPS), tile, cols),
                                   lambda k, i, p: (block(k, i, p)[0], 0, block(k, i, p)[1], 0))],
            out_specs=pl.BlockSpec((None, tile, cols),
                                   lambda k, i, p: (block(k, i, p)[0], p[0] * per_half + block(k, i, p)[1], 0))),
        out_shape=jax.ShapeDtypeStruct((n, 2 * half, cols), F32),
        compiler_params=_params(dimension_semantics=("arbitrary", "arbitrary")),
    )(sched, part32, recv16)


def _adamw_math(w, g, m, v):
    m = ADAM_B1 * m + (1.0 - ADAM_B1) * g
    v = ADAM_B2 * v + (1.0 - ADAM_B2) * (g * g)
    m_hat = m / (1.0 - ADAM_B1 ** ADAM_STEP)
    v_hat = v / (1.0 - ADAM_B2 ** ADAM_STEP)
    delta = -ADAM_LR * (m_hat / (jnp.sqrt(v_hat) + ADAM_EPS) + ADAM_WD * w)
    return delta, m, v


def _adamw_units(w, m, v, grads, pick, name):
    rows, cols = w.shape
    bc = grads[0].shape[-1]
    tile = min(rows // 2, 512)
    n_g = len(grads)

    def body(pick_ref, w_ref, m_ref, v_ref, *refs):
        g_refs, (g_out, d_ref, nm_ref, nv_ref) = refs[:n_g], refs[n_g:]
        p = pl.program_id(0)
        for a in range(n_g):
            @pl.when(pick_ref[0, p] == a)
            def _(a=a):
                g = g_refs[a][...]
                g_out[...] = g
                d_ref[...], nm_ref[...], nv_ref[...] = _adamw_math(w_ref[...], g, m_ref[...], v_ref[...])

    blk = pl.BlockSpec((tile, bc), lambda p, i, pick: (i, p))

    def g_spec(a):
        return pl.BlockSpec((None, tile, bc),
                            lambda p, i, pick: (jnp.where(pick[0, p] == a, pick[1, p], 0),
                                                jnp.where(pick[0, p] == a, i, 0), 0))

    return pl.pallas_call(
        body, name=name,
        grid_spec=pltpu.PrefetchScalarGridSpec(
            num_scalar_prefetch=1, grid=(cols // bc, rows // tile),
            in_specs=[blk] * 3 + [g_spec(a) for a in range(n_g)],
            out_specs=[blk] * 4),
        out_shape=[jax.ShapeDtypeStruct(w.shape, F32)] * 4,
        compiler_params=_params(dimension_semantics=("arbitrary", "arbitrary")),
    )(pick, w, m, v, *grads)


ROW_NORM1, ROW_SCALE, ROW_LB, ROW_REC, ROW_FINAL, ROW_LOSS = 0, 1, 2, 4, 5, 6


SMALL_ROWS = (ROW_NORM1, ROW_SCALE, ROW_LB, ROW_REC, ROW_FINAL)


def _small_update(parts, gathered, params):
    n_p = len(params)

    def body(own_ref, p_ref, *refs):
        ins, loss_ref, outs = refs[:3 * n_p], refs[3 * n_p], refs[3 * n_p + 1:]
        x, y, c = _place()
        me = 4 * x + 2 * y + c
        slot = lambda d: jnp.where(me == d, own_ref[...], p_ref[d])
        tot = slot(0)
        for d in range(1, 8):
            tot = tot + slot(d)
        for i, r in enumerate(SMALL_ROWS):
            w = ins[3 * i][...]
            g = tot[r:r + 1, :]
            if r == ROW_LB:
                mx = jnp.maximum(w[0:1, :], w[1:2, :])
                e0 = jnp.exp(w[0:1, :] - mx)
                e1 = jnp.exp(w[1:2, :] - mx)
                lb = e0 / (e0 + e1)
                g = g * lb * (1.0 - lb)
                g = jnp.concatenate([g, -g], axis=0)
            outs[4 * i][...] = g
            outs[4 * i + 1][...], outs[4 * i + 2][...], outs[4 * i + 3][...] = _adamw_math(
                w, g, ins[3 * i + 1][...], ins[3 * i + 2][...])
        loss_ref[...] = (0.5 / D_MODEL) * jnp.sum(tot[ROW_LOSS:ROW_LOSS + 1, :], axis=-1, keepdims=True)

    flat = [a for wmv in params for a in wmv]
    return pl.pallas_call(
        body, name="small_update",
        out_shape=[jax.ShapeDtypeStruct((1, 1), F32)]
                  + [jax.ShapeDtypeStruct(w.shape, F32) for w, _, _ in params for _ in range(4)],
        compiler_params=_params(),
    )(parts, gathered, *flat)


SHARD_OWNERS = tuple(range(N_SHARDS))
BLOCKS_POOL = (0, 1, 2, 3)
BLOCKS_A = (4, 6, 8, 10, 11)
BLOCKS_B = (5, 7, 9)
BLOCK_GROUPS = (BLOCKS_POOL, BLOCKS_A, BLOCKS_B)


def _block_owners(blocks):
    return tuple(j // (W_IN_SHARD // COL_BLK) for j in blocks)


def kernel(x, norm1_g, w_in, pool_w, pool_scale, lb_logits, rec_norm_g, w_out, final_norm_g, loss_target, m_norm1_g, m_w_in, m_pool_w, m_pool_scale, m_lb_logits, m_rec_norm_g, m_w_out, m_final_norm_g, v_norm1_g, v_w_in, v_pool_w, v_pool_scale, v_lb_logits, v_rec_norm_g, v_w_out, v_final_norm_g):
    xi, yi, ci = _place()
    chip = 2 * xi + yi
    place = jnp.stack([ci, chip]).astype(jnp.int32)
    pw_rows = N_GROUPS * PW_SHARD
    flat_pw = lambda a: a.reshape(pw_rows, PG)
    x2, target, gf = x[0], loss_target[0], final_norm_g.reshape(1, D_MODEL)
    consts = {n: jnp.asarray(a, BF16 if n.startswith("tri") else F32) for n, a in _chunk_constants().items()}

    proj, h, w_in_g, w_out_slots, pw_slots = _in_proj(x2, norm1_g, _cast_w_in(w_in[0], place), place,
                                                      [w_out[0], flat_pw(pool_w)])
    (y_rec, o_raw, st_prev), ((w_out_g, pw_g),) = _rec_fwd(
        proj, lb_logits, rec_norm_g, consts, [_ex_gather([w_out_slots, pw_slots])])
    w_out_g = w_out_g.reshape(2 * D_MODEL, D_MODEL)
    pw_full = pw_g.reshape(N_SHARDS, N_GROUPS, PW_SHARD, PG).transpose(1, 0, 2, 3).reshape(N_GROUPS, PG, PG)
    y_pool = _pool_fwd(proj, pw_full, pool_scale)
    dout, dout_b, part_out = _out_proj_loss(y_pool, y_rec, w_out_g, x2, target, gf)

    p_out32, p_out16 = _grad_w_out(y_pool, y_rec, dout_b)
    (dpool, p_pw32, p_pw16, dscale), ((rb_out,),) = _pool_bwd(proj, dout_b, w_out_g, pw_full, pool_scale,
                                                              [_ex_send([p_out16], [SHARD_OWNERS])])
    g_out = _sum_units(p_out32, rb_out, place, SHARD_OWNERS, 256, "sum_w_out")
    p_inp32, p_inp16 = _grad_w_in(h, dpool, [(0, 0), (0, 1), (1, 0), (1, 1)], "grad_w_in_pool", 1)

    pool_owners, a_owners, b_owners = (_block_owners(b) for b in BLOCK_GROUPS)
    rec_args = (proj, o_raw, st_prev, dout_b, w_out_g, lb_logits, rec_norm_g, consts)
    (drec_a, part_a), ((rb_inp,),) = _rec_bwd(
        *rec_args, 0, "rec_bwd_a", [_ex_send([p_inp16], [pool_owners], units=[(0, 1)])], gate_of=HALF_HEADS)
    p_ina32, p_ina16 = _grad_w_in(h, drec_a, [(n, 0) for n in range(5)], "grad_w_in_a", 2)

    (drec_b, part_b), ((rb_inp, rb_ina, rb_pw),) = _rec_bwd(
        *rec_args, HALF_HEADS, "rec_bwd_b",
        [_ex_send([p_inp16, p_ina16, p_pw16], [pool_owners, a_owners, SHARD_OWNERS],
                  units=[(2, 3), tuple(range(len(a_owners))), SHARD_OWNERS], landed=[rb_inp, None, None])],
        own_gate=False)
    g_inp = _sum_units(p_inp32, rb_inp, place, pool_owners, 512, "sum_w_in_pool")
    g_ina = _sum_units(p_ina32, rb_ina, place, a_owners, 512, "sum_w_in_a")
    g_pw = _sum_units(p_pw32, rb_pw, place, SHARD_OWNERS, 128, "sum_pool_w")
    p_inb32, p_inb16 = _grad_w_in(h, drec_b, [(n, 0) for n in range(3)], "grad_w_in_b", 3)

    dproj = ([(dpool, 0, 0), (dpool, 0, 1), (dpool, 1, 0), (dpool, 1, 1)]
             + [(d, n, 0) for n in range(3) for d in (drec_a, drec_b)] + [(drec_a, 3, 0), (drec_a, 4, 0)])
    (dx, part_x), ((rb_inb,),) = _grad_x(dproj, w_in_g, x2, norm1_g, dout, [_ex_send([p_inb16], [b_owners])])
    g_inb = _sum_units(p_inb32, rb_inb, place, b_owners, 512, "sum_w_in_b")
    zero = jnp.zeros((1, D_MODEL), F32)
    part_rec = jnp.concatenate([part_a, part_b], axis=1)
    parts = jnp.concatenate([part_x[0:1], dscale, part_rec[1:2], zero, part_rec[0:1], part_out[0:1],
                             part_out[1:2], zero], axis=0)
    _, ((g_out, g_pw, g_inp, g_ina, g_inb), (gathered,)) = _call(
        None, name="join_halves",
        exchanges=[_ex_join([g_out, g_pw, g_inp, g_ina, g_inb],
                            [SHARD_OWNERS, SHARD_OWNERS, pool_owners, a_owners, b_owners]),
                   _ex_gather_small(parts)])

    group_of = np.zeros((D_PROJ // COL_BLK,), np.int32)
    index_of = np.zeros((D_PROJ // COL_BLK,), np.int32)
    for gi, blocks in enumerate(BLOCK_GROUPS):
        for i, j in enumerate(blocks):
            group_of[j], index_of[j] = gi, i
    per_shard = W_IN_SHARD // COL_BLK
    pick_in = jnp.stack([lax.dynamic_slice(jnp.asarray(group_of), (per_shard * chip,), (per_shard,)),
                         lax.dynamic_slice(jnp.asarray(index_of), (per_shard * chip,), (per_shard,))])
    pick_own = jnp.stack([jnp.zeros((1,), jnp.int32), chip.reshape(1).astype(jnp.int32)])
    big = [_adamw_units(w_in[0], m_w_in[0], v_w_in[0], [g_inp, g_ina, g_inb], pick_in, "adamw_w_in"),
           _adamw_units(w_out[0], m_w_out[0], v_w_out[0], [g_out], pick_own, "adamw_w_out"),
           _adamw_units(flat_pw(pool_w), flat_pw(m_pool_w), flat_pw(v_pool_w), [g_pw], pick_own, "adamw_pool_w")]

    row = lambda a: a.reshape(1, D_MODEL)
    loss, *small = _small_update(parts, gathered, [
        (norm1_g, m_norm1_g, v_norm1_g), (pool_scale, m_pool_scale, v_pool_scale),
        (lb_logits, m_lb_logits, v_lb_logits), (rec_norm_g, m_rec_norm_g, v_rec_norm_g),
        (row(final_norm_g), row(m_final_norm_g), row(v_final_norm_g))])

    def leaves(k):
        norm1, scale, lb, rec, final = (small[4 * i + k] for i in range(len(SMALL_ROWS)))
        return (norm1, big[0][k][None], big[2][k].reshape(pool_w.shape), scale, lb, rec,
                big[1][k][None], final.reshape(D_MODEL))

    return (loss.reshape(()), dx[None], *leaves(0), *leaves(1), *leaves(2), *leaves(3))
```
